```python
import math
import jax, jax.numpy as jnp
from jax import lax
import numpy as np

D_MODEL = 1024
BATCH = 8
SEQ = 2048
DEPTH = 1

RW_HEADS = 8
RW_HEAD_DIM = 64
RW_WIDTH = RW_HEADS * RW_HEAD_DIM
DECAY_LORA = 64
AAA_LORA = 64
GATE_LORA = 128
GN_EPS = 64e-5

NSA_HEADS = 8
NSA_KV_HEADS = 2
NSA_HEAD_DIM = 64
NSA_WIDTH = NSA_HEADS * NSA_HEAD_DIM
NSA_KV_WIDTH = NSA_KV_HEADS * NSA_HEAD_DIM
CMP_BLOCK = 32
CMP_STRIDE = 16
SEL_BLOCK = 64
N_SELECT = 8
WINDOW = 512
Q_BLOCK = 128
N_NSA_BRANCH = 3
FORCE_SCORE = 1e6
NEG_INF = -1e30

N_GROUPS = 4
EXPERTS_PER_GROUP = 8
N_EXPERTS = N_GROUPS * EXPERTS_PER_GROUP
TOP_K = 2
D_EXPERT = 512
ROW_BLOCK = 128

PLE_DIM = 256
NORM_EPS = 1e-6

RW_SIZES = (RW_WIDTH, RW_WIDTH, RW_WIDTH, DECAY_LORA, AAA_LORA, GATE_LORA)
REST_SIZES = (NSA_WIDTH, NSA_KV_WIDTH, NSA_KV_WIDTH, NSA_KV_WIDTH, NSA_KV_WIDTH,
              NSA_KV_WIDTH, NSA_KV_WIDTH, N_NSA_BRANCH * NSA_HEADS, D_MODEL, D_MODEL)
RW_IN = 3 * RW_WIDTH + DECAY_LORA + AAA_LORA + GATE_LORA
N_IN = RW_IN + NSA_WIDTH + 6 * NSA_KV_WIDTH + N_NSA_BRANCH * NSA_HEADS + 2 * D_MODEL

kernel_name = 'hybrid_rwkv7_nsa_hiermoe_block'


def _rmsnorm(x, g):
    xf = x.astype(jnp.float32)
    y = xf * lax.rsqrt(jnp.mean(xf * xf, axis=-1, keepdims=True) + NORM_EPS)
    return (y * g.astype(jnp.float32)).astype(x.dtype)


def _split(z, sizes):
    offs = np.cumsum(sizes)[:-1].tolist()
    return jnp.split(z, offs, axis=-1)


def _token_shift(z, mu):
    prev = jnp.pad(z, ((0, 0), (1, 0), (0, 0)))[:, :-1]
    return z + (prev - z) * mu


def _rwkv7_time_mix(zr, zk, zv, zw, za, zg, w0, w2, a0, a2, g2, k_k, k_a, r_k, ln_w, ln_b):
    B, S, _ = zr.shape
    f32 = jnp.float32
    heads = lambda t: t.reshape(B, S, RW_HEADS, RW_HEAD_DIM)
    tm = lambda t: jnp.moveaxis(t, 1, 0)
    w_raw = (w0 + jnp.tanh(zw) @ w2).astype(f32)
    decay = jnp.exp(-jnp.exp(-jax.nn.softplus(-w_raw) - 0.5))
    a = jax.nn.sigmoid((a0 + za @ a2).astype(f32))
    g = jax.nn.sigmoid(zg) @ g2
    kk = heads(zk * k_k).astype(f32)
    kk = kk / jnp.maximum(jnp.sqrt(jnp.sum(kk * kk, axis=-1, keepdims=True)), 1e-12)
    k = zk * (1 + (a - 1) * k_a)
    r_h, k_h, v_h = heads(zr).astype(f32), heads(k).astype(f32), heads(zv).astype(f32)
    a_h, w_h = heads(a), heads(decay)

    def step(state, inp):
        r_t, w_t, k_t, v_t, kk_t, a_t = inp
        sa = jnp.einsum('bhij,bhj->bhi', state, -kk_t)
        state = (state * w_t[:, :, None, :]
                 + sa[..., None] * (kk_t * a_t)[:, :, None, :]
                 + v_t[..., None] * k_t[:, :, None, :])
        return state, jnp.einsum('bhij,bhj->bhi', state, r_t)

    state0 = jnp.zeros((B, RW_HEADS, RW_HEAD_DIM, RW_HEAD_DIM), f32)
    _, o = lax.scan(step, state0, (tm(r_h), tm(w_h), tm(k_h), tm(v_h), tm(kk), tm(a_h)))
    o = jnp.moveaxis(o, 0, 1)
    mu = jnp.mean(o, axis=-1, keepdims=True)
    var = jnp.mean(jnp.square(o - mu), axis=-1, keepdims=True)
    o = ((o - mu) * lax.rsqrt(var + GN_EPS)).reshape(B, S, RW_WIDTH) * ln_w + ln_b
    bonus = jnp.sum(r_h * k_h * r_k, axis=-1, keepdims=True) * v_h
    o = (o + bonus.reshape(B, S, RW_WIDTH)) * g
    return o.astype(zr.dtype)


def _nsa(q, kc, vc, ks, vs, kw, vw, gate_logits, cmp_pos_k, cmp_pos_v,
         cmp_k_w1, cmp_k_w2, cmp_v_w1, cmp_v_w2, gate_b):
    B, S, _ = q.shape
    f32 = jnp.float32
    G, R, Dh = NSA_KV_HEADS, NSA_HEADS // NSA_KV_HEADS, NSA_HEAD_DIM
    out_dtype = q.dtype
    q = q.reshape(B, S, G, R, Dh).astype(f32) * (Dh ** -0.5)
    kv = lambda t: t.reshape(B, S, G, Dh).astype(f32)
    kc, vc, ks, vs, kw, vw = kv(kc), kv(vc), kv(ks), kv(vs), kv(kw), kv(vw)
    slopes = jnp.asarray((2.0 ** (-8.0 * np.arange(1, NSA_HEADS + 1) / NSA_HEADS))
                         .astype(np.float32).reshape(G, R))
    t_pos = jnp.arange(S)

    n_cmp = (S - CMP_BLOCK) // CMP_STRIDE + 1
    cmp_start = np.arange(n_cmp) * CMP_STRIDE
    win = cmp_start[:, None] + np.arange(CMP_BLOCK)[None, :]

    def compress(t, pos_emb, w1, w2):
        blocks = t[:, win] + pos_emb[None, None, :, None, :]
        blocks = jnp.moveaxis(blocks, 3, 2).reshape(B, n_cmp, G, CMP_BLOCK * Dh)
        return jax.nn.gelu(blocks @ w1.astype(f32)) @ w2.astype(f32)

    kcmp = compress(kc, cmp_pos_k.astype(f32), cmp_k_w1, cmp_k_w2)
    vcmp = compress(vc, cmp_pos_v.astype(f32), cmp_v_w1, cmp_v_w2)
    cmp_end = jnp.asarray(cmp_start + CMP_BLOCK - 1)
    cmp_ctr = jnp.asarray((cmp_start + 0.5 * (CMP_BLOCK - 1)).astype(np.float32))
    dist_c = t_pos.astype(f32)[:, None] - cmp_ctr[None, :]
    mask_c = cmp_end[None, :] <= t_pos[:, None]
    s_c = jnp.einsum('bqgrd,bjgd->bgrqj', q, kcmp) - slopes[..., None, None] * dist_c
    p_c = jax.nn.softmax(jnp.where(mask_c, s_c, NEG_INF), axis=-1) * mask_c
    o_cmp = jnp.einsum('bgrqj,bjgd->bqgrd', p_c, vcmp)

    n_sel = S // SEL_BLOCK
    n_pick = min(N_SELECT, n_sel)
    sel_start = np.arange(n_sel) * SEL_BLOCK
    overlap = np.clip(np.minimum(cmp_start[:, None] + CMP_BLOCK, sel_start[None, :] + SEL_BLOCK)
                      - np.maximum(cmp_start[:, None], sel_start[None, :]), 0, None) / CMP_BLOCK
    imp = jnp.einsum('bgrqj,jk->bgqk', p_c, jnp.asarray(overlap.astype(np.float32)))
    blk_t = (t_pos // SEL_BLOCK)[:, None]
    sel_blk = jnp.arange(n_sel)[None, :]
    forced = (sel_blk == 0) | (sel_blk == blk_t) | (sel_blk == blk_t - 1)
    imp = jnp.where(forced, FORCE_SCORE, jnp.where(sel_blk <= blk_t, imp, -FORCE_SCORE))
    _, sel_idx = lax.top_k(imp, n_pick)

    ks_blocks = jnp.moveaxis(ks.reshape(B, n_sel, SEL_BLOCK, G, Dh), 3, 1)
    vs_blocks = jnp.moveaxis(vs.reshape(B, n_sel, SEL_BLOCK, G, Dh), 3, 1)
    kw_pad = jnp.pad(kw, ((0, 0), (WINDOW, 0), (0, 0), (0, 0)))
    vw_pad = jnp.pad(vw, ((0, 0), (WINDOW, 0), (0, 0), (0, 0)))
    bidx = jnp.arange(B)[:, None, None, None]
    gidx = jnp.arange(G)[None, :, None, None]
    n_keys = n_pick * SEL_BLOCK

    def query_block(i):
        qs = i * Q_BLOCK
        qb = lax.dynamic_slice_in_dim(q, qs, Q_BLOCK, axis=1)
        tq = qs + jnp.arange(Q_BLOCK)
        idx = lax.dynamic_slice_in_dim(sel_idx, qs, Q_BLOCK, axis=2)
        kg = ks_blocks[bidx, gidx, idx].reshape(B, G, Q_BLOCK, n_keys, Dh)
        vg = vs_blocks[bidx, gidx, idx].reshape(B, G, Q_BLOCK, n_keys, Dh)
        kpos = (idx[..., None] * SEL_BLOCK + jnp.arange(SEL_BLOCK)).reshape(B, G, Q_BLOCK, n_keys)
        dist = (tq[None, None, :, None] - kpos)[:, :, None]
        s = (jnp.einsum('bqgrd,bgqkd->bgrqk', qb, kg)
             - slopes[None, :, :, None, None] * dist.astype(f32))
        p = jax.nn.softmax(jnp.where(dist >= 0, s, NEG_INF), axis=-1)
        o_sel = jnp.einsum('bgrqk,bgqkd->bqgrd', p, vg)
        kwb = lax.dynamic_slice_in_dim(kw_pad, qs, Q_BLOCK + WINDOW, axis=1)
        vwb = lax.dynamic_slice_in_dim(vw_pad, qs, Q_BLOCK + WINDOW, axis=1)
        kp = qs - WINDOW + jnp.arange(Q_BLOCK + WINDOW)
        dw = tq[:, None] - kp[None, :]
        mw = (dw >= 0) & (dw < WINDOW) & (kp[None, :] >= 0)
        sw = jnp.einsum('bqgrd,bkgd->bgrqk', qb, kwb) - slopes[..., None, None] * dw.astype(f32)
        pw = jax.nn.softmax(jnp.where(mw, sw, NEG_INF), axis=-1)
        o_win = jnp.einsum('bgrqk,bkgd->bqgrd', pw, vwb)
        return o_sel, o_win

    o_sel, o_win = lax.map(query_block, jnp.arange(S // Q_BLOCK))
    o_sel = jnp.moveaxis(o_sel, 0, 1).reshape(B, S, G, R, Dh)
    o_win = jnp.moveaxis(o_win, 0, 1).reshape(B, S, G, R, Dh)

    gate = jax.nn.sigmoid((gate_logits + gate_b).astype(f32)).reshape(B, S, G, R, N_NSA_BRANCH)
    o = gate[..., 0:1] * o_cmp + gate[..., 1:2] * o_sel + gate[..., 2:3] * o_win
    return o.reshape(B, S, NSA_WIDTH).astype(out_dtype)


def _hier_moe(h, w_group, b_group, w_router, b_router, w_gate_up, w_down):
    B, S, D = h.shape
    T = B * S
    A = T * TOP_K
    f32 = jnp.float32
    hf = h.reshape(T, D)
    g_logits = (hf @ w_group).astype(f32) + b_group
    g_sel = jnp.argmax(g_logits, axis=-1)
    p_group = jnp.take_along_axis(jax.nn.softmax(g_logits, axis=-1), g_sel[:, None], axis=1)
    e_logits = ((hf @ w_router).astype(f32) + b_router).reshape(T, N_GROUPS, EXPERTS_PER_GROUP)
    e_logits = jnp.take_along_axis(e_logits, g_sel[:, None, None], axis=1)[:, 0]
    top_p, top_i = lax.top_k(jax.nn.softmax(e_logits, axis=-1), TOP_K)
    gate = p_group * top_p / jnp.sum(top_p, axis=-1, keepdims=True)
    expert = g_sel[:, None] * EXPERTS_PER_GROUP + top_i

    e_flat = expert.reshape(A)
    order = jnp.argsort(e_flat)
    e_sorted = e_flat[order]
    tok_sorted = order // TOP_K
    counts = jnp.bincount(e_flat, length=N_EXPERTS)
    starts = jnp.cumsum(counts) - counts
    padded = (counts + ROW_BLOCK - 1) // ROW_BLOCK * ROW_BLOCK
    pends = jnp.cumsum(padded)
    dest = (pends - padded)[e_sorted] + jnp.arange(A) - starts[e_sorted]
    n_rows = A + N_EXPERTS * ROW_BLOCK
    n_blk = n_rows // ROW_BLOCK
    rows = jnp.zeros((n_rows, D), h.dtype).at[dest].set(hf[tok_sorted])
    blk_expert = jnp.minimum(jnp.searchsorted(pends, jnp.arange(n_blk) * ROW_BLOCK, side='right'),
                             N_EXPERTS - 1)

    def expert_block(args):
        xb, e = args
        gate_h, up_h = jnp.split(xb @ w_gate_up[e], 2, axis=-1)
        return (jax.nn.silu(gate_h) * up_h) @ w_down[e]

    out_rows = lax.map(expert_block, (rows.reshape(n_blk, ROW_BLOCK, D), blk_expert)).reshape(n_rows, D)
    contrib = out_rows[dest].astype(f32) * gate.reshape(A)[order][:, None]
    y = jax.ops.segment_sum(contrib, tok_sorted, num_segments=T)
    return y.astype(h.dtype).reshape(B, S, D)


def setup_inputs(seed: int = 0) -> dict:
    key = jax.random.key(seed)
    keys = iter(jax.random.split(key, 48))
    f32 = jnp.float32
    nrm = lambda shape, scale: jax.random.normal(next(keys), shape, f32) * scale
    gain = lambda shape: 1.0 + nrm(shape, 0.05)
    L, Dh = DEPTH, NSA_HEAD_DIM
    return {
        'x': nrm((BATCH, SEQ, D_MODEL), 1.0),
        'p': nrm((DEPTH, BATCH, SEQ, PLE_DIM), 1.0),
        'g_mix': gain((L, D_MODEL)),
        'w_in': nrm((L, D_MODEL, N_IN), D_MODEL ** -0.5),
        'mu_shift': jax.random.uniform(next(keys), (L, RW_IN), f32, 0.1, 0.9),
        'rw_w0': jax.random.uniform(next(keys), (L, RW_WIDTH), f32, -6.0, 1.0),
        'rw_w2': nrm((L, DECAY_LORA, RW_WIDTH), 0.1 * DECAY_LORA ** -0.5),
        'rw_a0': nrm((L, RW_WIDTH), 0.1),
        'rw_a2': nrm((L, AAA_LORA, RW_WIDTH), 0.3 * AAA_LORA ** -0.5),
        'rw_g2': nrm((L, GATE_LORA, RW_WIDTH), GATE_LORA ** -0.5),
        'rw_k_k': 0.85 + nrm((L, RW_WIDTH), 0.05),
        'rw_k_a': 1.0 + nrm((L, RW_WIDTH), 0.05),
        'rw_r_k': nrm((L, RW_HEADS, RW_HEAD_DIM), 0.1),
        'rw_ln_w': gain((L, RW_WIDTH)),
        'rw_ln_b': nrm((L, RW_WIDTH), 0.02),
        'cmp_pos_k': nrm((L, CMP_BLOCK, Dh), 0.1),
        'cmp_pos_v': nrm((L, CMP_BLOCK, Dh), 0.1),
        'cmp_k_w1': nrm((L, CMP_BLOCK * Dh, Dh), (CMP_BLOCK * Dh) ** -0.5),
        'cmp_k_w2': nrm((L, Dh, Dh), Dh ** -0.5),
        'cmp_v_w1': nrm((L, CMP_BLOCK * Dh, Dh), (CMP_BLOCK * Dh) ** -0.5),
        'cmp_v_w2': nrm((L, Dh, Dh), Dh ** -0.5),
        'nsa_gate_b': nrm((L, N_NSA_BRANCH * NSA_HEADS), 0.1),
        'w_up_rwkv': nrm((L, RW_WIDTH, D_MODEL), RW_WIDTH ** -0.5),
        'w_up_nsa': nrm((L, NSA_WIDTH, D_MODEL), NSA_WIDTH ** -0.5),
        'w_out': nrm((L, D_MODEL, D_MODEL), D_MODEL ** -0.5),
        'g_ffn': gain((L, D_MODEL)),
        'w_group': nrm((L, D_MODEL, N_GROUPS), D_MODEL ** -0.5),
        'b_group': nrm((L, N_GROUPS), 0.01),
        'w_router': nrm((L, D_MODEL, N_EXPERTS), D_MODEL ** -0.5),
        'b_router': nrm((L, N_EXPERTS), 0.01),
        'w_exp_gate_up': nrm((L, N_EXPERTS, D_MODEL, 2 * D_EXPERT), D_MODEL ** -0.5),
        'w_exp_down': nrm((L, N_EXPERTS, D_EXPERT, D_MODEL), D_EXPERT ** -0.5),
        'w_ple_proj': nrm((L, PLE_DIM, D_MODEL), PLE_DIM ** -0.5),
        'g_ple_norm': gain((L, D_MODEL)),
        'g_ple_gate_in': gain((L, D_MODEL)),
        'w_ple_gate': nrm((L, D_MODEL, D_MODEL), D_MODEL ** -0.5),
        'g_final': gain((D_MODEL,)),
    }


def reference(x, p, g_mix, w_in, mu_shift, rw_w0, rw_w2, rw_a0, rw_a2, rw_g2, rw_k_k, rw_k_a,
              rw_r_k, rw_ln_w, rw_ln_b, cmp_pos_k, cmp_pos_v, cmp_k_w1, cmp_k_w2, cmp_v_w1,
              cmp_v_w2, nsa_gate_b, w_up_rwkv, w_up_nsa, w_out, g_ffn, w_group, b_group,
              w_router, b_router, w_exp_gate_up, w_exp_down, w_ple_proj, g_ple_norm,
              g_ple_gate_in, w_ple_gate, g_final):
    for i in range(DEPTH):
        h = _rmsnorm(x, g_mix[i])
        z = h @ w_in[i]
        zr, zk, zv, zw, za, zg = _split(_token_shift(z[..., :RW_IN], mu_shift[i]), RW_SIZES)
        zq, zkc, zvc, zks, zvs, zkw, zvw, zgate, zmr, zmn = _split(z[..., RW_IN:], REST_SIZES)
        y_rw = _rwkv7_time_mix(zr, zk, zv, zw, za, zg, rw_w0[i], rw_w2[i], rw_a0[i], rw_a2[i],
                               rw_g2[i], rw_k_k[i], rw_k_a[i], rw_r_k[i], rw_ln_w[i], rw_ln_b[i])
        y_nsa = _nsa(zq, zkc, zvc, zks, zvs, zkw, zvw, zgate, cmp_pos_k[i], cmp_pos_v[i],
                     cmp_k_w1[i], cmp_k_w2[i], cmp_v_w1[i], cmp_v_w2[i], nsa_gate_b[i])
        mixed = (jax.nn.sigmoid(zmr) * (y_rw @ w_up_rwkv[i])
                 + jax.nn.sigmoid(zmn) * (y_nsa @ w_up_nsa[i]))
        x = x + mixed @ w_out[i]
        x = x + _hier_moe(_rmsnorm(x, g_ffn[i]), w_group[i], b_group[i], w_router[i],
                          b_router[i], w_exp_gate_up[i], w_exp_down[i])
        e = _rmsnorm(p[i] @ w_ple_proj[i], g_ple_norm[i])
        x = x + jax.nn.sigmoid(_rmsnorm(x, g_ple_gate_in[i]) @ w_ple_gate[i]) * e
    return _rmsnorm(x, g_final)
```

```python
import functools
import math

import jax
import jax.numpy as jnp
import numpy as np
from jax import lax
from jax.experimental import pallas as pl
from jax.experimental.pallas import tpu as pltpu

F32 = jnp.float32
BF16 = jnp.bfloat16
HI = lax.Precision.HIGHEST

D_MODEL = 1024
RW_HEADS = 8
RW_HEAD_DIM = 64
RW_WIDTH = 512
DECAY_LORA = 64
AAA_LORA = 64
GATE_LORA = 128
GN_EPS = 64e-5
RW_IN = 3 * RW_WIDTH + DECAY_LORA + AAA_LORA + GATE_LORA

NSA_HEADS = 8
NSA_KV_HEADS = 2
NSA_REP = NSA_HEADS // NSA_KV_HEADS
NSA_HEAD_DIM = 64
NSA_WIDTH = 512
NSA_KV_WIDTH = 128
CMP_BLOCK = 32
CMP_STRIDE = 16
SEL_BLOCK = 64
N_SELECT = 8
WINDOW = 512
N_NSA_BRANCH = 3
FORCE_SCORE = 1e6
NEG_INF = -1e30

N_GROUPS = 4
EXPERTS_PER_GROUP = 8
N_EXPERTS = 32
TOP_K = 2
D_EXPERT = 512
ROW_BLOCK = 128
PLE_DIM = 256
NORM_EPS = 1e-6

N_GATE = N_NSA_BRANCH * NSA_HEADS
ATT_IN = NSA_WIDTH + 6 * NSA_KV_WIDTH
KV_OFF = RW_IN + NSA_WIDTH
KV_IN = 6 * NSA_KV_WIDTH
GATE_OFF = RW_IN + ATT_IN
MERGE_OFF = GATE_OFF + N_GATE
LANE = 128

RW_CHUNK = 64
RW_SUB = 16
TQ = 128
TK = 128
TM_PROJ = 256
VMEM_LIMIT = 56 * 1024 * 1024


def _bdot(a, b):
    return jnp.dot(a.astype(BF16), b.astype(BF16), preferred_element_type=F32)


def _bdot_nt(a, b):
    return lax.dot_general(a.astype(BF16), b.astype(BF16), (((1,), (1,)), ((), ())),
                           preferred_element_type=F32)


def _rms(x, g):
    return x * lax.rsqrt(jnp.mean(x * x, axis=-1, keepdims=True) + NORM_EPS) * g


def _proj_kernel(x_ref, g_ref, w_ref, mu_ref, zrw_ref, zq_ref, zkv_ref, zmg_ref, zgate_ref,
                 carry_ref, *, tiles_per_seq):
    i = pl.program_id(0)
    tm = x_ref.shape[0]

    @pl.when(i % tiles_per_seq == 0)
    def _():
        carry_ref[...] = jnp.zeros_like(carry_ref)

    h = _rms(x_ref[...], g_ref[...]).astype(BF16)
    z = jnp.dot(h, w_ref[:, 0:RW_IN], preferred_element_type=F32)
    row = lax.broadcasted_iota(jnp.int32, (tm, 1), 0)
    prev = jnp.where(row == 0, carry_ref[7:8, :], pltpu.roll(z, 1, 0))
    carry_ref[...] = z[tm - 8:tm, :]
    zrw_ref[...] = z + (prev - z) * mu_ref[...]
    zq_ref[...] = jnp.dot(h, w_ref[:, RW_IN:KV_OFF], preferred_element_type=F32)
    zkv_ref[...] = jnp.dot(h, w_ref[:, KV_OFF:GATE_OFF], preferred_element_type=F32)
    zmg_ref[...] = jnp.dot(h, w_ref[:, GATE_OFF:GATE_OFF + 2 * D_MODEL], preferred_element_type=F32)
    zgate_ref[...] = jnp.dot(h, w_ref[:, GATE_OFF + 2 * D_MODEL:], preferred_element_type=F32)


def _proj(x2d, g_mix, w_in, mu, seq):
    T = x2d.shape[0]
    tm = TM_PROJ
    wp = jnp.concatenate(
        [w_in[:, :GATE_OFF], w_in[:, MERGE_OFF:],
         jnp.pad(w_in[:, GATE_OFF:MERGE_OFF], ((0, 0), (0, LANE - N_GATE)))], axis=1).astype(BF16)
    npad = wp.shape[1]
    row = lambda i: (i, 0)
    fixed = lambda i: (0, 0)
    return pl.pallas_call(
        functools.partial(_proj_kernel, tiles_per_seq=seq // tm),
        grid=(T // tm,),
        in_specs=[pl.BlockSpec((tm, D_MODEL), row), pl.BlockSpec((1, D_MODEL), fixed),
                  pl.BlockSpec((D_MODEL, npad), fixed), pl.BlockSpec((1, RW_IN), fixed)],
        out_specs=[pl.BlockSpec((tm, RW_IN), row), pl.BlockSpec((tm, NSA_WIDTH), row),
                   pl.BlockSpec((tm, KV_IN), row), pl.BlockSpec((tm, 2 * D_MODEL), row),
                   pl.BlockSpec((tm, LANE), row)],
        out_shape=[jax.ShapeDtypeStruct((T, RW_IN), F32), jax.ShapeDtypeStruct((T, NSA_WIDTH), F32),
                   jax.ShapeDtypeStruct((T, KV_IN), F32), jax.ShapeDtypeStruct((T, 2 * D_MODEL), F32),
                   jax.ShapeDtypeStruct((T, LANE), F32)],
        scratch_shapes=[pltpu.VMEM((8, RW_IN), F32)],
        compiler_params=pltpu.CompilerParams(dimension_semantics=("arbitrary",),
                                             vmem_limit_bytes=VMEM_LIMIT),
        name="proj",
    )(x2d, g_mix.reshape(1, D_MODEL), wp, mu.reshape(1, RW_IN))


def _heads(x):
    return jnp.stack([x[:, h * RW_HEAD_DIM:(h + 1) * RW_HEAD_DIM] for h in range(RW_HEADS)], axis=0)


def _hmm(a, b):
    return jnp.einsum("hij,hjk->hik", a, b, precision=HI, preferred_element_type=F32)


def _hmm_nt(a, b):
    return jnp.einsum("hid,hjd->hij", a, b, precision=HI, preferred_element_type=F32)


def _unit_lower_inverse(a_strict, sub_mask, eye):
    ad = jnp.where(sub_mask, a_strict, 0.0)
    ao = a_strict - ad
    td = eye - ad
    pw = ad
    for _ in range(int(math.log2(RW_SUB)) - 1):
        pw = _hmm(pw, pw)
        td = _hmm(td, eye + pw)
    n = _hmm(td, ao)
    t = eye - n
    pw = n
    for _ in range(int(math.log2(RW_CHUNK // RW_SUB)) - 1):
        pw = _hmm(pw, pw)
        t = _hmm(t, eye + pw)
    return _hmm(t, td)


def _rwkv_kernel(z_ref, w0_ref, w2_ref, a0_ref, a2_ref, g2_ref, kk_ref, ka_ref, rk_ref, lnw_ref,
                 lnb_ref, hsum_ref, o_ref, s_ref):
    c = pl.program_id(1)
    C = RW_CHUNK

    @pl.when(c == 0)
    def _():
        s_ref[...] = jnp.zeros_like(s_ref)

    z = z_ref[0]
    zr, zk, zv = z[:, 0:512], z[:, 512:1024], z[:, 1024:1536]
    zw, za, zg = z[:, 1536:1600], z[:, 1600:1664], z[:, 1664:1792]
    w_raw = w0_ref[...] + _bdot(jnp.tanh(zw), w2_ref[...])
    logw = -jax.nn.sigmoid(w_raw) * math.exp(-0.5)
    a = jax.nn.sigmoid(a0_ref[...] + _bdot(za, a2_ref[...]))
    gate = _bdot(jax.nn.sigmoid(zg), g2_ref[...])
    kk = zk * kk_ref[...]
    ss = jnp.dot(kk * kk, hsum_ref[...], precision=HI, preferred_element_type=F32)
    kk = kk / jnp.maximum(jnp.sqrt(ss), 1e-12)
    k = zk * (1.0 + (a - 1.0) * ka_ref[...])
    b = kk * a

    ti = lax.broadcasted_iota(jnp.int32, (C, C), 0)
    si = lax.broadcasted_iota(jnp.int32, (C, C), 1)
    incl = ti >= si
    strict = ti > si
    cum = jnp.dot(incl.astype(F32), logw, precision=HI, preferred_element_type=F32)
    g_t = jnp.exp(cum)
    g_inv = jnp.exp(-cum)
    g_prev = jnp.exp(cum - logw)
    cum_last = cum[C - 1:C, :]
    g_end = jnp.exp(cum_last - cum)

    rt, kt, bt = _heads(zr * g_t), _heads(k * g_inv), _heads(b * g_inv)
    qt = _heads(kk * g_prev)
    kh, bh = _heads(k * g_end), _heads(b * g_end)
    v = _heads(zv)

    a_kb = jnp.where(strict[None], _hmm_nt(qt, bt), 0.0)
    a_kk = jnp.where(strict[None], _hmm_nt(qt, kt), 0.0)
    a_rk = jnp.where(incl[None], _hmm_nt(rt, kt), 0.0)
    a_rb = jnp.where(incl[None], _hmm_nt(rt, bt), 0.0)
    eye = (ti == si).astype(F32)[None]
    sub_shift = int(math.log2(RW_SUB))
    sub_mask = ((ti >> sub_shift) == (si >> sub_shift))[None]
    t_inv = _unit_lower_inverse(a_kb, sub_mask, eye)

    s = s_ref[...]
    u = _hmm(t_inv, _hmm_nt(qt, s) + _hmm(a_kk, v))
    o = _hmm_nt(rt, s) + _hmm(a_rk, v) - _hmm(a_rb, u)
    s_new = (s * jnp.exp(_heads(cum_last))
             + _hmm(jnp.swapaxes(v, 1, 2), kh) - _hmm(jnp.swapaxes(u, 1, 2), bh))
    s_ref[...] = s_new

    mu = jnp.mean(o, axis=-1, keepdims=True)
    var = jnp.mean(jnp.square(o - mu), axis=-1, keepdims=True)
    on = (o - mu) * lax.rsqrt(var + GN_EPS)
    rkk = zr * k * rk_ref[...]
    for h in range(RW_HEADS):
        sl = slice(h * RW_HEAD_DIM, (h + 1) * RW_HEAD_DIM)
        bonus = jnp.sum(rkk[:, sl], axis=-1, keepdims=True) * zv[:, sl]
        o_ref[0, :, sl] = (on[h] * lnw_ref[:, sl] + lnb_ref[:, sl] + bonus) * gate[:, sl]


def _rwkv(zrw, w0, w2, a0, a2, g2, k_k, k_a, r_k, ln_w, ln_b):
    B, S, _ = zrw.shape
    C = RW_CHUNK
    hid = np.arange(RW_WIDTH) // RW_HEAD_DIM
    hsum = jnp.asarray((hid[:, None] == hid[None, :]).astype(np.float32))
    vec = lambda a: a.reshape(1, RW_WIDTH)
    fixed = lambda shape: pl.BlockSpec(shape, lambda b, c: (0,) * len(shape))
    return pl.pallas_call(
        _rwkv_kernel,
        grid=(B, S // C),
        in_specs=[pl.BlockSpec((1, C, RW_IN), lambda b, c: (b, c, 0)),
                  fixed((1, RW_WIDTH)), fixed((DECAY_LORA, RW_WIDTH)),
                  fixed((1, RW_WIDTH)), fixed((AAA_LORA, RW_WIDTH)),
                  fixed((GATE_LORA, RW_WIDTH)), fixed((1, RW_WIDTH)), fixed((1, RW_WIDTH)),
                  fixed((1, RW_WIDTH)), fixed((1, RW_WIDTH)), fixed((1, RW_WIDTH)),
                  fixed((RW_WIDTH, RW_WIDTH))],
        out_specs=pl.BlockSpec((1, C, RW_WIDTH), lambda b, c: (b, c, 0)),
        out_shape=jax.ShapeDtypeStruct((B, S, RW_WIDTH), F32),
        scratch_shapes=[pltpu.VMEM((RW_HEADS, RW_HEAD_DIM, RW_HEAD_DIM), F32)],
        compiler_params=pltpu.CompilerParams(dimension_semantics=("parallel", "arbitrary"),
                                             vmem_limit_bytes=VMEM_LIMIT),
        name="rwkv",
    )(zrw, vec(w0), w2, vec(a0), a2, g2, vec(k_k), vec(k_a), vec(r_k), vec(ln_w), vec(ln_b), hsum)


def _gelu_tanh(x):
    return 0.5 * x * (1.0 + jnp.tanh(math.sqrt(2.0 / math.pi) * (x + 0.044715 * (x * x * x))))


def _nsa_cmp_kernel(zk_ref, zv_ref, pk_ref, pv_ref, kw1_ref, kw2_ref, vw1_ref, vw2_ref, kc_ref, vc_ref):
    n_grp = zk_ref.shape[1] // CMP_STRIDE
    Dh = NSA_HEAD_DIM
    half = CMP_BLOCK // 2
    jrow = lax.broadcasted_iota(jnp.int32, (n_grp, 1), 0)
    for z_ref, pos_ref, w1_ref, w2_ref, out_ref in ((zk_ref, pk_ref, kw1_ref, kw2_ref, kc_ref),
                                                     (zv_ref, pv_ref, vw1_ref, vw2_ref, vc_ref)):
        for g in range(NSA_KV_HEADS):
            lo = jnp.zeros((n_grp, Dh), F32)
            hi = jnp.zeros((n_grp, Dh), F32)
            for l in range(half):
                xs = z_ref[0, pl.ds(l, n_grp, stride=CMP_STRIDE), :]
                xg = xs[:, g * Dh:(g + 1) * Dh]
                lo = lo + _bdot(xg + pos_ref[l:l + 1, :], w1_ref[l * Dh:(l + 1) * Dh, :])
                hi = hi + _bdot(xg + pos_ref[half + l:half + l + 1, :],
                                w1_ref[(half + l) * Dh:(half + l + 1) * Dh, :])
            pre = lo + pltpu.roll(hi, n_grp - 1, 0)
            out = _bdot(_gelu_tanh(pre), w2_ref[...])
            out_ref[0, g] = jnp.where(jrow < n_grp - 1, out, 0.0)


def _nsa_cmp(zkv, pos_k, pos_v, kw1, kw2, vw1, vw2):
    B, S, _ = zkv.shape
    n_grp = S // CMP_STRIDE
    Dh = NSA_HEAD_DIM
    fixed = lambda shape: pl.BlockSpec(shape, lambda b: (0,) * len(shape))
    out = jax.ShapeDtypeStruct((B, NSA_KV_HEADS, n_grp, Dh), F32)
    ospec = pl.BlockSpec((1, NSA_KV_HEADS, n_grp, Dh), lambda b: (b, 0, 0, 0))
    return pl.pallas_call(
        _nsa_cmp_kernel,
        grid=(B,),
        in_specs=[pl.BlockSpec((1, S, NSA_KV_WIDTH), lambda b: (b, 0, 0)),
                  pl.BlockSpec((1, S, NSA_KV_WIDTH), lambda b: (b, 0, 1)),
                  fixed((CMP_BLOCK, Dh)), fixed((CMP_BLOCK, Dh)),
                  fixed((CMP_BLOCK * Dh, Dh)), fixed((Dh, Dh)),
                  fixed((CMP_BLOCK * Dh, Dh)), fixed((Dh, Dh))],
        out_specs=[ospec, ospec],
        out_shape=[out, out],
        compiler_params=pltpu.CompilerParams(dimension_semantics=("parallel",),
                                             vmem_limit_bytes=VMEM_LIMIT),
        name="nsa_cmp",
    )(zkv, zkv, pos_k, pos_v, kw1, kw2, vw1, vw2)


def _nsa_attn_kernel(q_ref, kv_ref, kc_ref, vc_ref, gl_ref, gb_ref, ovl_ref, exp_ref, slope_ref,
                     o_ref):
    i = pl.program_id(1)
    Dh = NSA_HEAD_DIM
    R = NSA_REP
    n_cmp_pad = kc_ref.shape[2]
    n_sel = ovl_ref.shape[1]
    t0 = i * TQ
    t_q = t0 + lax.broadcasted_iota(jnp.int32, (TQ, 1), 0)
    t_r = jnp.concatenate([t_q] * R, axis=0)
    sgate = jax.nn.sigmoid(gl_ref[0] + gb_ref[...])
    lane_k = lax.broadcasted_iota(jnp.int32, (1, TK), 1)

    for g in range(NSA_KV_HEADS):
        q4 = jnp.concatenate(
            [q_ref[0, :, (g * R + r) * Dh:(g * R + r + 1) * Dh] for r in range(R)], axis=0) * (Dh ** -0.5)
        q4b = q4.astype(BF16)
        slope = jnp.concatenate(
            [jnp.broadcast_to(slope_ref[:, g * R + r:g * R + r + 1], (TQ, 1)) for r in range(R)], axis=0)

        jc = lax.broadcasted_iota(jnp.int32, (1, n_cmp_pad), 1)
        s_c = lax.dot_general(q4, kc_ref[0, g], (((1,), (1,)), ((), ())), precision=HI,
                              preferred_element_type=F32)
        dist_c = t_r.astype(F32) - (jc * CMP_STRIDE).astype(F32) - 0.5 * (CMP_BLOCK - 1)
        mask_c = (jc * CMP_STRIDE + (CMP_BLOCK - 1) <= t_r) & (jc < n_cmp_pad - 1)
        s_c = jnp.where(mask_c, s_c - slope * dist_c, NEG_INF)
        e_c = jnp.exp(s_c - jnp.max(s_c, axis=-1, keepdims=True))
        p_c = e_c / jnp.sum(e_c, axis=-1, keepdims=True) * mask_c.astype(F32)
        o_cmp = _bdot(p_c, vc_ref[0, g])

        p_sum = p_c[0:TQ]
        for r in range(1, R):
            p_sum = p_sum + p_c[r * TQ:(r + 1) * TQ]
        imp = jnp.dot(p_sum, ovl_ref[...], precision=HI, preferred_element_type=F32)
        kb = lax.broadcasted_iota(jnp.int32, (1, n_sel), 1)
        kbf = kb.astype(F32)
        blk_t = t_q >> int(math.log2(SEL_BLOCK))
        forced = (kb == 0) | (kb == blk_t) | (kb == blk_t - 1)
        cur = jnp.where(forced, FORCE_SCORE, jnp.where(kb <= blk_t, imp, -FORCE_SCORE))
        sel = jnp.zeros((TQ, n_sel), F32)
        for _ in range(min(N_SELECT, n_sel)):
            mx = jnp.max(cur, axis=-1, keepdims=True)
            first = jnp.min(jnp.where(cur == mx, kbf, float(n_sel)), axis=-1, keepdims=True)
            hit = kbf == first
            sel = jnp.where(hit, 1.0, sel)
            cur = jnp.where(hit, -3e38, cur)
        selb = sel.astype(BF16)

        def attend(j, carry, k_off, v_off, selected):
            m, l, acc = carry
            k_t = kv_ref[0, pl.ds(j * TK, TK), k_off + g * Dh:k_off + (g + 1) * Dh]
            v_t = kv_ref[0, pl.ds(j * TK, TK), v_off + g * Dh:v_off + (g + 1) * Dh]
            dist = t_r - (j * TK + lane_k)
            if selected:
                pick = jnp.dot(selb, exp_ref[j], preferred_element_type=F32)
                ok = (dist >= 0) & (jnp.concatenate([pick] * R, axis=0) > 0.5)
            else:
                ok = (dist >= 0) & (dist < WINDOW)
            s = jnp.where(ok, _bdot_nt(q4b, k_t) - slope * dist.astype(F32), NEG_INF)
            m_new = jnp.maximum(m, jnp.max(s, axis=-1, keepdims=True))
            alpha = jnp.exp(m - m_new)
            p = jnp.exp(s - m_new)
            l = alpha * l + jnp.sum(p, axis=-1, keepdims=True)
            acc = alpha * acc + _bdot(p, v_t)
            return m_new, l, acc

        init = (jnp.full((R * TQ, 1), NEG_INF, F32), jnp.zeros((R * TQ, 1), F32),
                jnp.zeros((R * TQ, Dh), F32))
        kvw = NSA_KV_WIDTH
        _, l_s, acc_s = lax.fori_loop(
            0, i + 1, functools.partial(attend, k_off=2 * kvw, v_off=3 * kvw, selected=True), init)
        _, l_w, acc_w = lax.fori_loop(
            jnp.maximum(i - WINDOW // TK, 0), i + 1,
            functools.partial(attend, k_off=4 * kvw, v_off=5 * kvw, selected=False), init)
        o_sel = acc_s / l_s
        o_win = acc_w / l_w

        for r in range(R):
            h = g * R + r
            rows = slice(r * TQ, (r + 1) * TQ)
            o_ref[0, :, h * Dh:(h + 1) * Dh] = (
                sgate[:, 3 * h:3 * h + 1] * o_cmp[rows]
                + sgate[:, 3 * h + 1:3 * h + 2] * o_sel[rows]
                + sgate[:, 3 * h + 2:3 * h + 3] * o_win[rows])


def _nsa_attn(zq, zkv, kcmp, vcmp, zgate, gate_b):
    B, S, _ = zq.shape
    n_sel = S // SEL_BLOCK
    n_cmp = (S - CMP_BLOCK) // CMP_STRIDE + 1
    n_cmp_pad = kcmp.shape[2]
    cmp_start = np.arange(n_cmp) * CMP_STRIDE
    sel_start = np.arange(n_sel) * SEL_BLOCK
    overlap = np.clip(np.minimum(cmp_start[:, None] + CMP_BLOCK, sel_start[None, :] + SEL_BLOCK)
                      - np.maximum(cmp_start[:, None], sel_start[None, :]), 0, None) / CMP_BLOCK
    ovl = np.zeros((n_cmp_pad, n_sel), np.float32)
    ovl[:n_cmp] = overlap
    n_kt = S // TK
    key_blk = (np.arange(n_kt)[:, None] * TK + np.arange(TK)[None, :]) // SEL_BLOCK
    expand = (np.arange(n_sel)[None, :, None] == key_blk[:, None, :]).astype(np.float32)
    slopes = (2.0 ** (-8.0 * np.arange(1, NSA_HEADS + 1) / NSA_HEADS)).astype(np.float32).reshape(1, NSA_HEADS)
    gb = jnp.pad(gate_b, (0, LANE - N_GATE)).reshape(1, LANE)
    fixed = lambda shape: pl.BlockSpec(shape, lambda b, i: (0,) * len(shape))
    return pl.pallas_call(
        _nsa_attn_kernel,
        grid=(B, S // TQ),
        in_specs=[pl.BlockSpec((1, TQ, NSA_WIDTH), lambda b, i: (b, i, 0)),
                  pl.BlockSpec((1, S, KV_IN), lambda b, i: (b, 0, 0)),
                  pl.BlockSpec((1, NSA_KV_HEADS, n_cmp_pad, NSA_HEAD_DIM), lambda b, i: (b, 0, 0, 0)),
                  pl.BlockSpec((1, NSA_KV_HEADS, n_cmp_pad, NSA_HEAD_DIM), lambda b, i: (b, 0, 0, 0)),
                  pl.BlockSpec((1, TQ, LANE), lambda b, i: (b, i, 0)),
                  fixed((1, LANE)), fixed((n_cmp_pad, n_sel)), fixed((n_kt, n_sel, TK)),
                  fixed((1, NSA_HEADS))],
        out_specs=pl.BlockSpec((1, TQ, NSA_WIDTH), lambda b, i: (b, i, 0)),
        out_shape=jax.ShapeDtypeStruct((B, S, NSA_WIDTH), F32),
        compiler_params=pltpu.CompilerParams(dimension_semantics=("parallel", "arbitrary"),
                                             vmem_limit_bytes=VMEM_LIMIT),
        name="nsa_attn",
    )(zq, zkv, kcmp, vcmp, zgate, gb,
      jnp.asarray(ovl), jnp.asarray(expand, dtype=BF16), jnp.asarray(slopes))


def _mix_kernel(x_ref, yr_ref, yn_ref, zmg_ref, ur_ref, un_ref, wo_ref, gf_ref, wr_ref, br_ref,
                x1_ref, h2_ref, rt_ref):
    tm = x_ref.shape[0]
    zmg = zmg_ref[...]
    mixed = (jax.nn.sigmoid(zmg[:, :D_MODEL]) * _bdot(yr_ref[...], ur_ref[...])
             + jax.nn.sigmoid(zmg[:, D_MODEL:]) * _bdot(yn_ref[...], un_ref[...]))
    x1 = x_ref[...] + _bdot(mixed, wo_ref[...])
    x1_ref[...] = x1
    h2 = _rms(x1, gf_ref[...])
    h2_ref[...] = h2
    logits = jnp.dot(h2, wr_ref[...], precision=HI, preferred_element_type=F32) + br_ref[...]
    lane = lax.broadcasted_iota(jnp.int32, (1, LANE), 1).astype(F32)
    gl = jnp.where(lane < N_GROUPS, logits, NEG_INF)
    gmax = jnp.max(gl, axis=-1, keepdims=True)
    g_sel = jnp.min(jnp.where(gl == gmax, lane, float(LANE)), axis=-1, keepdims=True)
    p_group = 1.0 / jnp.sum(jnp.exp(gl - gmax), axis=-1, keepdims=True)
    e_lane = lane - N_GROUPS
    in_grp = (e_lane >= g_sel * EXPERTS_PER_GROUP) & (e_lane < (g_sel + 1.0) * EXPERTS_PER_GROUP)
    el = jnp.where(in_grp, logits, NEG_INF)
    m1 = jnp.max(el, axis=-1, keepdims=True)
    i1 = jnp.min(jnp.where(el == m1, e_lane, float(LANE)), axis=-1, keepdims=True)
    el2 = jnp.where(e_lane == i1, 2.0 * NEG_INF, el)
    m2 = jnp.max(el2, axis=-1, keepdims=True)
    i2 = jnp.min(jnp.where(el2 == m2, e_lane, float(LANE)), axis=-1, keepdims=True)
    r2 = jnp.exp(m2 - m1)
    g1 = p_group / (1.0 + r2)
    g2 = p_group * r2 / (1.0 + r2)
    rt_ref[...] = jnp.where(lane == 0, i1, jnp.where(lane == 1, i2,
                                                     jnp.where(lane == 2, g1, jnp.where(lane == 3, g2, 0.0))))


def _mix(x2d, y_rw, y_nsa, zmg, w_up_r, w_up_n, w_out, g_ffn, w_group, b_group, w_router, b_router):
    T = x2d.shape[0]
    tm = TM_PROJ
    n_r = N_GROUPS + N_EXPERTS
    wr = jnp.pad(jnp.concatenate([w_group, w_router], axis=1), ((0, 0), (0, LANE - n_r)))
    br = jnp.pad(jnp.concatenate([b_group, b_router]), (0, LANE - n_r)).reshape(1, LANE)
    row = lambda i: (i, 0)
    fixed = lambda i: (0, 0)
    return pl.pallas_call(
        _mix_kernel,
        grid=(T // tm,),
        in_specs=[pl.BlockSpec((tm, D_MODEL), row), pl.BlockSpec((tm, RW_WIDTH), row),
                  pl.BlockSpec((tm, NSA_WIDTH), row), pl.BlockSpec((tm, 2 * D_MODEL), row),
                  pl.BlockSpec((RW_WIDTH, D_MODEL), fixed), pl.BlockSpec((NSA_WIDTH, D_MODEL), fixed),
                  pl.BlockSpec((D_MODEL, D_MODEL), fixed), pl.BlockSpec((1, D_MODEL), fixed),
                  pl.BlockSpec((D_MODEL, LANE), fixed), pl.BlockSpec((1, LANE), fixed)],
        out_specs=[pl.BlockSpec((tm, D_MODEL), row), pl.BlockSpec((tm, D_MODEL), row),
                   pl.BlockSpec((tm, LANE), row)],
        out_shape=[jax.ShapeDtypeStruct((T, D_MODEL), F32), jax.ShapeDtypeStruct((T, D_MODEL), F32),
                   jax.ShapeDtypeStruct((T, LANE), F32)],
        compiler_params=pltpu.CompilerParams(dimension_semantics=("parallel",),
                                             vmem_limit_bytes=VMEM_LIMIT),
        name="mix",
    )(x2d, y_rw, y_nsa, zmg, w_up_r.astype(BF16), w_up_n.astype(BF16), w_out.astype(BF16),
      g_ffn.reshape(1, D_MODEL), wr, br)


def _moe_kernel(be_ref, nv_ref, rtok_ref, rdst_ref, h2_hbm, rg_ref, wgu_ref, wd_ref, eo_hbm,
                xbuf, obuf, wgu_b, wd_b, gsem, ssem):
    i = pl.program_id(0)
    nv = nv_ref[i]
    base = i * ROW_BLOCK

    @pl.when(i == 0)
    def _():
        xbuf[...] = jnp.zeros_like(xbuf)

    changed = jnp.logical_or(i == 0, be_ref[i] != be_ref[jnp.maximum(i - 1, 0)])

    @pl.when(jnp.logical_and(nv > 0, changed))
    def _():
        wgu_b[...] = wgu_ref[0].astype(BF16)
        wd_b[...] = wd_ref[0].astype(BF16)

    def row_in(r):
        return pltpu.make_async_copy(h2_hbm.at[pl.ds(rtok_ref[base + r], 1)], xbuf.at[pl.ds(r, 1)], gsem)

    def row_out(r):
        return pltpu.make_async_copy(obuf.at[pl.ds(r, 1)], eo_hbm.at[pl.ds(rdst_ref[base + r], 1)], ssem)

    def start(copy):
        def body(r, c):
            copy(r).start()
            return c
        return body

    def wait(copy):
        def body(r, c):
            copy(r).wait()
            return c
        return body

    @pl.when(nv > 0)
    def _():
        lax.fori_loop(0, nv, start(row_in), 0)
        lax.fori_loop(0, nv, wait(row_in), 0)
        gu = jnp.dot(xbuf[...].astype(BF16), wgu_b[...], preferred_element_type=F32)
        gate_h, up_h = gu[:, :D_EXPERT], gu[:, D_EXPERT:]
        mid = gate_h * jax.nn.sigmoid(gate_h) * up_h
        obuf[...] = jnp.dot(mid.astype(BF16), wd_b[...], preferred_element_type=F32) * rg_ref[...]
        lax.fori_loop(0, nv, start(row_out), 0)
        lax.fori_loop(0, nv, wait(row_out), 0)


def _moe(h2, blk_expert, blk_nv, row_tok, row_dst, row_gate, w_gate_up, w_down):
    T = h2.shape[0]
    n_blk = blk_expert.shape[0]
    grid_spec = pltpu.PrefetchScalarGridSpec(
        num_scalar_prefetch=4,
        grid=(n_blk,),
        in_specs=[pl.BlockSpec(memory_space=pl.ANY),
                  pl.BlockSpec((ROW_BLOCK, 1), lambda i, be, nv, rt, rd: (i, 0)),
                  pl.BlockSpec((1, D_MODEL, 2 * D_EXPERT), lambda i, be, nv, rt, rd: (be[i], 0, 0)),
                  pl.BlockSpec((1, D_EXPERT, D_MODEL), lambda i, be, nv, rt, rd: (be[i], 0, 0))],
        out_specs=pl.BlockSpec(memory_space=pl.ANY),
        scratch_shapes=[pltpu.VMEM((ROW_BLOCK, D_MODEL), F32), pltpu.VMEM((ROW_BLOCK, D_MODEL), F32),
                        pltpu.VMEM((D_MODEL, 2 * D_EXPERT), BF16), pltpu.VMEM((D_EXPERT, D_MODEL), BF16),
                        pltpu.SemaphoreType.DMA(()), pltpu.SemaphoreType.DMA(())])
    return pl.pallas_call(
        _moe_kernel,
        grid_spec=grid_spec,
        out_shape=jax.ShapeDtypeStruct((TOP_K * T, D_MODEL), F32),
        compiler_params=pltpu.CompilerParams(dimension_semantics=("arbitrary",),
                                             vmem_limit_bytes=VMEM_LIMIT),
        name="moe",
    )(blk_expert, blk_nv, row_tok, row_dst, h2, row_gate, w_gate_up, w_down)


def _route_tables(rt, T):
    A = T * TOP_K
    n_rows = A + N_EXPERTS * ROW_BLOCK
    n_blk = n_rows // ROW_BLOCK
    e_flat = rt[:, 0:TOP_K].astype(jnp.int32).reshape(A)
    gate_flat = rt[:, TOP_K:2 * TOP_K].reshape(A)
    order = jnp.argsort(e_flat)
    e_sorted = e_flat[order]
    counts = jnp.bincount(e_flat, length=N_EXPERTS)
    starts = jnp.cumsum(counts) - counts
    padded = (counts + ROW_BLOCK - 1) // ROW_BLOCK * ROW_BLOCK
    pends = jnp.cumsum(padded)
    pstarts = pends - padded
    dest = pstarts[e_sorted] + jnp.arange(A) - starts[e_sorted]
    tok = (order // TOP_K).astype(jnp.int32)
    slot = (order % TOP_K).astype(jnp.int32)
    row_tok = jnp.zeros((n_rows,), jnp.int32).at[dest].set(tok)
    row_dst = jnp.zeros((n_rows,), jnp.int32).at[dest].set(slot * T + tok)
    row_gate = jnp.zeros((n_rows,), F32).at[dest].set(gate_flat[order]).reshape(n_rows, 1)
    blk_start = jnp.arange(n_blk) * ROW_BLOCK
    blk_expert = jnp.minimum(jnp.searchsorted(pends, blk_start, side="right"), N_EXPERTS - 1)
    blk_nv = jnp.clip(counts[blk_expert] - (blk_start - pstarts[blk_expert]), 0, ROW_BLOCK)
    return (blk_expert.astype(jnp.int32), blk_nv.astype(jnp.int32), row_tok, row_dst, row_gate)


def _ple_kernel(x1_ref, e0_ref, e1_ref, p_ref, wpp_ref, gpn_ref, ggi_ref, wpg_ref, gfin_ref, y_ref,
                *, last_layer):
    x2 = x1_ref[...] + (e0_ref[...] + e1_ref[...])
    e = _rms(_bdot(p_ref[...], wpp_ref[...]), gpn_ref[...])
    gate = jax.nn.sigmoid(_bdot(_rms(x2, ggi_ref[...]), wpg_ref[...]))
    x3 = x2 + gate * e
    y_ref[...] = _rms(x3, gfin_ref[...]) if last_layer else x3


def _ple(x1, eo, p2d, w_pp, g_pn, g_gi, w_pg, g_final, last_layer):
    T = x1.shape[0]
    tm = TM_PROJ
    nt = T // tm
    row = lambda i: (i, 0)
    fixed = lambda i: (0, 0)
    vec = lambda a: a.reshape(1, D_MODEL)
    return pl.pallas_call(
        functools.partial(_ple_kernel, last_layer=last_layer),
        grid=(nt,),
        in_specs=[pl.BlockSpec((tm, D_MODEL), row), pl.BlockSpec((tm, D_MODEL), row),
                  pl.BlockSpec((tm, D_MODEL), lambda i: (i + nt, 0)), pl.BlockSpec((tm, PLE_DIM), row),
                  pl.BlockSpec((PLE_DIM, D_MODEL), fixed), pl.BlockSpec((1, D_MODEL), fixed),
                  pl.BlockSpec((1, D_MODEL), fixed), pl.BlockSpec((D_MODEL, D_MODEL), fixed),
                  pl.BlockSpec((1, D_MODEL), fixed)],
        out_specs=pl.BlockSpec((tm, D_MODEL), row),
        out_shape=jax.ShapeDtypeStruct((T, D_MODEL), F32),
        compiler_params=pltpu.CompilerParams(dimension_semantics=("parallel",),
                                             vmem_limit_bytes=VMEM_LIMIT),
        name="ple",
    )(x1, eo, eo, p2d, w_pp.astype(BF16), vec(g_pn), vec(g_gi), w_pg.astype(BF16), vec(g_final))


def kernel(x, p, g_mix, w_in, mu_shift, rw_w0, rw_w2, rw_a0, rw_a2, rw_g2, rw_k_k, rw_k_a, rw_r_k, rw_ln_w, rw_ln_b, cmp_pos_k, cmp_pos_v, cmp_k_w1, cmp_k_w2, cmp_v_w1, cmp_v_w2, nsa_gate_b, w_up_rwkv, w_up_nsa, w_out, g_ffn, w_group, b_group, w_router, b_router, w_exp_gate_up, w_exp_down, w_ple_proj, g_ple_norm, g_ple_gate_in, w_ple_gate, g_final):
    B, S, D = x.shape
    T = B * S
    depth = p.shape[0]
    xc = x.reshape(T, D)
    for i in range(depth):
        zrw, zq, zkv, zmg, zgate = _proj(xc, g_mix[i], w_in[i], mu_shift[i], S)
        y_rw = _rwkv(zrw.reshape(B, S, RW_IN), rw_w0[i], rw_w2[i], rw_a0[i], rw_a2[i], rw_g2[i],
                     rw_k_k[i], rw_k_a[i], rw_r_k[i], rw_ln_w[i], rw_ln_b[i])
        zkv3 = zkv.reshape(B, S, KV_IN)
        kcmp, vcmp = _nsa_cmp(zkv3, cmp_pos_k[i], cmp_pos_v[i], cmp_k_w1[i], cmp_k_w2[i],
                              cmp_v_w1[i], cmp_v_w2[i])
        y_nsa = _nsa_attn(zq.reshape(B, S, NSA_WIDTH), zkv3, kcmp, vcmp, zgate.reshape(B, S, LANE),
                          nsa_gate_b[i])
        x1, h2, rt = _mix(xc, y_rw.reshape(T, RW_WIDTH), y_nsa.reshape(T, NSA_WIDTH), zmg,
                          w_up_rwkv[i], w_up_nsa[i], w_out[i], g_ffn[i], w_group[i], b_group[i],
                          w_router[i], b_router[i])
        tables = _route_tables(rt, T)
        eo = _moe(h2, *tables, w_exp_gate_up[i], w_exp_down[i])
        xc = _ple(x1, eo, p[i].reshape(T, PLE_DIM), w_ple_proj[i], g_ple_norm[i], g_ple_gate_in[i],
                  w_ple_gate[i], g_final, i == depth - 1)
    return xc.reshape(B, S, D)
```

```python
import functools
import math

import jax
import jax.numpy as jnp
import numpy as np
from jax import lax
from jax.experimental import pallas as pl
from jax.experimental.pallas import tpu as pltpu

F32 = jnp.float32
BF16 = jnp.bfloat16
HI = lax.Precision.HIGHEST

D_MODEL = 1024
RW_HEADS = 8
RW_HEAD_DIM = 64
RW_WIDTH = 512
DECAY_LORA = 64
AAA_LORA = 64
GATE_LORA = 128
GN_EPS = 64e-5
RW_IN = 3 * RW_WIDTH + DECAY_LORA + AAA_LORA + GATE_LORA

NSA_HEADS = 8
NSA_KV_HEADS = 2
NSA_REP = NSA_HEADS // NSA_KV_HEADS
NSA_HEAD_DIM = 64
NSA_WIDTH = 512
NSA_KV_WIDTH = 128
CMP_BLOCK = 32
CMP_STRIDE = 16
SEL_BLOCK = 64
N_SELECT = 8
WINDOW = 512
N_NSA_BRANCH = 3
FORCE_SCORE = 1e6
NEG_INF = -1e30

N_GROUPS = 4
EXPERTS_PER_GROUP = 8
N_EXPERTS = 32
TOP_K = 2
D_EXPERT = 512
ROW_BLOCK = 128
PLE_DIM = 256
NORM_EPS = 1e-6

N_GATE = N_NSA_BRANCH * NSA_HEADS
ATT_IN = NSA_WIDTH + 6 * NSA_KV_WIDTH
KV_OFF = RW_IN + NSA_WIDTH
KV_IN = 6 * NSA_KV_WIDTH
GATE_OFF = RW_IN + ATT_IN
MERGE_OFF = GATE_OFF + N_GATE
LANE = 128

RW_CHUNK = 64
RW_SUB = 16
TQ = 256
TK = 256
TM_PROJ = 256
VMEM_LIMIT = 56 * 1024 * 1024


def _bdot(a, b):
    return jnp.dot(a.astype(BF16), b.astype(BF16), preferred_element_type=F32)


def _bdot_nt(a, b):
    return lax.dot_general(a.astype(BF16), b.astype(BF16), (((1,), (1,)), ((), ())),
                           preferred_element_type=F32)


def _rms(x, g):
    return x * lax.rsqrt(jnp.mean(x * x, axis=-1, keepdims=True) + NORM_EPS) * g


def _proj_kernel(x_ref, g_ref, w_ref, mu_ref, zrw_ref, zq_ref, zkv_ref, zmg_ref, zgate_ref,
                 carry_ref, *, tiles_per_seq):
    i = pl.program_id(0)
    tm = x_ref.shape[0]

    @pl.when(i % tiles_per_seq == 0)
    def _():
        carry_ref[...] = jnp.zeros_like(carry_ref)

    h = _rms(x_ref[...], g_ref[...]).astype(BF16)
    z = jnp.dot(h, w_ref[:, 0:RW_IN], preferred_element_type=F32)
    row = lax.broadcasted_iota(jnp.int32, (tm, 1), 0)
    prev = jnp.where(row == 0, carry_ref[7:8, :], pltpu.roll(z, 1, 0))
    carry_ref[...] = z[tm - 8:tm, :]
    zrw_ref[...] = z + (prev - z) * mu_ref[...]
    zq_ref[...] = jnp.dot(h, w_ref[:, RW_IN:KV_OFF], preferred_element_type=F32)
    zkv_ref[...] = jnp.dot(h, w_ref[:, KV_OFF:GATE_OFF], preferred_element_type=F32)
    zmg_ref[...] = jnp.dot(h, w_ref[:, GATE_OFF:GATE_OFF + 2 * D_MODEL], preferred_element_type=F32)
    zgate_ref[...] = jnp.dot(h, w_ref[:, GATE_OFF + 2 * D_MODEL:], preferred_element_type=F32)


def _proj(x2d, g_mix, w_in, mu, seq):
    T = x2d.shape[0]
    tm = TM_PROJ
    wp = jnp.concatenate(
        [w_in[:, :GATE_OFF], w_in[:, MERGE_OFF:],
         jnp.pad(w_in[:, GATE_OFF:MERGE_OFF], ((0, 0), (0, LANE - N_GATE)))], axis=1).astype(BF16)
    npad = wp.shape[1]
    row = lambda i: (i, 0)
    fixed = lambda i: (0, 0)
    return pl.pallas_call(
        functools.partial(_proj_kernel, tiles_per_seq=seq // tm),
        grid=(T // tm,),
        in_specs=[pl.BlockSpec((tm, D_MODEL), row), pl.BlockSpec((1, D_MODEL), fixed),
                  pl.BlockSpec((D_MODEL, npad), fixed), pl.BlockSpec((1, RW_IN), fixed)],
        out_specs=[pl.BlockSpec((tm, RW_IN), row), pl.BlockSpec((tm, NSA_WIDTH), row),
                   pl.BlockSpec((tm, KV_IN), row), pl.BlockSpec((tm, 2 * D_MODEL), row),
                   pl.BlockSpec((tm, LANE), row)],
        out_shape=[jax.ShapeDtypeStruct((T, RW_IN), F32), jax.ShapeDtypeStruct((T, NSA_WIDTH), F32),
                   jax.ShapeDtypeStruct((T, KV_IN), F32), jax.ShapeDtypeStruct((T, 2 * D_MODEL), F32),
                   jax.ShapeDtypeStruct((T, LANE), F32)],
        scratch_shapes=[pltpu.VMEM((8, RW_IN), F32)],
        compiler_params=pltpu.CompilerParams(dimension_semantics=("arbitrary",),
                                             vmem_limit_bytes=VMEM_LIMIT),
        name="proj",
    )(x2d, g_mix.reshape(1, D_MODEL), wp, mu.reshape(1, RW_IN))


def _heads(x):
    return jnp.stack([x[:, h * RW_HEAD_DIM:(h + 1) * RW_HEAD_DIM] for h in range(RW_HEADS)], axis=0)


def _hmm(a, b):
    return jnp.einsum("hij,hjk->hik", a, b, precision=HI, preferred_element_type=F32)


def _hmm_nt(a, b):
    return jnp.einsum("hid,hjd->hij", a, b, precision=HI, preferred_element_type=F32)


def _unit_lower_inverse(a_strict, sub_mask, eye):
    ad = jnp.where(sub_mask, a_strict, 0.0)
    ao = a_strict - ad
    td = eye - ad
    pw = ad
    for _ in range(int(math.log2(RW_SUB)) - 1):
        pw = _hmm(pw, pw)
        td = _hmm(td, eye + pw)
    n = _hmm(td, ao)
    t = eye - n
    pw = n
    for _ in range(int(math.log2(RW_CHUNK // RW_SUB)) - 1):
        pw = _hmm(pw, pw)
        t = _hmm(t, eye + pw)
    return _hmm(t, td)


def _rwkv_kernel(z_ref, w0_ref, w2_ref, a0_ref, a2_ref, g2_ref, kk_ref, ka_ref, rk_ref, lnw_ref,
                 lnb_ref, hsum_ref, o_ref, s_ref):
    c = pl.program_id(1)
    C = RW_CHUNK

    @pl.when(c == 0)
    def _():
        s_ref[...] = jnp.zeros_like(s_ref)

    z = z_ref[0]
    zr, zk, zv = z[:, 0:512], z[:, 512:1024], z[:, 1024:1536]
    zw, za, zg = z[:, 1536:1600], z[:, 1600:1664], z[:, 1664:1792]
    w_raw = w0_ref[...] + _bdot(jnp.tanh(zw), w2_ref[...])
    logw = -jax.nn.sigmoid(w_raw) * math.exp(-0.5)
    a = jax.nn.sigmoid(a0_ref[...] + _bdot(za, a2_ref[...]))
    gate = _bdot(jax.nn.sigmoid(zg), g2_ref[...])
    kk = zk * kk_ref[...]
    ss = jnp.dot(kk * kk, hsum_ref[...], precision=HI, preferred_element_type=F32)
    kk = kk / jnp.maximum(jnp.sqrt(ss), 1e-12)
    k = zk * (1.0 + (a - 1.0) * ka_ref[...])
    b = kk * a

    ti = lax.broadcasted_iota(jnp.int32, (C, C), 0)
    si = lax.broadcasted_iota(jnp.int32, (C, C), 1)
    incl = ti >= si
    strict = ti > si
    cum = jnp.dot(incl.astype(F32), logw, precision=HI, preferred_element_type=F32)
    g_t = jnp.exp(cum)
    g_inv = jnp.exp(-cum)
    g_prev = jnp.exp(cum - logw)
    cum_last = cum[C - 1:C, :]
    g_end = jnp.exp(cum_last - cum)

    rt, kt, bt = _heads(zr * g_t), _heads(k * g_inv), _heads(b * g_inv)
    qt = _heads(kk * g_prev)
    kh, bh = _heads(k * g_end), _heads(b * g_end)
    v = _heads(zv)

    a_kb = jnp.where(strict[None], _hmm_nt(qt, bt), 0.0)
    a_kk = jnp.where(strict[None], _hmm_nt(qt, kt), 0.0)
    a_rk = jnp.where(incl[None], _hmm_nt(rt, kt), 0.0)
    a_rb = jnp.where(incl[None], _hmm_nt(rt, bt), 0.0)
    eye = (ti == si).astype(F32)[None]
    sub_shift = int(math.log2(RW_SUB))
    sub_mask = ((ti >> sub_shift) == (si >> sub_shift))[None]
    t_inv = _unit_lower_inverse(a_kb, sub_mask, eye)

    s = s_ref[...]
    u = _hmm(t_inv, _hmm_nt(qt, s) + _hmm(a_kk, v))
    o = _hmm_nt(rt, s) + _hmm(a_rk, v) - _hmm(a_rb, u)
    s_new = (s * jnp.exp(_heads(cum_last))
             + _hmm(jnp.swapaxes(v, 1, 2), kh) - _hmm(jnp.swapaxes(u, 1, 2), bh))
    s_ref[...] = s_new

    mu = jnp.mean(o, axis=-1, keepdims=True)
    var = jnp.mean(jnp.square(o - mu), axis=-1, keepdims=True)
    on = (o - mu) * lax.rsqrt(var + GN_EPS)
    rkk = zr * k * rk_ref[...]
    for h in range(RW_HEADS):
        sl = slice(h * RW_HEAD_DIM, (h + 1) * RW_HEAD_DIM)
        bonus = jnp.sum(rkk[:, sl], axis=-1, keepdims=True) * zv[:, sl]
        o_ref[0, :, sl] = (on[h] * lnw_ref[:, sl] + lnb_ref[:, sl] + bonus) * gate[:, sl]


def _rwkv(zrw, w0, w2, a0, a2, g2, k_k, k_a, r_k, ln_w, ln_b):
    B, S, _ = zrw.shape
    C = RW_CHUNK
    hid = np.arange(RW_WIDTH) // RW_HEAD_DIM
    hsum = jnp.asarray((hid[:, None] == hid[None, :]).astype(np.float32))
    vec = lambda a: a.reshape(1, RW_WIDTH)
    fixed = lambda shape: pl.BlockSpec(shape, lambda b, c: (0,) * len(shape))
    return pl.pallas_call(
        _rwkv_kernel,
        grid=(B, S // C),
        in_specs=[pl.BlockSpec((1, C, RW_IN), lambda b, c: (b, c, 0)),
                  fixed((1, RW_WIDTH)), fixed((DECAY_LORA, RW_WIDTH)),
                  fixed((1, RW_WIDTH)), fixed((AAA_LORA, RW_WIDTH)),
                  fixed((GATE_LORA, RW_WIDTH)), fixed((1, RW_WIDTH)), fixed((1, RW_WIDTH)),
                  fixed((1, RW_WIDTH)), fixed((1, RW_WIDTH)), fixed((1, RW_WIDTH)),
                  fixed((RW_WIDTH, RW_WIDTH))],
        out_specs=pl.BlockSpec((1, C, RW_WIDTH), lambda b, c: (b, c, 0)),
        out_shape=jax.ShapeDtypeStruct((B, S, RW_WIDTH), F32),
        scratch_shapes=[pltpu.VMEM((RW_HEADS, RW_HEAD_DIM, RW_HEAD_DIM), F32)],
        compiler_params=pltpu.CompilerParams(dimension_semantics=("parallel", "arbitrary"),
                                             vmem_limit_bytes=VMEM_LIMIT),
        name="rwkv",
    )(zrw, vec(w0), w2, vec(a0), a2, g2, vec(k_k), vec(k_a), vec(r_k), vec(ln_w), vec(ln_b), hsum)


def _gelu_tanh(x):
    return 0.5 * x * (1.0 + jnp.tanh(math.sqrt(2.0 / math.pi) * (x + 0.044715 * (x * x * x))))


def _key_features(pos_hi, pos_lo, block, n, n_sel):
    lane = lax.broadcasted_iota(jnp.int32, (n, NSA_HEAD_DIM), 1)
    feat = jnp.where(lane == n_sel, pos_hi, jnp.where(lane == n_sel + 1, pos_lo, 0.0))
    return feat if block is None else jnp.where(lane == block, 1.0, feat)


def _nsa_prep_kernel(zkc_ref, zvc_ref, zks_ref, zvs_ref, zkw_ref, zvw_ref, pk_ref, pv_ref, kw1_ref,
                     kw2_ref, vw1_ref, vw2_ref, kc_ref, vct_ref, ksa_ref, kwa_ref, vst_ref, vwt_ref):
    S = zkc_ref.shape[1]
    n_grp = S // CMP_STRIDE
    Dh = NSA_HEAD_DIM
    half = CMP_BLOCK // 2
    n_sel = S // SEL_BLOCK
    jrow = lax.broadcasted_iota(jnp.int32, (n_grp, 1), 0)
    cmp_feat = _key_features((jrow >> 3).astype(F32),
                             ((jrow & 7) * CMP_STRIDE).astype(F32) + 0.5 * (CMP_BLOCK - 1),
                             None, n_grp, n_sel)
    for is_v, (z_ref, pos_ref, w1_ref, w2_ref) in enumerate(((zkc_ref, pk_ref, kw1_ref, kw2_ref),
                                                            (zvc_ref, pv_ref, vw1_ref, vw2_ref))):
        for g in range(NSA_KV_HEADS):
            lo = jnp.zeros((n_grp, Dh), F32)
            hi = jnp.zeros((n_grp, Dh), F32)
            for l in range(half):
                xs = z_ref[0, pl.ds(l, n_grp, stride=CMP_STRIDE), :]
                xg = xs[:, g * Dh:(g + 1) * Dh]
                lo = lo + _bdot(xg + pos_ref[l:l + 1, :], w1_ref[l * Dh:(l + 1) * Dh, :])
                hi = hi + _bdot(xg + pos_ref[half + l:half + l + 1, :],
                                w1_ref[(half + l) * Dh:(half + l + 1) * Dh, :])
            pre = lo + pltpu.roll(hi, n_grp - 1, 0)
            out = jnp.where(jrow < n_grp - 1, _bdot(_gelu_tanh(pre), w2_ref[...]), 0.0)
            if is_v:
                out_t = jnp.concatenate([out, jnp.zeros_like(out)], axis=1).T
                vct_ref[0, g] = out_t[0:Dh, :].astype(BF16)
            else:
                kc_ref[0, g] = jnp.concatenate([out, cmp_feat], axis=1)

    prow = lax.broadcasted_iota(jnp.int32, (S, 1), 0)
    p_hi, p_lo = (prow >> 7).astype(F32), (prow & (LANE - 1)).astype(F32)
    for z_ref, out_ref, block in ((zks_ref, ksa_ref, prow >> int(math.log2(SEL_BLOCK))),
                                  (zkw_ref, kwa_ref, None)):
        kfull = z_ref[0]
        key_feat = _key_features(p_hi, p_lo, block, S, n_sel)
        for g in range(NSA_KV_HEADS):
            out_ref[0, g] = jnp.concatenate([kfull[:, g * Dh:(g + 1) * Dh], key_feat], axis=1).astype(BF16)
    for z_ref, out_ref in ((zvs_ref, vst_ref), (zvw_ref, vwt_ref)):
        for j in range(S // TK):
            out_ref[0, j] = z_ref[0, j * TK:(j + 1) * TK, :].T.astype(BF16)


def _nsa_prep(zkv, pos_k, pos_v, kw1, kw2, vw1, vw2):
    B, S, _ = zkv.shape
    n_grp = S // CMP_STRIDE
    n_kt = S // TK
    Dh = NSA_HEAD_DIM
    G = NSA_KV_HEADS
    fixed = lambda shape: pl.BlockSpec(shape, lambda b: (0,) * len(shape))
    col = lambda c: pl.BlockSpec((1, S, NSA_KV_WIDTH), lambda b: (b, 0, c))
    whole = lambda shape: pl.BlockSpec((1,) + shape, lambda b: (b,) + (0,) * len(shape))
    shapes = [((G, n_grp, 2 * Dh), F32), ((G, Dh, n_grp), BF16), ((G, S, 2 * Dh), BF16),
              ((G, S, 2 * Dh), BF16), ((n_kt, G * Dh, TK), BF16), ((n_kt, G * Dh, TK), BF16)]
    return pl.pallas_call(
        _nsa_prep_kernel,
        grid=(B,),
        in_specs=[col(c) for c in range(6)] + [
            fixed((CMP_BLOCK, Dh)), fixed((CMP_BLOCK, Dh)),
            fixed((CMP_BLOCK * Dh, Dh)), fixed((Dh, Dh)),
            fixed((CMP_BLOCK * Dh, Dh)), fixed((Dh, Dh))],
        out_specs=[whole(s) for s, _ in shapes],
        out_shape=[jax.ShapeDtypeStruct((B,) + s, d) for s, d in shapes],
        compiler_params=pltpu.CompilerParams(dimension_semantics=("parallel",),
                                             vmem_limit_bytes=VMEM_LIMIT),
        name="nsa_prep",
    )(zkv, zkv, zkv, zkv, zkv, zkv, pos_k, pos_v, kw1, kw2, vw1, vw2)


def _nsa_attn_kernel(q_ref, kc_ref, vct_ref, ksa_ref, kwa_ref, vst_ref, vwt_ref, gl_ref, gb_ref,
                     ovlt_ref, slope_ref, o_ref, acc_ref, ot_ref):
    i = pl.program_id(1)
    Dh = NSA_HEAD_DIM
    R = NSA_REP
    N = R * TQ
    n_cmp_pad = kc_ref.shape[2]
    n_sel = ovlt_ref.shape[0]
    G = NSA_KV_HEADS
    nt = (((1,), (1,)), ((), ()))
    log2e = math.log2(math.e)
    t0 = i * TQ
    t_row = t0 + lax.broadcasted_iota(jnp.int32, (1, TQ), 1)
    c_col = lax.broadcasted_iota(jnp.int32, (TK, 1), 0)
    sgate_t = jax.nn.sigmoid(gl_ref[0] + gb_ref[...]).T
    lane_f = lax.broadcasted_iota(jnp.int32, (1, Dh), 1)
    heads = lambda x: jnp.concatenate([x] * R, axis=1)

    def key_dist(j):
        return t_row - (j * TK + c_col)

    def queries(g, sel_feat):
        parts = []
        for r in range(R):
            h = g * R + r
            sl = slope_ref[:, h:h + 1] * log2e
            feat = jnp.where(lane_f == n_sel, sl * LANE, jnp.where(lane_f == n_sel + 1, sl, sel_feat))
            parts.append(jnp.concatenate(
                [q_ref[0, :, h * Dh:(h + 1) * Dh] * (Dh ** -0.5 * log2e), jnp.broadcast_to(feat, (TQ, Dh))],
                axis=1))
        return jnp.concatenate(parts, axis=0)

    qab, o_cmp = [], []
    for g in range(G):
        jc = lax.broadcasted_iota(jnp.int32, (n_cmp_pad, 1), 0)
        ok_c = (jc * CMP_STRIDE + (CMP_BLOCK - 1) <= t_row) & (jc < n_cmp_pad - 1)
        s_c = (lax.dot_general(kc_ref[0, g], queries(g, 0.0), nt, precision=HI, preferred_element_type=F32)
               + heads(jnp.where(ok_c, 0.0, NEG_INF)))
        e_c = jnp.exp2(s_c - jnp.max(s_c, axis=0, keepdims=True))
        any_c = heads(t_row >= CMP_BLOCK - 1)
        p_c = e_c * jnp.where(any_c, 1.0 / jnp.sum(e_c, axis=0, keepdims=True), 0.0)
        o_cmp.append(jnp.dot(vct_ref[0, g], p_c.astype(BF16), preferred_element_type=F32))

        p_sum = p_c[:, 0:TQ]
        for r in range(1, R):
            p_sum = p_sum + p_c[:, r * TQ:(r + 1) * TQ]
        imp = jnp.dot(ovlt_ref[...], p_sum, precision=HI, preferred_element_type=F32)
        kb = lax.broadcasted_iota(jnp.int32, (n_sel, 1), 0)
        kbf = kb.astype(F32)
        blk_t = t_row >> int(math.log2(SEL_BLOCK))
        forced = (kb == 0) | (kb == blk_t) | (kb == blk_t - 1)
        cur = jnp.where(forced, FORCE_SCORE, jnp.where(kb <= blk_t, imp, -FORCE_SCORE))
        sel_bias = jnp.full((n_sel, TQ), NEG_INF, F32)
        for _ in range(min(N_SELECT, n_sel)):
            mx = jnp.max(cur, axis=0, keepdims=True)
            first = jnp.min(jnp.where(cur == mx, kbf, float(n_sel)), axis=0, keepdims=True)
            hit = kbf == first
            sel_bias = jnp.where(hit, 0.0, sel_bias)
            cur = jnp.where(hit, -3e38, cur)
        sel_feat = jnp.concatenate([sel_bias, jnp.zeros((LANE - n_sel, TQ), F32)], axis=0).T[:, 0:Dh]
        qab.append(queries(g, sel_feat).astype(BF16))

    def tile(j, carry, k_ref, vt_ref, bias, slot):
        out = []
        for g in range(G):
            m, l = carry[g]
            k = k_ref[0, g, pl.ds(pl.multiple_of(j * TK, TK), TK), :]
            s = lax.dot_general(k, qab[g], nt, preferred_element_type=F32)
            if bias is not None:
                s = s + heads(bias)
            m_new = jnp.maximum(m, jnp.max(s, axis=0, keepdims=True))
            alpha = jnp.exp2(m - m_new)
            p = jnp.exp2(s - m_new)
            l = alpha * l + jnp.sum(p, axis=0, keepdims=True)
            vt = vt_ref[0, j, g * Dh:(g + 1) * Dh, :]
            acc_ref[slot, g] = (alpha * acc_ref[slot, g]
                                + jnp.dot(vt, p.astype(BF16), preferred_element_type=F32))
            out.append((m_new, l))
        return tuple(out)

    def window_bias(j):
        d = key_dist(j)
        return jnp.where((d >= 0) & (d < WINDOW), 0.0, NEG_INF)

    init = ((jnp.full((1, N), NEG_INF, F32), jnp.zeros((1, N), F32)),) * G
    acc_ref[...] = jnp.zeros_like(acc_ref)
    causal = jnp.where(key_dist(i) >= 0, 0.0, NEG_INF)
    carry = lax.fori_loop(0, i, lambda j, c: tile(j, c, ksa_ref, vst_ref, None, 0), init)
    stat_s = tile(i, carry, ksa_ref, vst_ref, causal, 0)
    carry = lax.fori_loop(jnp.maximum(i - WINDOW // TK, 0), i,
                          lambda j, c: tile(j, c, kwa_ref, vwt_ref, window_bias(j), 1), init)
    stat_w = tile(i, carry, kwa_ref, vwt_ref, causal, 1)

    for g in range(G):
        o_sel = acc_ref[0, g] * (1.0 / stat_s[g][1])
        o_win = acc_ref[1, g] * (1.0 / stat_w[g][1])
        for r in range(R):
            h = g * R + r
            cols = slice(r * TQ, (r + 1) * TQ)
            ot_ref[h * Dh:(h + 1) * Dh, :] = (
                sgate_t[3 * h:3 * h + 1, :] * o_cmp[g][:, cols]
                + sgate_t[3 * h + 1:3 * h + 2, :] * o_sel[:, cols]
                + sgate_t[3 * h + 2:3 * h + 3, :] * o_win[:, cols])
    o_ref[0] = ot_ref[...].T


def _nsa_attn(zq, kc, vct, ksa, kwa, vst, vwt, zgate, gate_b):
    B, S, _ = zq.shape
    n_sel = S // SEL_BLOCK
    n_cmp = (S - CMP_BLOCK) // CMP_STRIDE + 1
    n_cmp_pad = kc.shape[2]
    n_kt = S // TK
    G, Dh = NSA_KV_HEADS, NSA_HEAD_DIM
    cmp_start = np.arange(n_cmp) * CMP_STRIDE
    sel_start = np.arange(n_sel) * SEL_BLOCK
    overlap = np.clip(np.minimum(cmp_start[:, None] + CMP_BLOCK, sel_start[None, :] + SEL_BLOCK)
                      - np.maximum(cmp_start[:, None], sel_start[None, :]), 0, None) / CMP_BLOCK
    ovlt = np.zeros((n_sel, n_cmp_pad), np.float32)
    ovlt[:, :n_cmp] = overlap.T
    slopes = (2.0 ** (-8.0 * np.arange(1, NSA_HEADS + 1) / NSA_HEADS)).astype(np.float32).reshape(1, NSA_HEADS)
    gb = jnp.pad(gate_b, (0, LANE - N_GATE)).reshape(1, LANE)
    fixed = lambda shape: pl.BlockSpec(shape, lambda b, i: (0,) * len(shape))
    per_b = lambda shape: pl.BlockSpec((1,) + shape, lambda b, i: (b,) + (0,) * len(shape))
    return pl.pallas_call(
        _nsa_attn_kernel,
        grid=(B, S // TQ),
        in_specs=[pl.BlockSpec((1, TQ, NSA_WIDTH), lambda b, i: (b, i, 0)),
                  per_b((G, n_cmp_pad, 2 * Dh)), per_b((G, Dh, n_cmp_pad)),
                  per_b((G, S, 2 * Dh)), per_b((G, S, 2 * Dh)),
                  per_b((n_kt, G * Dh, TK)), per_b((n_kt, G * Dh, TK)),
                  pl.BlockSpec((1, TQ, LANE), lambda b, i: (b, i, 0)),
                  fixed((1, LANE)), fixed((n_sel, n_cmp_pad)), fixed((1, NSA_HEADS))],
        out_specs=pl.BlockSpec((1, TQ, NSA_WIDTH), lambda b, i: (b, i, 0)),
        out_shape=jax.ShapeDtypeStruct((B, S, NSA_WIDTH), F32),
        scratch_shapes=[pltpu.VMEM((2, G, Dh, NSA_REP * TQ), F32),
                        pltpu.VMEM((NSA_WIDTH, TQ), F32)],
        compiler_params=pltpu.CompilerParams(dimension_semantics=("parallel", "arbitrary"),
                                             vmem_limit_bytes=VMEM_LIMIT),
        name="nsa_attn",
    )(zq, kc, vct, ksa, kwa, vst, vwt, zgate, gb, jnp.asarray(ovlt), jnp.asarray(slopes))


def _mix_kernel(x_ref, yr_ref, yn_ref, zmg_ref, ur_ref, un_ref, wo_ref, gf_ref, wr_ref, br_ref,
                x1_ref, h2_ref, rt_ref):
    tm = x_ref.shape[0]
    zmg = zmg_ref[...]
    mixed = (jax.nn.sigmoid(zmg[:, :D_MODEL]) * _bdot(yr_ref[...], ur_ref[...])
             + jax.nn.sigmoid(zmg[:, D_MODEL:]) * _bdot(yn_ref[...], un_ref[...]))
    x1 = x_ref[...] + _bdot(mixed, wo_ref[...])
    x1_ref[...] = x1
    h2 = _rms(x1, gf_ref[...])
    h2_ref[...] = h2
    logits = jnp.dot(h2, wr_ref[...], precision=HI, preferred_element_type=F32) + br_ref[...]
    lane = lax.broadcasted_iota(jnp.int32, (1, LANE), 1).astype(F32)
    gl = jnp.where(lane < N_GROUPS, logits, NEG_INF)
    gmax = jnp.max(gl, axis=-1, keepdims=True)
    g_sel = jnp.min(jnp.where(gl == gmax, lane, float(LANE)), axis=-1, keepdims=True)
    p_group = 1.0 / jnp.sum(jnp.exp(gl - gmax), axis=-1, keepdims=True)
    e_lane = lane - N_GROUPS
    in_grp = (e_lane >= g_sel * EXPERTS_PER_GROUP) & (e_lane < (g_sel + 1.0) * EXPERTS_PER_GROUP)
    el = jnp.where(in_grp, logits, NEG_INF)
    m1 = jnp.max(el, axis=-1, keepdims=True)
    i1 = jnp.min(jnp.where(el == m1, e_lane, float(LANE)), axis=-1, keepdims=True)
    el2 = jnp.where(e_lane == i1, 2.0 * NEG_INF, el)
    m2 = jnp.max(el2, axis=-1, keepdims=True)
    i2 = jnp.min(jnp.where(el2 == m2, e_lane, float(LANE)), axis=-1, keepdims=True)
    r2 = jnp.exp(m2 - m1)
    g1 = p_group / (1.0 + r2)
    g2 = p_group * r2 / (1.0 + r2)
    rt_ref[...] = jnp.where(lane == 0, i1, jnp.where(lane == 1, i2,
                                                     jnp.where(lane == 2, g1, jnp.where(lane == 3, g2, 0.0))))


def _mix(x2d, y_rw, y_nsa, zmg, w_up_r, w_up_n, w_out, g_ffn, w_group, b_group, w_router, b_router):
    T = x2d.shape[0]
    tm = TM_PROJ
    n_r = N_GROUPS + N_EXPERTS
    wr = jnp.pad(jnp.concatenate([w_group, w_router], axis=1), ((0, 0), (0, LANE - n_r)))
    br = jnp.pad(jnp.concatenate([b_group, b_router]), (0, LANE - n_r)).reshape(1, LANE)
    row = lambda i: (i, 0)
    fixed = lambda i: (0, 0)
    return pl.pallas_call(
        _mix_kernel,
        grid=(T // tm,),
        in_specs=[pl.BlockSpec((tm, D_MODEL), row), pl.BlockSpec((tm, RW_WIDTH), row),
                  pl.BlockSpec((tm, NSA_WIDTH), row), pl.BlockSpec((tm, 2 * D_MODEL), row),
                  pl.BlockSpec((RW_WIDTH, D_MODEL), fixed), pl.BlockSpec((NSA_WIDTH, D_MODEL), fixed),
                  pl.BlockSpec((D_MODEL, D_MODEL), fixed), pl.BlockSpec((1, D_MODEL), fixed),
                  pl.BlockSpec((D_MODEL, LANE), fixed), pl.BlockSpec((1, LANE), fixed)],
        out_specs=[pl.BlockSpec((tm, D_MODEL), row), pl.BlockSpec((tm, D_MODEL), row),
                   pl.BlockSpec((tm, LANE), row)],
        out_shape=[jax.ShapeDtypeStruct((T, D_MODEL), F32), jax.ShapeDtypeStruct((T, D_MODEL), F32),
                   jax.ShapeDtypeStruct((T, LANE), F32)],
        compiler_params=pltpu.CompilerParams(dimension_semantics=("parallel",),
                                             vmem_limit_bytes=VMEM_LIMIT),
        name="mix",
    )(x2d, y_rw, y_nsa, zmg, w_up_r.astype(BF16), w_up_n.astype(BF16), w_out.astype(BF16),
      g_ffn.reshape(1, D_MODEL), wr, br)


def _moe_kernel(be_ref, nv_ref, rtok_ref, rdst_ref, h2_hbm, rg_ref, wgu_ref, wd_ref, eo_hbm,
                xbuf, obuf, wgu_b, wd_b, gsem, ssem):
    i = pl.program_id(0)
    nv = nv_ref[i]
    base = i * ROW_BLOCK

    @pl.when(i == 0)
    def _():
        xbuf[...] = jnp.zeros_like(xbuf)

    changed = jnp.logical_or(i == 0, be_ref[i] != be_ref[jnp.maximum(i - 1, 0)])

    @pl.when(jnp.logical_and(nv > 0, changed))
    def _():
        wgu_b[...] = wgu_ref[0].astype(BF16)
        wd_b[...] = wd_ref[0].astype(BF16)

    def row_in(r):
        return pltpu.make_async_copy(h2_hbm.at[pl.ds(rtok_ref[base + r], 1)], xbuf.at[pl.ds(r, 1)], gsem)

    def row_out(r):
        return pltpu.make_async_copy(obuf.at[pl.ds(r, 1)], eo_hbm.at[pl.ds(rdst_ref[base + r], 1)], ssem)

    def start(copy):
        def body(r, c):
            copy(r).start()
            return c
        return body

    def wait(copy):
        def body(r, c):
            copy(r).wait()
            return c
        return body

    @pl.when(nv > 0)
    def _():
        lax.fori_loop(0, nv, start(row_in), 0)
        lax.fori_loop(0, nv, wait(row_in), 0)
        gu = jnp.dot(xbuf[...].astype(BF16), wgu_b[...], preferred_element_type=F32)
        gate_h, up_h = gu[:, :D_EXPERT], gu[:, D_EXPERT:]
        mid = gate_h * jax.nn.sigmoid(gate_h) * up_h
        obuf[...] = jnp.dot(mid.astype(BF16), wd_b[...], preferred_element_type=F32) * rg_ref[...]
        lax.fori_loop(0, nv, start(row_out), 0)
        lax.fori_loop(0, nv, wait(row_out), 0)


def _moe(h2, blk_expert, blk_nv, row_tok, row_dst, row_gate, w_gate_up, w_down):
    T = h2.shape[0]
    n_blk = blk_expert.shape[0]
    grid_spec = pltpu.PrefetchScalarGridSpec(
        num_scalar_prefetch=4,
        grid=(n_blk,),
        in_specs=[pl.BlockSpec(memory_space=pl.ANY),
                  pl.BlockSpec((ROW_BLOCK, 1), lambda i, be, nv, rt, rd: (i, 0)),
                  pl.BlockSpec((1, D_MODEL, 2 * D_EXPERT), lambda i, be, nv, rt, rd: (be[i], 0, 0)),
                  pl.BlockSpec((1, D_EXPERT, D_MODEL), lambda i, be, nv, rt, rd: (be[i], 0, 0))],
        out_specs=pl.BlockSpec(memory_space=pl.ANY),
        scratch_shapes=[pltpu.VMEM((ROW_BLOCK, D_MODEL), F32), pltpu.VMEM((ROW_BLOCK, D_MODEL), F32),
                        pltpu.VMEM((D_MODEL, 2 * D_EXPERT), BF16), pltpu.VMEM((D_EXPERT, D_MODEL), BF16),
                        pltpu.SemaphoreType.DMA(()), pltpu.SemaphoreType.DMA(())])
    return pl.pallas_call(
        _moe_kernel,
        grid_spec=grid_spec,
        out_shape=jax.ShapeDtypeStruct((TOP_K * T, D_MODEL), F32),
        compiler_params=pltpu.CompilerParams(dimension_semantics=("arbitrary",),
                                             vmem_limit_bytes=VMEM_LIMIT),
        name="moe",
    )(blk_expert, blk_nv, row_tok, row_dst, h2, row_gate, w_gate_up, w_down)


def _route_tables(rt, T):
    A = T * TOP_K
    n_rows = A + N_EXPERTS * ROW_BLOCK
    n_blk = n_rows // ROW_BLOCK
    e_flat = rt[:, 0:TOP_K].astype(jnp.int32).reshape(A)
    gate_flat = rt[:, TOP_K:2 * TOP_K].reshape(A)
    order = jnp.argsort(e_flat)
    e_sorted = e_flat[order]
    counts = jnp.bincount(e_flat, length=N_EXPERTS)
    starts = jnp.cumsum(counts) - counts
    padded = (counts + ROW_BLOCK - 1) // ROW_BLOCK * ROW_BLOCK
    pends = jnp.cumsum(padded)
    pstarts = pends - padded
    dest = pstarts[e_sorted] + jnp.arange(A) - starts[e_sorted]
    tok = (order // TOP_K).astype(jnp.int32)
    slot = (order % TOP_K).astype(jnp.int32)
    row_tok = jnp.zeros((n_rows,), jnp.int32).at[dest].set(tok)
    row_dst = jnp.zeros((n_rows,), jnp.int32).at[dest].set(slot * T + tok)
    row_gate = jnp.zeros((n_rows,), F32).at[dest].set(gate_flat[order]).reshape(n_rows, 1)
    blk_start = jnp.arange(n_blk) * ROW_BLOCK
    blk_expert = jnp.minimum(jnp.searchsorted(pends, blk_start, side="right"), N_EXPERTS - 1)
    blk_nv = jnp.clip(counts[blk_expert] - (blk_start - pstarts[blk_expert]), 0, ROW_BLOCK)
    return (blk_expert.astype(jnp.int32), blk_nv.astype(jnp.int32), row_tok, row_dst, row_gate)


def _ple_kernel(x1_ref, e0_ref, e1_ref, p_ref, wpp_ref, gpn_ref, ggi_ref, wpg_ref, gfin_ref, y_ref,
                *, last_layer):
    x2 = x1_ref[...] + (e0_ref[...] + e1_ref[...])
    e = _rms(_bdot(p_ref[...], wpp_ref[...]), gpn_ref[...])
    gate = jax.nn.sigmoid(_bdot(_rms(x2, ggi_ref[...]), wpg_ref[...]))
    x3 = x2 + gate * e
    y_ref[...] = _rms(x3, gfin_ref[...]) if last_layer else x3


def _ple(x1, eo, p2d, w_pp, g_pn, g_gi, w_pg, g_final, last_layer):
    T = x1.shape[0]
    tm = TM_PROJ
    nt = T // tm
    row = lambda i: (i, 0)
    fixed = lambda i: (0, 0)
    vec = lambda a: a.reshape(1, D_MODEL)
    return pl.pallas_call(
        functools.partial(_ple_kernel, last_layer=last_layer),
        grid=(nt,),
        in_specs=[pl.BlockSpec((tm, D_MODEL), row), pl.BlockSpec((tm, D_MODEL), row),
                  pl.BlockSpec((tm, D_MODEL), lambda i: (i + nt, 0)), pl.BlockSpec((tm, PLE_DIM), row),
                  pl.BlockSpec((PLE_DIM, D_MODEL), fixed), pl.BlockSpec((1, D_MODEL), fixed),
                  pl.BlockSpec((1, D_MODEL), fixed), pl.BlockSpec((D_MODEL, D_MODEL), fixed),
                  pl.BlockSpec((1, D_MODEL), fixed)],
        out_specs=pl.BlockSpec((tm, D_MODEL), row),
        out_shape=jax.ShapeDtypeStruct((T, D_MODEL), F32),
        compiler_params=pltpu.CompilerParams(dimension_semantics=("parallel",),
                                             vmem_limit_bytes=VMEM_LIMIT),
        name="ple",
    )(x1, eo, eo, p2d, w_pp.astype(BF16), vec(g_pn), vec(g_gi), w_pg.astype(BF16), vec(g_final))


def kernel(x, p, g_mix, w_in, mu_shift, rw_w0, rw_w2, rw_a0, rw_a2, rw_g2, rw_k_k, rw_k_a, rw_r_k, rw_ln_w, rw_ln_b, cmp_pos_k, cmp_pos_v, cmp_k_w1, cmp_k_w2, cmp_v_w1, cmp_v_w2, nsa_gate_b, w_up_rwkv, w_up_nsa, w_out, g_ffn, w_group, b_group, w_router, b_router, w_exp_gate_up, w_exp_down, w_ple_proj, g_ple_norm, g_ple_gate_in, w_ple_gate, g_final):
    B, S, D = x.shape
    T = B * S
    depth = p.shape[0]
    xc = x.reshape(T, D)
    for i in range(depth):
        zrw, zq, zkv, zmg, zgate = _proj(xc, g_mix[i], w_in[i], mu_shift[i], S)
        y_rw = _rwkv(zrw.reshape(B, S, RW_IN), rw_w0[i], rw_w2[i], rw_a0[i], rw_a2[i], rw_g2[i],
                     rw_k_k[i], rw_k_a[i], rw_r_k[i], rw_ln_w[i], rw_ln_b[i])
        zkv3 = zkv.reshape(B, S, KV_IN)
        nsa_kv = _nsa_prep(zkv3, cmp_pos_k[i], cmp_pos_v[i], cmp_k_w1[i], cmp_k_w2[i],
                           cmp_v_w1[i], cmp_v_w2[i])
        y_nsa = _nsa_attn(zq.reshape(B, S, NSA_WIDTH), *nsa_kv, zgate.reshape(B, S, LANE),
                          nsa_gate_b[i])
        x1, h2, rt = _mix(xc, y_rw.reshape(T, RW_WIDTH), y_nsa.reshape(T, NSA_WIDTH), zmg,
                          w_up_rwkv[i], w_up_nsa[i], w_out[i], g_ffn[i], w_group[i], b_group[i],
                          w_router[i], b_router[i])
        tables = _route_tables(rt, T)
        eo = _moe(h2, *tables, w_exp_gate_up[i], w_exp_down[i])
        xc = _ple(x1, eo, p[i].reshape(T, PLE_DIM), w_ple_proj[i], g_ple_norm[i], g_ple_gate_in[i],
                  w_ple_gate[i], g_final, i == depth - 1)
    return xc.reshape(B, S, D)
```

```python
import functools
import math

import jax
import jax.numpy as jnp
import numpy as np
from jax import lax
from jax.experimental import pallas as pl
from jax.experimental.pallas import tpu as pltpu

F32 = jnp.float32
BF16 = jnp.bfloat16
HI = lax.Precision.HIGHEST

D_MODEL = 1024
RW_HEADS = 8
RW_HEAD_DIM = 64
RW_WIDTH = 512
DECAY_LORA = 64
AAA_LORA = 64
GATE_LORA = 128
GN_EPS = 64e-5
RW_IN = 3 * RW_WIDTH + DECAY_LORA + AAA_LORA + GATE_LORA

NSA_HEADS = 8
NSA_KV_HEADS = 2
NSA_REP = NSA_HEADS // NSA_KV_HEADS
NSA_HEAD_DIM = 64
NSA_WIDTH = 512
NSA_KV_WIDTH = 128
CMP_BLOCK = 32
CMP_STRIDE = 16
SEL_BLOCK = 64
N_SELECT = 8
WINDOW = 512
N_NSA_BRANCH = 3
FORCE_SCORE = 1e6
NEG_INF = -1e30

N_GROUPS = 4
EXPERTS_PER_GROUP = 8
N_EXPERTS = 32
TOP_K = 2
D_EXPERT = 512
ROW_BLOCK = 128
PLE_DIM = 256
NORM_EPS = 1e-6

N_GATE = N_NSA_BRANCH * NSA_HEADS
ATT_IN = NSA_WIDTH + 6 * NSA_KV_WIDTH
KV_OFF = RW_IN + NSA_WIDTH
KV_IN = 6 * NSA_KV_WIDTH
GATE_OFF = RW_IN + ATT_IN
MERGE_OFF = GATE_OFF + N_GATE
LANE = 128

RW_CHUNK = 64
RW_SUB = 16
RW_ROWS = 4
TQ = 256
TK = 256
TM_PROJ = 256
VMEM_LIMIT = 56 * 1024 * 1024


def _bdot(a, b):
    return jnp.dot(a.astype(BF16), b.astype(BF16), preferred_element_type=F32)


def _bdot_nt(a, b):
    return lax.dot_general(a.astype(BF16), b.astype(BF16), (((1,), (1,)), ((), ())),
                           preferred_element_type=F32)


def _rms(x, g):
    return x * lax.rsqrt(jnp.mean(x * x, axis=-1, keepdims=True) + NORM_EPS) * g


def _proj_kernel(x_ref, g_ref, w_ref, mu_ref, zrw_ref, zq_ref, zkv_ref, zmg_ref, zgate_ref,
                 carry_ref, *, tiles_per_seq):
    i = pl.program_id(0)
    tm = x_ref.shape[0]

    @pl.when(i % tiles_per_seq == 0)
    def _():
        carry_ref[...] = jnp.zeros_like(carry_ref)

    h = _rms(x_ref[...], g_ref[...]).astype(BF16)
    z = jnp.dot(h, w_ref[:, 0:RW_IN], preferred_element_type=F32)
    row = lax.broadcasted_iota(jnp.int32, (tm, 1), 0)
    prev = jnp.where(row == 0, carry_ref[7:8, :], pltpu.roll(z, 1, 0))
    carry_ref[...] = z[tm - 8:tm, :]
    zrw_ref[...] = z + (prev - z) * mu_ref[...]
    zq_ref[...] = jnp.dot(h, w_ref[:, RW_IN:KV_OFF], preferred_element_type=F32)
    zkv_ref[...] = jnp.dot(h, w_ref[:, KV_OFF:GATE_OFF], preferred_element_type=F32)
    zmg_ref[...] = jnp.dot(h, w_ref[:, GATE_OFF:GATE_OFF + 2 * D_MODEL], preferred_element_type=F32)
    zgate_ref[...] = jnp.dot(h, w_ref[:, GATE_OFF + 2 * D_MODEL:], preferred_element_type=F32)


def _proj(x2d, g_mix, w_in, mu, seq):
    T = x2d.shape[0]
    tm = TM_PROJ
    wp = jnp.concatenate(
        [w_in[:, :GATE_OFF], w_in[:, MERGE_OFF:],
         jnp.pad(w_in[:, GATE_OFF:MERGE_OFF], ((0, 0), (0, LANE - N_GATE)))], axis=1).astype(BF16)
    npad = wp.shape[1]
    row = lambda i: (i, 0)
    fixed = lambda i: (0, 0)
    return pl.pallas_call(
        functools.partial(_proj_kernel, tiles_per_seq=seq // tm),
        grid=(T // tm,),
        in_specs=[pl.BlockSpec((tm, D_MODEL), row), pl.BlockSpec((1, D_MODEL), fixed),
                  pl.BlockSpec((D_MODEL, npad), fixed), pl.BlockSpec((1, RW_IN), fixed)],
        out_specs=[pl.BlockSpec((tm, RW_IN), row), pl.BlockSpec((tm, NSA_WIDTH), row),
                   pl.BlockSpec((tm, KV_IN), row), pl.BlockSpec((tm, 2 * D_MODEL), row),
                   pl.BlockSpec((tm, LANE), row)],
        out_shape=[jax.ShapeDtypeStruct((T, RW_IN), F32), jax.ShapeDtypeStruct((T, NSA_WIDTH), F32),
                   jax.ShapeDtypeStruct((T, KV_IN), F32), jax.ShapeDtypeStruct((T, 2 * D_MODEL), F32),
                   jax.ShapeDtypeStruct((T, LANE), F32)],
        scratch_shapes=[pltpu.VMEM((8, RW_IN), F32)],
        compiler_params=pltpu.CompilerParams(dimension_semantics=("arbitrary",),
                                             vmem_limit_bytes=VMEM_LIMIT),
        name="proj",
    )(x2d, g_mix.reshape(1, D_MODEL), wp, mu.reshape(1, RW_IN))


PAIR = 2 * RW_HEAD_DIM


def _pair_blocks(x):
    low = lax.broadcasted_iota(jnp.int32, (1, PAIR), 1) < RW_HEAD_DIM
    return jnp.concatenate([jnp.where(low, x, 0.0), jnp.where(low, 0.0, x)], axis=0)


def _pmm(a, b):
    return _bdot(a, _pair_blocks(b))


def _unit_lower_inverse(a_strict, sub_mask, eye):
    ad = [jnp.where(sub_mask, a, 0.0) for a in a_strict]
    ao = [a - d for a, d in zip(a_strict, ad)]
    td = [eye - d for d in ad]
    pw = ad
    for _ in range(int(math.log2(RW_SUB)) - 1):
        pw = [_pmm(x, x) for x in pw]
        td = [_pmm(t, eye + x) for t, x in zip(td, pw)]
    n = [_pmm(t, o) for t, o in zip(td, ao)]
    t = [eye - x for x in n]
    pw = n
    for _ in range(int(math.log2(RW_CHUNK // RW_SUB)) - 1):
        pw = [_pmm(x, x) for x in pw]
        t = [_pmm(a, eye + x) for a, x in zip(t, pw)]
    return [_pmm(a, d) for a, d in zip(t, td)]


def _rwkv_kernel(z_ref, w0_ref, w2_ref, a0_ref, a2_ref, g2_ref, kk_ref, ka_ref, rk_ref, lnw_ref,
                 lnb_ref, avg_ref, o_ref, h_ref):
    c = pl.program_id(1)
    C = RW_CHUNK
    n_pair = RW_WIDTH // PAIR
    nt = (((1,), (1,)), ((), ()))

    @pl.when(c == 0)
    def _():
        h_ref[...] = jnp.zeros_like(h_ref)

    ti = lax.broadcasted_iota(jnp.int32, (C, 1), 0)
    si = lax.broadcasted_iota(jnp.int32, (1, PAIR), 1) & (RW_HEAD_DIM - 1)
    incl, strict = ti >= si, ti > si
    eye = (ti == si).astype(F32)
    sub_shift = int(math.log2(RW_SUB))
    sub_mask = (ti >> sub_shift) == (si >> sub_shift)
    row2 = lax.broadcasted_iota(jnp.int32, (PAIR, 1), 0)
    col2 = lax.broadcasted_iota(jnp.int32, (1, PAIR), 1)
    same_head = (row2 < RW_HEAD_DIM) == (col2 < RW_HEAD_DIM)
    eye2 = row2 == col2
    tri = (lax.broadcasted_iota(jnp.int32, (C, C), 0)
           >= lax.broadcasted_iota(jnp.int32, (C, C), 1)).astype(BF16)

    def head_mean(x):
        xs = jnp.concatenate([x[:, p * PAIR:(p + 1) * PAIR] for p in range(n_pair)], axis=0)
        ms = _bdot(xs, avg_ref[...])
        return jnp.concatenate([ms[p * C:(p + 1) * C] for p in range(n_pair)], axis=1)

    n_rows = z_ref.shape[0]
    rows = []
    for n in range(n_rows):
        z = z_ref[n]
        zr, zk, zv = z[:, 0:512], z[:, 512:1024], z[:, 1024:1536]
        zw, za, zg = z[:, 1536:1600], z[:, 1600:1664], z[:, 1664:1792]
        w_raw = w0_ref[...] + _bdot(jnp.tanh(zw), w2_ref[...])
        logw = -jax.nn.sigmoid(w_raw) * math.exp(-0.5)
        a = jax.nn.sigmoid(a0_ref[...] + _bdot(za, a2_ref[...]))
        gate = _bdot(jax.nn.sigmoid(zg), g2_ref[...])
        kk = zk * kk_ref[...]
        kk = kk / jnp.maximum(jnp.sqrt(head_mean(kk * kk) * RW_HEAD_DIM), 1e-12)
        k = zk * (1.0 + (a - 1.0) * ka_ref[...])
        b = kk * a

        w_hi = logw.astype(BF16)
        w_lo = (logw - w_hi.astype(F32)).astype(BF16)
        cum = (jnp.dot(tri, w_hi, preferred_element_type=F32)
               + jnp.dot(tri, w_lo, preferred_element_type=F32))
        cum_last = cum[C - 1:C, :]
        g_inv = jnp.exp(-cum)
        g_end = jnp.exp(cum_last - cum)
        rows.append(dict(rt=zr * jnp.exp(cum), kt=k * g_inv, bt=b * g_inv, qt=kk * jnp.exp(cum - logw),
                         kh=k * g_end, bh=b * g_end, v=zv, g_last=jnp.exp(cum_last), gate=gate,
                         bonus=head_mean(zr * k * rk_ref[...]) * RW_HEAD_DIM * zv))

    chains = [(n, slice(p * PAIR, (p + 1) * PAIR)) for n in range(n_rows) for p in range(n_pair)]
    part = lambda name: [rows[n][name][:, sl] for n, sl in chains]
    qt, rt, kt, bt, kh, bh, v = (part(x) for x in ("qt", "rt", "kt", "bt", "kh", "bh", "v"))
    lhs = [jnp.concatenate([q, r], axis=0).astype(BF16) for q, r in zip(qt, rt)]
    ab = [lax.dot_general(l, _pair_blocks(x).astype(BF16), nt, preferred_element_type=F32)
          for l, x in zip(lhs, bt)]
    ak = [lax.dot_general(l, _pair_blocks(x).astype(BF16), nt, preferred_element_type=F32)
          for l, x in zip(lhs, kt)]
    a_kb = [jnp.where(strict, x[0:C], 0.0) for x in ab]
    a_rb = [jnp.where(incl, x[C:2 * C], 0.0) for x in ab]
    a_kk = [jnp.where(strict, x[0:C], 0.0) for x in ak]
    a_rk = [jnp.where(incl, x[C:2 * C], 0.0) for x in ak]
    t_inv = _unit_lower_inverse(a_kb, sub_mask, eye)

    h = [h_ref[n, sl.start // PAIR] for n, sl in chains]
    vb = [_pair_blocks(x) for x in v]
    rhs = [_bdot(jnp.concatenate([q, akk], axis=1), jnp.concatenate([hh, vv], axis=0))
           for q, akk, hh, vv in zip(qt, a_kk, h, vb)]
    u = [_pmm(t, x) for t, x in zip(t_inv, rhs)]
    outs = [_bdot(jnp.concatenate([r, ark, -arb], axis=1), jnp.concatenate([hh, vv, _pair_blocks(uu)], axis=0))
            for r, ark, arb, hh, vv, uu in zip(rt, a_rk, a_rb, h, vb, u)]
    upd = [_bdot(jnp.concatenate([x, -y], axis=0).T, jnp.concatenate([vv, uu], axis=0))
           for x, y, vv, uu in zip(kh, bh, v, u)]
    for (n, sl), hh, dd in zip(chains, h, upd):
        decay_col = jnp.sum(jnp.where(eye2, rows[n]["g_last"][:, sl], 0.0), axis=1, keepdims=True)
        h_ref[n, sl.start // PAIR] = decay_col * hh + jnp.where(same_head, dd, 0.0)

    for n in range(n_rows):
        o = jnp.concatenate(outs[n * n_pair:(n + 1) * n_pair], axis=1)
        d = o - head_mean(o)
        on = d * lax.rsqrt(head_mean(d * d) + GN_EPS)
        o_ref[n] = (on * lnw_ref[...] + lnb_ref[...] + rows[n]["bonus"]) * rows[n]["gate"]


def _rwkv(zrw, w0, w2, a0, a2, g2, k_k, k_a, r_k, ln_w, ln_b):
    B, S, _ = zrw.shape
    C = RW_CHUNK
    nb = RW_ROWS
    hid = np.arange(PAIR) // RW_HEAD_DIM
    avg = jnp.asarray((hid[:, None] == hid[None, :]).astype(np.float32) / RW_HEAD_DIM)
    vec = lambda a: a.reshape(1, RW_WIDTH)
    fixed = lambda shape: pl.BlockSpec(shape, lambda b, c: (0,) * len(shape))
    return pl.pallas_call(
        _rwkv_kernel,
        grid=(B // nb, S // C),
        in_specs=[pl.BlockSpec((nb, C, RW_IN), lambda b, c: (b, c, 0)),
                  fixed((1, RW_WIDTH)), fixed((DECAY_LORA, RW_WIDTH)),
                  fixed((1, RW_WIDTH)), fixed((AAA_LORA, RW_WIDTH)),
                  fixed((GATE_LORA, RW_WIDTH)), fixed((1, RW_WIDTH)), fixed((1, RW_WIDTH)),
                  fixed((1, RW_WIDTH)), fixed((1, RW_WIDTH)), fixed((1, RW_WIDTH)),
                  fixed((PAIR, PAIR))],
        out_specs=pl.BlockSpec((nb, C, RW_WIDTH), lambda b, c: (b, c, 0)),
        out_shape=jax.ShapeDtypeStruct((B, S, RW_WIDTH), F32),
        scratch_shapes=[pltpu.VMEM((nb, RW_WIDTH // PAIR, PAIR, PAIR), F32)],
        compiler_params=pltpu.CompilerParams(dimension_semantics=("parallel", "arbitrary"),
                                             vmem_limit_bytes=VMEM_LIMIT),
        name="rwkv",
    )(zrw, vec(w0), w2, vec(a0), a2, g2, vec(k_k), vec(k_a), vec(r_k), vec(ln_w), vec(ln_b), avg)


def _gelu_tanh(x):
    return 0.5 * x * (1.0 + jnp.tanh(math.sqrt(2.0 / math.pi) * (x + 0.044715 * (x * x * x))))


def _key_features(pos_hi, pos_lo, block, n, n_sel):
    lane = lax.broadcasted_iota(jnp.int32, (n, NSA_HEAD_DIM), 1)
    feat = jnp.where(lane == n_sel, pos_hi, jnp.where(lane == n_sel + 1, pos_lo, 0.0))
    return feat if block is None else jnp.where(lane == block, 1.0, feat)


def _nsa_prep_kernel(zkc_ref, zvc_ref, zks_ref, zvs_ref, zkw_ref, zvw_ref, pk_ref, pv_ref, kw1_ref,
                     kw2_ref, vw1_ref, vw2_ref, kc_ref, vct_ref, ksa_ref, kwa_ref, vst_ref, vwt_ref):
    S = zkc_ref.shape[1]
    n_grp = S // CMP_STRIDE
    Dh = NSA_HEAD_DIM
    half = CMP_BLOCK // 2
    n_sel = S // SEL_BLOCK
    jrow = lax.broadcasted_iota(jnp.int32, (n_grp, 1), 0)
    cmp_feat = _key_features((jrow >> 3).astype(F32),
                             ((jrow & 7) * CMP_STRIDE).astype(F32) + 0.5 * (CMP_BLOCK - 1),
                             None, n_grp, n_sel)
    for is_v, (z_ref, pos_ref, w1_ref, w2_ref) in enumerate(((zkc_ref, pk_ref, kw1_ref, kw2_ref),
                                                            (zvc_ref, pv_ref, vw1_ref, vw2_ref))):
        for g in range(NSA_KV_HEADS):
            lo = jnp.zeros((n_grp, Dh), F32)
            hi = jnp.zeros((n_grp, Dh), F32)
            for l in range(half):
                xs = z_ref[0, pl.ds(l, n_grp, stride=CMP_STRIDE), :]
                xg = xs[:, g * Dh:(g + 1) * Dh]
                lo = lo + _bdot(xg + pos_ref[l:l + 1, :], w1_ref[l * Dh:(l + 1) * Dh, :])
                hi = hi + _bdot(xg + pos_ref[half + l:half + l + 1, :],
                                w1_ref[(half + l) * Dh:(half + l + 1) * Dh, :])
            pre = lo + pltpu.roll(hi, n_grp - 1, 0)
            out = jnp.where(jrow < n_grp - 1, _bdot(_gelu_tanh(pre), w2_ref[...]), 0.0)
            if is_v:
                out_t = jnp.concatenate([out, jnp.zeros_like(out)], axis=1).T
                vct_ref[0, g] = out_t[0:Dh, :].astype(BF16)
            else:
                kc_ref[0, g] = jnp.concatenate([out, cmp_feat], axis=1)

    prow = lax.broadcasted_iota(jnp.int32, (S, 1), 0)
    p_hi, p_lo = (prow >> 7).astype(F32), (prow & (LANE - 1)).astype(F32)
    for z_ref, out_ref, block in ((zks_ref, ksa_ref, prow >> int(math.log2(SEL_BLOCK))),
                                  (zkw_ref, kwa_ref, None)):
        kfull = z_ref[0]
        key_feat = _key_features(p_hi, p_lo, block, S, n_sel)
        for g in range(NSA_KV_HEADS):
            out_ref[0, g] = jnp.concatenate([kfull[:, g * Dh:(g + 1) * Dh], key_feat], axis=1).astype(BF16)
    for z_ref, out_ref in ((zvs_ref, vst_ref), (zvw_ref, vwt_ref)):
        for j in range(S // TK):
            out_ref[0, j] = z_ref[0, j * TK:(j + 1) * TK, :].T.astype(BF16)


def _nsa_prep(zkv, pos_k, pos_v, kw1, kw2, vw1, vw2):
    B, S, _ = zkv.shape
    n_grp = S // CMP_STRIDE
    n_kt = S // TK
    Dh = NSA_HEAD_DIM
    G = NSA_KV_HEADS
    fixed = lambda shape: pl.BlockSpec(shape, lambda b: (0,) * len(shape))
    col = lambda c: pl.BlockSpec((1, S, NSA_KV_WIDTH), lambda b: (b, 0, c))
    whole = lambda shape: pl.BlockSpec((1,) + shape, lambda b: (b,) + (0,) * len(shape))
    shapes = [((G, n_grp, 2 * Dh), F32), ((G, Dh, n_grp), BF16), ((G, S, 2 * Dh), BF16),
              ((G, S, 2 * Dh), BF16), ((n_kt, G * Dh, TK), BF16), ((n_kt, G * Dh, TK), BF16)]
    return pl.pallas_call(
        _nsa_prep_kernel,
        grid=(B,),
        in_specs=[col(c) for c in range(6)] + [
            fixed((CMP_BLOCK, Dh)), fixed((CMP_BLOCK, Dh)),
            fixed((CMP_BLOCK * Dh, Dh)), fixed((Dh, Dh)),
            fixed((CMP_BLOCK * Dh, Dh)), fixed((Dh, Dh))],
        out_specs=[whole(s) for s, _ in shapes],
        out_shape=[jax.ShapeDtypeStruct((B,) + s, d) for s, d in shapes],
        compiler_params=pltpu.CompilerParams(dimension_semantics=("parallel",),
                                             vmem_limit_bytes=VMEM_LIMIT),
        name="nsa_prep",
    )(zkv, zkv, zkv, zkv, zkv, zkv, pos_k, pos_v, kw1, kw2, vw1, vw2)


def _nsa_attn_kernel(q_ref, kc_ref, vct_ref, ksa_ref, kwa_ref, vst_ref, vwt_ref, gl_ref, gb_ref,
                     ovlt_ref, slope_ref, o_ref, acc_ref, ot_ref):
    i = pl.program_id(1)
    Dh = NSA_HEAD_DIM
    R = NSA_REP
    N = R * TQ
    n_cmp_pad = kc_ref.shape[2]
    n_sel = ovlt_ref.shape[0]
    G = NSA_KV_HEADS
    nt = (((1,), (1,)), ((), ()))
    log2e = math.log2(math.e)
    t0 = i * TQ
    t_row = t0 + lax.broadcasted_iota(jnp.int32, (1, TQ), 1)
    c_col = lax.broadcasted_iota(jnp.int32, (TK, 1), 0)
    sgate_t = jax.nn.sigmoid(gl_ref[0] + gb_ref[...]).T
    lane_f = lax.broadcasted_iota(jnp.int32, (1, Dh), 1)
    heads = lambda x: jnp.concatenate([x] * R, axis=1)

    def key_dist(j):
        return t_row - (j * TK + c_col)

    def queries(g, sel_feat):
        parts = []
        for r in range(R):
            h = g * R + r
            sl = slope_ref[:, h:h + 1] * log2e
            feat = jnp.where(lane_f == n_sel, sl * LANE, jnp.where(lane_f == n_sel + 1, sl, sel_feat))
            parts.append(jnp.concatenate(
                [q_ref[0, :, h * Dh:(h + 1) * Dh] * (Dh ** -0.5 * log2e), jnp.broadcast_to(feat, (TQ, Dh))],
                axis=1))
        return jnp.concatenate(parts, axis=0)

    qab, o_cmp = [], []
    for g in range(G):
        jc = lax.broadcasted_iota(jnp.int32, (n_cmp_pad, 1), 0)
        ok_c = (jc * CMP_STRIDE + (CMP_BLOCK - 1) <= t_row) & (jc < n_cmp_pad - 1)
        s_c = (lax.dot_general(kc_ref[0, g], queries(g, 0.0), nt, precision=HI, preferred_element_type=F32)
               + heads(jnp.where(ok_c, 0.0, NEG_INF)))
        e_c = jnp.exp2(s_c - jnp.max(s_c, axis=0, keepdims=True))
        any_c = heads(t_row >= CMP_BLOCK - 1)
        p_c = e_c * jnp.where(any_c, 1.0 / jnp.sum(e_c, axis=0, keepdims=True), 0.0)
        o_cmp.append(jnp.dot(vct_ref[0, g], p_c.astype(BF16), preferred_element_type=F32))

        p_sum = p_c[:, 0:TQ]
        for r in range(1, R):
            p_sum = p_sum + p_c[:, r * TQ:(r + 1) * TQ]
        imp = jnp.dot(ovlt_ref[...], p_sum, precision=HI, preferred_element_type=F32)
        kb = lax.broadcasted_iota(jnp.int32, (n_sel, 1), 0)
        kbf = kb.astype(F32)
        blk_t = t_row >> int(math.log2(SEL_BLOCK))
        forced = (kb == 0) | (kb == blk_t) | (kb == blk_t - 1)
        cur = jnp.where(forced, FORCE_SCORE, jnp.where(kb <= blk_t, imp, -FORCE_SCORE))
        sel_bias = jnp.full((n_sel, TQ), NEG_INF, F32)
        for _ in range(min(N_SELECT, n_sel)):
            mx = jnp.max(cur, axis=0, keepdims=True)
            first = jnp.min(jnp.where(cur == mx, kbf, float(n_sel)), axis=0, keepdims=True)
            hit = kbf == first
            sel_bias = jnp.where(hit, 0.0, sel_bias)
            cur = jnp.where(hit, -3e38, cur)
        sel_feat = jnp.concatenate([sel_bias, jnp.zeros((LANE - n_sel, TQ), F32)], axis=0).T[:, 0:Dh]
        qab.append(queries(g, sel_feat).astype(BF16))

    def tile(j, carry, k_ref, vt_ref, bias, slot):
        out = []
        for g in range(G):
            m, l = carry[g]
            k = k_ref[0, g, pl.ds(pl.multiple_of(j * TK, TK), TK), :]
            s = lax.dot_general(k, qab[g], nt, preferred_element_type=F32)
            if bias is not None:
                s = s + heads(bias)
            m_new = jnp.maximum(m, jnp.max(s, axis=0, keepdims=True))
            alpha = jnp.exp2(m - m_new)
            p = jnp.exp2(s - m_new)
            l = alpha * l + jnp.sum(p, axis=0, keepdims=True)
            vt = vt_ref[0, j, g * Dh:(g + 1) * Dh, :]
            acc_ref[slot, g] = (alpha * acc_ref[slot, g]
                                + jnp.dot(vt, p.astype(BF16), preferred_element_type=F32))
            out.append((m_new, l))
        return tuple(out)

    def window_bias(j):
        d = key_dist(j)
        return jnp.where((d >= 0) & (d < WINDOW), 0.0, NEG_INF)

    init = ((jnp.full((1, N), NEG_INF, F32), jnp.zeros((1, N), F32)),) * G
    acc_ref[...] = jnp.zeros_like(acc_ref)
    causal = jnp.where(key_dist(i) >= 0, 0.0, NEG_INF)
    carry = lax.fori_loop(0, i, lambda j, c: tile(j, c, ksa_ref, vst_ref, None, 0), init)
    stat_s = tile(i, carry, ksa_ref, vst_ref, causal, 0)
    carry = lax.fori_loop(jnp.maximum(i - WINDOW // TK, 0), i,
                          lambda j, c: tile(j, c, kwa_ref, vwt_ref, window_bias(j), 1), init)
    stat_w = tile(i, carry, kwa_ref, vwt_ref, causal, 1)

    for g in range(G):
        o_sel = acc_ref[0, g] * (1.0 / stat_s[g][1])
        o_win = acc_ref[1, g] * (1.0 / stat_w[g][1])
        for r in range(R):
            h = g * R + r
            cols = slice(r * TQ, (r + 1) * TQ)
            ot_ref[h * Dh:(h + 1) * Dh, :] = (
                sgate_t[3 * h:3 * h + 1, :] * o_cmp[g][:, cols]
                + sgate_t[3 * h + 1:3 * h + 2, :] * o_sel[:, cols]
                + sgate_t[3 * h + 2:3 * h + 3, :] * o_win[:, cols])
    o_ref[0] = ot_ref[...].T


def _nsa_attn(zq, kc, vct, ksa, kwa, vst, vwt, zgate, gate_b):
    B, S, _ = zq.shape
    n_sel = S // SEL_BLOCK
    n_cmp = (S - CMP_BLOCK) // CMP_STRIDE + 1
    n_cmp_pad = kc.shape[2]
    n_kt = S // TK
    G, Dh = NSA_KV_HEADS, NSA_HEAD_DIM
    cmp_start = np.arange(n_cmp) * CMP_STRIDE
    sel_start = np.arange(n_sel) * SEL_BLOCK
    overlap = np.clip(np.minimum(cmp_start[:, None] + CMP_BLOCK, sel_start[None, :] + SEL_BLOCK)
                      - np.maximum(cmp_start[:, None], sel_start[None, :]), 0, None) / CMP_BLOCK
    ovlt = np.zeros((n_sel, n_cmp_pad), np.float32)
    ovlt[:, :n_cmp] = overlap.T
    slopes = (2.0 ** (-8.0 * np.arange(1, NSA_HEADS + 1) / NSA_HEADS)).astype(np.float32).reshape(1, NSA_HEADS)
    gb = jnp.pad(gate_b, (0, LANE - N_GATE)).reshape(1, LANE)
    fixed = lambda shape: pl.BlockSpec(shape, lambda b, i: (0,) * len(shape))
    per_b = lambda shape: pl.BlockSpec((1,) + shape, lambda b, i: (b,) + (0,) * len(shape))
    return pl.pallas_call(
        _nsa_attn_kernel,
        grid=(B, S // TQ),
        in_specs=[pl.BlockSpec((1, TQ, NSA_WIDTH), lambda b, i: (b, i, 0)),
                  per_b((G, n_cmp_pad, 2 * Dh)), per_b((G, Dh, n_cmp_pad)),
                  per_b((G, S, 2 * Dh)), per_b((G, S, 2 * Dh)),
                  per_b((n_kt, G * Dh, TK)), per_b((n_kt, G * Dh, TK)),
                  pl.BlockSpec((1, TQ, LANE), lambda b, i: (b, i, 0)),
                  fixed((1, LANE)), fixed((n_sel, n_cmp_pad)), fixed((1, NSA_HEADS))],
        out_specs=pl.BlockSpec((1, TQ, NSA_WIDTH), lambda b, i: (b, i, 0)),
        out_shape=jax.ShapeDtypeStruct((B, S, NSA_WIDTH), F32),
        scratch_shapes=[pltpu.VMEM((2, G, Dh, NSA_REP * TQ), F32),
                        pltpu.VMEM((NSA_WIDTH, TQ), F32)],
        compiler_params=pltpu.CompilerParams(dimension_semantics=("parallel", "arbitrary"),
                                             vmem_limit_bytes=VMEM_LIMIT),
        name="nsa_attn",
    )(zq, kc, vct, ksa, kwa, vst, vwt, zgate, gb, jnp.asarray(ovlt), jnp.asarray(slopes))


def _mix_kernel(x_ref, yr_ref, yn_ref, zmg_ref, ur_ref, un_ref, wo_ref, gf_ref, wr_ref, br_ref,
                x1_ref, h2_ref, rt_ref):
    tm = x_ref.shape[0]
    zmg = zmg_ref[...]
    mixed = (jax.nn.sigmoid(zmg[:, :D_MODEL]) * _bdot(yr_ref[...], ur_ref[...])
             + jax.nn.sigmoid(zmg[:, D_MODEL:]) * _bdot(yn_ref[...], un_ref[...]))
    x1 = x_ref[...] + _bdot(mixed, wo_ref[...])
    x1_ref[...] = x1
    h2 = _rms(x1, gf_ref[...])
    h2_ref[...] = h2
    logits = jnp.dot(h2, wr_ref[...], precision=HI, preferred_element_type=F32) + br_ref[...]
    lane = lax.broadcasted_iota(jnp.int32, (1, LANE), 1).astype(F32)
    gl = jnp.where(lane < N_GROUPS, logits, NEG_INF)
    gmax = jnp.max(gl, axis=-1, keepdims=True)
    g_sel = jnp.min(jnp.where(gl == gmax, lane, float(LANE)), axis=-1, keepdims=True)
    p_group = 1.0 / jnp.sum(jnp.exp(gl - gmax), axis=-1, keepdims=True)
    e_lane = lane - N_GROUPS
    in_grp = (e_lane >= g_sel * EXPERTS_PER_GROUP) & (e_lane < (g_sel + 1.0) * EXPERTS_PER_GROUP)
    el = jnp.where(in_grp, logits, NEG_INF)
    m1 = jnp.max(el, axis=-1, keepdims=True)
    i1 = jnp.min(jnp.where(el == m1, e_lane, float(LANE)), axis=-1, keepdims=True)
    el2 = jnp.where(e_lane == i1, 2.0 * NEG_INF, el)
    m2 = jnp.max(el2, axis=-1, keepdims=True)
    i2 = jnp.min(jnp.where(el2 == m2, e_lane, float(LANE)), axis=-1, keepdims=True)
    r2 = jnp.exp(m2 - m1)
    g1 = p_group / (1.0 + r2)
    g2 = p_group * r2 / (1.0 + r2)
    rt_ref[...] = jnp.where(lane == 0, i1, jnp.where(lane == 1, i2,
                                                     jnp.where(lane == 2, g1, jnp.where(lane == 3, g2, 0.0))))


def _mix(x2d, y_rw, y_nsa, zmg, w_up_r, w_up_n, w_out, g_ffn, w_group, b_group, w_router, b_router):
    T = x2d.shape[0]
    tm = TM_PROJ
    n_r = N_GROUPS + N_EXPERTS
    wr = jnp.pad(jnp.concatenate([w_group, w_router], axis=1), ((0, 0), (0, LANE - n_r)))
    br = jnp.pad(jnp.concatenate([b_group, b_router]), (0, LANE - n_r)).reshape(1, LANE)
    row = lambda i: (i, 0)
    fixed = lambda i: (0, 0)
    return pl.pallas_call(
        _mix_kernel,
        grid=(T // tm,),
        in_specs=[pl.BlockSpec((tm, D_MODEL), row), pl.BlockSpec((tm, RW_WIDTH), row),
                  pl.BlockSpec((tm, NSA_WIDTH), row), pl.BlockSpec((tm, 2 * D_MODEL), row),
                  pl.BlockSpec((RW_WIDTH, D_MODEL), fixed), pl.BlockSpec((NSA_WIDTH, D_MODEL), fixed),
                  pl.BlockSpec((D_MODEL, D_MODEL), fixed), pl.BlockSpec((1, D_MODEL), fixed),
                  pl.BlockSpec((D_MODEL, LANE), fixed), pl.BlockSpec((1, LANE), fixed)],
        out_specs=[pl.BlockSpec((tm, D_MODEL), row), pl.BlockSpec((tm, D_MODEL), row),
                   pl.BlockSpec((tm, LANE), row)],
        out_shape=[jax.ShapeDtypeStruct((T, D_MODEL), F32), jax.ShapeDtypeStruct((T, D_MODEL), F32),
                   jax.ShapeDtypeStruct((T, LANE), F32)],
        compiler_params=pltpu.CompilerParams(dimension_semantics=("parallel",),
                                             vmem_limit_bytes=VMEM_LIMIT),
        name="mix",
    )(x2d, y_rw, y_nsa, zmg, w_up_r.astype(BF16), w_up_n.astype(BF16), w_out.astype(BF16),
      g_ffn.reshape(1, D_MODEL), wr, br)


def _moe_kernel(be_ref, nv_ref, rtok_ref, rdst_ref, h2_hbm, rg_ref, wgu_ref, wd_ref, eo_hbm,
                xbuf, obuf, wgu_b, wd_b, gsem, ssem):
    i = pl.program_id(0)
    nv = nv_ref[i]
    base = i * ROW_BLOCK

    @pl.when(i == 0)
    def _():
        xbuf[...] = jnp.zeros_like(xbuf)

    changed = jnp.logical_or(i == 0, be_ref[i] != be_ref[jnp.maximum(i - 1, 0)])

    @pl.when(jnp.logical_and(nv > 0, changed))
    def _():
        wgu_b[...] = wgu_ref[0].astype(BF16)
        wd_b[...] = wd_ref[0].astype(BF16)

    def row_in(r):
        return pltpu.make_async_copy(h2_hbm.at[pl.ds(rtok_ref[base + r], 1)], xbuf.at[pl.ds(r, 1)], gsem)

    def row_out(r):
        return pltpu.make_async_copy(obuf.at[pl.ds(r, 1)], eo_hbm.at[pl.ds(rdst_ref[base + r], 1)], ssem)

    def start(copy):
        def body(r, c):
            copy(r).start()
            return c
        return body

    def wait(copy):
        def body(r, c):
            copy(r).wait()
            return c
        return body

    @pl.when(nv > 0)
    def _():
        lax.fori_loop(0, nv, start(row_in), 0)
        lax.fori_loop(0, nv, wait(row_in), 0)
        gu = jnp.dot(xbuf[...].astype(BF16), wgu_b[...], preferred_element_type=F32)
        gate_h, up_h = gu[:, :D_EXPERT], gu[:, D_EXPERT:]
        mid = gate_h * jax.nn.sigmoid(gate_h) * up_h
        obuf[...] = jnp.dot(mid.astype(BF16), wd_b[...], preferred_element_type=F32) * rg_ref[...]
        lax.fori_loop(0, nv, start(row_out), 0)
        lax.fori_loop(0, nv, wait(row_out), 0)


def _moe(h2, blk_expert, blk_nv, row_tok, row_dst, row_gate, w_gate_up, w_down):
    T = h2.shape[0]
    n_blk = blk_expert.shape[0]
    grid_spec = pltpu.PrefetchScalarGridSpec(
        num_scalar_prefetch=4,
        grid=(n_blk,),
        in_specs=[pl.BlockSpec(memory_space=pl.ANY),
                  pl.BlockSpec((ROW_BLOCK, 1), lambda i, be, nv, rt, rd: (i, 0)),
                  pl.BlockSpec((1, D_MODEL, 2 * D_EXPERT), lambda i, be, nv, rt, rd: (be[i], 0, 0)),
                  pl.BlockSpec((1, D_EXPERT, D_MODEL), lambda i, be, nv, rt, rd: (be[i], 0, 0))],
        out_specs=pl.BlockSpec(memory_space=pl.ANY),
        scratch_shapes=[pltpu.VMEM((ROW_BLOCK, D_MODEL), F32), pltpu.VMEM((ROW_BLOCK, D_MODEL), F32),
                        pltpu.VMEM((D_MODEL, 2 * D_EXPERT), BF16), pltpu.VMEM((D_EXPERT, D_MODEL), BF16),
                        pltpu.SemaphoreType.DMA(()), pltpu.SemaphoreType.DMA(())])
    return pl.pallas_call(
        _moe_kernel,
        grid_spec=grid_spec,
        out_shape=jax.ShapeDtypeStruct((TOP_K * T, D_MODEL), F32),
        compiler_params=pltpu.CompilerParams(dimension_semantics=("arbitrary",),
                                             vmem_limit_bytes=VMEM_LIMIT),
        name="moe",
    )(blk_expert, blk_nv, row_tok, row_dst, h2, row_gate, w_gate_up, w_down)


def _route_tables(rt, T):
    A = T * TOP_K
    n_rows = A + N_EXPERTS * ROW_BLOCK
    n_blk = n_rows // ROW_BLOCK
    e_flat = rt[:, 0:TOP_K].astype(jnp.int32).reshape(A)
    gate_flat = rt[:, TOP_K:2 * TOP_K].reshape(A)
    order = jnp.argsort(e_flat)
    e_sorted = e_flat[order]
    counts = jnp.bincount(e_flat, length=N_EXPERTS)
    starts = jnp.cumsum(counts) - counts
    padded = (counts + ROW_BLOCK - 1) // ROW_BLOCK * ROW_BLOCK
    pends = jnp.cumsum(padded)
    pstarts = pends - padded
    dest = pstarts[e_sorted] + jnp.arange(A) - starts[e_sorted]
    tok = (order // TOP_K).astype(jnp.int32)
    slot = (order % TOP_K).astype(jnp.int32)
    row_tok = jnp.zeros((n_rows,), jnp.int32).at[dest].set(tok)
    row_dst = jnp.zeros((n_rows,), jnp.int32).at[dest].set(slot * T + tok)
    row_gate = jnp.zeros((n_rows,), F32).at[dest].set(gate_flat[order]).reshape(n_rows, 1)
    blk_start = jnp.arange(n_blk) * ROW_BLOCK
    blk_expert = jnp.minimum(jnp.searchsorted(pends, blk_start, side="right"), N_EXPERTS - 1)
    blk_nv = jnp.clip(counts[blk_expert] - (blk_start - pstarts[blk_expert]), 0, ROW_BLOCK)
    return (blk_expert.astype(jnp.int32), blk_nv.astype(jnp.int32), row_tok, row_dst, row_gate)


def _ple_kernel(x1_ref, e0_ref, e1_ref, p_ref, wpp_ref, gpn_ref, ggi_ref, wpg_ref, gfin_ref, y_ref,
                *, last_layer):
    x2 = x1_ref[...] + (e0_ref[...] + e1_ref[...])
    e = _rms(_bdot(p_ref[...], wpp_ref[...]), gpn_ref[...])
    gate = jax.nn.sigmoid(_bdot(_rms(x2, ggi_ref[...]), wpg_ref[...]))
    x3 = x2 + gate * e
    y_ref[...] = _rms(x3, gfin_ref[...]) if last_layer else x3


def _ple(x1, eo, p2d, w_pp, g_pn, g_gi, w_pg, g_final, last_layer):
    T = x1.shape[0]
    tm = TM_PROJ
    nt = T // tm
    row = lambda i: (i, 0)
    fixed = lambda i: (0, 0)
    vec = lambda a: a.reshape(1, D_MODEL)
    return pl.pallas_call(
        functools.partial(_ple_kernel, last_layer=last_layer),
        grid=(nt,),
        in_specs=[pl.BlockSpec((tm, D_MODEL), row), pl.BlockSpec((tm, D_MODEL), row),
                  pl.BlockSpec((tm, D_MODEL), lambda i: (i + nt, 0)), pl.BlockSpec((tm, PLE_DIM), row),
                  pl.BlockSpec((PLE_DIM, D_MODEL), fixed), pl.BlockSpec((1, D_MODEL), fixed),
                  pl.BlockSpec((1, D_MODEL), fixed), pl.BlockSpec((D_MODEL, D_MODEL), fixed),
                  pl.BlockSpec((1, D_MODEL), fixed)],
        out_specs=pl.BlockSpec((tm, D_MODEL), row),
        out_shape=jax.ShapeDtypeStruct((T, D_MODEL), F32),
        compiler_params=pltpu.CompilerParams(dimension_semantics=("parallel",),
                                             vmem_limit_bytes=VMEM_LIMIT),
        name="ple",
    )(x1, eo, eo, p2d, w_pp.astype(BF16), vec(g_pn), vec(g_gi), w_pg.astype(BF16), vec(g_final))


def kernel(x, p, g_mix, w_in, mu_shift, rw_w0, rw_w2, rw_a0, rw_a2, rw_g2, rw_k_k, rw_k_a, rw_r_k, rw_ln_w, rw_ln_b, cmp_pos_k, cmp_pos_v, cmp_k_w1, cmp_k_w2, cmp_v_w1, cmp_v_w2, nsa_gate_b, w_up_rwkv, w_up_nsa, w_out, g_ffn, w_group, b_group, w_router, b_router, w_exp_gate_up, w_exp_down, w_ple_proj, g_ple_norm, g_ple_gate_in, w_ple_gate, g_final):
    B, S, D = x.shape
    T = B * S
    depth = p.shape[0]
    xc = x.reshape(T, D)
    for i in range(depth):
        zrw, zq, zkv, zmg, zgate = _proj(xc, g_mix[i], w_in[i], mu_shift[i], S)
        y_rw = _rwkv(zrw.reshape(B, S, RW_IN), rw_w0[i], rw_w2[i], rw_a0[i], rw_a2[i], rw_g2[i],
                     rw_k_k[i], rw_k_a[i], rw_r_k[i], rw_ln_w[i], rw_ln_b[i])
        zkv3 = zkv.reshape(B, S, KV_IN)
        nsa_kv = _nsa_prep(zkv3, cmp_pos_k[i], cmp_pos_v[i], cmp_k_w1[i], cmp_k_w2[i],
                           cmp_v_w1[i], cmp_v_w2[i])
        y_nsa = _nsa_attn(zq.reshape(B, S, NSA_WIDTH), *nsa_kv, zgate.reshape(B, S, LANE),
                          nsa_gate_b[i])
        x1, h2, rt = _mix(xc, y_rw.reshape(T, RW_WIDTH), y_nsa.reshape(T, NSA_WIDTH), zmg,
                          w_up_rwkv[i], w_up_nsa[i], w_out[i], g_ffn[i], w_group[i], b_group[i],
                          w_router[i], b_router[i])
        tables = _route_tables(rt, T)
        eo = _moe(h2, *tables, w_exp_gate_up[i], w_exp_down[i])
        xc = _ple(x1, eo, p[i].reshape(T, PLE_DIM), w_ple_proj[i], g_ple_norm[i], g_ple_gate_in[i],
                  w_ple_gate[i], g_final, i == depth - 1)
    return xc.reshape(B, S, D)
```

```python
import functools
import math

import jax
import jax.numpy as jnp
import numpy as np
from jax import lax
from jax.experimental import pallas as pl
from jax.experimental.pallas import tpu as pltpu

F32 = jnp.float32
BF16 = jnp.bfloat16
HI = lax.Precision.HIGHEST

D_MODEL = 1024
RW_HEADS = 8
RW_HEAD_DIM = 64
RW_WIDTH = 512
DECAY_LORA = 64
AAA_LORA = 64
GATE_LORA = 128
GN_EPS = 64e-5
RW_IN = 3 * RW_WIDTH + DECAY_LORA + AAA_LORA + GATE_LORA

NSA_HEADS = 8
NSA_KV_HEADS = 2
NSA_REP = NSA_HEADS // NSA_KV_HEADS
NSA_HEAD_DIM = 64
NSA_WIDTH = 512
NSA_KV_WIDTH = 128
CMP_BLOCK = 32
CMP_STRIDE = 16
SEL_BLOCK = 64
N_SELECT = 8
WINDOW = 512
N_NSA_BRANCH = 3
FORCE_SCORE = 1e6
NEG_INF = -1e30

N_GROUPS = 4
EXPERTS_PER_GROUP = 8
N_EXPERTS = 32
TOP_K = 2
D_EXPERT = 512
ROW_BLOCK = 128
PLE_DIM = 256
NORM_EPS = 1e-6

N_GATE = N_NSA_BRANCH * NSA_HEADS
ATT_IN = NSA_WIDTH + 6 * NSA_KV_WIDTH
KV_OFF = RW_IN + NSA_WIDTH
KV_IN = 6 * NSA_KV_WIDTH
GATE_OFF = RW_IN + ATT_IN
MERGE_OFF = GATE_OFF + N_GATE
LANE = 128

RW_CHUNK = 64
RW_SUB = 16
RW_ROWS = 4
TQ = 256
TK = 256
TM_PROJ = 256
VMEM_LIMIT = 56 * 1024 * 1024


def _bdot(a, b):
    return jnp.dot(a.astype(BF16), b.astype(BF16), preferred_element_type=F32)


def _bdot_nt(a, b):
    return lax.dot_general(a.astype(BF16), b.astype(BF16), (((1,), (1,)), ((), ())),
                           preferred_element_type=F32)


def _rms(x, g):
    return x * lax.rsqrt(jnp.mean(x * x, axis=-1, keepdims=True) + NORM_EPS) * g


def _proj_kernel(x_ref, g_ref, w_ref, mu_ref, zrw_ref, zq_ref, zkv_ref, zmg_ref, zgate_ref,
                 carry_ref, *, tiles_per_seq):
    i = pl.program_id(0)
    tm = x_ref.shape[0]

    @pl.when(i % tiles_per_seq == 0)
    def _():
        carry_ref[...] = jnp.zeros_like(carry_ref)

    h = _rms(x_ref[...], g_ref[...]).astype(BF16)
    z = jnp.dot(h, w_ref[:, 0:RW_IN], preferred_element_type=F32)
    row = lax.broadcasted_iota(jnp.int32, (tm, 1), 0)
    prev = jnp.where(row == 0, carry_ref[7:8, :], pltpu.roll(z, 1, 0))
    carry_ref[...] = z[tm - 8:tm, :]
    zrw_ref[...] = z + (prev - z) * mu_ref[...]
    zq_ref[...] = jnp.dot(h, w_ref[:, RW_IN:KV_OFF], preferred_element_type=F32)
    zkv_ref[...] = jnp.dot(h, w_ref[:, KV_OFF:GATE_OFF], preferred_element_type=F32)
    zmg_ref[...] = jnp.dot(h, w_ref[:, GATE_OFF:GATE_OFF + 2 * D_MODEL], preferred_element_type=F32)
    zgate_ref[...] = jnp.dot(h, w_ref[:, GATE_OFF + 2 * D_MODEL:], preferred_element_type=F32)


def _proj(x2d, g_mix, w_in, mu, seq):
    T = x2d.shape[0]
    tm = TM_PROJ
    wp = jnp.concatenate(
        [w_in[:, :GATE_OFF], w_in[:, MERGE_OFF:],
         jnp.pad(w_in[:, GATE_OFF:MERGE_OFF], ((0, 0), (0, LANE - N_GATE)))], axis=1).astype(BF16)
    npad = wp.shape[1]
    row = lambda i: (i, 0)
    fixed = lambda i: (0, 0)
    return pl.pallas_call(
        functools.partial(_proj_kernel, tiles_per_seq=seq // tm),
        grid=(T // tm,),
        in_specs=[pl.BlockSpec((tm, D_MODEL), row), pl.BlockSpec((1, D_MODEL), fixed),
                  pl.BlockSpec((D_MODEL, npad), fixed), pl.BlockSpec((1, RW_IN), fixed)],
        out_specs=[pl.BlockSpec((tm, RW_IN), row), pl.BlockSpec((tm, NSA_WIDTH), row),
                   pl.BlockSpec((tm, KV_IN), row), pl.BlockSpec((tm, 2 * D_MODEL), row),
                   pl.BlockSpec((tm, LANE), row)],
        out_shape=[jax.ShapeDtypeStruct((T, RW_IN), F32), jax.ShapeDtypeStruct((T, NSA_WIDTH), F32),
                   jax.ShapeDtypeStruct((T, KV_IN), F32), jax.ShapeDtypeStruct((T, 2 * D_MODEL), F32),
                   jax.ShapeDtypeStruct((T, LANE), F32)],
        scratch_shapes=[pltpu.VMEM((8, RW_IN), F32)],
        compiler_params=pltpu.CompilerParams(dimension_semantics=("arbitrary",),
                                             vmem_limit_bytes=VMEM_LIMIT),
        name="proj",
    )(x2d, g_mix.reshape(1, D_MODEL), wp, mu.reshape(1, RW_IN))


PAIR = 2 * RW_HEAD_DIM


def _pair_blocks(x):
    low = lax.broadcasted_iota(jnp.int32, (1, PAIR), 1) < RW_HEAD_DIM
    return jnp.concatenate([jnp.where(low, x, 0.0), jnp.where(low, 0.0, x)], axis=0)


def _pmm(a, b):
    return _bdot(a, _pair_blocks(b))


def _unit_lower_inverse(a_strict, sub_mask, eye):
    ad = [jnp.where(sub_mask, a, 0.0) for a in a_strict]
    ao = [a - d for a, d in zip(a_strict, ad)]
    td = [eye - d for d in ad]
    pw = ad
    for _ in range(int(math.log2(RW_SUB)) - 1):
        pw = [_pmm(x, x) for x in pw]
        td = [_pmm(t, eye + x) for t, x in zip(td, pw)]
    n = [_pmm(t, o) for t, o in zip(td, ao)]
    t = [eye - x for x in n]
    pw = n
    for _ in range(int(math.log2(RW_CHUNK // RW_SUB)) - 1):
        pw = [_pmm(x, x) for x in pw]
        t = [_pmm(a, eye + x) for a, x in zip(t, pw)]
    return [_pmm(a, d) for a, d in zip(t, td)]


def _rwkv_kernel(z_ref, w0_ref, w2_ref, a0_ref, a2_ref, g2_ref, kk_ref, ka_ref, rk_ref, lnw_ref,
                 lnb_ref, avg_ref, o_ref, h_ref):
    c = pl.program_id(1)
    C = RW_CHUNK
    n_pair = RW_WIDTH // PAIR
    nt = (((1,), (1,)), ((), ()))

    @pl.when(c == 0)
    def _():
        h_ref[...] = jnp.zeros_like(h_ref)

    ti = lax.broadcasted_iota(jnp.int32, (C, 1), 0)
    si = lax.broadcasted_iota(jnp.int32, (1, PAIR), 1) & (RW_HEAD_DIM - 1)
    incl, strict = ti >= si, ti > si
    eye = (ti == si).astype(F32)
    sub_shift = int(math.log2(RW_SUB))
    sub_mask = (ti >> sub_shift) == (si >> sub_shift)
    row2 = lax.broadcasted_iota(jnp.int32, (PAIR, 1), 0)
    col2 = lax.broadcasted_iota(jnp.int32, (1, PAIR), 1)
    same_head = (row2 < RW_HEAD_DIM) == (col2 < RW_HEAD_DIM)
    eye2 = row2 == col2
    tri = (lax.broadcasted_iota(jnp.int32, (C, C), 0)
           >= lax.broadcasted_iota(jnp.int32, (C, C), 1)).astype(BF16)

    def head_mean(x):
        xs = jnp.concatenate([x[:, p * PAIR:(p + 1) * PAIR] for p in range(n_pair)], axis=0)
        ms = _bdot(xs, avg_ref[...])
        return jnp.concatenate([ms[p * C:(p + 1) * C] for p in range(n_pair)], axis=1)

    n_rows = z_ref.shape[0]
    rows = []
    for n in range(n_rows):
        z = z_ref[n]
        zr, zk, zv = z[:, 0:512], z[:, 512:1024], z[:, 1024:1536]
        zw, za, zg = z[:, 1536:1600], z[:, 1600:1664], z[:, 1664:1792]
        w_raw = w0_ref[...] + _bdot(jnp.tanh(zw), w2_ref[...])
        logw = -jax.nn.sigmoid(w_raw) * math.exp(-0.5)
        a = jax.nn.sigmoid(a0_ref[...] + _bdot(za, a2_ref[...]))
        gate = _bdot(jax.nn.sigmoid(zg), g2_ref[...])
        kk = zk * kk_ref[...]
        kk = kk / jnp.maximum(jnp.sqrt(head_mean(kk * kk) * RW_HEAD_DIM), 1e-12)
        k = zk * (1.0 + (a - 1.0) * ka_ref[...])
        b = kk * a

        w_hi = logw.astype(BF16)
        w_lo = (logw - w_hi.astype(F32)).astype(BF16)
        cum = (jnp.dot(tri, w_hi, preferred_element_type=F32)
               + jnp.dot(tri, w_lo, preferred_element_type=F32))
        cum_last = cum[C - 1:C, :]
        g_inv = jnp.exp(-cum)
        g_end = jnp.exp(cum_last - cum)
        rows.append(dict(rt=zr * jnp.exp(cum), kt=k * g_inv, bt=b * g_inv, qt=kk * jnp.exp(cum - logw),
                         kh=k * g_end, bh=b * g_end, v=zv, g_last=jnp.exp(cum_last), gate=gate,
                         bonus=head_mean(zr * k * rk_ref[...]) * RW_HEAD_DIM * zv))

    chains = [(n, slice(p * PAIR, (p + 1) * PAIR)) for n in range(n_rows) for p in range(n_pair)]
    part = lambda name: [rows[n][name][:, sl] for n, sl in chains]
    qt, rt, kt, bt, kh, bh, v = (part(x) for x in ("qt", "rt", "kt", "bt", "kh", "bh", "v"))
    lhs = [jnp.concatenate([q, r], axis=0).astype(BF16) for q, r in zip(qt, rt)]
    ab = [lax.dot_general(l, _pair_blocks(x).astype(BF16), nt, preferred_element_type=F32)
          for l, x in zip(lhs, bt)]
    ak = [lax.dot_general(l, _pair_blocks(x).astype(BF16), nt, preferred_element_type=F32)
          for l, x in zip(lhs, kt)]
    a_kb = [jnp.where(strict, x[0:C], 0.0) for x in ab]
    a_rb = [jnp.where(incl, x[C:2 * C], 0.0) for x in ab]
    a_kk = [jnp.where(strict, x[0:C], 0.0) for x in ak]
    a_rk = [jnp.where(incl, x[C:2 * C], 0.0) for x in ak]
    t_inv = _unit_lower_inverse(a_kb, sub_mask, eye)

    h = [h_ref[n, sl.start // PAIR] for n, sl in chains]
    vb = [_pair_blocks(x) for x in v]
    rhs = [_bdot(jnp.concatenate([q, akk], axis=1), jnp.concatenate([hh, vv], axis=0))
           for q, akk, hh, vv in zip(qt, a_kk, h, vb)]
    u = [_pmm(t, x) for t, x in zip(t_inv, rhs)]
    outs = [_bdot(jnp.concatenate([r, ark, -arb], axis=1), jnp.concatenate([hh, vv, _pair_blocks(uu)], axis=0))
            for r, ark, arb, hh, vv, uu in zip(rt, a_rk, a_rb, h, vb, u)]
    upd = [_bdot(jnp.concatenate([x, -y], axis=0).T, jnp.concatenate([vv, uu], axis=0))
           for x, y, vv, uu in zip(kh, bh, v, u)]
    for (n, sl), hh, dd in zip(chains, h, upd):
        decay_col = jnp.sum(jnp.where(eye2, rows[n]["g_last"][:, sl], 0.0), axis=1, keepdims=True)
        h_ref[n, sl.start // PAIR] = decay_col * hh + jnp.where(same_head, dd, 0.0)

    for n in range(n_rows):
        o = jnp.concatenate(outs[n * n_pair:(n + 1) * n_pair], axis=1)
        d = o - head_mean(o)
        on = d * lax.rsqrt(head_mean(d * d) + GN_EPS)
        o_ref[n] = (on * lnw_ref[...] + lnb_ref[...] + rows[n]["bonus"]) * rows[n]["gate"]


def _rwkv(zrw, w0, w2, a0, a2, g2, k_k, k_a, r_k, ln_w, ln_b):
    B, S, _ = zrw.shape
    C = RW_CHUNK
    nb = RW_ROWS
    hid = np.arange(PAIR) // RW_HEAD_DIM
    avg = jnp.asarray((hid[:, None] == hid[None, :]).astype(np.float32) / RW_HEAD_DIM)
    vec = lambda a: a.reshape(1, RW_WIDTH)
    fixed = lambda shape: pl.BlockSpec(shape, lambda b, c: (0,) * len(shape))
    return pl.pallas_call(
        _rwkv_kernel,
        grid=(B // nb, S // C),
        in_specs=[pl.BlockSpec((nb, C, RW_IN), lambda b, c: (b, c, 0)),
                  fixed((1, RW_WIDTH)), fixed((DECAY_LORA, RW_WIDTH)),
                  fixed((1, RW_WIDTH)), fixed((AAA_LORA, RW_WIDTH)),
                  fixed((GATE_LORA, RW_WIDTH)), fixed((1, RW_WIDTH)), fixed((1, RW_WIDTH)),
                  fixed((1, RW_WIDTH)), fixed((1, RW_WIDTH)), fixed((1, RW_WIDTH)),
                  fixed((PAIR, PAIR))],
        out_specs=pl.BlockSpec((nb, C, RW_WIDTH), lambda b, c: (b, c, 0)),
        out_shape=jax.ShapeDtypeStruct((B, S, RW_WIDTH), F32),
        scratch_shapes=[pltpu.VMEM((nb, RW_WIDTH // PAIR, PAIR, PAIR), F32)],
        compiler_params=pltpu.CompilerParams(dimension_semantics=("parallel", "arbitrary"),
                                             vmem_limit_bytes=VMEM_LIMIT),
        name="rwkv",
    )(zrw, vec(w0), w2, vec(a0), a2, g2, vec(k_k), vec(k_a), vec(r_k), vec(ln_w), vec(ln_b), avg)


def _gelu_tanh(x):
    return 0.5 * x * (1.0 + jnp.tanh(math.sqrt(2.0 / math.pi) * (x + 0.044715 * (x * x * x))))


def _key_features(pos_hi, pos_lo, block, n, n_sel):
    lane = lax.broadcasted_iota(jnp.int32, (n, NSA_HEAD_DIM), 1)
    feat = jnp.where(lane == n_sel, pos_hi, jnp.where(lane == n_sel + 1, pos_lo, 0.0))
    return feat if block is None else jnp.where(lane == block, 1.0, feat)


def _nsa_prep_kernel(zkc_ref, zvc_ref, zks_ref, zvs_ref, zkw_ref, zvw_ref, pk_ref, pv_ref, kw1_ref,
                     kw2_ref, vw1_ref, vw2_ref, kc_ref, vct_ref, ksa_ref, kwa_ref, vst_ref, vwt_ref):
    S = zkc_ref.shape[1]
    n_grp = S // CMP_STRIDE
    Dh = NSA_HEAD_DIM
    half = CMP_BLOCK // 2
    n_sel = S // SEL_BLOCK
    jrow = lax.broadcasted_iota(jnp.int32, (n_grp, 1), 0)
    cmp_feat = _key_features((jrow >> 3).astype(F32),
                             ((jrow & 7) * CMP_STRIDE).astype(F32) + 0.5 * (CMP_BLOCK - 1),
                             None, n_grp, n_sel)
    for is_v, (z_ref, pos_ref, w1_ref, w2_ref) in enumerate(((zkc_ref, pk_ref, kw1_ref, kw2_ref),
                                                            (zvc_ref, pv_ref, vw1_ref, vw2_ref))):
        for g in range(NSA_KV_HEADS):
            lo = jnp.zeros((n_grp, Dh), F32)
            hi = jnp.zeros((n_grp, Dh), F32)
            for l in range(half):
                xs = z_ref[0, pl.ds(l, n_grp, stride=CMP_STRIDE), :]
                xg = xs[:, g * Dh:(g + 1) * Dh]
                lo = lo + _bdot(xg + pos_ref[l:l + 1, :], w1_ref[l * Dh:(l + 1) * Dh, :])
                hi = hi + _bdot(xg + pos_ref[half + l:half + l + 1, :],
                                w1_ref[(half + l) * Dh:(half + l + 1) * Dh, :])
            pre = lo + pltpu.roll(hi, n_grp - 1, 0)
            out = jnp.where(jrow < n_grp - 1, _bdot(_gelu_tanh(pre), w2_ref[...]), 0.0)
            if is_v:
                out_t = jnp.concatenate([out, jnp.zeros_like(out)], axis=1).T
                vct_ref[0, g] = out_t[0:Dh, :].astype(BF16)
            else:
                kc_ref[0, g] = jnp.concatenate([out, cmp_feat], axis=1)

    prow = lax.broadcasted_iota(jnp.int32, (S, 1), 0)
    p_hi, p_lo = (prow >> 7).astype(F32), (prow & (LANE - 1)).astype(F32)
    for z_ref, out_ref, block in ((zks_ref, ksa_ref, prow >> int(math.log2(SEL_BLOCK))),
                                  (zkw_ref, kwa_ref, None)):
        kfull = z_ref[0]
        key_feat = _key_features(p_hi, p_lo, block, S, n_sel)
        for g in range(NSA_KV_HEADS):
            out_ref[0, g] = jnp.concatenate([kfull[:, g * Dh:(g + 1) * Dh], key_feat], axis=1).astype(BF16)
    for z_ref, out_ref in ((zvs_ref, vst_ref), (zvw_ref, vwt_ref)):
        for j in range(S // TK):
            out_ref[0, j] = z_ref[0, j * TK:(j + 1) * TK, :].T.astype(BF16)


def _nsa_prep(zkv, pos_k, pos_v, kw1, kw2, vw1, vw2):
    B, S, _ = zkv.shape
    n_grp = S // CMP_STRIDE
    n_kt = S // TK
    Dh = NSA_HEAD_DIM
    G = NSA_KV_HEADS
    fixed = lambda shape: pl.BlockSpec(shape, lambda b: (0,) * len(shape))
    col = lambda c: pl.BlockSpec((1, S, NSA_KV_WIDTH), lambda b: (b, 0, c))
    whole = lambda shape: pl.BlockSpec((1,) + shape, lambda b: (b,) + (0,) * len(shape))
    shapes = [((G, n_grp, 2 * Dh), F32), ((G, Dh, n_grp), BF16), ((G, S, 2 * Dh), BF16),
              ((G, S, 2 * Dh), BF16), ((n_kt, G * Dh, TK), BF16), ((n_kt, G * Dh, TK), BF16)]
    return pl.pallas_call(
        _nsa_prep_kernel,
        grid=(B,),
        in_specs=[col(c) for c in range(6)] + [
            fixed((CMP_BLOCK, Dh)), fixed((CMP_BLOCK, Dh)),
            fixed((CMP_BLOCK * Dh, Dh)), fixed((Dh, Dh)),
            fixed((CMP_BLOCK * Dh, Dh)), fixed((Dh, Dh))],
        out_specs=[whole(s) for s, _ in shapes],
        out_shape=[jax.ShapeDtypeStruct((B,) + s, d) for s, d in shapes],
        compiler_params=pltpu.CompilerParams(dimension_semantics=("parallel",),
                                             vmem_limit_bytes=VMEM_LIMIT),
        name="nsa_prep",
    )(zkv, zkv, zkv, zkv, zkv, zkv, pos_k, pos_v, kw1, kw2, vw1, vw2)


def _nsa_attn_kernel(q_ref, kc_ref, vct_ref, ksa_ref, kwa_ref, vst_ref, vwt_ref, gl_ref, gb_ref,
                     ovlt_ref, slope_ref, o_ref, acc_ref, ot_ref):
    i = pl.program_id(1)
    Dh = NSA_HEAD_DIM
    R = NSA_REP
    N = R * TQ
    n_cmp_pad = kc_ref.shape[2]
    n_sel = ovlt_ref.shape[0]
    G = NSA_KV_HEADS
    nt = (((1,), (1,)), ((), ()))
    log2e = math.log2(math.e)
    t0 = i * TQ
    t_row = t0 + lax.broadcasted_iota(jnp.int32, (1, TQ), 1)
    c_col = lax.broadcasted_iota(jnp.int32, (TK, 1), 0)
    sgate_t = jax.nn.sigmoid(gl_ref[0] + gb_ref[...]).T
    lane_f = lax.broadcasted_iota(jnp.int32, (1, Dh), 1)
    heads = lambda x: jnp.concatenate([x] * R, axis=1)

    def key_dist(j):
        return t_row - (j * TK + c_col)

    def queries(g, sel_feat):
        parts = []
        for r in range(R):
            h = g * R + r
            sl = slope_ref[:, h:h + 1] * log2e
            feat = jnp.where(lane_f == n_sel, sl * LANE, jnp.where(lane_f == n_sel + 1, sl, sel_feat))
            parts.append(jnp.concatenate(
                [q_ref[0, :, h * Dh:(h + 1) * Dh] * (Dh ** -0.5 * log2e), jnp.broadcast_to(feat, (TQ, Dh))],
                axis=1))
        return jnp.concatenate(parts, axis=0)

    qab, o_cmp = [], []
    for g in range(G):
        jc = lax.broadcasted_iota(jnp.int32, (n_cmp_pad, 1), 0)
        ok_c = (jc * CMP_STRIDE + (CMP_BLOCK - 1) <= t_row) & (jc < n_cmp_pad - 1)
        s_c = (lax.dot_general(kc_ref[0, g], queries(g, 0.0), nt, precision=HI, preferred_element_type=F32)
               + heads(jnp.where(ok_c, 0.0, NEG_INF)))
        e_c = jnp.exp2(s_c - jnp.max(s_c, axis=0, keepdims=True))
        any_c = heads(t_row >= CMP_BLOCK - 1)
        p_c = e_c * jnp.where(any_c, 1.0 / jnp.sum(e_c, axis=0, keepdims=True), 0.0)
        o_cmp.append(jnp.dot(vct_ref[0, g], p_c.astype(BF16), preferred_element_type=F32))

        p_sum = p_c[:, 0:TQ]
        for r in range(1, R):
            p_sum = p_sum + p_c[:, r * TQ:(r + 1) * TQ]
        imp = jnp.dot(ovlt_ref[...], p_sum, precision=HI, preferred_element_type=F32)
        kb = lax.broadcasted_iota(jnp.int32, (n_sel, 1), 0)
        kbf = kb.astype(F32)
        blk_t = t_row >> int(math.log2(SEL_BLOCK))
        forced = (kb == 0) | (kb == blk_t) | (kb == blk_t - 1)
        cur = jnp.where(forced, FORCE_SCORE, jnp.where(kb <= blk_t, imp, -FORCE_SCORE))
        sel_bias = jnp.full((n_sel, TQ), NEG_INF, F32)
        for _ in range(min(N_SELECT, n_sel)):
            mx = jnp.max(cur, axis=0, keepdims=True)
            first = jnp.min(jnp.where(cur == mx, kbf, float(n_sel)), axis=0, keepdims=True)
            hit = kbf == first
            sel_bias = jnp.where(hit, 0.0, sel_bias)
            cur = jnp.where(hit, -3e38, cur)
        sel_feat = jnp.concatenate([sel_bias, jnp.zeros((LANE - n_sel, TQ), F32)], axis=0).T[:, 0:Dh]
        qab.append(queries(g, sel_feat).astype(BF16))

    def tile(j, carry, k_ref, vt_ref, bias, slot):
        out = []
        for g in range(G):
            m, l = carry[g]
            k = k_ref[0, g, pl.ds(pl.multiple_of(j * TK, TK), TK), :]
            s = lax.dot_general(k, qab[g], nt, preferred_element_type=F32)
            if bias is not None:
                s = s + heads(bias)
            m_new = jnp.maximum(m, jnp.max(s, axis=0, keepdims=True))
            alpha = jnp.exp2(m - m_new)
            p = jnp.exp2(s - m_new)
            l = alpha * l + jnp.sum(p, axis=0, keepdims=True)
            vt = vt_ref[0, j, g * Dh:(g + 1) * Dh, :]
            acc_ref[slot, g] = (alpha * acc_ref[slot, g]
                                + jnp.dot(vt, p.astype(BF16), preferred_element_type=F32))
            out.append((m_new, l))
        return tuple(out)

    def window_bias(j):
        d = key_dist(j)
        return jnp.where((d >= 0) & (d < WINDOW), 0.0, NEG_INF)

    init = ((jnp.full((1, N), NEG_INF, F32), jnp.zeros((1, N), F32)),) * G
    acc_ref[...] = jnp.zeros_like(acc_ref)
    causal = jnp.where(key_dist(i) >= 0, 0.0, NEG_INF)
    carry = lax.fori_loop(0, i, lambda j, c: tile(j, c, ksa_ref, vst_ref, None, 0), init)
    stat_s = tile(i, carry, ksa_ref, vst_ref, causal, 0)
    carry = lax.fori_loop(jnp.maximum(i - WINDOW // TK, 0), i,
                          lambda j, c: tile(j, c, kwa_ref, vwt_ref, window_bias(j), 1), init)
    stat_w = tile(i, carry, kwa_ref, vwt_ref, causal, 1)

    for g in range(G):
        o_sel = acc_ref[0, g] * (1.0 / stat_s[g][1])
        o_win = acc_ref[1, g] * (1.0 / stat_w[g][1])
        for r in range(R):
            h = g * R + r
            cols = slice(r * TQ, (r + 1) * TQ)
            ot_ref[h * Dh:(h + 1) * Dh, :] = (
                sgate_t[3 * h:3 * h + 1, :] * o_cmp[g][:, cols]
                + sgate_t[3 * h + 1:3 * h + 2, :] * o_sel[:, cols]
                + sgate_t[3 * h + 2:3 * h + 3, :] * o_win[:, cols])
    o_ref[0] = ot_ref[...].T


def _nsa_attn(zq, kc, vct, ksa, kwa, vst, vwt, zgate, gate_b):
    B, S, _ = zq.shape
    n_sel = S // SEL_BLOCK
    n_cmp = (S - CMP_BLOCK) // CMP_STRIDE + 1
    n_cmp_pad = kc.shape[2]
    n_kt = S // TK
    G, Dh = NSA_KV_HEADS, NSA_HEAD_DIM
    cmp_start = np.arange(n_cmp) * CMP_STRIDE
    sel_start = np.arange(n_sel) * SEL_BLOCK
    overlap = np.clip(np.minimum(cmp_start[:, None] + CMP_BLOCK, sel_start[None, :] + SEL_BLOCK)
                      - np.maximum(cmp_start[:, None], sel_start[None, :]), 0, None) / CMP_BLOCK
    ovlt = np.zeros((n_sel, n_cmp_pad), np.float32)
    ovlt[:, :n_cmp] = overlap.T
    slopes = (2.0 ** (-8.0 * np.arange(1, NSA_HEADS + 1) / NSA_HEADS)).astype(np.float32).reshape(1, NSA_HEADS)
    gb = jnp.pad(gate_b, (0, LANE - N_GATE)).reshape(1, LANE)
    fixed = lambda shape: pl.BlockSpec(shape, lambda b, i: (0,) * len(shape))
    per_b = lambda shape: pl.BlockSpec((1,) + shape, lambda b, i: (b,) + (0,) * len(shape))
    return pl.pallas_call(
        _nsa_attn_kernel,
        grid=(B, S // TQ),
        in_specs=[pl.BlockSpec((1, TQ, NSA_WIDTH), lambda b, i: (b, i, 0)),
                  per_b((G, n_cmp_pad, 2 * Dh)), per_b((G, Dh, n_cmp_pad)),
                  per_b((G, S, 2 * Dh)), per_b((G, S, 2 * Dh)),
                  per_b((n_kt, G * Dh, TK)), per_b((n_kt, G * Dh, TK)),
                  pl.BlockSpec((1, TQ, LANE), lambda b, i: (b, i, 0)),
                  fixed((1, LANE)), fixed((n_sel, n_cmp_pad)), fixed((1, NSA_HEADS))],
        out_specs=pl.BlockSpec((1, TQ, NSA_WIDTH), lambda b, i: (b, i, 0)),
        out_shape=jax.ShapeDtypeStruct((B, S, NSA_WIDTH), F32),
        scratch_shapes=[pltpu.VMEM((2, G, Dh, NSA_REP * TQ), F32),
                        pltpu.VMEM((NSA_WIDTH, TQ), F32)],
        compiler_params=pltpu.CompilerParams(dimension_semantics=("parallel", "arbitrary"),
                                             vmem_limit_bytes=VMEM_LIMIT),
        name="nsa_attn",
    )(zq, kc, vct, ksa, kwa, vst, vwt, zgate, gb, jnp.asarray(ovlt), jnp.asarray(slopes))


def _mix_kernel(x_ref, yr_ref, yn_ref, zmg_ref, ur_ref, un_ref, wo_ref, gf_ref, wr_ref, br_ref,
                x1_ref, h2_ref, rt_ref, cnt_ref):
    tm = x_ref.shape[0]
    zmg = zmg_ref[...]
    mixed = (jax.nn.sigmoid(zmg[:, :D_MODEL]) * _bdot(yr_ref[...], ur_ref[...])
             + jax.nn.sigmoid(zmg[:, D_MODEL:]) * _bdot(yn_ref[...], un_ref[...]))
    x1 = x_ref[...] + _bdot(mixed, wo_ref[...])
    x1_ref[...] = x1
    h2 = _rms(x1, gf_ref[...])
    h2_ref[...] = h2
    logits = jnp.dot(h2, wr_ref[...], precision=HI, preferred_element_type=F32) + br_ref[...]
    lane = lax.broadcasted_iota(jnp.int32, (1, LANE), 1).astype(F32)
    gl = jnp.where(lane < N_GROUPS, logits, NEG_INF)
    gmax = jnp.max(gl, axis=-1, keepdims=True)
    g_sel = jnp.min(jnp.where(gl == gmax, lane, float(LANE)), axis=-1, keepdims=True)
    p_group = 1.0 / jnp.sum(jnp.exp(gl - gmax), axis=-1, keepdims=True)
    e_lane = lane - N_GROUPS
    in_grp = (e_lane >= g_sel * EXPERTS_PER_GROUP) & (e_lane < (g_sel + 1.0) * EXPERTS_PER_GROUP)
    el = jnp.where(in_grp, logits, NEG_INF)
    m1 = jnp.max(el, axis=-1, keepdims=True)
    i1 = jnp.min(jnp.where(el == m1, e_lane, float(LANE)), axis=-1, keepdims=True)
    el2 = jnp.where(e_lane == i1, 2.0 * NEG_INF, el)
    m2 = jnp.max(el2, axis=-1, keepdims=True)
    i2 = jnp.min(jnp.where(el2 == m2, e_lane, float(LANE)), axis=-1, keepdims=True)
    r2 = jnp.exp(m2 - m1)
    g1 = p_group / (1.0 + r2)
    g2 = p_group * r2 / (1.0 + r2)

    @pl.when(pl.program_id(0) == 0)
    def _():
        cnt_ref[...] = jnp.zeros_like(cnt_ref)

    pick1, pick2 = e_lane == i1, e_lane == i2
    both = pick1.astype(F32) + pick2.astype(F32)
    earlier = (lax.broadcasted_iota(jnp.int32, (tm, tm), 1)
               < lax.broadcasted_iota(jnp.int32, (tm, tm), 0)).astype(BF16)
    before = jnp.dot(earlier, both.astype(BF16), preferred_element_type=F32) + cnt_ref[0:1, :]
    r1 = jnp.sum(jnp.where(pick1, before, 0.0), axis=-1, keepdims=True)
    r2 = jnp.sum(jnp.where(pick2, before, 0.0), axis=-1, keepdims=True)
    cnt_ref[...] = cnt_ref[...] + jnp.sum(both, axis=0, keepdims=True)
    fields = (i1, i2, g1, g2, r1, r2)
    out = jnp.zeros((tm, LANE), F32)
    for f, val in enumerate(fields):
        out = jnp.where(lane == f, val, out)
    rt_ref[...] = out


def _mix(x2d, y_rw, y_nsa, zmg, w_up_r, w_up_n, w_out, g_ffn, w_group, b_group, w_router, b_router):
    T = x2d.shape[0]
    tm = TM_PROJ
    n_r = N_GROUPS + N_EXPERTS
    wr = jnp.pad(jnp.concatenate([w_group, w_router], axis=1), ((0, 0), (0, LANE - n_r)))
    br = jnp.pad(jnp.concatenate([b_group, b_router]), (0, LANE - n_r)).reshape(1, LANE)
    row = lambda i: (i, 0)
    fixed = lambda i: (0, 0)
    return pl.pallas_call(
        _mix_kernel,
        grid=(T // tm,),
        in_specs=[pl.BlockSpec((tm, D_MODEL), row), pl.BlockSpec((tm, RW_WIDTH), row),
                  pl.BlockSpec((tm, NSA_WIDTH), row), pl.BlockSpec((tm, 2 * D_MODEL), row),
                  pl.BlockSpec((RW_WIDTH, D_MODEL), fixed), pl.BlockSpec((NSA_WIDTH, D_MODEL), fixed),
                  pl.BlockSpec((D_MODEL, D_MODEL), fixed), pl.BlockSpec((1, D_MODEL), fixed),
                  pl.BlockSpec((D_MODEL, LANE), fixed), pl.BlockSpec((1, LANE), fixed)],
        out_specs=[pl.BlockSpec((tm, D_MODEL), row), pl.BlockSpec((tm, D_MODEL), row),
                   pl.BlockSpec((tm, LANE), row), pl.BlockSpec((8, LANE), fixed)],
        out_shape=[jax.ShapeDtypeStruct((T, D_MODEL), F32), jax.ShapeDtypeStruct((T, D_MODEL), F32),
                   jax.ShapeDtypeStruct((T, LANE), F32), jax.ShapeDtypeStruct((8, LANE), F32)],
        compiler_params=pltpu.CompilerParams(dimension_semantics=("arbitrary",),
                                             vmem_limit_bytes=VMEM_LIMIT),
        name="mix",
    )(x2d, y_rw, y_nsa, zmg, w_up_r.astype(BF16), w_up_n.astype(BF16), w_out.astype(BF16),
      g_ffn.reshape(1, D_MODEL), wr, br)


def _route_tables(rt, cnt, T):
    n_rows = T * TOP_K + N_EXPERTS * ROW_BLOCK
    n_blk = n_rows // ROW_BLOCK
    counts = cnt[0, N_GROUPS:N_GROUPS + N_EXPERTS].astype(jnp.int32)
    padded = (counts + ROW_BLOCK - 1) // ROW_BLOCK * ROW_BLOCK
    pends = jnp.cumsum(padded)
    pstarts = pends - padded
    expert = rt[:, 0:TOP_K].astype(jnp.int32)
    rank = rt[:, 2 * TOP_K:3 * TOP_K].astype(jnp.int32)
    seg_start = jnp.sum(jnp.where(expert[..., None] == jnp.arange(N_EXPERTS), pstarts, 0), axis=-1)
    dest = (seg_start + rank).reshape(T * TOP_K)
    blk_start = jnp.arange(n_blk) * ROW_BLOCK
    blk_expert = jnp.minimum(jnp.sum(pends[None, :] <= blk_start[:, None], axis=1), N_EXPERTS - 1)
    blk_nv = jnp.clip(counts[blk_expert] - (blk_start - pstarts[blk_expert]), 0, ROW_BLOCK)
    return dest.astype(jnp.int32), blk_expert.astype(jnp.int32), blk_nv.astype(jnp.int32)


def _row_copies(index_ref, base, n, make):
    for t in range(n):
        for k in range(TOP_K):
            make(base + t, k, index_ref[(base + t) * TOP_K + k]).start()


def _dispatch_kernel(dest_ref, h2_hbm, xs_init_hbm, xs_hbm, sem):
    del xs_init_hbm
    i = pl.program_id(0)
    slot = i % 2
    n_copy = TM_PROJ * TOP_K

    def row_copy(tok, k, row):
        return pltpu.make_async_copy(h2_hbm.at[pl.ds(tok, 1)], xs_hbm.at[pl.ds(row, 1)], sem.at[slot])

    def drain(s):
        pltpu.make_async_copy(h2_hbm.at[pl.ds(0, n_copy)], xs_hbm.at[pl.ds(0, n_copy)], sem.at[s]).wait()

    _row_copies(dest_ref, i * TM_PROJ, TM_PROJ, row_copy)

    @pl.when(i > 0)
    def _():
        drain(1 - slot)

    @pl.when(i == pl.num_programs(0) - 1)
    def _():
        drain(slot)


def _dispatch(h2, dest):
    T = h2.shape[0]
    n_rows = T * TOP_K + N_EXPERTS * ROW_BLOCK
    grid_spec = pltpu.PrefetchScalarGridSpec(
        num_scalar_prefetch=1,
        grid=(T // TM_PROJ,),
        in_specs=[pl.BlockSpec(memory_space=pl.ANY), pl.BlockSpec(memory_space=pl.ANY)],
        out_specs=pl.BlockSpec(memory_space=pl.ANY),
        scratch_shapes=[pltpu.SemaphoreType.DMA((2,))])
    return pl.pallas_call(
        _dispatch_kernel,
        grid_spec=grid_spec,
        out_shape=jax.ShapeDtypeStruct((n_rows, D_MODEL), F32),
        input_output_aliases={2: 0},
        compiler_params=pltpu.CompilerParams(dimension_semantics=("arbitrary",),
                                             vmem_limit_bytes=VMEM_LIMIT),
        name="dispatch",
    )(dest, h2, jnp.zeros((n_rows, D_MODEL), F32))


def _expert_kernel(be_ref, nv_ref, xs_ref, wgu_ref, wd_ref, o_ref, wgu_b, wd_b):
    i = pl.program_id(0)
    nv = nv_ref[i]
    changed = jnp.logical_or(i == 0, be_ref[i] != be_ref[jnp.maximum(i - 1, 0)])

    @pl.when(jnp.logical_and(nv > 0, changed))
    def _():
        wgu_b[...] = wgu_ref[0].astype(BF16)
        wd_b[...] = wd_ref[0].astype(BF16)

    @pl.when(nv > 0)
    def _():
        gu = jnp.dot(xs_ref[...].astype(BF16), wgu_b[...], preferred_element_type=F32)
        gate_h, up_h = gu[:, :D_EXPERT], gu[:, D_EXPERT:]
        mid = gate_h * jax.nn.sigmoid(gate_h) * up_h
        o_ref[...] = jnp.dot(mid.astype(BF16), wd_b[...], preferred_element_type=F32)

    @pl.when(nv == 0)
    def _():
        o_ref[...] = jnp.zeros_like(o_ref)


def _experts(xs, blk_expert, blk_nv, w_gate_up, w_down):
    n_rows = xs.shape[0]
    grid_spec = pltpu.PrefetchScalarGridSpec(
        num_scalar_prefetch=2,
        grid=(n_rows // ROW_BLOCK,),
        in_specs=[pl.BlockSpec((ROW_BLOCK, D_MODEL), lambda i, be, nv: (i, 0)),
                  pl.BlockSpec((1, D_MODEL, 2 * D_EXPERT), lambda i, be, nv: (be[i], 0, 0)),
                  pl.BlockSpec((1, D_EXPERT, D_MODEL), lambda i, be, nv: (be[i], 0, 0))],
        out_specs=pl.BlockSpec((ROW_BLOCK, D_MODEL), lambda i, be, nv: (i, 0)),
        scratch_shapes=[pltpu.VMEM((D_MODEL, 2 * D_EXPERT), BF16), pltpu.VMEM((D_EXPERT, D_MODEL), BF16)])
    return pl.pallas_call(
        _expert_kernel,
        grid_spec=grid_spec,
        out_shape=jax.ShapeDtypeStruct((n_rows, D_MODEL), F32),
        compiler_params=pltpu.CompilerParams(dimension_semantics=("arbitrary",),
                                             vmem_limit_bytes=VMEM_LIMIT),
        name="experts",
    )(blk_expert, blk_nv, xs, w_gate_up, w_down)


def _ple_kernel(dest_ref, x1_ref, rt_ref, p_ref, rows_hbm, wpp_ref, gpn_ref, ggi_ref, wpg_ref, gfin_ref,
                y_ref, ebuf, sem, *, last_layer):
    i = pl.program_id(0)
    tm = x1_ref.shape[0]
    slot = i % 2

    def fetch(tile, s):
        def row_copy(tok, k, row):
            return pltpu.make_async_copy(rows_hbm.at[pl.ds(row, 1)],
                                         ebuf.at[s, k, pl.ds(tok - tile * tm, 1)], sem.at[s])
        _row_copies(dest_ref, tile * tm, tm, row_copy)

    @pl.when(i == 0)
    def _():
        fetch(0, 0)

    @pl.when(i + 1 < pl.num_programs(0))
    def _():
        fetch(i + 1, 1 - slot)

    for k in range(TOP_K):
        pltpu.make_async_copy(rows_hbm.at[pl.ds(0, tm)], ebuf.at[slot, k], sem.at[slot]).wait()
    moe = rt_ref[:, TOP_K:TOP_K + 1] * ebuf[slot, 0]
    for k in range(1, TOP_K):
        moe = moe + rt_ref[:, TOP_K + k:TOP_K + k + 1] * ebuf[slot, k]
    x2 = x1_ref[...] + moe
    e = _rms(_bdot(p_ref[...], wpp_ref[...]), gpn_ref[...])
    gate = jax.nn.sigmoid(_bdot(_rms(x2, ggi_ref[...]), wpg_ref[...]))
    x3 = x2 + gate * e
    y_ref[...] = _rms(x3, gfin_ref[...]) if last_layer else x3


def _ple(x1, rt, dest, expert_rows, p2d, w_pp, g_pn, g_gi, w_pg, g_final, last_layer):
    T = x1.shape[0]
    tm = TM_PROJ
    row = lambda i, d: (i, 0)
    fixed = lambda i, d: (0, 0)
    vec = lambda a: a.reshape(1, D_MODEL)
    grid_spec = pltpu.PrefetchScalarGridSpec(
        num_scalar_prefetch=1,
        grid=(T // tm,),
        in_specs=[pl.BlockSpec((tm, D_MODEL), row), pl.BlockSpec((tm, LANE), row),
                  pl.BlockSpec((tm, PLE_DIM), row), pl.BlockSpec(memory_space=pl.ANY),
                  pl.BlockSpec((PLE_DIM, D_MODEL), fixed), pl.BlockSpec((1, D_MODEL), fixed),
                  pl.BlockSpec((1, D_MODEL), fixed), pl.BlockSpec((D_MODEL, D_MODEL), fixed),
                  pl.BlockSpec((1, D_MODEL), fixed)],
        out_specs=pl.BlockSpec((tm, D_MODEL), row),
        scratch_shapes=[pltpu.VMEM((2, TOP_K, tm, D_MODEL), F32), pltpu.SemaphoreType.DMA((2,))])
    return pl.pallas_call(
        functools.partial(_ple_kernel, last_layer=last_layer),
        grid_spec=grid_spec,
        out_shape=jax.ShapeDtypeStruct((T, D_MODEL), F32),
        compiler_params=pltpu.CompilerParams(dimension_semantics=("arbitrary",),
                                             vmem_limit_bytes=VMEM_LIMIT),
        name="ple",
    )(dest, x1, rt, p2d, expert_rows, w_pp.astype(BF16), vec(g_pn), vec(g_gi), w_pg.astype(BF16),
      vec(g_final))


def kernel(x, p, g_mix, w_in, mu_shift, rw_w0, rw_w2, rw_a0, rw_a2, rw_g2, rw_k_k, rw_k_a, rw_r_k, rw_ln_w, rw_ln_b, cmp_pos_k, cmp_pos_v, cmp_k_w1, cmp_k_w2, cmp_v_w1, cmp_v_w2, nsa_gate_b, w_up_rwkv, w_up_nsa, w_out, g_ffn, w_group, b_group, w_router, b_router, w_exp_gate_up, w_exp_down, w_ple_proj, g_ple_norm, g_ple_gate_in, w_ple_gate, g_final):
    B, S, D = x.shape
    T = B * S
    depth = p.shape[0]
    xc = x.reshape(T, D)
    for i in range(depth):
        zrw, zq, zkv, zmg, zgate = _proj(xc, g_mix[i], w_in[i], mu_shift[i], S)
        y_rw = _rwkv(zrw.reshape(B, S, RW_IN), rw_w0[i], rw_w2[i], rw_a0[i], rw_a2[i], rw_g2[i],
                     rw_k_k[i], rw_k_a[i], rw_r_k[i], rw_ln_w[i], rw_ln_b[i])
        zkv3 = zkv.reshape(B, S, KV_IN)
        nsa_kv = _nsa_prep(zkv3, cmp_pos_k[i], cmp_pos_v[i], cmp_k_w1[i], cmp_k_w2[i],
                           cmp_v_w1[i], cmp_v_w2[i])
        y_nsa = _nsa_attn(zq.reshape(B, S, NSA_WIDTH), *nsa_kv, zgate.reshape(B, S, LANE),
                          nsa_gate_b[i])
        x1, h2, rt, cnt = _mix(xc, y_rw.reshape(T, RW_WIDTH), y_nsa.reshape(T, NSA_WIDTH), zmg,
                               w_up_rwkv[i], w_up_nsa[i], w_out[i], g_ffn[i], w_group[i], b_group[i],
                               w_router[i], b_router[i])
        dest, blk_expert, blk_nv = _route_tables(rt, cnt, T)
        expert_rows = _experts(_dispatch(h2, dest), blk_expert, blk_nv, w_exp_gate_up[i], w_exp_down[i])
        xc = _ple(x1, rt, dest, expert_rows, p[i].reshape(T, PLE_DIM), w_ple_proj[i], g_ple_norm[i],
                  g_ple_gate_in[i], w_ple_gate[i], g_final, i == depth - 1)
    return xc.reshape(B, S, D)
```

```python
import functools
import math

import jax
import jax.numpy as jnp
import numpy as np
from jax import lax
from jax.experimental import pallas as pl
from jax.experimental.pallas import tpu as pltpu

F32 = jnp.float32
BF16 = jnp.bfloat16
HI = lax.Precision.HIGHEST

D_MODEL = 1024
RW_HEADS = 8
RW_HEAD_DIM = 64
RW_WIDTH = 512
DECAY_LORA = 64
AAA_LORA = 64
GATE_LORA = 128
GN_EPS = 64e-5
RW_IN = 3 * RW_WIDTH + DECAY_LORA + AAA_LORA + GATE_LORA

NSA_HEADS = 8
NSA_KV_HEADS = 2
NSA_REP = NSA_HEADS // NSA_KV_HEADS
NSA_HEAD_DIM = 64
NSA_WIDTH = 512
NSA_KV_WIDTH = 128
CMP_BLOCK = 32
CMP_STRIDE = 16
SEL_BLOCK = 64
N_SELECT = 8
WINDOW = 512
N_NSA_BRANCH = 3
FORCE_SCORE = 1e6
NEG_INF = -1e30

N_GROUPS = 4
EXPERTS_PER_GROUP = 8
N_EXPERTS = 32
TOP_K = 2
D_EXPERT = 512
ROW_BLOCK = 256
PLE_DIM = 256
NORM_EPS = 1e-6

N_GATE = N_NSA_BRANCH * NSA_HEADS
ATT_IN = NSA_WIDTH + 6 * NSA_KV_WIDTH
KV_OFF = RW_IN + NSA_WIDTH
KV_IN = 6 * NSA_KV_WIDTH
GATE_OFF = RW_IN + ATT_IN
MERGE_OFF = GATE_OFF + N_GATE
LANE = 128

RW_CHUNK = 64
RW_SUB = 16
RW_ROWS = 4
TQ = 256
TK = 256
TM_PROJ = 256
VMEM_LIMIT = 56 * 1024 * 1024


def _bdot(a, b):
    return jnp.dot(a.astype(BF16), b.astype(BF16), preferred_element_type=F32)


def _bdot_nt(a, b):
    return lax.dot_general(a.astype(BF16), b.astype(BF16), (((1,), (1,)), ((), ())),
                           preferred_element_type=F32)


def _rms(x, g):
    return x * lax.rsqrt(jnp.mean(x * x, axis=-1, keepdims=True) + NORM_EPS) * g


def _proj_kernel(x_ref, g_ref, w_ref, mu_ref, zrw_ref, zq_ref, zkv_ref, zmg_ref, zgate_ref,
                 carry_ref, *, tiles_per_seq):
    i = pl.program_id(0)
    tm = x_ref.shape[0]

    @pl.when(i % tiles_per_seq == 0)
    def _():
        carry_ref[...] = jnp.zeros_like(carry_ref)

    h = _rms(x_ref[...], g_ref[...]).astype(BF16)
    z = jnp.dot(h, w_ref[:, 0:RW_IN], preferred_element_type=F32)
    row = lax.broadcasted_iota(jnp.int32, (tm, 1), 0)
    prev = jnp.where(row == 0, carry_ref[7:8, :], pltpu.roll(z, 1, 0))
    carry_ref[...] = z[tm - 8:tm, :]
    zrw_ref[...] = z + (prev - z) * mu_ref[...]
    zq_ref[...] = jnp.dot(h, w_ref[:, RW_IN:KV_OFF], preferred_element_type=F32)
    zkv_ref[...] = jnp.dot(h, w_ref[:, KV_OFF:GATE_OFF], preferred_element_type=F32)
    zmg_ref[...] = jnp.dot(h, w_ref[:, GATE_OFF:GATE_OFF + 2 * D_MODEL], preferred_element_type=F32)
    zgate_ref[...] = jnp.dot(h, w_ref[:, GATE_OFF + 2 * D_MODEL:], preferred_element_type=F32)


def _proj(x2d, g_mix, w_in, mu, seq):
    T = x2d.shape[0]
    tm = TM_PROJ
    wp = jnp.concatenate(
        [w_in[:, :GATE_OFF], w_in[:, MERGE_OFF:],
         jnp.pad(w_in[:, GATE_OFF:MERGE_OFF], ((0, 0), (0, LANE - N_GATE)))], axis=1).astype(BF16)
    npad = wp.shape[1]
    row = lambda i: (i, 0)
    fixed = lambda i: (0, 0)
    return pl.pallas_call(
        functools.partial(_proj_kernel, tiles_per_seq=seq // tm),
        grid=(T // tm,),
        in_specs=[pl.BlockSpec((tm, D_MODEL), row), pl.BlockSpec((1, D_MODEL), fixed),
                  pl.BlockSpec((D_MODEL, npad), fixed), pl.BlockSpec((1, RW_IN), fixed)],
        out_specs=[pl.BlockSpec((tm, RW_IN), row), pl.BlockSpec((tm, NSA_WIDTH), row),
                   pl.BlockSpec((tm, KV_IN), row), pl.BlockSpec((tm, 2 * D_MODEL), row),
                   pl.BlockSpec((tm, LANE), row)],
        out_shape=[jax.ShapeDtypeStruct((T, RW_IN), F32), jax.ShapeDtypeStruct((T, NSA_WIDTH), F32),
                   jax.ShapeDtypeStruct((T, KV_IN), F32), jax.ShapeDtypeStruct((T, 2 * D_MODEL), F32),
                   jax.ShapeDtypeStruct((T, LANE), F32)],
        scratch_shapes=[pltpu.VMEM((8, RW_IN), F32)],
        compiler_params=pltpu.CompilerParams(dimension_semantics=("arbitrary",),
                                             vmem_limit_bytes=VMEM_LIMIT),
        name="proj",
    )(x2d, g_mix.reshape(1, D_MODEL), wp, mu.reshape(1, RW_IN))


PAIR = 2 * RW_HEAD_DIM


def _pair_blocks(x):
    low = lax.broadcasted_iota(jnp.int32, (1, PAIR), 1) < RW_HEAD_DIM
    return jnp.concatenate([jnp.where(low, x, 0.0), jnp.where(low, 0.0, x)], axis=0)


def _pmm(a, b):
    return _bdot(a, _pair_blocks(b))


def _unit_lower_inverse(a_strict, sub_mask, eye):
    ad = [jnp.where(sub_mask, a, 0.0) for a in a_strict]
    ao = [a - d for a, d in zip(a_strict, ad)]
    td = [eye - d for d in ad]
    pw = ad
    for _ in range(int(math.log2(RW_SUB)) - 1):
        pw = [_pmm(x, x) for x in pw]
        td = [_pmm(t, eye + x) for t, x in zip(td, pw)]
    n = [_pmm(t, o) for t, o in zip(td, ao)]
    t = [eye - x for x in n]
    pw = n
    for _ in range(int(math.log2(RW_CHUNK // RW_SUB)) - 1):
        pw = [_pmm(x, x) for x in pw]
        t = [_pmm(a, eye + x) for a, x in zip(t, pw)]
    return [_pmm(a, d) for a, d in zip(t, td)]


def _rwkv_kernel(z_ref, w0_ref, w2_ref, a0_ref, a2_ref, g2_ref, kk_ref, ka_ref, rk_ref, lnw_ref,
                 lnb_ref, avg_ref, o_ref, h_ref):
    c = pl.program_id(1)
    C = RW_CHUNK
    n_pair = RW_WIDTH // PAIR
    nt = (((1,), (1,)), ((), ()))

    @pl.when(c == 0)
    def _():
        h_ref[...] = jnp.zeros_like(h_ref)

    ti = lax.broadcasted_iota(jnp.int32, (C, 1), 0)
    si = lax.broadcasted_iota(jnp.int32, (1, PAIR), 1) & (RW_HEAD_DIM - 1)
    incl, strict = ti >= si, ti > si
    eye = (ti == si).astype(F32)
    sub_shift = int(math.log2(RW_SUB))
    sub_mask = (ti >> sub_shift) == (si >> sub_shift)
    row2 = lax.broadcasted_iota(jnp.int32, (PAIR, 1), 0)
    col2 = lax.broadcasted_iota(jnp.int32, (1, PAIR), 1)
    same_head = (row2 < RW_HEAD_DIM) == (col2 < RW_HEAD_DIM)
    eye2 = row2 == col2
    tri = (lax.broadcasted_iota(jnp.int32, (C, C), 0)
           >= lax.broadcasted_iota(jnp.int32, (C, C), 1)).astype(BF16)

    def head_mean(x):
        xs = jnp.concatenate([x[:, p * PAIR:(p + 1) * PAIR] for p in range(n_pair)], axis=0)
        ms = _bdot(xs, avg_ref[...])
        return jnp.concatenate([ms[p * C:(p + 1) * C] for p in range(n_pair)], axis=1)

    n_rows = z_ref.shape[0]
    rows = []
    for n in range(n_rows):
        z = z_ref[n]
        zr, zk, zv = z[:, 0:512], z[:, 512:1024], z[:, 1024:1536]
        zw, za, zg = z[:, 1536:1600], z[:, 1600:1664], z[:, 1664:1792]
        w_raw = w0_ref[...] + _bdot(jnp.tanh(zw), w2_ref[...])
        logw = -jax.nn.sigmoid(w_raw) * math.exp(-0.5)
        a = jax.nn.sigmoid(a0_ref[...] + _bdot(za, a2_ref[...]))
        gate = _bdot(jax.nn.sigmoid(zg), g2_ref[...])
        kk = zk * kk_ref[...]
        kk = kk / jnp.maximum(jnp.sqrt(head_mean(kk * kk) * RW_HEAD_DIM), 1e-12)
        k = zk * (1.0 + (a - 1.0) * ka_ref[...])
        b = kk * a

        w_hi = logw.astype(BF16)
        w_lo = (logw - w_hi.astype(F32)).astype(BF16)
        cum = (jnp.dot(tri, w_hi, preferred_element_type=F32)
               + jnp.dot(tri, w_lo, preferred_element_type=F32))
        cum_last = cum[C - 1:C, :]
        g_inv = jnp.exp(-cum)
        g_end = jnp.exp(cum_last - cum)
        rows.append(dict(rt=zr * jnp.exp(cum), kt=k * g_inv, bt=b * g_inv, qt=kk * jnp.exp(cum - logw),
                         kh=k * g_end, bh=b * g_end, v=zv, g_last=jnp.exp(cum_last), gate=gate,
                         bonus=head_mean(zr * k * rk_ref[...]) * RW_HEAD_DIM * zv))

    chains = [(n, slice(p * PAIR, (p + 1) * PAIR)) for n in range(n_rows) for p in range(n_pair)]
    part = lambda name: [rows[n][name][:, sl] for n, sl in chains]
    qt, rt, kt, bt, kh, bh, v = (part(x) for x in ("qt", "rt", "kt", "bt", "kh", "bh", "v"))
    lhs = [jnp.concatenate([q, r], axis=0).astype(BF16) for q, r in zip(qt, rt)]
    ab = [lax.dot_general(l, _pair_blocks(x).astype(BF16), nt, preferred_element_type=F32)
          for l, x in zip(lhs, bt)]
    ak = [lax.dot_general(l, _pair_blocks(x).astype(BF16), nt, preferred_element_type=F32)
          for l, x in zip(lhs, kt)]
    a_kb = [jnp.where(strict, x[0:C], 0.0) for x in ab]
    a_rb = [jnp.where(incl, x[C:2 * C], 0.0) for x in ab]
    a_kk = [jnp.where(strict, x[0:C], 0.0) for x in ak]
    a_rk = [jnp.where(incl, x[C:2 * C], 0.0) for x in ak]
    t_inv = _unit_lower_inverse(a_kb, sub_mask, eye)

    h = [h_ref[n, sl.start // PAIR] for n, sl in chains]
    vb = [_pair_blocks(x) for x in v]
    rhs = [_bdot(jnp.concatenate([q, akk], axis=1), jnp.concatenate([hh, vv], axis=0))
           for q, akk, hh, vv in zip(qt, a_kk, h, vb)]
    u = [_pmm(t, x) for t, x in zip(t_inv, rhs)]
    outs = [_bdot(jnp.concatenate([r, ark, -arb], axis=1), jnp.concatenate([hh, vv, _pair_blocks(uu)], axis=0))
            for r, ark, arb, hh, vv, uu in zip(rt, a_rk, a_rb, h, vb, u)]
    upd = [_bdot(jnp.concatenate([x, -y], axis=0).T, jnp.concatenate([vv, uu], axis=0))
           for x, y, vv, uu in zip(kh, bh, v, u)]
    for (n, sl), hh, dd in zip(chains, h, upd):
        decay_col = jnp.sum(jnp.where(eye2, rows[n]["g_last"][:, sl], 0.0), axis=1, keepdims=True)
        h_ref[n, sl.start // PAIR] = decay_col * hh + jnp.where(same_head, dd, 0.0)

    for n in range(n_rows):
        o = jnp.concatenate(outs[n * n_pair:(n + 1) * n_pair], axis=1)
        d = o - head_mean(o)
        on = d * lax.rsqrt(head_mean(d * d) + GN_EPS)
        o_ref[n] = (on * lnw_ref[...] + lnb_ref[...] + rows[n]["bonus"]) * rows[n]["gate"]


def _rwkv(zrw, w0, w2, a0, a2, g2, k_k, k_a, r_k, ln_w, ln_b):
    B, S, _ = zrw.shape
    C = RW_CHUNK
    nb = RW_ROWS
    hid = np.arange(PAIR) // RW_HEAD_DIM
    avg = jnp.asarray((hid[:, None] == hid[None, :]).astype(np.float32) / RW_HEAD_DIM)
    vec = lambda a: a.reshape(1, RW_WIDTH)
    fixed = lambda shape: pl.BlockSpec(shape, lambda b, c: (0,) * len(shape))
    return pl.pallas_call(
        _rwkv_kernel,
        grid=(B // nb, S // C),
        in_specs=[pl.BlockSpec((nb, C, RW_IN), lambda b, c: (b, c, 0)),
                  fixed((1, RW_WIDTH)), fixed((DECAY_LORA, RW_WIDTH)),
                  fixed((1, RW_WIDTH)), fixed((AAA_LORA, RW_WIDTH)),
                  fixed((GATE_LORA, RW_WIDTH)), fixed((1, RW_WIDTH)), fixed((1, RW_WIDTH)),
                  fixed((1, RW_WIDTH)), fixed((1, RW_WIDTH)), fixed((1, RW_WIDTH)),
                  fixed((PAIR, PAIR))],
        out_specs=pl.BlockSpec((nb, C, RW_WIDTH), lambda b, c: (b, c, 0)),
        out_shape=jax.ShapeDtypeStruct((B, S, RW_WIDTH), F32),
        scratch_shapes=[pltpu.VMEM((nb, RW_WIDTH // PAIR, PAIR, PAIR), F32)],
        compiler_params=pltpu.CompilerParams(dimension_semantics=("parallel", "arbitrary"),
                                             vmem_limit_bytes=VMEM_LIMIT),
        name="rwkv",
    )(zrw, vec(w0), w2, vec(a0), a2, g2, vec(k_k), vec(k_a), vec(r_k), vec(ln_w), vec(ln_b), avg)


def _gelu_tanh(x):
    return 0.5 * x * (1.0 + jnp.tanh(math.sqrt(2.0 / math.pi) * (x + 0.044715 * (x * x * x))))


def _key_features(pos_hi, pos_lo, block, n, n_sel):
    lane = lax.broadcasted_iota(jnp.int32, (n, NSA_HEAD_DIM), 1)
    feat = jnp.where(lane == n_sel, pos_hi, jnp.where(lane == n_sel + 1, pos_lo, 0.0))
    return feat if block is None else jnp.where(lane == block, 1.0, feat)


def _nsa_prep_kernel(zkc_ref, zvc_ref, zks_ref, zvs_ref, zkw_ref, zvw_ref, pk_ref, pv_ref, kw1_ref,
                     kw2_ref, vw1_ref, vw2_ref, kc_ref, vct_ref, ksa_ref, kwa_ref, vst_ref, vwt_ref):
    S = zkc_ref.shape[1]
    n_grp = S // CMP_STRIDE
    Dh = NSA_HEAD_DIM
    half = CMP_BLOCK // 2
    n_sel = S // SEL_BLOCK
    jrow = lax.broadcasted_iota(jnp.int32, (n_grp, 1), 0)
    cmp_feat = _key_features((jrow >> 3).astype(F32),
                             ((jrow & 7) * CMP_STRIDE).astype(F32) + 0.5 * (CMP_BLOCK - 1),
                             None, n_grp, n_sel)
    for is_v, (z_ref, pos_ref, w1_ref, w2_ref) in enumerate(((zkc_ref, pk_ref, kw1_ref, kw2_ref),
                                                            (zvc_ref, pv_ref, vw1_ref, vw2_ref))):
        for g in range(NSA_KV_HEADS):
            lo = jnp.zeros((n_grp, Dh), F32)
            hi = jnp.zeros((n_grp, Dh), F32)
            for l in range(half):
                xs = z_ref[0, pl.ds(l, n_grp, stride=CMP_STRIDE), :]
                xg = xs[:, g * Dh:(g + 1) * Dh]
                lo = lo + _bdot(xg + pos_ref[l:l + 1, :], w1_ref[l * Dh:(l + 1) * Dh, :])
                hi = hi + _bdot(xg + pos_ref[half + l:half + l + 1, :],
                                w1_ref[(half + l) * Dh:(half + l + 1) * Dh, :])
            pre = lo + pltpu.roll(hi, n_grp - 1, 0)
            out = jnp.where(jrow < n_grp - 1, _bdot(_gelu_tanh(pre), w2_ref[...]), 0.0)
            if is_v:
                out_t = jnp.concatenate([out, jnp.zeros_like(out)], axis=1).T
                vct_ref[0, g] = out_t[0:Dh, :].astype(BF16)
            else:
                kc_ref[0, g] = jnp.concatenate([out, cmp_feat], axis=1)

    prow = lax.broadcasted_iota(jnp.int32, (S, 1), 0)
    p_hi, p_lo = (prow >> 7).astype(F32), (prow & (LANE - 1)).astype(F32)
    for z_ref, out_ref, block in ((zks_ref, ksa_ref, prow >> int(math.log2(SEL_BLOCK))),
                                  (zkw_ref, kwa_ref, None)):
        kfull = z_ref[0]
        key_feat = _key_features(p_hi, p_lo, block, S, n_sel)
        for g in range(NSA_KV_HEADS):
            out_ref[0, g] = jnp.concatenate([kfull[:, g * Dh:(g + 1) * Dh], key_feat], axis=1).astype(BF16)
    for z_ref, out_ref in ((zvs_ref, vst_ref), (zvw_ref, vwt_ref)):
        for j in range(S // TK):
            out_ref[0, j] = z_ref[0, j * TK:(j + 1) * TK, :].T.astype(BF16)


def _nsa_prep(zkv, pos_k, pos_v, kw1, kw2, vw1, vw2):
    B, S, _ = zkv.shape
    n_grp = S // CMP_STRIDE
    n_kt = S // TK
    Dh = NSA_HEAD_DIM
    G = NSA_KV_HEADS
    fixed = lambda shape: pl.BlockSpec(shape, lambda b: (0,) * len(shape))
    col = lambda c: pl.BlockSpec((1, S, NSA_KV_WIDTH), lambda b: (b, 0, c))
    whole = lambda shape: pl.BlockSpec((1,) + shape, lambda b: (b,) + (0,) * len(shape))
    shapes = [((G, n_grp, 2 * Dh), F32), ((G, Dh, n_grp), BF16), ((G, S, 2 * Dh), BF16),
              ((G, S, 2 * Dh), BF16), ((n_kt, G * Dh, TK), BF16), ((n_kt, G * Dh, TK), BF16)]
    return pl.pallas_call(
        _nsa_prep_kernel,
        grid=(B,),
        in_specs=[col(c) for c in range(6)] + [
            fixed((CMP_BLOCK, Dh)), fixed((CMP_BLOCK, Dh)),
            fixed((CMP_BLOCK * Dh, Dh)), fixed((Dh, Dh)),
            fixed((CMP_BLOCK * Dh, Dh)), fixed((Dh, Dh))],
        out_specs=[whole(s) for s, _ in shapes],
        out_shape=[jax.ShapeDtypeStruct((B,) + s, d) for s, d in shapes],
        compiler_params=pltpu.CompilerParams(dimension_semantics=("parallel",),
                                             vmem_limit_bytes=VMEM_LIMIT),
        name="nsa_prep",
    )(zkv, zkv, zkv, zkv, zkv, zkv, pos_k, pos_v, kw1, kw2, vw1, vw2)


def _nsa_attn_kernel(q_ref, kc_ref, vct_ref, ksa_ref, kwa_ref, vst_ref, vwt_ref, gl_ref, gb_ref,
                     ovlt_ref, slope_ref, o_ref, acc_ref, ot_ref):
    i = pl.program_id(1)
    Dh = NSA_HEAD_DIM
    R = NSA_REP
    N = R * TQ
    n_cmp_pad = kc_ref.shape[2]
    n_sel = ovlt_ref.shape[0]
    G = NSA_KV_HEADS
    nt = (((1,), (1,)), ((), ()))
    log2e = math.log2(math.e)
    t0 = i * TQ
    t_row = t0 + lax.broadcasted_iota(jnp.int32, (1, TQ), 1)
    c_col = lax.broadcasted_iota(jnp.int32, (TK, 1), 0)
    sgate_t = jax.nn.sigmoid(gl_ref[0] + gb_ref[...]).T
    lane_f = lax.broadcasted_iota(jnp.int32, (1, Dh), 1)
    heads = lambda x: jnp.concatenate([x] * R, axis=1)

    def key_dist(j):
        return t_row - (j * TK + c_col)

    def queries(g, sel_feat):
        parts = []
        for r in range(R):
            h = g * R + r
            sl = slope_ref[:, h:h + 1] * log2e
            feat = jnp.where(lane_f == n_sel, sl * LANE, jnp.where(lane_f == n_sel + 1, sl, sel_feat))
            parts.append(jnp.concatenate(
                [q_ref[0, :, h * Dh:(h + 1) * Dh] * (Dh ** -0.5 * log2e), jnp.broadcast_to(feat, (TQ, Dh))],
                axis=1))
        return jnp.concatenate(parts, axis=0)

    qab, o_cmp = [], []
    for g in range(G):
        jc = lax.broadcasted_iota(jnp.int32, (n_cmp_pad, 1), 0)
        ok_c = (jc * CMP_STRIDE + (CMP_BLOCK - 1) <= t_row) & (jc < n_cmp_pad - 1)
        s_c = (lax.dot_general(kc_ref[0, g], queries(g, 0.0), nt, precision=HI, preferred_element_type=F32)
               + heads(jnp.where(ok_c, 0.0, NEG_INF)))
        e_c = jnp.exp2(s_c - jnp.max(s_c, axis=0, keepdims=True))
        any_c = heads(t_row >= CMP_BLOCK - 1)
        p_c = e_c * jnp.where(any_c, 1.0 / jnp.sum(e_c, axis=0, keepdims=True), 0.0)
        o_cmp.append(jnp.dot(vct_ref[0, g], p_c.astype(BF16), preferred_element_type=F32))

        p_sum = p_c[:, 0:TQ]
        for r in range(1, R):
            p_sum = p_sum + p_c[:, r * TQ:(r + 1) * TQ]
        imp = jnp.dot(ovlt_ref[...], p_sum, precision=HI, preferred_element_type=F32)
        kb = lax.broadcasted_iota(jnp.int32, (n_sel, 1), 0)
        kbf = kb.astype(F32)
        blk_t = t_row >> int(math.log2(SEL_BLOCK))
        forced = (kb == 0) | (kb == blk_t) | (kb == blk_t - 1)
        cur = jnp.where(forced, FORCE_SCORE, jnp.where(kb <= blk_t, imp, -FORCE_SCORE))
        sel_bias = jnp.full((n_sel, TQ), NEG_INF, F32)
        for _ in range(min(N_SELECT, n_sel)):
            mx = jnp.max(cur, axis=0, keepdims=True)
            first = jnp.min(jnp.where(cur == mx, kbf, float(n_sel)), axis=0, keepdims=True)
            hit = kbf == first
            sel_bias = jnp.where(hit, 0.0, sel_bias)
            cur = jnp.where(hit, -3e38, cur)
        sel_feat = jnp.concatenate([sel_bias, jnp.zeros((LANE - n_sel, TQ), F32)], axis=0).T[:, 0:Dh]
        qab.append(queries(g, sel_feat).astype(BF16))

    def tile(j, carry, k_ref, vt_ref, bias, slot):
        out = []
        for g in range(G):
            m, l = carry[g]
            k = k_ref[0, g, pl.ds(pl.multiple_of(j * TK, TK), TK), :]
            s = lax.dot_general(k, qab[g], nt, preferred_element_type=F32)
            if bias is not None:
                s = s + heads(bias)
            m_new = jnp.maximum(m, jnp.max(s, axis=0, keepdims=True))
            alpha = jnp.exp2(m - m_new)
            p = jnp.exp2(s - m_new)
            l = alpha * l + jnp.sum(p, axis=0, keepdims=True)
            vt = vt_ref[0, j, g * Dh:(g + 1) * Dh, :]
            acc_ref[slot, g] = (alpha * acc_ref[slot, g]
                                + jnp.dot(vt, p.astype(BF16), preferred_element_type=F32))
            out.append((m_new, l))
        return tuple(out)

    def window_bias(j):
        d = key_dist(j)
        return jnp.where((d >= 0) & (d < WINDOW), 0.0, NEG_INF)

    init = ((jnp.full((1, N), NEG_INF, F32), jnp.zeros((1, N), F32)),) * G
    acc_ref[...] = jnp.zeros_like(acc_ref)
    causal = jnp.where(key_dist(i) >= 0, 0.0, NEG_INF)
    carry = lax.fori_loop(0, i, lambda j, c: tile(j, c, ksa_ref, vst_ref, None, 0), init)
    stat_s = tile(i, carry, ksa_ref, vst_ref, causal, 0)
    carry = lax.fori_loop(jnp.maximum(i - WINDOW // TK, 0), i,
                          lambda j, c: tile(j, c, kwa_ref, vwt_ref, window_bias(j), 1), init)
    stat_w = tile(i, carry, kwa_ref, vwt_ref, causal, 1)

    for g in range(G):
        o_sel = acc_ref[0, g] * (1.0 / stat_s[g][1])
        o_win = acc_ref[1, g] * (1.0 / stat_w[g][1])
        for r in range(R):
            h = g * R + r
            cols = slice(r * TQ, (r + 1) * TQ)
            ot_ref[h * Dh:(h + 1) * Dh, :] = (
                sgate_t[3 * h:3 * h + 1, :] * o_cmp[g][:, cols]
                + sgate_t[3 * h + 1:3 * h + 2, :] * o_sel[:, cols]
                + sgate_t[3 * h + 2:3 * h + 3, :] * o_win[:, cols])
    o_ref[0] = ot_ref[...].T


def _nsa_attn(zq, kc, vct, ksa, kwa, vst, vwt, zgate, gate_b):
    B, S, _ = zq.shape
    n_sel = S // SEL_BLOCK
    n_cmp = (S - CMP_BLOCK) // CMP_STRIDE + 1
    n_cmp_pad = kc.shape[2]
    n_kt = S // TK
    G, Dh = NSA_KV_HEADS, NSA_HEAD_DIM
    cmp_start = np.arange(n_cmp) * CMP_STRIDE
    sel_start = np.arange(n_sel) * SEL_BLOCK
    overlap = np.clip(np.minimum(cmp_start[:, None] + CMP_BLOCK, sel_start[None, :] + SEL_BLOCK)
                      - np.maximum(cmp_start[:, None], sel_start[None, :]), 0, None) / CMP_BLOCK
    ovlt = np.zeros((n_sel, n_cmp_pad), np.float32)
    ovlt[:, :n_cmp] = overlap.T
    slopes = (2.0 ** (-8.0 * np.arange(1, NSA_HEADS + 1) / NSA_HEADS)).astype(np.float32).reshape(1, NSA_HEADS)
    gb = jnp.pad(gate_b, (0, LANE - N_GATE)).reshape(1, LANE)
    fixed = lambda shape: pl.BlockSpec(shape, lambda b, i: (0,) * len(shape))
    per_b = lambda shape: pl.BlockSpec((1,) + shape, lambda b, i: (b,) + (0,) * len(shape))
    return pl.pallas_call(
        _nsa_attn_kernel,
        grid=(B, S // TQ),
        in_specs=[pl.BlockSpec((1, TQ, NSA_WIDTH), lambda b, i: (b, i, 0)),
                  per_b((G, n_cmp_pad, 2 * Dh)), per_b((G, Dh, n_cmp_pad)),
                  per_b((G, S, 2 * Dh)), per_b((G, S, 2 * Dh)),
                  per_b((n_kt, G * Dh, TK)), per_b((n_kt, G * Dh, TK)),
                  pl.BlockSpec((1, TQ, LANE), lambda b, i: (b, i, 0)),
                  fixed((1, LANE)), fixed((n_sel, n_cmp_pad)), fixed((1, NSA_HEADS))],
        out_specs=pl.BlockSpec((1, TQ, NSA_WIDTH), lambda b, i: (b, i, 0)),
        out_shape=jax.ShapeDtypeStruct((B, S, NSA_WIDTH), F32),
        scratch_shapes=[pltpu.VMEM((2, G, Dh, NSA_REP * TQ), F32),
                        pltpu.VMEM((NSA_WIDTH, TQ), F32)],
        compiler_params=pltpu.CompilerParams(dimension_semantics=("parallel", "arbitrary"),
                                             vmem_limit_bytes=VMEM_LIMIT),
        name="nsa_attn",
    )(zq, kc, vct, ksa, kwa, vst, vwt, zgate, gb, jnp.asarray(ovlt), jnp.asarray(slopes))


def _mix_kernel(x_ref, yr_ref, yn_ref, zmg_ref, ur_ref, un_ref, wo_ref, gf_ref, wr_ref, br_ref,
                x1_ref, h2_ref, rt_ref, cnt_ref):
    tm = x_ref.shape[0]
    zmg = zmg_ref[...]
    mixed = (jax.nn.sigmoid(zmg[:, :D_MODEL]) * _bdot(yr_ref[...], ur_ref[...])
             + jax.nn.sigmoid(zmg[:, D_MODEL:]) * _bdot(yn_ref[...], un_ref[...]))
    x1 = x_ref[...] + _bdot(mixed, wo_ref[...])
    x1_ref[...] = x1
    h2 = _rms(x1, gf_ref[...])
    h2_ref[...] = h2
    logits = jnp.dot(h2, wr_ref[...], precision=HI, preferred_element_type=F32) + br_ref[...]
    lane = lax.broadcasted_iota(jnp.int32, (1, LANE), 1).astype(F32)
    gl = jnp.where(lane < N_GROUPS, logits, NEG_INF)
    gmax = jnp.max(gl, axis=-1, keepdims=True)
    g_sel = jnp.min(jnp.where(gl == gmax, lane, float(LANE)), axis=-1, keepdims=True)
    p_group = 1.0 / jnp.sum(jnp.exp(gl - gmax), axis=-1, keepdims=True)
    e_lane = lane - N_GROUPS
    in_grp = (e_lane >= g_sel * EXPERTS_PER_GROUP) & (e_lane < (g_sel + 1.0) * EXPERTS_PER_GROUP)
    el = jnp.where(in_grp, logits, NEG_INF)
    m1 = jnp.max(el, axis=-1, keepdims=True)
    i1 = jnp.min(jnp.where(el == m1, e_lane, float(LANE)), axis=-1, keepdims=True)
    el2 = jnp.where(e_lane == i1, 2.0 * NEG_INF, el)
    m2 = jnp.max(el2, axis=-1, keepdims=True)
    i2 = jnp.min(jnp.where(el2 == m2, e_lane, float(LANE)), axis=-1, keepdims=True)
    r2 = jnp.exp(m2 - m1)
    g1 = p_group / (1.0 + r2)
    g2 = p_group * r2 / (1.0 + r2)

    @pl.when(pl.program_id(0) == 0)
    def _():
        cnt_ref[...] = jnp.zeros_like(cnt_ref)

    pick1, pick2 = e_lane == i1, e_lane == i2
    both = pick1.astype(F32) + pick2.astype(F32)
    earlier = (lax.broadcasted_iota(jnp.int32, (tm, tm), 1)
               < lax.broadcasted_iota(jnp.int32, (tm, tm), 0)).astype(BF16)
    before = jnp.dot(earlier, both.astype(BF16), preferred_element_type=F32) + cnt_ref[0:1, :]
    r1 = jnp.sum(jnp.where(pick1, before, 0.0), axis=-1, keepdims=True)
    r2 = jnp.sum(jnp.where(pick2, before, 0.0), axis=-1, keepdims=True)
    cnt_ref[...] = cnt_ref[...] + jnp.sum(both, axis=0, keepdims=True)
    fields = (i1, i2, g1, g2, r1, r2)
    out = jnp.zeros((tm, LANE), F32)
    for f, val in enumerate(fields):
        out = jnp.where(lane == f, val, out)
    rt_ref[...] = out


def _mix(x2d, y_rw, y_nsa, zmg, w_up_r, w_up_n, w_out, g_ffn, w_group, b_group, w_router, b_router):
    T = x2d.shape[0]
    tm = TM_PROJ
    n_r = N_GROUPS + N_EXPERTS
    wr = jnp.pad(jnp.concatenate([w_group, w_router], axis=1), ((0, 0), (0, LANE - n_r)))
    br = jnp.pad(jnp.concatenate([b_group, b_router]), (0, LANE - n_r)).reshape(1, LANE)
    row = lambda i: (i, 0)
    fixed = lambda i: (0, 0)
    return pl.pallas_call(
        _mix_kernel,
        grid=(T // tm,),
        in_specs=[pl.BlockSpec((tm, D_MODEL), row), pl.BlockSpec((tm, RW_WIDTH), row),
                  pl.BlockSpec((tm, NSA_WIDTH), row), pl.BlockSpec((tm, 2 * D_MODEL), row),
                  pl.BlockSpec((RW_WIDTH, D_MODEL), fixed), pl.BlockSpec((NSA_WIDTH, D_MODEL), fixed),
                  pl.BlockSpec((D_MODEL, D_MODEL), fixed), pl.BlockSpec((1, D_MODEL), fixed),
                  pl.BlockSpec((D_MODEL, LANE), fixed), pl.BlockSpec((1, LANE), fixed)],
        out_specs=[pl.BlockSpec((tm, D_MODEL), row), pl.BlockSpec((tm, D_MODEL), row),
                   pl.BlockSpec((tm, LANE), row), pl.BlockSpec((8, LANE), fixed)],
        out_shape=[jax.ShapeDtypeStruct((T, D_MODEL), F32), jax.ShapeDtypeStruct((T, D_MODEL), F32),
                   jax.ShapeDtypeStruct((T, LANE), F32), jax.ShapeDtypeStruct((8, LANE), F32)],
        compiler_params=pltpu.CompilerParams(dimension_semantics=("arbitrary",),
                                             vmem_limit_bytes=VMEM_LIMIT),
        name="mix",
    )(x2d, y_rw, y_nsa, zmg, w_up_r.astype(BF16), w_up_n.astype(BF16), w_out.astype(BF16),
      g_ffn.reshape(1, D_MODEL), wr, br)


def _route_tables(rt, cnt, T):
    n_rows = T * TOP_K + N_EXPERTS * ROW_BLOCK
    n_blk = n_rows // ROW_BLOCK
    counts = cnt[0, N_GROUPS:N_GROUPS + N_EXPERTS].astype(jnp.int32)
    padded = (counts + ROW_BLOCK - 1) // ROW_BLOCK * ROW_BLOCK
    pends = jnp.cumsum(padded)
    pstarts = pends - padded
    expert = rt[:, 0:TOP_K].astype(jnp.int32)
    rank = rt[:, 2 * TOP_K:3 * TOP_K].astype(jnp.int32)
    seg_start = jnp.sum(jnp.where(expert[..., None] == jnp.arange(N_EXPERTS), pstarts, 0), axis=-1)
    dest = (seg_start + rank).reshape(T * TOP_K)
    blk_start = jnp.arange(n_blk) * ROW_BLOCK
    blk_expert = jnp.minimum(jnp.sum(pends[None, :] <= blk_start[:, None], axis=1), N_EXPERTS - 1)
    blk_nv = jnp.clip(counts[blk_expert] - (blk_start - pstarts[blk_expert]), 0, ROW_BLOCK)
    return dest.astype(jnp.int32), blk_expert.astype(jnp.int32), blk_nv.astype(jnp.int32)


def _row_copies(index_ref, base, n, make):
    for t in range(n):
        for k in range(TOP_K):
            make(base + t, k, index_ref[(base + t) * TOP_K + k]).start()


def _dispatch_kernel(dest_ref, h2_ref, xs_init_hbm, xs_hbm, sem):
    del xs_init_hbm
    i = pl.program_id(0)
    tm = h2_ref.shape[0]

    def row_copy(tok, k, row):
        return pltpu.make_async_copy(h2_ref.at[pl.ds(tok - i * tm, 1)], xs_hbm.at[pl.ds(row, 1)], sem)

    _row_copies(dest_ref, i * tm, tm, row_copy)
    for _ in range(TOP_K):
        pltpu.make_async_copy(h2_ref, xs_hbm.at[pl.ds(0, tm)], sem).wait()


def _dispatch(h2, dest):
    T = h2.shape[0]
    n_rows = T * TOP_K + N_EXPERTS * ROW_BLOCK
    grid_spec = pltpu.PrefetchScalarGridSpec(
        num_scalar_prefetch=1,
        grid=(T // TM_PROJ,),
        in_specs=[pl.BlockSpec((TM_PROJ, D_MODEL), lambda i, d: (i, 0)), pl.BlockSpec(memory_space=pl.ANY)],
        out_specs=pl.BlockSpec(memory_space=pl.ANY),
        scratch_shapes=[pltpu.SemaphoreType.DMA(())])
    return pl.pallas_call(
        _dispatch_kernel,
        grid_spec=grid_spec,
        out_shape=jax.ShapeDtypeStruct((n_rows, D_MODEL), F32),
        input_output_aliases={2: 0},
        compiler_params=pltpu.CompilerParams(dimension_semantics=("arbitrary",),
                                             vmem_limit_bytes=VMEM_LIMIT),
        name="dispatch",
    )(dest, h2, jnp.zeros((n_rows, D_MODEL), F32))


def _expert_kernel(be_ref, nv_ref, xs_ref, wgu_ref, wd_ref, o_ref, wgu_b, wd_b):
    i = pl.program_id(0)
    nv = nv_ref[i]
    changed = jnp.logical_or(i == 0, be_ref[i] != be_ref[jnp.maximum(i - 1, 0)])

    @pl.when(jnp.logical_and(nv > 0, changed))
    def _():
        wgu_b[...] = wgu_ref[0].astype(BF16)
        wd_b[...] = wd_ref[0].astype(BF16)

    @pl.when(nv > 0)
    def _():
        gu = jnp.dot(xs_ref[...].astype(BF16), wgu_b[...], preferred_element_type=F32)
        gate_h, up_h = gu[:, :D_EXPERT], gu[:, D_EXPERT:]
        mid = gate_h * jax.nn.sigmoid(gate_h) * up_h
        o_ref[...] = jnp.dot(mid.astype(BF16), wd_b[...], preferred_element_type=F32)

    @pl.when(nv == 0)
    def _():
        o_ref[...] = jnp.zeros_like(o_ref)


def _experts(xs, blk_expert, blk_nv, w_gate_up, w_down):
    n_rows = xs.shape[0]
    grid_spec = pltpu.PrefetchScalarGridSpec(
        num_scalar_prefetch=2,
        grid=(n_rows // ROW_BLOCK,),
        in_specs=[pl.BlockSpec((ROW_BLOCK, D_MODEL), lambda i, be, nv: (i, 0)),
                  pl.BlockSpec((1, D_MODEL, 2 * D_EXPERT), lambda i, be, nv: (be[i], 0, 0)),
                  pl.BlockSpec((1, D_EXPERT, D_MODEL), lambda i, be, nv: (be[i], 0, 0))],
        out_specs=pl.BlockSpec((ROW_BLOCK, D_MODEL), lambda i, be, nv: (i, 0)),
        scratch_shapes=[pltpu.VMEM((D_MODEL, 2 * D_EXPERT), BF16), pltpu.VMEM((D_EXPERT, D_MODEL), BF16)])
    return pl.pallas_call(
        _expert_kernel,
        grid_spec=grid_spec,
        out_shape=jax.ShapeDtypeStruct((n_rows, D_MODEL), F32),
        compiler_params=pltpu.CompilerParams(dimension_semantics=("arbitrary",),
                                             vmem_limit_bytes=VMEM_LIMIT),
        name="experts",
    )(blk_expert, blk_nv, xs, w_gate_up, w_down)


def _ple_kernel(dest_ref, x1_ref, rt_ref, p_ref, rows_hbm, wpp_ref, gpn_ref, ggi_ref, wpg_ref, gfin_ref,
                y_ref, ebuf, sem, *, last_layer):
    i = pl.program_id(0)
    tm = x1_ref.shape[0]
    slot = i % 2

    def fetch(tile, s):
        def row_copy(tok, k, row):
            return pltpu.make_async_copy(rows_hbm.at[pl.ds(row, 1)],
                                         ebuf.at[s, k, pl.ds(tok - tile * tm, 1)], sem.at[s])
        _row_copies(dest_ref, tile * tm, tm, row_copy)

    @pl.when(i == 0)
    def _():
        fetch(0, 0)

    @pl.when(i + 1 < pl.num_programs(0))
    def _():
        fetch(i + 1, 1 - slot)

    for k in range(TOP_K):
        pltpu.make_async_copy(rows_hbm.at[pl.ds(0, tm)], ebuf.at[slot, k], sem.at[slot]).wait()
    moe = rt_ref[:, TOP_K:TOP_K + 1] * ebuf[slot, 0]
    for k in range(1, TOP_K):
        moe = moe + rt_ref[:, TOP_K + k:TOP_K + k + 1] * ebuf[slot, k]
    x2 = x1_ref[...] + moe
    e = _rms(_bdot(p_ref[...], wpp_ref[...]), gpn_ref[...])
    gate = jax.nn.sigmoid(_bdot(_rms(x2, ggi_ref[...]), wpg_ref[...]))
    x3 = x2 + gate * e
    y_ref[...] = _rms(x3, gfin_ref[...]) if last_layer else x3


def _ple(x1, rt, dest, expert_rows, p2d, w_pp, g_pn, g_gi, w_pg, g_final, last_layer):
    T = x1.shape[0]
    tm = TM_PROJ
    row = lambda i, d: (i, 0)
    fixed = lambda i, d: (0, 0)
    vec = lambda a: a.reshape(1, D_MODEL)
    grid_spec = pltpu.PrefetchScalarGridSpec(
        num_scalar_prefetch=1,
        grid=(T // tm,),
        in_specs=[pl.BlockSpec((tm, D_MODEL), row), pl.BlockSpec((tm, LANE), row),
                  pl.BlockSpec((tm, PLE_DIM), row), pl.BlockSpec(memory_space=pl.ANY),
                  pl.BlockSpec((PLE_DIM, D_MODEL), fixed), pl.BlockSpec((1, D_MODEL), fixed),
                  pl.BlockSpec((1, D_MODEL), fixed), pl.BlockSpec((D_MODEL, D_MODEL), fixed),
                  pl.BlockSpec((1, D_MODEL), fixed)],
        out_specs=pl.BlockSpec((tm, D_MODEL), row),
        scratch_shapes=[pltpu.VMEM((2, TOP_K, tm, D_MODEL), F32), pltpu.SemaphoreType.DMA((2,))])
    return pl.pallas_call(
        functools.partial(_ple_kernel, last_layer=last_layer),
        grid_spec=grid_spec,
        out_shape=jax.ShapeDtypeStruct((T, D_MODEL), F32),
        compiler_params=pltpu.CompilerParams(dimension_semantics=("arbitrary",),
                                             vmem_limit_bytes=VMEM_LIMIT),
        name="ple",
    )(dest, x1, rt, p2d, expert_rows, w_pp.astype(BF16), vec(g_pn), vec(g_gi), w_pg.astype(BF16),
      vec(g_final))


def kernel(x, p, g_mix, w_in, mu_shift, rw_w0, rw_w2, rw_a0, rw_a2, rw_g2, rw_k_k, rw_k_a, rw_r_k, rw_ln_w, rw_ln_b, cmp_pos_k, cmp_pos_v, cmp_k_w1, cmp_k_w2, cmp_v_w1, cmp_v_w2, nsa_gate_b, w_up_rwkv, w_up_nsa, w_out, g_ffn, w_group, b_group, w_router, b_router, w_exp_gate_up, w_exp_down, w_ple_proj, g_ple_norm, g_ple_gate_in, w_ple_gate, g_final):
    B, S, D = x.shape
    T = B * S
    depth = p.shape[0]
    xc = x.reshape(T, D)
    for i in range(depth):
        zrw, zq, zkv, zmg, zgate = _proj(xc, g_mix[i], w_in[i], mu_shift[i], S)
        y_rw = _rwkv(zrw.reshape(B, S, RW_IN), rw_w0[i], rw_w2[i], rw_a0[i], rw_a2[i], rw_g2[i],
                     rw_k_k[i], rw_k_a[i], rw_r_k[i], rw_ln_w[i], rw_ln_b[i])
        zkv3 = zkv.reshape(B, S, KV_IN)
        nsa_kv = _nsa_prep(zkv3, cmp_pos_k[i], cmp_pos_v[i], cmp_k_w1[i], cmp_k_w2[i],
                           cmp_v_w1[i], cmp_v_w2[i])
        y_nsa = _nsa_attn(zq.reshape(B, S, NSA_WIDTH), *nsa_kv, zgate.reshape(B, S, LANE),
                          nsa_gate_b[i])
        x1, h2, rt, cnt = _mix(xc, y_rw.reshape(T, RW_WIDTH), y_nsa.reshape(T, NSA_WIDTH), zmg,
                               w_up_rwkv[i], w_up_nsa[i], w_out[i], g_ffn[i], w_group[i], b_group[i],
                               w_router[i], b_router[i])
        dest, blk_expert, blk_nv = _route_tables(rt, cnt, T)
        expert_rows = _experts(_dispatch(h2, dest), blk_expert, blk_nv, w_exp_gate_up[i], w_exp_down[i])
        xc = _ple(x1, rt, dest, expert_rows, p[i].reshape(T, PLE_DIM), w_ple_proj[i], g_ple_norm[i],
                  g_ple_gate_in[i], w_ple_gate[i], g_final, i == depth - 1)
    return xc.reshape(B, S, D)
```

```python
import functools
import math

import jax
import jax.numpy as jnp
import numpy as np
from jax import lax
from jax.experimental import pallas as pl
from jax.experimental.pallas import tpu as pltpu

F32 = jnp.float32
BF16 = jnp.bfloat16
HI = lax.Precision.HIGHEST

D_MODEL = 1024
RW_HEADS = 8
RW_HEAD_DIM = 64
RW_WIDTH = 512
DECAY_LORA = 64
AAA_LORA = 64
GATE_LORA = 128
GN_EPS = 64e-5
RW_IN = 3 * RW_WIDTH + DECAY_LORA + AAA_LORA + GATE_LORA

NSA_HEADS = 8
NSA_KV_HEADS = 2
NSA_REP = NSA_HEADS // NSA_KV_HEADS
NSA_HEAD_DIM = 64
NSA_WIDTH = 512
NSA_KV_WIDTH = 128
CMP_BLOCK = 32
CMP_STRIDE = 16
SEL_BLOCK = 64
N_SELECT = 8
WINDOW = 512
N_NSA_BRANCH = 3
FORCE_SCORE = 1e6
NEG_INF = -1e30

N_GROUPS = 4
EXPERTS_PER_GROUP = 8
N_EXPERTS = 32
TOP_K = 2
D_EXPERT = 512
ROW_BLOCK = 256
PLE_DIM = 256
NORM_EPS = 1e-6

N_GATE = N_NSA_BRANCH * NSA_HEADS
ATT_IN = NSA_WIDTH + 6 * NSA_KV_WIDTH
KV_OFF = RW_IN + NSA_WIDTH
KV_IN = 6 * NSA_KV_WIDTH
GATE_OFF = RW_IN + ATT_IN
MERGE_OFF = GATE_OFF + N_GATE
LANE = 128

RW_CHUNK = 64
RW_SUB = 16
RW_ROWS = 4
TQ = 256
TK = 256
TM_PROJ = 256
VMEM_LIMIT = 56 * 1024 * 1024


def _bdot(a, b):
    return jnp.dot(a.astype(BF16), b.astype(BF16), preferred_element_type=F32)


def _bdot_nt(a, b):
    return lax.dot_general(a.astype(BF16), b.astype(BF16), (((1,), (1,)), ((), ())),
                           preferred_element_type=F32)


def _rms(x, g):
    return x * lax.rsqrt(jnp.mean(x * x, axis=-1, keepdims=True) + NORM_EPS) * g


def _proj_kernel(x_ref, g_ref, w_ref, mu_ref, zrw_ref, zq_ref, zkv_ref, zmg_ref, zgate_ref,
                 carry_ref, *, tiles_per_seq):
    i = pl.program_id(0)
    tm = x_ref.shape[0]

    @pl.when(i % tiles_per_seq == 0)
    def _():
        carry_ref[...] = jnp.zeros_like(carry_ref)

    h = _rms(x_ref[...], g_ref[...]).astype(BF16)
    z = jnp.dot(h, w_ref[:, 0:RW_IN], preferred_element_type=F32)
    row = lax.broadcasted_iota(jnp.int32, (tm, 1), 0)
    prev = jnp.where(row == 0, carry_ref[7:8, :], pltpu.roll(z, 1, 0))
    carry_ref[...] = z[tm - 8:tm, :]
    zrw_ref[...] = z + (prev - z) * mu_ref[...]
    zq_ref[...] = jnp.dot(h, w_ref[:, RW_IN:KV_OFF], preferred_element_type=F32)
    zkv_ref[...] = jnp.dot(h, w_ref[:, KV_OFF:GATE_OFF], preferred_element_type=F32)
    zmg_ref[...] = jnp.dot(h, w_ref[:, GATE_OFF:GATE_OFF + 2 * D_MODEL], preferred_element_type=F32)
    zgate_ref[...] = jnp.dot(h, w_ref[:, GATE_OFF + 2 * D_MODEL:], preferred_element_type=F32)


def _proj(x2d, g_mix, w_in, mu, seq):
    T = x2d.shape[0]
    tm = TM_PROJ
    wp = jnp.concatenate(
        [w_in[:, :GATE_OFF], w_in[:, MERGE_OFF:],
         jnp.pad(w_in[:, GATE_OFF:MERGE_OFF], ((0, 0), (0, LANE - N_GATE)))], axis=1).astype(BF16)
    npad = wp.shape[1]
    row = lambda i: (i, 0)
    fixed = lambda i: (0, 0)
    return pl.pallas_call(
        functools.partial(_proj_kernel, tiles_per_seq=seq // tm),
        grid=(T // tm,),
        in_specs=[pl.BlockSpec((tm, D_MODEL), row), pl.BlockSpec((1, D_MODEL), fixed),
                  pl.BlockSpec((D_MODEL, npad), fixed), pl.BlockSpec((1, RW_IN), fixed)],
        out_specs=[pl.BlockSpec((tm, RW_IN), row), pl.BlockSpec((tm, NSA_WIDTH), row),
                   pl.BlockSpec((tm, KV_IN), row), pl.BlockSpec((tm, 2 * D_MODEL), row),
                   pl.BlockSpec((tm, LANE), row)],
        out_shape=[jax.ShapeDtypeStruct((T, RW_IN), F32), jax.ShapeDtypeStruct((T, NSA_WIDTH), F32),
                   jax.ShapeDtypeStruct((T, KV_IN), F32), jax.ShapeDtypeStruct((T, 2 * D_MODEL), F32),
                   jax.ShapeDtypeStruct((T, LANE), F32)],
        scratch_shapes=[pltpu.VMEM((8, RW_IN), F32)],
        compiler_params=pltpu.CompilerParams(dimension_semantics=("arbitrary",),
                                             vmem_limit_bytes=VMEM_LIMIT),
        name="proj",
    )(x2d, g_mix.reshape(1, D_MODEL), wp, mu.reshape(1, RW_IN))


PAIR = 2 * RW_HEAD_DIM


def _pair_blocks(x):
    low = lax.broadcasted_iota(jnp.int32, (1, PAIR), 1) < RW_HEAD_DIM
    return jnp.concatenate([jnp.where(low, x, 0.0), jnp.where(low, 0.0, x)], axis=0)


def _pmm(a, b):
    return _bdot(a, _pair_blocks(b))


def _unit_lower_inverse(a_strict, sub_mask, eye):
    ad = [jnp.where(sub_mask, a, 0.0) for a in a_strict]
    ao = [a - d for a, d in zip(a_strict, ad)]
    td = [eye - d for d in ad]
    pw = ad
    for _ in range(int(math.log2(RW_SUB)) - 1):
        pw = [_pmm(x, x) for x in pw]
        td = [_pmm(t, eye + x) for t, x in zip(td, pw)]
    n = [_pmm(t, o) for t, o in zip(td, ao)]
    t = [eye - x for x in n]
    pw = n
    for _ in range(int(math.log2(RW_CHUNK // RW_SUB)) - 1):
        pw = [_pmm(x, x) for x in pw]
        t = [_pmm(a, eye + x) for a, x in zip(t, pw)]
    return [_pmm(a, d) for a, d in zip(t, td)]


def _rwkv_kernel(z_ref, w0_ref, w2_ref, a0_ref, a2_ref, g2_ref, kk_ref, ka_ref, rk_ref, lnw_ref,
                 lnb_ref, avg_ref, o_ref, h_ref):
    c = pl.program_id(1)
    C = RW_CHUNK
    n_pair = RW_WIDTH // PAIR
    nt = (((1,), (1,)), ((), ()))

    @pl.when(c == 0)
    def _():
        h_ref[...] = jnp.zeros_like(h_ref)

    ti = lax.broadcasted_iota(jnp.int32, (C, 1), 0)
    si = lax.broadcasted_iota(jnp.int32, (1, PAIR), 1) & (RW_HEAD_DIM - 1)
    incl, strict = ti >= si, ti > si
    eye = (ti == si).astype(F32)
    sub_shift = int(math.log2(RW_SUB))
    sub_mask = (ti >> sub_shift) == (si >> sub_shift)
    row2 = lax.broadcasted_iota(jnp.int32, (PAIR, 1), 0)
    col2 = lax.broadcasted_iota(jnp.int32, (1, PAIR), 1)
    same_head = (row2 < RW_HEAD_DIM) == (col2 < RW_HEAD_DIM)
    eye2 = row2 == col2
    tri = (lax.broadcasted_iota(jnp.int32, (C, C), 0)
           >= lax.broadcasted_iota(jnp.int32, (C, C), 1)).astype(BF16)

    def head_mean(x):
        xs = jnp.concatenate([x[:, p * PAIR:(p + 1) * PAIR] for p in range(n_pair)], axis=0)
        ms = _bdot(xs, avg_ref[...])
        return jnp.concatenate([ms[p * C:(p + 1) * C] for p in range(n_pair)], axis=1)

    n_rows = z_ref.shape[0]
    rows = []
    for n in range(n_rows):
        z = z_ref[n]
        zr, zk, zv = z[:, 0:512], z[:, 512:1024], z[:, 1024:1536]
        zw, za, zg = z[:, 1536:1600], z[:, 1600:1664], z[:, 1664:1792]
        w_raw = w0_ref[...] + _bdot(jnp.tanh(zw), w2_ref[...])
        logw = -jax.nn.sigmoid(w_raw) * math.exp(-0.5)
        a = jax.nn.sigmoid(a0_ref[...] + _bdot(za, a2_ref[...]))
        gate = _bdot(jax.nn.sigmoid(zg), g2_ref[...])
        kk = zk * kk_ref[...]
        kk = kk / jnp.maximum(jnp.sqrt(head_mean(kk * kk) * RW_HEAD_DIM), 1e-12)
        k = zk * (1.0 + (a - 1.0) * ka_ref[...])
        b = kk * a

        w_hi = logw.astype(BF16)
        w_lo = (logw - w_hi.astype(F32)).astype(BF16)
        cum = (jnp.dot(tri, w_hi, preferred_element_type=F32)
               + jnp.dot(tri, w_lo, preferred_element_type=F32))
        cum_last = cum[C - 1:C, :]
        g_inv = jnp.exp(-cum)
        g_end = jnp.exp(cum_last - cum)
        rows.append(dict(rt=zr * jnp.exp(cum), kt=k * g_inv, bt=b * g_inv, qt=kk * jnp.exp(cum - logw),
                         kh=k * g_end, bh=b * g_end, v=zv, g_last=jnp.exp(cum_last), gate=gate,
                         bonus=head_mean(zr * k * rk_ref[...]) * RW_HEAD_DIM * zv))

    chains = [(n, slice(p * PAIR, (p + 1) * PAIR)) for n in range(n_rows) for p in range(n_pair)]
    part = lambda name: [rows[n][name][:, sl] for n, sl in chains]
    qt, rt, kt, bt, kh, bh, v = (part(x) for x in ("qt", "rt", "kt", "bt", "kh", "bh", "v"))
    lhs = [jnp.concatenate([q, r], axis=0).astype(BF16) for q, r in zip(qt, rt)]
    ab = [lax.dot_general(l, _pair_blocks(x).astype(BF16), nt, preferred_element_type=F32)
          for l, x in zip(lhs, bt)]
    ak = [lax.dot_general(l, _pair_blocks(x).astype(BF16), nt, preferred_element_type=F32)
          for l, x in zip(lhs, kt)]
    a_kb = [jnp.where(strict, x[0:C], 0.0) for x in ab]
    a_rb = [jnp.where(incl, x[C:2 * C], 0.0) for x in ab]
    a_kk = [jnp.where(strict, x[0:C], 0.0) for x in ak]
    a_rk = [jnp.where(incl, x[C:2 * C], 0.0) for x in ak]
    t_inv = _unit_lower_inverse(a_kb, sub_mask, eye)

    h = [h_ref[n, sl.start // PAIR] for n, sl in chains]
    vb = [_pair_blocks(x) for x in v]
    rhs = [_bdot(jnp.concatenate([q, akk], axis=1), jnp.concatenate([hh, vv], axis=0))
           for q, akk, hh, vv in zip(qt, a_kk, h, vb)]
    u = [_pmm(t, x) for t, x in zip(t_inv, rhs)]
    outs = [_bdot(jnp.concatenate([r, ark, -arb], axis=1), jnp.concatenate([hh, vv, _pair_blocks(uu)], axis=0))
            for r, ark, arb, hh, vv, uu in zip(rt, a_rk, a_rb, h, vb, u)]
    upd = [_bdot(jnp.concatenate([x, -y], axis=0).T, jnp.concatenate([vv, uu], axis=0))
           for x, y, vv, uu in zip(kh, bh, v, u)]
    for (n, sl), hh, dd in zip(chains, h, upd):
        decay_col = jnp.sum(jnp.where(eye2, rows[n]["g_last"][:, sl], 0.0), axis=1, keepdims=True)
        h_ref[n, sl.start // PAIR] = decay_col * hh + jnp.where(same_head, dd, 0.0)

    for n in range(n_rows):
        o = jnp.concatenate(outs[n * n_pair:(n + 1) * n_pair], axis=1)
        d = o - head_mean(o)
        on = d * lax.rsqrt(head_mean(d * d) + GN_EPS)
        o_ref[n] = (on * lnw_ref[...] + lnb_ref[...] + rows[n]["bonus"]) * rows[n]["gate"]


def _rwkv(zrw, w0, w2, a0, a2, g2, k_k, k_a, r_k, ln_w, ln_b):
    B, S, _ = zrw.shape
    C = RW_CHUNK
    nb = RW_ROWS
    hid = np.arange(PAIR) // RW_HEAD_DIM
    avg = jnp.asarray((hid[:, None] == hid[None, :]).astype(np.float32) / RW_HEAD_DIM)
    vec = lambda a: a.reshape(1, RW_WIDTH)
    fixed = lambda shape: pl.BlockSpec(shape, lambda b, c: (0,) * len(shape))
    return pl.pallas_call(
        _rwkv_kernel,
        grid=(B // nb, S // C),
        in_specs=[pl.BlockSpec((nb, C, RW_IN), lambda b, c: (b, c, 0)),
                  fixed((1, RW_WIDTH)), fixed((DECAY_LORA, RW_WIDTH)),
                  fixed((1, RW_WIDTH)), fixed((AAA_LORA, RW_WIDTH)),
                  fixed((GATE_LORA, RW_WIDTH)), fixed((1, RW_WIDTH)), fixed((1, RW_WIDTH)),
                  fixed((1, RW_WIDTH)), fixed((1, RW_WIDTH)), fixed((1, RW_WIDTH)),
                  fixed((PAIR, PAIR))],
        out_specs=pl.BlockSpec((nb, C, RW_WIDTH), lambda b, c: (b, c, 0)),
        out_shape=jax.ShapeDtypeStruct((B, S, RW_WIDTH), F32),
        scratch_shapes=[pltpu.VMEM((nb, RW_WIDTH // PAIR, PAIR, PAIR), F32)],
        compiler_params=pltpu.CompilerParams(dimension_semantics=("parallel", "arbitrary"),
                                             vmem_limit_bytes=VMEM_LIMIT),
        name="rwkv",
    )(zrw, vec(w0), w2, vec(a0), a2, g2, vec(k_k), vec(k_a), vec(r_k), vec(ln_w), vec(ln_b), avg)


def _gelu_tanh(x):
    return 0.5 * x * (1.0 + jnp.tanh(math.sqrt(2.0 / math.pi) * (x + 0.044715 * (x * x * x))))


def _key_features(pos_hi, pos_lo, block, n, n_sel):
    lane = lax.broadcasted_iota(jnp.int32, (n, NSA_HEAD_DIM), 1)
    feat = jnp.where(lane == n_sel, pos_hi, jnp.where(lane == n_sel + 1, pos_lo, 0.0))
    return feat if block is None else jnp.where(lane == block, 1.0, feat)


def _nsa_prep_kernel(zkc_ref, zvc_ref, zks_ref, zvs_ref, zkw_ref, zvw_ref, pk_ref, pv_ref, kw1_ref,
                     kw2_ref, vw1_ref, vw2_ref, kc_ref, vct_ref, ksa_ref, kwa_ref, vst_ref, vwt_ref):
    S = zkc_ref.shape[1]
    n_grp = S // CMP_STRIDE
    Dh = NSA_HEAD_DIM
    half = CMP_BLOCK // 2
    n_sel = S // SEL_BLOCK
    jrow = lax.broadcasted_iota(jnp.int32, (n_grp, 1), 0)
    cmp_feat = _key_features((jrow >> 3).astype(F32),
                             ((jrow & 7) * CMP_STRIDE).astype(F32) + 0.5 * (CMP_BLOCK - 1),
                             None, n_grp, n_sel)
    for is_v, (z_ref, pos_ref, w1_ref, w2_ref) in enumerate(((zkc_ref, pk_ref, kw1_ref, kw2_ref),
                                                            (zvc_ref, pv_ref, vw1_ref, vw2_ref))):
        for g in range(NSA_KV_HEADS):
            lo = jnp.zeros((n_grp, Dh), F32)
            hi = jnp.zeros((n_grp, Dh), F32)
            for l in range(half):
                xs = z_ref[0, pl.ds(l, n_grp, stride=CMP_STRIDE), :]
                xg = xs[:, g * Dh:(g + 1) * Dh]
                lo = lo + _bdot(xg + pos_ref[l:l + 1, :], w1_ref[l * Dh:(l + 1) * Dh, :])
                hi = hi + _bdot(xg + pos_ref[half + l:half + l + 1, :],
                                w1_ref[(half + l) * Dh:(half + l + 1) * Dh, :])
            pre = lo + pltpu.roll(hi, n_grp - 1, 0)
            out = jnp.where(jrow < n_grp - 1, _bdot(_gelu_tanh(pre), w2_ref[...]), 0.0)
            if is_v:
                out_t = jnp.concatenate([out, jnp.zeros_like(out)], axis=1).T
                vct_ref[0, g] = out_t[0:Dh, :].astype(BF16)
            else:
                kc_ref[0, g] = jnp.concatenate([out, cmp_feat], axis=1)

    prow = lax.broadcasted_iota(jnp.int32, (S, 1), 0)
    p_hi, p_lo = (prow >> 7).astype(F32), (prow & (LANE - 1)).astype(F32)
    for z_ref, out_ref, block in ((zks_ref, ksa_ref, prow >> int(math.log2(SEL_BLOCK))),
                                  (zkw_ref, kwa_ref, None)):
        kfull = z_ref[0]
        key_feat = _key_features(p_hi, p_lo, block, S, n_sel)
        for g in range(NSA_KV_HEADS):
            out_ref[0, g] = jnp.concatenate([kfull[:, g * Dh:(g + 1) * Dh], key_feat], axis=1).astype(BF16)
    for z_ref, out_ref in ((zvs_ref, vst_ref), (zvw_ref, vwt_ref)):
        for j in range(S // TK):
            out_ref[0, j] = z_ref[0, j * TK:(j + 1) * TK, :].T.astype(BF16)


def _nsa_prep(zkv, pos_k, pos_v, kw1, kw2, vw1, vw2):
    B, S, _ = zkv.shape
    n_grp = S // CMP_STRIDE
    n_kt = S // TK
    Dh = NSA_HEAD_DIM
    G = NSA_KV_HEADS
    fixed = lambda shape: pl.BlockSpec(shape, lambda b: (0,) * len(shape))
    col = lambda c: pl.BlockSpec((1, S, NSA_KV_WIDTH), lambda b: (b, 0, c))
    whole = lambda shape: pl.BlockSpec((1,) + shape, lambda b: (b,) + (0,) * len(shape))
    shapes = [((G, n_grp, 2 * Dh), F32), ((G, Dh, n_grp), BF16), ((G, S, 2 * Dh), BF16),
              ((G, S, 2 * Dh), BF16), ((n_kt, G * Dh, TK), BF16), ((n_kt, G * Dh, TK), BF16)]
    return pl.pallas_call(
        _nsa_prep_kernel,
        grid=(B,),
        in_specs=[col(c) for c in range(6)] + [
            fixed((CMP_BLOCK, Dh)), fixed((CMP_BLOCK, Dh)),
            fixed((CMP_BLOCK * Dh, Dh)), fixed((Dh, Dh)),
            fixed((CMP_BLOCK * Dh, Dh)), fixed((Dh, Dh))],
        out_specs=[whole(s) for s, _ in shapes],
        out_shape=[jax.ShapeDtypeStruct((B,) + s, d) for s, d in shapes],
        compiler_params=pltpu.CompilerParams(dimension_semantics=("parallel",),
                                             vmem_limit_bytes=VMEM_LIMIT),
        name="nsa_prep",
    )(zkv, zkv, zkv, zkv, zkv, zkv, pos_k, pos_v, kw1, kw2, vw1, vw2)


def _nsa_attn_kernel(q_ref, kc_ref, vct_ref, ksa_ref, kwa_ref, vst_ref, vwt_ref, gl_ref, gb_ref,
                     ovlt_ref, slope_ref, o_ref, acc_ref, ot_ref):
    i = pl.program_id(1)
    Dh = NSA_HEAD_DIM
    R = NSA_REP
    N = R * TQ
    n_cmp_pad = kc_ref.shape[2]
    n_sel = ovlt_ref.shape[0]
    G = NSA_KV_HEADS
    nt = (((1,), (1,)), ((), ()))
    log2e = math.log2(math.e)
    t0 = i * TQ
    t_row = t0 + lax.broadcasted_iota(jnp.int32, (1, TQ), 1)
    c_col = lax.broadcasted_iota(jnp.int32, (TK, 1), 0)
    sgate_t = jax.nn.sigmoid(gl_ref[0] + gb_ref[...]).T
    lane_f = lax.broadcasted_iota(jnp.int32, (1, Dh), 1)
    heads = lambda x: jnp.concatenate([x] * R, axis=1)

    def key_dist(j):
        return t_row - (j * TK + c_col)

    def queries(g, sel_feat):
        parts = []
        for r in range(R):
            h = g * R + r
            sl = slope_ref[:, h:h + 1] * log2e
            feat = jnp.where(lane_f == n_sel, sl * LANE, jnp.where(lane_f == n_sel + 1, sl, sel_feat))
            parts.append(jnp.concatenate(
                [q_ref[0, :, h * Dh:(h + 1) * Dh] * (Dh ** -0.5 * log2e), jnp.broadcast_to(feat, (TQ, Dh))],
                axis=1))
        return jnp.concatenate(parts, axis=0)

    qab, o_cmp = [], []
    for g in range(G):
        jc = lax.broadcasted_iota(jnp.int32, (n_cmp_pad, 1), 0)
        ok_c = (jc * CMP_STRIDE + (CMP_BLOCK - 1) <= t_row) & (jc < n_cmp_pad - 1)
        s_c = (lax.dot_general(kc_ref[0, g], queries(g, 0.0), nt, precision=HI, preferred_element_type=F32)
               + heads(jnp.where(ok_c, 0.0, NEG_INF)))
        e_c = jnp.exp2(s_c - jnp.max(s_c, axis=0, keepdims=True))
        any_c = heads(t_row >= CMP_BLOCK - 1)
        p_c = e_c * jnp.where(any_c, 1.0 / jnp.sum(e_c, axis=0, keepdims=True), 0.0)
        o_cmp.append(jnp.dot(vct_ref[0, g], p_c.astype(BF16), preferred_element_type=F32))

        p_sum = p_c[:, 0:TQ]
        for r in range(1, R):
            p_sum = p_sum + p_c[:, r * TQ:(r + 1) * TQ]
        imp = jnp.dot(ovlt_ref[...], p_sum, precision=HI, preferred_element_type=F32)
        kb = lax.broadcasted_iota(jnp.int32, (n_sel, 1), 0)
        kbf = kb.astype(F32)
        blk_t = t_row >> int(math.log2(SEL_BLOCK))
        forced = (kb == 0) | (kb == blk_t) | (kb == blk_t - 1)
        cur = jnp.where(forced, FORCE_SCORE, jnp.where(kb <= blk_t, imp, -FORCE_SCORE))
        sel_bias = jnp.full((n_sel, TQ), NEG_INF, F32)
        for _ in range(min(N_SELECT, n_sel)):
            mx = jnp.max(cur, axis=0, keepdims=True)
            first = jnp.min(jnp.where(cur == mx, kbf, float(n_sel)), axis=0, keepdims=True)
            hit = kbf == first
            sel_bias = jnp.where(hit, 0.0, sel_bias)
            cur = jnp.where(hit, -3e38, cur)
        sel_feat = jnp.concatenate([sel_bias, jnp.zeros((LANE - n_sel, TQ), F32)], axis=0).T[:, 0:Dh]
        qab.append(queries(g, sel_feat).astype(BF16))

    def tile(j, carry, k_ref, vt_ref, bias, slot):
        out = []
        for g in range(G):
            m, l = carry[g]
            k = k_ref[0, g, pl.ds(pl.multiple_of(j * TK, TK), TK), :]
            s = lax.dot_general(k, qab[g], nt, preferred_element_type=F32)
            if bias is not None:
                s = s + heads(bias)
            m_new = jnp.maximum(m, jnp.max(s, axis=0, keepdims=True))
            alpha = jnp.exp2(m - m_new)
            p = jnp.exp2(s - m_new)
            l = alpha * l + jnp.sum(p, axis=0, keepdims=True)
            vt = vt_ref[0, j, g * Dh:(g + 1) * Dh, :]
            acc_ref[slot, g] = (alpha * acc_ref[slot, g]
                                + jnp.dot(vt, p.astype(BF16), preferred_element_type=F32))
            out.append((m_new, l))
        return tuple(out)

    def window_bias(j):
        d = key_dist(j)
        return jnp.where((d >= 0) & (d < WINDOW), 0.0, NEG_INF)

    init = ((jnp.full((1, N), NEG_INF, F32), jnp.zeros((1, N), F32)),) * G
    acc_ref[...] = jnp.zeros_like(acc_ref)
    causal = jnp.where(key_dist(i) >= 0, 0.0, NEG_INF)
    carry = lax.fori_loop(0, i, lambda j, c: tile(j, c, ksa_ref, vst_ref, None, 0), init)
    stat_s = tile(i, carry, ksa_ref, vst_ref, causal, 0)
    carry = lax.fori_loop(jnp.maximum(i - WINDOW // TK, 0), i,
                          lambda j, c: tile(j, c, kwa_ref, vwt_ref, window_bias(j), 1), init)
    stat_w = tile(i, carry, kwa_ref, vwt_ref, causal, 1)

    for g in range(G):
        o_sel = acc_ref[0, g] * (1.0 / stat_s[g][1])
        o_win = acc_ref[1, g] * (1.0 / stat_w[g][1])
        for r in range(R):
            h = g * R + r
            cols = slice(r * TQ, (r + 1) * TQ)
            ot_ref[h * Dh:(h + 1) * Dh, :] = (
                sgate_t[3 * h:3 * h + 1, :] * o_cmp[g][:, cols]
                + sgate_t[3 * h + 1:3 * h + 2, :] * o_sel[:, cols]
                + sgate_t[3 * h + 2:3 * h + 3, :] * o_win[:, cols])
    o_ref[0] = ot_ref[...].T


def _nsa_attn(zq, kc, vct, ksa, kwa, vst, vwt, zgate, gate_b):
    B, S, _ = zq.shape
    n_sel = S // SEL_BLOCK
    n_cmp = (S - CMP_BLOCK) // CMP_STRIDE + 1
    n_cmp_pad = kc.shape[2]
    n_kt = S // TK
    G, Dh = NSA_KV_HEADS, NSA_HEAD_DIM
    cmp_start = np.arange(n_cmp) * CMP_STRIDE
    sel_start = np.arange(n_sel) * SEL_BLOCK
    overlap = np.clip(np.minimum(cmp_start[:, None] + CMP_BLOCK, sel_start[None, :] + SEL_BLOCK)
                      - np.maximum(cmp_start[:, None], sel_start[None, :]), 0, None) / CMP_BLOCK
    ovlt = np.zeros((n_sel, n_cmp_pad), np.float32)
    ovlt[:, :n_cmp] = overlap.T
    slopes = (2.0 ** (-8.0 * np.arange(1, NSA_HEADS + 1) / NSA_HEADS)).astype(np.float32).reshape(1, NSA_HEADS)
    gb = jnp.pad(gate_b, (0, LANE - N_GATE)).reshape(1, LANE)
    fixed = lambda shape: pl.BlockSpec(shape, lambda b, i: (0,) * len(shape))
    per_b = lambda shape: pl.BlockSpec((1,) + shape, lambda b, i: (b,) + (0,) * len(shape))
    return pl.pallas_call(
        _nsa_attn_kernel,
        grid=(B, S // TQ),
        in_specs=[pl.BlockSpec((1, TQ, NSA_WIDTH), lambda b, i: (b, i, 0)),
                  per_b((G, n_cmp_pad, 2 * Dh)), per_b((G, Dh, n_cmp_pad)),
                  per_b((G, S, 2 * Dh)), per_b((G, S, 2 * Dh)),
                  per_b((n_kt, G * Dh, TK)), per_b((n_kt, G * Dh, TK)),
                  pl.BlockSpec((1, TQ, LANE), lambda b, i: (b, i, 0)),
                  fixed((1, LANE)), fixed((n_sel, n_cmp_pad)), fixed((1, NSA_HEADS))],
        out_specs=pl.BlockSpec((1, TQ, NSA_WIDTH), lambda b, i: (b, i, 0)),
        out_shape=jax.ShapeDtypeStruct((B, S, NSA_WIDTH), F32),
        scratch_shapes=[pltpu.VMEM((2, G, Dh, NSA_REP * TQ), F32),
                        pltpu.VMEM((NSA_WIDTH, TQ), F32)],
        compiler_params=pltpu.CompilerParams(dimension_semantics=("parallel", "arbitrary"),
                                             vmem_limit_bytes=VMEM_LIMIT),
        name="nsa_attn",
    )(zq, kc, vct, ksa, kwa, vst, vwt, zgate, gb, jnp.asarray(ovlt), jnp.asarray(slopes))


def _mix_kernel(x_ref, yr_ref, yn_ref, zmg_ref, ur_ref, un_ref, wo_ref, gf_ref, wr_ref, br_ref,
                x1_ref, h2_ref, rt_ref, cnt_ref):
    tm = x_ref.shape[0]
    zmg = zmg_ref[...]
    mixed = (jax.nn.sigmoid(zmg[:, :D_MODEL]) * _bdot(yr_ref[...], ur_ref[...])
             + jax.nn.sigmoid(zmg[:, D_MODEL:]) * _bdot(yn_ref[...], un_ref[...]))
    x1 = x_ref[...] + _bdot(mixed, wo_ref[...])
    x1_ref[...] = x1
    h2 = _rms(x1, gf_ref[...])
    h2_ref[...] = h2
    n_row = wr_ref.shape[0]
    logits = lax.dot_general(wr_ref[...], h2, (((1,), (1,)), ((), ())), precision=HI,
                             preferred_element_type=F32) + br_ref[...]
    row = lax.broadcasted_iota(jnp.int32, (n_row, 1), 0).astype(F32)
    gl = jnp.where(row < N_GROUPS, logits, NEG_INF)
    gmax = jnp.max(gl, axis=0, keepdims=True)
    g_sel = jnp.min(jnp.where(gl == gmax, row, float(n_row)), axis=0, keepdims=True)
    p_group = 1.0 / jnp.sum(jnp.exp(gl - gmax), axis=0, keepdims=True)
    e_row = row - N_GROUPS
    in_grp = ((e_row >= g_sel * EXPERTS_PER_GROUP) & (e_row < (g_sel + 1.0) * EXPERTS_PER_GROUP)
              & (e_row < N_EXPERTS))
    el = jnp.where(in_grp, logits, NEG_INF)
    m1 = jnp.max(el, axis=0, keepdims=True)
    i1 = jnp.min(jnp.where(el == m1, e_row, float(n_row)), axis=0, keepdims=True)
    el2 = jnp.where(e_row == i1, 2.0 * NEG_INF, el)
    m2 = jnp.max(el2, axis=0, keepdims=True)
    i2 = jnp.min(jnp.where(el2 == m2, e_row, float(n_row)), axis=0, keepdims=True)
    r2 = jnp.exp(m2 - m1)
    g1 = p_group / (1.0 + r2)
    g2 = p_group * r2 / (1.0 + r2)

    @pl.when(pl.program_id(0) == 0)
    def _():
        cnt_ref[...] = jnp.zeros_like(cnt_ref)

    pick1, pick2 = e_row == i1, e_row == i2
    both = pick1.astype(F32) + pick2.astype(F32)
    earlier = (lax.broadcasted_iota(jnp.int32, (tm, tm), 0)
               < lax.broadcasted_iota(jnp.int32, (tm, tm), 1)).astype(BF16)
    before = jnp.dot(both.astype(BF16), earlier, preferred_element_type=F32) + cnt_ref[:, 0:1]
    rank1 = jnp.sum(jnp.where(pick1, before, 0.0), axis=0, keepdims=True)
    rank2 = jnp.sum(jnp.where(pick2, before, 0.0), axis=0, keepdims=True)
    cnt_ref[...] = cnt_ref[...] + jnp.sum(both, axis=1, keepdims=True)
    rt_ref[...] = jnp.concatenate([i1, i2, g1, g2, rank1, rank2, jnp.zeros((2, tm), F32)], axis=0)


def _mix(x2d, y_rw, y_nsa, zmg, w_up_r, w_up_n, w_out, g_ffn, w_group, b_group, w_router, b_router):
    T = x2d.shape[0]
    tm = TM_PROJ
    n_r = N_GROUPS + N_EXPERTS
    n_row = -(-n_r // 8) * 8
    wr = jnp.pad(jnp.concatenate([w_group, w_router], axis=1).T, ((0, n_row - n_r), (0, 0)))
    br = jnp.pad(jnp.concatenate([b_group, b_router]), (0, n_row - n_r)).reshape(n_row, 1)
    row = lambda i: (i, 0)
    fixed = lambda i: (0, 0)
    return pl.pallas_call(
        _mix_kernel,
        grid=(T // tm,),
        in_specs=[pl.BlockSpec((tm, D_MODEL), row), pl.BlockSpec((tm, RW_WIDTH), row),
                  pl.BlockSpec((tm, NSA_WIDTH), row), pl.BlockSpec((tm, 2 * D_MODEL), row),
                  pl.BlockSpec((RW_WIDTH, D_MODEL), fixed), pl.BlockSpec((NSA_WIDTH, D_MODEL), fixed),
                  pl.BlockSpec((D_MODEL, D_MODEL), fixed), pl.BlockSpec((1, D_MODEL), fixed),
                  pl.BlockSpec((n_row, D_MODEL), fixed), pl.BlockSpec((n_row, 1), fixed)],
        out_specs=[pl.BlockSpec((tm, D_MODEL), row), pl.BlockSpec((tm, D_MODEL), row),
                   pl.BlockSpec((8, tm), lambda i: (0, i)), pl.BlockSpec((n_row, LANE), fixed)],
        out_shape=[jax.ShapeDtypeStruct((T, D_MODEL), F32), jax.ShapeDtypeStruct((T, D_MODEL), F32),
                   jax.ShapeDtypeStruct((8, T), F32), jax.ShapeDtypeStruct((n_row, LANE), F32)],
        compiler_params=pltpu.CompilerParams(dimension_semantics=("arbitrary",),
                                             vmem_limit_bytes=VMEM_LIMIT),
        name="mix",
    )(x2d, y_rw, y_nsa, zmg, w_up_r.astype(BF16), w_up_n.astype(BF16), w_out.astype(BF16),
      g_ffn.reshape(1, D_MODEL), wr, br)


def _route_tables(rt, cnt, T):
    n_rows = T * TOP_K + N_EXPERTS * ROW_BLOCK
    n_blk = n_rows // ROW_BLOCK
    counts = cnt[N_GROUPS:N_GROUPS + N_EXPERTS, 0].astype(jnp.int32)
    padded = (counts + ROW_BLOCK - 1) // ROW_BLOCK * ROW_BLOCK
    pends = jnp.cumsum(padded)
    pstarts = pends - padded
    expert = rt[0:TOP_K].astype(jnp.int32)
    rank = rt[2 * TOP_K:3 * TOP_K].astype(jnp.int32)
    seg_start = jnp.sum(jnp.where(expert[..., None] == jnp.arange(N_EXPERTS), pstarts, 0), axis=-1)
    dest = (seg_start + rank).T.reshape(T * TOP_K)
    gates = rt[TOP_K:2 * TOP_K].T
    blk_start = jnp.arange(n_blk) * ROW_BLOCK
    blk_expert = jnp.minimum(jnp.sum(pends[None, :] <= blk_start[:, None], axis=1), N_EXPERTS - 1)
    blk_nv = jnp.clip(counts[blk_expert] - (blk_start - pstarts[blk_expert]), 0, ROW_BLOCK)
    return dest.astype(jnp.int32), gates, blk_expert.astype(jnp.int32), blk_nv.astype(jnp.int32)


def _row_copies(index_ref, base, n, make):
    for t in range(n):
        for k in range(TOP_K):
            make(base + t, k, index_ref[(base + t) * TOP_K + k]).start()


def _dispatch_kernel(dest_ref, h2_ref, xs_init_hbm, xs_hbm, sem):
    del xs_init_hbm
    i = pl.program_id(0)
    tm = h2_ref.shape[0]

    def row_copy(tok, k, row):
        return pltpu.make_async_copy(h2_ref.at[pl.ds(tok - i * tm, 1)], xs_hbm.at[pl.ds(row, 1)], sem)

    _row_copies(dest_ref, i * tm, tm, row_copy)
    for _ in range(TOP_K):
        pltpu.make_async_copy(h2_ref, xs_hbm.at[pl.ds(0, tm)], sem).wait()


def _dispatch(h2, dest):
    T = h2.shape[0]
    n_rows = T * TOP_K + N_EXPERTS * ROW_BLOCK
    grid_spec = pltpu.PrefetchScalarGridSpec(
        num_scalar_prefetch=1,
        grid=(T // TM_PROJ,),
        in_specs=[pl.BlockSpec((TM_PROJ, D_MODEL), lambda i, d: (i, 0)), pl.BlockSpec(memory_space=pl.ANY)],
        out_specs=pl.BlockSpec(memory_space=pl.ANY),
        scratch_shapes=[pltpu.SemaphoreType.DMA(())])
    return pl.pallas_call(
        _dispatch_kernel,
        grid_spec=grid_spec,
        out_shape=jax.ShapeDtypeStruct((n_rows, D_MODEL), F32),
        input_output_aliases={2: 0},
        compiler_params=pltpu.CompilerParams(dimension_semantics=("arbitrary",),
                                             vmem_limit_bytes=VMEM_LIMIT),
        name="dispatch",
    )(dest, h2, jnp.zeros((n_rows, D_MODEL), F32))


def _expert_kernel(be_ref, nv_ref, xs_ref, wgu_ref, wd_ref, o_ref, wgu_b, wd_b):
    i = pl.program_id(0)
    nv = nv_ref[i]
    changed = jnp.logical_or(i == 0, be_ref[i] != be_ref[jnp.maximum(i - 1, 0)])

    @pl.when(jnp.logical_and(nv > 0, changed))
    def _():
        wgu_b[...] = wgu_ref[0].astype(BF16)
        wd_b[...] = wd_ref[0].astype(BF16)

    @pl.when(nv > 0)
    def _():
        gu = jnp.dot(xs_ref[...].astype(BF16), wgu_b[...], preferred_element_type=F32)
        gate_h, up_h = gu[:, :D_EXPERT], gu[:, D_EXPERT:]
        mid = gate_h * jax.nn.sigmoid(gate_h) * up_h
        o_ref[...] = jnp.dot(mid.astype(BF16), wd_b[...], preferred_element_type=F32)

    @pl.when(nv == 0)
    def _():
        o_ref[...] = jnp.zeros_like(o_ref)


def _experts(xs, blk_expert, blk_nv, w_gate_up, w_down):
    n_rows = xs.shape[0]
    grid_spec = pltpu.PrefetchScalarGridSpec(
        num_scalar_prefetch=2,
        grid=(n_rows // ROW_BLOCK,),
        in_specs=[pl.BlockSpec((ROW_BLOCK, D_MODEL), lambda i, be, nv: (i, 0)),
                  pl.BlockSpec((1, D_MODEL, 2 * D_EXPERT), lambda i, be, nv: (be[i], 0, 0)),
                  pl.BlockSpec((1, D_EXPERT, D_MODEL), lambda i, be, nv: (be[i], 0, 0))],
        out_specs=pl.BlockSpec((ROW_BLOCK, D_MODEL), lambda i, be, nv: (i, 0)),
        scratch_shapes=[pltpu.VMEM((D_MODEL, 2 * D_EXPERT), BF16), pltpu.VMEM((D_EXPERT, D_MODEL), BF16)])
    return pl.pallas_call(
        _expert_kernel,
        grid_spec=grid_spec,
        out_shape=jax.ShapeDtypeStruct((n_rows, D_MODEL), F32),
        compiler_params=pltpu.CompilerParams(dimension_semantics=("arbitrary",),
                                             vmem_limit_bytes=VMEM_LIMIT),
        name="experts",
    )(blk_expert, blk_nv, xs, w_gate_up, w_down)


def _ple_kernel(dest_ref, x1_ref, g_ref, p_ref, rows_hbm, wpp_ref, gpn_ref, ggi_ref, wpg_ref, gfin_ref,
                y_ref, ebuf, sem, *, last_layer):
    i = pl.program_id(0)
    tm = x1_ref.shape[0]
    slot = i % 2

    def fetch(tile, s):
        def row_copy(tok, k, row):
            return pltpu.make_async_copy(rows_hbm.at[pl.ds(row, 1)],
                                         ebuf.at[s, k, pl.ds(tok - tile * tm, 1)], sem.at[s])
        _row_copies(dest_ref, tile * tm, tm, row_copy)

    def drain(s):
        for k in range(TOP_K):
            pltpu.make_async_copy(rows_hbm.at[pl.ds(0, tm)], ebuf.at[s, k], sem.at[s]).wait()

    last = pl.num_programs(0) - 1

    @pl.when(i == 0)
    def _():
        fetch(0, 0)

    fetch(jnp.minimum(i + 1, last), 1 - slot)
    drain(slot)
    moe = g_ref[:, 0:1] * ebuf[slot, 0]
    for k in range(1, TOP_K):
        moe = moe + g_ref[:, k:k + 1] * ebuf[slot, k]
    x2 = x1_ref[...] + moe
    e = _rms(_bdot(p_ref[...], wpp_ref[...]), gpn_ref[...])
    gate = jax.nn.sigmoid(_bdot(_rms(x2, ggi_ref[...]), wpg_ref[...]))
    x3 = x2 + gate * e
    y_ref[...] = _rms(x3, gfin_ref[...]) if last_layer else x3

    @pl.when(i == last)
    def _():
        drain(1 - slot)


def _ple(x1, gates, dest, expert_rows, p2d, w_pp, g_pn, g_gi, w_pg, g_final, last_layer):
    T = x1.shape[0]
    tm = TM_PROJ
    row = lambda i, d: (i, 0)
    fixed = lambda i, d: (0, 0)
    vec = lambda a: a.reshape(1, D_MODEL)
    grid_spec = pltpu.PrefetchScalarGridSpec(
        num_scalar_prefetch=1,
        grid=(T // tm,),
        in_specs=[pl.BlockSpec((tm, D_MODEL), row), pl.BlockSpec((tm, TOP_K), row),
                  pl.BlockSpec((tm, PLE_DIM), row), pl.BlockSpec(memory_space=pl.ANY),
                  pl.BlockSpec((PLE_DIM, D_MODEL), fixed), pl.BlockSpec((1, D_MODEL), fixed),
                  pl.BlockSpec((1, D_MODEL), fixed), pl.BlockSpec((D_MODEL, D_MODEL), fixed),
                  pl.BlockSpec((1, D_MODEL), fixed)],
        out_specs=pl.BlockSpec((tm, D_MODEL), row),
        scratch_shapes=[pltpu.VMEM((2, TOP_K, tm, D_MODEL), F32), pltpu.SemaphoreType.DMA((2,))])
    return pl.pallas_call(
        functools.partial(_ple_kernel, last_layer=last_layer),
        grid_spec=grid_spec,
        out_shape=jax.ShapeDtypeStruct((T, D_MODEL), F32),
        compiler_params=pltpu.CompilerParams(dimension_semantics=("arbitrary",),
                                             vmem_limit_bytes=VMEM_LIMIT),
        name="ple",
    )(dest, x1, gates, p2d, expert_rows, w_pp.astype(BF16), vec(g_pn), vec(g_gi), w_pg.astype(BF16),
      vec(g_final))


def kernel(x, p, g_mix, w_in, mu_shift, rw_w0, rw_w2, rw_a0, rw_a2, rw_g2, rw_k_k, rw_k_a, rw_r_k, rw_ln_w, rw_ln_b, cmp_pos_k, cmp_pos_v, cmp_k_w1, cmp_k_w2, cmp_v_w1, cmp_v_w2, nsa_gate_b, w_up_rwkv, w_up_nsa, w_out, g_ffn, w_group, b_group, w_router, b_router, w_exp_gate_up, w_exp_down, w_ple_proj, g_ple_norm, g_ple_gate_in, w_ple_gate, g_final):
    B, S, D = x.shape
    T = B * S
    depth = p.shape[0]
    xc = x.reshape(T, D)
    for i in range(depth):
        zrw, zq, zkv, zmg, zgate = _proj(xc, g_mix[i], w_in[i], mu_shift[i], S)
        y_rw = _rwkv(zrw.reshape(B, S, RW_IN), rw_w0[i], rw_w2[i], rw_a0[i], rw_a2[i], rw_g2[i],
                     rw_k_k[i], rw_k_a[i], rw_r_k[i], rw_ln_w[i], rw_ln_b[i])
        zkv3 = zkv.reshape(B, S, KV_IN)
        nsa_kv = _nsa_prep(zkv3, cmp_pos_k[i], cmp_pos_v[i], cmp_k_w1[i], cmp_k_w2[i],
                           cmp_v_w1[i], cmp_v_w2[i])
        y_nsa = _nsa_attn(zq.reshape(B, S, NSA_WIDTH), *nsa_kv, zgate.reshape(B, S, LANE),
                          nsa_gate_b[i])
        x1, h2, rt, cnt = _mix(xc, y_rw.reshape(T, RW_WIDTH), y_nsa.reshape(T, NSA_WIDTH), zmg,
                               w_up_rwkv[i], w_up_nsa[i], w_out[i], g_ffn[i], w_group[i], b_group[i],
                               w_router[i], b_router[i])
        dest, gates, blk_expert, blk_nv = _route_tables(rt, cnt, T)
        expert_rows = _experts(_dispatch(h2, dest), blk_expert, blk_nv, w_exp_gate_up[i], w_exp_down[i])
        xc = _ple(x1, gates, dest, expert_rows, p[i].reshape(T, PLE_DIM), w_ple_proj[i], g_ple_norm[i],
                  g_ple_gate_in[i], w_ple_gate[i], g_final, i == depth - 1)
    return xc.reshape(B, S, D)
```

```python
import functools
import math

import jax
import jax.numpy as jnp
import numpy as np
from jax import lax
from jax.experimental import pallas as pl
from jax.experimental.pallas import tpu as pltpu

F32 = jnp.float32
BF16 = jnp.bfloat16
HI = lax.Precision.HIGHEST

D_MODEL = 1024
RW_HEADS = 8
RW_HEAD_DIM = 64
RW_WIDTH = 512
DECAY_LORA = 64
AAA_LORA = 64
GATE_LORA = 128
GN_EPS = 64e-5
RW_IN = 3 * RW_WIDTH + DECAY_LORA + AAA_LORA + GATE_LORA

NSA_HEADS = 8
NSA_KV_HEADS = 2
NSA_REP = NSA_HEADS // NSA_KV_HEADS
NSA_HEAD_DIM = 64
NSA_WIDTH = 512
NSA_KV_WIDTH = 128
CMP_BLOCK = 32
CMP_STRIDE = 16
SEL_BLOCK = 64
N_SELECT = 8
WINDOW = 512
N_NSA_BRANCH = 3
FORCE_SCORE = 1e6
NEG_INF = -1e30

N_GROUPS = 4
EXPERTS_PER_GROUP = 8
N_EXPERTS = 32
TOP_K = 2
D_EXPERT = 512
ROW_BLOCK = 256
PLE_DIM = 256
NORM_EPS = 1e-6

N_GATE = N_NSA_BRANCH * NSA_HEADS
ATT_IN = NSA_WIDTH + 6 * NSA_KV_WIDTH
KV_OFF = RW_IN + NSA_WIDTH
KV_IN = 6 * NSA_KV_WIDTH
GATE_OFF = RW_IN + ATT_IN
MERGE_OFF = GATE_OFF + N_GATE
LANE = 128

RW_CHUNK = 64
RW_SUB = 16
RW_ROWS = 4
TQ = 256
TK = 256
V_ROWS = NSA_HEAD_DIM + 16
TM_PROJ = 256
VMEM_LIMIT = 56 * 1024 * 1024


def _bdot(a, b):
    return jnp.dot(a.astype(BF16), b.astype(BF16), preferred_element_type=F32)


def _bdot_nt(a, b):
    return lax.dot_general(a.astype(BF16), b.astype(BF16), (((1,), (1,)), ((), ())),
                           preferred_element_type=F32)


def _rms(x, g):
    return x * lax.rsqrt(jnp.mean(x * x, axis=-1, keepdims=True) + NORM_EPS) * g


def _proj_kernel(x_ref, g_ref, w_ref, mu_ref, zrw_ref, zq_ref, zkv_ref, zmg_ref, zgate_ref,
                 carry_ref, *, tiles_per_seq):
    i = pl.program_id(0)
    tm = x_ref.shape[0]

    @pl.when(i % tiles_per_seq == 0)
    def _():
        carry_ref[...] = jnp.zeros_like(carry_ref)

    h = _rms(x_ref[...], g_ref[...]).astype(BF16)
    z = jnp.dot(h, w_ref[:, 0:RW_IN], preferred_element_type=F32)
    row = lax.broadcasted_iota(jnp.int32, (tm, 1), 0)
    prev = jnp.where(row == 0, carry_ref[7:8, :], pltpu.roll(z, 1, 0))
    carry_ref[...] = z[tm - 8:tm, :]
    zrw_ref[...] = z + (prev - z) * mu_ref[...]
    zq_ref[...] = jnp.dot(h, w_ref[:, RW_IN:KV_OFF], preferred_element_type=F32)
    zkv_ref[...] = jnp.dot(h, w_ref[:, KV_OFF:GATE_OFF], preferred_element_type=F32)
    zmg_ref[...] = jnp.dot(h, w_ref[:, GATE_OFF:GATE_OFF + 2 * D_MODEL], preferred_element_type=F32)
    zgate_ref[...] = jnp.dot(h, w_ref[:, GATE_OFF + 2 * D_MODEL:], preferred_element_type=F32)


def _proj(x2d, g_mix, w_in, mu, seq):
    T = x2d.shape[0]
    tm = TM_PROJ
    wp = jnp.concatenate(
        [w_in[:, :GATE_OFF], w_in[:, MERGE_OFF:],
         jnp.pad(w_in[:, GATE_OFF:MERGE_OFF], ((0, 0), (0, LANE - N_GATE)))], axis=1).astype(BF16)
    npad = wp.shape[1]
    row = lambda i: (i, 0)
    fixed = lambda i: (0, 0)
    return pl.pallas_call(
        functools.partial(_proj_kernel, tiles_per_seq=seq // tm),
        grid=(T // tm,),
        in_specs=[pl.BlockSpec((tm, D_MODEL), row), pl.BlockSpec((1, D_MODEL), fixed),
                  pl.BlockSpec((D_MODEL, npad), fixed), pl.BlockSpec((1, RW_IN), fixed)],
        out_specs=[pl.BlockSpec((tm, RW_IN), row), pl.BlockSpec((tm, NSA_WIDTH), row),
                   pl.BlockSpec((tm, KV_IN), row), pl.BlockSpec((tm, 2 * D_MODEL), row),
                   pl.BlockSpec((tm, LANE), row)],
        out_shape=[jax.ShapeDtypeStruct((T, RW_IN), F32), jax.ShapeDtypeStruct((T, NSA_WIDTH), F32),
                   jax.ShapeDtypeStruct((T, KV_IN), F32), jax.ShapeDtypeStruct((T, 2 * D_MODEL), F32),
                   jax.ShapeDtypeStruct((T, LANE), F32)],
        scratch_shapes=[pltpu.VMEM((8, RW_IN), F32)],
        compiler_params=pltpu.CompilerParams(dimension_semantics=("arbitrary",),
                                             vmem_limit_bytes=VMEM_LIMIT),
        name="proj",
    )(x2d, g_mix.reshape(1, D_MODEL), wp, mu.reshape(1, RW_IN))


PAIR = 2 * RW_HEAD_DIM


def _pair_blocks(x):
    low = lax.broadcasted_iota(jnp.int32, (1, PAIR), 1) < RW_HEAD_DIM
    return jnp.concatenate([jnp.where(low, x, 0.0), jnp.where(low, 0.0, x)], axis=0)


def _pmm(a, b):
    return _bdot(a, _pair_blocks(b))


def _unit_lower_inverse(a_strict, sub_mask, eye):
    ad = [jnp.where(sub_mask, a, 0.0) for a in a_strict]
    ao = [a - d for a, d in zip(a_strict, ad)]
    td = [eye - d for d in ad]
    pw = ad
    for _ in range(int(math.log2(RW_SUB)) - 1):
        pw = [_pmm(x, x) for x in pw]
        td = [_pmm(t, eye + x) for t, x in zip(td, pw)]
    n = [_pmm(t, o) for t, o in zip(td, ao)]
    t = [eye - x for x in n]
    pw = n
    for _ in range(int(math.log2(RW_CHUNK // RW_SUB)) - 1):
        pw = [_pmm(x, x) for x in pw]
        t = [_pmm(a, eye + x) for a, x in zip(t, pw)]
    return [_pmm(a, d) for a, d in zip(t, td)]


def _rwkv_kernel(z_ref, w0_ref, w2_ref, a0_ref, a2_ref, g2_ref, kk_ref, ka_ref, rk_ref, lnw_ref,
                 lnb_ref, avg_ref, o_ref, h_ref):
    c = pl.program_id(1)
    C = RW_CHUNK
    n_pair = RW_WIDTH // PAIR
    nt = (((1,), (1,)), ((), ()))

    @pl.when(c == 0)
    def _():
        h_ref[...] = jnp.zeros_like(h_ref)

    ti = lax.broadcasted_iota(jnp.int32, (C, 1), 0)
    si = lax.broadcasted_iota(jnp.int32, (1, PAIR), 1) & (RW_HEAD_DIM - 1)
    incl, strict = ti >= si, ti > si
    eye = (ti == si).astype(F32)
    sub_shift = int(math.log2(RW_SUB))
    sub_mask = (ti >> sub_shift) == (si >> sub_shift)
    row2 = lax.broadcasted_iota(jnp.int32, (PAIR, 1), 0)
    col2 = lax.broadcasted_iota(jnp.int32, (1, PAIR), 1)
    same_head = (row2 < RW_HEAD_DIM) == (col2 < RW_HEAD_DIM)
    eye2 = row2 == col2
    tri = (lax.broadcasted_iota(jnp.int32, (C, C), 0)
           >= lax.broadcasted_iota(jnp.int32, (C, C), 1)).astype(BF16)

    def head_mean(x):
        xs = jnp.concatenate([x[:, p * PAIR:(p + 1) * PAIR] for p in range(n_pair)], axis=0)
        ms = _bdot(xs, avg_ref[...])
        return jnp.concatenate([ms[p * C:(p + 1) * C] for p in range(n_pair)], axis=1)

    n_rows = z_ref.shape[0]
    rows = []
    for n in range(n_rows):
        z = z_ref[n]
        zr, zk, zv = z[:, 0:512], z[:, 512:1024], z[:, 1024:1536]
        zw, za, zg = z[:, 1536:1600], z[:, 1600:1664], z[:, 1664:1792]
        w_raw = w0_ref[...] + _bdot(jnp.tanh(zw), w2_ref[...])
        logw = -jax.nn.sigmoid(w_raw) * math.exp(-0.5)
        a = jax.nn.sigmoid(a0_ref[...] + _bdot(za, a2_ref[...]))
        gate = _bdot(jax.nn.sigmoid(zg), g2_ref[...])
        kk = zk * kk_ref[...]
        kk = kk / jnp.maximum(jnp.sqrt(head_mean(kk * kk) * RW_HEAD_DIM), 1e-12)
        k = zk * (1.0 + (a - 1.0) * ka_ref[...])
        b = kk * a

        w_hi = logw.astype(BF16)
        w_lo = (logw - w_hi.astype(F32)).astype(BF16)
        cum = (jnp.dot(tri, w_hi, preferred_element_type=F32)
               + jnp.dot(tri, w_lo, preferred_element_type=F32))
        cum_last = cum[C - 1:C, :]
        g_inv = jnp.exp(-cum)
        g_end = jnp.exp(cum_last - cum)
        rows.append(dict(rt=zr * jnp.exp(cum), kt=k * g_inv, bt=b * g_inv, qt=kk * jnp.exp(cum - logw),
                         kh=k * g_end, bh=b * g_end, v=zv, g_last=jnp.exp(cum_last), gate=gate,
                         bonus=head_mean(zr * k * rk_ref[...]) * RW_HEAD_DIM * zv))

    chains = [(n, slice(p * PAIR, (p + 1) * PAIR)) for n in range(n_rows) for p in range(n_pair)]
    part = lambda name: [rows[n][name][:, sl] for n, sl in chains]
    qt, rt, kt, bt, kh, bh, v = (part(x) for x in ("qt", "rt", "kt", "bt", "kh", "bh", "v"))
    lhs = [jnp.concatenate([q, r], axis=0).astype(BF16) for q, r in zip(qt, rt)]
    ab = [lax.dot_general(l, _pair_blocks(x).astype(BF16), nt, preferred_element_type=F32)
          for l, x in zip(lhs, bt)]
    ak = [lax.dot_general(l, _pair_blocks(x).astype(BF16), nt, preferred_element_type=F32)
          for l, x in zip(lhs, kt)]
    a_kb = [jnp.where(strict, x[0:C], 0.0) for x in ab]
    a_rb = [jnp.where(incl, x[C:2 * C], 0.0) for x in ab]
    a_kk = [jnp.where(strict, x[0:C], 0.0) for x in ak]
    a_rk = [jnp.where(incl, x[C:2 * C], 0.0) for x in ak]
    t_inv = _unit_lower_inverse(a_kb, sub_mask, eye)

    h = [h_ref[n, sl.start // PAIR] for n, sl in chains]
    vb = [_pair_blocks(x) for x in v]
    rhs = [_bdot(jnp.concatenate([q, akk], axis=1), jnp.concatenate([hh, vv], axis=0))
           for q, akk, hh, vv in zip(qt, a_kk, h, vb)]
    u = [_pmm(t, x) for t, x in zip(t_inv, rhs)]
    outs = [_bdot(jnp.concatenate([r, ark, -arb], axis=1), jnp.concatenate([hh, vv, _pair_blocks(uu)], axis=0))
            for r, ark, arb, hh, vv, uu in zip(rt, a_rk, a_rb, h, vb, u)]
    upd = [_bdot(jnp.concatenate([x, -y], axis=0).T, jnp.concatenate([vv, uu], axis=0))
           for x, y, vv, uu in zip(kh, bh, v, u)]
    for (n, sl), hh, dd in zip(chains, h, upd):
        decay_col = jnp.sum(jnp.where(eye2, rows[n]["g_last"][:, sl], 0.0), axis=1, keepdims=True)
        h_ref[n, sl.start // PAIR] = decay_col * hh + jnp.where(same_head, dd, 0.0)

    for n in range(n_rows):
        o = jnp.concatenate(outs[n * n_pair:(n + 1) * n_pair], axis=1)
        d = o - head_mean(o)
        on = d * lax.rsqrt(head_mean(d * d) + GN_EPS)
        o_ref[n] = (on * lnw_ref[...] + lnb_ref[...] + rows[n]["bonus"]) * rows[n]["gate"]


def _rwkv(zrw, w0, w2, a0, a2, g2, k_k, k_a, r_k, ln_w, ln_b):
    B, S, _ = zrw.shape
    C = RW_CHUNK
    nb = RW_ROWS
    hid = np.arange(PAIR) // RW_HEAD_DIM
    avg = jnp.asarray((hid[:, None] == hid[None, :]).astype(np.float32) / RW_HEAD_DIM)
    vec = lambda a: a.reshape(1, RW_WIDTH)
    fixed = lambda shape: pl.BlockSpec(shape, lambda b, c: (0,) * len(shape))
    return pl.pallas_call(
        _rwkv_kernel,
        grid=(B // nb, S // C),
        in_specs=[pl.BlockSpec((nb, C, RW_IN), lambda b, c: (b, c, 0)),
                  fixed((1, RW_WIDTH)), fixed((DECAY_LORA, RW_WIDTH)),
                  fixed((1, RW_WIDTH)), fixed((AAA_LORA, RW_WIDTH)),
                  fixed((GATE_LORA, RW_WIDTH)), fixed((1, RW_WIDTH)), fixed((1, RW_WIDTH)),
                  fixed((1, RW_WIDTH)), fixed((1, RW_WIDTH)), fixed((1, RW_WIDTH)),
                  fixed((PAIR, PAIR))],
        out_specs=pl.BlockSpec((nb, C, RW_WIDTH), lambda b, c: (b, c, 0)),
        out_shape=jax.ShapeDtypeStruct((B, S, RW_WIDTH), F32),
        scratch_shapes=[pltpu.VMEM((nb, RW_WIDTH // PAIR, PAIR, PAIR), F32)],
        compiler_params=pltpu.CompilerParams(dimension_semantics=("parallel", "arbitrary"),
                                             vmem_limit_bytes=VMEM_LIMIT),
        name="rwkv",
    )(zrw, vec(w0), w2, vec(a0), a2, g2, vec(k_k), vec(k_a), vec(r_k), vec(ln_w), vec(ln_b), avg)


def _gelu_tanh(x):
    return 0.5 * x * (1.0 + jnp.tanh(math.sqrt(2.0 / math.pi) * (x + 0.044715 * (x * x * x))))


def _key_features(pos_hi, pos_lo, block, n, n_sel):
    lane = lax.broadcasted_iota(jnp.int32, (n, NSA_HEAD_DIM), 1)
    feat = jnp.where(lane == n_sel, pos_hi, jnp.where(lane == n_sel + 1, pos_lo, 0.0))
    return feat if block is None else jnp.where(lane == block, 1.0, feat)


def _nsa_prep_kernel(zkc_ref, zvc_ref, zks_ref, zvs_ref, zkw_ref, zvw_ref, pk_ref, pv_ref, kw1_ref,
                     kw2_ref, vw1_ref, vw2_ref, kc_ref, vct_ref, ksa_ref, kwa_ref, vst_ref, vwt_ref):
    S = zkc_ref.shape[1]
    n_grp = S // CMP_STRIDE
    Dh = NSA_HEAD_DIM
    half = CMP_BLOCK // 2
    n_sel = S // SEL_BLOCK
    jrow = lax.broadcasted_iota(jnp.int32, (n_grp, 1), 0)
    cmp_feat = _key_features((jrow >> 3).astype(F32),
                             ((jrow & 7) * CMP_STRIDE).astype(F32) + 0.5 * (CMP_BLOCK - 1),
                             None, n_grp, n_sel)
    for is_v, (z_ref, pos_ref, w1_ref, w2_ref) in enumerate(((zkc_ref, pk_ref, kw1_ref, kw2_ref),
                                                            (zvc_ref, pv_ref, vw1_ref, vw2_ref))):
        for g in range(NSA_KV_HEADS):
            lo = jnp.zeros((n_grp, Dh), F32)
            hi = jnp.zeros((n_grp, Dh), F32)
            for l in range(half):
                xs = z_ref[0, pl.ds(l, n_grp, stride=CMP_STRIDE), :]
                xg = xs[:, g * Dh:(g + 1) * Dh]
                lo = lo + _bdot(xg + pos_ref[l:l + 1, :], w1_ref[l * Dh:(l + 1) * Dh, :])
                hi = hi + _bdot(xg + pos_ref[half + l:half + l + 1, :],
                                w1_ref[(half + l) * Dh:(half + l + 1) * Dh, :])
            pre = lo + pltpu.roll(hi, n_grp - 1, 0)
            out = jnp.where(jrow < n_grp - 1, _bdot(_gelu_tanh(pre), w2_ref[...]), 0.0)
            if is_v:
                out_t = jnp.concatenate([out, jnp.zeros_like(out)], axis=1).T
                vct_ref[0, g] = out_t[0:Dh, :].astype(BF16)
            else:
                kc_ref[0, g] = jnp.concatenate([out, cmp_feat], axis=1)

    prow = lax.broadcasted_iota(jnp.int32, (S, 1), 0)
    p_hi, p_lo = (prow >> 7).astype(F32), (prow & (LANE - 1)).astype(F32)
    for z_ref, out_ref, block in ((zks_ref, ksa_ref, prow >> int(math.log2(SEL_BLOCK))),
                                  (zkw_ref, kwa_ref, None)):
        kfull = z_ref[0]
        key_feat = _key_features(p_hi, p_lo, block, S, n_sel)
        for g in range(NSA_KV_HEADS):
            out_ref[0, g] = jnp.concatenate([kfull[:, g * Dh:(g + 1) * Dh], key_feat], axis=1).astype(BF16)
    ones_row = (lax.broadcasted_iota(jnp.int32, (V_ROWS - Dh, TK), 0) == 0).astype(F32)
    for z_ref, out_ref in ((zvs_ref, vst_ref), (zvw_ref, vwt_ref)):
        for j in range(S // TK):
            vt = z_ref[0, j * TK:(j + 1) * TK, :].T
            out_ref[0, j] = jnp.concatenate(
                [piece for g in range(NSA_KV_HEADS) for piece in (vt[g * Dh:(g + 1) * Dh], ones_row)],
                axis=0).astype(BF16)


def _nsa_prep(zkv, pos_k, pos_v, kw1, kw2, vw1, vw2):
    B, S, _ = zkv.shape
    n_grp = S // CMP_STRIDE
    n_kt = S // TK
    Dh = NSA_HEAD_DIM
    G = NSA_KV_HEADS
    fixed = lambda shape: pl.BlockSpec(shape, lambda b: (0,) * len(shape))
    col = lambda c: pl.BlockSpec((1, S, NSA_KV_WIDTH), lambda b: (b, 0, c))
    whole = lambda shape: pl.BlockSpec((1,) + shape, lambda b: (b,) + (0,) * len(shape))
    shapes = [((G, n_grp, 2 * Dh), F32), ((G, Dh, n_grp), BF16), ((G, S, 2 * Dh), BF16),
              ((G, S, 2 * Dh), BF16), ((n_kt, G * V_ROWS, TK), BF16), ((n_kt, G * V_ROWS, TK), BF16)]
    return pl.pallas_call(
        _nsa_prep_kernel,
        grid=(B,),
        in_specs=[col(c) for c in range(6)] + [
            fixed((CMP_BLOCK, Dh)), fixed((CMP_BLOCK, Dh)),
            fixed((CMP_BLOCK * Dh, Dh)), fixed((Dh, Dh)),
            fixed((CMP_BLOCK * Dh, Dh)), fixed((Dh, Dh))],
        out_specs=[whole(s) for s, _ in shapes],
        out_shape=[jax.ShapeDtypeStruct((B,) + s, d) for s, d in shapes],
        compiler_params=pltpu.CompilerParams(dimension_semantics=("parallel",),
                                             vmem_limit_bytes=VMEM_LIMIT),
        name="nsa_prep",
    )(zkv, zkv, zkv, zkv, zkv, zkv, pos_k, pos_v, kw1, kw2, vw1, vw2)


def _nsa_attn_kernel(q_ref, kc_ref, vct_ref, ksa_ref, kwa_ref, vst_ref, vwt_ref, gl_ref, gb_ref,
                     ovlt_ref, slope_ref, o_ref, acc_ref, ot_ref):
    i = pl.program_id(1)
    Dh = NSA_HEAD_DIM
    R = NSA_REP
    N = R * TQ
    n_cmp_pad = kc_ref.shape[2]
    n_sel = ovlt_ref.shape[0]
    G = NSA_KV_HEADS
    nt = (((1,), (1,)), ((), ()))
    log2e = math.log2(math.e)
    t0 = i * TQ
    t_row = t0 + lax.broadcasted_iota(jnp.int32, (1, TQ), 1)
    c_col = lax.broadcasted_iota(jnp.int32, (TK, 1), 0)
    sgate_t = jax.nn.sigmoid(gl_ref[0] + gb_ref[...]).T
    lane_f = lax.broadcasted_iota(jnp.int32, (1, Dh), 1)
    heads = lambda x: jnp.concatenate([x] * R, axis=1)

    def key_dist(j):
        return t_row - (j * TK + c_col)

    def queries(g, sel_feat):
        parts = []
        for r in range(R):
            h = g * R + r
            sl = slope_ref[:, h:h + 1] * log2e
            feat = jnp.where(lane_f == n_sel, sl * LANE, jnp.where(lane_f == n_sel + 1, sl, sel_feat))
            parts.append(jnp.concatenate(
                [q_ref[0, :, h * Dh:(h + 1) * Dh] * (Dh ** -0.5 * log2e), jnp.broadcast_to(feat, (TQ, Dh))],
                axis=1))
        return jnp.concatenate(parts, axis=0)

    qab, o_cmp = [], []
    for g in range(G):
        jc = lax.broadcasted_iota(jnp.int32, (n_cmp_pad, 1), 0)
        ok_c = (jc * CMP_STRIDE + (CMP_BLOCK - 1) <= t_row) & (jc < n_cmp_pad - 1)
        s_c = (lax.dot_general(kc_ref[0, g], queries(g, 0.0), nt, precision=HI, preferred_element_type=F32)
               + heads(jnp.where(ok_c, 0.0, NEG_INF)))
        e_c = jnp.exp2(s_c - jnp.max(s_c, axis=0, keepdims=True))
        any_c = heads(t_row >= CMP_BLOCK - 1)
        p_c = e_c * jnp.where(any_c, 1.0 / jnp.sum(e_c, axis=0, keepdims=True), 0.0)
        o_cmp.append(jnp.dot(vct_ref[0, g], p_c.astype(BF16), preferred_element_type=F32))

        p_sum = p_c[:, 0:TQ]
        for r in range(1, R):
            p_sum = p_sum + p_c[:, r * TQ:(r + 1) * TQ]
        imp = jnp.dot(ovlt_ref[...], p_sum, precision=HI, preferred_element_type=F32)
        kb = lax.broadcasted_iota(jnp.int32, (n_sel, 1), 0)
        kbf = kb.astype(F32)
        blk_t = t_row >> int(math.log2(SEL_BLOCK))
        forced = (kb == 0) | (kb == blk_t) | (kb == blk_t - 1)
        cur = jnp.where(forced, FORCE_SCORE, jnp.where(kb <= blk_t, imp, -FORCE_SCORE))
        sel_bias = jnp.full((n_sel, TQ), NEG_INF, F32)
        for _ in range(min(N_SELECT, n_sel)):
            mx = jnp.max(cur, axis=0, keepdims=True)
            first = jnp.min(jnp.where(cur == mx, kbf, float(n_sel)), axis=0, keepdims=True)
            hit = kbf == first
            sel_bias = jnp.where(hit, 0.0, sel_bias)
            cur = jnp.where(hit, -3e38, cur)
        sel_feat = jnp.concatenate([sel_bias, jnp.zeros((LANE - n_sel, TQ), F32)], axis=0).T[:, 0:Dh]
        qab.append(queries(g, sel_feat).astype(BF16))

    VR = vst_ref.shape[2] // G

    def tile(j, m, k_ref, vt_ref, bias, slot):
        ks = [k_ref[0, g, pl.ds(pl.multiple_of(j * TK, TK), TK), :] for g in range(G)]
        vts = [vt_ref[0, j, g * VR:(g + 1) * VR, :] for g in range(G)]
        hs = range(NSA_HEADS)
        s = [lax.dot_general(ks[h // R], qab[h // R][(h % R) * TQ:(h % R + 1) * TQ], nt,
                             preferred_element_type=F32) for h in hs]
        if bias is not None:
            s = [x + bias for x in s]
        m_new = [jnp.maximum(m[h], jnp.max(s[h], axis=0, keepdims=True)) for h in hs]
        alpha = [jnp.exp2(m[h] - m_new[h]) for h in hs]
        p = [jnp.exp2(s[h] - m_new[h]).astype(BF16) for h in hs]
        pv = [jnp.dot(vts[h // R], p[h], preferred_element_type=F32) for h in hs]
        for h in hs:
            acc_ref[slot, h] = alpha[h] * acc_ref[slot, h] + pv[h]
        return tuple(m_new)

    def window_bias(j):
        d = key_dist(j)
        return jnp.where((d >= 0) & (d < WINDOW), 0.0, NEG_INF)

    init = (jnp.full((1, TQ), NEG_INF, F32),) * NSA_HEADS
    acc_ref[...] = jnp.zeros_like(acc_ref)
    causal = jnp.where(key_dist(i) >= 0, 0.0, NEG_INF)
    carry = lax.fori_loop(0, i, lambda j, c: tile(j, c, ksa_ref, vst_ref, None, 0), init)
    tile(i, carry, ksa_ref, vst_ref, causal, 0)
    carry = lax.fori_loop(jnp.maximum(i - WINDOW // TK, 0), i,
                          lambda j, c: tile(j, c, kwa_ref, vwt_ref, window_bias(j), 1), init)
    tile(i, carry, kwa_ref, vwt_ref, causal, 1)

    for h in range(NSA_HEADS):
        g, r = divmod(h, R)
        acc_s, acc_w = acc_ref[0, h], acc_ref[1, h]
        ot_ref[h * Dh:(h + 1) * Dh, :] = (
            sgate_t[3 * h:3 * h + 1, :] * o_cmp[g][:, r * TQ:(r + 1) * TQ]
            + sgate_t[3 * h + 1:3 * h + 2, :] * (acc_s[0:Dh] * (1.0 / acc_s[Dh:Dh + 1]))
            + sgate_t[3 * h + 2:3 * h + 3, :] * (acc_w[0:Dh] * (1.0 / acc_w[Dh:Dh + 1])))
    o_ref[0] = ot_ref[...].T


def _nsa_attn(zq, kc, vct, ksa, kwa, vst, vwt, zgate, gate_b):
    B, S, _ = zq.shape
    n_sel = S // SEL_BLOCK
    n_cmp = (S - CMP_BLOCK) // CMP_STRIDE + 1
    n_cmp_pad = kc.shape[2]
    n_kt = S // TK
    G, Dh = NSA_KV_HEADS, NSA_HEAD_DIM
    cmp_start = np.arange(n_cmp) * CMP_STRIDE
    sel_start = np.arange(n_sel) * SEL_BLOCK
    overlap = np.clip(np.minimum(cmp_start[:, None] + CMP_BLOCK, sel_start[None, :] + SEL_BLOCK)
                      - np.maximum(cmp_start[:, None], sel_start[None, :]), 0, None) / CMP_BLOCK
    ovlt = np.zeros((n_sel, n_cmp_pad), np.float32)
    ovlt[:, :n_cmp] = overlap.T
    slopes = (2.0 ** (-8.0 * np.arange(1, NSA_HEADS + 1) / NSA_HEADS)).astype(np.float32).reshape(1, NSA_HEADS)
    gb = jnp.pad(gate_b, (0, LANE - N_GATE)).reshape(1, LANE)
    fixed = lambda shape: pl.BlockSpec(shape, lambda b, i: (0,) * len(shape))
    per_b = lambda shape: pl.BlockSpec((1,) + shape, lambda b, i: (b,) + (0,) * len(shape))
    return pl.pallas_call(
        _nsa_attn_kernel,
        grid=(B, S // TQ),
        in_specs=[pl.BlockSpec((1, TQ, NSA_WIDTH), lambda b, i: (b, i, 0)),
                  per_b((G, n_cmp_pad, 2 * Dh)), per_b((G, Dh, n_cmp_pad)),
                  per_b((G, S, 2 * Dh)), per_b((G, S, 2 * Dh)),
                  per_b((n_kt, G * V_ROWS, TK)), per_b((n_kt, G * V_ROWS, TK)),
                  pl.BlockSpec((1, TQ, LANE), lambda b, i: (b, i, 0)),
                  fixed((1, LANE)), fixed((n_sel, n_cmp_pad)), fixed((1, NSA_HEADS))],
        out_specs=pl.BlockSpec((1, TQ, NSA_WIDTH), lambda b, i: (b, i, 0)),
        out_shape=jax.ShapeDtypeStruct((B, S, NSA_WIDTH), F32),
        scratch_shapes=[pltpu.VMEM((2, NSA_HEADS, V_ROWS, TQ), F32),
                        pltpu.VMEM((NSA_WIDTH, TQ), F32)],
        compiler_params=pltpu.CompilerParams(dimension_semantics=("parallel", "arbitrary"),
                                             vmem_limit_bytes=VMEM_LIMIT),
        name="nsa_attn",
    )(zq, kc, vct, ksa, kwa, vst, vwt, zgate, gb, jnp.asarray(ovlt), jnp.asarray(slopes))


def _mix_kernel(x_ref, yr_ref, yn_ref, zmg_ref, ur_ref, un_ref, wo_ref, gf_ref, wr_ref, br_ref,
                x1_ref, h2_ref, rt_ref, cnt_ref):
    tm = x_ref.shape[0]
    zmg = zmg_ref[...]
    mixed = (jax.nn.sigmoid(zmg[:, :D_MODEL]) * _bdot(yr_ref[...], ur_ref[...])
             + jax.nn.sigmoid(zmg[:, D_MODEL:]) * _bdot(yn_ref[...], un_ref[...]))
    x1 = x_ref[...] + _bdot(mixed, wo_ref[...])
    x1_ref[...] = x1
    h2 = _rms(x1, gf_ref[...])
    h2_ref[...] = h2
    n_row = wr_ref.shape[0]
    logits = lax.dot_general(wr_ref[...], h2, (((1,), (1,)), ((), ())), precision=HI,
                             preferred_element_type=F32) + br_ref[...]
    row = lax.broadcasted_iota(jnp.int32, (n_row, 1), 0).astype(F32)
    gl = jnp.where(row < N_GROUPS, logits, NEG_INF)
    gmax = jnp.max(gl, axis=0, keepdims=True)
    g_sel = jnp.min(jnp.where(gl == gmax, row, float(n_row)), axis=0, keepdims=True)
    p_group = 1.0 / jnp.sum(jnp.exp(gl - gmax), axis=0, keepdims=True)
    e_row = row - N_GROUPS
    in_grp = ((e_row >= g_sel * EXPERTS_PER_GROUP) & (e_row < (g_sel + 1.0) * EXPERTS_PER_GROUP)
              & (e_row < N_EXPERTS))
    el = jnp.where(in_grp, logits, NEG_INF)
    m1 = jnp.max(el, axis=0, keepdims=True)
    i1 = jnp.min(jnp.where(el == m1, e_row, float(n_row)), axis=0, keepdims=True)
    el2 = jnp.where(e_row == i1, 2.0 * NEG_INF, el)
    m2 = jnp.max(el2, axis=0, keepdims=True)
    i2 = jnp.min(jnp.where(el2 == m2, e_row, float(n_row)), axis=0, keepdims=True)
    r2 = jnp.exp(m2 - m1)
    g1 = p_group / (1.0 + r2)
    g2 = p_group * r2 / (1.0 + r2)

    @pl.when(pl.program_id(0) == 0)
    def _():
        cnt_ref[...] = jnp.zeros_like(cnt_ref)

    pick1, pick2 = e_row == i1, e_row == i2
    both = pick1.astype(F32) + pick2.astype(F32)
    earlier = (lax.broadcasted_iota(jnp.int32, (tm, tm), 0)
               < lax.broadcasted_iota(jnp.int32, (tm, tm), 1)).astype(BF16)
    before = jnp.dot(both.astype(BF16), earlier, preferred_element_type=F32) + cnt_ref[:, 0:1]
    rank1 = jnp.sum(jnp.where(pick1, before, 0.0), axis=0, keepdims=True)
    rank2 = jnp.sum(jnp.where(pick2, before, 0.0), axis=0, keepdims=True)
    cnt_ref[...] = cnt_ref[...] + jnp.sum(both, axis=1, keepdims=True)
    rt_ref[...] = jnp.concatenate([i1, i2, g1, g2, rank1, rank2, jnp.zeros((2, tm), F32)], axis=0)


def _mix(x2d, y_rw, y_nsa, zmg, w_up_r, w_up_n, w_out, g_ffn, w_group, b_group, w_router, b_router):
    T = x2d.shape[0]
    tm = TM_PROJ
    n_r = N_GROUPS + N_EXPERTS
    n_row = -(-n_r // 8) * 8
    wr = jnp.pad(jnp.concatenate([w_group, w_router], axis=1).T, ((0, n_row - n_r), (0, 0)))
    br = jnp.pad(jnp.concatenate([b_group, b_router]), (0, n_row - n_r)).reshape(n_row, 1)
    row = lambda i: (i, 0)
    fixed = lambda i: (0, 0)
    return pl.pallas_call(
        _mix_kernel,
        grid=(T // tm,),
        in_specs=[pl.BlockSpec((tm, D_MODEL), row), pl.BlockSpec((tm, RW_WIDTH), row),
                  pl.BlockSpec((tm, NSA_WIDTH), row), pl.BlockSpec((tm, 2 * D_MODEL), row),
                  pl.BlockSpec((RW_WIDTH, D_MODEL), fixed), pl.BlockSpec((NSA_WIDTH, D_MODEL), fixed),
                  pl.BlockSpec((D_MODEL, D_MODEL), fixed), pl.BlockSpec((1, D_MODEL), fixed),
                  pl.BlockSpec((n_row, D_MODEL), fixed), pl.BlockSpec((n_row, 1), fixed)],
        out_specs=[pl.BlockSpec((tm, D_MODEL), row), pl.BlockSpec((tm, D_MODEL), row),
                   pl.BlockSpec((8, tm), lambda i: (0, i)), pl.BlockSpec((n_row, LANE), fixed)],
        out_shape=[jax.ShapeDtypeStruct((T, D_MODEL), F32), jax.ShapeDtypeStruct((T, D_MODEL), F32),
                   jax.ShapeDtypeStruct((8, T), F32), jax.ShapeDtypeStruct((n_row, LANE), F32)],
        compiler_params=pltpu.CompilerParams(dimension_semantics=("arbitrary",),
                                             vmem_limit_bytes=VMEM_LIMIT),
        name="mix",
    )(x2d, y_rw, y_nsa, zmg, w_up_r.astype(BF16), w_up_n.astype(BF16), w_out.astype(BF16),
      g_ffn.reshape(1, D_MODEL), wr, br)


def _route_tables(rt, cnt, T):
    n_rows = T * TOP_K + N_EXPERTS * ROW_BLOCK
    n_blk = n_rows // ROW_BLOCK
    counts = cnt[N_GROUPS:N_GROUPS + N_EXPERTS, 0].astype(jnp.int32)
    padded = (counts + ROW_BLOCK - 1) // ROW_BLOCK * ROW_BLOCK
    pends = jnp.cumsum(padded)
    pstarts = pends - padded
    expert = rt[0:TOP_K].astype(jnp.int32)
    rank = rt[2 * TOP_K:3 * TOP_K].astype(jnp.int32)
    seg_start = jnp.sum(jnp.where(expert[..., None] == jnp.arange(N_EXPERTS), pstarts, 0), axis=-1)
    dest = (seg_start + rank).T.reshape(T * TOP_K)
    gates = rt[TOP_K:2 * TOP_K].T
    blk_start = jnp.arange(n_blk) * ROW_BLOCK
    blk_expert = jnp.minimum(jnp.sum(pends[None, :] <= blk_start[:, None], axis=1), N_EXPERTS - 1)
    blk_nv = jnp.clip(counts[blk_expert] - (blk_start - pstarts[blk_expert]), 0, ROW_BLOCK)
    return dest.astype(jnp.int32), gates, blk_expert.astype(jnp.int32), blk_nv.astype(jnp.int32)


def _row_copies(index_ref, base, n, make):
    for t in range(n):
        for k in range(TOP_K):
            make(base + t, k, index_ref[(base + t) * TOP_K + k]).start()


def _dispatch_kernel(dest_ref, h2_ref, xs_init_hbm, xs_hbm, sem):
    del xs_init_hbm
    i = pl.program_id(0)
    tm = h2_ref.shape[0]

    def row_copy(tok, k, row):
        return pltpu.make_async_copy(h2_ref.at[pl.ds(tok - i * tm, 1)], xs_hbm.at[pl.ds(row, 1)], sem)

    _row_copies(dest_ref, i * tm, tm, row_copy)
    for _ in range(TOP_K):
        pltpu.make_async_copy(h2_ref, xs_hbm.at[pl.ds(0, tm)], sem).wait()


def _dispatch(h2, dest):
    T = h2.shape[0]
    n_rows = T * TOP_K + N_EXPERTS * ROW_BLOCK
    grid_spec = pltpu.PrefetchScalarGridSpec(
        num_scalar_prefetch=1,
        grid=(T // TM_PROJ,),
        in_specs=[pl.BlockSpec((TM_PROJ, D_MODEL), lambda i, d: (i, 0)), pl.BlockSpec(memory_space=pl.ANY)],
        out_specs=pl.BlockSpec(memory_space=pl.ANY),
        scratch_shapes=[pltpu.SemaphoreType.DMA(())])
    return pl.pallas_call(
        _dispatch_kernel,
        grid_spec=grid_spec,
        out_shape=jax.ShapeDtypeStruct((n_rows, D_MODEL), F32),
        input_output_aliases={2: 0},
        compiler_params=pltpu.CompilerParams(dimension_semantics=("arbitrary",),
                                             vmem_limit_bytes=VMEM_LIMIT),
        name="dispatch",
    )(dest, h2, jnp.zeros((n_rows, D_MODEL), F32))


def _expert_kernel(be_ref, nv_ref, xs_ref, wgu_ref, wd_ref, o_ref, wgu_b, wd_b):
    i = pl.program_id(0)
    nv = nv_ref[i]
    changed = jnp.logical_or(i == 0, be_ref[i] != be_ref[jnp.maximum(i - 1, 0)])

    @pl.when(jnp.logical_and(nv > 0, changed))
    def _():
        wgu_b[...] = wgu_ref[0].astype(BF16)
        wd_b[...] = wd_ref[0].astype(BF16)

    @pl.when(nv > 0)
    def _():
        gu = jnp.dot(xs_ref[...].astype(BF16), wgu_b[...], preferred_element_type=F32)
        gate_h, up_h = gu[:, :D_EXPERT], gu[:, D_EXPERT:]
        mid = gate_h * jax.nn.sigmoid(gate_h) * up_h
        o_ref[...] = jnp.dot(mid.astype(BF16), wd_b[...], preferred_element_type=F32)

    @pl.when(nv == 0)
    def _():
        o_ref[...] = jnp.zeros_like(o_ref)


def _experts(xs, blk_expert, blk_nv, w_gate_up, w_down):
    n_rows = xs.shape[0]
    grid_spec = pltpu.PrefetchScalarGridSpec(
        num_scalar_prefetch=2,
        grid=(n_rows // ROW_BLOCK,),
        in_specs=[pl.BlockSpec((ROW_BLOCK, D_MODEL), lambda i, be, nv: (i, 0)),
                  pl.BlockSpec((1, D_MODEL, 2 * D_EXPERT), lambda i, be, nv: (be[i], 0, 0)),
                  pl.BlockSpec((1, D_EXPERT, D_MODEL), lambda i, be, nv: (be[i], 0, 0))],
        out_specs=pl.BlockSpec((ROW_BLOCK, D_MODEL), lambda i, be, nv: (i, 0)),
        scratch_shapes=[pltpu.VMEM((D_MODEL, 2 * D_EXPERT), BF16), pltpu.VMEM((D_EXPERT, D_MODEL), BF16)])
    return pl.pallas_call(
        _expert_kernel,
        grid_spec=grid_spec,
        out_shape=jax.ShapeDtypeStruct((n_rows, D_MODEL), F32),
        compiler_params=pltpu.CompilerParams(dimension_semantics=("arbitrary",),
                                             vmem_limit_bytes=VMEM_LIMIT),
        name="experts",
    )(blk_expert, blk_nv, xs, w_gate_up, w_down)


def _ple_kernel(dest_ref, x1_ref, g_ref, p_ref, rows_hbm, wpp_ref, gpn_ref, ggi_ref, wpg_ref, gfin_ref,
                y_ref, ebuf, sem, *, last_layer):
    i = pl.program_id(0)
    tm = x1_ref.shape[0]
    slot = i % 2

    def fetch(tile, s):
        def row_copy(tok, k, row):
            return pltpu.make_async_copy(rows_hbm.at[pl.ds(row, 1)],
                                         ebuf.at[s, k, pl.ds(tok - tile * tm, 1)], sem.at[s])
        _row_copies(dest_ref, tile * tm, tm, row_copy)

    def drain(s):
        for k in range(TOP_K):
            pltpu.make_async_copy(rows_hbm.at[pl.ds(0, tm)], ebuf.at[s, k], sem.at[s]).wait()

    last = pl.num_programs(0) - 1

    @pl.when(i == 0)
    def _():
        fetch(0, 0)

    fetch(jnp.minimum(i + 1, last), 1 - slot)
    drain(slot)
    moe = g_ref[:, 0:1] * ebuf[slot, 0]
    for k in range(1, TOP_K):
        moe = moe + g_ref[:, k:k + 1] * ebuf[slot, k]
    x2 = x1_ref[...] + moe
    e = _rms(_bdot(p_ref[...], wpp_ref[...]), gpn_ref[...])
    gate = jax.nn.sigmoid(_bdot(_rms(x2, ggi_ref[...]), wpg_ref[...]))
    x3 = x2 + gate * e
    y_ref[...] = _rms(x3, gfin_ref[...]) if last_layer else x3

    @pl.when(i == last)
    def _():
        drain(1 - slot)


def _ple(x1, gates, dest, expert_rows, p2d, w_pp, g_pn, g_gi, w_pg, g_final, last_layer):
    T = x1.shape[0]
    tm = TM_PROJ
    row = lambda i, d: (i, 0)
    fixed = lambda i, d: (0, 0)
    vec = lambda a: a.reshape(1, D_MODEL)
    grid_spec = pltpu.PrefetchScalarGridSpec(
        num_scalar_prefetch=1,
        grid=(T // tm,),
        in_specs=[pl.BlockSpec((tm, D_MODEL), row), pl.BlockSpec((tm, TOP_K), row),
                  pl.BlockSpec((tm, PLE_DIM), row), pl.BlockSpec(memory_space=pl.ANY),
                  pl.BlockSpec((PLE_DIM, D_MODEL), fixed), pl.BlockSpec((1, D_MODEL), fixed),
                  pl.BlockSpec((1, D_MODEL), fixed), pl.BlockSpec((D_MODEL, D_MODEL), fixed),
                  pl.BlockSpec((1, D_MODEL), fixed)],
        out_specs=pl.BlockSpec((tm, D_MODEL), row),
        scratch_shapes=[pltpu.VMEM((2, TOP_K, tm, D_MODEL), F32), pltpu.SemaphoreType.DMA((2,))])
    return pl.pallas_call(
        functools.partial(_ple_kernel, last_layer=last_layer),
        grid_spec=grid_spec,
        out_shape=jax.ShapeDtypeStruct((T, D_MODEL), F32),
        compiler_params=pltpu.CompilerParams(dimension_semantics=("arbitrary",),
                                             vmem_limit_bytes=VMEM_LIMIT),
        name="ple",
    )(dest, x1, gates, p2d, expert_rows, w_pp.astype(BF16), vec(g_pn), vec(g_gi), w_pg.astype(BF16),
      vec(g_final))


def kernel(x, p, g_mix, w_in, mu_shift, rw_w0, rw_w2, rw_a0, rw_a2, rw_g2, rw_k_k, rw_k_a, rw_r_k, rw_ln_w, rw_ln_b, cmp_pos_k, cmp_pos_v, cmp_k_w1, cmp_k_w2, cmp_v_w1, cmp_v_w2, nsa_gate_b, w_up_rwkv, w_up_nsa, w_out, g_ffn, w_group, b_group, w_router, b_router, w_exp_gate_up, w_exp_down, w_ple_proj, g_ple_norm, g_ple_gate_in, w_ple_gate, g_final):
    B, S, D = x.shape
    T = B * S
    depth = p.shape[0]
    xc = x.reshape(T, D)
    for i in range(depth):
        zrw, zq, zkv, zmg, zgate = _proj(xc, g_mix[i], w_in[i], mu_shift[i], S)
        y_rw = _rwkv(zrw.reshape(B, S, RW_IN), rw_w0[i], rw_w2[i], rw_a0[i], rw_a2[i], rw_g2[i],
                     rw_k_k[i], rw_k_a[i], rw_r_k[i], rw_ln_w[i], rw_ln_b[i])
        zkv3 = zkv.reshape(B, S, KV_IN)
        nsa_kv = _nsa_prep(zkv3, cmp_pos_k[i], cmp_pos_v[i], cmp_k_w1[i], cmp_k_w2[i],
                           cmp_v_w1[i], cmp_v_w2[i])
        y_nsa = _nsa_attn(zq.reshape(B, S, NSA_WIDTH), *nsa_kv, zgate.reshape(B, S, LANE),
                          nsa_gate_b[i])
        x1, h2, rt, cnt = _mix(xc, y_rw.reshape(T, RW_WIDTH), y_nsa.reshape(T, NSA_WIDTH), zmg,
                               w_up_rwkv[i], w_up_nsa[i], w_out[i], g_ffn[i], w_group[i], b_group[i],
                               w_router[i], b_router[i])
        dest, gates, blk_expert, blk_nv = _route_tables(rt, cnt, T)
        expert_rows = _experts(_dispatch(h2, dest), blk_expert, blk_nv, w_exp_gate_up[i], w_exp_down[i])
        xc = _ple(x1, gates, dest, expert_rows, p[i].reshape(T, PLE_DIM), w_ple_proj[i], g_ple_norm[i],
                  g_ple_gate_in[i], w_ple_gate[i], g_final, i == depth - 1)
    return xc.reshape(B, S, D)
```

```python
import functools
import math

import jax
import jax.numpy as jnp
import numpy as np
from jax import lax
from jax.experimental import pallas as pl
from jax.experimental.pallas import tpu as pltpu

F32 = jnp.float32
BF16 = jnp.bfloat16
HI = lax.Precision.HIGHEST

D_MODEL = 1024
RW_HEADS = 8
RW_HEAD_DIM = 64
RW_WIDTH = 512
DECAY_LORA = 64
AAA_LORA = 64
GATE_LORA = 128
GN_EPS = 64e-5
RW_IN = 3 * RW_WIDTH + DECAY_LORA + AAA_LORA + GATE_LORA

NSA_HEADS = 8
NSA_KV_HEADS = 2
NSA_REP = NSA_HEADS // NSA_KV_HEADS
NSA_HEAD_DIM = 64
NSA_WIDTH = 512
NSA_KV_WIDTH = 128
CMP_BLOCK = 32
CMP_STRIDE = 16
SEL_BLOCK = 64
N_SELECT = 8
WINDOW = 512
N_NSA_BRANCH = 3
FORCE_SCORE = 1e6
NEG_INF = -1e30

N_GROUPS = 4
EXPERTS_PER_GROUP = 8
N_EXPERTS = 32
TOP_K = 2
D_EXPERT = 512
ROW_BLOCK = 256
PLE_DIM = 256
NORM_EPS = 1e-6

N_GATE = N_NSA_BRANCH * NSA_HEADS
ATT_IN = NSA_WIDTH + 6 * NSA_KV_WIDTH
KV_OFF = RW_IN + NSA_WIDTH
KV_IN = 6 * NSA_KV_WIDTH
GATE_OFF = RW_IN + ATT_IN
MERGE_OFF = GATE_OFF + N_GATE
LANE = 128

RW_CHUNK = 64
RW_SUB = 16
RW_ROWS = 4
TQ = 256
TK = 256
V_ROWS = NSA_HEAD_DIM + 16
TM_PROJ = 256
MIX_CHUNKS = 2
VMEM_LIMIT = 56 * 1024 * 1024


def _bdot(a, b):
    return jnp.dot(a.astype(BF16), b.astype(BF16), preferred_element_type=F32)


def _bdot_nt(a, b):
    return lax.dot_general(a.astype(BF16), b.astype(BF16), (((1,), (1,)), ((), ())),
                           preferred_element_type=F32)


def _rms(x, g):
    return x * lax.rsqrt(jnp.mean(x * x, axis=-1, keepdims=True) + NORM_EPS) * g


def _proj_kernel(x_ref, g_ref, w_ref, mu_ref, zrw_ref, zq_ref, zkv_ref, zmg_ref, zgate_ref,
                 carry_ref, *, tiles_per_seq):
    i = pl.program_id(0)
    tm = x_ref.shape[0]

    @pl.when(i % tiles_per_seq == 0)
    def _():
        carry_ref[...] = jnp.zeros_like(carry_ref)

    h = _rms(x_ref[...], g_ref[...]).astype(BF16)
    z = jnp.dot(h, w_ref[:, 0:RW_IN], preferred_element_type=F32)
    row = lax.broadcasted_iota(jnp.int32, (tm, 1), 0)
    prev = jnp.where(row == 0, carry_ref[7:8, :], pltpu.roll(z, 1, 0))
    carry_ref[...] = z[tm - 8:tm, :]
    zrw_ref[...] = z + (prev - z) * mu_ref[...]
    zq_ref[...] = jnp.dot(h, w_ref[:, RW_IN:KV_OFF], preferred_element_type=F32)
    zkv_ref[...] = jnp.dot(h, w_ref[:, KV_OFF:GATE_OFF], preferred_element_type=F32)
    zmg_ref[...] = jnp.dot(h, w_ref[:, GATE_OFF:GATE_OFF + 2 * D_MODEL], preferred_element_type=F32)
    zgate_ref[...] = jnp.dot(h, w_ref[:, GATE_OFF + 2 * D_MODEL:], preferred_element_type=F32)


def _proj(x2d, g_mix, w_in, mu, seq):
    T = x2d.shape[0]
    tm = TM_PROJ
    wp = jnp.concatenate(
        [w_in[:, :GATE_OFF], w_in[:, MERGE_OFF:],
         jnp.pad(w_in[:, GATE_OFF:MERGE_OFF], ((0, 0), (0, LANE - N_GATE)))], axis=1).astype(BF16)
    npad = wp.shape[1]
    row = lambda i: (i, 0)
    fixed = lambda i: (0, 0)
    return pl.pallas_call(
        functools.partial(_proj_kernel, tiles_per_seq=seq // tm),
        grid=(T // tm,),
        in_specs=[pl.BlockSpec((tm, D_MODEL), row), pl.BlockSpec((1, D_MODEL), fixed),
                  pl.BlockSpec((D_MODEL, npad), fixed), pl.BlockSpec((1, RW_IN), fixed)],
        out_specs=[pl.BlockSpec((tm, RW_IN), row), pl.BlockSpec((tm, NSA_WIDTH), row),
                   pl.BlockSpec((tm, KV_IN), row), pl.BlockSpec((tm, 2 * D_MODEL), row),
                   pl.BlockSpec((tm, LANE), row)],
        out_shape=[jax.ShapeDtypeStruct((T, RW_IN), F32), jax.ShapeDtypeStruct((T, NSA_WIDTH), F32),
                   jax.ShapeDtypeStruct((T, KV_IN), F32), jax.ShapeDtypeStruct((T, 2 * D_MODEL), F32),
                   jax.ShapeDtypeStruct((T, LANE), F32)],
        scratch_shapes=[pltpu.VMEM((8, RW_IN), F32)],
        compiler_params=pltpu.CompilerParams(dimension_semantics=("arbitrary",),
                                             vmem_limit_bytes=VMEM_LIMIT),
        name="proj",
    )(x2d, g_mix.reshape(1, D_MODEL), wp, mu.reshape(1, RW_IN))


PAIR = 2 * RW_HEAD_DIM


def _pair_blocks(x):
    low = lax.broadcasted_iota(jnp.int32, (1, PAIR), 1) < RW_HEAD_DIM
    return jnp.concatenate([jnp.where(low, x, 0.0), jnp.where(low, 0.0, x)], axis=0)


def _pmm(a, b):
    return _bdot(a, _pair_blocks(b))


def _unit_lower_inverse(a_strict, sub_mask, eye):
    ad = [jnp.where(sub_mask, a, 0.0) for a in a_strict]
    ao = [a - d for a, d in zip(a_strict, ad)]
    td = [eye - d for d in ad]
    pw = ad
    for _ in range(int(math.log2(RW_SUB)) - 1):
        pw = [_pmm(x, x) for x in pw]
        td = [_pmm(t, eye + x) for t, x in zip(td, pw)]
    n = [_pmm(t, o) for t, o in zip(td, ao)]
    t = [eye - x for x in n]
    pw = n
    for _ in range(int(math.log2(RW_CHUNK // RW_SUB)) - 1):
        pw = [_pmm(x, x) for x in pw]
        t = [_pmm(a, eye + x) for a, x in zip(t, pw)]
    return [_pmm(a, d) for a, d in zip(t, td)]


def _rwkv_kernel(z_ref, w0_ref, w2_ref, a0_ref, a2_ref, g2_ref, kk_ref, ka_ref, rk_ref, lnw_ref,
                 lnb_ref, avg_ref, o_ref, h_ref):
    c = pl.program_id(1)
    C = RW_CHUNK
    n_pair = RW_WIDTH // PAIR
    nt = (((1,), (1,)), ((), ()))

    @pl.when(c == 0)
    def _():
        h_ref[...] = jnp.zeros_like(h_ref)

    ti = lax.broadcasted_iota(jnp.int32, (C, 1), 0)
    si = lax.broadcasted_iota(jnp.int32, (1, PAIR), 1) & (RW_HEAD_DIM - 1)
    incl, strict = ti >= si, ti > si
    eye = (ti == si).astype(F32)
    sub_shift = int(math.log2(RW_SUB))
    sub_mask = (ti >> sub_shift) == (si >> sub_shift)
    row2 = lax.broadcasted_iota(jnp.int32, (PAIR, 1), 0)
    col2 = lax.broadcasted_iota(jnp.int32, (1, PAIR), 1)
    same_head = (row2 < RW_HEAD_DIM) == (col2 < RW_HEAD_DIM)
    eye2 = row2 == col2
    tri = (lax.broadcasted_iota(jnp.int32, (C, C), 0)
           >= lax.broadcasted_iota(jnp.int32, (C, C), 1)).astype(BF16)

    def head_mean(x):
        xs = jnp.concatenate([x[:, p * PAIR:(p + 1) * PAIR] for p in range(n_pair)], axis=0)
        ms = _bdot(xs, avg_ref[...])
        return jnp.concatenate([ms[p * C:(p + 1) * C] for p in range(n_pair)], axis=1)

    n_rows = z_ref.shape[0]
    rows = []
    for n in range(n_rows):
        z = z_ref[n]
        zr, zk, zv = z[:, 0:512], z[:, 512:1024], z[:, 1024:1536]
        zw, za, zg = z[:, 1536:1600], z[:, 1600:1664], z[:, 1664:1792]
        w_raw = w0_ref[...] + _bdot(jnp.tanh(zw), w2_ref[...])
        logw = -jax.nn.sigmoid(w_raw) * math.exp(-0.5)
        a = jax.nn.sigmoid(a0_ref[...] + _bdot(za, a2_ref[...]))
        gate = _bdot(jax.nn.sigmoid(zg), g2_ref[...])
        kk = zk * kk_ref[...]
        kk = kk / jnp.maximum(jnp.sqrt(head_mean(kk * kk) * RW_HEAD_DIM), 1e-12)
        k = zk * (1.0 + (a - 1.0) * ka_ref[...])
        b = kk * a

        w_hi = logw.astype(BF16)
        w_lo = (logw - w_hi.astype(F32)).astype(BF16)
        cum = (jnp.dot(tri, w_hi, preferred_element_type=F32)
               + jnp.dot(tri, w_lo, preferred_element_type=F32))
        cum_last = cum[C - 1:C, :]
        g_inv = jnp.exp(-cum)
        g_end = jnp.exp(cum_last - cum)
        rows.append(dict(rt=zr * jnp.exp(cum), kt=k * g_inv, bt=b * g_inv, qt=kk * jnp.exp(cum - logw),
                         kh=k * g_end, bh=b * g_end, v=zv, g_last=jnp.exp(cum_last), gate=gate,
                         bonus=head_mean(zr * k * rk_ref[...]) * RW_HEAD_DIM * zv))

    chains = [(n, slice(p * PAIR, (p + 1) * PAIR)) for n in range(n_rows) for p in range(n_pair)]
    part = lambda name: [rows[n][name][:, sl] for n, sl in chains]
    qt, rt, kt, bt, kh, bh, v = (part(x) for x in ("qt", "rt", "kt", "bt", "kh", "bh", "v"))
    lhs = [jnp.concatenate([q, r], axis=0).astype(BF16) for q, r in zip(qt, rt)]
    ab = [lax.dot_general(l, _pair_blocks(x).astype(BF16), nt, preferred_element_type=F32)
          for l, x in zip(lhs, bt)]
    ak = [lax.dot_general(l, _pair_blocks(x).astype(BF16), nt, preferred_element_type=F32)
          for l, x in zip(lhs, kt)]
    a_kb = [jnp.where(strict, x[0:C], 0.0) for x in ab]
    a_rb = [jnp.where(incl, x[C:2 * C], 0.0) for x in ab]
    a_kk = [jnp.where(strict, x[0:C], 0.0) for x in ak]
    a_rk = [jnp.where(incl, x[C:2 * C], 0.0) for x in ak]
    t_inv = _unit_lower_inverse(a_kb, sub_mask, eye)

    h = [h_ref[n, sl.start // PAIR] for n, sl in chains]
    vb = [_pair_blocks(x) for x in v]
    rhs = [_bdot(jnp.concatenate([q, akk], axis=1), jnp.concatenate([hh, vv], axis=0))
           for q, akk, hh, vv in zip(qt, a_kk, h, vb)]
    u = [_pmm(t, x) for t, x in zip(t_inv, rhs)]
    outs = [_bdot(jnp.concatenate([r, ark, -arb], axis=1), jnp.concatenate([hh, vv, _pair_blocks(uu)], axis=0))
            for r, ark, arb, hh, vv, uu in zip(rt, a_rk, a_rb, h, vb, u)]
    upd = [_bdot(jnp.concatenate([x, -y], axis=0).T, jnp.concatenate([vv, uu], axis=0))
           for x, y, vv, uu in zip(kh, bh, v, u)]
    for (n, sl), hh, dd in zip(chains, h, upd):
        decay_col = jnp.sum(jnp.where(eye2, rows[n]["g_last"][:, sl], 0.0), axis=1, keepdims=True)
        h_ref[n, sl.start // PAIR] = decay_col * hh + jnp.where(same_head, dd, 0.0)

    for n in range(n_rows):
        o = jnp.concatenate(outs[n * n_pair:(n + 1) * n_pair], axis=1)
        d = o - head_mean(o)
        on = d * lax.rsqrt(head_mean(d * d) + GN_EPS)
        o_ref[n] = (on * lnw_ref[...] + lnb_ref[...] + rows[n]["bonus"]) * rows[n]["gate"]


def _rwkv(zrw, w0, w2, a0, a2, g2, k_k, k_a, r_k, ln_w, ln_b):
    B, S, _ = zrw.shape
    C = RW_CHUNK
    nb = RW_ROWS
    hid = np.arange(PAIR) // RW_HEAD_DIM
    avg = jnp.asarray((hid[:, None] == hid[None, :]).astype(np.float32) / RW_HEAD_DIM)
    vec = lambda a: a.reshape(1, RW_WIDTH)
    fixed = lambda shape: pl.BlockSpec(shape, lambda b, c: (0,) * len(shape))
    return pl.pallas_call(
        _rwkv_kernel,
        grid=(B // nb, S // C),
        in_specs=[pl.BlockSpec((nb, C, RW_IN), lambda b, c: (b, c, 0)),
                  fixed((1, RW_WIDTH)), fixed((DECAY_LORA, RW_WIDTH)),
                  fixed((1, RW_WIDTH)), fixed((AAA_LORA, RW_WIDTH)),
                  fixed((GATE_LORA, RW_WIDTH)), fixed((1, RW_WIDTH)), fixed((1, RW_WIDTH)),
                  fixed((1, RW_WIDTH)), fixed((1, RW_WIDTH)), fixed((1, RW_WIDTH)),
                  fixed((PAIR, PAIR))],
        out_specs=pl.BlockSpec((nb, C, RW_WIDTH), lambda b, c: (b, c, 0)),
        out_shape=jax.ShapeDtypeStruct((B, S, RW_WIDTH), F32),
        scratch_shapes=[pltpu.VMEM((nb, RW_WIDTH // PAIR, PAIR, PAIR), F32)],
        compiler_params=pltpu.CompilerParams(dimension_semantics=("parallel", "arbitrary"),
                                             vmem_limit_bytes=VMEM_LIMIT),
        name="rwkv",
    )(zrw, vec(w0), w2, vec(a0), a2, g2, vec(k_k), vec(k_a), vec(r_k), vec(ln_w), vec(ln_b), avg)


def _gelu_tanh(x):
    return 0.5 * x * (1.0 + jnp.tanh(math.sqrt(2.0 / math.pi) * (x + 0.044715 * (x * x * x))))


def _key_features(pos_hi, pos_lo, block, n, n_sel):
    lane = lax.broadcasted_iota(jnp.int32, (n, NSA_HEAD_DIM), 1)
    feat = jnp.where(lane == n_sel, pos_hi, jnp.where(lane == n_sel + 1, pos_lo, 0.0))
    return feat if block is None else jnp.where(lane == block, 1.0, feat)


def _nsa_prep_kernel(zkc_ref, zvc_ref, zks_ref, zvs_ref, zkw_ref, zvw_ref, pk_ref, pv_ref, kw1_ref,
                     kw2_ref, vw1_ref, vw2_ref, kc_ref, vct_ref, ksa_ref, kwa_ref, vst_ref, vwt_ref):
    S = zkc_ref.shape[1]
    n_grp = S // CMP_STRIDE
    Dh = NSA_HEAD_DIM
    half = CMP_BLOCK // 2
    n_sel = S // SEL_BLOCK
    jrow = lax.broadcasted_iota(jnp.int32, (n_grp, 1), 0)
    cmp_feat = _key_features((jrow >> 3).astype(F32),
                             ((jrow & 7) * CMP_STRIDE).astype(F32) + 0.5 * (CMP_BLOCK - 1),
                             None, n_grp, n_sel)
    for is_v, (z_ref, pos_ref, w1_ref, w2_ref) in enumerate(((zkc_ref, pk_ref, kw1_ref, kw2_ref),
                                                            (zvc_ref, pv_ref, vw1_ref, vw2_ref))):
        for g in range(NSA_KV_HEADS):
            lo = jnp.zeros((n_grp, Dh), F32)
            hi = jnp.zeros((n_grp, Dh), F32)
            for l in range(half):
                xs = z_ref[0, pl.ds(l, n_grp, stride=CMP_STRIDE), :]
                xg = xs[:, g * Dh:(g + 1) * Dh]
                lo = lo + _bdot(xg + pos_ref[l:l + 1, :], w1_ref[l * Dh:(l + 1) * Dh, :])
                hi = hi + _bdot(xg + pos_ref[half + l:half + l + 1, :],
                                w1_ref[(half + l) * Dh:(half + l + 1) * Dh, :])
            pre = lo + pltpu.roll(hi, n_grp - 1, 0)
            out = jnp.where(jrow < n_grp - 1, _bdot(_gelu_tanh(pre), w2_ref[...]), 0.0)
            if is_v:
                out_t = jnp.concatenate([out, jnp.zeros_like(out)], axis=1).T
                vct_ref[0, g] = out_t[0:Dh, :].astype(BF16)
            else:
                kc_ref[0, g] = jnp.concatenate([out, cmp_feat], axis=1)

    prow = lax.broadcasted_iota(jnp.int32, (S, 1), 0)
    p_hi, p_lo = (prow >> 7).astype(F32), (prow & (LANE - 1)).astype(F32)
    for z_ref, out_ref, block in ((zks_ref, ksa_ref, prow >> int(math.log2(SEL_BLOCK))),
                                  (zkw_ref, kwa_ref, None)):
        kfull = z_ref[0]
        key_feat = _key_features(p_hi, p_lo, block, S, n_sel)
        for g in range(NSA_KV_HEADS):
            out_ref[0, g] = jnp.concatenate([kfull[:, g * Dh:(g + 1) * Dh], key_feat], axis=1).astype(BF16)
    ones_row = (lax.broadcasted_iota(jnp.int32, (V_ROWS - Dh, TK), 0) == 0).astype(F32)
    for z_ref, out_ref in ((zvs_ref, vst_ref), (zvw_ref, vwt_ref)):
        for j in range(S // TK):
            vt = z_ref[0, j * TK:(j + 1) * TK, :].T
            out_ref[0, j] = jnp.concatenate(
                [piece for g in range(NSA_KV_HEADS) for piece in (vt[g * Dh:(g + 1) * Dh], ones_row)],
                axis=0).astype(BF16)


def _nsa_prep(zkv, pos_k, pos_v, kw1, kw2, vw1, vw2):
    B, S, _ = zkv.shape
    n_grp = S // CMP_STRIDE
    n_kt = S // TK
    Dh = NSA_HEAD_DIM
    G = NSA_KV_HEADS
    fixed = lambda shape: pl.BlockSpec(shape, lambda b: (0,) * len(shape))
    col = lambda c: pl.BlockSpec((1, S, NSA_KV_WIDTH), lambda b: (b, 0, c))
    whole = lambda shape: pl.BlockSpec((1,) + shape, lambda b: (b,) + (0,) * len(shape))
    shapes = [((G, n_grp, 2 * Dh), F32), ((G, Dh, n_grp), BF16), ((G, S, 2 * Dh), BF16),
              ((G, S, 2 * Dh), BF16), ((n_kt, G * V_ROWS, TK), BF16), ((n_kt, G * V_ROWS, TK), BF16)]
    return pl.pallas_call(
        _nsa_prep_kernel,
        grid=(B,),
        in_specs=[col(c) for c in range(6)] + [
            fixed((CMP_BLOCK, Dh)), fixed((CMP_BLOCK, Dh)),
            fixed((CMP_BLOCK * Dh, Dh)), fixed((Dh, Dh)),
            fixed((CMP_BLOCK * Dh, Dh)), fixed((Dh, Dh))],
        out_specs=[whole(s) for s, _ in shapes],
        out_shape=[jax.ShapeDtypeStruct((B,) + s, d) for s, d in shapes],
        compiler_params=pltpu.CompilerParams(dimension_semantics=("parallel",),
                                             vmem_limit_bytes=VMEM_LIMIT),
        name="nsa_prep",
    )(zkv, zkv, zkv, zkv, zkv, zkv, pos_k, pos_v, kw1, kw2, vw1, vw2)


def _nsa_attn_kernel(q_ref, kc_ref, vct_ref, ksa_ref, kwa_ref, vst_ref, vwt_ref, gl_ref, gb_ref,
                     ovlt_ref, slope_ref, o_ref, acc_ref, ot_ref):
    i = pl.program_id(1)
    Dh = NSA_HEAD_DIM
    R = NSA_REP
    N = R * TQ
    n_cmp_pad = kc_ref.shape[2]
    n_sel = ovlt_ref.shape[0]
    G = NSA_KV_HEADS
    nt = (((1,), (1,)), ((), ()))
    log2e = math.log2(math.e)
    t0 = i * TQ
    t_row = t0 + lax.broadcasted_iota(jnp.int32, (1, TQ), 1)
    c_col = lax.broadcasted_iota(jnp.int32, (TK, 1), 0)
    sgate_t = jax.nn.sigmoid(gl_ref[0] + gb_ref[...]).T
    lane_f = lax.broadcasted_iota(jnp.int32, (1, Dh), 1)
    heads = lambda x: jnp.concatenate([x] * R, axis=1)

    def key_dist(j):
        return t_row - (j * TK + c_col)

    def queries(g, sel_feat):
        parts = []
        for r in range(R):
            h = g * R + r
            sl = slope_ref[:, h:h + 1] * log2e
            feat = jnp.where(lane_f == n_sel, sl * LANE, jnp.where(lane_f == n_sel + 1, sl, sel_feat))
            parts.append(jnp.concatenate(
                [q_ref[0, :, h * Dh:(h + 1) * Dh] * (Dh ** -0.5 * log2e), jnp.broadcast_to(feat, (TQ, Dh))],
                axis=1))
        return jnp.concatenate(parts, axis=0)

    qab, o_cmp = [], []
    for g in range(G):
        jc = lax.broadcasted_iota(jnp.int32, (n_cmp_pad, 1), 0)
        ok_c = (jc * CMP_STRIDE + (CMP_BLOCK - 1) <= t_row) & (jc < n_cmp_pad - 1)
        s_c = (lax.dot_general(kc_ref[0, g], queries(g, 0.0), nt, precision=HI, preferred_element_type=F32)
               + heads(jnp.where(ok_c, 0.0, NEG_INF)))
        e_c = jnp.exp2(s_c - jnp.max(s_c, axis=0, keepdims=True))
        any_c = heads(t_row >= CMP_BLOCK - 1)
        p_c = e_c * jnp.where(any_c, 1.0 / jnp.sum(e_c, axis=0, keepdims=True), 0.0)
        o_cmp.append(jnp.dot(vct_ref[0, g], p_c.astype(BF16), preferred_element_type=F32))

        p_sum = p_c[:, 0:TQ]
        for r in range(1, R):
            p_sum = p_sum + p_c[:, r * TQ:(r + 1) * TQ]
        imp = jnp.dot(ovlt_ref[...], p_sum, precision=HI, preferred_element_type=F32)
        kb = lax.broadcasted_iota(jnp.int32, (n_sel, 1), 0)
        kbf = kb.astype(F32)
        blk_t = t_row >> int(math.log2(SEL_BLOCK))
        forced = (kb == 0) | (kb == blk_t) | (kb == blk_t - 1)
        cur = jnp.where(forced, FORCE_SCORE, jnp.where(kb <= blk_t, imp, -FORCE_SCORE))
        sel_bias = jnp.full((n_sel, TQ), NEG_INF, F32)
        for _ in range(min(N_SELECT, n_sel)):
            mx = jnp.max(cur, axis=0, keepdims=True)
            first = jnp.min(jnp.where(cur == mx, kbf, float(n_sel)), axis=0, keepdims=True)
            hit = kbf == first
            sel_bias = jnp.where(hit, 0.0, sel_bias)
            cur = jnp.where(hit, -3e38, cur)
        sel_feat = jnp.concatenate([sel_bias, jnp.zeros((LANE - n_sel, TQ), F32)], axis=0).T[:, 0:Dh]
        qab.append(queries(g, sel_feat).astype(BF16))

    VR = vst_ref.shape[2] // G

    def tile(j, m, k_ref, vt_ref, bias, slot):
        ks = [k_ref[0, g, pl.ds(pl.multiple_of(j * TK, TK), TK), :] for g in range(G)]
        vts = [vt_ref[0, j, g * VR:(g + 1) * VR, :] for g in range(G)]
        hs = range(NSA_HEADS)
        s = [lax.dot_general(ks[h // R], qab[h // R][(h % R) * TQ:(h % R + 1) * TQ], nt,
                             preferred_element_type=F32) for h in hs]
        if bias is not None:
            s = [x + bias for x in s]
        m_new = [jnp.maximum(m[h], jnp.max(s[h], axis=0, keepdims=True)) for h in hs]
        alpha = [jnp.exp2(m[h] - m_new[h]) for h in hs]
        p = [jnp.exp2(s[h] - m_new[h]).astype(BF16) for h in hs]
        pv = [jnp.dot(vts[h // R], p[h], preferred_element_type=F32) for h in hs]
        for h in hs:
            acc_ref[slot, h] = alpha[h] * acc_ref[slot, h] + pv[h]
        return tuple(m_new)

    def window_bias(j):
        d = key_dist(j)
        return jnp.where((d >= 0) & (d < WINDOW), 0.0, NEG_INF)

    init = (jnp.full((1, TQ), NEG_INF, F32),) * NSA_HEADS
    acc_ref[...] = jnp.zeros_like(acc_ref)
    causal = jnp.where(key_dist(i) >= 0, 0.0, NEG_INF)
    carry = lax.fori_loop(0, i, lambda j, c: tile(j, c, ksa_ref, vst_ref, None, 0), init)
    tile(i, carry, ksa_ref, vst_ref, causal, 0)
    carry = lax.fori_loop(jnp.maximum(i - WINDOW // TK, 0), i,
                          lambda j, c: tile(j, c, kwa_ref, vwt_ref, window_bias(j), 1), init)
    tile(i, carry, kwa_ref, vwt_ref, causal, 1)

    for h in range(NSA_HEADS):
        g, r = divmod(h, R)
        acc_s, acc_w = acc_ref[0, h], acc_ref[1, h]
        ot_ref[h * Dh:(h + 1) * Dh, :] = (
            sgate_t[3 * h:3 * h + 1, :] * o_cmp[g][:, r * TQ:(r + 1) * TQ]
            + sgate_t[3 * h + 1:3 * h + 2, :] * (acc_s[0:Dh] * (1.0 / acc_s[Dh:Dh + 1]))
            + sgate_t[3 * h + 2:3 * h + 3, :] * (acc_w[0:Dh] * (1.0 / acc_w[Dh:Dh + 1])))
    o_ref[0] = ot_ref[...].T


def _nsa_attn(zq, kc, vct, ksa, kwa, vst, vwt, zgate, gate_b):
    B, S, _ = zq.shape
    n_sel = S // SEL_BLOCK
    n_cmp = (S - CMP_BLOCK) // CMP_STRIDE + 1
    n_cmp_pad = kc.shape[2]
    n_kt = S // TK
    G, Dh = NSA_KV_HEADS, NSA_HEAD_DIM
    cmp_start = np.arange(n_cmp) * CMP_STRIDE
    sel_start = np.arange(n_sel) * SEL_BLOCK
    overlap = np.clip(np.minimum(cmp_start[:, None] + CMP_BLOCK, sel_start[None, :] + SEL_BLOCK)
                      - np.maximum(cmp_start[:, None], sel_start[None, :]), 0, None) / CMP_BLOCK
    ovlt = np.zeros((n_sel, n_cmp_pad), np.float32)
    ovlt[:, :n_cmp] = overlap.T
    slopes = (2.0 ** (-8.0 * np.arange(1, NSA_HEADS + 1) / NSA_HEADS)).astype(np.float32).reshape(1, NSA_HEADS)
    gb = jnp.pad(gate_b, (0, LANE - N_GATE)).reshape(1, LANE)
    fixed = lambda shape: pl.BlockSpec(shape, lambda b, i: (0,) * len(shape))
    per_b = lambda shape: pl.BlockSpec((1,) + shape, lambda b, i: (b,) + (0,) * len(shape))
    return pl.pallas_call(
        _nsa_attn_kernel,
        grid=(B, S // TQ),
        in_specs=[pl.BlockSpec((1, TQ, NSA_WIDTH), lambda b, i: (b, i, 0)),
                  per_b((G, n_cmp_pad, 2 * Dh)), per_b((G, Dh, n_cmp_pad)),
                  per_b((G, S, 2 * Dh)), per_b((G, S, 2 * Dh)),
                  per_b((n_kt, G * V_ROWS, TK)), per_b((n_kt, G * V_ROWS, TK)),
                  pl.BlockSpec((1, TQ, LANE), lambda b, i: (b, i, 0)),
                  fixed((1, LANE)), fixed((n_sel, n_cmp_pad)), fixed((1, NSA_HEADS))],
        out_specs=pl.BlockSpec((1, TQ, NSA_WIDTH), lambda b, i: (b, i, 0)),
        out_shape=jax.ShapeDtypeStruct((B, S, NSA_WIDTH), F32),
        scratch_shapes=[pltpu.VMEM((2, NSA_HEADS, V_ROWS, TQ), F32),
                        pltpu.VMEM((NSA_WIDTH, TQ), F32)],
        compiler_params=pltpu.CompilerParams(dimension_semantics=("parallel", "arbitrary"),
                                             vmem_limit_bytes=VMEM_LIMIT),
        name="nsa_attn",
    )(zq, kc, vct, ksa, kwa, vst, vwt, zgate, gb, jnp.asarray(ovlt), jnp.asarray(slopes))


def _mix_kernel(x_ref, yr_ref, yn_ref, zmg_ref, ur_ref, un_ref, wo_ref, gf_ref, wr_ref, br_ref,
                x1_ref, h2_ref, rt_ref, cnt_ref):
    tm = x_ref.shape[0]
    nt = (((1,), (1,)), ((), ()))
    chunks = [slice(c * TM_PROJ, (c + 1) * TM_PROJ) for c in range(tm // TM_PROJ)]
    up_r = [_bdot(yr_ref[c, :], ur_ref[...]) for c in chunks]
    up_n = [_bdot(yn_ref[c, :], un_ref[...]) for c in chunks]
    mixed = [jax.nn.sigmoid(zmg_ref[c, 0:D_MODEL]) * a + jax.nn.sigmoid(zmg_ref[c, D_MODEL:2 * D_MODEL]) * b
             for c, a, b in zip(chunks, up_r, up_n)]
    x1 = [x_ref[c, :] + _bdot(m, wo_ref[...]) for c, m in zip(chunks, mixed)]
    h2 = [_rms(v, gf_ref[...]) for v in x1]
    for c, v, h in zip(chunks, x1, h2):
        x1_ref[c, :] = v
        h2_ref[c, :] = h
    n_row = wr_ref.shape[0]
    logits = jnp.concatenate(
        [lax.dot_general(wr_ref[...], h, nt, precision=HI, preferred_element_type=F32) for h in h2],
        axis=1) + br_ref[...]
    row = lax.broadcasted_iota(jnp.int32, (n_row, 1), 0).astype(F32)
    gl = jnp.where(row < N_GROUPS, logits, NEG_INF)
    gmax = jnp.max(gl, axis=0, keepdims=True)
    g_sel = jnp.min(jnp.where(gl == gmax, row, float(n_row)), axis=0, keepdims=True)
    p_group = 1.0 / jnp.sum(jnp.exp(gl - gmax), axis=0, keepdims=True)
    e_row = row - N_GROUPS
    in_grp = ((e_row >= g_sel * EXPERTS_PER_GROUP) & (e_row < (g_sel + 1.0) * EXPERTS_PER_GROUP)
              & (e_row < N_EXPERTS))
    el = jnp.where(in_grp, logits, NEG_INF)
    m1 = jnp.max(el, axis=0, keepdims=True)
    i1 = jnp.min(jnp.where(el == m1, e_row, float(n_row)), axis=0, keepdims=True)
    el2 = jnp.where(e_row == i1, 2.0 * NEG_INF, el)
    m2 = jnp.max(el2, axis=0, keepdims=True)
    i2 = jnp.min(jnp.where(el2 == m2, e_row, float(n_row)), axis=0, keepdims=True)
    r2 = jnp.exp(m2 - m1)
    g1 = p_group / (1.0 + r2)
    g2 = p_group * r2 / (1.0 + r2)

    @pl.when(pl.program_id(0) == 0)
    def _():
        cnt_ref[...] = jnp.zeros_like(cnt_ref)

    pick1, pick2 = e_row == i1, e_row == i2
    both = pick1.astype(F32) + pick2.astype(F32)
    earlier = (lax.broadcasted_iota(jnp.int32, (tm, tm), 0)
               < lax.broadcasted_iota(jnp.int32, (tm, tm), 1)).astype(BF16)
    before = jnp.dot(both.astype(BF16), earlier, preferred_element_type=F32) + cnt_ref[:, 0:1]
    rank1 = jnp.sum(jnp.where(pick1, before, 0.0), axis=0, keepdims=True)
    rank2 = jnp.sum(jnp.where(pick2, before, 0.0), axis=0, keepdims=True)
    cnt_ref[...] = cnt_ref[...] + jnp.sum(both, axis=1, keepdims=True)
    rt_ref[...] = jnp.concatenate([i1, i2, g1, g2, rank1, rank2, jnp.zeros((2, tm), F32)], axis=0)


def _mix(x2d, y_rw, y_nsa, zmg, w_up_r, w_up_n, w_out, g_ffn, w_group, b_group, w_router, b_router):
    T = x2d.shape[0]
    tm = MIX_CHUNKS * TM_PROJ
    n_r = N_GROUPS + N_EXPERTS
    n_row = -(-n_r // 8) * 8
    wr = jnp.pad(jnp.concatenate([w_group, w_router], axis=1).T, ((0, n_row - n_r), (0, 0)))
    br = jnp.pad(jnp.concatenate([b_group, b_router]), (0, n_row - n_r)).reshape(n_row, 1)
    row = lambda i: (i, 0)
    fixed = lambda i: (0, 0)
    return pl.pallas_call(
        _mix_kernel,
        grid=(T // tm,),
        in_specs=[pl.BlockSpec((tm, D_MODEL), row), pl.BlockSpec((tm, RW_WIDTH), row),
                  pl.BlockSpec((tm, NSA_WIDTH), row), pl.BlockSpec((tm, 2 * D_MODEL), row),
                  pl.BlockSpec((RW_WIDTH, D_MODEL), fixed), pl.BlockSpec((NSA_WIDTH, D_MODEL), fixed),
                  pl.BlockSpec((D_MODEL, D_MODEL), fixed), pl.BlockSpec((1, D_MODEL), fixed),
                  pl.BlockSpec((n_row, D_MODEL), fixed), pl.BlockSpec((n_row, 1), fixed)],
        out_specs=[pl.BlockSpec((tm, D_MODEL), row), pl.BlockSpec((tm, D_MODEL), row),
                   pl.BlockSpec((8, tm), lambda i: (0, i)), pl.BlockSpec((n_row, LANE), fixed)],
        out_shape=[jax.ShapeDtypeStruct((T, D_MODEL), F32), jax.ShapeDtypeStruct((T, D_MODEL), F32),
                   jax.ShapeDtypeStruct((8, T), F32), jax.ShapeDtypeStruct((n_row, LANE), F32)],
        compiler_params=pltpu.CompilerParams(dimension_semantics=("arbitrary",),
                                             vmem_limit_bytes=VMEM_LIMIT),
        name="mix",
    )(x2d, y_rw, y_nsa, zmg, w_up_r.astype(BF16), w_up_n.astype(BF16), w_out.astype(BF16),
      g_ffn.reshape(1, D_MODEL), wr, br)


def _route_tables(rt, cnt, T):
    n_rows = T * TOP_K + N_EXPERTS * ROW_BLOCK
    n_blk = n_rows // ROW_BLOCK
    counts = cnt[N_GROUPS:N_GROUPS + N_EXPERTS, 0].astype(jnp.int32)
    padded = (counts + ROW_BLOCK - 1) // ROW_BLOCK * ROW_BLOCK
    pends = jnp.cumsum(padded)
    pstarts = pends - padded
    expert = rt[0:TOP_K].astype(jnp.int32)
    rank = rt[2 * TOP_K:3 * TOP_K].astype(jnp.int32)
    seg_start = jnp.sum(jnp.where(expert[..., None] == jnp.arange(N_EXPERTS), pstarts, 0), axis=-1)
    dest = (seg_start + rank).T.reshape(T * TOP_K)
    gates = rt[TOP_K:2 * TOP_K].T
    blk_start = jnp.arange(n_blk) * ROW_BLOCK
    blk_expert = jnp.minimum(jnp.sum(pends[None, :] <= blk_start[:, None], axis=1), N_EXPERTS - 1)
    blk_nv = jnp.clip(counts[blk_expert] - (blk_start - pstarts[blk_expert]), 0, ROW_BLOCK)
    return dest.astype(jnp.int32), gates, blk_expert.astype(jnp.int32), blk_nv.astype(jnp.int32)


def _row_copies(index_ref, base, n, make):
    for t in range(n):
        for k in range(TOP_K):
            make(base + t, k, index_ref[(base + t) * TOP_K + k]).start()


def _dispatch_kernel(dest_ref, h2_ref, xs_init_hbm, xs_hbm, sem):
    del xs_init_hbm
    i = pl.program_id(0)
    tm = h2_ref.shape[0]

    def row_copy(tok, k, row):
        return pltpu.make_async_copy(h2_ref.at[pl.ds(tok - i * tm, 1)], xs_hbm.at[pl.ds(row, 1)], sem)

    _row_copies(dest_ref, i * tm, tm, row_copy)
    for _ in range(TOP_K):
        pltpu.make_async_copy(h2_ref, xs_hbm.at[pl.ds(0, tm)], sem).wait()


def _dispatch(h2, dest):
    T = h2.shape[0]
    n_rows = T * TOP_K + N_EXPERTS * ROW_BLOCK
    grid_spec = pltpu.PrefetchScalarGridSpec(
        num_scalar_prefetch=1,
        grid=(T // TM_PROJ,),
        in_specs=[pl.BlockSpec((TM_PROJ, D_MODEL), lambda i, d: (i, 0)), pl.BlockSpec(memory_space=pl.ANY)],
        out_specs=pl.BlockSpec(memory_space=pl.ANY),
        scratch_shapes=[pltpu.SemaphoreType.DMA(())])
    return pl.pallas_call(
        _dispatch_kernel,
        grid_spec=grid_spec,
        out_shape=jax.ShapeDtypeStruct((n_rows, D_MODEL), F32),
        input_output_aliases={2: 0},
        compiler_params=pltpu.CompilerParams(dimension_semantics=("arbitrary",),
                                             vmem_limit_bytes=VMEM_LIMIT),
        name="dispatch",
    )(dest, h2, jnp.zeros((n_rows, D_MODEL), F32))


def _expert_kernel(be_ref, nv_ref, xs_ref, wgu_ref, wd_ref, o_ref, wgu_b, wd_b):
    i = pl.program_id(0)
    nv = nv_ref[i]
    changed = jnp.logical_or(i == 0, be_ref[i] != be_ref[jnp.maximum(i - 1, 0)])

    @pl.when(jnp.logical_and(nv > 0, changed))
    def _():
        wgu_b[...] = wgu_ref[0].astype(BF16)
        wd_b[...] = wd_ref[0].astype(BF16)

    @pl.when(nv > 0)
    def _():
        gu = jnp.dot(xs_ref[...].astype(BF16), wgu_b[...], preferred_element_type=F32)
        gate_h, up_h = gu[:, :D_EXPERT], gu[:, D_EXPERT:]
        mid = gate_h * jax.nn.sigmoid(gate_h) * up_h
        o_ref[...] = jnp.dot(mid.astype(BF16), wd_b[...], preferred_element_type=F32)

    @pl.when(nv == 0)
    def _():
        o_ref[...] = jnp.zeros_like(o_ref)


def _experts(xs, blk_expert, blk_nv, w_gate_up, w_down):
    n_rows = xs.shape[0]
    grid_spec = pltpu.PrefetchScalarGridSpec(
        num_scalar_prefetch=2,
        grid=(n_rows // ROW_BLOCK,),
        in_specs=[pl.BlockSpec((ROW_BLOCK, D_MODEL), lambda i, be, nv: (i, 0)),
                  pl.BlockSpec((1, D_MODEL, 2 * D_EXPERT), lambda i, be, nv: (be[i], 0, 0)),
                  pl.BlockSpec((1, D_EXPERT, D_MODEL), lambda i, be, nv: (be[i], 0, 0))],
        out_specs=pl.BlockSpec((ROW_BLOCK, D_MODEL), lambda i, be, nv: (i, 0)),
        scratch_shapes=[pltpu.VMEM((D_MODEL, 2 * D_EXPERT), BF16), pltpu.VMEM((D_EXPERT, D_MODEL), BF16)])
    return pl.pallas_call(
        _expert_kernel,
        grid_spec=grid_spec,
        out_shape=jax.ShapeDtypeStruct((n_rows, D_MODEL), F32),
        compiler_params=pltpu.CompilerParams(dimension_semantics=("arbitrary",),
                                             vmem_limit_bytes=VMEM_LIMIT),
        name="experts",
    )(blk_expert, blk_nv, xs, w_gate_up, w_down)


def _ple_kernel(dest_ref, x1_ref, g_ref, p_ref, rows_hbm, wpp_ref, gpn_ref, ggi_ref, wpg_ref, gfin_ref,
                y_ref, ebuf, sem, *, last_layer):
    i = pl.program_id(0)
    tm = x1_ref.shape[0]
    slot = i % 2

    def fetch(tile, s):
        def row_copy(tok, k, row):
            return pltpu.make_async_copy(rows_hbm.at[pl.ds(row, 1)],
                                         ebuf.at[s, k, pl.ds(tok - tile * tm, 1)], sem.at[s])
        _row_copies(dest_ref, tile * tm, tm, row_copy)

    def drain(s):
        for k in range(TOP_K):
            pltpu.make_async_copy(rows_hbm.at[pl.ds(0, tm)], ebuf.at[s, k], sem.at[s]).wait()

    last = pl.num_programs(0) - 1

    @pl.when(i == 0)
    def _():
        fetch(0, 0)

    fetch(jnp.minimum(i + 1, last), 1 - slot)
    drain(slot)
    moe = g_ref[:, 0:1] * ebuf[slot, 0]
    for k in range(1, TOP_K):
        moe = moe + g_ref[:, k:k + 1] * ebuf[slot, k]
    x2 = x1_ref[...] + moe
    e = _rms(_bdot(p_ref[...], wpp_ref[...]), gpn_ref[...])
    gate = jax.nn.sigmoid(_bdot(_rms(x2, ggi_ref[...]), wpg_ref[...]))
    x3 = x2 + gate * e
    y_ref[...] = _rms(x3, gfin_ref[...]) if last_layer else x3

    @pl.when(i == last)
    def _():
        drain(1 - slot)


def _ple(x1, gates, dest, expert_rows, p2d, w_pp, g_pn, g_gi, w_pg, g_final, last_layer):
    T = x1.shape[0]
    tm = TM_PROJ
    row = lambda i, d: (i, 0)
    fixed = lambda i, d: (0, 0)
    vec = lambda a: a.reshape(1, D_MODEL)
    grid_spec = pltpu.PrefetchScalarGridSpec(
        num_scalar_prefetch=1,
        grid=(T // tm,),
        in_specs=[pl.BlockSpec((tm, D_MODEL), row), pl.BlockSpec((tm, TOP_K), row),
                  pl.BlockSpec((tm, PLE_DIM), row), pl.BlockSpec(memory_space=pl.ANY),
                  pl.BlockSpec((PLE_DIM, D_MODEL), fixed), pl.BlockSpec((1, D_MODEL), fixed),
                  pl.BlockSpec((1, D_MODEL), fixed), pl.BlockSpec((D_MODEL, D_MODEL), fixed),
                  pl.BlockSpec((1, D_MODEL), fixed)],
        out_specs=pl.BlockSpec((tm, D_MODEL), row),
        scratch_shapes=[pltpu.VMEM((2, TOP_K, tm, D_MODEL), F32), pltpu.SemaphoreType.DMA((2,))])
    return pl.pallas_call(
        functools.partial(_ple_kernel, last_layer=last_layer),
        grid_spec=grid_spec,
        out_shape=jax.ShapeDtypeStruct((T, D_MODEL), F32),
        compiler_params=pltpu.CompilerParams(dimension_semantics=("arbitrary",),
                                             vmem_limit_bytes=VMEM_LIMIT),
        name="ple",
    )(dest, x1, gates, p2d, expert_rows, w_pp.astype(BF16), vec(g_pn), vec(g_gi), w_pg.astype(BF16),
      vec(g_final))


def kernel(x, p, g_mix, w_in, mu_shift, rw_w0, rw_w2, rw_a0, rw_a2, rw_g2, rw_k_k, rw_k_a, rw_r_k, rw_ln_w, rw_ln_b, cmp_pos_k, cmp_pos_v, cmp_k_w1, cmp_k_w2, cmp_v_w1, cmp_v_w2, nsa_gate_b, w_up_rwkv, w_up_nsa, w_out, g_ffn, w_group, b_group, w_router, b_router, w_exp_gate_up, w_exp_down, w_ple_proj, g_ple_norm, g_ple_gate_in, w_ple_gate, g_final):
    B, S, D = x.shape
    T = B * S
    depth = p.shape[0]
    xc = x.reshape(T, D)
    for i in range(depth):
        zrw, zq, zkv, zmg, zgate = _proj(xc, g_mix[i], w_in[i], mu_shift[i], S)
        y_rw = _rwkv(zrw.reshape(B, S, RW_IN), rw_w0[i], rw_w2[i], rw_a0[i], rw_a2[i], rw_g2[i],
                     rw_k_k[i], rw_k_a[i], rw_r_k[i], rw_ln_w[i], rw_ln_b[i])
        zkv3 = zkv.reshape(B, S, KV_IN)
        nsa_kv = _nsa_prep(zkv3, cmp_pos_k[i], cmp_pos_v[i], cmp_k_w1[i], cmp_k_w2[i],
                           cmp_v_w1[i], cmp_v_w2[i])
        y_nsa = _nsa_attn(zq.reshape(B, S, NSA_WIDTH), *nsa_kv, zgate.reshape(B, S, LANE),
                          nsa_gate_b[i])
        x1, h2, rt, cnt = _mix(xc, y_rw.reshape(T, RW_WIDTH), y_nsa.reshape(T, NSA_WIDTH), zmg,
                               w_up_rwkv[i], w_up_nsa[i], w_out[i], g_ffn[i], w_group[i], b_group[i],
                               w_router[i], b_router[i])
        dest, gates, blk_expert, blk_nv = _route_tables(rt, cnt, T)
        expert_rows = _experts(_dispatch(h2, dest), blk_expert, blk_nv, w_exp_gate_up[i], w_exp_down[i])
        xc = _ple(x1, gates, dest, expert_rows, p[i].reshape(T, PLE_DIM), w_ple_proj[i], g_ple_norm[i],
                  g_ple_gate_in[i], w_ple_gate[i], g_final, i == depth - 1)
    return xc.reshape(B, S, D)
```

```python
import functools
import math

import jax
import jax.numpy as jnp
import numpy as np
from jax import lax
from jax.experimental import pallas as pl
from jax.experimental.pallas import tpu as pltpu

F32 = jnp.float32
BF16 = jnp.bfloat16
HI = lax.Precision.HIGHEST

D_MODEL = 1024
RW_HEADS = 8
RW_HEAD_DIM = 64
RW_WIDTH = 512
DECAY_LORA = 64
AAA_LORA = 64
GATE_LORA = 128
GN_EPS = 64e-5
RW_IN = 3 * RW_WIDTH + DECAY_LORA + AAA_LORA + GATE_LORA

NSA_HEADS = 8
NSA_KV_HEADS = 2
NSA_REP = NSA_HEADS // NSA_KV_HEADS
NSA_HEAD_DIM = 64
NSA_WIDTH = 512
NSA_KV_WIDTH = 128
CMP_BLOCK = 32
CMP_STRIDE = 16
SEL_BLOCK = 64
N_SELECT = 8
WINDOW = 512
N_NSA_BRANCH = 3
FORCE_SCORE = 1e6
NEG_INF = -1e30

N_GROUPS = 4
EXPERTS_PER_GROUP = 8
N_EXPERTS = 32
TOP_K = 2
D_EXPERT = 512
ROW_BLOCK = 256
PLE_DIM = 256
NORM_EPS = 1e-6

N_GATE = N_NSA_BRANCH * NSA_HEADS
ATT_IN = NSA_WIDTH + 6 * NSA_KV_WIDTH
KV_OFF = RW_IN + NSA_WIDTH
KV_IN = 6 * NSA_KV_WIDTH
GATE_OFF = RW_IN + ATT_IN
MERGE_OFF = GATE_OFF + N_GATE
LANE = 128

RW_CHUNK = 64
RW_SUB = 16
RW_ROWS = 4
TQ = 256
TK = 256
V_ROWS = NSA_HEAD_DIM + 16
TM_PROJ = 256
MIX_CHUNKS = 2
VMEM_LIMIT = 56 * 1024 * 1024


def _bdot(a, b):
    return jnp.dot(a.astype(BF16), b.astype(BF16), preferred_element_type=F32)


def _bdot_nt(a, b):
    return lax.dot_general(a.astype(BF16), b.astype(BF16), (((1,), (1,)), ((), ())),
                           preferred_element_type=F32)


def _rms(x, g):
    return x * lax.rsqrt(jnp.mean(x * x, axis=-1, keepdims=True) + NORM_EPS) * g


def _proj_kernel(x_ref, g_ref, w_ref, mu_ref, zrw_ref, zq_ref, zkv_ref, zmg_ref, zgate_ref,
                 carry_ref, *, tiles_per_seq):
    i = pl.program_id(0)
    tm = x_ref.shape[0]

    @pl.when(i % tiles_per_seq == 0)
    def _():
        carry_ref[...] = jnp.zeros_like(carry_ref)

    h = _rms(x_ref[...], g_ref[...]).astype(BF16)
    z = jnp.dot(h, w_ref[:, 0:RW_IN], preferred_element_type=F32)
    row = lax.broadcasted_iota(jnp.int32, (tm, 1), 0)
    prev = jnp.where(row == 0, carry_ref[7:8, :], pltpu.roll(z, 1, 0))
    carry_ref[...] = z[tm - 8:tm, :]
    zrw_ref[...] = z + (prev - z) * mu_ref[...]
    zq_ref[...] = jnp.dot(h, w_ref[:, RW_IN:KV_OFF], preferred_element_type=F32)
    zkv_ref[...] = jnp.dot(h, w_ref[:, KV_OFF:GATE_OFF], preferred_element_type=F32)
    zmg_ref[...] = jnp.dot(h, w_ref[:, GATE_OFF:GATE_OFF + 2 * D_MODEL], preferred_element_type=F32)
    zgate_ref[...] = jnp.dot(h, w_ref[:, GATE_OFF + 2 * D_MODEL:], preferred_element_type=F32)


def _proj(x2d, g_mix, w_in, mu, seq):
    T = x2d.shape[0]
    tm = TM_PROJ
    wp = jnp.concatenate(
        [w_in[:, :GATE_OFF], w_in[:, MERGE_OFF:],
         jnp.pad(w_in[:, GATE_OFF:MERGE_OFF], ((0, 0), (0, LANE - N_GATE)))], axis=1).astype(BF16)
    npad = wp.shape[1]
    row = lambda i: (i, 0)
    fixed = lambda i: (0, 0)
    return pl.pallas_call(
        functools.partial(_proj_kernel, tiles_per_seq=seq // tm),
        grid=(T // tm,),
        in_specs=[pl.BlockSpec((tm, D_MODEL), row), pl.BlockSpec((1, D_MODEL), fixed),
                  pl.BlockSpec((D_MODEL, npad), fixed), pl.BlockSpec((1, RW_IN), fixed)],
        out_specs=[pl.BlockSpec((tm, RW_IN), row), pl.BlockSpec((tm, NSA_WIDTH), row),
                   pl.BlockSpec((tm, KV_IN), row), pl.BlockSpec((tm, 2 * D_MODEL), row),
                   pl.BlockSpec((tm, LANE), row)],
        out_shape=[jax.ShapeDtypeStruct((T, RW_IN), F32), jax.ShapeDtypeStruct((T, NSA_WIDTH), F32),
                   jax.ShapeDtypeStruct((T, KV_IN), F32), jax.ShapeDtypeStruct((T, 2 * D_MODEL), F32),
                   jax.ShapeDtypeStruct((T, LANE), F32)],
        scratch_shapes=[pltpu.VMEM((8, RW_IN), F32)],
        compiler_params=pltpu.CompilerParams(dimension_semantics=("arbitrary",),
                                             vmem_limit_bytes=VMEM_LIMIT),
        name="proj",
    )(x2d, g_mix.reshape(1, D_MODEL), wp, mu.reshape(1, RW_IN))


PAIR = 2 * RW_HEAD_DIM


def _pair_blocks(x):
    low = lax.broadcasted_iota(jnp.int32, (1, PAIR), 1) < RW_HEAD_DIM
    return jnp.concatenate([jnp.where(low, x, 0.0), jnp.where(low, 0.0, x)], axis=0)


def _pmm(a, b):
    return _bdot(a, _pair_blocks(b))


def _unit_lower_inverse(a_strict, sub_mask, eye):
    ad = [jnp.where(sub_mask, a, 0.0) for a in a_strict]
    ao = [a - d for a, d in zip(a_strict, ad)]
    td = [eye - d for d in ad]
    pw = ad
    for _ in range(int(math.log2(RW_SUB)) - 1):
        pw = [_pmm(x, x) for x in pw]
        td = [_pmm(t, eye + x) for t, x in zip(td, pw)]
    n = [_pmm(t, o) for t, o in zip(td, ao)]
    t = [eye - x for x in n]
    pw = n
    for _ in range(int(math.log2(RW_CHUNK // RW_SUB)) - 1):
        pw = [_pmm(x, x) for x in pw]
        t = [_pmm(a, eye + x) for a, x in zip(t, pw)]
    return [_pmm(a, d) for a, d in zip(t, td)]


def _rwkv_kernel(z_ref, w0_ref, w2_ref, a0_ref, a2_ref, g2_ref, kk_ref, ka_ref, rk_ref, lnw_ref,
                 lnb_ref, avg_ref, o_ref, h_ref):
    c = pl.program_id(1)
    C = RW_CHUNK
    n_pair = RW_WIDTH // PAIR
    nt = (((1,), (1,)), ((), ()))

    @pl.when(c == 0)
    def _():
        h_ref[...] = jnp.zeros_like(h_ref)

    ti = lax.broadcasted_iota(jnp.int32, (C, 1), 0)
    si = lax.broadcasted_iota(jnp.int32, (1, PAIR), 1) & (RW_HEAD_DIM - 1)
    incl, strict = ti >= si, ti > si
    eye = (ti == si).astype(F32)
    sub_shift = int(math.log2(RW_SUB))
    sub_mask = (ti >> sub_shift) == (si >> sub_shift)
    row2 = lax.broadcasted_iota(jnp.int32, (PAIR, 1), 0)
    col2 = lax.broadcasted_iota(jnp.int32, (1, PAIR), 1)
    same_head = (row2 < RW_HEAD_DIM) == (col2 < RW_HEAD_DIM)
    eye2 = row2 == col2
    tri = (lax.broadcasted_iota(jnp.int32, (C, C), 0)
           >= lax.broadcasted_iota(jnp.int32, (C, C), 1)).astype(BF16)

    def head_mean(x):
        xs = jnp.concatenate([x[:, p * PAIR:(p + 1) * PAIR] for p in range(n_pair)], axis=0)
        ms = _bdot(xs, avg_ref[...])
        return jnp.concatenate([ms[p * C:(p + 1) * C] for p in range(n_pair)], axis=1)

    n_rows = z_ref.shape[0]
    rows = []
    for n in range(n_rows):
        z = z_ref[n]
        zr, zk, zv = z[:, 0:512], z[:, 512:1024], z[:, 1024:1536]
        zw, za, zg = z[:, 1536:1600], z[:, 1600:1664], z[:, 1664:1792]
        w_raw = w0_ref[...] + _bdot(jnp.tanh(zw), w2_ref[...])
        logw = -jax.nn.sigmoid(w_raw) * math.exp(-0.5)
        a = jax.nn.sigmoid(a0_ref[...] + _bdot(za, a2_ref[...]))
        gate = _bdot(jax.nn.sigmoid(zg), g2_ref[...])
        kk = zk * kk_ref[...]
        kk = kk / jnp.maximum(jnp.sqrt(head_mean(kk * kk) * RW_HEAD_DIM), 1e-12)
        k = zk * (1.0 + (a - 1.0) * ka_ref[...])
        b = kk * a

        w_hi = logw.astype(BF16)
        w_lo = (logw - w_hi.astype(F32)).astype(BF16)
        cum = (jnp.dot(tri, w_hi, preferred_element_type=F32)
               + jnp.dot(tri, w_lo, preferred_element_type=F32))
        cum_last = cum[C - 1:C, :]
        g_inv = jnp.exp(-cum)
        g_end = jnp.exp(cum_last - cum)
        rows.append(dict(rt=zr * jnp.exp(cum), kt=k * g_inv, bt=b * g_inv, qt=kk * jnp.exp(cum - logw),
                         kh=k * g_end, bh=b * g_end, v=zv, g_last=jnp.exp(cum_last), gate=gate,
                         bonus=head_mean(zr * k * rk_ref[...]) * RW_HEAD_DIM * zv))

    chains = [(n, slice(p * PAIR, (p + 1) * PAIR)) for n in range(n_rows) for p in range(n_pair)]
    part = lambda name: [rows[n][name][:, sl] for n, sl in chains]
    qt, rt, kt, bt, kh, bh, v = (part(x) for x in ("qt", "rt", "kt", "bt", "kh", "bh", "v"))
    lhs = [jnp.concatenate([q, r], axis=0).astype(BF16) for q, r in zip(qt, rt)]
    ab = [lax.dot_general(l, _pair_blocks(x).astype(BF16), nt, preferred_element_type=F32)
          for l, x in zip(lhs, bt)]
    ak = [lax.dot_general(l, _pair_blocks(x).astype(BF16), nt, preferred_element_type=F32)
          for l, x in zip(lhs, kt)]
    a_kb = [jnp.where(strict, x[0:C], 0.0) for x in ab]
    a_rb = [jnp.where(incl, x[C:2 * C], 0.0) for x in ab]
    a_kk = [jnp.where(strict, x[0:C], 0.0) for x in ak]
    a_rk = [jnp.where(incl, x[C:2 * C], 0.0) for x in ak]
    t_inv = _unit_lower_inverse(a_kb, sub_mask, eye)

    h = [h_ref[n, sl.start // PAIR] for n, sl in chains]
    vb = [_pair_blocks(x) for x in v]
    rhs = [_bdot(jnp.concatenate([q, akk], axis=1), jnp.concatenate([hh, vv], axis=0))
           for q, akk, hh, vv in zip(qt, a_kk, h, vb)]
    u = [_pmm(t, x) for t, x in zip(t_inv, rhs)]
    outs = [_bdot(jnp.concatenate([r, ark, -arb], axis=1), jnp.concatenate([hh, vv, _pair_blocks(uu)], axis=0))
            for r, ark, arb, hh, vv, uu in zip(rt, a_rk, a_rb, h, vb, u)]
    upd = [_bdot(jnp.concatenate([x, -y], axis=0).T, jnp.concatenate([vv, uu], axis=0))
           for x, y, vv, uu in zip(kh, bh, v, u)]
    for (n, sl), hh, dd in zip(chains, h, upd):
        decay_col = jnp.sum(jnp.where(eye2, rows[n]["g_last"][:, sl], 0.0), axis=1, keepdims=True)
        h_ref[n, sl.start // PAIR] = decay_col * hh + jnp.where(same_head, dd, 0.0)

    for n in range(n_rows):
        o = jnp.concatenate(outs[n * n_pair:(n + 1) * n_pair], axis=1)
        d = o - head_mean(o)
        on = d * lax.rsqrt(head_mean(d * d) + GN_EPS)
        o_ref[n] = (on * lnw_ref[...] + lnb_ref[...] + rows[n]["bonus"]) * rows[n]["gate"]


def _rwkv(zrw, w0, w2, a0, a2, g2, k_k, k_a, r_k, ln_w, ln_b):
    B, S, _ = zrw.shape
    C = RW_CHUNK
    nb = RW_ROWS
    hid = np.arange(PAIR) // RW_HEAD_DIM
    avg = jnp.asarray((hid[:, None] == hid[None, :]).astype(np.float32) / RW_HEAD_DIM)
    vec = lambda a: a.reshape(1, RW_WIDTH)
    fixed = lambda shape: pl.BlockSpec(shape, lambda b, c: (0,) * len(shape))
    return pl.pallas_call(
        _rwkv_kernel,
        grid=(B // nb, S // C),
        in_specs=[pl.BlockSpec((nb, C, RW_IN), lambda b, c: (b, c, 0)),
                  fixed((1, RW_WIDTH)), fixed((DECAY_LORA, RW_WIDTH)),
                  fixed((1, RW_WIDTH)), fixed((AAA_LORA, RW_WIDTH)),
                  fixed((GATE_LORA, RW_WIDTH)), fixed((1, RW_WIDTH)), fixed((1, RW_WIDTH)),
                  fixed((1, RW_WIDTH)), fixed((1, RW_WIDTH)), fixed((1, RW_WIDTH)),
                  fixed((PAIR, PAIR))],
        out_specs=pl.BlockSpec((nb, C, RW_WIDTH), lambda b, c: (b, c, 0)),
        out_shape=jax.ShapeDtypeStruct((B, S, RW_WIDTH), F32),
        scratch_shapes=[pltpu.VMEM((nb, RW_WIDTH // PAIR, PAIR, PAIR), F32)],
        compiler_params=pltpu.CompilerParams(dimension_semantics=("parallel", "arbitrary"),
                                             vmem_limit_bytes=VMEM_LIMIT),
        name="rwkv",
    )(zrw, vec(w0), w2, vec(a0), a2, g2, vec(k_k), vec(k_a), vec(r_k), vec(ln_w), vec(ln_b), avg)


def _gelu_tanh(x):
    return 0.5 * x * (1.0 + jnp.tanh(math.sqrt(2.0 / math.pi) * (x + 0.044715 * (x * x * x))))


def _key_features(pos_hi, pos_lo, block, n, n_sel):
    lane = lax.broadcasted_iota(jnp.int32, (n, NSA_HEAD_DIM), 1)
    feat = jnp.where(lane == n_sel, pos_hi, jnp.where(lane == n_sel + 1, pos_lo, 0.0))
    return feat if block is None else jnp.where(lane == block, 1.0, feat)


def _nsa_prep_kernel(zkc_ref, zvc_ref, zks_ref, zvs_ref, zkw_ref, zvw_ref, pk_ref, pv_ref, kw1_ref,
                     kw2_ref, vw1_ref, vw2_ref, kc_ref, vct_ref, ksa_ref, kwa_ref, vst_ref, vwt_ref):
    S = zkc_ref.shape[1]
    n_grp = S // CMP_STRIDE
    Dh = NSA_HEAD_DIM
    half = CMP_BLOCK // 2
    n_sel = S // SEL_BLOCK
    jrow = lax.broadcasted_iota(jnp.int32, (n_grp, 1), 0)
    cmp_feat = _key_features((jrow >> 3).astype(F32),
                             ((jrow & 7) * CMP_STRIDE).astype(F32) + 0.5 * (CMP_BLOCK - 1),
                             None, n_grp, n_sel)
    for is_v, (z_ref, pos_ref, w1_ref, w2_ref) in enumerate(((zkc_ref, pk_ref, kw1_ref, kw2_ref),
                                                            (zvc_ref, pv_ref, vw1_ref, vw2_ref))):
        for g in range(NSA_KV_HEADS):
            lo = jnp.zeros((n_grp, Dh), F32)
            hi = jnp.zeros((n_grp, Dh), F32)
            for l in range(half):
                xs = z_ref[0, pl.ds(l, n_grp, stride=CMP_STRIDE), :]
                xg = xs[:, g * Dh:(g + 1) * Dh]
                lo = lo + _bdot(xg + pos_ref[l:l + 1, :], w1_ref[l * Dh:(l + 1) * Dh, :])
                hi = hi + _bdot(xg + pos_ref[half + l:half + l + 1, :],
                                w1_ref[(half + l) * Dh:(half + l + 1) * Dh, :])
            pre = lo + pltpu.roll(hi, n_grp - 1, 0)
            out = jnp.where(jrow < n_grp - 1, _bdot(_gelu_tanh(pre), w2_ref[...]), 0.0)
            if is_v:
                out_t = jnp.concatenate([out, jnp.zeros_like(out)], axis=1).T
                vct_ref[0, g] = out_t[0:Dh, :].astype(BF16)
            else:
                kc_ref[0, g] = jnp.concatenate([out, cmp_feat], axis=1)

    prow = lax.broadcasted_iota(jnp.int32, (S, 1), 0)
    p_hi, p_lo = (prow >> 7).astype(F32), (prow & (LANE - 1)).astype(F32)
    for z_ref, out_ref, block in ((zks_ref, ksa_ref, prow >> int(math.log2(SEL_BLOCK))),
                                  (zkw_ref, kwa_ref, None)):
        kfull = z_ref[0]
        key_feat = _key_features(p_hi, p_lo, block, S, n_sel)
        for g in range(NSA_KV_HEADS):
            out_ref[0, g] = jnp.concatenate([kfull[:, g * Dh:(g + 1) * Dh], key_feat], axis=1).astype(BF16)
    ones_row = (lax.broadcasted_iota(jnp.int32, (V_ROWS - Dh, TK), 0) == 0).astype(F32)
    for z_ref, out_ref in ((zvs_ref, vst_ref), (zvw_ref, vwt_ref)):
        for j in range(S // TK):
            vt = z_ref[0, j * TK:(j + 1) * TK, :].T
            out_ref[0, j] = jnp.concatenate(
                [piece for g in range(NSA_KV_HEADS) for piece in (vt[g * Dh:(g + 1) * Dh], ones_row)],
                axis=0).astype(BF16)


def _nsa_prep(zkv, pos_k, pos_v, kw1, kw2, vw1, vw2):
    B, S, _ = zkv.shape
    n_grp = S // CMP_STRIDE
    n_kt = S // TK
    Dh = NSA_HEAD_DIM
    G = NSA_KV_HEADS
    fixed = lambda shape: pl.BlockSpec(shape, lambda b: (0,) * len(shape))
    col = lambda c: pl.BlockSpec((1, S, NSA_KV_WIDTH), lambda b: (b, 0, c))
    whole = lambda shape: pl.BlockSpec((1,) + shape, lambda b: (b,) + (0,) * len(shape))
    shapes = [((G, n_grp, 2 * Dh), F32), ((G, Dh, n_grp), BF16), ((G, S, 2 * Dh), BF16),
              ((G, S, 2 * Dh), BF16), ((n_kt, G * V_ROWS, TK), BF16), ((n_kt, G * V_ROWS, TK), BF16)]
    return pl.pallas_call(
        _nsa_prep_kernel,
        grid=(B,),
        in_specs=[col(c) for c in range(6)] + [
            fixed((CMP_BLOCK, Dh)), fixed((CMP_BLOCK, Dh)),
            fixed((CMP_BLOCK * Dh, Dh)), fixed((Dh, Dh)),
            fixed((CMP_BLOCK * Dh, Dh)), fixed((Dh, Dh))],
        out_specs=[whole(s) for s, _ in shapes],
        out_shape=[jax.ShapeDtypeStruct((B,) + s, d) for s, d in shapes],
        compiler_params=pltpu.CompilerParams(dimension_semantics=("parallel",),
                                             vmem_limit_bytes=VMEM_LIMIT),
        name="nsa_prep",
    )(zkv, zkv, zkv, zkv, zkv, zkv, pos_k, pos_v, kw1, kw2, vw1, vw2)


def _nsa_attn_kernel(q_ref, kc_ref, vct_ref, ksa_ref, kwa_ref, vst_ref, vwt_ref, gl_ref, gb_ref,
                     ovlt_ref, slope_ref, o_ref, acc_ref, ot_ref):
    i = pl.program_id(1)
    Dh = NSA_HEAD_DIM
    R = NSA_REP
    N = R * TQ
    n_cmp_pad = kc_ref.shape[2]
    n_sel = ovlt_ref.shape[0]
    G = NSA_KV_HEADS
    nt = (((1,), (1,)), ((), ()))
    log2e = math.log2(math.e)
    t0 = i * TQ
    t_row = t0 + lax.broadcasted_iota(jnp.int32, (1, TQ), 1)
    c_col = lax.broadcasted_iota(jnp.int32, (TK, 1), 0)
    sgate_t = jax.nn.sigmoid(gl_ref[0] + gb_ref[...]).T
    lane_f = lax.broadcasted_iota(jnp.int32, (1, Dh), 1)
    heads = lambda x: jnp.concatenate([x] * R, axis=1)

    def key_dist(j):
        return t_row - (j * TK + c_col)

    def queries(g, sel_feat):
        parts = []
        for r in range(R):
            h = g * R + r
            sl = slope_ref[:, h:h + 1] * log2e
            feat = jnp.where(lane_f == n_sel, sl * LANE, jnp.where(lane_f == n_sel + 1, sl, sel_feat))
            parts.append(jnp.concatenate(
                [q_ref[0, :, h * Dh:(h + 1) * Dh] * (Dh ** -0.5 * log2e), jnp.broadcast_to(feat, (TQ, Dh))],
                axis=1))
        return jnp.concatenate(parts, axis=0)

    qab, o_cmp = [], []
    for g in range(G):
        jc = lax.broadcasted_iota(jnp.int32, (n_cmp_pad, 1), 0)
        ok_c = (jc * CMP_STRIDE + (CMP_BLOCK - 1) <= t_row) & (jc < n_cmp_pad - 1)
        s_c = (lax.dot_general(kc_ref[0, g], queries(g, 0.0), nt, precision=HI, preferred_element_type=F32)
               + heads(jnp.where(ok_c, 0.0, NEG_INF)))
        e_c = jnp.exp2(s_c - jnp.max(s_c, axis=0, keepdims=True))
        any_c = heads(t_row >= CMP_BLOCK - 1)
        p_c = e_c * jnp.where(any_c, 1.0 / jnp.sum(e_c, axis=0, keepdims=True), 0.0)
        o_cmp.append(jnp.dot(vct_ref[0, g], p_c.astype(BF16), preferred_element_type=F32))

        p_sum = p_c[:, 0:TQ]
        for r in range(1, R):
            p_sum = p_sum + p_c[:, r * TQ:(r + 1) * TQ]
        imp = jnp.dot(ovlt_ref[...], p_sum, precision=HI, preferred_element_type=F32)
        kb = lax.broadcasted_iota(jnp.int32, (n_sel, 1), 0)
        kbf = kb.astype(F32)
        blk_t = t_row >> int(math.log2(SEL_BLOCK))
        forced = (kb == 0) | (kb == blk_t) | (kb == blk_t - 1)
        cur = jnp.where(forced, FORCE_SCORE, jnp.where(kb <= blk_t, imp, -FORCE_SCORE))
        sel_bias = jnp.full((n_sel, TQ), NEG_INF, F32)
        for _ in range(min(N_SELECT, n_sel)):
            mx = jnp.max(cur, axis=0, keepdims=True)
            first = jnp.min(jnp.where(cur == mx, kbf, float(n_sel)), axis=0, keepdims=True)
            hit = kbf == first
            sel_bias = jnp.where(hit, 0.0, sel_bias)
            cur = jnp.where(hit, -3e38, cur)
        sel_feat = jnp.concatenate([sel_bias, jnp.zeros((LANE - n_sel, TQ), F32)], axis=0).T[:, 0:Dh]
        qab.append(queries(g, sel_feat).astype(BF16))

    VR = vst_ref.shape[2] // G

    def tile(j, m, k_ref, vt_ref, bias, slot):
        ks = [k_ref[0, g, pl.ds(pl.multiple_of(j * TK, TK), TK), :] for g in range(G)]
        vts = [vt_ref[0, j, g * VR:(g + 1) * VR, :] for g in range(G)]
        hs = range(NSA_HEADS)
        s = [lax.dot_general(ks[h // R], qab[h // R][(h % R) * TQ:(h % R + 1) * TQ], nt,
                             preferred_element_type=F32) for h in hs]
        if bias is not None:
            s = [x + bias for x in s]
        m_new = [jnp.maximum(m[h], jnp.max(s[h], axis=0, keepdims=True)) for h in hs]
        alpha = [jnp.exp2(m[h] - m_new[h]) for h in hs]
        p = [jnp.exp2(s[h] - m_new[h]).astype(BF16) for h in hs]
        pv = [jnp.dot(vts[h // R], p[h], preferred_element_type=F32) for h in hs]
        for h in hs:
            acc_ref[slot, h] = alpha[h] * acc_ref[slot, h] + pv[h]
        return tuple(m_new)

    def window_bias(j):
        d = key_dist(j)
        return jnp.where((d >= 0) & (d < WINDOW), 0.0, NEG_INF)

    init = (jnp.full((1, TQ), NEG_INF, F32),) * NSA_HEADS
    acc_ref[...] = jnp.zeros_like(acc_ref)
    causal = jnp.where(key_dist(i) >= 0, 0.0, NEG_INF)
    carry = lax.fori_loop(0, i, lambda j, c: tile(j, c, ksa_ref, vst_ref, None, 0), init)
    tile(i, carry, ksa_ref, vst_ref, causal, 0)
    carry = lax.fori_loop(jnp.maximum(i - WINDOW // TK, 0), i,
                          lambda j, c: tile(j, c, kwa_ref, vwt_ref, window_bias(j), 1), init)
    tile(i, carry, kwa_ref, vwt_ref, causal, 1)

    for h in range(NSA_HEADS):
        g, r = divmod(h, R)
        acc_s, acc_w = acc_ref[0, h], acc_ref[1, h]
        ot_ref[h * Dh:(h + 1) * Dh, :] = (
            sgate_t[3 * h:3 * h + 1, :] * o_cmp[g][:, r * TQ:(r + 1) * TQ]
            + sgate_t[3 * h + 1:3 * h + 2, :] * (acc_s[0:Dh] * (1.0 / acc_s[Dh:Dh + 1]))
            + sgate_t[3 * h + 2:3 * h + 3, :] * (acc_w[0:Dh] * (1.0 / acc_w[Dh:Dh + 1])))
    o_ref[0] = ot_ref[...].T


def _nsa_attn(zq, kc, vct, ksa, kwa, vst, vwt, zgate, gate_b):
    B, S, _ = zq.shape
    n_sel = S // SEL_BLOCK
    n_cmp = (S - CMP_BLOCK) // CMP_STRIDE + 1
    n_cmp_pad = kc.shape[2]
    n_kt = S // TK
    G, Dh = NSA_KV_HEADS, NSA_HEAD_DIM
    cmp_start = np.arange(n_cmp) * CMP_STRIDE
    sel_start = np.arange(n_sel) * SEL_BLOCK
    overlap = np.clip(np.minimum(cmp_start[:, None] + CMP_BLOCK, sel_start[None, :] + SEL_BLOCK)
                      - np.maximum(cmp_start[:, None], sel_start[None, :]), 0, None) / CMP_BLOCK
    ovlt = np.zeros((n_sel, n_cmp_pad), np.float32)
    ovlt[:, :n_cmp] = overlap.T
    slopes = (2.0 ** (-8.0 * np.arange(1, NSA_HEADS + 1) / NSA_HEADS)).astype(np.float32).reshape(1, NSA_HEADS)
    gb = jnp.pad(gate_b, (0, LANE - N_GATE)).reshape(1, LANE)
    fixed = lambda shape: pl.BlockSpec(shape, lambda b, i: (0,) * len(shape))
    per_b = lambda shape: pl.BlockSpec((1,) + shape, lambda b, i: (b,) + (0,) * len(shape))
    return pl.pallas_call(
        _nsa_attn_kernel,
        grid=(B, S // TQ),
        in_specs=[pl.BlockSpec((1, TQ, NSA_WIDTH), lambda b, i: (b, i, 0)),
                  per_b((G, n_cmp_pad, 2 * Dh)), per_b((G, Dh, n_cmp_pad)),
                  per_b((G, S, 2 * Dh)), per_b((G, S, 2 * Dh)),
                  per_b((n_kt, G * V_ROWS, TK)), per_b((n_kt, G * V_ROWS, TK)),
                  pl.BlockSpec((1, TQ, LANE), lambda b, i: (b, i, 0)),
                  fixed((1, LANE)), fixed((n_sel, n_cmp_pad)), fixed((1, NSA_HEADS))],
        out_specs=pl.BlockSpec((1, TQ, NSA_WIDTH), lambda b, i: (b, i, 0)),
        out_shape=jax.ShapeDtypeStruct((B, S, NSA_WIDTH), F32),
        scratch_shapes=[pltpu.VMEM((2, NSA_HEADS, V_ROWS, TQ), F32),
                        pltpu.VMEM((NSA_WIDTH, TQ), F32)],
        compiler_params=pltpu.CompilerParams(dimension_semantics=("parallel", "arbitrary"),
                                             vmem_limit_bytes=VMEM_LIMIT),
        name="nsa_attn",
    )(zq, kc, vct, ksa, kwa, vst, vwt, zgate, gb, jnp.asarray(ovlt), jnp.asarray(slopes))


def _mix_kernel(x_ref, yr_ref, yn_ref, zmg_ref, ur_ref, un_ref, wo_ref, gf_ref, wr_ref, br_ref,
                x1_ref, h2_ref, rt_ref, cnt_ref):
    tm = x_ref.shape[0]
    nt = (((1,), (1,)), ((), ()))
    chunks = [slice(c * TM_PROJ, (c + 1) * TM_PROJ) for c in range(tm // TM_PROJ)]
    up_r = [_bdot(yr_ref[c, :], ur_ref[...]) for c in chunks]
    up_n = [_bdot(yn_ref[c, :], un_ref[...]) for c in chunks]
    mixed = [jax.nn.sigmoid(zmg_ref[c, 0:D_MODEL]) * a + jax.nn.sigmoid(zmg_ref[c, D_MODEL:2 * D_MODEL]) * b
             for c, a, b in zip(chunks, up_r, up_n)]
    x1 = [x_ref[c, :] + _bdot(m, wo_ref[...]) for c, m in zip(chunks, mixed)]
    h2 = [_rms(v, gf_ref[...]) for v in x1]
    for c, v, h in zip(chunks, x1, h2):
        x1_ref[c, :] = v
        _store_row_tiles(h2_ref.at[c], h)
    n_row = wr_ref.shape[0]
    logits = jnp.concatenate(
        [lax.dot_general(wr_ref[...], h, nt, precision=HI, preferred_element_type=F32) for h in h2],
        axis=1) + br_ref[...]
    row = lax.broadcasted_iota(jnp.int32, (n_row, 1), 0).astype(F32)
    gl = jnp.where(row < N_GROUPS, logits, NEG_INF)
    gmax = jnp.max(gl, axis=0, keepdims=True)
    g_sel = jnp.min(jnp.where(gl == gmax, row, float(n_row)), axis=0, keepdims=True)
    p_group = 1.0 / jnp.sum(jnp.exp(gl - gmax), axis=0, keepdims=True)
    e_row = row - N_GROUPS
    in_grp = ((e_row >= g_sel * EXPERTS_PER_GROUP) & (e_row < (g_sel + 1.0) * EXPERTS_PER_GROUP)
              & (e_row < N_EXPERTS))
    el = jnp.where(in_grp, logits, NEG_INF)
    m1 = jnp.max(el, axis=0, keepdims=True)
    i1 = jnp.min(jnp.where(el == m1, e_row, float(n_row)), axis=0, keepdims=True)
    el2 = jnp.where(e_row == i1, 2.0 * NEG_INF, el)
    m2 = jnp.max(el2, axis=0, keepdims=True)
    i2 = jnp.min(jnp.where(el2 == m2, e_row, float(n_row)), axis=0, keepdims=True)
    r2 = jnp.exp(m2 - m1)
    g1 = p_group / (1.0 + r2)
    g2 = p_group * r2 / (1.0 + r2)

    @pl.when(pl.program_id(0) == 0)
    def _():
        cnt_ref[...] = jnp.zeros_like(cnt_ref)

    pick1, pick2 = e_row == i1, e_row == i2
    both = pick1.astype(F32) + pick2.astype(F32)
    earlier = (lax.broadcasted_iota(jnp.int32, (tm, tm), 0)
               < lax.broadcasted_iota(jnp.int32, (tm, tm), 1)).astype(BF16)
    before = jnp.dot(both.astype(BF16), earlier, preferred_element_type=F32) + cnt_ref[:, 0:1]
    rank1 = jnp.sum(jnp.where(pick1, before, 0.0), axis=0, keepdims=True)
    rank2 = jnp.sum(jnp.where(pick2, before, 0.0), axis=0, keepdims=True)
    cnt_ref[...] = cnt_ref[...] + jnp.sum(both, axis=1, keepdims=True)
    rt_ref[...] = jnp.concatenate([i1, i2, g1, g2, rank1, rank2, jnp.zeros((2, tm), F32)], axis=0)


def _mix(x2d, y_rw, y_nsa, zmg, w_up_r, w_up_n, w_out, g_ffn, w_group, b_group, w_router, b_router):
    T = x2d.shape[0]
    tm = MIX_CHUNKS * TM_PROJ
    n_r = N_GROUPS + N_EXPERTS
    n_row = -(-n_r // 8) * 8
    wr = jnp.pad(jnp.concatenate([w_group, w_router], axis=1).T, ((0, n_row - n_r), (0, 0)))
    br = jnp.pad(jnp.concatenate([b_group, b_router]), (0, n_row - n_r)).reshape(n_row, 1)
    row = lambda i: (i, 0)
    fixed = lambda i: (0, 0)
    return pl.pallas_call(
        _mix_kernel,
        grid=(T // tm,),
        in_specs=[pl.BlockSpec((tm, D_MODEL), row), pl.BlockSpec((tm, RW_WIDTH), row),
                  pl.BlockSpec((tm, NSA_WIDTH), row), pl.BlockSpec((tm, 2 * D_MODEL), row),
                  pl.BlockSpec((RW_WIDTH, D_MODEL), fixed), pl.BlockSpec((NSA_WIDTH, D_MODEL), fixed),
                  pl.BlockSpec((D_MODEL, D_MODEL), fixed), pl.BlockSpec((1, D_MODEL), fixed),
                  pl.BlockSpec((n_row, D_MODEL), fixed), pl.BlockSpec((n_row, 1), fixed)],
        out_specs=[pl.BlockSpec((tm, D_MODEL), row), pl.BlockSpec((tm,) + ROW_TILE, lambda i: (i, 0, 0)),
                   pl.BlockSpec((8, tm), lambda i: (0, i)), pl.BlockSpec((n_row, LANE), fixed)],
        out_shape=[jax.ShapeDtypeStruct((T, D_MODEL), F32), jax.ShapeDtypeStruct((T,) + ROW_TILE, F32),
                   jax.ShapeDtypeStruct((8, T), F32), jax.ShapeDtypeStruct((n_row, LANE), F32)],
        compiler_params=pltpu.CompilerParams(dimension_semantics=("arbitrary",),
                                             vmem_limit_bytes=VMEM_LIMIT),
        name="mix",
    )(x2d, y_rw, y_nsa, zmg, w_up_r.astype(BF16), w_up_n.astype(BF16), w_out.astype(BF16),
      g_ffn.reshape(1, D_MODEL), wr, br)


def _route_tables(rt, cnt, T):
    n_rows = T * TOP_K + N_EXPERTS * ROW_BLOCK
    n_blk = n_rows // ROW_BLOCK
    counts = cnt[N_GROUPS:N_GROUPS + N_EXPERTS, 0].astype(jnp.int32)
    padded = (counts + ROW_BLOCK - 1) // ROW_BLOCK * ROW_BLOCK
    pends = jnp.cumsum(padded)
    pstarts = pends - padded
    expert = rt[0:TOP_K].astype(jnp.int32)
    rank = rt[2 * TOP_K:3 * TOP_K].astype(jnp.int32)
    seg_start = jnp.sum(jnp.where(expert[..., None] == jnp.arange(N_EXPERTS), pstarts, 0), axis=-1)
    dest = (seg_start + rank).T.reshape(T * TOP_K)
    gates = rt[TOP_K:2 * TOP_K].T
    blk_start = jnp.arange(n_blk) * ROW_BLOCK
    blk_expert = jnp.minimum(jnp.sum(pends[None, :] <= blk_start[:, None], axis=1), N_EXPERTS - 1)
    n_active = (pends[N_EXPERTS - 1:] // ROW_BLOCK).astype(jnp.int32)
    return dest.astype(jnp.int32), gates, blk_expert.astype(jnp.int32), n_active


ROW_TILE = (D_MODEL // LANE, LANE)


def _store_row_tiles(ref, x):
    for c in range(ROW_TILE[0]):
        ref[:, c, :] = x[:, c * LANE:(c + 1) * LANE]


def _load_row_tiles(ref, idx):
    return jnp.concatenate([ref[(*idx, slice(None), c, slice(None))] for c in range(ROW_TILE[0])], axis=1)


TOK_BITS = 14
SLOT_SHIFT = int(math.log2(TOP_K))


def _row_info_kernel(dest_ref, word_ref, fill_hbm, info_ref, sem):
    fill = pltpu.make_async_copy(fill_hbm, info_ref, sem)
    fill.start()
    fill.wait()

    def body(a, carry):
        info_ref[dest_ref[a]] = word_ref[a]
        return carry

    lax.fori_loop(0, dest_ref.shape[0], body, 0, unroll=16)


def _row_info(dest, n_tok):
    n_rows = n_tok * TOP_K + N_EXPERTS * ROW_BLOCK
    assert n_tok <= 1 << TOK_BITS and (n_tok * TOP_K + 2 * ROW_BLOCK) << TOK_BITS < 2 ** 31
    a = jnp.arange(n_tok * TOP_K, dtype=jnp.int32)
    tok, k = a // TOP_K, a % TOP_K
    word = tok | ((k * n_tok + tok) << TOK_BITS)
    row = jnp.arange(n_rows, dtype=jnp.int32)
    spare = n_tok * TOP_K + ((row // ROW_BLOCK) % 2) * ROW_BLOCK + row % ROW_BLOCK
    return pl.pallas_call(
        _row_info_kernel,
        in_specs=[pl.BlockSpec(memory_space=pltpu.SMEM), pl.BlockSpec(memory_space=pltpu.SMEM),
                  pl.BlockSpec(memory_space=pl.ANY)],
        out_specs=pl.BlockSpec(memory_space=pltpu.SMEM),
        out_shape=jax.ShapeDtypeStruct((n_rows,), jnp.int32),
        scratch_shapes=[pltpu.SemaphoreType.DMA(())],
        name="row_info",
    )(dest, word, spare << TOK_BITS)


def _expert_kernel(be_ref, nact_ref, info_ref, h2_hbm, wgu_ref, wd_ref, eo_hbm,
                   xbuf, obuf, wgu_b, wd_b, gsem, ssem):
    i = pl.program_id(0)
    n_act = nact_ref[0]
    tok_mask = (1 << TOK_BITS) - 1

    def gather(blk, s):
        for r in range(ROW_BLOCK):
            tok = info_ref[blk * ROW_BLOCK + r] & tok_mask
            pltpu.make_async_copy(h2_hbm.at[tok], xbuf.at[s, r], gsem.at[s]).start()

    def scatter(blk, s):
        for r in range(ROW_BLOCK):
            row = info_ref[blk * ROW_BLOCK + r] >> TOK_BITS
            pltpu.make_async_copy(obuf.at[s, pl.ds(r, 1)], eo_hbm.at[pl.ds(row, 1)], ssem.at[s]).start()

    def drain_gather(s):
        pltpu.make_async_copy(h2_hbm.at[pl.ds(0, ROW_BLOCK)], xbuf.at[s], gsem.at[s]).wait()

    def drain_scatter(s):
        pltpu.make_async_copy(obuf.at[s], eo_hbm.at[pl.ds(0, ROW_BLOCK)], ssem.at[s]).wait()

    def block(s, first):
        drain_gather(s)
        gather(jnp.minimum(i + 1, n_act - 1), 1 - s)
        if not first:
            scatter(i - 1, 1 - s)
        gu = jnp.dot(_load_row_tiles(xbuf, (s,)).astype(BF16), wgu_b[...], preferred_element_type=F32)
        gate_h, up_h = gu[:, :D_EXPERT], gu[:, D_EXPERT:]
        mid = gate_h * jax.nn.sigmoid(gate_h) * up_h
        obuf[s] = jnp.dot(mid.astype(BF16), wd_b[...], preferred_element_type=F32)

    @pl.when(jnp.logical_and(i < n_act, jnp.logical_or(i == 0, be_ref[i] != be_ref[jnp.maximum(i - 1, 0)])))
    def _():
        wgu_b[...] = wgu_ref[0].astype(BF16)
        wd_b[...] = wd_ref[0].astype(BF16)

    @pl.when(i == 0)
    def _():
        gather(0, 0)
        block(0, first=True)

    for s in range(2):
        mine = i % 2 == s

        @pl.when(jnp.logical_and(mine, jnp.logical_and(i >= 1, i < n_act)))
        def _():
            @pl.when(i >= 2)
            def _():
                drain_scatter(s)
            block(s, first=False)

        @pl.when(jnp.logical_and(mine, i == n_act - 1))
        def _():
            drain_gather(1 - s)

            @pl.when(i >= 1)
            def _():
                drain_scatter(1 - s)
            scatter(i, s)
            drain_scatter(s)
            obuf[s] = jnp.zeros_like(obuf[s])
            for half in range(2):
                spare = eo_hbm.at[pl.ds(eo_hbm.shape[0] - (2 - half) * ROW_BLOCK, ROW_BLOCK)]
                pltpu.make_async_copy(obuf.at[s], spare, ssem.at[s]).start()
            for half in range(2):
                drain_scatter(s)


def _experts(h2, info, blk_expert, n_active, w_gate_up, w_down):
    n_tok = h2.shape[0]
    n_blk = blk_expert.shape[0]
    grid_spec = pltpu.PrefetchScalarGridSpec(
        num_scalar_prefetch=3,
        grid=(n_blk,),
        in_specs=[pl.BlockSpec(memory_space=pl.ANY),
                  pl.BlockSpec((1, D_MODEL, 2 * D_EXPERT), lambda i, be, na, info: (be[i], 0, 0)),
                  pl.BlockSpec((1, D_EXPERT, D_MODEL), lambda i, be, na, info: (be[i], 0, 0))],
        out_specs=pl.BlockSpec(memory_space=pl.ANY),
        scratch_shapes=[pltpu.VMEM((2, ROW_BLOCK) + ROW_TILE, F32), pltpu.VMEM((2, ROW_BLOCK, D_MODEL), F32),
                        pltpu.VMEM((D_MODEL, 2 * D_EXPERT), BF16), pltpu.VMEM((D_EXPERT, D_MODEL), BF16),
                        pltpu.SemaphoreType.DMA((2,)), pltpu.SemaphoreType.DMA((2,))])
    return pl.pallas_call(
        _expert_kernel,
        grid_spec=grid_spec,
        out_shape=jax.ShapeDtypeStruct((n_tok * TOP_K + 2 * ROW_BLOCK, D_MODEL), F32),
        compiler_params=pltpu.CompilerParams(dimension_semantics=("arbitrary",),
                                             vmem_limit_bytes=VMEM_LIMIT),
        name="experts",
    )(blk_expert, n_active, info, h2, w_gate_up, w_down)


def _ple_kernel(x1_ref, g_ref, p_ref, *rest, last_layer):
    eo_refs, (wpp_ref, gpn_ref, ggi_ref, wpg_ref, gfin_ref, y_ref) = rest[:TOP_K], rest[TOP_K:]
    tm = x1_ref.shape[0]
    chunks = [slice(c * TM_PROJ, (c + 1) * TM_PROJ) for c in range(tm // TM_PROJ)]
    x2 = []
    for c in chunks:
        moe = g_ref[c, 0:1] * eo_refs[0][c, :]
        for k in range(1, TOP_K):
            moe = moe + g_ref[c, k:k + 1] * eo_refs[k][c, :]
        x2.append(x1_ref[c, :] + moe)
    e = [_rms(_bdot(p_ref[c, :], wpp_ref[...]), gpn_ref[...]) for c in chunks]
    gate = [jax.nn.sigmoid(_bdot(_rms(v, ggi_ref[...]), wpg_ref[...])) for v in x2]
    for c, v, g, ee in zip(chunks, x2, gate, e):
        x3 = v + g * ee
        y_ref[c, :] = _rms(x3, gfin_ref[...]) if last_layer else x3


def _ple(x1, gates, expert_out, p2d, w_pp, g_pn, g_gi, w_pg, g_final, last_layer):
    T = x1.shape[0]
    tm = MIX_CHUNKS * TM_PROJ
    nt = T // tm
    row = lambda i: (i, 0)
    fixed = lambda i: (0, 0)
    vec = lambda a: a.reshape(1, D_MODEL)
    slot_rows = [pl.BlockSpec((tm, D_MODEL), functools.partial(lambda i, k: (i + k * nt, 0), k=k))
                 for k in range(TOP_K)]
    return pl.pallas_call(
        functools.partial(_ple_kernel, last_layer=last_layer),
        grid=(nt,),
        in_specs=[pl.BlockSpec((tm, D_MODEL), row), pl.BlockSpec((tm, TOP_K), row),
                  pl.BlockSpec((tm, PLE_DIM), row)] + slot_rows + [
                  pl.BlockSpec((PLE_DIM, D_MODEL), fixed), pl.BlockSpec((1, D_MODEL), fixed),
                  pl.BlockSpec((1, D_MODEL), fixed), pl.BlockSpec((D_MODEL, D_MODEL), fixed),
                  pl.BlockSpec((1, D_MODEL), fixed)],
        out_specs=pl.BlockSpec((tm, D_MODEL), row),
        out_shape=jax.ShapeDtypeStruct((T, D_MODEL), F32),
        compiler_params=pltpu.CompilerParams(dimension_semantics=("parallel",),
                                             vmem_limit_bytes=VMEM_LIMIT),
        name="ple",
    )(x1, gates, p2d, *([expert_out] * TOP_K), w_pp.astype(BF16), vec(g_pn), vec(g_gi),
      w_pg.astype(BF16), vec(g_final))


def kernel(x, p, g_mix, w_in, mu_shift, rw_w0, rw_w2, rw_a0, rw_a2, rw_g2, rw_k_k, rw_k_a, rw_r_k, rw_ln_w, rw_ln_b, cmp_pos_k, cmp_pos_v, cmp_k_w1, cmp_k_w2, cmp_v_w1, cmp_v_w2, nsa_gate_b, w_up_rwkv, w_up_nsa, w_out, g_ffn, w_group, b_group, w_router, b_router, w_exp_gate_up, w_exp_down, w_ple_proj, g_ple_norm, g_ple_gate_in, w_ple_gate, g_final):
    B, S, D = x.shape
    T = B * S
    depth = p.shape[0]
    xc = x.reshape(T, D)
    for i in range(depth):
        zrw, zq, zkv, zmg, zgate = _proj(xc, g_mix[i], w_in[i], mu_shift[i], S)
        y_rw = _rwkv(zrw.reshape(B, S, RW_IN), rw_w0[i], rw_w2[i], rw_a0[i], rw_a2[i], rw_g2[i],
                     rw_k_k[i], rw_k_a[i], rw_r_k[i], rw_ln_w[i], rw_ln_b[i])
        zkv3 = zkv.reshape(B, S, KV_IN)
        nsa_kv = _nsa_prep(zkv3, cmp_pos_k[i], cmp_pos_v[i], cmp_k_w1[i], cmp_k_w2[i],
                           cmp_v_w1[i], cmp_v_w2[i])
        y_nsa = _nsa_attn(zq.reshape(B, S, NSA_WIDTH), *nsa_kv, zgate.reshape(B, S, LANE),
                          nsa_gate_b[i])
        x1, h2, rt, cnt = _mix(xc, y_rw.reshape(T, RW_WIDTH), y_nsa.reshape(T, NSA_WIDTH), zmg,
                               w_up_rwkv[i], w_up_nsa[i], w_out[i], g_ffn[i], w_group[i], b_group[i],
                               w_router[i], b_router[i])
        dest, gates, blk_expert, n_active = _route_tables(rt, cnt, T)
        expert_out = _experts(h2, _row_info(dest, T), blk_expert, n_active, w_exp_gate_up[i],
                              w_exp_down[i])
        xc = _ple(x1, gates, expert_out, p[i].reshape(T, PLE_DIM), w_ple_proj[i], g_ple_norm[i],
                  g_ple_gate_in[i], w_ple_gate[i], g_final, i == depth - 1)
    return xc.reshape(B, S, D)
```

```python
import functools
import math

import jax
import jax.numpy as jnp
import numpy as np
from jax import lax
from jax.experimental import pallas as pl
from jax.experimental.pallas import tpu as pltpu

F32 = jnp.float32
BF16 = jnp.bfloat16
HI = lax.Precision.HIGHEST

D_MODEL = 1024
RW_HEADS = 8
RW_HEAD_DIM = 64
RW_WIDTH = 512
DECAY_LORA = 64
AAA_LORA = 64
GATE_LORA = 128
GN_EPS = 64e-5
RW_IN = 3 * RW_WIDTH + DECAY_LORA + AAA_LORA + GATE_LORA

NSA_HEADS = 8
NSA_KV_HEADS = 2
NSA_REP = NSA_HEADS // NSA_KV_HEADS
NSA_HEAD_DIM = 64
NSA_WIDTH = 512
NSA_KV_WIDTH = 128
CMP_BLOCK = 32
CMP_STRIDE = 16
SEL_BLOCK = 64
N_SELECT = 8
WINDOW = 512
N_NSA_BRANCH = 3
FORCE_SCORE = 1e6
NEG_INF = -1e30

N_GROUPS = 4
EXPERTS_PER_GROUP = 8
N_EXPERTS = 32
TOP_K = 2
D_EXPERT = 512
ROW_BLOCK = 256
PLE_DIM = 256
NORM_EPS = 1e-6

N_GATE = N_NSA_BRANCH * NSA_HEADS
ATT_IN = NSA_WIDTH + 6 * NSA_KV_WIDTH
KV_OFF = RW_IN + NSA_WIDTH
KV_IN = 6 * NSA_KV_WIDTH
GATE_OFF = RW_IN + ATT_IN
MERGE_OFF = GATE_OFF + N_GATE
LANE = 128

RW_CHUNK = 64
RW_SUB = 16
RW_ROWS = 4
TQ = 256
TK = 256
V_ROWS = NSA_HEAD_DIM + 16
TM_PROJ = 256
MIX_CHUNKS = 2
VMEM_LIMIT = 56 * 1024 * 1024


def _bdot(a, b):
    return jnp.dot(a.astype(BF16), b.astype(BF16), preferred_element_type=F32)


def _bdot_nt(a, b):
    return lax.dot_general(a.astype(BF16), b.astype(BF16), (((1,), (1,)), ((), ())),
                           preferred_element_type=F32)


def _rms(x, g):
    return x * lax.rsqrt(jnp.mean(x * x, axis=-1, keepdims=True) + NORM_EPS) * g


def _proj_kernel(x_ref, g_ref, w_ref, mu_ref, zrw_ref, zq_ref, zkv_ref, zmg_ref, zgate_ref,
                 carry_ref, *, tiles_per_seq):
    i = pl.program_id(0)
    tm = x_ref.shape[0]

    @pl.when(i % tiles_per_seq == 0)
    def _():
        carry_ref[...] = jnp.zeros_like(carry_ref)

    h = _rms(x_ref[...], g_ref[...]).astype(BF16)
    z = jnp.dot(h, w_ref[:, 0:RW_IN], preferred_element_type=F32)
    row = lax.broadcasted_iota(jnp.int32, (tm, 1), 0)
    prev = jnp.where(row == 0, carry_ref[7:8, :], pltpu.roll(z, 1, 0))
    carry_ref[...] = z[tm - 8:tm, :]
    zrw_ref[...] = z + (prev - z) * mu_ref[...]
    zq_ref[...] = jnp.dot(h, w_ref[:, RW_IN:KV_OFF], preferred_element_type=F32)
    zkv_ref[...] = jnp.dot(h, w_ref[:, KV_OFF:GATE_OFF], preferred_element_type=F32)
    zmg_ref[...] = jnp.dot(h, w_ref[:, GATE_OFF:GATE_OFF + 2 * D_MODEL], preferred_element_type=F32)
    zgate_ref[...] = jnp.dot(h, w_ref[:, GATE_OFF + 2 * D_MODEL:], preferred_element_type=F32)


def _proj(x2d, g_mix, w_in, mu, seq):
    T = x2d.shape[0]
    tm = TM_PROJ
    wp = jnp.concatenate(
        [w_in[:, :GATE_OFF], w_in[:, MERGE_OFF:],
         jnp.pad(w_in[:, GATE_OFF:MERGE_OFF], ((0, 0), (0, LANE - N_GATE)))], axis=1).astype(BF16)
    npad = wp.shape[1]
    row = lambda i: (i, 0)
    fixed = lambda i: (0, 0)
    return pl.pallas_call(
        functools.partial(_proj_kernel, tiles_per_seq=seq // tm),
        grid=(T // tm,),
        in_specs=[pl.BlockSpec((tm, D_MODEL), row), pl.BlockSpec((1, D_MODEL), fixed),
                  pl.BlockSpec((D_MODEL, npad), fixed), pl.BlockSpec((1, RW_IN), fixed)],
        out_specs=[pl.BlockSpec((tm, RW_IN), row), pl.BlockSpec((tm, NSA_WIDTH), row),
                   pl.BlockSpec((tm, KV_IN), row), pl.BlockSpec((tm, 2 * D_MODEL), row),
                   pl.BlockSpec((tm, LANE), row)],
        out_shape=[jax.ShapeDtypeStruct((T, RW_IN), F32), jax.ShapeDtypeStruct((T, NSA_WIDTH), F32),
                   jax.ShapeDtypeStruct((T, KV_IN), F32), jax.ShapeDtypeStruct((T, 2 * D_MODEL), F32),
                   jax.ShapeDtypeStruct((T, LANE), F32)],
        scratch_shapes=[pltpu.VMEM((8, RW_IN), F32)],
        compiler_params=pltpu.CompilerParams(dimension_semantics=("arbitrary",),
                                             vmem_limit_bytes=VMEM_LIMIT),
        name="proj",
    )(x2d, g_mix.reshape(1, D_MODEL), wp, mu.reshape(1, RW_IN))


PAIR = 2 * RW_HEAD_DIM


def _pair_blocks(x):
    low = lax.broadcasted_iota(jnp.int32, (1, PAIR), 1) < RW_HEAD_DIM
    return jnp.concatenate([jnp.where(low, x, 0.0), jnp.where(low, 0.0, x)], axis=0)


def _pmm(a, b):
    return _bdot(a, _pair_blocks(b))


def _unit_lower_inverse(a_strict, sub_mask, eye):
    ad = [jnp.where(sub_mask, a, 0.0) for a in a_strict]
    ao = [a - d for a, d in zip(a_strict, ad)]
    td = [eye - d for d in ad]
    pw = ad
    for _ in range(int(math.log2(RW_SUB)) - 1):
        pw = [_pmm(x, x) for x in pw]
        td = [_pmm(t, eye + x) for t, x in zip(td, pw)]
    n = [_pmm(t, o) for t, o in zip(td, ao)]
    t = [eye - x for x in n]
    pw = n
    for _ in range(int(math.log2(RW_CHUNK // RW_SUB)) - 1):
        pw = [_pmm(x, x) for x in pw]
        t = [_pmm(a, eye + x) for a, x in zip(t, pw)]
    return [_pmm(a, d) for a, d in zip(t, td)]


def _rwkv_kernel(z_ref, w0_ref, w2_ref, a0_ref, a2_ref, g2_ref, kk_ref, ka_ref, rk_ref, lnw_ref,
                 lnb_ref, avg_ref, o_ref, h_ref):
    c = pl.program_id(1)
    C = RW_CHUNK
    n_pair = RW_WIDTH // PAIR
    nt = (((1,), (1,)), ((), ()))

    @pl.when(c == 0)
    def _():
        h_ref[...] = jnp.zeros_like(h_ref)

    ti = lax.broadcasted_iota(jnp.int32, (C, 1), 0)
    si = lax.broadcasted_iota(jnp.int32, (1, PAIR), 1) & (RW_HEAD_DIM - 1)
    incl, strict = ti >= si, ti > si
    eye = (ti == si).astype(F32)
    sub_shift = int(math.log2(RW_SUB))
    sub_mask = (ti >> sub_shift) == (si >> sub_shift)
    row2 = lax.broadcasted_iota(jnp.int32, (PAIR, 1), 0)
    col2 = lax.broadcasted_iota(jnp.int32, (1, PAIR), 1)
    same_head = (row2 < RW_HEAD_DIM) == (col2 < RW_HEAD_DIM)
    eye2 = row2 == col2
    tri = (lax.broadcasted_iota(jnp.int32, (C, C), 0)
           >= lax.broadcasted_iota(jnp.int32, (C, C), 1)).astype(BF16)

    def head_mean(x):
        xs = jnp.concatenate([x[:, p * PAIR:(p + 1) * PAIR] for p in range(n_pair)], axis=0)
        ms = _bdot(xs, avg_ref[...])
        return jnp.concatenate([ms[p * C:(p + 1) * C] for p in range(n_pair)], axis=1)

    n_rows = z_ref.shape[0]
    rows = []
    for n in range(n_rows):
        z = z_ref[n]
        zr, zk, zv = z[:, 0:512], z[:, 512:1024], z[:, 1024:1536]
        zw, za, zg = z[:, 1536:1600], z[:, 1600:1664], z[:, 1664:1792]
        w_raw = w0_ref[...] + _bdot(jnp.tanh(zw), w2_ref[...])
        logw = -jax.nn.sigmoid(w_raw) * math.exp(-0.5)
        a = jax.nn.sigmoid(a0_ref[...] + _bdot(za, a2_ref[...]))
        gate = _bdot(jax.nn.sigmoid(zg), g2_ref[...])
        kk = zk * kk_ref[...]
        kk = kk / jnp.maximum(jnp.sqrt(head_mean(kk * kk) * RW_HEAD_DIM), 1e-12)
        k = zk * (1.0 + (a - 1.0) * ka_ref[...])
        b = kk * a

        w_hi = logw.astype(BF16)
        w_lo = (logw - w_hi.astype(F32)).astype(BF16)
        cum = (jnp.dot(tri, w_hi, preferred_element_type=F32)
               + jnp.dot(tri, w_lo, preferred_element_type=F32))
        cum_last = cum[C - 1:C, :]
        g_inv = jnp.exp(-cum)
        g_end = jnp.exp(cum_last - cum)
        rows.append(dict(rt=zr * jnp.exp(cum), kt=k * g_inv, bt=b * g_inv, qt=kk * jnp.exp(cum - logw),
                         kh=k * g_end, bh=b * g_end, v=zv, g_last=jnp.exp(cum_last), gate=gate,
                         bonus=head_mean(zr * k * rk_ref[...]) * RW_HEAD_DIM * zv))

    chains = [(n, slice(p * PAIR, (p + 1) * PAIR)) for n in range(n_rows) for p in range(n_pair)]
    part = lambda name: [rows[n][name][:, sl] for n, sl in chains]
    qt, rt, kt, bt, kh, bh, v = (part(x) for x in ("qt", "rt", "kt", "bt", "kh", "bh", "v"))
    lhs = [jnp.concatenate([q, r], axis=0).astype(BF16) for q, r in zip(qt, rt)]
    ab = [lax.dot_general(l, _pair_blocks(x).astype(BF16), nt, preferred_element_type=F32)
          for l, x in zip(lhs, bt)]
    ak = [lax.dot_general(l, _pair_blocks(x).astype(BF16), nt, preferred_element_type=F32)
          for l, x in zip(lhs, kt)]
    a_kb = [jnp.where(strict, x[0:C], 0.0) for x in ab]
    a_rb = [jnp.where(incl, x[C:2 * C], 0.0) for x in ab]
    a_kk = [jnp.where(strict, x[0:C], 0.0) for x in ak]
    a_rk = [jnp.where(incl, x[C:2 * C], 0.0) for x in ak]
    t_inv = _unit_lower_inverse(a_kb, sub_mask, eye)

    h = [h_ref[n, sl.start // PAIR] for n, sl in chains]
    vb = [_pair_blocks(x) for x in v]
    rhs = [_bdot(jnp.concatenate([q, akk], axis=1), jnp.concatenate([hh, vv], axis=0))
           for q, akk, hh, vv in zip(qt, a_kk, h, vb)]
    u = [_pmm(t, x) for t, x in zip(t_inv, rhs)]
    outs = [_bdot(jnp.concatenate([r, ark, -arb], axis=1), jnp.concatenate([hh, vv, _pair_blocks(uu)], axis=0))
            for r, ark, arb, hh, vv, uu in zip(rt, a_rk, a_rb, h, vb, u)]
    upd = [_bdot(jnp.concatenate([x, -y], axis=0).T, jnp.concatenate([vv, uu], axis=0))
           for x, y, vv, uu in zip(kh, bh, v, u)]
    for (n, sl), hh, dd in zip(chains, h, upd):
        decay_col = jnp.sum(jnp.where(eye2, rows[n]["g_last"][:, sl], 0.0), axis=1, keepdims=True)
        h_ref[n, sl.start // PAIR] = decay_col * hh + jnp.where(same_head, dd, 0.0)

    for n in range(n_rows):
        o = jnp.concatenate(outs[n * n_pair:(n + 1) * n_pair], axis=1)
        d = o - head_mean(o)
        on = d * lax.rsqrt(head_mean(d * d) + GN_EPS)
        o_ref[n] = (on * lnw_ref[...] + lnb_ref[...] + rows[n]["bonus"]) * rows[n]["gate"]


def _rwkv(zrw, w0, w2, a0, a2, g2, k_k, k_a, r_k, ln_w, ln_b):
    B, S, _ = zrw.shape
    C = RW_CHUNK
    nb = RW_ROWS
    hid = np.arange(PAIR) // RW_HEAD_DIM
    avg = jnp.asarray((hid[:, None] == hid[None, :]).astype(np.float32) / RW_HEAD_DIM)
    vec = lambda a: a.reshape(1, RW_WIDTH)
    fixed = lambda shape: pl.BlockSpec(shape, lambda b, c: (0,) * len(shape))
    return pl.pallas_call(
        _rwkv_kernel,
        grid=(B // nb, S // C),
        in_specs=[pl.BlockSpec((nb, C, RW_IN), lambda b, c: (b, c, 0)),
                  fixed((1, RW_WIDTH)), fixed((DECAY_LORA, RW_WIDTH)),
                  fixed((1, RW_WIDTH)), fixed((AAA_LORA, RW_WIDTH)),
                  fixed((GATE_LORA, RW_WIDTH)), fixed((1, RW_WIDTH)), fixed((1, RW_WIDTH)),
                  fixed((1, RW_WIDTH)), fixed((1, RW_WIDTH)), fixed((1, RW_WIDTH)),
                  fixed((PAIR, PAIR))],
        out_specs=pl.BlockSpec((nb, C, RW_WIDTH), lambda b, c: (b, c, 0)),
        out_shape=jax.ShapeDtypeStruct((B, S, RW_WIDTH), F32),
        scratch_shapes=[pltpu.VMEM((nb, RW_WIDTH // PAIR, PAIR, PAIR), F32)],
        compiler_params=pltpu.CompilerParams(dimension_semantics=("parallel", "arbitrary"),
                                             vmem_limit_bytes=VMEM_LIMIT),
        name="rwkv",
    )(zrw, vec(w0), w2, vec(a0), a2, g2, vec(k_k), vec(k_a), vec(r_k), vec(ln_w), vec(ln_b), avg)


def _gelu_tanh(x):
    return 0.5 * x * (1.0 + jnp.tanh(math.sqrt(2.0 / math.pi) * (x + 0.044715 * (x * x * x))))


def _key_features(pos_hi, pos_lo, block, n, n_sel):
    lane = lax.broadcasted_iota(jnp.int32, (n, NSA_HEAD_DIM), 1)
    feat = jnp.where(lane == n_sel, pos_hi, jnp.where(lane == n_sel + 1, pos_lo, 0.0))
    return feat if block is None else jnp.where(lane == block, 1.0, feat)


def _nsa_prep_kernel(zkc_ref, zvc_ref, zks_ref, zvs_ref, zkw_ref, zvw_ref, pk_ref, pv_ref, kw1_ref,
                     kw2_ref, vw1_ref, vw2_ref, kc_ref, vct_ref, ksa_ref, kwa_ref, vst_ref, vwt_ref):
    S = zkc_ref.shape[1]
    n_grp = S // CMP_STRIDE
    Dh = NSA_HEAD_DIM
    half = CMP_BLOCK // 2
    n_sel = S // SEL_BLOCK
    jrow = lax.broadcasted_iota(jnp.int32, (n_grp, 1), 0)
    cmp_feat = _key_features((jrow >> 3).astype(F32),
                             ((jrow & 7) * CMP_STRIDE).astype(F32) + 0.5 * (CMP_BLOCK - 1),
                             None, n_grp, n_sel)
    for is_v, (z_ref, pos_ref, w1_ref, w2_ref) in enumerate(((zkc_ref, pk_ref, kw1_ref, kw2_ref),
                                                            (zvc_ref, pv_ref, vw1_ref, vw2_ref))):
        for g in range(NSA_KV_HEADS):
            lo = jnp.zeros((n_grp, Dh), F32)
            hi = jnp.zeros((n_grp, Dh), F32)
            for l in range(half):
                xs = z_ref[0, pl.ds(l, n_grp, stride=CMP_STRIDE), :]
                xg = xs[:, g * Dh:(g + 1) * Dh]
                lo = lo + _bdot(xg + pos_ref[l:l + 1, :], w1_ref[l * Dh:(l + 1) * Dh, :])
                hi = hi + _bdot(xg + pos_ref[half + l:half + l + 1, :],
                                w1_ref[(half + l) * Dh:(half + l + 1) * Dh, :])
            pre = lo + pltpu.roll(hi, n_grp - 1, 0)
            out = jnp.where(jrow < n_grp - 1, _bdot(_gelu_tanh(pre), w2_ref[...]), 0.0)
            if is_v:
                out_t = jnp.concatenate([out, jnp.zeros_like(out)], axis=1).T
                vct_ref[0, g] = out_t[0:Dh, :].astype(BF16)
            else:
                kc_ref[0, g] = jnp.concatenate([out, cmp_feat], axis=1)

    prow = lax.broadcasted_iota(jnp.int32, (S, 1), 0)
    p_hi, p_lo = (prow >> 7).astype(F32), (prow & (LANE - 1)).astype(F32)
    for z_ref, out_ref, block in ((zks_ref, ksa_ref, prow >> int(math.log2(SEL_BLOCK))),
                                  (zkw_ref, kwa_ref, None)):
        kfull = z_ref[0]
        key_feat = _key_features(p_hi, p_lo, block, S, n_sel)
        for g in range(NSA_KV_HEADS):
            out_ref[0, g] = jnp.concatenate([kfull[:, g * Dh:(g + 1) * Dh], key_feat], axis=1).astype(BF16)
    ones_row = (lax.broadcasted_iota(jnp.int32, (V_ROWS - Dh, TK), 0) == 0).astype(F32)
    for z_ref, out_ref in ((zvs_ref, vst_ref), (zvw_ref, vwt_ref)):
        for j in range(S // TK):
            vt = z_ref[0, j * TK:(j + 1) * TK, :].T
            out_ref[0, j] = jnp.concatenate(
                [piece for g in range(NSA_KV_HEADS) for piece in (vt[g * Dh:(g + 1) * Dh], ones_row)],
                axis=0).astype(BF16)


def _nsa_prep(zkv, pos_k, pos_v, kw1, kw2, vw1, vw2):
    B, S, _ = zkv.shape
    n_grp = S // CMP_STRIDE
    n_kt = S // TK
    Dh = NSA_HEAD_DIM
    G = NSA_KV_HEADS
    fixed = lambda shape: pl.BlockSpec(shape, lambda b: (0,) * len(shape))
    col = lambda c: pl.BlockSpec((1, S, NSA_KV_WIDTH), lambda b: (b, 0, c))
    whole = lambda shape: pl.BlockSpec((1,) + shape, lambda b: (b,) + (0,) * len(shape))
    shapes = [((G, n_grp, 2 * Dh), F32), ((G, Dh, n_grp), BF16), ((G, S, 2 * Dh), BF16),
              ((G, S, 2 * Dh), BF16), ((n_kt, G * V_ROWS, TK), BF16), ((n_kt, G * V_ROWS, TK), BF16)]
    return pl.pallas_call(
        _nsa_prep_kernel,
        grid=(B,),
        in_specs=[col(c) for c in range(6)] + [
            fixed((CMP_BLOCK, Dh)), fixed((CMP_BLOCK, Dh)),
            fixed((CMP_BLOCK * Dh, Dh)), fixed((Dh, Dh)),
            fixed((CMP_BLOCK * Dh, Dh)), fixed((Dh, Dh))],
        out_specs=[whole(s) for s, _ in shapes],
        out_shape=[jax.ShapeDtypeStruct((B,) + s, d) for s, d in shapes],
        compiler_params=pltpu.CompilerParams(dimension_semantics=("parallel",),
                                             vmem_limit_bytes=VMEM_LIMIT),
        name="nsa_prep",
    )(zkv, zkv, zkv, zkv, zkv, zkv, pos_k, pos_v, kw1, kw2, vw1, vw2)


def _nsa_attn_kernel(q_ref, kc_ref, vct_ref, ksa_ref, kwa_ref, vst_ref, vwt_ref, gl_ref, gb_ref,
                     ovlt_ref, slope_ref, o_ref, acc_ref, ot_ref):
    i = pl.program_id(1)
    Dh = NSA_HEAD_DIM
    R = NSA_REP
    N = R * TQ
    n_cmp_pad = kc_ref.shape[2]
    n_sel = ovlt_ref.shape[0]
    G = NSA_KV_HEADS
    nt = (((1,), (1,)), ((), ()))
    log2e = math.log2(math.e)
    t0 = i * TQ
    t_row = t0 + lax.broadcasted_iota(jnp.int32, (1, TQ), 1)
    c_col = lax.broadcasted_iota(jnp.int32, (TK, 1), 0)
    sgate_t = jax.nn.sigmoid(gl_ref[0] + gb_ref[...]).T
    lane_f = lax.broadcasted_iota(jnp.int32, (1, Dh), 1)
    heads = lambda x: jnp.concatenate([x] * R, axis=1)

    def key_dist(j):
        return t_row - (j * TK + c_col)

    def queries(g, sel_feat):
        parts = []
        for r in range(R):
            h = g * R + r
            sl = slope_ref[:, h:h + 1] * log2e
            feat = jnp.where(lane_f == n_sel, sl * LANE, jnp.where(lane_f == n_sel + 1, sl, sel_feat))
            parts.append(jnp.concatenate(
                [q_ref[0, :, h * Dh:(h + 1) * Dh] * (Dh ** -0.5 * log2e), jnp.broadcast_to(feat, (TQ, Dh))],
                axis=1))
        return jnp.concatenate(parts, axis=0)

    qab, o_cmp = [], []
    for g in range(G):
        jc = lax.broadcasted_iota(jnp.int32, (n_cmp_pad, 1), 0)
        ok_c = (jc * CMP_STRIDE + (CMP_BLOCK - 1) <= t_row) & (jc < n_cmp_pad - 1)
        s_c = (lax.dot_general(kc_ref[0, g], queries(g, 0.0), nt, precision=HI, preferred_element_type=F32)
               + heads(jnp.where(ok_c, 0.0, NEG_INF)))
        e_c = jnp.exp2(s_c - jnp.max(s_c, axis=0, keepdims=True))
        any_c = heads(t_row >= CMP_BLOCK - 1)
        p_c = e_c * jnp.where(any_c, 1.0 / jnp.sum(e_c, axis=0, keepdims=True), 0.0)
        o_cmp.append(jnp.dot(vct_ref[0, g], p_c.astype(BF16), preferred_element_type=F32))

        p_sum = p_c[:, 0:TQ]
        for r in range(1, R):
            p_sum = p_sum + p_c[:, r * TQ:(r + 1) * TQ]
        imp = jnp.dot(ovlt_ref[...], p_sum, precision=HI, preferred_element_type=F32)
        kb = lax.broadcasted_iota(jnp.int32, (n_sel, 1), 0)
        kbf = kb.astype(F32)
        blk_t = t_row >> int(math.log2(SEL_BLOCK))
        forced = (kb == 0) | (kb == blk_t) | (kb == blk_t - 1)
        cur = jnp.where(forced, FORCE_SCORE, jnp.where(kb <= blk_t, imp, -FORCE_SCORE))
        sel_bias = jnp.full((n_sel, TQ), NEG_INF, F32)
        for _ in range(min(N_SELECT, n_sel)):
            mx = jnp.max(cur, axis=0, keepdims=True)
            first = jnp.min(jnp.where(cur == mx, kbf, float(n_sel)), axis=0, keepdims=True)
            hit = kbf == first
            sel_bias = jnp.where(hit, 0.0, sel_bias)
            cur = jnp.where(hit, -3e38, cur)
        sel_feat = jnp.concatenate([sel_bias, jnp.zeros((LANE - n_sel, TQ), F32)], axis=0).T[:, 0:Dh]
        qab.append(queries(g, sel_feat).astype(BF16))

    VR = vst_ref.shape[2] // G

    def tile(j, m, k_ref, vt_ref, bias, slot):
        ks = [k_ref[0, g, pl.ds(pl.multiple_of(j * TK, TK), TK), :] for g in range(G)]
        vts = [vt_ref[0, j, g * VR:(g + 1) * VR, :] for g in range(G)]
        hs = range(NSA_HEADS)
        s = [lax.dot_general(ks[h // R], qab[h // R][(h % R) * TQ:(h % R + 1) * TQ], nt,
                             preferred_element_type=F32) for h in hs]
        if bias is not None:
            s = [x + bias for x in s]
        m_new = [jnp.maximum(m[h], jnp.max(s[h], axis=0, keepdims=True)) for h in hs]
        alpha = [jnp.exp2(m[h] - m_new[h]) for h in hs]
        p = [jnp.exp2(s[h] - m_new[h]).astype(BF16) for h in hs]
        pv = [jnp.dot(vts[h // R], p[h], preferred_element_type=F32) for h in hs]
        for h in hs:
            acc_ref[slot, h] = alpha[h] * acc_ref[slot, h] + pv[h]
        return tuple(m_new)

    def window_bias(j):
        d = key_dist(j)
        return jnp.where((d >= 0) & (d < WINDOW), 0.0, NEG_INF)

    init = (jnp.full((1, TQ), NEG_INF, F32),) * NSA_HEADS
    acc_ref[...] = jnp.zeros_like(acc_ref)
    causal = jnp.where(key_dist(i) >= 0, 0.0, NEG_INF)
    carry = lax.fori_loop(0, i, lambda j, c: tile(j, c, ksa_ref, vst_ref, None, 0), init)
    tile(i, carry, ksa_ref, vst_ref, causal, 0)
    carry = lax.fori_loop(jnp.maximum(i - WINDOW // TK, 0), i,
                          lambda j, c: tile(j, c, kwa_ref, vwt_ref, window_bias(j), 1), init)
    tile(i, carry, kwa_ref, vwt_ref, causal, 1)

    for h in range(NSA_HEADS):
        g, r = divmod(h, R)
        acc_s, acc_w = acc_ref[0, h], acc_ref[1, h]
        ot_ref[h * Dh:(h + 1) * Dh, :] = (
            sgate_t[3 * h:3 * h + 1, :] * o_cmp[g][:, r * TQ:(r + 1) * TQ]
            + sgate_t[3 * h + 1:3 * h + 2, :] * (acc_s[0:Dh] * (1.0 / acc_s[Dh:Dh + 1]))
            + sgate_t[3 * h + 2:3 * h + 3, :] * (acc_w[0:Dh] * (1.0 / acc_w[Dh:Dh + 1])))
    o_ref[0] = ot_ref[...].T


def _nsa_attn(zq, kc, vct, ksa, kwa, vst, vwt, zgate, gate_b):
    B, S, _ = zq.shape
    n_sel = S // SEL_BLOCK
    n_cmp = (S - CMP_BLOCK) // CMP_STRIDE + 1
    n_cmp_pad = kc.shape[2]
    n_kt = S // TK
    G, Dh = NSA_KV_HEADS, NSA_HEAD_DIM
    cmp_start = np.arange(n_cmp) * CMP_STRIDE
    sel_start = np.arange(n_sel) * SEL_BLOCK
    overlap = np.clip(np.minimum(cmp_start[:, None] + CMP_BLOCK, sel_start[None, :] + SEL_BLOCK)
                      - np.maximum(cmp_start[:, None], sel_start[None, :]), 0, None) / CMP_BLOCK
    ovlt = np.zeros((n_sel, n_cmp_pad), np.float32)
    ovlt[:, :n_cmp] = overlap.T
    slopes = (2.0 ** (-8.0 * np.arange(1, NSA_HEADS + 1) / NSA_HEADS)).astype(np.float32).reshape(1, NSA_HEADS)
    gb = jnp.pad(gate_b, (0, LANE - N_GATE)).reshape(1, LANE)
    fixed = lambda shape: pl.BlockSpec(shape, lambda b, i: (0,) * len(shape))
    per_b = lambda shape: pl.BlockSpec((1,) + shape, lambda b, i: (b,) + (0,) * len(shape))
    return pl.pallas_call(
        _nsa_attn_kernel,
        grid=(B, S // TQ),
        in_specs=[pl.BlockSpec((1, TQ, NSA_WIDTH), lambda b, i: (b, i, 0)),
                  per_b((G, n_cmp_pad, 2 * Dh)), per_b((G, Dh, n_cmp_pad)),
                  per_b((G, S, 2 * Dh)), per_b((G, S, 2 * Dh)),
                  per_b((n_kt, G * V_ROWS, TK)), per_b((n_kt, G * V_ROWS, TK)),
                  pl.BlockSpec((1, TQ, LANE), lambda b, i: (b, i, 0)),
                  fixed((1, LANE)), fixed((n_sel, n_cmp_pad)), fixed((1, NSA_HEADS))],
        out_specs=pl.BlockSpec((1, TQ, NSA_WIDTH), lambda b, i: (b, i, 0)),
        out_shape=jax.ShapeDtypeStruct((B, S, NSA_WIDTH), F32),
        scratch_shapes=[pltpu.VMEM((2, NSA_HEADS, V_ROWS, TQ), F32),
                        pltpu.VMEM((NSA_WIDTH, TQ), F32)],
        compiler_params=pltpu.CompilerParams(dimension_semantics=("parallel", "arbitrary"),
                                             vmem_limit_bytes=VMEM_LIMIT),
        name="nsa_attn",
    )(zq, kc, vct, ksa, kwa, vst, vwt, zgate, gb, jnp.asarray(ovlt), jnp.asarray(slopes))


def _mix_kernel(x_ref, yr_ref, yn_ref, zmg_ref, ur_ref, un_ref, wo_ref, gf_ref, wr_ref, br_ref,
                x1_ref, h2_ref, rt_ref, cnt_ref):
    tm = x_ref.shape[0]
    nt = (((1,), (1,)), ((), ()))
    chunks = [slice(c * TM_PROJ, (c + 1) * TM_PROJ) for c in range(tm // TM_PROJ)]
    up_r = [_bdot(yr_ref[c, :], ur_ref[...]) for c in chunks]
    up_n = [_bdot(yn_ref[c, :], un_ref[...]) for c in chunks]
    mixed = [jax.nn.sigmoid(zmg_ref[c, 0:D_MODEL]) * a + jax.nn.sigmoid(zmg_ref[c, D_MODEL:2 * D_MODEL]) * b
             for c, a, b in zip(chunks, up_r, up_n)]
    x1 = [x_ref[c, :] + _bdot(m, wo_ref[...]) for c, m in zip(chunks, mixed)]
    h2 = [_rms(v, gf_ref[...]) for v in x1]
    for c, v, h in zip(chunks, x1, h2):
        x1_ref[c, :] = v
        _store_row_tiles(h2_ref.at[c], h)
    n_row = wr_ref.shape[0]
    logits = jnp.concatenate(
        [lax.dot_general(wr_ref[...], h, nt, precision=HI, preferred_element_type=F32) for h in h2],
        axis=1) + br_ref[...]
    row = lax.broadcasted_iota(jnp.int32, (n_row, 1), 0).astype(F32)
    gl = jnp.where(row < N_GROUPS, logits, NEG_INF)
    gmax = jnp.max(gl, axis=0, keepdims=True)
    g_sel = jnp.min(jnp.where(gl == gmax, row, float(n_row)), axis=0, keepdims=True)
    p_group = 1.0 / jnp.sum(jnp.exp(gl - gmax), axis=0, keepdims=True)
    e_row = row - N_GROUPS
    in_grp = ((e_row >= g_sel * EXPERTS_PER_GROUP) & (e_row < (g_sel + 1.0) * EXPERTS_PER_GROUP)
              & (e_row < N_EXPERTS))
    el = jnp.where(in_grp, logits, NEG_INF)
    m1 = jnp.max(el, axis=0, keepdims=True)
    i1 = jnp.min(jnp.where(el == m1, e_row, float(n_row)), axis=0, keepdims=True)
    el2 = jnp.where(e_row == i1, 2.0 * NEG_INF, el)
    m2 = jnp.max(el2, axis=0, keepdims=True)
    i2 = jnp.min(jnp.where(el2 == m2, e_row, float(n_row)), axis=0, keepdims=True)
    r2 = jnp.exp(m2 - m1)
    g1 = p_group / (1.0 + r2)
    g2 = p_group * r2 / (1.0 + r2)

    @pl.when(pl.program_id(0) == 0)
    def _():
        cnt_ref[...] = jnp.zeros_like(cnt_ref)

    pick1, pick2 = e_row == i1, e_row == i2
    both = pick1.astype(F32) + pick2.astype(F32)
    earlier = (lax.broadcasted_iota(jnp.int32, (tm, tm), 0)
               < lax.broadcasted_iota(jnp.int32, (tm, tm), 1)).astype(BF16)
    before = jnp.dot(both.astype(BF16), earlier, preferred_element_type=F32) + cnt_ref[:, 0:1]
    rank1 = jnp.sum(jnp.where(pick1, before, 0.0), axis=0, keepdims=True)
    rank2 = jnp.sum(jnp.where(pick2, before, 0.0), axis=0, keepdims=True)
    cnt_ref[...] = cnt_ref[...] + jnp.sum(both, axis=1, keepdims=True)
    rt_ref[...] = jnp.concatenate([i1, i2, g1, g2, rank1, rank2, jnp.zeros((2, tm), F32)], axis=0)


def _mix(x2d, y_rw, y_nsa, zmg, w_up_r, w_up_n, w_out, g_ffn, w_group, b_group, w_router, b_router):
    T = x2d.shape[0]
    tm = MIX_CHUNKS * TM_PROJ
    n_r = N_GROUPS + N_EXPERTS
    n_row = -(-n_r // 8) * 8
    wr = jnp.pad(jnp.concatenate([w_group, w_router], axis=1).T, ((0, n_row - n_r), (0, 0)))
    br = jnp.pad(jnp.concatenate([b_group, b_router]), (0, n_row - n_r)).reshape(n_row, 1)
    row = lambda i: (i, 0)
    fixed = lambda i: (0, 0)
    return pl.pallas_call(
        _mix_kernel,
        grid=(T // tm,),
        in_specs=[pl.BlockSpec((tm, D_MODEL), row), pl.BlockSpec((tm, RW_WIDTH), row),
                  pl.BlockSpec((tm, NSA_WIDTH), row), pl.BlockSpec((tm, 2 * D_MODEL), row),
                  pl.BlockSpec((RW_WIDTH, D_MODEL), fixed), pl.BlockSpec((NSA_WIDTH, D_MODEL), fixed),
                  pl.BlockSpec((D_MODEL, D_MODEL), fixed), pl.BlockSpec((1, D_MODEL), fixed),
                  pl.BlockSpec((n_row, D_MODEL), fixed), pl.BlockSpec((n_row, 1), fixed)],
        out_specs=[pl.BlockSpec((tm, D_MODEL), row), pl.BlockSpec((tm,) + ROW_TILE, lambda i: (i, 0, 0)),
                   pl.BlockSpec((8, tm), lambda i: (0, i)), pl.BlockSpec((n_row, LANE), fixed)],
        out_shape=[jax.ShapeDtypeStruct((T, D_MODEL), F32), jax.ShapeDtypeStruct((T,) + ROW_TILE, F32),
                   jax.ShapeDtypeStruct((8, T), F32), jax.ShapeDtypeStruct((n_row, LANE), F32)],
        compiler_params=pltpu.CompilerParams(dimension_semantics=("arbitrary",),
                                             vmem_limit_bytes=VMEM_LIMIT),
        name="mix",
    )(x2d, y_rw, y_nsa, zmg, w_up_r.astype(BF16), w_up_n.astype(BF16), w_out.astype(BF16),
      g_ffn.reshape(1, D_MODEL), wr, br)


def _route_tables(rt, cnt, T):
    n_rows = T * TOP_K + N_EXPERTS * ROW_BLOCK
    n_blk = n_rows // ROW_BLOCK
    counts = cnt[N_GROUPS:N_GROUPS + N_EXPERTS, 0].astype(jnp.int32)
    padded = (counts + ROW_BLOCK - 1) // ROW_BLOCK * ROW_BLOCK
    pends = jnp.cumsum(padded)
    pstarts = pends - padded
    expert = rt[0:TOP_K].astype(jnp.int32)
    rank = rt[2 * TOP_K:3 * TOP_K].astype(jnp.int32)
    seg_start = jnp.sum(jnp.where(expert[..., None] == jnp.arange(N_EXPERTS), pstarts, 0), axis=-1)
    dest = (seg_start + rank).T.reshape(T * TOP_K)
    gates = rt[TOP_K:2 * TOP_K].T
    blk_start = jnp.arange(n_blk) * ROW_BLOCK
    blk_expert = jnp.minimum(jnp.sum(pends[None, :] <= blk_start[:, None], axis=1), N_EXPERTS - 1)
    n_active = (pends[N_EXPERTS - 1:] // ROW_BLOCK).astype(jnp.int32)
    return dest.astype(jnp.int32), gates, blk_expert.astype(jnp.int32), n_active


ROW_TILE = (D_MODEL // LANE, LANE)


def _store_row_tiles(ref, x):
    for c in range(ROW_TILE[0]):
        ref[:, c, :] = x[:, c * LANE:(c + 1) * LANE]


def _load_row_tiles(ref, idx):
    return jnp.concatenate([ref[(*idx, slice(None), c, slice(None))] for c in range(ROW_TILE[0])], axis=1)


TOK_BITS = 14
SLOT_SHIFT = int(math.log2(TOP_K))


def _row_info_kernel(dest_ref, word_ref, fill_hbm, info_ref, sem):
    fill = pltpu.make_async_copy(fill_hbm, info_ref, sem)
    fill.start()
    fill.wait()

    def body(a, carry):
        info_ref[dest_ref[a]] = word_ref[a]
        return carry

    lax.fori_loop(0, dest_ref.shape[0], body, 0, unroll=16)


def _row_info(dest, n_tok):
    n_rows = n_tok * TOP_K + N_EXPERTS * ROW_BLOCK
    assert n_tok <= 1 << TOK_BITS and (n_tok * TOP_K + 2 * ROW_BLOCK) << TOK_BITS < 2 ** 31
    a = jnp.arange(n_tok * TOP_K, dtype=jnp.int32)
    tok, k = a // TOP_K, a % TOP_K
    word = tok | ((k * n_tok + tok) << TOK_BITS)
    row = jnp.arange(n_rows, dtype=jnp.int32)
    spare = n_tok * TOP_K + ((row // ROW_BLOCK) % 2) * ROW_BLOCK + row % ROW_BLOCK
    return pl.pallas_call(
        _row_info_kernel,
        in_specs=[pl.BlockSpec(memory_space=pltpu.SMEM), pl.BlockSpec(memory_space=pltpu.SMEM),
                  pl.BlockSpec(memory_space=pl.ANY)],
        out_specs=pl.BlockSpec(memory_space=pltpu.SMEM),
        out_shape=jax.ShapeDtypeStruct((n_rows,), jnp.int32),
        scratch_shapes=[pltpu.SemaphoreType.DMA(())],
        name="row_info",
    )(dest, word, spare << TOK_BITS)


def _expert_kernel(be_ref, nact_ref, info_ref, h2_hbm, wgu_ref, wd_ref, eo_hbm,
                   xbuf, obuf, wgu_b, wd_b, gsem, ssem):
    i = pl.program_id(0)
    n_act = nact_ref[0]
    tok_mask = (1 << TOK_BITS) - 1

    def gather(blk, s):
        for r in range(ROW_BLOCK):
            tok = info_ref[blk * ROW_BLOCK + r] & tok_mask
            pltpu.make_async_copy(h2_hbm.at[tok], xbuf.at[s, r], gsem.at[s]).start()

    def scatter(blk, s):
        for r in range(ROW_BLOCK):
            row = info_ref[blk * ROW_BLOCK + r] >> TOK_BITS
            pltpu.async_copy(obuf.at[s, pl.ds(r, 1)], eo_hbm.at[pl.ds(row, 1)], ssem.at[s], priority=1)

    def drain_gather(s):
        pltpu.make_async_copy(h2_hbm.at[pl.ds(0, ROW_BLOCK)], xbuf.at[s], gsem.at[s]).wait()

    def drain_scatter(s):
        pltpu.make_async_copy(obuf.at[s], eo_hbm.at[pl.ds(0, ROW_BLOCK)], ssem.at[s]).wait()

    def block(s, first):
        drain_gather(s)
        gather(jnp.minimum(i + 1, n_act - 1), 1 - s)
        if not first:
            scatter(i - 1, 1 - s)
        gu = jnp.dot(_load_row_tiles(xbuf, (s,)).astype(BF16), wgu_b[...], preferred_element_type=F32)
        gate_h, up_h = gu[:, :D_EXPERT], gu[:, D_EXPERT:]
        mid = gate_h * jax.nn.sigmoid(gate_h) * up_h
        obuf[s] = jnp.dot(mid.astype(BF16), wd_b[...], preferred_element_type=F32)

    @pl.when(jnp.logical_and(i < n_act, jnp.logical_or(i == 0, be_ref[i] != be_ref[jnp.maximum(i - 1, 0)])))
    def _():
        wgu_b[...] = wgu_ref[0].astype(BF16)
        wd_b[...] = wd_ref[0].astype(BF16)

    @pl.when(i == 0)
    def _():
        gather(0, 0)
        block(0, first=True)

    for s in range(2):
        mine = i % 2 == s

        @pl.when(jnp.logical_and(mine, jnp.logical_and(i >= 1, i < n_act)))
        def _():
            @pl.when(i >= 2)
            def _():
                drain_scatter(s)
            block(s, first=False)

        @pl.when(jnp.logical_and(mine, i == n_act - 1))
        def _():
            drain_gather(1 - s)

            @pl.when(i >= 1)
            def _():
                drain_scatter(1 - s)
            scatter(i, s)
            drain_scatter(s)
            obuf[s] = jnp.zeros_like(obuf[s])
            for half in range(2):
                spare = eo_hbm.at[pl.ds(eo_hbm.shape[0] - (2 - half) * ROW_BLOCK, ROW_BLOCK)]
                pltpu.make_async_copy(obuf.at[s], spare, ssem.at[s]).start()
            for half in range(2):
                drain_scatter(s)


def _experts(h2, info, blk_expert, n_active, w_gate_up, w_down):
    n_tok = h2.shape[0]
    n_blk = blk_expert.shape[0]
    grid_spec = pltpu.PrefetchScalarGridSpec(
        num_scalar_prefetch=3,
        grid=(n_blk,),
        in_specs=[pl.BlockSpec(memory_space=pl.ANY),
                  pl.BlockSpec((1, D_MODEL, 2 * D_EXPERT), lambda i, be, na, info: (be[i], 0, 0)),
                  pl.BlockSpec((1, D_EXPERT, D_MODEL), lambda i, be, na, info: (be[i], 0, 0))],
        out_specs=pl.BlockSpec(memory_space=pl.ANY),
        scratch_shapes=[pltpu.VMEM((2, ROW_BLOCK) + ROW_TILE, F32), pltpu.VMEM((2, ROW_BLOCK, D_MODEL), F32),
                        pltpu.VMEM((D_MODEL, 2 * D_EXPERT), BF16), pltpu.VMEM((D_EXPERT, D_MODEL), BF16),
                        pltpu.SemaphoreType.DMA((2,)), pltpu.SemaphoreType.DMA((2,))])
    return pl.pallas_call(
        _expert_kernel,
        grid_spec=grid_spec,
        out_shape=jax.ShapeDtypeStruct((n_tok * TOP_K + 2 * ROW_BLOCK, D_MODEL), F32),
        compiler_params=pltpu.CompilerParams(dimension_semantics=("arbitrary",),
                                             vmem_limit_bytes=VMEM_LIMIT),
        name="experts",
    )(blk_expert, n_active, info, h2, w_gate_up, w_down)


def _ple_kernel(x1_ref, g_ref, p_ref, *rest, last_layer):
    eo_refs, (wpp_ref, gpn_ref, ggi_ref, wpg_ref, gfin_ref, y_ref) = rest[:TOP_K], rest[TOP_K:]
    tm = x1_ref.shape[0]
    chunks = [slice(c * TM_PROJ, (c + 1) * TM_PROJ) for c in range(tm // TM_PROJ)]
    x2 = []
    for c in chunks:
        moe = g_ref[c, 0:1] * eo_refs[0][c, :]
        for k in range(1, TOP_K):
            moe = moe + g_ref[c, k:k + 1] * eo_refs[k][c, :]
        x2.append(x1_ref[c, :] + moe)
    e = [_rms(_bdot(p_ref[c, :], wpp_ref[...]), gpn_ref[...]) for c in chunks]
    gate = [jax.nn.sigmoid(_bdot(_rms(v, ggi_ref[...]), wpg_ref[...])) for v in x2]
    for c, v, g, ee in zip(chunks, x2, gate, e):
        x3 = v + g * ee
        y_ref[c, :] = _rms(x3, gfin_ref[...]) if last_layer else x3


def _ple(x1, gates, expert_out, p2d, w_pp, g_pn, g_gi, w_pg, g_final, last_layer):
    T = x1.shape[0]
    tm = MIX_CHUNKS * TM_PROJ
    nt = T // tm
    row = lambda i: (i, 0)
    fixed = lambda i: (0, 0)
    vec = lambda a: a.reshape(1, D_MODEL)
    slot_rows = [pl.BlockSpec((tm, D_MODEL), functools.partial(lambda i, k: (i + k * nt, 0), k=k))
                 for k in range(TOP_K)]
    return pl.pallas_call(
        functools.partial(_ple_kernel, last_layer=last_layer),
        grid=(nt,),
        in_specs=[pl.BlockSpec((tm, D_MODEL), row), pl.BlockSpec((tm, TOP_K), row),
                  pl.BlockSpec((tm, PLE_DIM), row)] + slot_rows + [
                  pl.BlockSpec((PLE_DIM, D_MODEL), fixed), pl.BlockSpec((1, D_MODEL), fixed),
                  pl.BlockSpec((1, D_MODEL), fixed), pl.BlockSpec((D_MODEL, D_MODEL), fixed),
                  pl.BlockSpec((1, D_MODEL), fixed)],
        out_specs=pl.BlockSpec((tm, D_MODEL), row),
        out_shape=jax.ShapeDtypeStruct((T, D_MODEL), F32),
        compiler_params=pltpu.CompilerParams(dimension_semantics=("parallel",),
                                             vmem_limit_bytes=VMEM_LIMIT),
        name="ple",
    )(x1, gates, p2d, *([expert_out] * TOP_K), w_pp.astype(BF16), vec(g_pn), vec(g_gi),
      w_pg.astype(BF16), vec(g_final))


def kernel(x, p, g_mix, w_in, mu_shift, rw_w0, rw_w2, rw_a0, rw_a2, rw_g2, rw_k_k, rw_k_a, rw_r_k, rw_ln_w, rw_ln_b, cmp_pos_k, cmp_pos_v, cmp_k_w1, cmp_k_w2, cmp_v_w1, cmp_v_w2, nsa_gate_b, w_up_rwkv, w_up_nsa, w_out, g_ffn, w_group, b_group, w_router, b_router, w_exp_gate_up, w_exp_down, w_ple_proj, g_ple_norm, g_ple_gate_in, w_ple_gate, g_final):
    B, S, D = x.shape
    T = B * S
    depth = p.shape[0]
    xc = x.reshape(T, D)
    for i in range(depth):
        zrw, zq, zkv, zmg, zgate = _proj(xc, g_mix[i], w_in[i], mu_shift[i], S)
        y_rw = _rwkv(zrw.reshape(B, S, RW_IN), rw_w0[i], rw_w2[i], rw_a0[i], rw_a2[i], rw_g2[i],
                     rw_k_k[i], rw_k_a[i], rw_r_k[i], rw_ln_w[i], rw_ln_b[i])
        zkv3 = zkv.reshape(B, S, KV_IN)
        nsa_kv = _nsa_prep(zkv3, cmp_pos_k[i], cmp_pos_v[i], cmp_k_w1[i], cmp_k_w2[i],
                           cmp_v_w1[i], cmp_v_w2[i])
        y_nsa = _nsa_attn(zq.reshape(B, S, NSA_WIDTH), *nsa_kv, zgate.reshape(B, S, LANE),
                          nsa_gate_b[i])
        x1, h2, rt, cnt = _mix(xc, y_rw.reshape(T, RW_WIDTH), y_nsa.reshape(T, NSA_WIDTH), zmg,
                               w_up_rwkv[i], w_up_nsa[i], w_out[i], g_ffn[i], w_group[i], b_group[i],
                               w_router[i], b_router[i])
        dest, gates, blk_expert, n_active = _route_tables(rt, cnt, T)
        expert_out = _experts(h2, _row_info(dest, T), blk_expert, n_active, w_exp_gate_up[i],
                              w_exp_down[i])
        xc = _ple(x1, gates, expert_out, p[i].reshape(T, PLE_DIM), w_ple_proj[i], g_ple_norm[i],
                  g_ple_gate_in[i], w_ple_gate[i], g_final, i == depth - 1)
    return xc.reshape(B, S, D)
```

```python
import functools
import math

import jax
import jax.numpy as jnp
import numpy as np
from jax import lax
from jax.experimental import pallas as pl
from jax.experimental.pallas import tpu as pltpu

F32 = jnp.float32
BF16 = jnp.bfloat16
HI = lax.Precision.HIGHEST

D_MODEL = 1024
RW_HEADS = 8
RW_HEAD_DIM = 64
RW_WIDTH = 512
DECAY_LORA = 64
AAA_LORA = 64
GATE_LORA = 128
GN_EPS = 64e-5
RW_IN = 3 * RW_WIDTH + DECAY_LORA + AAA_LORA + GATE_LORA

NSA_HEADS = 8
NSA_KV_HEADS = 2
NSA_REP = NSA_HEADS // NSA_KV_HEADS
NSA_HEAD_DIM = 64
NSA_WIDTH = 512
NSA_KV_WIDTH = 128
CMP_BLOCK = 32
CMP_STRIDE = 16
SEL_BLOCK = 64
N_SELECT = 8
WINDOW = 512
N_NSA_BRANCH = 3
FORCE_SCORE = 1e6
NEG_INF = -1e30

N_GROUPS = 4
EXPERTS_PER_GROUP = 8
N_EXPERTS = 32
TOP_K = 2
D_EXPERT = 512
ROW_BLOCK = 256
PLE_DIM = 256
NORM_EPS = 1e-6

N_GATE = N_NSA_BRANCH * NSA_HEADS
ATT_IN = NSA_WIDTH + 6 * NSA_KV_WIDTH
KV_OFF = RW_IN + NSA_WIDTH
KV_IN = 6 * NSA_KV_WIDTH
GATE_OFF = RW_IN + ATT_IN
MERGE_OFF = GATE_OFF + N_GATE
LANE = 128

RW_CHUNK = 64
RW_SUB = 16
RW_ROWS = 4
TQ = 256
TK = 256
V_ROWS = NSA_HEAD_DIM + 16
TM_PROJ = 256
MIX_CHUNKS = 2
VMEM_LIMIT = 56 * 1024 * 1024


def _bdot(a, b):
    return jnp.dot(a.astype(BF16), b.astype(BF16), preferred_element_type=F32)


def _bdot_nt(a, b):
    return lax.dot_general(a.astype(BF16), b.astype(BF16), (((1,), (1,)), ((), ())),
                           preferred_element_type=F32)


def _rms(x, g):
    return x * lax.rsqrt(jnp.mean(x * x, axis=-1, keepdims=True) + NORM_EPS) * g


def _proj_kernel(x_ref, g_ref, w_ref, mu_ref, zrw_ref, zq_ref, zkv_ref, zmg_ref, zgate_ref,
                 carry_ref, *, tiles_per_seq):
    i = pl.program_id(0)
    tm = x_ref.shape[0]

    @pl.when(i % tiles_per_seq == 0)
    def _():
        carry_ref[...] = jnp.zeros_like(carry_ref)

    h = _rms(x_ref[...], g_ref[...]).astype(BF16)
    z = jnp.dot(h, w_ref[:, 0:RW_IN], preferred_element_type=F32)
    row = lax.broadcasted_iota(jnp.int32, (tm, 1), 0)
    prev = jnp.where(row == 0, carry_ref[7:8, :], pltpu.roll(z, 1, 0))
    carry_ref[...] = z[tm - 8:tm, :]
    zrw_ref[...] = z + (prev - z) * mu_ref[...]
    zq_ref[...] = jnp.dot(h, w_ref[:, RW_IN:KV_OFF], preferred_element_type=F32)
    zkv_ref[...] = jnp.dot(h, w_ref[:, KV_OFF:GATE_OFF], preferred_element_type=F32)
    zmg_ref[...] = jnp.dot(h, w_ref[:, GATE_OFF:GATE_OFF + 2 * D_MODEL], preferred_element_type=F32)
    zgate_ref[...] = jnp.dot(h, w_ref[:, GATE_OFF + 2 * D_MODEL:], preferred_element_type=F32)


def _proj(x2d, g_mix, w_in, mu, seq):
    T = x2d.shape[0]
    tm = TM_PROJ
    wp = jnp.concatenate(
        [w_in[:, :GATE_OFF], w_in[:, MERGE_OFF:],
         jnp.pad(w_in[:, GATE_OFF:MERGE_OFF], ((0, 0), (0, LANE - N_GATE)))], axis=1).astype(BF16)
    npad = wp.shape[1]
    row = lambda i: (i, 0)
    fixed = lambda i: (0, 0)
    return pl.pallas_call(
        functools.partial(_proj_kernel, tiles_per_seq=seq // tm),
        grid=(T // tm,),
        in_specs=[pl.BlockSpec((tm, D_MODEL), row), pl.BlockSpec((1, D_MODEL), fixed),
                  pl.BlockSpec((D_MODEL, npad), fixed), pl.BlockSpec((1, RW_IN), fixed)],
        out_specs=[pl.BlockSpec((tm, RW_IN), row), pl.BlockSpec((tm, NSA_WIDTH), row),
                   pl.BlockSpec((tm, KV_IN), row), pl.BlockSpec((tm, 2 * D_MODEL), row),
                   pl.BlockSpec((tm, LANE), row)],
        out_shape=[jax.ShapeDtypeStruct((T, RW_IN), F32), jax.ShapeDtypeStruct((T, NSA_WIDTH), F32),
                   jax.ShapeDtypeStruct((T, KV_IN), F32), jax.ShapeDtypeStruct((T, 2 * D_MODEL), F32),
                   jax.ShapeDtypeStruct((T, LANE), F32)],
        scratch_shapes=[pltpu.VMEM((8, RW_IN), F32)],
        compiler_params=pltpu.CompilerParams(dimension_semantics=("arbitrary",),
                                             vmem_limit_bytes=VMEM_LIMIT),
        name="proj",
    )(x2d, g_mix.reshape(1, D_MODEL), wp, mu.reshape(1, RW_IN))


PAIR = 2 * RW_HEAD_DIM


def _pair_blocks(x):
    low = lax.broadcasted_iota(jnp.int32, (1, PAIR), 1) < RW_HEAD_DIM
    return jnp.concatenate([jnp.where(low, x, 0.0), jnp.where(low, 0.0, x)], axis=0)


def _pmm(a, b):
    return _bdot(a, _pair_blocks(b))


def _unit_lower_inverse(a_strict, sub_mask, eye):
    ad = [jnp.where(sub_mask, a, 0.0) for a in a_strict]
    ao = [a - d for a, d in zip(a_strict, ad)]
    td = [eye - d for d in ad]
    pw = ad
    for _ in range(int(math.log2(RW_SUB)) - 1):
        pw = [_pmm(x, x) for x in pw]
        td = [_pmm(t, eye + x) for t, x in zip(td, pw)]
    n = [_pmm(t, o) for t, o in zip(td, ao)]
    t = [eye - x for x in n]
    pw = n
    for _ in range(int(math.log2(RW_CHUNK // RW_SUB)) - 1):
        pw = [_pmm(x, x) for x in pw]
        t = [_pmm(a, eye + x) for a, x in zip(t, pw)]
    return [_pmm(a, d) for a, d in zip(t, td)]


def _rwkv_kernel(z_ref, w0_ref, w2_ref, a0_ref, a2_ref, g2_ref, kk_ref, ka_ref, rk_ref, lnw_ref,
                 lnb_ref, avg_ref, o_ref, h_ref):
    c = pl.program_id(1)
    C = RW_CHUNK
    n_pair = RW_WIDTH // PAIR
    nt = (((1,), (1,)), ((), ()))

    @pl.when(c == 0)
    def _():
        h_ref[...] = jnp.zeros_like(h_ref)

    ti = lax.broadcasted_iota(jnp.int32, (C, 1), 0)
    si = lax.broadcasted_iota(jnp.int32, (1, PAIR), 1) & (RW_HEAD_DIM - 1)
    incl, strict = ti >= si, ti > si
    eye = (ti == si).astype(F32)
    sub_shift = int(math.log2(RW_SUB))
    sub_mask = (ti >> sub_shift) == (si >> sub_shift)
    row2 = lax.broadcasted_iota(jnp.int32, (PAIR, 1), 0)
    col2 = lax.broadcasted_iota(jnp.int32, (1, PAIR), 1)
    same_head = (row2 < RW_HEAD_DIM) == (col2 < RW_HEAD_DIM)
    eye2 = row2 == col2
    tri = (lax.broadcasted_iota(jnp.int32, (C, C), 0)
           >= lax.broadcasted_iota(jnp.int32, (C, C), 1)).astype(BF16)

    def head_mean(x):
        xs = jnp.concatenate([x[:, p * PAIR:(p + 1) * PAIR] for p in range(n_pair)], axis=0)
        ms = _bdot(xs, avg_ref[...])
        return jnp.concatenate([ms[p * C:(p + 1) * C] for p in range(n_pair)], axis=1)

    n_rows = z_ref.shape[0]
    rows = []
    for n in range(n_rows):
        z = z_ref[n]
        zr, zk, zv = z[:, 0:512], z[:, 512:1024], z[:, 1024:1536]
        zw, za, zg = z[:, 1536:1600], z[:, 1600:1664], z[:, 1664:1792]
        w_raw = w0_ref[...] + _bdot(jnp.tanh(zw), w2_ref[...])
        logw = -jax.nn.sigmoid(w_raw) * math.exp(-0.5)
        a = jax.nn.sigmoid(a0_ref[...] + _bdot(za, a2_ref[...]))
        gate = _bdot(jax.nn.sigmoid(zg), g2_ref[...])
        kk = zk * kk_ref[...]
        kk = kk / jnp.maximum(jnp.sqrt(head_mean(kk * kk) * RW_HEAD_DIM), 1e-12)
        k = zk * (1.0 + (a - 1.0) * ka_ref[...])
        b = kk * a

        w_hi = logw.astype(BF16)
        w_lo = (logw - w_hi.astype(F32)).astype(BF16)
        cum = (jnp.dot(tri, w_hi, preferred_element_type=F32)
               + jnp.dot(tri, w_lo, preferred_element_type=F32))
        cum_last = cum[C - 1:C, :]
        g_inv = jnp.exp(-cum)
        g_end = jnp.exp(cum_last - cum)
        rows.append(dict(rt=zr * jnp.exp(cum), kt=k * g_inv, bt=b * g_inv, qt=kk * jnp.exp(cum - logw),
                         kh=k * g_end, bh=b * g_end, v=zv, g_last=jnp.exp(cum_last), gate=gate,
                         bonus=head_mean(zr * k * rk_ref[...]) * RW_HEAD_DIM * zv))

    chains = [(n, slice(p * PAIR, (p + 1) * PAIR)) for n in range(n_rows) for p in range(n_pair)]
    part = lambda name: [rows[n][name][:, sl] for n, sl in chains]
    qt, rt, kt, bt, kh, bh, v = (part(x) for x in ("qt", "rt", "kt", "bt", "kh", "bh", "v"))
    lhs = [jnp.concatenate([q, r], axis=0).astype(BF16) for q, r in zip(qt, rt)]
    ab = [lax.dot_general(l, _pair_blocks(x).astype(BF16), nt, preferred_element_type=F32)
          for l, x in zip(lhs, bt)]
    ak = [lax.dot_general(l, _pair_blocks(x).astype(BF16), nt, preferred_element_type=F32)
          for l, x in zip(lhs, kt)]
    a_kb = [jnp.where(strict, x[0:C], 0.0) for x in ab]
    a_rb = [jnp.where(incl, x[C:2 * C], 0.0) for x in ab]
    a_kk = [jnp.where(strict, x[0:C], 0.0) for x in ak]
    a_rk = [jnp.where(incl, x[C:2 * C], 0.0) for x in ak]
    t_inv = _unit_lower_inverse(a_kb, sub_mask, eye)

    h = [h_ref[n, sl.start // PAIR] for n, sl in chains]
    vb = [_pair_blocks(x) for x in v]
    rhs = [_bdot(jnp.concatenate([q, akk], axis=1), jnp.concatenate([hh, vv], axis=0))
           for q, akk, hh, vv in zip(qt, a_kk, h, vb)]
    u = [_pmm(t, x) for t, x in zip(t_inv, rhs)]
    outs = [_bdot(jnp.concatenate([r, ark, -arb], axis=1), jnp.concatenate([hh, vv, _pair_blocks(uu)], axis=0))
            for r, ark, arb, hh, vv, uu in zip(rt, a_rk, a_rb, h, vb, u)]
    upd = [_bdot(jnp.concatenate([x, -y], axis=0).T, jnp.concatenate([vv, uu], axis=0))
           for x, y, vv, uu in zip(kh, bh, v, u)]
    for (n, sl), hh, dd in zip(chains, h, upd):
        decay_col = jnp.sum(jnp.where(eye2, rows[n]["g_last"][:, sl], 0.0), axis=1, keepdims=True)
        h_ref[n, sl.start // PAIR] = decay_col * hh + jnp.where(same_head, dd, 0.0)

    for n in range(n_rows):
        o = jnp.concatenate(outs[n * n_pair:(n + 1) * n_pair], axis=1)
        d = o - head_mean(o)
        on = d * lax.rsqrt(head_mean(d * d) + GN_EPS)
        o_ref[n] = (on * lnw_ref[...] + lnb_ref[...] + rows[n]["bonus"]) * rows[n]["gate"]


def _rwkv(zrw, w0, w2, a0, a2, g2, k_k, k_a, r_k, ln_w, ln_b):
    B, S, _ = zrw.shape
    C = RW_CHUNK
    nb = RW_ROWS
    hid = np.arange(PAIR) // RW_HEAD_DIM
    avg = jnp.asarray((hid[:, None] == hid[None, :]).astype(np.float32) / RW_HEAD_DIM)
    vec = lambda a: a.reshape(1, RW_WIDTH)
    fixed = lambda shape: pl.BlockSpec(shape, lambda b, c: (0,) * len(shape))
    return pl.pallas_call(
        _rwkv_kernel,
        grid=(B // nb, S // C),
        in_specs=[pl.BlockSpec((nb, C, RW_IN), lambda b, c: (b, c, 0)),
                  fixed((1, RW_WIDTH)), fixed((DECAY_LORA, RW_WIDTH)),
                  fixed((1, RW_WIDTH)), fixed((AAA_LORA, RW_WIDTH)),
                  fixed((GATE_LORA, RW_WIDTH)), fixed((1, RW_WIDTH)), fixed((1, RW_WIDTH)),
                  fixed((1, RW_WIDTH)), fixed((1, RW_WIDTH)), fixed((1, RW_WIDTH)),
                  fixed((PAIR, PAIR))],
        out_specs=pl.BlockSpec((nb, C, RW_WIDTH), lambda b, c: (b, c, 0)),
        out_shape=jax.ShapeDtypeStruct((B, S, RW_WIDTH), F32),
        scratch_shapes=[pltpu.VMEM((nb, RW_WIDTH // PAIR, PAIR, PAIR), F32)],
        compiler_params=pltpu.CompilerParams(dimension_semantics=("parallel", "arbitrary"),
                                             vmem_limit_bytes=VMEM_LIMIT),
        name="rwkv",
    )(zrw, vec(w0), w2, vec(a0), a2, g2, vec(k_k), vec(k_a), vec(r_k), vec(ln_w), vec(ln_b), avg)


def _gelu_tanh(x):
    return 0.5 * x * (1.0 + jnp.tanh(math.sqrt(2.0 / math.pi) * (x + 0.044715 * (x * x * x))))


def _key_features(pos_hi, pos_lo, block, n, n_sel):
    lane = lax.broadcasted_iota(jnp.int32, (n, NSA_HEAD_DIM), 1)
    feat = jnp.where(lane == n_sel, pos_hi, jnp.where(lane == n_sel + 1, pos_lo, 0.0))
    return feat if block is None else jnp.where(lane == block, 1.0, feat)


def _nsa_prep_kernel(zkc_ref, zvc_ref, zks_ref, zvs_ref, zkw_ref, zvw_ref, pk_ref, pv_ref, kw1_ref,
                     kw2_ref, vw1_ref, vw2_ref, kc_ref, vct_ref, ksa_ref, kwa_ref, vst_ref, vwt_ref):
    S = zkc_ref.shape[1]
    n_grp = S // CMP_STRIDE
    Dh = NSA_HEAD_DIM
    half = CMP_BLOCK // 2
    n_sel = S // SEL_BLOCK
    jrow = lax.broadcasted_iota(jnp.int32, (n_grp, 1), 0)
    cmp_feat = _key_features((jrow >> 3).astype(F32),
                             ((jrow & 7) * CMP_STRIDE).astype(F32) + 0.5 * (CMP_BLOCK - 1),
                             None, n_grp, n_sel)
    for is_v, (z_ref, pos_ref, w1_ref, w2_ref) in enumerate(((zkc_ref, pk_ref, kw1_ref, kw2_ref),
                                                            (zvc_ref, pv_ref, vw1_ref, vw2_ref))):
        for g in range(NSA_KV_HEADS):
            lo = jnp.zeros((n_grp, Dh), F32)
            hi = jnp.zeros((n_grp, Dh), F32)
            for l in range(half):
                xs = z_ref[0, pl.ds(l, n_grp, stride=CMP_STRIDE), :]
                xg = xs[:, g * Dh:(g + 1) * Dh]
                lo = lo + _bdot(xg + pos_ref[l:l + 1, :], w1_ref[l * Dh:(l + 1) * Dh, :])
                hi = hi + _bdot(xg + pos_ref[half + l:half + l + 1, :],
                                w1_ref[(half + l) * Dh:(half + l + 1) * Dh, :])
            pre = lo + pltpu.roll(hi, n_grp - 1, 0)
            out = jnp.where(jrow < n_grp - 1, _bdot(_gelu_tanh(pre), w2_ref[...]), 0.0)
            if is_v:
                out_t = jnp.concatenate([out, jnp.zeros_like(out)], axis=1).T
                vct_ref[0, g] = out_t[0:Dh, :].astype(BF16)
            else:
                kc_ref[0, g] = jnp.concatenate([out, cmp_feat], axis=1)

    prow = lax.broadcasted_iota(jnp.int32, (S, 1), 0)
    p_hi, p_lo = (prow >> 7).astype(F32), (prow & (LANE - 1)).astype(F32)
    for z_ref, out_ref, block in ((zks_ref, ksa_ref, prow >> int(math.log2(SEL_BLOCK))),
                                  (zkw_ref, kwa_ref, None)):
        kfull = z_ref[0]
        key_feat = _key_features(p_hi, p_lo, block, S, n_sel)
        for g in range(NSA_KV_HEADS):
            out_ref[0, g] = jnp.concatenate([kfull[:, g * Dh:(g + 1) * Dh], key_feat], axis=1).astype(BF16)
    ones_row = (lax.broadcasted_iota(jnp.int32, (V_ROWS - Dh, TK), 0) == 0).astype(F32)
    for z_ref, out_ref in ((zvs_ref, vst_ref), (zvw_ref, vwt_ref)):
        for j in range(S // TK):
            vt = z_ref[0, j * TK:(j + 1) * TK, :].T
            out_ref[0, j] = jnp.concatenate(
                [piece for g in range(NSA_KV_HEADS) for piece in (vt[g * Dh:(g + 1) * Dh], ones_row)],
                axis=0).astype(BF16)


def _nsa_prep(zkv, pos_k, pos_v, kw1, kw2, vw1, vw2):
    B, S, _ = zkv.shape
    n_grp = S // CMP_STRIDE
    n_kt = S // TK
    Dh = NSA_HEAD_DIM
    G = NSA_KV_HEADS
    fixed = lambda shape: pl.BlockSpec(shape, lambda b: (0,) * len(shape))
    col = lambda c: pl.BlockSpec((1, S, NSA_KV_WIDTH), lambda b: (b, 0, c))
    whole = lambda shape: pl.BlockSpec((1,) + shape, lambda b: (b,) + (0,) * len(shape))
    shapes = [((G, n_grp, 2 * Dh), F32), ((G, Dh, n_grp), BF16), ((G, S, 2 * Dh), BF16),
              ((G, S, 2 * Dh), BF16), ((n_kt, G * V_ROWS, TK), BF16), ((n_kt, G * V_ROWS, TK), BF16)]
    return pl.pallas_call(
        _nsa_prep_kernel,
        grid=(B,),
        in_specs=[col(c) for c in range(6)] + [
            fixed((CMP_BLOCK, Dh)), fixed((CMP_BLOCK, Dh)),
            fixed((CMP_BLOCK * Dh, Dh)), fixed((Dh, Dh)),
            fixed((CMP_BLOCK * Dh, Dh)), fixed((Dh, Dh))],
        out_specs=[whole(s) for s, _ in shapes],
        out_shape=[jax.ShapeDtypeStruct((B,) + s, d) for s, d in shapes],
        compiler_params=pltpu.CompilerParams(dimension_semantics=("parallel",),
                                             vmem_limit_bytes=VMEM_LIMIT),
        name="nsa_prep",
    )(zkv, zkv, zkv, zkv, zkv, zkv, pos_k, pos_v, kw1, kw2, vw1, vw2)


def _nsa_attn_kernel(q_ref, kc_ref, vct_ref, ksa_ref, kwa_ref, vst_ref, vwt_ref, gl_ref, gb_ref,
                     ovlt_ref, slope_ref, o_ref, acc_ref, ot_ref):
    i = pl.program_id(1)
    Dh = NSA_HEAD_DIM
    R = NSA_REP
    N = R * TQ
    n_cmp_pad = kc_ref.shape[2]
    n_sel = ovlt_ref.shape[0]
    G = NSA_KV_HEADS
    nt = (((1,), (1,)), ((), ()))
    log2e = math.log2(math.e)
    t0 = i * TQ
    t_row = t0 + lax.broadcasted_iota(jnp.int32, (1, TQ), 1)
    c_col = lax.broadcasted_iota(jnp.int32, (TK, 1), 0)
    sgate_t = jax.nn.sigmoid(gl_ref[0] + gb_ref[...]).T
    lane_f = lax.broadcasted_iota(jnp.int32, (1, Dh), 1)
    heads = lambda x: jnp.concatenate([x] * R, axis=1)

    def key_dist(j):
        return t_row - (j * TK + c_col)

    def queries(g, sel_feat):
        parts = []
        for r in range(R):
            h = g * R + r
            sl = slope_ref[:, h:h + 1] * log2e
            feat = jnp.where(lane_f == n_sel, sl * LANE, jnp.where(lane_f == n_sel + 1, sl, sel_feat))
            parts.append(jnp.concatenate(
                [q_ref[0, :, h * Dh:(h + 1) * Dh] * (Dh ** -0.5 * log2e), jnp.broadcast_to(feat, (TQ, Dh))],
                axis=1))
        return jnp.concatenate(parts, axis=0)

    qab, o_cmp = [], []
    for g in range(G):
        jc = lax.broadcasted_iota(jnp.int32, (n_cmp_pad, 1), 0)
        ok_c = (jc * CMP_STRIDE + (CMP_BLOCK - 1) <= t_row) & (jc < n_cmp_pad - 1)
        s_c = (lax.dot_general(kc_ref[0, g], queries(g, 0.0), nt, precision=HI, preferred_element_type=F32)
               + heads(jnp.where(ok_c, 0.0, NEG_INF)))
        e_c = jnp.exp2(s_c - jnp.max(s_c, axis=0, keepdims=True))
        any_c = heads(t_row >= CMP_BLOCK - 1)
        p_c = e_c * jnp.where(any_c, 1.0 / jnp.sum(e_c, axis=0, keepdims=True), 0.0)
        o_cmp.append(jnp.dot(vct_ref[0, g], p_c.astype(BF16), preferred_element_type=F32))

        p_sum = p_c[:, 0:TQ]
        for r in range(1, R):
            p_sum = p_sum + p_c[:, r * TQ:(r + 1) * TQ]
        imp = jnp.dot(ovlt_ref[...], p_sum, precision=HI, preferred_element_type=F32)
        kb = lax.broadcasted_iota(jnp.int32, (n_sel, 1), 0)
        kbf = kb.astype(F32)
        blk_t = t_row >> int(math.log2(SEL_BLOCK))
        forced = (kb == 0) | (kb == blk_t) | (kb == blk_t - 1)
        cur = jnp.where(forced, FORCE_SCORE, jnp.where(kb <= blk_t, imp, -FORCE_SCORE))
        sel_bias = jnp.full((n_sel, TQ), NEG_INF, F32)
        for _ in range(min(N_SELECT, n_sel)):
            mx = jnp.max(cur, axis=0, keepdims=True)
            first = jnp.min(jnp.where(cur == mx, kbf, float(n_sel)), axis=0, keepdims=True)
            hit = kbf == first
            sel_bias = jnp.where(hit, 0.0, sel_bias)
            cur = jnp.where(hit, -3e38, cur)
        sel_feat = jnp.concatenate([sel_bias, jnp.zeros((LANE - n_sel, TQ), F32)], axis=0).T[:, 0:Dh]
        qab.append(queries(g, sel_feat).astype(BF16))

    VR = vst_ref.shape[2] // G

    def tile(j, m, k_ref, vt_ref, bias, slot):
        ks = [k_ref[0, g, pl.ds(pl.multiple_of(j * TK, TK), TK), :] for g in range(G)]
        vts = [vt_ref[0, j, g * VR:(g + 1) * VR, :] for g in range(G)]
        hs = range(NSA_HEADS)
        s = [lax.dot_general(ks[h // R], qab[h // R][(h % R) * TQ:(h % R + 1) * TQ], nt,
                             preferred_element_type=F32) for h in hs]
        if bias is not None:
            s = [x + bias for x in s]
        m_new = [jnp.maximum(m[h], jnp.max(s[h], axis=0, keepdims=True)) for h in hs]
        alpha = [jnp.exp2(m[h] - m_new[h]) for h in hs]
        p = [jnp.exp2(s[h] - m_new[h]).astype(BF16) for h in hs]
        pv = [jnp.dot(vts[h // R], p[h], preferred_element_type=F32) for h in hs]
        for h in hs:
            acc_ref[slot, h] = alpha[h] * acc_ref[slot, h] + pv[h]
        return tuple(m_new)

    def window_bias(j):
        d = key_dist(j)
        return jnp.where((d >= 0) & (d < WINDOW), 0.0, NEG_INF)

    init = (jnp.full((1, TQ), NEG_INF, F32),) * NSA_HEADS
    acc_ref[...] = jnp.zeros_like(acc_ref)
    causal = jnp.where(key_dist(i) >= 0, 0.0, NEG_INF)
    carry = lax.fori_loop(0, i, lambda j, c: tile(j, c, ksa_ref, vst_ref, None, 0), init)
    tile(i, carry, ksa_ref, vst_ref, causal, 0)
    carry = lax.fori_loop(jnp.maximum(i - WINDOW // TK, 0), i,
                          lambda j, c: tile(j, c, kwa_ref, vwt_ref, window_bias(j), 1), init)
    tile(i, carry, kwa_ref, vwt_ref, causal, 1)

    for h in range(NSA_HEADS):
        g, r = divmod(h, R)
        acc_s, acc_w = acc_ref[0, h], acc_ref[1, h]
        ot_ref[h * Dh:(h + 1) * Dh, :] = (
            sgate_t[3 * h:3 * h + 1, :] * o_cmp[g][:, r * TQ:(r + 1) * TQ]
            + sgate_t[3 * h + 1:3 * h + 2, :] * (acc_s[0:Dh] * (1.0 / acc_s[Dh:Dh + 1]))
            + sgate_t[3 * h + 2:3 * h + 3, :] * (acc_w[0:Dh] * (1.0 / acc_w[Dh:Dh + 1])))
    o_ref[0] = ot_ref[...].T


def _nsa_attn(zq, kc, vct, ksa, kwa, vst, vwt, zgate, gate_b):
    B, S, _ = zq.shape
    n_sel = S // SEL_BLOCK
    n_cmp = (S - CMP_BLOCK) // CMP_STRIDE + 1
    n_cmp_pad = kc.shape[2]
    n_kt = S // TK
    G, Dh = NSA_KV_HEADS, NSA_HEAD_DIM
    cmp_start = np.arange(n_cmp) * CMP_STRIDE
    sel_start = np.arange(n_sel) * SEL_BLOCK
    overlap = np.clip(np.minimum(cmp_start[:, None] + CMP_BLOCK, sel_start[None, :] + SEL_BLOCK)
                      - np.maximum(cmp_start[:, None], sel_start[None, :]), 0, None) / CMP_BLOCK
    ovlt = np.zeros((n_sel, n_cmp_pad), np.float32)
    ovlt[:, :n_cmp] = overlap.T
    slopes = (2.0 ** (-8.0 * np.arange(1, NSA_HEADS + 1) / NSA_HEADS)).astype(np.float32).reshape(1, NSA_HEADS)
    gb = jnp.pad(gate_b, (0, LANE - N_GATE)).reshape(1, LANE)
    fixed = lambda shape: pl.BlockSpec(shape, lambda b, i: (0,) * len(shape))
    per_b = lambda shape: pl.BlockSpec((1,) + shape, lambda b, i: (b,) + (0,) * len(shape))
    return pl.pallas_call(
        _nsa_attn_kernel,
        grid=(B, S // TQ),
        in_specs=[pl.BlockSpec((1, TQ, NSA_WIDTH), lambda b, i: (b, i, 0)),
                  per_b((G, n_cmp_pad, 2 * Dh)), per_b((G, Dh, n_cmp_pad)),
                  per_b((G, S, 2 * Dh)), per_b((G, S, 2 * Dh)),
                  per_b((n_kt, G * V_ROWS, TK)), per_b((n_kt, G * V_ROWS, TK)),
                  pl.BlockSpec((1, TQ, LANE), lambda b, i: (b, i, 0)),
                  fixed((1, LANE)), fixed((n_sel, n_cmp_pad)), fixed((1, NSA_HEADS))],
        out_specs=pl.BlockSpec((1, TQ, NSA_WIDTH), lambda b, i: (b, i, 0)),
        out_shape=jax.ShapeDtypeStruct((B, S, NSA_WIDTH), F32),
        scratch_shapes=[pltpu.VMEM((2, NSA_HEADS, V_ROWS, TQ), F32),
                        pltpu.VMEM((NSA_WIDTH, TQ), F32)],
        compiler_params=pltpu.CompilerParams(dimension_semantics=("parallel", "arbitrary"),
                                             vmem_limit_bytes=VMEM_LIMIT),
        name="nsa_attn",
    )(zq, kc, vct, ksa, kwa, vst, vwt, zgate, gb, jnp.asarray(ovlt), jnp.asarray(slopes))


def _mix_kernel(x_ref, yr_ref, yn_ref, zmg_ref, ur_ref, un_ref, wo_ref, gf_ref, wr_ref, br_ref,
                x1_ref, h2_ref, rt_ref, cnt_ref):
    tm = x_ref.shape[0]
    nt = (((1,), (1,)), ((), ()))
    chunks = [slice(c * TM_PROJ, (c + 1) * TM_PROJ) for c in range(tm // TM_PROJ)]
    up_r = [_bdot(yr_ref[c, :], ur_ref[...]) for c in chunks]
    up_n = [_bdot(yn_ref[c, :], un_ref[...]) for c in chunks]
    mixed = [jax.nn.sigmoid(zmg_ref[c, 0:D_MODEL]) * a + jax.nn.sigmoid(zmg_ref[c, D_MODEL:2 * D_MODEL]) * b
             for c, a, b in zip(chunks, up_r, up_n)]
    x1 = [x_ref[c, :] + _bdot(m, wo_ref[...]) for c, m in zip(chunks, mixed)]
    h2 = [_rms(v, gf_ref[...]) for v in x1]
    for c, v, h in zip(chunks, x1, h2):
        x1_ref[c, :] = v
        _store_row_tiles(h2_ref.at[c], h)
    n_row = wr_ref.shape[0]
    logits = jnp.concatenate(
        [lax.dot_general(wr_ref[...], h, nt, precision=HI, preferred_element_type=F32) for h in h2],
        axis=1) + br_ref[...]
    row = lax.broadcasted_iota(jnp.int32, (n_row, 1), 0).astype(F32)
    gl = jnp.where(row < N_GROUPS, logits, NEG_INF)
    gmax = jnp.max(gl, axis=0, keepdims=True)
    g_sel = jnp.min(jnp.where(gl == gmax, row, float(n_row)), axis=0, keepdims=True)
    p_group = 1.0 / jnp.sum(jnp.exp(gl - gmax), axis=0, keepdims=True)
    e_row = row - N_GROUPS
    in_grp = ((e_row >= g_sel * EXPERTS_PER_GROUP) & (e_row < (g_sel + 1.0) * EXPERTS_PER_GROUP)
              & (e_row < N_EXPERTS))
    el = jnp.where(in_grp, logits, NEG_INF)
    m1 = jnp.max(el, axis=0, keepdims=True)
    i1 = jnp.min(jnp.where(el == m1, e_row, float(n_row)), axis=0, keepdims=True)
    el2 = jnp.where(e_row == i1, 2.0 * NEG_INF, el)
    m2 = jnp.max(el2, axis=0, keepdims=True)
    i2 = jnp.min(jnp.where(el2 == m2, e_row, float(n_row)), axis=0, keepdims=True)
    r2 = jnp.exp(m2 - m1)
    g1 = p_group / (1.0 + r2)
    g2 = p_group * r2 / (1.0 + r2)

    @pl.when(pl.program_id(0) == 0)
    def _():
        cnt_ref[...] = jnp.zeros_like(cnt_ref)

    pick1, pick2 = e_row == i1, e_row == i2
    both = pick1.astype(F32) + pick2.astype(F32)
    earlier = (lax.broadcasted_iota(jnp.int32, (tm, tm), 0)
               < lax.broadcasted_iota(jnp.int32, (tm, tm), 1)).astype(BF16)
    before = jnp.dot(both.astype(BF16), earlier, preferred_element_type=F32) + cnt_ref[:, 0:1]
    rank1 = jnp.sum(jnp.where(pick1, before, 0.0), axis=0, keepdims=True)
    rank2 = jnp.sum(jnp.where(pick2, before, 0.0), axis=0, keepdims=True)
    cnt_ref[...] = cnt_ref[...] + jnp.sum(both, axis=1, keepdims=True)
    rt_ref[...] = jnp.concatenate([i1, i2, g1, g2, rank1, rank2, jnp.zeros((2, tm), F32)], axis=0)


def _mix(x2d, y_rw, y_nsa, zmg, w_up_r, w_up_n, w_out, g_ffn, w_group, b_group, w_router, b_router):
    T = x2d.shape[0]
    tm = MIX_CHUNKS * TM_PROJ
    n_r = N_GROUPS + N_EXPERTS
    n_row = -(-n_r // 8) * 8
    wr = jnp.pad(jnp.concatenate([w_group, w_router], axis=1).T, ((0, n_row - n_r), (0, 0)))
    br = jnp.pad(jnp.concatenate([b_group, b_router]), (0, n_row - n_r)).reshape(n_row, 1)
    row = lambda i: (i, 0)
    fixed = lambda i: (0, 0)
    return pl.pallas_call(
        _mix_kernel,
        grid=(T // tm,),
        in_specs=[pl.BlockSpec((tm, D_MODEL), row), pl.BlockSpec((tm, RW_WIDTH), row),
                  pl.BlockSpec((tm, NSA_WIDTH), row), pl.BlockSpec((tm, 2 * D_MODEL), row),
                  pl.BlockSpec((RW_WIDTH, D_MODEL), fixed), pl.BlockSpec((NSA_WIDTH, D_MODEL), fixed),
                  pl.BlockSpec((D_MODEL, D_MODEL), fixed), pl.BlockSpec((1, D_MODEL), fixed),
                  pl.BlockSpec((n_row, D_MODEL), fixed), pl.BlockSpec((n_row, 1), fixed)],
        out_specs=[pl.BlockSpec((tm, D_MODEL), row), pl.BlockSpec((tm,) + ROW_TILE, lambda i: (i, 0, 0)),
                   pl.BlockSpec((8, tm), lambda i: (0, i)), pl.BlockSpec((n_row, LANE), fixed)],
        out_shape=[jax.ShapeDtypeStruct((T, D_MODEL), F32), jax.ShapeDtypeStruct((T,) + ROW_TILE, F32),
                   jax.ShapeDtypeStruct((8, T), F32), jax.ShapeDtypeStruct((n_row, LANE), F32)],
        compiler_params=pltpu.CompilerParams(dimension_semantics=("arbitrary",),
                                             vmem_limit_bytes=VMEM_LIMIT),
        name="mix",
    )(x2d, y_rw, y_nsa, zmg, w_up_r.astype(BF16), w_up_n.astype(BF16), w_out.astype(BF16),
      g_ffn.reshape(1, D_MODEL), wr, br)


def _route_tables(rt, cnt, T):
    n_rows = T * TOP_K + N_EXPERTS * ROW_BLOCK
    n_blk = n_rows // ROW_BLOCK
    counts = cnt[N_GROUPS:N_GROUPS + N_EXPERTS, 0].astype(jnp.int32)
    padded = (counts + ROW_BLOCK - 1) // ROW_BLOCK * ROW_BLOCK
    pends = jnp.cumsum(padded)
    pstarts = pends - padded
    expert = rt[0:TOP_K].astype(jnp.int32)
    rank = rt[2 * TOP_K:3 * TOP_K].astype(jnp.int32)
    seg_start = jnp.sum(jnp.where(expert[..., None] == jnp.arange(N_EXPERTS), pstarts, 0), axis=-1)
    dest = (seg_start + rank).T.reshape(T * TOP_K)
    gates = rt[TOP_K:2 * TOP_K].T
    blk_start = jnp.arange(n_blk) * ROW_BLOCK
    blk_expert = jnp.minimum(jnp.sum(pends[None, :] <= blk_start[:, None], axis=1), N_EXPERTS - 1)
    n_active = (pends[N_EXPERTS - 1:] // ROW_BLOCK).astype(jnp.int32)
    return dest.astype(jnp.int32), gates, blk_expert.astype(jnp.int32), n_active


ROW_TILE = (D_MODEL // LANE, LANE)


def _store_row_tiles(ref, x):
    for c in range(ROW_TILE[0]):
        ref[:, c, :] = x[:, c * LANE:(c + 1) * LANE]


def _load_row_tiles(ref, idx):
    return jnp.concatenate([ref[(*idx, slice(None), c, slice(None))] for c in range(ROW_TILE[0])], axis=1)


EXPERT_BUFS = 3
TOK_BITS = 14
SLOT_SHIFT = int(math.log2(TOP_K))


def _row_info_kernel(dest_ref, word_ref, fill_hbm, info_ref, sem):
    fill = pltpu.make_async_copy(fill_hbm, info_ref, sem)
    fill.start()
    fill.wait()

    def body(a, carry):
        info_ref[dest_ref[a]] = word_ref[a]
        return carry

    lax.fori_loop(0, dest_ref.shape[0], body, 0, unroll=16)


def _row_info(dest, n_tok):
    n_rows = n_tok * TOP_K + N_EXPERTS * ROW_BLOCK
    assert n_tok <= 1 << TOK_BITS and (n_tok * TOP_K + 2 * ROW_BLOCK) << TOK_BITS < 2 ** 31
    a = jnp.arange(n_tok * TOP_K, dtype=jnp.int32)
    tok, k = a // TOP_K, a % TOP_K
    word = tok | ((k * n_tok + tok) << TOK_BITS)
    row = jnp.arange(n_rows, dtype=jnp.int32)
    spare = n_tok * TOP_K + ((row // ROW_BLOCK) % 2) * ROW_BLOCK + row % ROW_BLOCK
    return pl.pallas_call(
        _row_info_kernel,
        in_specs=[pl.BlockSpec(memory_space=pltpu.SMEM), pl.BlockSpec(memory_space=pltpu.SMEM),
                  pl.BlockSpec(memory_space=pl.ANY)],
        out_specs=pl.BlockSpec(memory_space=pltpu.SMEM),
        out_shape=jax.ShapeDtypeStruct((n_rows,), jnp.int32),
        scratch_shapes=[pltpu.SemaphoreType.DMA(())],
        name="row_info",
    )(dest, word, spare << TOK_BITS)


def _expert_kernel(be_ref, nact_ref, info_ref, h2_hbm, wgu_ref, wd_ref, eo_hbm,
                   xbuf, obuf, wgu_b, wd_b, gsem, ssem):
    i = pl.program_id(0)
    n_act = nact_ref[0]
    tok_mask = (1 << TOK_BITS) - 1

    def gather(blk, s):
        for r in range(ROW_BLOCK):
            tok = info_ref[blk * ROW_BLOCK + r] & tok_mask
            pltpu.make_async_copy(h2_hbm.at[tok], xbuf.at[s, r], gsem.at[s]).start()

    def scatter(blk, s):
        for r in range(ROW_BLOCK):
            row = info_ref[blk * ROW_BLOCK + r] >> TOK_BITS
            pltpu.async_copy(obuf.at[s, pl.ds(r, 1)], eo_hbm.at[pl.ds(row, 1)], ssem.at[s], priority=1)

    def drain_gather(s):
        pltpu.make_async_copy(h2_hbm.at[pl.ds(0, ROW_BLOCK)], xbuf.at[s], gsem.at[s]).wait()

    def drain_scatter(s):
        pltpu.make_async_copy(obuf.at[s], eo_hbm.at[pl.ds(0, ROW_BLOCK)], ssem.at[s]).wait()

    n_buf = xbuf.shape[0]

    def block(s, first):
        drain_gather(s)
        gather(jnp.minimum(i + 2, n_act - 1), (s + 2) % n_buf)
        if not first:
            scatter(i - 1, (s + 2) % n_buf)
        gu = jnp.dot(_load_row_tiles(xbuf, (s,)).astype(BF16), wgu_b[...], preferred_element_type=F32)
        gate_h, up_h = gu[:, :D_EXPERT], gu[:, D_EXPERT:]
        mid = gate_h * jax.nn.sigmoid(gate_h) * up_h
        obuf[s] = jnp.dot(mid.astype(BF16), wd_b[...], preferred_element_type=F32)

    @pl.when(jnp.logical_and(i < n_act, jnp.logical_or(i == 0, be_ref[i] != be_ref[jnp.maximum(i - 1, 0)])))
    def _():
        wgu_b[...] = wgu_ref[0].astype(BF16)
        wd_b[...] = wd_ref[0].astype(BF16)

    @pl.when(i == 0)
    def _():
        gather(0, 0)
        gather(jnp.minimum(1, n_act - 1), 1)
        block(0, first=True)

    for s in range(n_buf):
        mine = i % n_buf == s

        @pl.when(jnp.logical_and(mine, jnp.logical_and(i >= 1, i < n_act)))
        def _():
            @pl.when(i >= n_buf)
            def _():
                drain_scatter(s)
            block(s, first=False)

        @pl.when(jnp.logical_and(mine, i == n_act - 1))
        def _():
            for other in ((s + 1) % n_buf, (s + 2) % n_buf):
                drain_gather(other)

            @pl.when(i >= 2)
            def _():
                drain_scatter((s + 1) % n_buf)

            @pl.when(i >= 1)
            def _():
                drain_scatter((s + 2) % n_buf)
            scatter(i, s)
            drain_scatter(s)
            obuf[s] = jnp.zeros_like(obuf[s])
            for half in range(2):
                spare = eo_hbm.at[pl.ds(eo_hbm.shape[0] - (2 - half) * ROW_BLOCK, ROW_BLOCK)]
                pltpu.make_async_copy(obuf.at[s], spare, ssem.at[s]).start()
            for half in range(2):
                drain_scatter(s)


def _experts(h2, info, blk_expert, n_active, w_gate_up, w_down):
    n_tok = h2.shape[0]
    n_blk = blk_expert.shape[0]
    grid_spec = pltpu.PrefetchScalarGridSpec(
        num_scalar_prefetch=3,
        grid=(n_blk,),
        in_specs=[pl.BlockSpec(memory_space=pl.ANY),
                  pl.BlockSpec((1, D_MODEL, 2 * D_EXPERT), lambda i, be, na, info: (be[i], 0, 0)),
                  pl.BlockSpec((1, D_EXPERT, D_MODEL), lambda i, be, na, info: (be[i], 0, 0))],
        out_specs=pl.BlockSpec(memory_space=pl.ANY),
        scratch_shapes=[pltpu.VMEM((EXPERT_BUFS, ROW_BLOCK) + ROW_TILE, F32),
                        pltpu.VMEM((EXPERT_BUFS, ROW_BLOCK, D_MODEL), F32),
                        pltpu.VMEM((D_MODEL, 2 * D_EXPERT), BF16), pltpu.VMEM((D_EXPERT, D_MODEL), BF16),
                        pltpu.SemaphoreType.DMA((EXPERT_BUFS,)), pltpu.SemaphoreType.DMA((EXPERT_BUFS,))])
    return pl.pallas_call(
        _expert_kernel,
        grid_spec=grid_spec,
        out_shape=jax.ShapeDtypeStruct((n_tok * TOP_K + 2 * ROW_BLOCK, D_MODEL), F32),
        compiler_params=pltpu.CompilerParams(dimension_semantics=("arbitrary",),
                                             vmem_limit_bytes=VMEM_LIMIT),
        name="experts",
    )(blk_expert, n_active, info, h2, w_gate_up, w_down)


def _ple_kernel(x1_ref, g_ref, p_ref, *rest, last_layer):
    eo_refs, (wpp_ref, gpn_ref, ggi_ref, wpg_ref, gfin_ref, y_ref) = rest[:TOP_K], rest[TOP_K:]
    tm = x1_ref.shape[0]
    chunks = [slice(c * TM_PROJ, (c + 1) * TM_PROJ) for c in range(tm // TM_PROJ)]
    x2 = []
    for c in chunks:
        moe = g_ref[c, 0:1] * eo_refs[0][c, :]
        for k in range(1, TOP_K):
            moe = moe + g_ref[c, k:k + 1] * eo_refs[k][c, :]
        x2.append(x1_ref[c, :] + moe)
    e = [_rms(_bdot(p_ref[c, :], wpp_ref[...]), gpn_ref[...]) for c in chunks]
    gate = [jax.nn.sigmoid(_bdot(_rms(v, ggi_ref[...]), wpg_ref[...])) for v in x2]
    for c, v, g, ee in zip(chunks, x2, gate, e):
        x3 = v + g * ee
        y_ref[c, :] = _rms(x3, gfin_ref[...]) if last_layer else x3


def _ple(x1, gates, expert_out, p2d, w_pp, g_pn, g_gi, w_pg, g_final, last_layer):
    T = x1.shape[0]
    tm = MIX_CHUNKS * TM_PROJ
    nt = T // tm
    row = lambda i: (i, 0)
    fixed = lambda i: (0, 0)
    vec = lambda a: a.reshape(1, D_MODEL)
    slot_rows = [pl.BlockSpec((tm, D_MODEL), functools.partial(lambda i, k: (i + k * nt, 0), k=k))
                 for k in range(TOP_K)]
    return pl.pallas_call(
        functools.partial(_ple_kernel, last_layer=last_layer),
        grid=(nt,),
        in_specs=[pl.BlockSpec((tm, D_MODEL), row), pl.BlockSpec((tm, TOP_K), row),
                  pl.BlockSpec((tm, PLE_DIM), row)] + slot_rows + [
                  pl.BlockSpec((PLE_DIM, D_MODEL), fixed), pl.BlockSpec((1, D_MODEL), fixed),
                  pl.BlockSpec((1, D_MODEL), fixed), pl.BlockSpec((D_MODEL, D_MODEL), fixed),
                  pl.BlockSpec((1, D_MODEL), fixed)],
        out_specs=pl.BlockSpec((tm, D_MODEL), row),
        out_shape=jax.ShapeDtypeStruct((T, D_MODEL), F32),
        compiler_params=pltpu.CompilerParams(dimension_semantics=("parallel",),
                                             vmem_limit_bytes=VMEM_LIMIT),
        name="ple",
    )(x1, gates, p2d, *([expert_out] * TOP_K), w_pp.astype(BF16), vec(g_pn), vec(g_gi),
      w_pg.astype(BF16), vec(g_final))


def kernel(x, p, g_mix, w_in, mu_shift, rw_w0, rw_w2, rw_a0, rw_a2, rw_g2, rw_k_k, rw_k_a, rw_r_k, rw_ln_w, rw_ln_b, cmp_pos_k, cmp_pos_v, cmp_k_w1, cmp_k_w2, cmp_v_w1, cmp_v_w2, nsa_gate_b, w_up_rwkv, w_up_nsa, w_out, g_ffn, w_group, b_group, w_router, b_router, w_exp_gate_up, w_exp_down, w_ple_proj, g_ple_norm, g_ple_gate_in, w_ple_gate, g_final):
    B, S, D = x.shape
    T = B * S
    depth = p.shape[0]
    xc = x.reshape(T, D)
    for i in range(depth):
        zrw, zq, zkv, zmg, zgate = _proj(xc, g_mix[i], w_in[i], mu_shift[i], S)
        y_rw = _rwkv(zrw.reshape(B, S, RW_IN), rw_w0[i], rw_w2[i], rw_a0[i], rw_a2[i], rw_g2[i],
                     rw_k_k[i], rw_k_a[i], rw_r_k[i], rw_ln_w[i], rw_ln_b[i])
        zkv3 = zkv.reshape(B, S, KV_IN)
        nsa_kv = _nsa_prep(zkv3, cmp_pos_k[i], cmp_pos_v[i], cmp_k_w1[i], cmp_k_w2[i],
                           cmp_v_w1[i], cmp_v_w2[i])
        y_nsa = _nsa_attn(zq.reshape(B, S, NSA_WIDTH), *nsa_kv, zgate.reshape(B, S, LANE),
                          nsa_gate_b[i])
        x1, h2, rt, cnt = _mix(xc, y_rw.reshape(T, RW_WIDTH), y_nsa.reshape(T, NSA_WIDTH), zmg,
                               w_up_rwkv[i], w_up_nsa[i], w_out[i], g_ffn[i], w_group[i], b_group[i],
                               w_router[i], b_router[i])
        dest, gates, blk_expert, n_active = _route_tables(rt, cnt, T)
        expert_out = _experts(h2, _row_info(dest, T), blk_expert, n_active, w_exp_gate_up[i],
                              w_exp_down[i])
        xc = _ple(x1, gates, expert_out, p[i].reshape(T, PLE_DIM), w_ple_proj[i], g_ple_norm[i],
                  g_ple_gate_in[i], w_ple_gate[i], g_final, i == depth - 1)
    return xc.reshape(B, S, D)
```

```python
import functools
import math

import jax
import jax.numpy as jnp
import numpy as np
from jax import lax
from jax.experimental import pallas as pl
from jax.experimental.pallas import tpu as pltpu

F32 = jnp.float32
BF16 = jnp.bfloat16
HI = lax.Precision.HIGHEST

D_MODEL = 1024
RW_HEADS = 8
RW_HEAD_DIM = 64
RW_WIDTH = 512
DECAY_LORA = 64
AAA_LORA = 64
GATE_LORA = 128
GN_EPS = 64e-5
RW_IN = 3 * RW_WIDTH + DECAY_LORA + AAA_LORA + GATE_LORA

NSA_HEADS = 8
NSA_KV_HEADS = 2
NSA_REP = NSA_HEADS // NSA_KV_HEADS
NSA_HEAD_DIM = 64
NSA_WIDTH = 512
NSA_KV_WIDTH = 128
CMP_BLOCK = 32
CMP_STRIDE = 16
SEL_BLOCK = 64
N_SELECT = 8
WINDOW = 512
N_NSA_BRANCH = 3
FORCE_SCORE = 1e6
NEG_INF = -1e30

N_GROUPS = 4
EXPERTS_PER_GROUP = 8
N_EXPERTS = 32
TOP_K = 2
D_EXPERT = 512
ROW_BLOCK = 256
PLE_DIM = 256
NORM_EPS = 1e-6

N_GATE = N_NSA_BRANCH * NSA_HEADS
ATT_IN = NSA_WIDTH + 6 * NSA_KV_WIDTH
KV_OFF = RW_IN + NSA_WIDTH
KV_IN = 6 * NSA_KV_WIDTH
GATE_OFF = RW_IN + ATT_IN
MERGE_OFF = GATE_OFF + N_GATE
LANE = 128

RW_CHUNK = 64
RW_SUB = 16
RW_ROWS = 4
TQ = 256
TK = 256
V_ROWS = NSA_HEAD_DIM + 16
TM_PROJ = 256
MIX_CHUNKS = 2
VMEM_LIMIT = 56 * 1024 * 1024


def _bdot(a, b):
    return jnp.dot(a.astype(BF16), b.astype(BF16), preferred_element_type=F32)


def _bdot_nt(a, b):
    return lax.dot_general(a.astype(BF16), b.astype(BF16), (((1,), (1,)), ((), ())),
                           preferred_element_type=F32)


def _rms(x, g):
    return x * lax.rsqrt(jnp.mean(x * x, axis=-1, keepdims=True) + NORM_EPS) * g


def _proj_kernel(x_ref, g_ref, w_ref, mu_ref, zrw_ref, zq_ref, zkv_ref, zmg_ref, zgate_ref,
                 carry_ref, *, tiles_per_seq):
    i = pl.program_id(0)
    tm = x_ref.shape[0]

    @pl.when(i % tiles_per_seq == 0)
    def _():
        carry_ref[...] = jnp.zeros_like(carry_ref)

    h = _rms(x_ref[...], g_ref[...]).astype(BF16)
    z = jnp.dot(h, w_ref[:, 0:RW_IN], preferred_element_type=F32)
    row = lax.broadcasted_iota(jnp.int32, (tm, 1), 0)
    prev = jnp.where(row == 0, carry_ref[7:8, :], pltpu.roll(z, 1, 0))
    carry_ref[...] = z[tm - 8:tm, :]
    zrw_ref[...] = z + (prev - z) * mu_ref[...]
    zq_ref[...] = jnp.dot(h, w_ref[:, RW_IN:KV_OFF], preferred_element_type=F32)
    zkv_ref[...] = jnp.dot(h, w_ref[:, KV_OFF:GATE_OFF], preferred_element_type=F32)
    zmg_ref[...] = jnp.dot(h, w_ref[:, GATE_OFF:GATE_OFF + 2 * D_MODEL], preferred_element_type=F32)
    zgate_ref[...] = jnp.dot(h, w_ref[:, GATE_OFF + 2 * D_MODEL:], preferred_element_type=F32)


def _proj(x2d, g_mix, w_in, mu, seq):
    T = x2d.shape[0]
    tm = TM_PROJ
    wp = jnp.concatenate(
        [w_in[:, :GATE_OFF], w_in[:, MERGE_OFF:],
         jnp.pad(w_in[:, GATE_OFF:MERGE_OFF], ((0, 0), (0, LANE - N_GATE)))], axis=1).astype(BF16)
    npad = wp.shape[1]
    row = lambda i: (i, 0)
    fixed = lambda i: (0, 0)
    return pl.pallas_call(
        functools.partial(_proj_kernel, tiles_per_seq=seq // tm),
        grid=(T // tm,),
        in_specs=[pl.BlockSpec((tm, D_MODEL), row), pl.BlockSpec((1, D_MODEL), fixed),
                  pl.BlockSpec((D_MODEL, npad), fixed), pl.BlockSpec((1, RW_IN), fixed)],
        out_specs=[pl.BlockSpec((tm, RW_IN), row), pl.BlockSpec((tm, NSA_WIDTH), row),
                   pl.BlockSpec((tm, KV_IN), row), pl.BlockSpec((tm, 2 * D_MODEL), row),
                   pl.BlockSpec((tm, LANE), row)],
        out_shape=[jax.ShapeDtypeStruct((T, RW_IN), F32), jax.ShapeDtypeStruct((T, NSA_WIDTH), F32),
                   jax.ShapeDtypeStruct((T, KV_IN), F32), jax.ShapeDtypeStruct((T, 2 * D_MODEL), F32),
                   jax.ShapeDtypeStruct((T, LANE), F32)],
        scratch_shapes=[pltpu.VMEM((8, RW_IN), F32)],
        compiler_params=pltpu.CompilerParams(dimension_semantics=("arbitrary",),
                                             vmem_limit_bytes=VMEM_LIMIT),
        name="proj",
    )(x2d, g_mix.reshape(1, D_MODEL), wp, mu.reshape(1, RW_IN))


PAIR = 2 * RW_HEAD_DIM


def _pair_blocks(x):
    low = lax.broadcasted_iota(jnp.int32, (1, PAIR), 1) < RW_HEAD_DIM
    return jnp.concatenate([jnp.where(low, x, 0.0), jnp.where(low, 0.0, x)], axis=0)


def _pmm(a, b):
    return _bdot(a, _pair_blocks(b))


def _unit_lower_inverse(a_strict, sub_mask, eye):
    ad = [jnp.where(sub_mask, a, 0.0) for a in a_strict]
    ao = [a - d for a, d in zip(a_strict, ad)]
    td = [eye - d for d in ad]
    pw = ad
    for _ in range(int(math.log2(RW_SUB)) - 1):
        pw = [_pmm(x, x) for x in pw]
        td = [_pmm(t, eye + x) for t, x in zip(td, pw)]
    n = [_pmm(t, o) for t, o in zip(td, ao)]
    t = [eye - x for x in n]
    pw = n
    for _ in range(int(math.log2(RW_CHUNK // RW_SUB)) - 1):
        pw = [_pmm(x, x) for x in pw]
        t = [_pmm(a, eye + x) for a, x in zip(t, pw)]
    return [_pmm(a, d) for a, d in zip(t, td)]


def _rwkv_kernel(z_ref, w0_ref, w2_ref, a0_ref, a2_ref, g2_ref, kk_ref, ka_ref, rk_ref, lnw_ref,
                 lnb_ref, avg_ref, o_ref, h_ref):
    c = pl.program_id(1)
    C = RW_CHUNK
    n_pair = RW_WIDTH // PAIR
    nt = (((1,), (1,)), ((), ()))

    @pl.when(c == 0)
    def _():
        h_ref[...] = jnp.zeros_like(h_ref)

    ti = lax.broadcasted_iota(jnp.int32, (C, 1), 0)
    si = lax.broadcasted_iota(jnp.int32, (1, PAIR), 1) & (RW_HEAD_DIM - 1)
    incl, strict = ti >= si, ti > si
    eye = (ti == si).astype(F32)
    sub_shift = int(math.log2(RW_SUB))
    sub_mask = (ti >> sub_shift) == (si >> sub_shift)
    row2 = lax.broadcasted_iota(jnp.int32, (PAIR, 1), 0)
    col2 = lax.broadcasted_iota(jnp.int32, (1, PAIR), 1)
    same_head = (row2 < RW_HEAD_DIM) == (col2 < RW_HEAD_DIM)
    eye2 = row2 == col2
    tri = (lax.broadcasted_iota(jnp.int32, (C, C), 0)
           >= lax.broadcasted_iota(jnp.int32, (C, C), 1)).astype(BF16)

    def head_mean(x):
        xs = jnp.concatenate([x[:, p * PAIR:(p + 1) * PAIR] for p in range(n_pair)], axis=0)
        ms = _bdot(xs, avg_ref[...])
        return jnp.concatenate([ms[p * C:(p + 1) * C] for p in range(n_pair)], axis=1)

    n_rows = z_ref.shape[0]
    rows = []
    for n in range(n_rows):
        z = z_ref[n]
        zr, zk, zv = z[:, 0:512], z[:, 512:1024], z[:, 1024:1536]
        zw, za, zg = z[:, 1536:1600], z[:, 1600:1664], z[:, 1664:1792]
        w_raw = w0_ref[...] + _bdot(jnp.tanh(zw), w2_ref[...])
        logw = -jax.nn.sigmoid(w_raw) * math.exp(-0.5)
        a = jax.nn.sigmoid(a0_ref[...] + _bdot(za, a2_ref[...]))
        gate = _bdot(jax.nn.sigmoid(zg), g2_ref[...])
        kk = zk * kk_ref[...]
        kk = kk / jnp.maximum(jnp.sqrt(head_mean(kk * kk) * RW_HEAD_DIM), 1e-12)
        k = zk * (1.0 + (a - 1.0) * ka_ref[...])
        b = kk * a

        w_hi = logw.astype(BF16)
        w_lo = (logw - w_hi.astype(F32)).astype(BF16)
        cum = (jnp.dot(tri, w_hi, preferred_element_type=F32)
               + jnp.dot(tri, w_lo, preferred_element_type=F32))
        cum_last = cum[C - 1:C, :]
        g_inv = jnp.exp(-cum)
        g_end = jnp.exp(cum_last - cum)
        rows.append(dict(rt=zr * jnp.exp(cum), kt=k * g_inv, bt=b * g_inv, qt=kk * jnp.exp(cum - logw),
                         kh=k * g_end, bh=b * g_end, v=zv, g_last=jnp.exp(cum_last), gate=gate,
                         bonus=head_mean(zr * k * rk_ref[...]) * RW_HEAD_DIM * zv))

    chains = [(n, slice(p * PAIR, (p + 1) * PAIR)) for n in range(n_rows) for p in range(n_pair)]
    part = lambda name: [rows[n][name][:, sl] for n, sl in chains]
    qt, rt, kt, bt, kh, bh, v = (part(x) for x in ("qt", "rt", "kt", "bt", "kh", "bh", "v"))
    lhs = [jnp.concatenate([q, r], axis=0).astype(BF16) for q, r in zip(qt, rt)]
    ab = [lax.dot_general(l, _pair_blocks(x).astype(BF16), nt, preferred_element_type=F32)
          for l, x in zip(lhs, bt)]
    ak = [lax.dot_general(l, _pair_blocks(x).astype(BF16), nt, preferred_element_type=F32)
          for l, x in zip(lhs, kt)]
    a_kb = [jnp.where(strict, x[0:C], 0.0) for x in ab]
    a_rb = [jnp.where(incl, x[C:2 * C], 0.0) for x in ab]
    a_kk = [jnp.where(strict, x[0:C], 0.0) for x in ak]
    a_rk = [jnp.where(incl, x[C:2 * C], 0.0) for x in ak]
    t_inv = _unit_lower_inverse(a_kb, sub_mask, eye)

    h = [h_ref[n, sl.start // PAIR] for n, sl in chains]
    vb = [_pair_blocks(x) for x in v]
    rhs = [_bdot(jnp.concatenate([q, akk], axis=1), jnp.concatenate([hh, vv], axis=0))
           for q, akk, hh, vv in zip(qt, a_kk, h, vb)]
    u = [_pmm(t, x) for t, x in zip(t_inv, rhs)]
    outs = [_bdot(jnp.concatenate([r, ark, -arb], axis=1), jnp.concatenate([hh, vv, _pair_blocks(uu)], axis=0))
            for r, ark, arb, hh, vv, uu in zip(rt, a_rk, a_rb, h, vb, u)]
    upd = [_bdot(jnp.concatenate([x, -y], axis=0).T, jnp.concatenate([vv, uu], axis=0))
           for x, y, vv, uu in zip(kh, bh, v, u)]
    for (n, sl), hh, dd in zip(chains, h, upd):
        decay_col = jnp.sum(jnp.where(eye2, rows[n]["g_last"][:, sl], 0.0), axis=1, keepdims=True)
        h_ref[n, sl.start // PAIR] = decay_col * hh + jnp.where(same_head, dd, 0.0)

    for n in range(n_rows):
        o = jnp.concatenate(outs[n * n_pair:(n + 1) * n_pair], axis=1)
        d = o - head_mean(o)
        on = d * lax.rsqrt(head_mean(d * d) + GN_EPS)
        o_ref[n] = (on * lnw_ref[...] + lnb_ref[...] + rows[n]["bonus"]) * rows[n]["gate"]


def _rwkv(zrw, w0, w2, a0, a2, g2, k_k, k_a, r_k, ln_w, ln_b):
    B, S, _ = zrw.shape
    C = RW_CHUNK
    nb = RW_ROWS
    hid = np.arange(PAIR) // RW_HEAD_DIM
    avg = jnp.asarray((hid[:, None] == hid[None, :]).astype(np.float32) / RW_HEAD_DIM)
    vec = lambda a: a.reshape(1, RW_WIDTH)
    fixed = lambda shape: pl.BlockSpec(shape, lambda b, c: (0,) * len(shape))
    return pl.pallas_call(
        _rwkv_kernel,
        grid=(B // nb, S // C),
        in_specs=[pl.BlockSpec((nb, C, RW_IN), lambda b, c: (b, c, 0)),
                  fixed((1, RW_WIDTH)), fixed((DECAY_LORA, RW_WIDTH)),
                  fixed((1, RW_WIDTH)), fixed((AAA_LORA, RW_WIDTH)),
                  fixed((GATE_LORA, RW_WIDTH)), fixed((1, RW_WIDTH)), fixed((1, RW_WIDTH)),
                  fixed((1, RW_WIDTH)), fixed((1, RW_WIDTH)), fixed((1, RW_WIDTH)),
                  fixed((PAIR, PAIR))],
        out_specs=pl.BlockSpec((nb, C, RW_WIDTH), lambda b, c: (b, c, 0)),
        out_shape=jax.ShapeDtypeStruct((B, S, RW_WIDTH), F32),
        scratch_shapes=[pltpu.VMEM((nb, RW_WIDTH // PAIR, PAIR, PAIR), F32)],
        compiler_params=pltpu.CompilerParams(dimension_semantics=("parallel", "arbitrary"),
                                             vmem_limit_bytes=VMEM_LIMIT),
        name="rwkv",
    )(zrw, vec(w0), w2, vec(a0), a2, g2, vec(k_k), vec(k_a), vec(r_k), vec(ln_w), vec(ln_b), avg)


def _gelu_tanh(x):
    return 0.5 * x * (1.0 + jnp.tanh(math.sqrt(2.0 / math.pi) * (x + 0.044715 * (x * x * x))))


def _key_features(pos_hi, pos_lo, block, n, n_sel):
    lane = lax.broadcasted_iota(jnp.int32, (n, NSA_HEAD_DIM), 1)
    feat = jnp.where(lane == n_sel, pos_hi, jnp.where(lane == n_sel + 1, pos_lo, 0.0))
    return feat if block is None else jnp.where(lane == block, 1.0, feat)


def _nsa_prep_kernel(zkc_ref, zvc_ref, zks_ref, zvs_ref, zkw_ref, zvw_ref, pk_ref, pv_ref, kw1_ref,
                     kw2_ref, vw1_ref, vw2_ref, kc_ref, vct_ref, ksa_ref, kwa_ref, vst_ref, vwt_ref):
    S = zkc_ref.shape[1]
    n_grp = S // CMP_STRIDE
    Dh = NSA_HEAD_DIM
    half = CMP_BLOCK // 2
    n_sel = S // SEL_BLOCK
    jrow = lax.broadcasted_iota(jnp.int32, (n_grp, 1), 0)
    cmp_feat = _key_features((jrow >> 3).astype(F32),
                             ((jrow & 7) * CMP_STRIDE).astype(F32) + 0.5 * (CMP_BLOCK - 1),
                             None, n_grp, n_sel)
    for is_v, (z_ref, pos_ref, w1_ref, w2_ref) in enumerate(((zkc_ref, pk_ref, kw1_ref, kw2_ref),
                                                            (zvc_ref, pv_ref, vw1_ref, vw2_ref))):
        for g in range(NSA_KV_HEADS):
            lo = jnp.zeros((n_grp, Dh), F32)
            hi = jnp.zeros((n_grp, Dh), F32)
            for l in range(half):
                xs = z_ref[0, pl.ds(l, n_grp, stride=CMP_STRIDE), :]
                xg = xs[:, g * Dh:(g + 1) * Dh]
                lo = lo + _bdot(xg + pos_ref[l:l + 1, :], w1_ref[l * Dh:(l + 1) * Dh, :])
                hi = hi + _bdot(xg + pos_ref[half + l:half + l + 1, :],
                                w1_ref[(half + l) * Dh:(half + l + 1) * Dh, :])
            pre = lo + pltpu.roll(hi, n_grp - 1, 0)
            out = jnp.where(jrow < n_grp - 1, _bdot(_gelu_tanh(pre), w2_ref[...]), 0.0)
            if is_v:
                out_t = jnp.concatenate([out, jnp.zeros_like(out)], axis=1).T
                vct_ref[0, g] = out_t[0:Dh, :].astype(BF16)
            else:
                kc_ref[0, g] = jnp.concatenate([out, cmp_feat], axis=1)

    prow = lax.broadcasted_iota(jnp.int32, (S, 1), 0)
    p_hi, p_lo = (prow >> 7).astype(F32), (prow & (LANE - 1)).astype(F32)
    for z_ref, out_ref, block in ((zks_ref, ksa_ref, prow >> int(math.log2(SEL_BLOCK))),
                                  (zkw_ref, kwa_ref, None)):
        kfull = z_ref[0]
        key_feat = _key_features(p_hi, p_lo, block, S, n_sel)
        for g in range(NSA_KV_HEADS):
            out_ref[0, g] = jnp.concatenate([kfull[:, g * Dh:(g + 1) * Dh], key_feat], axis=1).astype(BF16)
    ones_row = (lax.broadcasted_iota(jnp.int32, (V_ROWS - Dh, TK), 0) == 0).astype(F32)
    for z_ref, out_ref in ((zvs_ref, vst_ref), (zvw_ref, vwt_ref)):
        for j in range(S // TK):
            vt = z_ref[0, j * TK:(j + 1) * TK, :].T
            out_ref[0, j] = jnp.concatenate(
                [piece for g in range(NSA_KV_HEADS) for piece in (vt[g * Dh:(g + 1) * Dh], ones_row)],
                axis=0).astype(BF16)


def _nsa_prep(zkv, pos_k, pos_v, kw1, kw2, vw1, vw2):
    B, S, _ = zkv.shape
    n_grp = S // CMP_STRIDE
    n_kt = S // TK
    Dh = NSA_HEAD_DIM
    G = NSA_KV_HEADS
    fixed = lambda shape: pl.BlockSpec(shape, lambda b: (0,) * len(shape))
    col = lambda c: pl.BlockSpec((1, S, NSA_KV_WIDTH), lambda b: (b, 0, c))
    whole = lambda shape: pl.BlockSpec((1,) + shape, lambda b: (b,) + (0,) * len(shape))
    shapes = [((G, n_grp, 2 * Dh), F32), ((G, Dh, n_grp), BF16), ((G, S, 2 * Dh), BF16),
              ((G, S, 2 * Dh), BF16), ((n_kt, G * V_ROWS, TK), BF16), ((n_kt, G * V_ROWS, TK), BF16)]
    return pl.pallas_call(
        _nsa_prep_kernel,
        grid=(B,),
        in_specs=[col(c) for c in range(6)] + [
            fixed((CMP_BLOCK, Dh)), fixed((CMP_BLOCK, Dh)),
            fixed((CMP_BLOCK * Dh, Dh)), fixed((Dh, Dh)),
            fixed((CMP_BLOCK * Dh, Dh)), fixed((Dh, Dh))],
        out_specs=[whole(s) for s, _ in shapes],
        out_shape=[jax.ShapeDtypeStruct((B,) + s, d) for s, d in shapes],
        compiler_params=pltpu.CompilerParams(dimension_semantics=("parallel",),
                                             vmem_limit_bytes=VMEM_LIMIT),
        name="nsa_prep",
    )(zkv, zkv, zkv, zkv, zkv, zkv, pos_k, pos_v, kw1, kw2, vw1, vw2)


def _nsa_attn_kernel(q_ref, kc_ref, vct_ref, ksa_ref, kwa_ref, vst_ref, vwt_ref, gl_ref, gb_ref,
                     ovlt_ref, slope_ref, o_ref, acc_ref, ot_ref):
    i = pl.program_id(1)
    Dh = NSA_HEAD_DIM
    R = NSA_REP
    N = R * TQ
    n_cmp_pad = kc_ref.shape[2]
    n_sel = ovlt_ref.shape[0]
    G = NSA_KV_HEADS
    nt = (((1,), (1,)), ((), ()))
    log2e = math.log2(math.e)
    t0 = i * TQ
    t_row = t0 + lax.broadcasted_iota(jnp.int32, (1, TQ), 1)
    c_col = lax.broadcasted_iota(jnp.int32, (TK, 1), 0)
    sgate_t = jax.nn.sigmoid(gl_ref[0] + gb_ref[...]).T
    lane_f = lax.broadcasted_iota(jnp.int32, (1, Dh), 1)
    heads = lambda x: jnp.concatenate([x] * R, axis=1)

    def key_dist(j):
        return t_row - (j * TK + c_col)

    def queries(g, sel_feat):
        parts = []
        for r in range(R):
            h = g * R + r
            sl = slope_ref[:, h:h + 1] * log2e
            feat = jnp.where(lane_f == n_sel, sl * LANE, jnp.where(lane_f == n_sel + 1, sl, sel_feat))
            parts.append(jnp.concatenate(
                [q_ref[0, :, h * Dh:(h + 1) * Dh] * (Dh ** -0.5 * log2e), jnp.broadcast_to(feat, (TQ, Dh))],
                axis=1))
        return jnp.concatenate(parts, axis=0)

    qab, o_cmp = [], []
    for g in range(G):
        jc = lax.broadcasted_iota(jnp.int32, (n_cmp_pad, 1), 0)
        ok_c = (jc * CMP_STRIDE + (CMP_BLOCK - 1) <= t_row) & (jc < n_cmp_pad - 1)
        s_c = (lax.dot_general(kc_ref[0, g], queries(g, 0.0), nt, precision=HI, preferred_element_type=F32)
               + heads(jnp.where(ok_c, 0.0, NEG_INF)))
        e_c = jnp.exp2(s_c - jnp.max(s_c, axis=0, keepdims=True))
        any_c = heads(t_row >= CMP_BLOCK - 1)
        p_c = e_c * jnp.where(any_c, 1.0 / jnp.sum(e_c, axis=0, keepdims=True), 0.0)
        o_cmp.append(jnp.dot(vct_ref[0, g], p_c.astype(BF16), preferred_element_type=F32))

        p_sum = p_c[:, 0:TQ]
        for r in range(1, R):
            p_sum = p_sum + p_c[:, r * TQ:(r + 1) * TQ]
        imp = jnp.dot(ovlt_ref[...], p_sum, precision=HI, preferred_element_type=F32)
        kb = lax.broadcasted_iota(jnp.int32, (n_sel, 1), 0)
        kbf = kb.astype(F32)
        blk_t = t_row >> int(math.log2(SEL_BLOCK))
        forced = (kb == 0) | (kb == blk_t) | (kb == blk_t - 1)
        cur = jnp.where(forced, FORCE_SCORE, jnp.where(kb <= blk_t, imp, -FORCE_SCORE))
        sel_bias = jnp.full((n_sel, TQ), NEG_INF, F32)
        for _ in range(min(N_SELECT, n_sel)):
            mx = jnp.max(cur, axis=0, keepdims=True)
            first = jnp.min(jnp.where(cur == mx, kbf, float(n_sel)), axis=0, keepdims=True)
            hit = kbf == first
            sel_bias = jnp.where(hit, 0.0, sel_bias)
            cur = jnp.where(hit, -3e38, cur)
        sel_feat = jnp.concatenate([sel_bias, jnp.zeros((LANE - n_sel, TQ), F32)], axis=0).T[:, 0:Dh]
        qab.append(queries(g, sel_feat).astype(BF16))

    VR = vst_ref.shape[2] // G

    def tile(j, m, k_ref, vt_ref, bias, slot):
        ks = [k_ref[0, g, pl.ds(pl.multiple_of(j * TK, TK), TK), :] for g in range(G)]
        vts = [vt_ref[0, j, g * VR:(g + 1) * VR, :] for g in range(G)]
        hs = range(NSA_HEADS)
        s = [lax.dot_general(ks[h // R], qab[h // R][(h % R) * TQ:(h % R + 1) * TQ], nt,
                             preferred_element_type=F32) for h in hs]
        if bias is not None:
            s = [x + bias for x in s]
        m_new = [jnp.maximum(m[h], jnp.max(s[h], axis=0, keepdims=True)) for h in hs]
        alpha = [jnp.exp2(m[h] - m_new[h]) for h in hs]
        p = [jnp.exp2(s[h] - m_new[h]).astype(BF16) for h in hs]
        pv = [jnp.dot(vts[h // R], p[h], preferred_element_type=F32) for h in hs]
        for h in hs:
            acc_ref[slot, h] = alpha[h] * acc_ref[slot, h] + pv[h]
        return tuple(m_new)

    def window_bias(j):
        d = key_dist(j)
        return jnp.where((d >= 0) & (d < WINDOW), 0.0, NEG_INF)

    init = (jnp.full((1, TQ), NEG_INF, F32),) * NSA_HEADS
    acc_ref[...] = jnp.zeros_like(acc_ref)
    causal = jnp.where(key_dist(i) >= 0, 0.0, NEG_INF)
    carry = lax.fori_loop(0, i, lambda j, c: tile(j, c, ksa_ref, vst_ref, None, 0), init)
    tile(i, carry, ksa_ref, vst_ref, causal, 0)
    carry = lax.fori_loop(jnp.maximum(i - WINDOW // TK, 0), i,
                          lambda j, c: tile(j, c, kwa_ref, vwt_ref, window_bias(j), 1), init)
    tile(i, carry, kwa_ref, vwt_ref, causal, 1)

    for h in range(NSA_HEADS):
        g, r = divmod(h, R)
        acc_s, acc_w = acc_ref[0, h], acc_ref[1, h]
        ot_ref[h * Dh:(h + 1) * Dh, :] = (
            sgate_t[3 * h:3 * h + 1, :] * o_cmp[g][:, r * TQ:(r + 1) * TQ]
            + sgate_t[3 * h + 1:3 * h + 2, :] * (acc_s[0:Dh] * (1.0 / acc_s[Dh:Dh + 1]))
            + sgate_t[3 * h + 2:3 * h + 3, :] * (acc_w[0:Dh] * (1.0 / acc_w[Dh:Dh + 1])))
    o_ref[0] = ot_ref[...].T


def _nsa_attn(zq, kc, vct, ksa, kwa, vst, vwt, zgate, gate_b):
    B, S, _ = zq.shape
    n_sel = S // SEL_BLOCK
    n_cmp = (S - CMP_BLOCK) // CMP_STRIDE + 1
    n_cmp_pad = kc.shape[2]
    n_kt = S // TK
    G, Dh = NSA_KV_HEADS, NSA_HEAD_DIM
    cmp_start = np.arange(n_cmp) * CMP_STRIDE
    sel_start = np.arange(n_sel) * SEL_BLOCK
    overlap = np.clip(np.minimum(cmp_start[:, None] + CMP_BLOCK, sel_start[None, :] + SEL_BLOCK)
                      - np.maximum(cmp_start[:, None], sel_start[None, :]), 0, None) / CMP_BLOCK
    ovlt = np.zeros((n_sel, n_cmp_pad), np.float32)
    ovlt[:, :n_cmp] = overlap.T
    slopes = (2.0 ** (-8.0 * np.arange(1, NSA_HEADS + 1) / NSA_HEADS)).astype(np.float32).reshape(1, NSA_HEADS)
    gb = jnp.pad(gate_b, (0, LANE - N_GATE)).reshape(1, LANE)
    fixed = lambda shape: pl.BlockSpec(shape, lambda b, i: (0,) * len(shape))
    per_b = lambda shape: pl.BlockSpec((1,) + shape, lambda b, i: (b,) + (0,) * len(shape))
    return pl.pallas_call(
        _nsa_attn_kernel,
        grid=(B, S // TQ),
        in_specs=[pl.BlockSpec((1, TQ, NSA_WIDTH), lambda b, i: (b, i, 0)),
                  per_b((G, n_cmp_pad, 2 * Dh)), per_b((G, Dh, n_cmp_pad)),
                  per_b((G, S, 2 * Dh)), per_b((G, S, 2 * Dh)),
                  per_b((n_kt, G * V_ROWS, TK)), per_b((n_kt, G * V_ROWS, TK)),
                  pl.BlockSpec((1, TQ, LANE), lambda b, i: (b, i, 0)),
                  fixed((1, LANE)), fixed((n_sel, n_cmp_pad)), fixed((1, NSA_HEADS))],
        out_specs=pl.BlockSpec((1, TQ, NSA_WIDTH), lambda b, i: (b, i, 0)),
        out_shape=jax.ShapeDtypeStruct((B, S, NSA_WIDTH), F32),
        scratch_shapes=[pltpu.VMEM((2, NSA_HEADS, V_ROWS, TQ), F32),
                        pltpu.VMEM((NSA_WIDTH, TQ), F32)],
        compiler_params=pltpu.CompilerParams(dimension_semantics=("parallel", "arbitrary"),
                                             vmem_limit_bytes=VMEM_LIMIT),
        name="nsa_attn",
    )(zq, kc, vct, ksa, kwa, vst, vwt, zgate, gb, jnp.asarray(ovlt), jnp.asarray(slopes))


def _mix_kernel(x_ref, yr_ref, yn_ref, zmg_ref, ur_ref, un_ref, wo_ref, gf_ref, wr_ref, br_ref,
                x1_ref, h2_ref, rt_ref, cnt_ref):
    tm = x_ref.shape[0]
    nt = (((1,), (1,)), ((), ()))
    chunks = [slice(c * TM_PROJ, (c + 1) * TM_PROJ) for c in range(tm // TM_PROJ)]
    up_r = [_bdot(yr_ref[c, :], ur_ref[...]) for c in chunks]
    up_n = [_bdot(yn_ref[c, :], un_ref[...]) for c in chunks]
    mixed = [jax.nn.sigmoid(zmg_ref[c, 0:D_MODEL]) * a + jax.nn.sigmoid(zmg_ref[c, D_MODEL:2 * D_MODEL]) * b
             for c, a, b in zip(chunks, up_r, up_n)]
    x1 = [x_ref[c, :] + _bdot(m, wo_ref[...]) for c, m in zip(chunks, mixed)]
    h2 = [_rms(v, gf_ref[...]) for v in x1]
    for c, v, h in zip(chunks, x1, h2):
        x1_ref[c, :] = v
        _store_row_tiles(h2_ref.at[c], h)
    n_row = wr_ref.shape[0]
    logits = jnp.concatenate(
        [lax.dot_general(wr_ref[...], h, nt, precision=HI, preferred_element_type=F32) for h in h2],
        axis=1) + br_ref[...]
    row = lax.broadcasted_iota(jnp.int32, (n_row, 1), 0).astype(F32)
    gl = jnp.where(row < N_GROUPS, logits, NEG_INF)
    gmax = jnp.max(gl, axis=0, keepdims=True)
    g_sel = jnp.min(jnp.where(gl == gmax, row, float(n_row)), axis=0, keepdims=True)
    p_group = 1.0 / jnp.sum(jnp.exp(gl - gmax), axis=0, keepdims=True)
    e_row = row - N_GROUPS
    in_grp = ((e_row >= g_sel * EXPERTS_PER_GROUP) & (e_row < (g_sel + 1.0) * EXPERTS_PER_GROUP)
              & (e_row < N_EXPERTS))
    el = jnp.where(in_grp, logits, NEG_INF)
    m1 = jnp.max(el, axis=0, keepdims=True)
    i1 = jnp.min(jnp.where(el == m1, e_row, float(n_row)), axis=0, keepdims=True)
    el2 = jnp.where(e_row == i1, 2.0 * NEG_INF, el)
    m2 = jnp.max(el2, axis=0, keepdims=True)
    i2 = jnp.min(jnp.where(el2 == m2, e_row, float(n_row)), axis=0, keepdims=True)
    r2 = jnp.exp(m2 - m1)
    g1 = p_group / (1.0 + r2)
    g2 = p_group * r2 / (1.0 + r2)

    @pl.when(pl.program_id(0) == 0)
    def _():
        cnt_ref[...] = jnp.zeros_like(cnt_ref)

    pick1, pick2 = e_row == i1, e_row == i2
    both = pick1.astype(F32) + pick2.astype(F32)
    earlier = (lax.broadcasted_iota(jnp.int32, (tm, tm), 0)
               < lax.broadcasted_iota(jnp.int32, (tm, tm), 1)).astype(BF16)
    before = jnp.dot(both.astype(BF16), earlier, preferred_element_type=F32) + cnt_ref[:, 0:1]
    rank1 = jnp.sum(jnp.where(pick1, before, 0.0), axis=0, keepdims=True)
    rank2 = jnp.sum(jnp.where(pick2, before, 0.0), axis=0, keepdims=True)
    cnt_ref[...] = cnt_ref[...] + jnp.sum(both, axis=1, keepdims=True)
    rt_ref[...] = jnp.concatenate([i1, i2, g1, g2, rank1, rank2, jnp.zeros((2, tm), F32)], axis=0)


def _mix(x2d, y_rw, y_nsa, zmg, w_up_r, w_up_n, w_out, g_ffn, w_group, b_group, w_router, b_router):
    T = x2d.shape[0]
    tm = MIX_CHUNKS * TM_PROJ
    n_r = N_GROUPS + N_EXPERTS
    n_row = -(-n_r // 8) * 8
    wr = jnp.pad(jnp.concatenate([w_group, w_router], axis=1).T, ((0, n_row - n_r), (0, 0)))
    br = jnp.pad(jnp.concatenate([b_group, b_router]), (0, n_row - n_r)).reshape(n_row, 1)
    row = lambda i: (i, 0)
    fixed = lambda i: (0, 0)
    return pl.pallas_call(
        _mix_kernel,
        grid=(T // tm,),
        in_specs=[pl.BlockSpec((tm, D_MODEL), row), pl.BlockSpec((tm, RW_WIDTH), row),
                  pl.BlockSpec((tm, NSA_WIDTH), row), pl.BlockSpec((tm, 2 * D_MODEL), row),
                  pl.BlockSpec((RW_WIDTH, D_MODEL), fixed), pl.BlockSpec((NSA_WIDTH, D_MODEL), fixed),
                  pl.BlockSpec((D_MODEL, D_MODEL), fixed), pl.BlockSpec((1, D_MODEL), fixed),
                  pl.BlockSpec((n_row, D_MODEL), fixed), pl.BlockSpec((n_row, 1), fixed)],
        out_specs=[pl.BlockSpec((tm, D_MODEL), row), pl.BlockSpec((tm,) + ROW_TILE, lambda i: (i, 0, 0)),
                   pl.BlockSpec((8, tm), lambda i: (0, i)), pl.BlockSpec((n_row, LANE), fixed)],
        out_shape=[jax.ShapeDtypeStruct((T, D_MODEL), F32), jax.ShapeDtypeStruct((T,) + ROW_TILE, F32),
                   jax.ShapeDtypeStruct((8, T), F32), jax.ShapeDtypeStruct((n_row, LANE), F32)],
        compiler_params=pltpu.CompilerParams(dimension_semantics=("arbitrary",),
                                             vmem_limit_bytes=VMEM_LIMIT),
        name="mix",
    )(x2d, y_rw, y_nsa, zmg, w_up_r.astype(BF16), w_up_n.astype(BF16), w_out.astype(BF16),
      g_ffn.reshape(1, D_MODEL), wr, br)


def _route_tables(rt, cnt, T):
    n_rows = T * TOP_K + N_EXPERTS * ROW_BLOCK
    n_blk = n_rows // ROW_BLOCK
    counts = cnt[N_GROUPS:N_GROUPS + N_EXPERTS, 0].astype(jnp.int32)
    padded = (counts + ROW_BLOCK - 1) // ROW_BLOCK * ROW_BLOCK
    pends = jnp.cumsum(padded)
    pstarts = pends - padded
    expert = rt[0:TOP_K].astype(jnp.int32)
    rank = rt[2 * TOP_K:3 * TOP_K].astype(jnp.int32)
    seg_start = jnp.sum(jnp.where(expert[..., None] == jnp.arange(N_EXPERTS), pstarts, 0), axis=-1)
    dest = (seg_start + rank).T.reshape(T * TOP_K)
    gates = rt[TOP_K:2 * TOP_K].T
    blk_start = jnp.arange(n_blk) * ROW_BLOCK
    blk_expert = jnp.minimum(jnp.sum(pends[None, :] <= blk_start[:, None], axis=1), N_EXPERTS - 1)
    n_active = (pends[N_EXPERTS - 1:] // ROW_BLOCK).astype(jnp.int32)
    return dest.astype(jnp.int32), gates, blk_expert.astype(jnp.int32), n_active


ROW_TILE = (D_MODEL // LANE, LANE)


def _store_row_tiles(ref, x):
    for c in range(ROW_TILE[0]):
        ref[:, c, :] = x[:, c * LANE:(c + 1) * LANE]


def _load_row_tiles(ref, idx):
    return jnp.concatenate([ref[(*idx, slice(None), c, slice(None))] for c in range(ROW_TILE[0])], axis=1)


EXPERT_BUFS = 3
TOK_BITS = 14
SLOT_SHIFT = int(math.log2(TOP_K))


def _row_info_kernel(dest_ref, word_ref, fill_hbm, info_ref, sem):
    fill = pltpu.make_async_copy(fill_hbm, info_ref, sem)
    fill.start()
    fill.wait()

    def body(a, carry):
        info_ref[dest_ref[a]] = word_ref[a]
        return carry

    lax.fori_loop(0, dest_ref.shape[0], body, 0, unroll=16)


def _row_info(dest, n_tok):
    n_rows = n_tok * TOP_K + N_EXPERTS * ROW_BLOCK
    assert n_tok <= 1 << TOK_BITS and (n_tok * TOP_K + 2 * ROW_BLOCK) << TOK_BITS < 2 ** 31
    a = jnp.arange(n_tok * TOP_K, dtype=jnp.int32)
    tok, k = a // TOP_K, a % TOP_K
    word = tok | ((k * n_tok + tok) << TOK_BITS)
    row = jnp.arange(n_rows, dtype=jnp.int32)
    spare = n_tok * TOP_K + ((row // ROW_BLOCK) % 2) * ROW_BLOCK + row % ROW_BLOCK
    return pl.pallas_call(
        _row_info_kernel,
        in_specs=[pl.BlockSpec(memory_space=pltpu.SMEM), pl.BlockSpec(memory_space=pltpu.SMEM),
                  pl.BlockSpec(memory_space=pl.ANY)],
        out_specs=pl.BlockSpec(memory_space=pltpu.SMEM),
        out_shape=jax.ShapeDtypeStruct((n_rows,), jnp.int32),
        scratch_shapes=[pltpu.SemaphoreType.DMA(())],
        name="row_info",
    )(dest, word, spare << TOK_BITS)


def _expert_kernel(be_ref, nact_ref, info_ref, h2_hbm, wgu_ref, wd_ref, eo_hbm,
                   xbuf, obuf, wgu_b, wd_b, gsem, ssem):
    i = pl.program_id(0)
    n_act = nact_ref[0]
    tok_mask = (1 << TOK_BITS) - 1

    def gather(blk, s):
        for r in range(ROW_BLOCK):
            tok = info_ref[blk * ROW_BLOCK + r] & tok_mask
            pltpu.async_copy(h2_hbm.at[tok], xbuf.at[s, r], gsem.at[s], priority=r % 2)

    def scatter(blk, s):
        for r in range(ROW_BLOCK):
            row = info_ref[blk * ROW_BLOCK + r] >> TOK_BITS
            pltpu.async_copy(obuf.at[s, pl.ds(r, 1)], eo_hbm.at[pl.ds(row, 1)], ssem.at[s],
                             priority=r % 2)

    def drain_gather(s):
        pltpu.make_async_copy(h2_hbm.at[pl.ds(0, ROW_BLOCK)], xbuf.at[s], gsem.at[s]).wait()

    def drain_scatter(s):
        pltpu.make_async_copy(obuf.at[s], eo_hbm.at[pl.ds(0, ROW_BLOCK)], ssem.at[s]).wait()

    n_buf = xbuf.shape[0]

    def block(s, first):
        drain_gather(s)
        gather(jnp.minimum(i + 2, n_act - 1), (s + 2) % n_buf)
        if not first:
            scatter(i - 1, (s + 2) % n_buf)
        gu = jnp.dot(_load_row_tiles(xbuf, (s,)).astype(BF16), wgu_b[...], preferred_element_type=F32)
        gate_h, up_h = gu[:, :D_EXPERT], gu[:, D_EXPERT:]
        mid = gate_h * jax.nn.sigmoid(gate_h) * up_h
        obuf[s] = jnp.dot(mid.astype(BF16), wd_b[...], preferred_element_type=F32)

    @pl.when(jnp.logical_and(i < n_act, jnp.logical_or(i == 0, be_ref[i] != be_ref[jnp.maximum(i - 1, 0)])))
    def _():
        wgu_b[...] = wgu_ref[0].astype(BF16)
        wd_b[...] = wd_ref[0].astype(BF16)

    @pl.when(i == 0)
    def _():
        gather(0, 0)
        gather(jnp.minimum(1, n_act - 1), 1)
        block(0, first=True)

    for s in range(n_buf):
        mine = i % n_buf == s

        @pl.when(jnp.logical_and(mine, jnp.logical_and(i >= 1, i < n_act)))
        def _():
            @pl.when(i >= n_buf)
            def _():
                drain_scatter(s)
            block(s, first=False)

        @pl.when(jnp.logical_and(mine, i == n_act - 1))
        def _():
            for other in ((s + 1) % n_buf, (s + 2) % n_buf):
                drain_gather(other)

            @pl.when(i >= 2)
            def _():
                drain_scatter((s + 1) % n_buf)

            @pl.when(i >= 1)
            def _():
                drain_scatter((s + 2) % n_buf)
            scatter(i, s)
            drain_scatter(s)
            obuf[s] = jnp.zeros_like(obuf[s])
            for half in range(2):
                spare = eo_hbm.at[pl.ds(eo_hbm.shape[0] - (2 - half) * ROW_BLOCK, ROW_BLOCK)]
                pltpu.make_async_copy(obuf.at[s], spare, ssem.at[s]).start()
            for half in range(2):
                drain_scatter(s)


def _experts(h2, info, blk_expert, n_active, w_gate_up, w_down):
    n_tok = h2.shape[0]
    n_blk = blk_expert.shape[0]
    grid_spec = pltpu.PrefetchScalarGridSpec(
        num_scalar_prefetch=3,
        grid=(n_blk,),
        in_specs=[pl.BlockSpec(memory_space=pl.ANY),
                  pl.BlockSpec((1, D_MODEL, 2 * D_EXPERT), lambda i, be, na, info: (be[i], 0, 0)),
                  pl.BlockSpec((1, D_EXPERT, D_MODEL), lambda i, be, na, info: (be[i], 0, 0))],
        out_specs=pl.BlockSpec(memory_space=pl.ANY),
        scratch_shapes=[pltpu.VMEM((EXPERT_BUFS, ROW_BLOCK) + ROW_TILE, F32),
                        pltpu.VMEM((EXPERT_BUFS, ROW_BLOCK, D_MODEL), F32),
                        pltpu.VMEM((D_MODEL, 2 * D_EXPERT), BF16), pltpu.VMEM((D_EXPERT, D_MODEL), BF16),
                        pltpu.SemaphoreType.DMA((EXPERT_BUFS,)), pltpu.SemaphoreType.DMA((EXPERT_BUFS,))])
    return pl.pallas_call(
        _expert_kernel,
        grid_spec=grid_spec,
        out_shape=jax.ShapeDtypeStruct((n_tok * TOP_K + 2 * ROW_BLOCK, D_MODEL), F32),
        compiler_params=pltpu.CompilerParams(dimension_semantics=("arbitrary",),
                                             vmem_limit_bytes=VMEM_LIMIT),
        name="experts",
    )(blk_expert, n_active, info, h2, w_gate_up, w_down)


def _ple_kernel(x1_ref, g_ref, p_ref, *rest, last_layer):
    eo_refs, (wpp_ref, gpn_ref, ggi_ref, wpg_ref, gfin_ref, y_ref) = rest[:TOP_K], rest[TOP_K:]
    tm = x1_ref.shape[0]
    chunks = [slice(c * TM_PROJ, (c + 1) * TM_PROJ) for c in range(tm // TM_PROJ)]
    x2 = []
    for c in chunks:
        moe = g_ref[c, 0:1] * eo_refs[0][c, :]
        for k in range(1, TOP_K):
            moe = moe + g_ref[c, k:k + 1] * eo_refs[k][c, :]
        x2.append(x1_ref[c, :] + moe)
    e = [_rms(_bdot(p_ref[c, :], wpp_ref[...]), gpn_ref[...]) for c in chunks]
    gate = [jax.nn.sigmoid(_bdot(_rms(v, ggi_ref[...]), wpg_ref[...])) for v in x2]
    for c, v, g, ee in zip(chunks, x2, gate, e):
        x3 = v + g * ee
        y_ref[c, :] = _rms(x3, gfin_ref[...]) if last_layer else x3


def _ple(x1, gates, expert_out, p2d, w_pp, g_pn, g_gi, w_pg, g_final, last_layer):
    T = x1.shape[0]
    tm = MIX_CHUNKS * TM_PROJ
    nt = T // tm
    row = lambda i: (i, 0)
    fixed = lambda i: (0, 0)
    vec = lambda a: a.reshape(1, D_MODEL)
    slot_rows = [pl.BlockSpec((tm, D_MODEL), functools.partial(lambda i, k: (i + k * nt, 0), k=k))
                 for k in range(TOP_K)]
    return pl.pallas_call(
        functools.partial(_ple_kernel, last_layer=last_layer),
        grid=(nt,),
        in_specs=[pl.BlockSpec((tm, D_MODEL), row), pl.BlockSpec((tm, TOP_K), row),
                  pl.BlockSpec((tm, PLE_DIM), row)] + slot_rows + [
                  pl.BlockSpec((PLE_DIM, D_MODEL), fixed), pl.BlockSpec((1, D_MODEL), fixed),
                  pl.BlockSpec((1, D_MODEL), fixed), pl.BlockSpec((D_MODEL, D_MODEL), fixed),
                  pl.BlockSpec((1, D_MODEL), fixed)],
        out_specs=pl.BlockSpec((tm, D_MODEL), row),
        out_shape=jax.ShapeDtypeStruct((T, D_MODEL), F32),
        compiler_params=pltpu.CompilerParams(dimension_semantics=("parallel",),
                                             vmem_limit_bytes=VMEM_LIMIT),
        name="ple",
    )(x1, gates, p2d, *([expert_out] * TOP_K), w_pp.astype(BF16), vec(g_pn), vec(g_gi),
      w_pg.astype(BF16), vec(g_final))


def kernel(x, p, g_mix, w_in, mu_shift, rw_w0, rw_w2, rw_a0, rw_a2, rw_g2, rw_k_k, rw_k_a, rw_r_k, rw_ln_w, rw_ln_b, cmp_pos_k, cmp_pos_v, cmp_k_w1, cmp_k_w2, cmp_v_w1, cmp_v_w2, nsa_gate_b, w_up_rwkv, w_up_nsa, w_out, g_ffn, w_group, b_group, w_router, b_router, w_exp_gate_up, w_exp_down, w_ple_proj, g_ple_norm, g_ple_gate_in, w_ple_gate, g_final):
    B, S, D = x.shape
    T = B * S
    depth = p.shape[0]
    xc = x.reshape(T, D)
    for i in range(depth):
        zrw, zq, zkv, zmg, zgate = _proj(xc, g_mix[i], w_in[i], mu_shift[i], S)
        y_rw = _rwkv(zrw.reshape(B, S, RW_IN), rw_w0[i], rw_w2[i], rw_a0[i], rw_a2[i], rw_g2[i],
                     rw_k_k[i], rw_k_a[i], rw_r_k[i], rw_ln_w[i], rw_ln_b[i])
        zkv3 = zkv.reshape(B, S, KV_IN)
        nsa_kv = _nsa_prep(zkv3, cmp_pos_k[i], cmp_pos_v[i], cmp_k_w1[i], cmp_k_w2[i],
                           cmp_v_w1[i], cmp_v_w2[i])
        y_nsa = _nsa_attn(zq.reshape(B, S, NSA_WIDTH), *nsa_kv, zgate.reshape(B, S, LANE),
                          nsa_gate_b[i])
        x1, h2, rt, cnt = _mix(xc, y_rw.reshape(T, RW_WIDTH), y_nsa.reshape(T, NSA_WIDTH), zmg,
                               w_up_rwkv[i], w_up_nsa[i], w_out[i], g_ffn[i], w_group[i], b_group[i],
                               w_router[i], b_router[i])
        dest, gates, blk_expert, n_active = _route_tables(rt, cnt, T)
        expert_out = _experts(h2, _row_info(dest, T), blk_expert, n_active, w_exp_gate_up[i],
                              w_exp_down[i])
        xc = _ple(x1, gates, expert_out, p[i].reshape(T, PLE_DIM), w_ple_proj[i], g_ple_norm[i],
                  g_ple_gate_in[i], w_ple_gate[i], g_final, i == depth - 1)
    return xc.reshape(B, S, D)
```

```python
import functools
import math

import jax
import jax.numpy as jnp
import numpy as np
from jax import lax
from jax.experimental import pallas as pl
from jax.experimental.pallas import tpu as pltpu

F32 = jnp.float32
BF16 = jnp.bfloat16
HI = lax.Precision.HIGHEST

D_MODEL = 1024
RW_HEADS = 8
RW_HEAD_DIM = 64
RW_WIDTH = 512
DECAY_LORA = 64
AAA_LORA = 64
GATE_LORA = 128
GN_EPS = 64e-5
RW_IN = 3 * RW_WIDTH + DECAY_LORA + AAA_LORA + GATE_LORA

NSA_HEADS = 8
NSA_KV_HEADS = 2
NSA_REP = NSA_HEADS // NSA_KV_HEADS
NSA_HEAD_DIM = 64
NSA_WIDTH = 512
NSA_KV_WIDTH = 128
CMP_BLOCK = 32
CMP_STRIDE = 16
SEL_BLOCK = 64
N_SELECT = 8
WINDOW = 512
N_NSA_BRANCH = 3
FORCE_SCORE = 1e6
NEG_INF = -1e30

N_GROUPS = 4
EXPERTS_PER_GROUP = 8
N_EXPERTS = 32
TOP_K = 2
D_EXPERT = 512
ROW_BLOCK = 128
PLE_DIM = 256
NORM_EPS = 1e-6

N_GATE = N_NSA_BRANCH * NSA_HEADS
ATT_IN = NSA_WIDTH + 6 * NSA_KV_WIDTH
KV_OFF = RW_IN + NSA_WIDTH
KV_IN = 6 * NSA_KV_WIDTH
GATE_OFF = RW_IN + ATT_IN
MERGE_OFF = GATE_OFF + N_GATE
LANE = 128

RW_CHUNK = 64
RW_SUB = 16
RW_ROWS = 4
TQ = 256
TK = 256
V_ROWS = NSA_HEAD_DIM + 16
TM_PROJ = 256
MIX_CHUNKS = 2
VMEM_LIMIT = 56 * 1024 * 1024


def _bdot(a, b):
    return jnp.dot(a.astype(BF16), b.astype(BF16), preferred_element_type=F32)


def _bdot_nt(a, b):
    return lax.dot_general(a.astype(BF16), b.astype(BF16), (((1,), (1,)), ((), ())),
                           preferred_element_type=F32)


def _rms(x, g):
    return x * lax.rsqrt(jnp.mean(x * x, axis=-1, keepdims=True) + NORM_EPS) * g


def _proj_kernel(x_ref, g_ref, w_ref, mu_ref, zrw_ref, zq_ref, zkv_ref, zmg_ref, zgate_ref,
                 carry_ref, *, tiles_per_seq):
    i = pl.program_id(0)
    tm = x_ref.shape[0]

    @pl.when(i % tiles_per_seq == 0)
    def _():
        carry_ref[...] = jnp.zeros_like(carry_ref)

    h = _rms(x_ref[...], g_ref[...]).astype(BF16)
    z = jnp.dot(h, w_ref[:, 0:RW_IN], preferred_element_type=F32)
    row = lax.broadcasted_iota(jnp.int32, (tm, 1), 0)
    prev = jnp.where(row == 0, carry_ref[7:8, :], pltpu.roll(z, 1, 0))
    carry_ref[...] = z[tm - 8:tm, :]
    zrw_ref[...] = z + (prev - z) * mu_ref[...]
    zq_ref[...] = jnp.dot(h, w_ref[:, RW_IN:KV_OFF], preferred_element_type=F32)
    zkv_ref[...] = jnp.dot(h, w_ref[:, KV_OFF:GATE_OFF], preferred_element_type=F32)
    zmg_ref[...] = jnp.dot(h, w_ref[:, GATE_OFF:GATE_OFF + 2 * D_MODEL],
                           preferred_element_type=F32).astype(zmg_ref.dtype)
    zgate_ref[...] = jnp.dot(h, w_ref[:, GATE_OFF + 2 * D_MODEL:], preferred_element_type=F32)


def _proj(x2d, g_mix, w_in, mu, seq):
    T = x2d.shape[0]
    tm = TM_PROJ
    wp = jnp.concatenate(
        [w_in[:, :GATE_OFF], w_in[:, MERGE_OFF:],
         jnp.pad(w_in[:, GATE_OFF:MERGE_OFF], ((0, 0), (0, LANE - N_GATE)))], axis=1).astype(BF16)
    npad = wp.shape[1]
    row = lambda i: (i, 0)
    fixed = lambda i: (0, 0)
    return pl.pallas_call(
        functools.partial(_proj_kernel, tiles_per_seq=seq // tm),
        grid=(T // tm,),
        in_specs=[pl.BlockSpec((tm, D_MODEL), row), pl.BlockSpec((1, D_MODEL), fixed),
                  pl.BlockSpec((D_MODEL, npad), fixed), pl.BlockSpec((1, RW_IN), fixed)],
        out_specs=[pl.BlockSpec((tm, RW_IN), row), pl.BlockSpec((tm, NSA_WIDTH), row),
                   pl.BlockSpec((tm, KV_IN), row), pl.BlockSpec((tm, 2 * D_MODEL), row),
                   pl.BlockSpec((tm, LANE), row)],
        out_shape=[jax.ShapeDtypeStruct((T, RW_IN), F32), jax.ShapeDtypeStruct((T, NSA_WIDTH), F32),
                   jax.ShapeDtypeStruct((T, KV_IN), F32), jax.ShapeDtypeStruct((T, 2 * D_MODEL), BF16),
                   jax.ShapeDtypeStruct((T, LANE), F32)],
        scratch_shapes=[pltpu.VMEM((8, RW_IN), F32)],
        compiler_params=pltpu.CompilerParams(dimension_semantics=("arbitrary",),
                                             vmem_limit_bytes=VMEM_LIMIT),
        name="proj",
    )(x2d, g_mix.reshape(1, D_MODEL), wp, mu.reshape(1, RW_IN))


PAIR = 2 * RW_HEAD_DIM


def _pair_blocks(x):
    low = lax.broadcasted_iota(jnp.int32, (1, PAIR), 1) < RW_HEAD_DIM
    return jnp.concatenate([jnp.where(low, x, 0.0), jnp.where(low, 0.0, x)], axis=0)


def _pmm(a, b):
    return _bdot(a, _pair_blocks(b))


def _unit_lower_inverse(a_strict, sub_mask, eye):
    ad = [jnp.where(sub_mask, a, 0.0) for a in a_strict]
    ao = [a - d for a, d in zip(a_strict, ad)]
    td = [eye - d for d in ad]
    pw = ad
    for _ in range(int(math.log2(RW_SUB)) - 1):
        pw = [_pmm(x, x) for x in pw]
        td = [_pmm(t, eye + x) for t, x in zip(td, pw)]
    n = [_pmm(t, o) for t, o in zip(td, ao)]
    t = [eye - x for x in n]
    pw = n
    for _ in range(int(math.log2(RW_CHUNK // RW_SUB)) - 1):
        pw = [_pmm(x, x) for x in pw]
        t = [_pmm(a, eye + x) for a, x in zip(t, pw)]
    return [_pmm(a, d) for a, d in zip(t, td)]


def _rwkv_kernel(z_ref, w0_ref, w2_ref, a0_ref, a2_ref, g2_ref, kk_ref, ka_ref, rk_ref, lnw_ref,
                 lnb_ref, avg_ref, o_ref, h_ref):
    c = pl.program_id(1)
    C = RW_CHUNK
    n_pair = RW_WIDTH // PAIR
    nt = (((1,), (1,)), ((), ()))

    @pl.when(c == 0)
    def _():
        h_ref[...] = jnp.zeros_like(h_ref)

    ti = lax.broadcasted_iota(jnp.int32, (C, 1), 0)
    si = lax.broadcasted_iota(jnp.int32, (1, PAIR), 1) & (RW_HEAD_DIM - 1)
    incl, strict = ti >= si, ti > si
    eye = (ti == si).astype(F32)
    sub_shift = int(math.log2(RW_SUB))
    sub_mask = (ti >> sub_shift) == (si >> sub_shift)
    row2 = lax.broadcasted_iota(jnp.int32, (PAIR, 1), 0)
    col2 = lax.broadcasted_iota(jnp.int32, (1, PAIR), 1)
    same_head = (row2 < RW_HEAD_DIM) == (col2 < RW_HEAD_DIM)
    eye2 = row2 == col2
    tri = (lax.broadcasted_iota(jnp.int32, (C, C), 0)
           >= lax.broadcasted_iota(jnp.int32, (C, C), 1)).astype(BF16)

    def head_mean(x):
        xs = jnp.concatenate([x[:, p * PAIR:(p + 1) * PAIR] for p in range(n_pair)], axis=0)
        ms = _bdot(xs, avg_ref[...])
        return jnp.concatenate([ms[p * C:(p + 1) * C] for p in range(n_pair)], axis=1)

    n_rows = z_ref.shape[0]
    rows = []
    for n in range(n_rows):
        z = z_ref[n]
        zr, zk, zv = z[:, 0:512], z[:, 512:1024], z[:, 1024:1536]
        zw, za, zg = z[:, 1536:1600], z[:, 1600:1664], z[:, 1664:1792]
        w_raw = w0_ref[...] + _bdot(jnp.tanh(zw), w2_ref[...])
        logw = -jax.nn.sigmoid(w_raw) * math.exp(-0.5)
        a = jax.nn.sigmoid(a0_ref[...] + _bdot(za, a2_ref[...]))
        gate = _bdot(jax.nn.sigmoid(zg), g2_ref[...])
        kk = zk * kk_ref[...]
        kk = kk / jnp.maximum(jnp.sqrt(head_mean(kk * kk) * RW_HEAD_DIM), 1e-12)
        k = zk * (1.0 + (a - 1.0) * ka_ref[...])
        b = kk * a

        w_hi = logw.astype(BF16)
        w_lo = (logw - w_hi.astype(F32)).astype(BF16)
        cum = (jnp.dot(tri, w_hi, preferred_element_type=F32)
               + jnp.dot(tri, w_lo, preferred_element_type=F32))
        cum_last = cum[C - 1:C, :]
        g_inv = jnp.exp(-cum)
        g_end = jnp.exp(cum_last - cum)
        rows.append(dict(rt=zr * jnp.exp(cum), kt=k * g_inv, bt=b * g_inv, qt=kk * jnp.exp(cum - logw),
                         kh=k * g_end, bh=b * g_end, v=zv, g_last=jnp.exp(cum_last), gate=gate,
                         bonus=head_mean(zr * k * rk_ref[...]) * RW_HEAD_DIM * zv))

    chains = [(n, slice(p * PAIR, (p + 1) * PAIR)) for n in range(n_rows) for p in range(n_pair)]
    part = lambda name: [rows[n][name][:, sl] for n, sl in chains]
    qt, rt, kt, bt, kh, bh, v = (part(x) for x in ("qt", "rt", "kt", "bt", "kh", "bh", "v"))
    lhs = [jnp.concatenate([q, r], axis=0).astype(BF16) for q, r in zip(qt, rt)]
    ab = [lax.dot_general(l, _pair_blocks(x).astype(BF16), nt, preferred_element_type=F32)
          for l, x in zip(lhs, bt)]
    ak = [lax.dot_general(l, _pair_blocks(x).astype(BF16), nt, preferred_element_type=F32)
          for l, x in zip(lhs, kt)]
    a_kb = [jnp.where(strict, x[0:C], 0.0) for x in ab]
    a_rb = [jnp.where(incl, x[C:2 * C], 0.0) for x in ab]
    a_kk = [jnp.where(strict, x[0:C], 0.0) for x in ak]
    a_rk = [jnp.where(incl, x[C:2 * C], 0.0) for x in ak]
    t_inv = _unit_lower_inverse(a_kb, sub_mask, eye)

    h = [h_ref[n, sl.start // PAIR] for n, sl in chains]
    vb = [_pair_blocks(x) for x in v]
    rhs = [_bdot(jnp.concatenate([q, akk], axis=1), jnp.concatenate([hh, vv], axis=0))
           for q, akk, hh, vv in zip(qt, a_kk, h, vb)]
    u = [_pmm(t, x) for t, x in zip(t_inv, rhs)]
    outs = [_bdot(jnp.concatenate([r, ark, -arb], axis=1), jnp.concatenate([hh, vv, _pair_blocks(uu)], axis=0))
            for r, ark, arb, hh, vv, uu in zip(rt, a_rk, a_rb, h, vb, u)]
    upd = [_bdot(jnp.concatenate([x, -y], axis=0).T, jnp.concatenate([vv, uu], axis=0))
           for x, y, vv, uu in zip(kh, bh, v, u)]
    for (n, sl), hh, dd in zip(chains, h, upd):
        decay_col = jnp.sum(jnp.where(eye2, rows[n]["g_last"][:, sl], 0.0), axis=1, keepdims=True)
        h_ref[n, sl.start // PAIR] = decay_col * hh + jnp.where(same_head, dd, 0.0)

    for n in range(n_rows):
        o = jnp.concatenate(outs[n * n_pair:(n + 1) * n_pair], axis=1)
        d = o - head_mean(o)
        on = d * lax.rsqrt(head_mean(d * d) + GN_EPS)
        o_ref[n] = ((on * lnw_ref[...] + lnb_ref[...] + rows[n]["bonus"]) * rows[n]["gate"]).astype(o_ref.dtype)


def _rwkv(zrw, w0, w2, a0, a2, g2, k_k, k_a, r_k, ln_w, ln_b):
    B, S, _ = zrw.shape
    C = RW_CHUNK
    nb = RW_ROWS
    hid = np.arange(PAIR) // RW_HEAD_DIM
    avg = jnp.asarray((hid[:, None] == hid[None, :]).astype(np.float32) / RW_HEAD_DIM)
    vec = lambda a: a.reshape(1, RW_WIDTH)
    fixed = lambda shape: pl.BlockSpec(shape, lambda b, c: (0,) * len(shape))
    return pl.pallas_call(
        _rwkv_kernel,
        grid=(B // nb, S // C),
        in_specs=[pl.BlockSpec((nb, C, RW_IN), lambda b, c: (b, c, 0)),
                  fixed((1, RW_WIDTH)), fixed((DECAY_LORA, RW_WIDTH)),
                  fixed((1, RW_WIDTH)), fixed((AAA_LORA, RW_WIDTH)),
                  fixed((GATE_LORA, RW_WIDTH)), fixed((1, RW_WIDTH)), fixed((1, RW_WIDTH)),
                  fixed((1, RW_WIDTH)), fixed((1, RW_WIDTH)), fixed((1, RW_WIDTH)),
                  fixed((PAIR, PAIR))],
        out_specs=pl.BlockSpec((nb, C, RW_WIDTH), lambda b, c: (b, c, 0)),
        out_shape=jax.ShapeDtypeStruct((B, S, RW_WIDTH), BF16),
        scratch_shapes=[pltpu.VMEM((nb, RW_WIDTH // PAIR, PAIR, PAIR), F32)],
        compiler_params=pltpu.CompilerParams(dimension_semantics=("parallel", "arbitrary"),
                                             vmem_limit_bytes=VMEM_LIMIT),
        name="rwkv",
    )(zrw, vec(w0), w2, vec(a0), a2, g2, vec(k_k), vec(k_a), vec(r_k), vec(ln_w), vec(ln_b), avg)


def _gelu_tanh(x):
    return 0.5 * x * (1.0 + jnp.tanh(math.sqrt(2.0 / math.pi) * (x + 0.044715 * (x * x * x))))


def _key_features(pos_hi, pos_lo, block, n, n_sel):
    lane = lax.broadcasted_iota(jnp.int32, (n, NSA_HEAD_DIM), 1)
    feat = jnp.where(lane == n_sel, pos_hi, jnp.where(lane == n_sel + 1, pos_lo, 0.0))
    return feat if block is None else jnp.where(lane == block, 1.0, feat)


def _nsa_prep_kernel(zkc_ref, zvc_ref, zks_ref, zvs_ref, zkw_ref, zvw_ref, pk_ref, pv_ref, kw1_ref,
                     kw2_ref, vw1_ref, vw2_ref, kc_ref, vct_ref, ksa_ref, kwa_ref, vst_ref, vwt_ref):
    S = zkc_ref.shape[1]
    n_grp = S // CMP_STRIDE
    Dh = NSA_HEAD_DIM
    half = CMP_BLOCK // 2
    n_sel = S // SEL_BLOCK
    jrow = lax.broadcasted_iota(jnp.int32, (n_grp, 1), 0)
    cmp_feat = _key_features((jrow >> 3).astype(F32),
                             ((jrow & 7) * CMP_STRIDE).astype(F32) + 0.5 * (CMP_BLOCK - 1),
                             None, n_grp, n_sel)
    for is_v, (z_ref, pos_ref, w1_ref, w2_ref) in enumerate(((zkc_ref, pk_ref, kw1_ref, kw2_ref),
                                                            (zvc_ref, pv_ref, vw1_ref, vw2_ref))):
        for g in range(NSA_KV_HEADS):
            lo = jnp.zeros((n_grp, Dh), F32)
            hi = jnp.zeros((n_grp, Dh), F32)
            for l in range(half):
                xs = z_ref[0, pl.ds(l, n_grp, stride=CMP_STRIDE), :]
                xg = xs[:, g * Dh:(g + 1) * Dh]
                lo = lo + _bdot(xg + pos_ref[l:l + 1, :], w1_ref[l * Dh:(l + 1) * Dh, :])
                hi = hi + _bdot(xg + pos_ref[half + l:half + l + 1, :],
                                w1_ref[(half + l) * Dh:(half + l + 1) * Dh, :])
            pre = lo + pltpu.roll(hi, n_grp - 1, 0)
            out = jnp.where(jrow < n_grp - 1, _bdot(_gelu_tanh(pre), w2_ref[...]), 0.0)
            if is_v:
                out_t = jnp.concatenate([out, jnp.zeros_like(out)], axis=1).T
                vct_ref[0, g] = out_t[0:Dh, :].astype(BF16)
            else:
                kc_ref[0, g] = jnp.concatenate([out, cmp_feat], axis=1)

    prow = lax.broadcasted_iota(jnp.int32, (S, 1), 0)
    p_hi, p_lo = (prow >> 7).astype(F32), (prow & (LANE - 1)).astype(F32)
    for z_ref, out_ref, block in ((zks_ref, ksa_ref, prow >> int(math.log2(SEL_BLOCK))),
                                  (zkw_ref, kwa_ref, None)):
        kfull = z_ref[0]
        key_feat = _key_features(p_hi, p_lo, block, S, n_sel)
        for g in range(NSA_KV_HEADS):
            out_ref[0, g] = jnp.concatenate([kfull[:, g * Dh:(g + 1) * Dh], key_feat], axis=1).astype(BF16)
    ones_row = (lax.broadcasted_iota(jnp.int32, (V_ROWS - Dh, TK), 0) == 0).astype(F32)
    for z_ref, out_ref in ((zvs_ref, vst_ref), (zvw_ref, vwt_ref)):
        for j in range(S // TK):
            vt = z_ref[0, j * TK:(j + 1) * TK, :].T
            out_ref[0, j] = jnp.concatenate(
                [piece for g in range(NSA_KV_HEADS) for piece in (vt[g * Dh:(g + 1) * Dh], ones_row)],
                axis=0).astype(BF16)


def _nsa_prep(zkv, pos_k, pos_v, kw1, kw2, vw1, vw2):
    B, S, _ = zkv.shape
    n_grp = S // CMP_STRIDE
    n_kt = S // TK
    Dh = NSA_HEAD_DIM
    G = NSA_KV_HEADS
    fixed = lambda shape: pl.BlockSpec(shape, lambda b: (0,) * len(shape))
    col = lambda c: pl.BlockSpec((1, S, NSA_KV_WIDTH), lambda b: (b, 0, c))
    whole = lambda shape: pl.BlockSpec((1,) + shape, lambda b: (b,) + (0,) * len(shape))
    shapes = [((G, n_grp, 2 * Dh), F32), ((G, Dh, n_grp), BF16), ((G, S, 2 * Dh), BF16),
              ((G, S, 2 * Dh), BF16), ((n_kt, G * V_ROWS, TK), BF16), ((n_kt, G * V_ROWS, TK), BF16)]
    return pl.pallas_call(
        _nsa_prep_kernel,
        grid=(B,),
        in_specs=[col(c) for c in range(6)] + [
            fixed((CMP_BLOCK, Dh)), fixed((CMP_BLOCK, Dh)),
            fixed((CMP_BLOCK * Dh, Dh)), fixed((Dh, Dh)),
            fixed((CMP_BLOCK * Dh, Dh)), fixed((Dh, Dh))],
        out_specs=[whole(s) for s, _ in shapes],
        out_shape=[jax.ShapeDtypeStruct((B,) + s, d) for s, d in shapes],
        compiler_params=pltpu.CompilerParams(dimension_semantics=("parallel",),
                                             vmem_limit_bytes=VMEM_LIMIT),
        name="nsa_prep",
    )(zkv, zkv, zkv, zkv, zkv, zkv, pos_k, pos_v, kw1, kw2, vw1, vw2)


def _nsa_attn_kernel(q_ref, kc_ref, vct_ref, ksa_ref, kwa_ref, vst_ref, vwt_ref, gl_ref, gb_ref,
                     ovlt_ref, slope_ref, o_ref, acc_ref, ot_ref):
    i = pl.program_id(1)
    Dh = NSA_HEAD_DIM
    R = NSA_REP
    N = R * TQ
    n_cmp_pad = kc_ref.shape[2]
    n_sel = ovlt_ref.shape[0]
    G = NSA_KV_HEADS
    nt = (((1,), (1,)), ((), ()))
    log2e = math.log2(math.e)
    t0 = i * TQ
    t_row = t0 + lax.broadcasted_iota(jnp.int32, (1, TQ), 1)
    c_col = lax.broadcasted_iota(jnp.int32, (TK, 1), 0)
    sgate_t = jax.nn.sigmoid(gl_ref[0] + gb_ref[...]).T
    lane_f = lax.broadcasted_iota(jnp.int32, (1, Dh), 1)
    heads = lambda x: jnp.concatenate([x] * R, axis=1)

    def key_dist(j):
        return t_row - (j * TK + c_col)

    def queries(g, sel_feat):
        parts = []
        for r in range(R):
            h = g * R + r
            sl = slope_ref[:, h:h + 1] * log2e
            feat = jnp.where(lane_f == n_sel, sl * LANE, jnp.where(lane_f == n_sel + 1, sl, sel_feat))
            parts.append(jnp.concatenate(
                [q_ref[0, :, h * Dh:(h + 1) * Dh] * (Dh ** -0.5 * log2e), jnp.broadcast_to(feat, (TQ, Dh))],
                axis=1))
        return jnp.concatenate(parts, axis=0)

    qab, o_cmp = [], []
    for g in range(G):
        jc = lax.broadcasted_iota(jnp.int32, (n_cmp_pad, 1), 0)
        ok_c = (jc * CMP_STRIDE + (CMP_BLOCK - 1) <= t_row) & (jc < n_cmp_pad - 1)
        s_c = (lax.dot_general(kc_ref[0, g], queries(g, 0.0), nt, precision=HI, preferred_element_type=F32)
               + heads(jnp.where(ok_c, 0.0, NEG_INF)))
        e_c = jnp.exp2(s_c - jnp.max(s_c, axis=0, keepdims=True))
        any_c = heads(t_row >= CMP_BLOCK - 1)
        p_c = e_c * jnp.where(any_c, 1.0 / jnp.sum(e_c, axis=0, keepdims=True), 0.0)
        o_cmp.append(jnp.dot(vct_ref[0, g], p_c.astype(BF16), preferred_element_type=F32))

        p_sum = p_c[:, 0:TQ]
        for r in range(1, R):
            p_sum = p_sum + p_c[:, r * TQ:(r + 1) * TQ]
        imp = jnp.dot(ovlt_ref[...], p_sum, precision=HI, preferred_element_type=F32)
        kb = lax.broadcasted_iota(jnp.int32, (n_sel, 1), 0)
        kbf = kb.astype(F32)
        blk_t = t_row >> int(math.log2(SEL_BLOCK))
        forced = (kb == 0) | (kb == blk_t) | (kb == blk_t - 1)
        cur = jnp.where(forced, FORCE_SCORE, jnp.where(kb <= blk_t, imp, -FORCE_SCORE))
        sel_bias = jnp.full((n_sel, TQ), NEG_INF, F32)
        for _ in range(min(N_SELECT, n_sel)):
            mx = jnp.max(cur, axis=0, keepdims=True)
            first = jnp.min(jnp.where(cur == mx, kbf, float(n_sel)), axis=0, keepdims=True)
            hit = kbf == first
            sel_bias = jnp.where(hit, 0.0, sel_bias)
            cur = jnp.where(hit, -3e38, cur)
        sel_feat = jnp.concatenate([sel_bias, jnp.zeros((LANE - n_sel, TQ), F32)], axis=0).T[:, 0:Dh]
        qab.append(queries(g, sel_feat).astype(BF16))

    VR = vst_ref.shape[2] // G

    def tile(j, m, k_ref, vt_ref, bias, slot):
        ks = [k_ref[0, g, pl.ds(pl.multiple_of(j * TK, TK), TK), :] for g in range(G)]
        vts = [vt_ref[0, j, g * VR:(g + 1) * VR, :] for g in range(G)]
        hs = range(NSA_HEADS)
        s = [lax.dot_general(ks[h // R], qab[h // R][(h % R) * TQ:(h % R + 1) * TQ], nt,
                             preferred_element_type=F32) for h in hs]
        if bias is not None:
            s = [x + bias for x in s]
        m_new = [jnp.maximum(m[h], jnp.max(s[h], axis=0, keepdims=True)) for h in hs]
        alpha = [jnp.exp2(m[h] - m_new[h]) for h in hs]
        p = [jnp.exp2(s[h] - m_new[h]).astype(BF16) for h in hs]
        pv = [jnp.dot(vts[h // R], p[h], preferred_element_type=F32) for h in hs]
        for h in hs:
            acc_ref[slot, h] = alpha[h] * acc_ref[slot, h] + pv[h]
        return tuple(m_new)

    def window_bias(j):
        d = key_dist(j)
        return jnp.where((d >= 0) & (d < WINDOW), 0.0, NEG_INF)

    init = (jnp.full((1, TQ), NEG_INF, F32),) * NSA_HEADS
    acc_ref[...] = jnp.zeros_like(acc_ref)
    causal = jnp.where(key_dist(i) >= 0, 0.0, NEG_INF)
    carry = lax.fori_loop(0, i, lambda j, c: tile(j, c, ksa_ref, vst_ref, None, 0), init)
    tile(i, carry, ksa_ref, vst_ref, causal, 0)
    carry = lax.fori_loop(jnp.maximum(i - WINDOW // TK, 0), i,
                          lambda j, c: tile(j, c, kwa_ref, vwt_ref, window_bias(j), 1), init)
    tile(i, carry, kwa_ref, vwt_ref, causal, 1)

    for h in range(NSA_HEADS):
        g, r = divmod(h, R)
        acc_s, acc_w = acc_ref[0, h], acc_ref[1, h]
        ot_ref[h * Dh:(h + 1) * Dh, :] = (
            sgate_t[3 * h:3 * h + 1, :] * o_cmp[g][:, r * TQ:(r + 1) * TQ]
            + sgate_t[3 * h + 1:3 * h + 2, :] * (acc_s[0:Dh] * (1.0 / acc_s[Dh:Dh + 1]))
            + sgate_t[3 * h + 2:3 * h + 3, :] * (acc_w[0:Dh] * (1.0 / acc_w[Dh:Dh + 1])))
    o_ref[0] = ot_ref[...].T.astype(o_ref.dtype)


def _nsa_attn(zq, kc, vct, ksa, kwa, vst, vwt, zgate, gate_b):
    B, S, _ = zq.shape
    n_sel = S // SEL_BLOCK
    n_cmp = (S - CMP_BLOCK) // CMP_STRIDE + 1
    n_cmp_pad = kc.shape[2]
    n_kt = S // TK
    G, Dh = NSA_KV_HEADS, NSA_HEAD_DIM
    cmp_start = np.arange(n_cmp) * CMP_STRIDE
    sel_start = np.arange(n_sel) * SEL_BLOCK
    overlap = np.clip(np.minimum(cmp_start[:, None] + CMP_BLOCK, sel_start[None, :] + SEL_BLOCK)
                      - np.maximum(cmp_start[:, None], sel_start[None, :]), 0, None) / CMP_BLOCK
    ovlt = np.zeros((n_sel, n_cmp_pad), np.float32)
    ovlt[:, :n_cmp] = overlap.T
    slopes = (2.0 ** (-8.0 * np.arange(1, NSA_HEADS + 1) / NSA_HEADS)).astype(np.float32).reshape(1, NSA_HEADS)
    gb = jnp.pad(gate_b, (0, LANE - N_GATE)).reshape(1, LANE)
    fixed = lambda shape: pl.BlockSpec(shape, lambda b, i: (0,) * len(shape))
    per_b = lambda shape: pl.BlockSpec((1,) + shape, lambda b, i: (b,) + (0,) * len(shape))
    return pl.pallas_call(
        _nsa_attn_kernel,
        grid=(B, S // TQ),
        in_specs=[pl.BlockSpec((1, TQ, NSA_WIDTH), lambda b, i: (b, i, 0)),
                  per_b((G, n_cmp_pad, 2 * Dh)), per_b((G, Dh, n_cmp_pad)),
                  per_b((G, S, 2 * Dh)), per_b((G, S, 2 * Dh)),
                  per_b((n_kt, G * V_ROWS, TK)), per_b((n_kt, G * V_ROWS, TK)),
                  pl.BlockSpec((1, TQ, LANE), lambda b, i: (b, i, 0)),
                  fixed((1, LANE)), fixed((n_sel, n_cmp_pad)), fixed((1, NSA_HEADS))],
        out_specs=pl.BlockSpec((1, TQ, NSA_WIDTH), lambda b, i: (b, i, 0)),
        out_shape=jax.ShapeDtypeStruct((B, S, NSA_WIDTH), BF16),
        scratch_shapes=[pltpu.VMEM((2, NSA_HEADS, V_ROWS, TQ), F32),
                        pltpu.VMEM((NSA_WIDTH, TQ), F32)],
        compiler_params=pltpu.CompilerParams(dimension_semantics=("parallel", "arbitrary"),
                                             vmem_limit_bytes=VMEM_LIMIT),
        name="nsa_attn",
    )(zq, kc, vct, ksa, kwa, vst, vwt, zgate, gb, jnp.asarray(ovlt), jnp.asarray(slopes))


def _mix_kernel(x_ref, yr_ref, yn_ref, zmg_ref, ur_ref, un_ref, wo_ref, gf_ref, wr_ref, br_ref,
                x1_ref, h2_ref, rt_ref, cnt_ref):
    tm = x_ref.shape[0]
    nt = (((1,), (1,)), ((), ()))
    chunks = [slice(c * TM_PROJ, (c + 1) * TM_PROJ) for c in range(tm // TM_PROJ)]
    up_r = [_bdot(yr_ref[c, :], ur_ref[...]) for c in chunks]
    up_n = [_bdot(yn_ref[c, :], un_ref[...]) for c in chunks]
    mixed = [jax.nn.sigmoid(zmg_ref[c, 0:D_MODEL].astype(F32)) * a
             + jax.nn.sigmoid(zmg_ref[c, D_MODEL:2 * D_MODEL].astype(F32)) * b
             for c, a, b in zip(chunks, up_r, up_n)]
    x1 = [x_ref[c, :] + _bdot(m, wo_ref[...]) for c, m in zip(chunks, mixed)]
    h2 = [_rms(v, gf_ref[...]) for v in x1]
    for c, v, h in zip(chunks, x1, h2):
        x1_ref[c, :] = v
        _store_row_tiles(h2_ref.at[c], h)
    n_row = wr_ref.shape[0]
    logits = jnp.concatenate(
        [lax.dot_general(wr_ref[...], h, nt, precision=HI, preferred_element_type=F32) for h in h2],
        axis=1) + br_ref[...]
    row = lax.broadcasted_iota(jnp.int32, (n_row, 1), 0).astype(F32)
    gl = jnp.where(row < N_GROUPS, logits, NEG_INF)
    gmax = jnp.max(gl, axis=0, keepdims=True)
    g_sel = jnp.min(jnp.where(gl == gmax, row, float(n_row)), axis=0, keepdims=True)
    p_group = 1.0 / jnp.sum(jnp.exp(gl - gmax), axis=0, keepdims=True)
    e_row = row - N_GROUPS
    in_grp = ((e_row >= g_sel * EXPERTS_PER_GROUP) & (e_row < (g_sel + 1.0) * EXPERTS_PER_GROUP)
              & (e_row < N_EXPERTS))
    el = jnp.where(in_grp, logits, NEG_INF)
    m1 = jnp.max(el, axis=0, keepdims=True)
    i1 = jnp.min(jnp.where(el == m1, e_row, float(n_row)), axis=0, keepdims=True)
    el2 = jnp.where(e_row == i1, 2.0 * NEG_INF, el)
    m2 = jnp.max(el2, axis=0, keepdims=True)
    i2 = jnp.min(jnp.where(el2 == m2, e_row, float(n_row)), axis=0, keepdims=True)
    r2 = jnp.exp(m2 - m1)
    g1 = p_group / (1.0 + r2)
    g2 = p_group * r2 / (1.0 + r2)

    @pl.when(pl.program_id(0) == 0)
    def _():
        cnt_ref[...] = jnp.zeros_like(cnt_ref)

    pick1, pick2 = e_row == i1, e_row == i2
    both = pick1.astype(F32) + pick2.astype(F32)
    earlier = (lax.broadcasted_iota(jnp.int32, (tm, tm), 0)
               < lax.broadcasted_iota(jnp.int32, (tm, tm), 1)).astype(BF16)
    before = jnp.dot(both.astype(BF16), earlier, preferred_element_type=F32) + cnt_ref[:, 0:1]
    rank1 = jnp.sum(jnp.where(pick1, before, 0.0), axis=0, keepdims=True)
    rank2 = jnp.sum(jnp.where(pick2, before, 0.0), axis=0, keepdims=True)
    cnt_ref[...] = cnt_ref[...] + jnp.sum(both, axis=1, keepdims=True)
    rt_ref[...] = jnp.concatenate([i1, i2, g1, g2, rank1, rank2, jnp.zeros((2, tm), F32)], axis=0)


def _mix(x2d, y_rw, y_nsa, zmg, w_up_r, w_up_n, w_out, g_ffn, w_group, b_group, w_router, b_router):
    T = x2d.shape[0]
    tm = MIX_CHUNKS * TM_PROJ
    n_r = N_GROUPS + N_EXPERTS
    n_row = -(-n_r // 8) * 8
    wr = jnp.pad(jnp.concatenate([w_group, w_router], axis=1).T, ((0, n_row - n_r), (0, 0)))
    br = jnp.pad(jnp.concatenate([b_group, b_router]), (0, n_row - n_r)).reshape(n_row, 1)
    row = lambda i: (i, 0)
    fixed = lambda i: (0, 0)
    return pl.pallas_call(
        _mix_kernel,
        grid=(T // tm,),
        in_specs=[pl.BlockSpec((tm, D_MODEL), row), pl.BlockSpec((tm, RW_WIDTH), row),
                  pl.BlockSpec((tm, NSA_WIDTH), row), pl.BlockSpec((tm, 2 * D_MODEL), row),
                  pl.BlockSpec((RW_WIDTH, D_MODEL), fixed), pl.BlockSpec((NSA_WIDTH, D_MODEL), fixed),
                  pl.BlockSpec((D_MODEL, D_MODEL), fixed), pl.BlockSpec((1, D_MODEL), fixed),
                  pl.BlockSpec((n_row, D_MODEL), fixed), pl.BlockSpec((n_row, 1), fixed)],
        out_specs=[pl.BlockSpec((tm, D_MODEL), row), pl.BlockSpec((tm,) + ROW_TILE, lambda i: (i, 0, 0)),
                   pl.BlockSpec((8, tm), lambda i: (0, i)), pl.BlockSpec((n_row, LANE), fixed)],
        out_shape=[jax.ShapeDtypeStruct((T, D_MODEL), F32), jax.ShapeDtypeStruct((T,) + ROW_TILE, F32),
                   jax.ShapeDtypeStruct((8, T), F32), jax.ShapeDtypeStruct((n_row, LANE), F32)],
        compiler_params=pltpu.CompilerParams(dimension_semantics=("arbitrary",),
                                             vmem_limit_bytes=VMEM_LIMIT),
        name="mix",
    )(x2d, y_rw, y_nsa, zmg, w_up_r.astype(BF16), w_up_n.astype(BF16), w_out.astype(BF16),
      g_ffn.reshape(1, D_MODEL), wr, br)


def _route_tables(rt, cnt, T):
    n_rows = T * TOP_K + N_EXPERTS * ROW_BLOCK
    n_blk = n_rows // ROW_BLOCK
    counts = cnt[N_GROUPS:N_GROUPS + N_EXPERTS, 0].astype(jnp.int32)
    padded = (counts + ROW_BLOCK - 1) // ROW_BLOCK * ROW_BLOCK
    pends = jnp.cumsum(padded)
    pstarts = pends - padded
    expert = rt[0:TOP_K].astype(jnp.int32)
    rank = rt[2 * TOP_K:3 * TOP_K].astype(jnp.int32)
    seg_start = jnp.sum(jnp.where(expert[..., None] == jnp.arange(N_EXPERTS), pstarts, 0), axis=-1)
    dest = (seg_start + rank).T.reshape(T * TOP_K)
    gates = rt[TOP_K:2 * TOP_K].T
    blk_start = jnp.arange(n_blk) * ROW_BLOCK
    blk_expert = jnp.minimum(jnp.sum(pends[None, :] <= blk_start[:, None], axis=1), N_EXPERTS - 1)
    n_active = (pends[N_EXPERTS - 1:] // ROW_BLOCK).astype(jnp.int32)
    return dest.astype(jnp.int32), gates, blk_expert.astype(jnp.int32), n_active


ROW_TILE = (D_MODEL // LANE, LANE)


def _store_row_tiles(ref, x):
    for c in range(ROW_TILE[0]):
        ref[:, c, :] = x[:, c * LANE:(c + 1) * LANE]


def _load_row_tiles(ref, idx):
    return jnp.concatenate([ref[(*idx, slice(None), c, slice(None))] for c in range(ROW_TILE[0])], axis=1)


EXPERT_BUFS = 3
TOK_BITS = 14
SLOT_SHIFT = int(math.log2(TOP_K))


def _row_info_kernel(dest_ref, word_ref, fill_hbm, info_ref, sem):
    fill = pltpu.make_async_copy(fill_hbm, info_ref, sem)
    fill.start()
    fill.wait()

    def body(a, carry):
        info_ref[dest_ref[a]] = word_ref[a]
        return carry

    lax.fori_loop(0, dest_ref.shape[0], body, 0, unroll=16)


def _row_info(dest, n_tok):
    n_rows = n_tok * TOP_K + N_EXPERTS * ROW_BLOCK
    assert n_tok <= 1 << TOK_BITS and (n_tok * TOP_K + 2 * ROW_BLOCK) << TOK_BITS < 2 ** 31
    a = jnp.arange(n_tok * TOP_K, dtype=jnp.int32)
    tok, k = a // TOP_K, a % TOP_K
    word = tok | ((k * n_tok + tok) << TOK_BITS)
    row = jnp.arange(n_rows, dtype=jnp.int32)
    spare = n_tok * TOP_K + ((row // ROW_BLOCK) % 2) * ROW_BLOCK + row % ROW_BLOCK
    return pl.pallas_call(
        _row_info_kernel,
        in_specs=[pl.BlockSpec(memory_space=pltpu.SMEM), pl.BlockSpec(memory_space=pltpu.SMEM),
                  pl.BlockSpec(memory_space=pl.ANY)],
        out_specs=pl.BlockSpec(memory_space=pltpu.SMEM),
        out_shape=jax.ShapeDtypeStruct((n_rows,), jnp.int32),
        scratch_shapes=[pltpu.SemaphoreType.DMA(())],
        name="row_info",
    )(dest, word, spare << TOK_BITS)


def _expert_kernel(be_ref, nact_ref, info_ref, h2_hbm, wgu_ref, wd_ref, eo_hbm,
                   xbuf, obuf, wgu_b, wd_b, gsem, ssem):
    i = pl.program_id(0)
    n_act = nact_ref[0]
    tok_mask = (1 << TOK_BITS) - 1

    def gather(blk, s):
        for r in range(ROW_BLOCK):
            tok = info_ref[blk * ROW_BLOCK + r] & tok_mask
            pltpu.make_async_copy(h2_hbm.at[tok], xbuf.at[s, r], gsem.at[s]).start()

    def scatter(blk, s):
        for r in range(ROW_BLOCK):
            row = info_ref[blk * ROW_BLOCK + r] >> TOK_BITS
            pltpu.make_async_copy(obuf.at[s, pl.ds(r, 1)], eo_hbm.at[pl.ds(row, 1)], ssem.at[s]).start()

    def drain_gather(s):
        pltpu.make_async_copy(h2_hbm.at[pl.ds(0, ROW_BLOCK)], xbuf.at[s], gsem.at[s]).wait()

    def drain_scatter(s):
        pltpu.make_async_copy(obuf.at[s], eo_hbm.at[pl.ds(0, ROW_BLOCK)], ssem.at[s]).wait()

    n_buf = xbuf.shape[0]

    def block(s, first):
        drain_gather(s)
        gather(jnp.minimum(i + 2, n_act - 1), (s + 2) % n_buf)
        if not first:
            scatter(i - 1, (s + 2) % n_buf)
        gu = jnp.dot(_load_row_tiles(xbuf, (s,)).astype(BF16), wgu_b[...], preferred_element_type=F32)
        gate_h, up_h = gu[:, :D_EXPERT], gu[:, D_EXPERT:]
        mid = gate_h * jax.nn.sigmoid(gate_h) * up_h
        obuf[s] = jnp.dot(mid.astype(BF16), wd_b[...], preferred_element_type=F32)

    @pl.when(jnp.logical_and(i < n_act, jnp.logical_or(i == 0, be_ref[i] != be_ref[jnp.maximum(i - 1, 0)])))
    def _():
        wgu_b[...] = wgu_ref[0].astype(BF16)
        wd_b[...] = wd_ref[0].astype(BF16)

    @pl.when(i == 0)
    def _():
        gather(0, 0)
        gather(jnp.minimum(1, n_act - 1), 1)
        block(0, first=True)

    for s in range(n_buf):
        mine = i % n_buf == s

        @pl.when(jnp.logical_and(mine, jnp.logical_and(i >= 1, i < n_act)))
        def _():
            @pl.when(i >= n_buf)
            def _():
                drain_scatter(s)
            block(s, first=False)

        @pl.when(jnp.logical_and(mine, i == n_act - 1))
        def _():
            for other in ((s + 1) % n_buf, (s + 2) % n_buf):
                drain_gather(other)

            @pl.when(i >= 2)
            def _():
                drain_scatter((s + 1) % n_buf)

            @pl.when(i >= 1)
            def _():
                drain_scatter((s + 2) % n_buf)
            scatter(i, s)
            drain_scatter(s)
            obuf[s] = jnp.zeros_like(obuf[s])
            for half in range(2):
                spare = eo_hbm.at[pl.ds(eo_hbm.shape[0] - (2 - half) * ROW_BLOCK, ROW_BLOCK)]
                pltpu.make_async_copy(obuf.at[s], spare, ssem.at[s]).start()
            for half in range(2):
                drain_scatter(s)


def _experts(h2, info, blk_expert, n_active, w_gate_up, w_down):
    n_tok = h2.shape[0]
    n_blk = blk_expert.shape[0]
    grid_spec = pltpu.PrefetchScalarGridSpec(
        num_scalar_prefetch=3,
        grid=(n_blk,),
        in_specs=[pl.BlockSpec(memory_space=pl.ANY),
                  pl.BlockSpec((1, D_MODEL, 2 * D_EXPERT), lambda i, be, na, info: (be[i], 0, 0)),
                  pl.BlockSpec((1, D_EXPERT, D_MODEL), lambda i, be, na, info: (be[i], 0, 0))],
        out_specs=pl.BlockSpec(memory_space=pl.ANY),
        scratch_shapes=[pltpu.VMEM((EXPERT_BUFS, ROW_BLOCK) + ROW_TILE, F32),
                        pltpu.VMEM((EXPERT_BUFS, ROW_BLOCK, D_MODEL), F32),
                        pltpu.VMEM((D_MODEL, 2 * D_EXPERT), BF16), pltpu.VMEM((D_EXPERT, D_MODEL), BF16),
                        pltpu.SemaphoreType.DMA((EXPERT_BUFS,)), pltpu.SemaphoreType.DMA((EXPERT_BUFS,))])
    return pl.pallas_call(
        _expert_kernel,
        grid_spec=grid_spec,
        out_shape=jax.ShapeDtypeStruct((n_tok * TOP_K + 2 * ROW_BLOCK, D_MODEL), F32),
        compiler_params=pltpu.CompilerParams(dimension_semantics=("arbitrary",),
                                             vmem_limit_bytes=VMEM_LIMIT),
        name="experts",
    )(blk_expert, n_active, info, h2, w_gate_up, w_down)


def _ple_kernel(x1_ref, g_ref, p_ref, *rest, last_layer):
    eo_refs, (wpp_ref, gpn_ref, ggi_ref, wpg_ref, gfin_ref, y_ref) = rest[:TOP_K], rest[TOP_K:]
    tm = x1_ref.shape[0]
    chunks = [slice(c * TM_PROJ, (c + 1) * TM_PROJ) for c in range(tm // TM_PROJ)]
    x2 = []
    for c in chunks:
        moe = g_ref[c, 0:1] * eo_refs[0][c, :]
        for k in range(1, TOP_K):
            moe = moe + g_ref[c, k:k + 1] * eo_refs[k][c, :]
        x2.append(x1_ref[c, :] + moe)
    e = [_rms(_bdot(p_ref[c, :], wpp_ref[...]), gpn_ref[...]) for c in chunks]
    gate = [jax.nn.sigmoid(_bdot(_rms(v, ggi_ref[...]), wpg_ref[...])) for v in x2]
    for c, v, g, ee in zip(chunks, x2, gate, e):
        x3 = v + g * ee
        y_ref[c, :] = _rms(x3, gfin_ref[...]) if last_layer else x3


def _ple(x1, gates, expert_out, p2d, w_pp, g_pn, g_gi, w_pg, g_final, last_layer):
    T = x1.shape[0]
    tm = MIX_CHUNKS * TM_PROJ
    nt = T // tm
    row = lambda i: (i, 0)
    fixed = lambda i: (0, 0)
    vec = lambda a: a.reshape(1, D_MODEL)
    slot_rows = [pl.BlockSpec((tm, D_MODEL), functools.partial(lambda i, k: (i + k * nt, 0), k=k))
                 for k in range(TOP_K)]
    return pl.pallas_call(
        functools.partial(_ple_kernel, last_layer=last_layer),
        grid=(nt,),
        in_specs=[pl.BlockSpec((tm, D_MODEL), row), pl.BlockSpec((tm, TOP_K), row),
                  pl.BlockSpec((tm, PLE_DIM), row)] + slot_rows + [
                  pl.BlockSpec((PLE_DIM, D_MODEL), fixed), pl.BlockSpec((1, D_MODEL), fixed),
                  pl.BlockSpec((1, D_MODEL), fixed), pl.BlockSpec((D_MODEL, D_MODEL), fixed),
                  pl.BlockSpec((1, D_MODEL), fixed)],
        out_specs=pl.BlockSpec((tm, D_MODEL), row),
        out_shape=jax.ShapeDtypeStruct((T, D_MODEL), F32),
        compiler_params=pltpu.CompilerParams(dimension_semantics=("parallel",),
                                             vmem_limit_bytes=VMEM_LIMIT),
        name="ple",
    )(x1, gates, p2d, *([expert_out] * TOP_K), w_pp.astype(BF16), vec(g_pn), vec(g_gi),
      w_pg.astype(BF16), vec(g_final))


def kernel(x, p, g_mix, w_in, mu_shift, rw_w0, rw_w2, rw_a0, rw_a2, rw_g2, rw_k_k, rw_k_a, rw_r_k, rw_ln_w, rw_ln_b, cmp_pos_k, cmp_pos_v, cmp_k_w1, cmp_k_w2, cmp_v_w1, cmp_v_w2, nsa_gate_b, w_up_rwkv, w_up_nsa, w_out, g_ffn, w_group, b_group, w_router, b_router, w_exp_gate_up, w_exp_down, w_ple_proj, g_ple_norm, g_ple_gate_in, w_ple_gate, g_final):
    B, S, D = x.shape
    T = B * S
    depth = p.shape[0]
    xc = x.reshape(T, D)
    for i in range(depth):
        zrw, zq, zkv, zmg, zgate = _proj(xc, g_mix[i], w_in[i], mu_shift[i], S)
        y_rw = _rwkv(zrw.reshape(B, S, RW_IN), rw_w0[i], rw_w2[i], rw_a0[i], rw_a2[i], rw_g2[i],
                     rw_k_k[i], rw_k_a[i], rw_r_k[i], rw_ln_w[i], rw_ln_b[i])
        zkv3 = zkv.reshape(B, S, KV_IN)
        nsa_kv = _nsa_prep(zkv3, cmp_pos_k[i], cmp_pos_v[i], cmp_k_w1[i], cmp_k_w2[i],
                           cmp_v_w1[i], cmp_v_w2[i])
        y_nsa = _nsa_attn(zq.reshape(B, S, NSA_WIDTH), *nsa_kv, zgate.reshape(B, S, LANE),
                          nsa_gate_b[i])
        x1, h2, rt, cnt = _mix(xc, y_rw.reshape(T, RW_WIDTH), y_nsa.reshape(T, NSA_WIDTH), zmg,
                               w_up_rwkv[i], w_up_nsa[i], w_out[i], g_ffn[i], w_group[i], b_group[i],
                               w_router[i], b_router[i])
        dest, gates, blk_expert, n_active = _route_tables(rt, cnt, T)
        expert_out = _experts(h2, _row_info(dest, T), blk_expert, n_active, w_exp_gate_up[i],
                              w_exp_down[i])
        xc = _ple(x1, gates, expert_out, p[i].reshape(T, PLE_DIM), w_ple_proj[i], g_ple_norm[i],
                  g_ple_gate_in[i], w_ple_gate[i], g_final, i == depth - 1)
    return xc.reshape(B, S, D)
```

```python
import functools
import math

import jax
import jax.numpy as jnp
import numpy as np
from jax import lax
from jax.experimental import pallas as pl
from jax.experimental.pallas import tpu as pltpu

F32 = jnp.float32
BF16 = jnp.bfloat16
HI = lax.Precision.HIGHEST

D_MODEL = 1024
RW_HEADS = 8
RW_HEAD_DIM = 64
RW_WIDTH = 512
DECAY_LORA = 64
AAA_LORA = 64
GATE_LORA = 128
GN_EPS = 64e-5
RW_IN = 3 * RW_WIDTH + DECAY_LORA + AAA_LORA + GATE_LORA

NSA_HEADS = 8
NSA_KV_HEADS = 2
NSA_REP = NSA_HEADS // NSA_KV_HEADS
NSA_HEAD_DIM = 64
NSA_WIDTH = 512
NSA_KV_WIDTH = 128
CMP_BLOCK = 32
CMP_STRIDE = 16
SEL_BLOCK = 64
N_SELECT = 8
WINDOW = 512
N_NSA_BRANCH = 3
FORCE_SCORE = 1e6
NEG_INF = -1e30

N_GROUPS = 4
EXPERTS_PER_GROUP = 8
N_EXPERTS = 32
TOP_K = 2
D_EXPERT = 512
ROW_BLOCK = 128
PLE_DIM = 256
NORM_EPS = 1e-6

N_GATE = N_NSA_BRANCH * NSA_HEADS
ATT_IN = NSA_WIDTH + 6 * NSA_KV_WIDTH
KV_OFF = RW_IN + NSA_WIDTH
KV_IN = 6 * NSA_KV_WIDTH
GATE_OFF = RW_IN + ATT_IN
MERGE_OFF = GATE_OFF + N_GATE
LANE = 128

RW_CHUNK = 64
RW_SUB = 16
RW_ROWS = 4
TQ = 256
TK = 256
V_ROWS = NSA_HEAD_DIM + 16
TM_PROJ = 256
MIX_CHUNKS = 2
VMEM_LIMIT = 56 * 1024 * 1024


def _bdot(a, b):
    return jnp.dot(a.astype(BF16), b.astype(BF16), preferred_element_type=F32)


def _bdot_nt(a, b):
    return lax.dot_general(a.astype(BF16), b.astype(BF16), (((1,), (1,)), ((), ())),
                           preferred_element_type=F32)


def _rms(x, g):
    return x * lax.rsqrt(jnp.mean(x * x, axis=-1, keepdims=True) + NORM_EPS) * g


def _proj_kernel(x_ref, g_ref, w_ref, mu_ref, zrw_ref, zq_ref, zkv_ref, zmg_ref, zgate_ref,
                 carry_ref, *, tiles_per_seq):
    i = pl.program_id(0)
    tm = x_ref.shape[0]

    @pl.when(i % tiles_per_seq == 0)
    def _():
        carry_ref[...] = jnp.zeros_like(carry_ref)

    h = _rms(x_ref[...], g_ref[...]).astype(BF16)
    z = jnp.dot(h, w_ref[:, 0:RW_IN], preferred_element_type=F32)
    row = lax.broadcasted_iota(jnp.int32, (tm, 1), 0)
    prev = jnp.where(row == 0, carry_ref[7:8, :], pltpu.roll(z, 1, 0))
    carry_ref[...] = z[tm - 8:tm, :]
    zrw_ref[...] = z + (prev - z) * mu_ref[...]
    zq_ref[...] = jnp.dot(h, w_ref[:, RW_IN:KV_OFF], preferred_element_type=F32)
    zkv_ref[...] = jnp.dot(h, w_ref[:, KV_OFF:GATE_OFF], preferred_element_type=F32)
    zmg_ref[...] = jnp.dot(h, w_ref[:, GATE_OFF:GATE_OFF + 2 * D_MODEL],
                           preferred_element_type=F32).astype(zmg_ref.dtype)
    zgate_ref[...] = jnp.dot(h, w_ref[:, GATE_OFF + 2 * D_MODEL:], preferred_element_type=F32)


def _proj(x2d, g_mix, w_in, mu, seq):
    T = x2d.shape[0]
    tm = TM_PROJ
    wp = jnp.concatenate(
        [w_in[:, :GATE_OFF], w_in[:, MERGE_OFF:],
         jnp.pad(w_in[:, GATE_OFF:MERGE_OFF], ((0, 0), (0, LANE - N_GATE)))], axis=1).astype(BF16)
    npad = wp.shape[1]
    row = lambda i: (i, 0)
    fixed = lambda i: (0, 0)
    return pl.pallas_call(
        functools.partial(_proj_kernel, tiles_per_seq=seq // tm),
        grid=(T // tm,),
        in_specs=[pl.BlockSpec((tm, D_MODEL), row), pl.BlockSpec((1, D_MODEL), fixed),
                  pl.BlockSpec((D_MODEL, npad), fixed), pl.BlockSpec((1, RW_IN), fixed)],
        out_specs=[pl.BlockSpec((tm, RW_IN), row), pl.BlockSpec((tm, NSA_WIDTH), row),
                   pl.BlockSpec((tm, KV_IN), row), pl.BlockSpec((tm, 2 * D_MODEL), row),
                   pl.BlockSpec((tm, LANE), row)],
        out_shape=[jax.ShapeDtypeStruct((T, RW_IN), F32), jax.ShapeDtypeStruct((T, NSA_WIDTH), F32),
                   jax.ShapeDtypeStruct((T, KV_IN), F32), jax.ShapeDtypeStruct((T, 2 * D_MODEL), BF16),
                   jax.ShapeDtypeStruct((T, LANE), F32)],
        scratch_shapes=[pltpu.VMEM((8, RW_IN), F32)],
        compiler_params=pltpu.CompilerParams(dimension_semantics=("arbitrary",),
                                             vmem_limit_bytes=VMEM_LIMIT),
        name="proj",
    )(x2d, g_mix.reshape(1, D_MODEL), wp, mu.reshape(1, RW_IN))


PAIR = 2 * RW_HEAD_DIM


def _pair_blocks(x):
    low = lax.broadcasted_iota(jnp.int32, (1, PAIR), 1) < RW_HEAD_DIM
    return jnp.concatenate([jnp.where(low, x, 0.0), jnp.where(low, 0.0, x)], axis=0)


def _pmm(a, b):
    return _bdot(a, _pair_blocks(b))


def _unit_lower_inverse(a_strict, sub_mask, eye):
    ad = [jnp.where(sub_mask, a, 0.0) for a in a_strict]
    ao = [a - d for a, d in zip(a_strict, ad)]
    td = [eye - d for d in ad]
    pw = ad
    for _ in range(int(math.log2(RW_SUB)) - 1):
        pw = [_pmm(x, x) for x in pw]
        td = [_pmm(t, eye + x) for t, x in zip(td, pw)]
    n = [_pmm(t, o) for t, o in zip(td, ao)]
    t = [eye - x for x in n]
    pw = n
    for _ in range(int(math.log2(RW_CHUNK // RW_SUB)) - 1):
        pw = [_pmm(x, x) for x in pw]
        t = [_pmm(a, eye + x) for a, x in zip(t, pw)]
    return [_pmm(a, d) for a, d in zip(t, td)]


def _rwkv_kernel(z_ref, w0_ref, w2_ref, a0_ref, a2_ref, g2_ref, kk_ref, ka_ref, rk_ref, lnw_ref,
                 lnb_ref, avg_ref, o_ref, h_ref):
    c = pl.program_id(1)
    C = RW_CHUNK
    n_pair = RW_WIDTH // PAIR
    nt = (((1,), (1,)), ((), ()))

    @pl.when(c == 0)
    def _():
        h_ref[...] = jnp.zeros_like(h_ref)

    ti = lax.broadcasted_iota(jnp.int32, (C, 1), 0)
    si = lax.broadcasted_iota(jnp.int32, (1, PAIR), 1) & (RW_HEAD_DIM - 1)
    incl, strict = ti >= si, ti > si
    eye = (ti == si).astype(F32)
    sub_shift = int(math.log2(RW_SUB))
    sub_mask = (ti >> sub_shift) == (si >> sub_shift)
    row2 = lax.broadcasted_iota(jnp.int32, (PAIR, 1), 0)
    col2 = lax.broadcasted_iota(jnp.int32, (1, PAIR), 1)
    same_head = (row2 < RW_HEAD_DIM) == (col2 < RW_HEAD_DIM)
    eye2 = row2 == col2
    tri = (lax.broadcasted_iota(jnp.int32, (C, C), 0)
           >= lax.broadcasted_iota(jnp.int32, (C, C), 1)).astype(BF16)

    def head_mean(x):
        xs = jnp.concatenate([x[:, p * PAIR:(p + 1) * PAIR] for p in range(n_pair)], axis=0)
        ms = _bdot(xs, avg_ref[...])
        return jnp.concatenate([ms[p * C:(p + 1) * C] for p in range(n_pair)], axis=1)

    n_rows = z_ref.shape[0]
    rows = []
    for n in range(n_rows):
        z = z_ref[n]
        zr, zk, zv = z[:, 0:512], z[:, 512:1024], z[:, 1024:1536]
        zw, za, zg = z[:, 1536:1600], z[:, 1600:1664], z[:, 1664:1792]
        w_raw = w0_ref[...] + _bdot(jnp.tanh(zw), w2_ref[...])
        logw = -jax.nn.sigmoid(w_raw) * math.exp(-0.5)
        a = jax.nn.sigmoid(a0_ref[...] + _bdot(za, a2_ref[...]))
        gate = _bdot(jax.nn.sigmoid(zg), g2_ref[...])
        kk = zk * kk_ref[...]
        kk = kk / jnp.maximum(jnp.sqrt(head_mean(kk * kk) * RW_HEAD_DIM), 1e-12)
        k = zk * (1.0 + (a - 1.0) * ka_ref[...])
        b = kk * a

        w_hi = logw.astype(BF16)
        w_lo = (logw - w_hi.astype(F32)).astype(BF16)
        cum = (jnp.dot(tri, w_hi, preferred_element_type=F32)
               + jnp.dot(tri, w_lo, preferred_element_type=F32))
        cum_last = cum[C - 1:C, :]
        g_inv = jnp.exp(-cum)
        g_end = jnp.exp(cum_last - cum)
        rows.append(dict(rt=zr * jnp.exp(cum), kt=k * g_inv, bt=b * g_inv, qt=kk * jnp.exp(cum - logw),
                         kh=k * g_end, bh=b * g_end, v=zv, g_last=jnp.exp(cum_last), gate=gate,
                         bonus=head_mean(zr * k * rk_ref[...]) * RW_HEAD_DIM * zv))

    chains = [(n, slice(p * PAIR, (p + 1) * PAIR)) for n in range(n_rows) for p in range(n_pair)]
    part = lambda name: [rows[n][name][:, sl] for n, sl in chains]
    qt, rt, kt, bt, kh, bh, v = (part(x) for x in ("qt", "rt", "kt", "bt", "kh", "bh", "v"))
    lhs = [jnp.concatenate([q, r], axis=0).astype(BF16) for q, r in zip(qt, rt)]
    ab = [lax.dot_general(l, _pair_blocks(x).astype(BF16), nt, preferred_element_type=F32)
          for l, x in zip(lhs, bt)]
    ak = [lax.dot_general(l, _pair_blocks(x).astype(BF16), nt, preferred_element_type=F32)
          for l, x in zip(lhs, kt)]
    a_kb = [jnp.where(strict, x[0:C], 0.0) for x in ab]
    a_rb = [jnp.where(incl, x[C:2 * C], 0.0) for x in ab]
    a_kk = [jnp.where(strict, x[0:C], 0.0) for x in ak]
    a_rk = [jnp.where(incl, x[C:2 * C], 0.0) for x in ak]
    t_inv = _unit_lower_inverse(a_kb, sub_mask, eye)

    h = [h_ref[n, sl.start // PAIR] for n, sl in chains]
    vb = [_pair_blocks(x) for x in v]
    rhs = [_bdot(jnp.concatenate([q, akk], axis=1), jnp.concatenate([hh, vv], axis=0))
           for q, akk, hh, vv in zip(qt, a_kk, h, vb)]
    u = [_pmm(t, x) for t, x in zip(t_inv, rhs)]
    outs = [_bdot(jnp.concatenate([r, ark, -arb], axis=1), jnp.concatenate([hh, vv, _pair_blocks(uu)], axis=0))
            for r, ark, arb, hh, vv, uu in zip(rt, a_rk, a_rb, h, vb, u)]
    upd = [_bdot(jnp.concatenate([x, -y], axis=0).T, jnp.concatenate([vv, uu], axis=0))
           for x, y, vv, uu in zip(kh, bh, v, u)]
    for (n, sl), hh, dd in zip(chains, h, upd):
        decay_col = jnp.sum(jnp.where(eye2, rows[n]["g_last"][:, sl], 0.0), axis=1, keepdims=True)
        h_ref[n, sl.start // PAIR] = decay_col * hh + jnp.where(same_head, dd, 0.0)

    for n in range(n_rows):
        o = jnp.concatenate(outs[n * n_pair:(n + 1) * n_pair], axis=1)
        d = o - head_mean(o)
        on = d * lax.rsqrt(head_mean(d * d) + GN_EPS)
        o_ref[n] = ((on * lnw_ref[...] + lnb_ref[...] + rows[n]["bonus"]) * rows[n]["gate"]).astype(o_ref.dtype)


def _rwkv(zrw, w0, w2, a0, a2, g2, k_k, k_a, r_k, ln_w, ln_b):
    B, S, _ = zrw.shape
    C = RW_CHUNK
    nb = RW_ROWS
    hid = np.arange(PAIR) // RW_HEAD_DIM
    avg = jnp.asarray((hid[:, None] == hid[None, :]).astype(np.float32) / RW_HEAD_DIM)
    vec = lambda a: a.reshape(1, RW_WIDTH)
    fixed = lambda shape: pl.BlockSpec(shape, lambda b, c: (0,) * len(shape))
    return pl.pallas_call(
        _rwkv_kernel,
        grid=(B // nb, S // C),
        in_specs=[pl.BlockSpec((nb, C, RW_IN), lambda b, c: (b, c, 0)),
                  fixed((1, RW_WIDTH)), fixed((DECAY_LORA, RW_WIDTH)),
                  fixed((1, RW_WIDTH)), fixed((AAA_LORA, RW_WIDTH)),
                  fixed((GATE_LORA, RW_WIDTH)), fixed((1, RW_WIDTH)), fixed((1, RW_WIDTH)),
                  fixed((1, RW_WIDTH)), fixed((1, RW_WIDTH)), fixed((1, RW_WIDTH)),
                  fixed((PAIR, PAIR))],
        out_specs=pl.BlockSpec((nb, C, RW_WIDTH), lambda b, c: (b, c, 0)),
        out_shape=jax.ShapeDtypeStruct((B, S, RW_WIDTH), BF16),
        scratch_shapes=[pltpu.VMEM((nb, RW_WIDTH // PAIR, PAIR, PAIR), F32)],
        compiler_params=pltpu.CompilerParams(dimension_semantics=("parallel", "arbitrary"),
                                             vmem_limit_bytes=VMEM_LIMIT),
        name="rwkv",
    )(zrw, vec(w0), w2, vec(a0), a2, g2, vec(k_k), vec(k_a), vec(r_k), vec(ln_w), vec(ln_b), avg)


def _gelu_tanh(x):
    return 0.5 * x * (1.0 + jnp.tanh(math.sqrt(2.0 / math.pi) * (x + 0.044715 * (x * x * x))))


def _key_features(pos_hi, pos_lo, block, n, n_sel):
    lane = lax.broadcasted_iota(jnp.int32, (n, NSA_HEAD_DIM), 1)
    feat = jnp.where(lane == n_sel, pos_hi, jnp.where(lane == n_sel + 1, pos_lo, 0.0))
    return feat if block is None else jnp.where(lane == block, 1.0, feat)


def _nsa_prep_kernel(zkc_ref, zvc_ref, zks_ref, zvs_ref, zkw_ref, zvw_ref, pk_ref, pv_ref, kw1_ref,
                     kw2_ref, vw1_ref, vw2_ref, kc_ref, vct_ref, ksa_ref, kwa_ref, vst_ref, vwt_ref):
    S = zkc_ref.shape[1]
    n_grp = S // CMP_STRIDE
    Dh = NSA_HEAD_DIM
    half = CMP_BLOCK // 2
    n_sel = S // SEL_BLOCK
    jrow = lax.broadcasted_iota(jnp.int32, (n_grp, 1), 0)
    cmp_feat = _key_features((jrow >> 3).astype(F32),
                             ((jrow & 7) * CMP_STRIDE).astype(F32) + 0.5 * (CMP_BLOCK - 1),
                             None, n_grp, n_sel)
    for is_v, (z_ref, pos_ref, w1_ref, w2_ref) in enumerate(((zkc_ref, pk_ref, kw1_ref, kw2_ref),
                                                            (zvc_ref, pv_ref, vw1_ref, vw2_ref))):
        for g in range(NSA_KV_HEADS):
            lo = jnp.zeros((n_grp, Dh), F32)
            hi = jnp.zeros((n_grp, Dh), F32)
            for l in range(half):
                xs = z_ref[0, pl.ds(l, n_grp, stride=CMP_STRIDE), :]
                xg = xs[:, g * Dh:(g + 1) * Dh]
                lo = lo + _bdot(xg + pos_ref[l:l + 1, :], w1_ref[l * Dh:(l + 1) * Dh, :])
                hi = hi + _bdot(xg + pos_ref[half + l:half + l + 1, :],
                                w1_ref[(half + l) * Dh:(half + l + 1) * Dh, :])
            pre = lo + pltpu.roll(hi, n_grp - 1, 0)
            out = jnp.where(jrow < n_grp - 1, _bdot(_gelu_tanh(pre), w2_ref[...]), 0.0)
            if is_v:
                out_t = jnp.concatenate([out, jnp.zeros_like(out)], axis=1).T
                vct_ref[0, g] = out_t[0:Dh, :].astype(BF16)
            else:
                kc_ref[0, g] = jnp.concatenate([out, cmp_feat], axis=1)

    prow = lax.broadcasted_iota(jnp.int32, (S, 1), 0)
    p_hi, p_lo = (prow >> 7).astype(F32), (prow & (LANE - 1)).astype(F32)
    for z_ref, out_ref, block in ((zks_ref, ksa_ref, prow >> int(math.log2(SEL_BLOCK))),
                                  (zkw_ref, kwa_ref, None)):
        kfull = z_ref[0]
        key_feat = _key_features(p_hi, p_lo, block, S, n_sel)
        for g in range(NSA_KV_HEADS):
            out_ref[0, g] = jnp.concatenate([kfull[:, g * Dh:(g + 1) * Dh], key_feat], axis=1).astype(BF16)
    ones_row = (lax.broadcasted_iota(jnp.int32, (V_ROWS - Dh, TK), 0) == 0).astype(F32)
    for z_ref, out_ref in ((zvs_ref, vst_ref), (zvw_ref, vwt_ref)):
        for j in range(S // TK):
            vt = z_ref[0, j * TK:(j + 1) * TK, :].T
            out_ref[0, j] = jnp.concatenate(
                [piece for g in range(NSA_KV_HEADS) for piece in (vt[g * Dh:(g + 1) * Dh], ones_row)],
                axis=0).astype(BF16)


def _nsa_prep(zkv, pos_k, pos_v, kw1, kw2, vw1, vw2):
    B, S, _ = zkv.shape
    n_grp = S // CMP_STRIDE
    n_kt = S // TK
    Dh = NSA_HEAD_DIM
    G = NSA_KV_HEADS
    fixed = lambda shape: pl.BlockSpec(shape, lambda b: (0,) * len(shape))
    col = lambda c: pl.BlockSpec((1, S, NSA_KV_WIDTH), lambda b: (b, 0, c))
    whole = lambda shape: pl.BlockSpec((1,) + shape, lambda b: (b,) + (0,) * len(shape))
    shapes = [((G, n_grp, 2 * Dh), F32), ((G, Dh, n_grp), BF16), ((G, S, 2 * Dh), BF16),
              ((G, S, 2 * Dh), BF16), ((n_kt, G * V_ROWS, TK), BF16), ((n_kt, G * V_ROWS, TK), BF16)]
    return pl.pallas_call(
        _nsa_prep_kernel,
        grid=(B,),
        in_specs=[col(c) for c in range(6)] + [
            fixed((CMP_BLOCK, Dh)), fixed((CMP_BLOCK, Dh)),
            fixed((CMP_BLOCK * Dh, Dh)), fixed((Dh, Dh)),
            fixed((CMP_BLOCK * Dh, Dh)), fixed((Dh, Dh))],
        out_specs=[whole(s) for s, _ in shapes],
        out_shape=[jax.ShapeDtypeStruct((B,) + s, d) for s, d in shapes],
        compiler_params=pltpu.CompilerParams(dimension_semantics=("parallel",),
                                             vmem_limit_bytes=VMEM_LIMIT),
        name="nsa_prep",
    )(zkv, zkv, zkv, zkv, zkv, zkv, pos_k, pos_v, kw1, kw2, vw1, vw2)


def _nsa_attn_kernel(q_ref, kc_ref, vct_ref, ksa_ref, kwa_ref, vst_ref, vwt_ref, gl_ref, gb_ref,
                     ovlt_ref, slope_ref, o_ref, acc_ref, ot_ref):
    i = pl.program_id(1)
    Dh = NSA_HEAD_DIM
    R = NSA_REP
    N = R * TQ
    n_cmp_pad = kc_ref.shape[2]
    n_sel = ovlt_ref.shape[0]
    G = NSA_KV_HEADS
    nt = (((1,), (1,)), ((), ()))
    log2e = math.log2(math.e)
    t0 = i * TQ
    t_row = t0 + lax.broadcasted_iota(jnp.int32, (1, TQ), 1)
    c_col = lax.broadcasted_iota(jnp.int32, (TK, 1), 0)
    sgate_t = jax.nn.sigmoid(gl_ref[0] + gb_ref[...]).T
    lane_f = lax.broadcasted_iota(jnp.int32, (1, Dh), 1)
    heads = lambda x: jnp.concatenate([x] * R, axis=1)

    def key_dist(j):
        return t_row - (j * TK + c_col)

    def queries(g, sel_feat):
        parts = []
        for r in range(R):
            h = g * R + r
            sl = slope_ref[:, h:h + 1] * log2e
            feat = jnp.where(lane_f == n_sel, sl * LANE, jnp.where(lane_f == n_sel + 1, sl, sel_feat))
            parts.append(jnp.concatenate(
                [q_ref[0, :, h * Dh:(h + 1) * Dh] * (Dh ** -0.5 * log2e), jnp.broadcast_to(feat, (TQ, Dh))],
                axis=1))
        return jnp.concatenate(parts, axis=0)

    qab, o_cmp = [], []
    for g in range(G):
        jc = lax.broadcasted_iota(jnp.int32, (n_cmp_pad, 1), 0)
        ok_c = (jc * CMP_STRIDE + (CMP_BLOCK - 1) <= t_row) & (jc < n_cmp_pad - 1)
        s_c = (lax.dot_general(kc_ref[0, g], queries(g, 0.0), nt, precision=HI, preferred_element_type=F32)
               + heads(jnp.where(ok_c, 0.0, NEG_INF)))
        e_c = jnp.exp2(s_c - jnp.max(s_c, axis=0, keepdims=True))
        any_c = heads(t_row >= CMP_BLOCK - 1)
        p_c = e_c * jnp.where(any_c, 1.0 / jnp.sum(e_c, axis=0, keepdims=True), 0.0)
        o_cmp.append(jnp.dot(vct_ref[0, g], p_c.astype(BF16), preferred_element_type=F32))

        p_sum = p_c[:, 0:TQ]
        for r in range(1, R):
            p_sum = p_sum + p_c[:, r * TQ:(r + 1) * TQ]
        imp = jnp.dot(ovlt_ref[...], p_sum, precision=HI, preferred_element_type=F32)
        kb = lax.broadcasted_iota(jnp.int32, (n_sel, 1), 0)
        kbf = kb.astype(F32)
        blk_t = t_row >> int(math.log2(SEL_BLOCK))
        forced = (kb == 0) | (kb == blk_t) | (kb == blk_t - 1)
        cur = jnp.where(forced, FORCE_SCORE, jnp.where(kb <= blk_t, imp, -FORCE_SCORE))
        sel_bias = jnp.full((n_sel, TQ), NEG_INF, F32)
        for _ in range(min(N_SELECT, n_sel)):
            mx = jnp.max(cur, axis=0, keepdims=True)
            first = jnp.min(jnp.where(cur == mx, kbf, float(n_sel)), axis=0, keepdims=True)
            hit = kbf == first
            sel_bias = jnp.where(hit, 0.0, sel_bias)
            cur = jnp.where(hit, -3e38, cur)
        sel_feat = jnp.concatenate([sel_bias, jnp.zeros((LANE - n_sel, TQ), F32)], axis=0).T[:, 0:Dh]
        qab.append(queries(g, sel_feat).astype(BF16))

    VR = vst_ref.shape[2] // G

    SEL, WIN = (ksa_ref, vst_ref, 0), (kwa_ref, vwt_ref, 1)

    def tile(j, branches):
        chains = [(b, h) for b in range(len(branches)) for h in range(NSA_HEADS)]
        ks = [[k_ref[0, g, pl.ds(pl.multiple_of(j * TK, TK), TK), :] for g in range(G)]
              for (k_ref, _, _), _, _ in branches]
        vts = [[vt_ref[0, j, g * VR:(g + 1) * VR, :] for g in range(G)]
               for (_, vt_ref, _), _, _ in branches]
        s = {(b, h): lax.dot_general(ks[b][h // R], qab[h // R][(h % R) * TQ:(h % R + 1) * TQ], nt,
                                     preferred_element_type=F32) for b, h in chains}
        s = {(b, h): x if branches[b][2] is None else x + branches[b][2] for (b, h), x in s.items()}
        m_new = {(b, h): jnp.maximum(branches[b][1][h], jnp.max(s[b, h], axis=0, keepdims=True))
                 for b, h in chains}
        alpha = {(b, h): jnp.exp2(branches[b][1][h] - m_new[b, h]) for b, h in chains}
        p = {c: jnp.exp2(s[c] - m_new[c]).astype(BF16) for c in chains}
        pv = {(b, h): jnp.dot(vts[b][h // R], p[b, h], preferred_element_type=F32) for b, h in chains}
        for b, h in chains:
            slot = branches[b][0][2]
            acc_ref[slot, h] = alpha[b, h] * acc_ref[slot, h] + pv[b, h]
        return tuple(tuple(m_new[b, h] for h in range(NSA_HEADS)) for b in range(len(branches)))

    def window_bias(j):
        d = key_dist(j)
        return jnp.where((d >= 0) & (d < WINDOW), 0.0, NEG_INF)

    init = (jnp.full((1, TQ), NEG_INF, F32),) * NSA_HEADS
    acc_ref[...] = jnp.zeros_like(acc_ref)
    causal = jnp.where(key_dist(i) >= 0, 0.0, NEG_INF)
    win_lo = jnp.maximum(i - WINDOW // TK, 0)
    m_sel = lax.fori_loop(0, win_lo, lambda j, m: tile(j, [(SEL, m, None)])[0], init)
    m_sel, m_win = lax.fori_loop(
        win_lo, i, lambda j, ms: tile(j, [(SEL, ms[0], None), (WIN, ms[1], window_bias(j))]), (m_sel, init))
    tile(i, [(SEL, m_sel, causal), (WIN, m_win, causal)])

    for h in range(NSA_HEADS):
        g, r = divmod(h, R)
        acc_s, acc_w = acc_ref[0, h], acc_ref[1, h]
        ot_ref[h * Dh:(h + 1) * Dh, :] = (
            sgate_t[3 * h:3 * h + 1, :] * o_cmp[g][:, r * TQ:(r + 1) * TQ]
            + sgate_t[3 * h + 1:3 * h + 2, :] * (acc_s[0:Dh] * (1.0 / acc_s[Dh:Dh + 1]))
            + sgate_t[3 * h + 2:3 * h + 3, :] * (acc_w[0:Dh] * (1.0 / acc_w[Dh:Dh + 1])))
    o_ref[0] = ot_ref[...].T.astype(o_ref.dtype)


def _nsa_attn(zq, kc, vct, ksa, kwa, vst, vwt, zgate, gate_b):
    B, S, _ = zq.shape
    n_sel = S // SEL_BLOCK
    n_cmp = (S - CMP_BLOCK) // CMP_STRIDE + 1
    n_cmp_pad = kc.shape[2]
    n_kt = S // TK
    G, Dh = NSA_KV_HEADS, NSA_HEAD_DIM
    cmp_start = np.arange(n_cmp) * CMP_STRIDE
    sel_start = np.arange(n_sel) * SEL_BLOCK
    overlap = np.clip(np.minimum(cmp_start[:, None] + CMP_BLOCK, sel_start[None, :] + SEL_BLOCK)
                      - np.maximum(cmp_start[:, None], sel_start[None, :]), 0, None) / CMP_BLOCK
    ovlt = np.zeros((n_sel, n_cmp_pad), np.float32)
    ovlt[:, :n_cmp] = overlap.T
    slopes = (2.0 ** (-8.0 * np.arange(1, NSA_HEADS + 1) / NSA_HEADS)).astype(np.float32).reshape(1, NSA_HEADS)
    gb = jnp.pad(gate_b, (0, LANE - N_GATE)).reshape(1, LANE)
    fixed = lambda shape: pl.BlockSpec(shape, lambda b, i: (0,) * len(shape))
    per_b = lambda shape: pl.BlockSpec((1,) + shape, lambda b, i: (b,) + (0,) * len(shape))
    return pl.pallas_call(
        _nsa_attn_kernel,
        grid=(B, S // TQ),
        in_specs=[pl.BlockSpec((1, TQ, NSA_WIDTH), lambda b, i: (b, i, 0)),
                  per_b((G, n_cmp_pad, 2 * Dh)), per_b((G, Dh, n_cmp_pad)),
                  per_b((G, S, 2 * Dh)), per_b((G, S, 2 * Dh)),
                  per_b((n_kt, G * V_ROWS, TK)), per_b((n_kt, G * V_ROWS, TK)),
                  pl.BlockSpec((1, TQ, LANE), lambda b, i: (b, i, 0)),
                  fixed((1, LANE)), fixed((n_sel, n_cmp_pad)), fixed((1, NSA_HEADS))],
        out_specs=pl.BlockSpec((1, TQ, NSA_WIDTH), lambda b, i: (b, i, 0)),
        out_shape=jax.ShapeDtypeStruct((B, S, NSA_WIDTH), BF16),
        scratch_shapes=[pltpu.VMEM((2, NSA_HEADS, V_ROWS, TQ), F32),
                        pltpu.VMEM((NSA_WIDTH, TQ), F32)],
        compiler_params=pltpu.CompilerParams(dimension_semantics=("parallel", "arbitrary"),
                                             vmem_limit_bytes=VMEM_LIMIT),
        name="nsa_attn",
    )(zq, kc, vct, ksa, kwa, vst, vwt, zgate, gb, jnp.asarray(ovlt), jnp.asarray(slopes))


def _mix_kernel(x_ref, yr_ref, yn_ref, zmg_ref, ur_ref, un_ref, wo_ref, gf_ref, wr_ref, br_ref,
                x1_ref, h2_ref, rt_ref, cnt_ref):
    tm = x_ref.shape[0]
    nt = (((1,), (1,)), ((), ()))
    chunks = [slice(c * TM_PROJ, (c + 1) * TM_PROJ) for c in range(tm // TM_PROJ)]
    up_r = [_bdot(yr_ref[c, :], ur_ref[...]) for c in chunks]
    up_n = [_bdot(yn_ref[c, :], un_ref[...]) for c in chunks]
    mixed = [jax.nn.sigmoid(zmg_ref[c, 0:D_MODEL].astype(F32)) * a
             + jax.nn.sigmoid(zmg_ref[c, D_MODEL:2 * D_MODEL].astype(F32)) * b
             for c, a, b in zip(chunks, up_r, up_n)]
    x1 = [x_ref[c, :] + _bdot(m, wo_ref[...]) for c, m in zip(chunks, mixed)]
    h2 = [_rms(v, gf_ref[...]) for v in x1]
    for c, v, h in zip(chunks, x1, h2):
        x1_ref[c, :] = v
        _store_row_tiles(h2_ref.at[c], h)
    n_row = wr_ref.shape[0]
    logits = jnp.concatenate(
        [lax.dot_general(wr_ref[...], h, nt, precision=HI, preferred_element_type=F32) for h in h2],
        axis=1) + br_ref[...]
    row = lax.broadcasted_iota(jnp.int32, (n_row, 1), 0).astype(F32)
    gl = jnp.where(row < N_GROUPS, logits, NEG_INF)
    gmax = jnp.max(gl, axis=0, keepdims=True)
    g_sel = jnp.min(jnp.where(gl == gmax, row, float(n_row)), axis=0, keepdims=True)
    p_group = 1.0 / jnp.sum(jnp.exp(gl - gmax), axis=0, keepdims=True)
    e_row = row - N_GROUPS
    in_grp = ((e_row >= g_sel * EXPERTS_PER_GROUP) & (e_row < (g_sel + 1.0) * EXPERTS_PER_GROUP)
              & (e_row < N_EXPERTS))
    el = jnp.where(in_grp, logits, NEG_INF)
    m1 = jnp.max(el, axis=0, keepdims=True)
    i1 = jnp.min(jnp.where(el == m1, e_row, float(n_row)), axis=0, keepdims=True)
    el2 = jnp.where(e_row == i1, 2.0 * NEG_INF, el)
    m2 = jnp.max(el2, axis=0, keepdims=True)
    i2 = jnp.min(jnp.where(el2 == m2, e_row, float(n_row)), axis=0, keepdims=True)
    r2 = jnp.exp(m2 - m1)
    g1 = p_group / (1.0 + r2)
    g2 = p_group * r2 / (1.0 + r2)

    @pl.when(pl.program_id(0) == 0)
    def _():
        cnt_ref[...] = jnp.zeros_like(cnt_ref)

    pick1, pick2 = e_row == i1, e_row == i2
    both = pick1.astype(F32) + pick2.astype(F32)
    earlier = (lax.broadcasted_iota(jnp.int32, (tm, tm), 0)
               < lax.broadcasted_iota(jnp.int32, (tm, tm), 1)).astype(BF16)
    before = jnp.dot(both.astype(BF16), earlier, preferred_element_type=F32) + cnt_ref[:, 0:1]
    rank1 = jnp.sum(jnp.where(pick1, before, 0.0), axis=0, keepdims=True)
    rank2 = jnp.sum(jnp.where(pick2, before, 0.0), axis=0, keepdims=True)
    cnt_ref[...] = cnt_ref[...] + jnp.sum(both, axis=1, keepdims=True)
    rt_ref[...] = jnp.concatenate([i1, i2, g1, g2, rank1, rank2, jnp.zeros((2, tm), F32)], axis=0)


def _mix(x2d, y_rw, y_nsa, zmg, w_up_r, w_up_n, w_out, g_ffn, w_group, b_group, w_router, b_router):
    T = x2d.shape[0]
    tm = MIX_CHUNKS * TM_PROJ
    n_r = N_GROUPS + N_EXPERTS
    n_row = -(-n_r // 8) * 8
    wr = jnp.pad(jnp.concatenate([w_group, w_router], axis=1).T, ((0, n_row - n_r), (0, 0)))
    br = jnp.pad(jnp.concatenate([b_group, b_router]), (0, n_row - n_r)).reshape(n_row, 1)
    row = lambda i: (i, 0)
    fixed = lambda i: (0, 0)
    return pl.pallas_call(
        _mix_kernel,
        grid=(T // tm,),
        in_specs=[pl.BlockSpec((tm, D_MODEL), row), pl.BlockSpec((tm, RW_WIDTH), row),
                  pl.BlockSpec((tm, NSA_WIDTH), row), pl.BlockSpec((tm, 2 * D_MODEL), row),
                  pl.BlockSpec((RW_WIDTH, D_MODEL), fixed), pl.BlockSpec((NSA_WIDTH, D_MODEL), fixed),
                  pl.BlockSpec((D_MODEL, D_MODEL), fixed), pl.BlockSpec((1, D_MODEL), fixed),
                  pl.BlockSpec((n_row, D_MODEL), fixed), pl.BlockSpec((n_row, 1), fixed)],
        out_specs=[pl.BlockSpec((tm, D_MODEL), row), pl.BlockSpec((tm,) + ROW_TILE, lambda i: (i, 0, 0)),
                   pl.BlockSpec((8, tm), lambda i: (0, i)), pl.BlockSpec((n_row, LANE), fixed)],
        out_shape=[jax.ShapeDtypeStruct((T, D_MODEL), F32), jax.ShapeDtypeStruct((T,) + ROW_TILE, F32),
                   jax.ShapeDtypeStruct((8, T), F32), jax.ShapeDtypeStruct((n_row, LANE), F32)],
        compiler_params=pltpu.CompilerParams(dimension_semantics=("arbitrary",),
                                             vmem_limit_bytes=VMEM_LIMIT),
        name="mix",
    )(x2d, y_rw, y_nsa, zmg, w_up_r.astype(BF16), w_up_n.astype(BF16), w_out.astype(BF16),
      g_ffn.reshape(1, D_MODEL), wr, br)


def _route_tables(rt, cnt, T):
    n_rows = T * TOP_K + N_EXPERTS * ROW_BLOCK
    n_blk = n_rows // ROW_BLOCK
    counts = cnt[N_GROUPS:N_GROUPS + N_EXPERTS, 0].astype(jnp.int32)
    padded = (counts + ROW_BLOCK - 1) // ROW_BLOCK * ROW_BLOCK
    pends = jnp.cumsum(padded)
    pstarts = pends - padded
    expert = rt[0:TOP_K].astype(jnp.int32)
    rank = rt[2 * TOP_K:3 * TOP_K].astype(jnp.int32)
    seg_start = jnp.sum(jnp.where(expert[..., None] == jnp.arange(N_EXPERTS), pstarts, 0), axis=-1)
    dest = (seg_start + rank).T.reshape(T * TOP_K)
    gates = rt[TOP_K:2 * TOP_K].T
    blk_start = jnp.arange(n_blk) * ROW_BLOCK
    blk_expert = jnp.minimum(jnp.sum(pends[None, :] <= blk_start[:, None], axis=1), N_EXPERTS - 1)
    n_active = (pends[N_EXPERTS - 1:] // ROW_BLOCK).astype(jnp.int32)
    return dest.astype(jnp.int32), gates, blk_expert.astype(jnp.int32), n_active


ROW_TILE = (D_MODEL // LANE, LANE)


def _store_row_tiles(ref, x):
    for c in range(ROW_TILE[0]):
        ref[:, c, :] = x[:, c * LANE:(c + 1) * LANE]


def _load_row_tiles(ref, idx):
    return jnp.concatenate([ref[(*idx, slice(None), c, slice(None))] for c in range(ROW_TILE[0])], axis=1)


EXPERT_BUFS = 3
TOK_BITS = 14


def _row_info_kernel(dest_ref, word_ref, fill_hbm, info_ref, sem):
    fill = pltpu.make_async_copy(fill_hbm, info_ref, sem)
    fill.start()
    fill.wait()

    def body(a, carry):
        info_ref[dest_ref[a]] = word_ref[a]
        return carry

    lax.fori_loop(0, dest_ref.shape[0], body, 0, unroll=16)


def _row_info(dest, n_tok):
    n_rows = n_tok * TOP_K + N_EXPERTS * ROW_BLOCK
    assert n_tok <= 1 << TOK_BITS and (n_tok * TOP_K + 2 * ROW_BLOCK) << TOK_BITS < 2 ** 31
    a = jnp.arange(n_tok * TOP_K, dtype=jnp.int32)
    tok, k = a // TOP_K, a % TOP_K
    word = tok | ((k * n_tok + tok) << TOK_BITS)
    row = jnp.arange(n_rows, dtype=jnp.int32)
    spare = n_tok * TOP_K + ((row // ROW_BLOCK) % 2) * ROW_BLOCK + row % ROW_BLOCK
    return pl.pallas_call(
        _row_info_kernel,
        in_specs=[pl.BlockSpec(memory_space=pltpu.SMEM), pl.BlockSpec(memory_space=pltpu.SMEM),
                  pl.BlockSpec(memory_space=pl.ANY)],
        out_specs=pl.BlockSpec(memory_space=pltpu.SMEM),
        out_shape=jax.ShapeDtypeStruct((n_rows,), jnp.int32),
        scratch_shapes=[pltpu.SemaphoreType.DMA(())],
        name="row_info",
    )(dest, word, spare << TOK_BITS)


def _expert_kernel(be_ref, nact_ref, info_ref, h2_hbm, wgu_ref, wd_ref, eo_hbm,
                   xbuf, obuf, wgu_b, wd_b, gsem, ssem):
    i = pl.program_id(0)
    n_act = nact_ref[0]
    tok_mask = (1 << TOK_BITS) - 1

    def gather(blk, s):
        for r in range(ROW_BLOCK):
            tok = info_ref[blk * ROW_BLOCK + r] & tok_mask
            pltpu.make_async_copy(h2_hbm.at[tok], xbuf.at[s, r], gsem.at[s]).start()

    def scatter(blk, s):
        for r in range(ROW_BLOCK):
            row = info_ref[blk * ROW_BLOCK + r] >> TOK_BITS
            pltpu.make_async_copy(obuf.at[s, pl.ds(r, 1)], eo_hbm.at[pl.ds(row, 1)], ssem.at[s]).start()

    def drain_gather(s):
        pltpu.make_async_copy(h2_hbm.at[pl.ds(0, ROW_BLOCK)], xbuf.at[s], gsem.at[s]).wait()

    def drain_scatter(s):
        pltpu.make_async_copy(obuf.at[s], eo_hbm.at[pl.ds(0, ROW_BLOCK)], ssem.at[s]).wait()

    n_buf = xbuf.shape[0]

    def block(s, first):
        drain_gather(s)
        gather(jnp.minimum(i + 2, n_act - 1), (s + 2) % n_buf)
        if not first:
            scatter(i - 1, (s + 2) % n_buf)
        gu = jnp.dot(_load_row_tiles(xbuf, (s,)).astype(BF16), wgu_b[...], preferred_element_type=F32)
        gate_h, up_h = gu[:, :D_EXPERT], gu[:, D_EXPERT:]
        mid = gate_h * jax.nn.sigmoid(gate_h) * up_h
        obuf[s] = jnp.dot(mid.astype(BF16), wd_b[...], preferred_element_type=F32)

    @pl.when(jnp.logical_and(i < n_act, jnp.logical_or(i == 0, be_ref[i] != be_ref[jnp.maximum(i - 1, 0)])))
    def _():
        wgu_b[...] = wgu_ref[0].astype(BF16)
        wd_b[...] = wd_ref[0].astype(BF16)

    @pl.when(i == 0)
    def _():
        gather(0, 0)
        gather(jnp.minimum(1, n_act - 1), 1)
        block(0, first=True)

    for s in range(n_buf):
        mine = i % n_buf == s

        @pl.when(jnp.logical_and(mine, jnp.logical_and(i >= 1, i < n_act)))
        def _():
            @pl.when(i >= n_buf)
            def _():
                drain_scatter(s)
            block(s, first=False)

        @pl.when(jnp.logical_and(mine, i == n_act - 1))
        def _():
            for other in ((s + 1) % n_buf, (s + 2) % n_buf):
                drain_gather(other)

            @pl.when(i >= 2)
            def _():
                drain_scatter((s + 1) % n_buf)

            @pl.when(i >= 1)
            def _():
                drain_scatter((s + 2) % n_buf)
            scatter(i, s)
            drain_scatter(s)
            obuf[s] = jnp.zeros_like(obuf[s])
            for half in range(2):
                spare = eo_hbm.at[pl.ds(eo_hbm.shape[0] - (2 - half) * ROW_BLOCK, ROW_BLOCK)]
                pltpu.make_async_copy(obuf.at[s], spare, ssem.at[s]).start()
            for half in range(2):
                drain_scatter(s)


def _experts(h2, info, blk_expert, n_active, w_gate_up, w_down):
    n_tok = h2.shape[0]
    n_blk = blk_expert.shape[0]
    grid_spec = pltpu.PrefetchScalarGridSpec(
        num_scalar_prefetch=3,
        grid=(n_blk,),
        in_specs=[pl.BlockSpec(memory_space=pl.ANY),
                  pl.BlockSpec((1, D_MODEL, 2 * D_EXPERT), lambda i, be, na, info: (be[i], 0, 0)),
                  pl.BlockSpec((1, D_EXPERT, D_MODEL), lambda i, be, na, info: (be[i], 0, 0))],
        out_specs=pl.BlockSpec(memory_space=pl.ANY),
        scratch_shapes=[pltpu.VMEM((EXPERT_BUFS, ROW_BLOCK) + ROW_TILE, F32),
                        pltpu.VMEM((EXPERT_BUFS, ROW_BLOCK, D_MODEL), F32),
                        pltpu.VMEM((D_MODEL, 2 * D_EXPERT), BF16), pltpu.VMEM((D_EXPERT, D_MODEL), BF16),
                        pltpu.SemaphoreType.DMA((EXPERT_BUFS,)), pltpu.SemaphoreType.DMA((EXPERT_BUFS,))])
    return pl.pallas_call(
        _expert_kernel,
        grid_spec=grid_spec,
        out_shape=jax.ShapeDtypeStruct((n_tok * TOP_K + 2 * ROW_BLOCK, D_MODEL), F32),
        compiler_params=pltpu.CompilerParams(dimension_semantics=("arbitrary",),
                                             vmem_limit_bytes=VMEM_LIMIT),
        name="experts",
    )(blk_expert, n_active, info, h2, w_gate_up, w_down)


def _ple_kernel(x1_ref, g_ref, p_ref, *rest, last_layer):
    eo_refs, (wpp_ref, gpn_ref, ggi_ref, wpg_ref, gfin_ref, y_ref) = rest[:TOP_K], rest[TOP_K:]
    tm = x1_ref.shape[0]
    chunks = [slice(c * TM_PROJ, (c + 1) * TM_PROJ) for c in range(tm // TM_PROJ)]
    x2 = []
    for c in chunks:
        moe = g_ref[c, 0:1] * eo_refs[0][c, :]
        for k in range(1, TOP_K):
            moe = moe + g_ref[c, k:k + 1] * eo_refs[k][c, :]
        x2.append(x1_ref[c, :] + moe)
    e = [_rms(_bdot(p_ref[c, :], wpp_ref[...]), gpn_ref[...]) for c in chunks]
    gate = [jax.nn.sigmoid(_bdot(_rms(v, ggi_ref[...]), wpg_ref[...])) for v in x2]
    for c, v, g, ee in zip(chunks, x2, gate, e):
        x3 = v + g * ee
        y_ref[c, :] = _rms(x3, gfin_ref[...]) if last_layer else x3


def _ple(x1, gates, expert_out, p2d, w_pp, g_pn, g_gi, w_pg, g_final, last_layer):
    T = x1.shape[0]
    tm = MIX_CHUNKS * TM_PROJ
    nt = T // tm
    row = lambda i: (i, 0)
    fixed = lambda i: (0, 0)
    vec = lambda a: a.reshape(1, D_MODEL)
    slot_rows = [pl.BlockSpec((tm, D_MODEL), functools.partial(lambda i, k: (i + k * nt, 0), k=k))
                 for k in range(TOP_K)]
    return pl.pallas_call(
        functools.partial(_ple_kernel, last_layer=last_layer),
        grid=(nt,),
        in_specs=[pl.BlockSpec((tm, D_MODEL), row), pl.BlockSpec((tm, TOP_K), row),
                  pl.BlockSpec((tm, PLE_DIM), row)] + slot_rows + [
                  pl.BlockSpec((PLE_DIM, D_MODEL), fixed), pl.BlockSpec((1, D_MODEL), fixed),
                  pl.BlockSpec((1, D_MODEL), fixed), pl.BlockSpec((D_MODEL, D_MODEL), fixed),
                  pl.BlockSpec((1, D_MODEL), fixed)],
        out_specs=pl.BlockSpec((tm, D_MODEL), row),
        out_shape=jax.ShapeDtypeStruct((T, D_MODEL), F32),
        compiler_params=pltpu.CompilerParams(dimension_semantics=("parallel",),
                                             vmem_limit_bytes=VMEM_LIMIT),
        name="ple",
    )(x1, gates, p2d, *([expert_out] * TOP_K), w_pp.astype(BF16), vec(g_pn), vec(g_gi),
      w_pg.astype(BF16), vec(g_final))


def kernel(x, p, g_mix, w_in, mu_shift, rw_w0, rw_w2, rw_a0, rw_a2, rw_g2, rw_k_k, rw_k_a, rw_r_k, rw_ln_w, rw_ln_b, cmp_pos_k, cmp_pos_v, cmp_k_w1, cmp_k_w2, cmp_v_w1, cmp_v_w2, nsa_gate_b, w_up_rwkv, w_up_nsa, w_out, g_ffn, w_group, b_group, w_router, b_router, w_exp_gate_up, w_exp_down, w_ple_proj, g_ple_norm, g_ple_gate_in, w_ple_gate, g_final):
    B, S, D = x.shape
    T = B * S
    depth = p.shape[0]
    xc = x.reshape(T, D)
    for i in range(depth):
        zrw, zq, zkv, zmg, zgate = _proj(xc, g_mix[i], w_in[i], mu_shift[i], S)
        y_rw = _rwkv(zrw.reshape(B, S, RW_IN), rw_w0[i], rw_w2[i], rw_a0[i], rw_a2[i], rw_g2[i],
                     rw_k_k[i], rw_k_a[i], rw_r_k[i], rw_ln_w[i], rw_ln_b[i])
        zkv3 = zkv.reshape(B, S, KV_IN)
        nsa_kv = _nsa_prep(zkv3, cmp_pos_k[i], cmp_pos_v[i], cmp_k_w1[i], cmp_k_w2[i],
                           cmp_v_w1[i], cmp_v_w2[i])
        y_nsa = _nsa_attn(zq.reshape(B, S, NSA_WIDTH), *nsa_kv, zgate.reshape(B, S, LANE),
                          nsa_gate_b[i])
        x1, h2, rt, cnt = _mix(xc, y_rw.reshape(T, RW_WIDTH), y_nsa.reshape(T, NSA_WIDTH), zmg,
                               w_up_rwkv[i], w_up_nsa[i], w_out[i], g_ffn[i], w_group[i], b_group[i],
                               w_router[i], b_router[i])
        dest, gates, blk_expert, n_active = _route_tables(rt, cnt, T)
        expert_out = _experts(h2, _row_info(dest, T), blk_expert, n_active, w_exp_gate_up[i],
                              w_exp_down[i])
        xc = _ple(x1, gates, expert_out, p[i].reshape(T, PLE_DIM), w_ple_proj[i], g_ple_norm[i],
                  g_ple_gate_in[i], w_ple_gate[i], g_final, i == depth - 1)
    return xc.reshape(B, S, D)
```

```python
import functools
import math

import jax
import jax.numpy as jnp
import numpy as np
from jax import lax
from jax.experimental import pallas as pl
from jax.experimental.pallas import tpu as pltpu

F32 = jnp.float32
BF16 = jnp.bfloat16

D_MODEL = 1024
RW_HEADS = 8
RW_HEAD_DIM = 64
RW_WIDTH = 512
DECAY_LORA = 64
AAA_LORA = 64
GATE_LORA = 128
GN_EPS = 64e-5
RW_IN = 3 * RW_WIDTH + DECAY_LORA + AAA_LORA + GATE_LORA

NSA_HEADS = 8
NSA_KV_HEADS = 2
NSA_REP = NSA_HEADS // NSA_KV_HEADS
NSA_HEAD_DIM = 64
NSA_WIDTH = 512
NSA_KV_WIDTH = 128
CMP_BLOCK = 32
CMP_STRIDE = 16
SEL_BLOCK = 64
N_SELECT = 8
WINDOW = 512
N_NSA_BRANCH = 3
FORCE_SCORE = 1e6
NEG_INF = -1e30

N_GROUPS = 4
EXPERTS_PER_GROUP = 8
N_EXPERTS = 32
TOP_K = 2
D_EXPERT = 512
ROW_BLOCK = 128
PLE_DIM = 256
NORM_EPS = 1e-6

N_GATE = N_NSA_BRANCH * NSA_HEADS
ATT_IN = NSA_WIDTH + 6 * NSA_KV_WIDTH
KV_OFF = RW_IN + NSA_WIDTH
KV_IN = 6 * NSA_KV_WIDTH
GATE_OFF = RW_IN + ATT_IN
MERGE_OFF = GATE_OFF + N_GATE
LANE = 128

RW_CHUNK = 64
RW_SUB = 16
RW_ROWS = 4
TQ = 256
TK = 256
V_ROWS = NSA_HEAD_DIM + 16
TM_PROJ = 256
MIX_CHUNKS = 2
VMEM_LIMIT = 56 * 1024 * 1024


def _bdot(a, b):
    return jnp.dot(a.astype(BF16), b.astype(BF16), preferred_element_type=F32)


def _bdot_nt(a, b):
    return lax.dot_general(a.astype(BF16), b.astype(BF16), (((1,), (1,)), ((), ())),
                           preferred_element_type=F32)


def _split_bf16(x):
    hi = x.astype(BF16)
    return hi, (x - hi.astype(F32)).astype(BF16)


def _dot3_nt(a, b):
    (a_hi, a_lo), (b_hi, b_lo) = _split_bf16(a), _split_bf16(b)
    return _bdot_nt(a_hi, b_hi) + _bdot_nt(a_hi, b_lo) + _bdot_nt(a_lo, b_hi)


def _rms(x, g):
    return x * lax.rsqrt(jnp.mean(x * x, axis=-1, keepdims=True) + NORM_EPS) * g


def _proj_kernel(x_ref, g_ref, w_ref, mu_ref, zrw_ref, zq_ref, zkv_ref, zmg_ref, zgate_ref,
                 carry_ref, *, tiles_per_seq):
    i = pl.program_id(0)
    tm = x_ref.shape[0]

    @pl.when(i % tiles_per_seq == 0)
    def _():
        carry_ref[...] = jnp.zeros_like(carry_ref)

    h = _rms(x_ref[...], g_ref[...]).astype(BF16)
    z = jnp.dot(h, w_ref[:, 0:RW_IN], preferred_element_type=F32)
    row = lax.broadcasted_iota(jnp.int32, (tm, 1), 0)
    prev = jnp.where(row == 0, carry_ref[7:8, :], pltpu.roll(z, 1, 0))
    carry_ref[...] = z[tm - 8:tm, :]
    zrw_ref[...] = z + (prev - z) * mu_ref[...]
    zq_ref[...] = jnp.dot(h, w_ref[:, RW_IN:KV_OFF], preferred_element_type=F32)
    zkv_ref[...] = jnp.dot(h, w_ref[:, KV_OFF:GATE_OFF], preferred_element_type=F32)
    zmg_ref[...] = jnp.dot(h, w_ref[:, GATE_OFF:GATE_OFF + 2 * D_MODEL],
                           preferred_element_type=F32).astype(zmg_ref.dtype)
    zgate_ref[...] = jnp.dot(h, w_ref[:, GATE_OFF + 2 * D_MODEL:], preferred_element_type=F32)


def _proj(x2d, g_mix, w_in, mu, seq):
    T = x2d.shape[0]
    tm = TM_PROJ
    wp = jnp.concatenate(
        [w_in[:, :GATE_OFF], w_in[:, MERGE_OFF:],
         jnp.pad(w_in[:, GATE_OFF:MERGE_OFF], ((0, 0), (0, LANE - N_GATE)))], axis=1).astype(BF16)
    npad = wp.shape[1]
    row = lambda i: (i, 0)
    fixed = lambda i: (0, 0)
    return pl.pallas_call(
        functools.partial(_proj_kernel, tiles_per_seq=seq // tm),
        grid=(T // tm,),
        in_specs=[pl.BlockSpec((tm, D_MODEL), row), pl.BlockSpec((1, D_MODEL), fixed),
                  pl.BlockSpec((D_MODEL, npad), fixed), pl.BlockSpec((1, RW_IN), fixed)],
        out_specs=[pl.BlockSpec((tm, RW_IN), row), pl.BlockSpec((tm, NSA_WIDTH), row),
                   pl.BlockSpec((tm, KV_IN), row), pl.BlockSpec((tm, 2 * D_MODEL), row),
                   pl.BlockSpec((tm, LANE), row)],
        out_shape=[jax.ShapeDtypeStruct((T, RW_IN), F32), jax.ShapeDtypeStruct((T, NSA_WIDTH), F32),
                   jax.ShapeDtypeStruct((T, KV_IN), F32), jax.ShapeDtypeStruct((T, 2 * D_MODEL), BF16),
                   jax.ShapeDtypeStruct((T, LANE), F32)],
        scratch_shapes=[pltpu.VMEM((8, RW_IN), F32)],
        compiler_params=pltpu.CompilerParams(dimension_semantics=("arbitrary",),
                                             vmem_limit_bytes=VMEM_LIMIT),
        name="proj",
    )(x2d, g_mix.reshape(1, D_MODEL), wp, mu.reshape(1, RW_IN))


PAIR = 2 * RW_HEAD_DIM


def _pair_blocks(x):
    low = lax.broadcasted_iota(jnp.int32, (1, PAIR), 1) < RW_HEAD_DIM
    return jnp.concatenate([jnp.where(low, x, 0.0), jnp.where(low, 0.0, x)], axis=0)


def _pmm(a, b):
    return _bdot(a, _pair_blocks(b))


def _unit_lower_inverse(a_strict, sub_mask, eye):
    ad = [jnp.where(sub_mask, a, 0.0) for a in a_strict]
    ao = [a - d for a, d in zip(a_strict, ad)]
    td = [eye - d for d in ad]
    pw = ad
    for _ in range(int(math.log2(RW_SUB)) - 1):
        pw = [_pmm(x, x) for x in pw]
        td = [_pmm(t, eye + x) for t, x in zip(td, pw)]
    n = [_pmm(t, o) for t, o in zip(td, ao)]
    t = [eye - x for x in n]
    pw = n
    for _ in range(int(math.log2(RW_CHUNK // RW_SUB)) - 1):
        pw = [_pmm(x, x) for x in pw]
        t = [_pmm(a, eye + x) for a, x in zip(t, pw)]
    return [_pmm(a, d) for a, d in zip(t, td)]


def _rwkv_kernel(z_ref, w0_ref, w2_ref, a0_ref, a2_ref, g2_ref, kk_ref, ka_ref, rk_ref, lnw_ref,
                 lnb_ref, avg_ref, o_ref, h_ref):
    c = pl.program_id(1)
    C = RW_CHUNK
    n_pair = RW_WIDTH // PAIR
    nt = (((1,), (1,)), ((), ()))

    @pl.when(c == 0)
    def _():
        h_ref[...] = jnp.zeros_like(h_ref)

    ti = lax.broadcasted_iota(jnp.int32, (C, 1), 0)
    si = lax.broadcasted_iota(jnp.int32, (1, PAIR), 1) & (RW_HEAD_DIM - 1)
    incl, strict = ti >= si, ti > si
    eye = (ti == si).astype(F32)
    sub_shift = int(math.log2(RW_SUB))
    sub_mask = (ti >> sub_shift) == (si >> sub_shift)
    row2 = lax.broadcasted_iota(jnp.int32, (PAIR, 1), 0)
    col2 = lax.broadcasted_iota(jnp.int32, (1, PAIR), 1)
    same_head = (row2 < RW_HEAD_DIM) == (col2 < RW_HEAD_DIM)
    eye2 = row2 == col2
    tri = (lax.broadcasted_iota(jnp.int32, (C, C), 0)
           >= lax.broadcasted_iota(jnp.int32, (C, C), 1)).astype(BF16)

    def head_mean(x):
        xs = jnp.concatenate([x[:, p * PAIR:(p + 1) * PAIR] for p in range(n_pair)], axis=0)
        ms = _bdot(xs, avg_ref[...])
        return jnp.concatenate([ms[p * C:(p + 1) * C] for p in range(n_pair)], axis=1)

    n_rows = z_ref.shape[0]
    rows = []
    for n in range(n_rows):
        z = z_ref[n]
        zr, zk, zv = z[:, 0:512], z[:, 512:1024], z[:, 1024:1536]
        zw, za, zg = z[:, 1536:1600], z[:, 1600:1664], z[:, 1664:1792]
        w_raw = w0_ref[...] + _bdot(jnp.tanh(zw), w2_ref[...])
        logw = -jax.nn.sigmoid(w_raw) * math.exp(-0.5)
        a = jax.nn.sigmoid(a0_ref[...] + _bdot(za, a2_ref[...]))
        gate = _bdot(jax.nn.sigmoid(zg), g2_ref[...])
        kk = zk * kk_ref[...]
        kk = kk / jnp.maximum(jnp.sqrt(head_mean(kk * kk) * RW_HEAD_DIM), 1e-12)
        k = zk * (1.0 + (a - 1.0) * ka_ref[...])
        b = kk * a

        w_hi = logw.astype(BF16)
        w_lo = (logw - w_hi.astype(F32)).astype(BF16)
        cum = (jnp.dot(tri, w_hi, preferred_element_type=F32)
               + jnp.dot(tri, w_lo, preferred_element_type=F32))
        cum_last = cum[C - 1:C, :]
        g_inv = jnp.exp(-cum)
        g_end = jnp.exp(cum_last - cum)
        rows.append(dict(rt=zr * jnp.exp(cum), kt=k * g_inv, bt=b * g_inv, qt=kk * jnp.exp(cum - logw),
                         kh=k * g_end, bh=b * g_end, v=zv, g_last=jnp.exp(cum_last), gate=gate,
                         bonus=head_mean(zr * k * rk_ref[...]) * RW_HEAD_DIM * zv))

    chains = [(n, slice(p * PAIR, (p + 1) * PAIR)) for n in range(n_rows) for p in range(n_pair)]
    part = lambda name: [rows[n][name][:, sl] for n, sl in chains]
    qt, rt, kt, bt, kh, bh, v = (part(x) for x in ("qt", "rt", "kt", "bt", "kh", "bh", "v"))
    lhs = [jnp.concatenate([q, r], axis=0).astype(BF16) for q, r in zip(qt, rt)]
    ab = [lax.dot_general(l, _pair_blocks(x).astype(BF16), nt, preferred_element_type=F32)
          for l, x in zip(lhs, bt)]
    ak = [lax.dot_general(l, _pair_blocks(x).astype(BF16), nt, preferred_element_type=F32)
          for l, x in zip(lhs, kt)]
    a_kb = [jnp.where(strict, x[0:C], 0.0) for x in ab]
    a_rb = [jnp.where(incl, x[C:2 * C], 0.0) for x in ab]
    a_kk = [jnp.where(strict, x[0:C], 0.0) for x in ak]
    a_rk = [jnp.where(incl, x[C:2 * C], 0.0) for x in ak]
    t_inv = _unit_lower_inverse(a_kb, sub_mask, eye)

    h = [h_ref[n, sl.start // PAIR] for n, sl in chains]
    vb = [_pair_blocks(x) for x in v]
    rhs = [_bdot(jnp.concatenate([q, akk], axis=1), jnp.concatenate([hh, vv], axis=0))
           for q, akk, hh, vv in zip(qt, a_kk, h, vb)]
    u = [_pmm(t, x) for t, x in zip(t_inv, rhs)]
    outs = [_bdot(jnp.concatenate([r, ark, -arb], axis=1), jnp.concatenate([hh, vv, _pair_blocks(uu)], axis=0))
            for r, ark, arb, hh, vv, uu in zip(rt, a_rk, a_rb, h, vb, u)]
    upd = [_bdot(jnp.concatenate([x, -y], axis=0).T, jnp.concatenate([vv, uu], axis=0))
           for x, y, vv, uu in zip(kh, bh, v, u)]
    for (n, sl), hh, dd in zip(chains, h, upd):
        decay_col = jnp.sum(jnp.where(eye2, rows[n]["g_last"][:, sl], 0.0), axis=1, keepdims=True)
        h_ref[n, sl.start // PAIR] = decay_col * hh + jnp.where(same_head, dd, 0.0)

    for n in range(n_rows):
        o = jnp.concatenate(outs[n * n_pair:(n + 1) * n_pair], axis=1)
        d = o - head_mean(o)
        on = d * lax.rsqrt(head_mean(d * d) + GN_EPS)
        o_ref[n] = ((on * lnw_ref[...] + lnb_ref[...] + rows[n]["bonus"]) * rows[n]["gate"]).astype(o_ref.dtype)


def _rwkv(zrw, w0, w2, a0, a2, g2, k_k, k_a, r_k, ln_w, ln_b):
    B, S, _ = zrw.shape
    C = RW_CHUNK
    nb = RW_ROWS
    hid = np.arange(PAIR) // RW_HEAD_DIM
    avg = jnp.asarray((hid[:, None] == hid[None, :]).astype(np.float32) / RW_HEAD_DIM)
    vec = lambda a: a.reshape(1, RW_WIDTH)
    fixed = lambda shape: pl.BlockSpec(shape, lambda b, c: (0,) * len(shape))
    return pl.pallas_call(
        _rwkv_kernel,
        grid=(B // nb, S // C),
        in_specs=[pl.BlockSpec((nb, C, RW_IN), lambda b, c: (b, c, 0)),
                  fixed((1, RW_WIDTH)), fixed((DECAY_LORA, RW_WIDTH)),
                  fixed((1, RW_WIDTH)), fixed((AAA_LORA, RW_WIDTH)),
                  fixed((GATE_LORA, RW_WIDTH)), fixed((1, RW_WIDTH)), fixed((1, RW_WIDTH)),
                  fixed((1, RW_WIDTH)), fixed((1, RW_WIDTH)), fixed((1, RW_WIDTH)),
                  fixed((PAIR, PAIR))],
        out_specs=pl.BlockSpec((nb, C, RW_WIDTH), lambda b, c: (b, c, 0)),
        out_shape=jax.ShapeDtypeStruct((B, S, RW_WIDTH), BF16),
        scratch_shapes=[pltpu.VMEM((nb, RW_WIDTH // PAIR, PAIR, PAIR), F32)],
        compiler_params=pltpu.CompilerParams(dimension_semantics=("parallel", "arbitrary"),
                                             vmem_limit_bytes=VMEM_LIMIT),
        name="rwkv",
    )(zrw, vec(w0), w2, vec(a0), a2, g2, vec(k_k), vec(k_a), vec(r_k), vec(ln_w), vec(ln_b), avg)


def _gelu_tanh(x):
    return 0.5 * x * (1.0 + jnp.tanh(math.sqrt(2.0 / math.pi) * (x + 0.044715 * (x * x * x))))


def _key_features(pos_hi, pos_lo, block, n, n_sel):
    lane = lax.broadcasted_iota(jnp.int32, (n, NSA_HEAD_DIM), 1)
    feat = jnp.where(lane == n_sel, pos_hi, jnp.where(lane == n_sel + 1, pos_lo, 0.0))
    return feat if block is None else jnp.where(lane == block, 1.0, feat)


def _nsa_prep_kernel(zkc_ref, zvc_ref, zks_ref, zvs_ref, zkw_ref, zvw_ref, pk_ref, pv_ref, kw1_ref,
                     kw2_ref, vw1_ref, vw2_ref, kc_ref, vct_ref, ksa_ref, kwa_ref, vst_ref, vwt_ref):
    S = zkc_ref.shape[1]
    n_grp = S // CMP_STRIDE
    Dh = NSA_HEAD_DIM
    half = CMP_BLOCK // 2
    n_sel = S // SEL_BLOCK
    jrow = lax.broadcasted_iota(jnp.int32, (n_grp, 1), 0)
    cmp_feat = _key_features((jrow >> 3).astype(F32),
                             ((jrow & 7) * CMP_STRIDE).astype(F32) + 0.5 * (CMP_BLOCK - 1),
                             None, n_grp, n_sel)
    for is_v, (z_ref, pos_ref, w1_ref, w2_ref) in enumerate(((zkc_ref, pk_ref, kw1_ref, kw2_ref),
                                                            (zvc_ref, pv_ref, vw1_ref, vw2_ref))):
        for g in range(NSA_KV_HEADS):
            lo = jnp.zeros((n_grp, Dh), F32)
            hi = jnp.zeros((n_grp, Dh), F32)
            for l in range(half):
                xs = z_ref[0, pl.ds(l, n_grp, stride=CMP_STRIDE), :]
                xg = xs[:, g * Dh:(g + 1) * Dh]
                lo = lo + _bdot(xg + pos_ref[l:l + 1, :], w1_ref[l * Dh:(l + 1) * Dh, :])
                hi = hi + _bdot(xg + pos_ref[half + l:half + l + 1, :],
                                w1_ref[(half + l) * Dh:(half + l + 1) * Dh, :])
            pre = lo + pltpu.roll(hi, n_grp - 1, 0)
            out = jnp.where(jrow < n_grp - 1, _bdot(_gelu_tanh(pre), w2_ref[...]), 0.0)
            if is_v:
                out_t = jnp.concatenate([out, jnp.zeros_like(out)], axis=1).T
                vct_ref[0, g] = out_t[0:Dh, :].astype(BF16)
            else:
                kc_ref[0, g] = jnp.concatenate([out, cmp_feat], axis=1)

    prow = lax.broadcasted_iota(jnp.int32, (S, 1), 0)
    p_hi, p_lo = (prow >> 7).astype(F32), (prow & (LANE - 1)).astype(F32)
    for z_ref, out_ref, block in ((zks_ref, ksa_ref, prow >> int(math.log2(SEL_BLOCK))),
                                  (zkw_ref, kwa_ref, None)):
        kfull = z_ref[0]
        key_feat = _key_features(p_hi, p_lo, block, S, n_sel)
        for g in range(NSA_KV_HEADS):
            out_ref[0, g] = jnp.concatenate([kfull[:, g * Dh:(g + 1) * Dh], key_feat], axis=1).astype(BF16)
    ones_row = (lax.broadcasted_iota(jnp.int32, (V_ROWS - Dh, TK), 0) == 0).astype(F32)
    for z_ref, out_ref in ((zvs_ref, vst_ref), (zvw_ref, vwt_ref)):
        for j in range(S // TK):
            vt = z_ref[0, j * TK:(j + 1) * TK, :].T
            out_ref[0, j] = jnp.concatenate(
                [piece for g in range(NSA_KV_HEADS) for piece in (vt[g * Dh:(g + 1) * Dh], ones_row)],
                axis=0).astype(BF16)


def _nsa_prep(zkv, pos_k, pos_v, kw1, kw2, vw1, vw2):
    B, S, _ = zkv.shape
    n_grp = S // CMP_STRIDE
    n_kt = S // TK
    Dh = NSA_HEAD_DIM
    G = NSA_KV_HEADS
    fixed = lambda shape: pl.BlockSpec(shape, lambda b: (0,) * len(shape))
    col = lambda c: pl.BlockSpec((1, S, NSA_KV_WIDTH), lambda b: (b, 0, c))
    whole = lambda shape: pl.BlockSpec((1,) + shape, lambda b: (b,) + (0,) * len(shape))
    shapes = [((G, n_grp, 2 * Dh), F32), ((G, Dh, n_grp), BF16), ((G, S, 2 * Dh), BF16),
              ((G, S, 2 * Dh), BF16), ((n_kt, G * V_ROWS, TK), BF16), ((n_kt, G * V_ROWS, TK), BF16)]
    return pl.pallas_call(
        _nsa_prep_kernel,
        grid=(B,),
        in_specs=[col(c) for c in range(6)] + [
            fixed((CMP_BLOCK, Dh)), fixed((CMP_BLOCK, Dh)),
            fixed((CMP_BLOCK * Dh, Dh)), fixed((Dh, Dh)),
            fixed((CMP_BLOCK * Dh, Dh)), fixed((Dh, Dh))],
        out_specs=[whole(s) for s, _ in shapes],
        out_shape=[jax.ShapeDtypeStruct((B,) + s, d) for s, d in shapes],
        compiler_params=pltpu.CompilerParams(dimension_semantics=("parallel",),
                                             vmem_limit_bytes=VMEM_LIMIT),
        name="nsa_prep",
    )(zkv, zkv, zkv, zkv, zkv, zkv, pos_k, pos_v, kw1, kw2, vw1, vw2)


def _nsa_attn_kernel(q_ref, kc_ref, vct_ref, ksa_ref, kwa_ref, vst_ref, vwt_ref, gl_ref, gb_ref,
                     ovlt_ref, slope_ref, o_ref, acc_ref, ot_ref):
    i = pl.program_id(1)
    Dh = NSA_HEAD_DIM
    R = NSA_REP
    N = R * TQ
    n_cmp_pad = kc_ref.shape[2]
    n_sel = ovlt_ref.shape[0]
    G = NSA_KV_HEADS
    nt = (((1,), (1,)), ((), ()))
    log2e = math.log2(math.e)
    t0 = i * TQ
    t_row = t0 + lax.broadcasted_iota(jnp.int32, (1, TQ), 1)
    c_col = lax.broadcasted_iota(jnp.int32, (TK, 1), 0)
    sgate_t = jax.nn.sigmoid(gl_ref[0] + gb_ref[...]).T
    lane_f = lax.broadcasted_iota(jnp.int32, (1, Dh), 1)
    heads = lambda x: jnp.concatenate([x] * R, axis=1)

    def key_dist(j):
        return t_row - (j * TK + c_col)

    def queries(g):
        parts = []
        for r in range(R):
            h = g * R + r
            sl = slope_ref[:, h:h + 1] * log2e
            feat = jnp.where(lane_f == n_sel, sl * LANE, jnp.where(lane_f == n_sel + 1, sl, 0.0))
            parts.append(jnp.concatenate(
                [q_ref[0, :, h * Dh:(h + 1) * Dh] * (Dh ** -0.5 * log2e), jnp.broadcast_to(feat, (TQ, Dh))],
                axis=1))
        return jnp.concatenate(parts, axis=0)

    sel_lanes = ((lax.broadcasted_iota(jnp.int32, (1, 2 * Dh), 1) >= Dh)
                 & (lax.broadcasted_iota(jnp.int32, (1, 2 * Dh), 1) < Dh + n_sel))
    qab, o_cmp = [], []
    for g in range(G):
        jc = lax.broadcasted_iota(jnp.int32, (n_cmp_pad, 1), 0)
        ok_c = (jc * CMP_STRIDE + (CMP_BLOCK - 1) <= t_row) & (jc < n_cmp_pad - 1)
        (q_hi, q_lo), (k_hi, k_lo) = _split_bf16(queries(g)), _split_bf16(kc_ref[0, g])
        s_c = (lax.dot_general(k_hi, q_hi, nt, preferred_element_type=F32)
               + lax.dot_general(k_hi, q_lo, nt, preferred_element_type=F32)
               + lax.dot_general(k_lo, q_hi, nt, preferred_element_type=F32)
               + heads(jnp.where(ok_c, 0.0, NEG_INF)))
        e_c = jnp.exp2(s_c - jnp.max(s_c, axis=0, keepdims=True))
        any_c = heads(t_row >= CMP_BLOCK - 1)
        p_c = e_c * jnp.where(any_c, 1.0 / jnp.sum(e_c, axis=0, keepdims=True), 0.0)
        o_cmp.append(jnp.dot(vct_ref[0, g], p_c.astype(BF16), preferred_element_type=F32))

        p_sum = p_c[:, 0:TQ]
        for r in range(1, R):
            p_sum = p_sum + p_c[:, r * TQ:(r + 1) * TQ]
        p_hi, p_lo = _split_bf16(p_sum)
        imp = (jnp.dot(ovlt_ref[...], p_hi, preferred_element_type=F32)
               + jnp.dot(ovlt_ref[...], p_lo, preferred_element_type=F32))
        kb = lax.broadcasted_iota(jnp.int32, (n_sel, 1), 0)
        kbf = kb.astype(F32)
        blk_t = t_row >> int(math.log2(SEL_BLOCK))
        forced = (kb == 0) | (kb == blk_t) | (kb == blk_t - 1)
        cur = jnp.where(forced, FORCE_SCORE, jnp.where(kb <= blk_t, imp, -FORCE_SCORE))
        sel_bias = jnp.full((n_sel, TQ), NEG_INF, F32)
        for _ in range(min(N_SELECT, n_sel)):
            mx = jnp.max(cur, axis=0, keepdims=True)
            first = jnp.min(jnp.where(cur == mx, kbf, float(n_sel)), axis=0, keepdims=True)
            hit = kbf == first
            sel_bias = jnp.where(hit, 0.0, sel_bias)
            cur = jnp.where(hit, -3e38, cur)
        sel_feat = jnp.concatenate([jnp.zeros((Dh, TQ), F32), sel_bias,
                                    jnp.zeros((Dh - n_sel, TQ), F32)], axis=0).T.astype(BF16)
        qab.append(jnp.where(sel_lanes, jnp.concatenate([sel_feat] * R, axis=0), q_hi))

    VR = vst_ref.shape[2] // G

    SEL, WIN = (ksa_ref, vst_ref, 0), (kwa_ref, vwt_ref, 1)

    def tile(j, branches):
        chains = [(b, h) for b in range(len(branches)) for h in range(NSA_HEADS)]
        ks = [[k_ref[0, g, pl.ds(pl.multiple_of(j * TK, TK), TK), :] for g in range(G)]
              for (k_ref, _, _), _, _ in branches]
        vts = [[vt_ref[0, j, g * VR:(g + 1) * VR, :] for g in range(G)]
               for (_, vt_ref, _), _, _ in branches]
        s = {(b, h): lax.dot_general(ks[b][h // R], qab[h // R][(h % R) * TQ:(h % R + 1) * TQ], nt,
                                     preferred_element_type=F32) for b, h in chains}
        s = {(b, h): x if branches[b][2] is None else x + branches[b][2] for (b, h), x in s.items()}
        m_new = {(b, h): jnp.maximum(branches[b][1][h], jnp.max(s[b, h], axis=0, keepdims=True))
                 for b, h in chains}
        alpha = {(b, h): jnp.exp2(branches[b][1][h] - m_new[b, h]) for b, h in chains}
        p = {c: jnp.exp2(s[c] - m_new[c]).astype(BF16) for c in chains}
        pv = {(b, h): jnp.dot(vts[b][h // R], p[b, h], preferred_element_type=F32) for b, h in chains}
        for b, h in chains:
            slot = branches[b][0][2]
            acc_ref[slot, h] = alpha[b, h] * acc_ref[slot, h] + pv[b, h]
        return tuple(tuple(m_new[b, h] for h in range(NSA_HEADS)) for b in range(len(branches)))

    def window_bias(j):
        d = key_dist(j)
        return jnp.where((d >= 0) & (d < WINDOW), 0.0, NEG_INF)

    init = (jnp.full((1, TQ), NEG_INF, F32),) * NSA_HEADS
    acc_ref[...] = jnp.zeros_like(acc_ref)
    causal = jnp.where(key_dist(i) >= 0, 0.0, NEG_INF)
    win_lo = jnp.maximum(i - WINDOW // TK, 0)
    m_sel = lax.fori_loop(0, win_lo, lambda j, m: tile(j, [(SEL, m, None)])[0], init)
    m_sel, m_win = lax.fori_loop(
        win_lo, i, lambda j, ms: tile(j, [(SEL, ms[0], None), (WIN, ms[1], window_bias(j))]), (m_sel, init))
    tile(i, [(SEL, m_sel, causal), (WIN, m_win, causal)])

    for h in range(NSA_HEADS):
        g, r = divmod(h, R)
        acc_s, acc_w = acc_ref[0, h], acc_ref[1, h]
        ot_ref[h * Dh:(h + 1) * Dh, :] = (
            sgate_t[3 * h:3 * h + 1, :] * o_cmp[g][:, r * TQ:(r + 1) * TQ]
            + sgate_t[3 * h + 1:3 * h + 2, :] * (acc_s[0:Dh] * (1.0 / acc_s[Dh:Dh + 1]))
            + sgate_t[3 * h + 2:3 * h + 3, :] * (acc_w[0:Dh] * (1.0 / acc_w[Dh:Dh + 1])))
    o_ref[0] = ot_ref[...].T.astype(o_ref.dtype)


def _nsa_attn(zq, kc, vct, ksa, kwa, vst, vwt, zgate, gate_b):
    B, S, _ = zq.shape
    n_sel = S // SEL_BLOCK
    n_cmp = (S - CMP_BLOCK) // CMP_STRIDE + 1
    n_cmp_pad = kc.shape[2]
    n_kt = S // TK
    G, Dh = NSA_KV_HEADS, NSA_HEAD_DIM
    cmp_start = np.arange(n_cmp) * CMP_STRIDE
    sel_start = np.arange(n_sel) * SEL_BLOCK
    overlap = np.clip(np.minimum(cmp_start[:, None] + CMP_BLOCK, sel_start[None, :] + SEL_BLOCK)
                      - np.maximum(cmp_start[:, None], sel_start[None, :]), 0, None) / CMP_BLOCK
    ovlt = np.zeros((n_sel, n_cmp_pad), np.float32)
    ovlt[:, :n_cmp] = overlap.T
    slopes = (2.0 ** (-8.0 * np.arange(1, NSA_HEADS + 1) / NSA_HEADS)).astype(np.float32).reshape(1, NSA_HEADS)
    gb = jnp.pad(gate_b, (0, LANE - N_GATE)).reshape(1, LANE)
    fixed = lambda shape: pl.BlockSpec(shape, lambda b, i: (0,) * len(shape))
    per_b = lambda shape: pl.BlockSpec((1,) + shape, lambda b, i: (b,) + (0,) * len(shape))
    return pl.pallas_call(
        _nsa_attn_kernel,
        grid=(B, S // TQ),
        in_specs=[pl.BlockSpec((1, TQ, NSA_WIDTH), lambda b, i: (b, i, 0)),
                  per_b((G, n_cmp_pad, 2 * Dh)), per_b((G, Dh, n_cmp_pad)),
                  per_b((G, S, 2 * Dh)), per_b((G, S, 2 * Dh)),
                  per_b((n_kt, G * V_ROWS, TK)), per_b((n_kt, G * V_ROWS, TK)),
                  pl.BlockSpec((1, TQ, LANE), lambda b, i: (b, i, 0)),
                  fixed((1, LANE)), fixed((n_sel, n_cmp_pad)), fixed((1, NSA_HEADS))],
        out_specs=pl.BlockSpec((1, TQ, NSA_WIDTH), lambda b, i: (b, i, 0)),
        out_shape=jax.ShapeDtypeStruct((B, S, NSA_WIDTH), BF16),
        scratch_shapes=[pltpu.VMEM((2, NSA_HEADS, V_ROWS, TQ), F32),
                        pltpu.VMEM((NSA_WIDTH, TQ), F32)],
        compiler_params=pltpu.CompilerParams(dimension_semantics=("parallel", "arbitrary"),
                                             vmem_limit_bytes=VMEM_LIMIT),
        name="nsa_attn",
    )(zq, kc, vct, ksa, kwa, vst, vwt, zgate, gb, jnp.asarray(ovlt, dtype=BF16), jnp.asarray(slopes))


def _mix_kernel(x_ref, yr_ref, yn_ref, zmg_ref, ur_ref, un_ref, wo_ref, gf_ref, wr_ref, br_ref,
                x1_ref, h2_ref, rt_ref, cnt_ref):
    tm = x_ref.shape[0]
    nt = (((1,), (1,)), ((), ()))
    chunks = [slice(c * TM_PROJ, (c + 1) * TM_PROJ) for c in range(tm // TM_PROJ)]
    up_r = [_bdot(yr_ref[c, :], ur_ref[...]) for c in chunks]
    up_n = [_bdot(yn_ref[c, :], un_ref[...]) for c in chunks]
    mixed = [jax.nn.sigmoid(zmg_ref[c, 0:D_MODEL].astype(F32)) * a
             + jax.nn.sigmoid(zmg_ref[c, D_MODEL:2 * D_MODEL].astype(F32)) * b
             for c, a, b in zip(chunks, up_r, up_n)]
    x1 = [x_ref[c, :] + _bdot(m, wo_ref[...]) for c, m in zip(chunks, mixed)]
    h2 = [_rms(v, gf_ref[...]) for v in x1]
    for c, v, h in zip(chunks, x1, h2):
        x1_ref[c, :] = v
        _store_row_tiles(h2_ref.at[c], h)
    n_row = wr_ref.shape[0]
    logits = jnp.concatenate([_dot3_nt(wr_ref[...], h) for h in h2], axis=1) + br_ref[...]
    row = lax.broadcasted_iota(jnp.int32, (n_row, 1), 0).astype(F32)
    gl = jnp.where(row < N_GROUPS, logits, NEG_INF)
    gmax = jnp.max(gl, axis=0, keepdims=True)
    g_sel = jnp.min(jnp.where(gl == gmax, row, float(n_row)), axis=0, keepdims=True)
    p_group = 1.0 / jnp.sum(jnp.exp(gl - gmax), axis=0, keepdims=True)
    e_row = row - N_GROUPS
    in_grp = ((e_row >= g_sel * EXPERTS_PER_GROUP) & (e_row < (g_sel + 1.0) * EXPERTS_PER_GROUP)
              & (e_row < N_EXPERTS))
    el = jnp.where(in_grp, logits, NEG_INF)
    m1 = jnp.max(el, axis=0, keepdims=True)
    i1 = jnp.min(jnp.where(el == m1, e_row, float(n_row)), axis=0, keepdims=True)
    el2 = jnp.where(e_row == i1, 2.0 * NEG_INF, el)
    m2 = jnp.max(el2, axis=0, keepdims=True)
    i2 = jnp.min(jnp.where(el2 == m2, e_row, float(n_row)), axis=0, keepdims=True)
    r2 = jnp.exp(m2 - m1)
    g1 = p_group / (1.0 + r2)
    g2 = p_group * r2 / (1.0 + r2)

    @pl.when(pl.program_id(0) == 0)
    def _():
        cnt_ref[...] = jnp.zeros_like(cnt_ref)

    pick1, pick2 = e_row == i1, e_row == i2
    both = pick1.astype(F32) + pick2.astype(F32)
    earlier = (lax.broadcasted_iota(jnp.int32, (tm, tm), 0)
               < lax.broadcasted_iota(jnp.int32, (tm, tm), 1)).astype(BF16)
    before = jnp.dot(both.astype(BF16), earlier, preferred_element_type=F32) + cnt_ref[:, 0:1]
    rank1 = jnp.sum(jnp.where(pick1, before, 0.0), axis=0, keepdims=True)
    rank2 = jnp.sum(jnp.where(pick2, before, 0.0), axis=0, keepdims=True)
    cnt_ref[...] = cnt_ref[...] + jnp.sum(both, axis=1, keepdims=True)
    rt_ref[...] = jnp.concatenate([i1, i2, g1, g2, rank1, rank2, jnp.zeros((2, tm), F32)], axis=0)


def _mix(x2d, y_rw, y_nsa, zmg, w_up_r, w_up_n, w_out, g_ffn, w_group, b_group, w_router, b_router):
    T = x2d.shape[0]
    tm = MIX_CHUNKS * TM_PROJ
    n_r = N_GROUPS + N_EXPERTS
    n_row = -(-n_r // 8) * 8
    wr = jnp.pad(jnp.concatenate([w_group, w_router], axis=1).T, ((0, n_row - n_r), (0, 0)))
    br = jnp.pad(jnp.concatenate([b_group, b_router]), (0, n_row - n_r)).reshape(n_row, 1)
    row = lambda i: (i, 0)
    fixed = lambda i: (0, 0)
    return pl.pallas_call(
        _mix_kernel,
        grid=(T // tm,),
        in_specs=[pl.BlockSpec((tm, D_MODEL), row), pl.BlockSpec((tm, RW_WIDTH), row),
                  pl.BlockSpec((tm, NSA_WIDTH), row), pl.BlockSpec((tm, 2 * D_MODEL), row),
                  pl.BlockSpec((RW_WIDTH, D_MODEL), fixed), pl.BlockSpec((NSA_WIDTH, D_MODEL), fixed),
                  pl.BlockSpec((D_MODEL, D_MODEL), fixed), pl.BlockSpec((1, D_MODEL), fixed),
                  pl.BlockSpec((n_row, D_MODEL), fixed), pl.BlockSpec((n_row, 1), fixed)],
        out_specs=[pl.BlockSpec((tm, D_MODEL), row), pl.BlockSpec((tm,) + ROW_TILE, lambda i: (i, 0, 0)),
                   pl.BlockSpec((8, tm), lambda i: (0, i)), pl.BlockSpec((n_row, LANE), fixed)],
        out_shape=[jax.ShapeDtypeStruct((T, D_MODEL), F32), jax.ShapeDtypeStruct((T,) + ROW_TILE, F32),
                   jax.ShapeDtypeStruct((8, T), F32), jax.ShapeDtypeStruct((n_row, LANE), F32)],
        compiler_params=pltpu.CompilerParams(dimension_semantics=("arbitrary",),
                                             vmem_limit_bytes=VMEM_LIMIT),
        name="mix",
    )(x2d, y_rw, y_nsa, zmg, w_up_r.astype(BF16), w_up_n.astype(BF16), w_out.astype(BF16),
      g_ffn.reshape(1, D_MODEL), wr, br)


def _route_tables(rt, cnt, T):
    n_rows = T * TOP_K + N_EXPERTS * ROW_BLOCK
    n_blk = n_rows // ROW_BLOCK
    counts = cnt[N_GROUPS:N_GROUPS + N_EXPERTS, 0].astype(jnp.int32)
    padded = (counts + ROW_BLOCK - 1) // ROW_BLOCK * ROW_BLOCK
    pends = jnp.cumsum(padded)
    pstarts = pends - padded
    expert = rt[0:TOP_K].astype(jnp.int32)
    rank = rt[2 * TOP_K:3 * TOP_K].astype(jnp.int32)
    seg_start = jnp.sum(jnp.where(expert[..., None] == jnp.arange(N_EXPERTS), pstarts, 0), axis=-1)
    dest = (seg_start + rank).T.reshape(T * TOP_K)
    gates = rt[TOP_K:2 * TOP_K].T
    blk_start = jnp.arange(n_blk) * ROW_BLOCK
    blk_expert = jnp.minimum(jnp.sum(pends[None, :] <= blk_start[:, None], axis=1), N_EXPERTS - 1)
    n_active = (pends[N_EXPERTS - 1:] // ROW_BLOCK).astype(jnp.int32)
    return dest.astype(jnp.int32), gates, blk_expert.astype(jnp.int32), n_active


ROW_TILE = (D_MODEL // LANE, LANE)


def _store_row_tiles(ref, x):
    for c in range(ROW_TILE[0]):
        ref[:, c, :] = x[:, c * LANE:(c + 1) * LANE]


def _load_row_tiles(ref, idx):
    return jnp.concatenate([ref[(*idx, slice(None), c, slice(None))] for c in range(ROW_TILE[0])], axis=1)


EXPERT_BUFS = 3
TOK_BITS = 14


def _row_info_kernel(dest_ref, fill_hbm, info_ref, sem, *, n_tok):
    fill = pltpu.make_async_copy(fill_hbm, info_ref, sem)
    fill.start()
    fill.wait()

    def body(t, carry):
        word = t + (t << TOK_BITS)
        for k in range(TOP_K):
            info_ref[dest_ref[t * TOP_K + k]] = word + ((k * n_tok) << TOK_BITS)
        return carry

    lax.fori_loop(0, n_tok, body, 0, unroll=8)


def _row_info(dest, n_tok):
    n_rows = n_tok * TOP_K + N_EXPERTS * ROW_BLOCK
    assert n_tok <= 1 << TOK_BITS and (n_tok * TOP_K + 2 * ROW_BLOCK) << TOK_BITS < 2 ** 31
    row = jnp.arange(n_rows, dtype=jnp.int32)
    spare = n_tok * TOP_K + ((row // ROW_BLOCK) % 2) * ROW_BLOCK + row % ROW_BLOCK
    return pl.pallas_call(
        functools.partial(_row_info_kernel, n_tok=n_tok),
        in_specs=[pl.BlockSpec(memory_space=pltpu.SMEM), pl.BlockSpec(memory_space=pl.ANY)],
        out_specs=pl.BlockSpec(memory_space=pltpu.SMEM),
        out_shape=jax.ShapeDtypeStruct((n_rows,), jnp.int32),
        scratch_shapes=[pltpu.SemaphoreType.DMA(())],
        name="row_info",
    )(dest, spare << TOK_BITS)


def _expert_kernel(be_ref, nact_ref, info_ref, h2_hbm, wgu_ref, wd_ref, eo_hbm,
                   xbuf, obuf, wgu_b, wd_b, gsem, ssem):
    i = pl.program_id(0)
    n_act = nact_ref[0]
    tok_mask = (1 << TOK_BITS) - 1

    def gather(blk, s):
        for r in range(ROW_BLOCK):
            tok = info_ref[blk * ROW_BLOCK + r] & tok_mask
            pltpu.make_async_copy(h2_hbm.at[tok], xbuf.at[s, r], gsem.at[s]).start()

    def scatter(blk, s):
        for r in range(ROW_BLOCK):
            row = info_ref[blk * ROW_BLOCK + r] >> TOK_BITS
            pltpu.make_async_copy(obuf.at[s, pl.ds(r, 1)], eo_hbm.at[pl.ds(row, 1)], ssem.at[s]).start()

    def drain_gather(s):
        pltpu.make_async_copy(h2_hbm.at[pl.ds(0, ROW_BLOCK)], xbuf.at[s], gsem.at[s]).wait()

    def drain_scatter(s):
        pltpu.make_async_copy(obuf.at[s], eo_hbm.at[pl.ds(0, ROW_BLOCK)], ssem.at[s]).wait()

    n_buf = xbuf.shape[0]

    def block(s, first):
        drain_gather(s)
        gather(jnp.minimum(i + 2, n_act - 1), (s + 2) % n_buf)
        if not first:
            scatter(i - 1, (s + 2) % n_buf)
        gu = jnp.dot(_load_row_tiles(xbuf, (s,)).astype(BF16), wgu_b[...], preferred_element_type=F32)
        gate_h, up_h = gu[:, :D_EXPERT], gu[:, D_EXPERT:]
        mid = gate_h * jax.nn.sigmoid(gate_h) * up_h
        obuf[s] = jnp.dot(mid.astype(BF16), wd_b[...], preferred_element_type=F32)

    @pl.when(jnp.logical_and(i < n_act, jnp.logical_or(i == 0, be_ref[i] != be_ref[jnp.maximum(i - 1, 0)])))
    def _():
        wgu_b[...] = wgu_ref[0].astype(BF16)
        wd_b[...] = wd_ref[0].astype(BF16)

    @pl.when(i == 0)
    def _():
        gather(0, 0)
        gather(jnp.minimum(1, n_act - 1), 1)
        block(0, first=True)

    for s in range(n_buf):
        mine = i % n_buf == s

        @pl.when(jnp.logical_and(mine, jnp.logical_and(i >= 1, i < n_act)))
        def _():
            @pl.when(i >= n_buf)
            def _():
                drain_scatter(s)
            block(s, first=False)

        @pl.when(jnp.logical_and(mine, i == n_act - 1))
        def _():
            for other in ((s + 1) % n_buf, (s + 2) % n_buf):
                drain_gather(other)

            @pl.when(i >= 2)
            def _():
                drain_scatter((s + 1) % n_buf)

            @pl.when(i >= 1)
            def _():
                drain_scatter((s + 2) % n_buf)
            scatter(i, s)
            drain_scatter(s)
            obuf[s] = jnp.zeros_like(obuf[s])
            for half in range(2):
                spare = eo_hbm.at[pl.ds(eo_hbm.shape[0] - (2 - half) * ROW_BLOCK, ROW_BLOCK)]
                pltpu.make_async_copy(obuf.at[s], spare, ssem.at[s]).start()
            for half in range(2):
                drain_scatter(s)


def _experts(h2, info, blk_expert, n_active, w_gate_up, w_down):
    n_tok = h2.shape[0]
    n_blk = blk_expert.shape[0]
    grid_spec = pltpu.PrefetchScalarGridSpec(
        num_scalar_prefetch=3,
        grid=(n_blk,),
        in_specs=[pl.BlockSpec(memory_space=pl.ANY),
                  pl.BlockSpec((1, D_MODEL, 2 * D_EXPERT), lambda i, be, na, info: (be[i], 0, 0)),
                  pl.BlockSpec((1, D_EXPERT, D_MODEL), lambda i, be, na, info: (be[i], 0, 0))],
        out_specs=pl.BlockSpec(memory_space=pl.ANY),
        scratch_shapes=[pltpu.VMEM((EXPERT_BUFS, ROW_BLOCK) + ROW_TILE, F32),
                        pltpu.VMEM((EXPERT_BUFS, ROW_BLOCK, D_MODEL), F32),
                        pltpu.VMEM((D_MODEL, 2 * D_EXPERT), BF16), pltpu.VMEM((D_EXPERT, D_MODEL), BF16),
                        pltpu.SemaphoreType.DMA((EXPERT_BUFS,)), pltpu.SemaphoreType.DMA((EXPERT_BUFS,))])
    return pl.pallas_call(
        _expert_kernel,
        grid_spec=grid_spec,
        out_shape=jax.ShapeDtypeStruct((n_tok * TOP_K + 2 * ROW_BLOCK, D_MODEL), F32),
        compiler_params=pltpu.CompilerParams(dimension_semantics=("arbitrary",),
                                             vmem_limit_bytes=VMEM_LIMIT),
        name="experts",
    )(blk_expert, n_active, info, h2, w_gate_up, w_down)


def _ple_kernel(x1_ref, g_ref, p_ref, *rest, last_layer):
    eo_refs, (wpp_ref, gpn_ref, ggi_ref, wpg_ref, gfin_ref, y_ref) = rest[:TOP_K], rest[TOP_K:]
    tm = x1_ref.shape[0]
    chunks = [slice(c * TM_PROJ, (c + 1) * TM_PROJ) for c in range(tm // TM_PROJ)]
    x2 = []
    for c in chunks:
        moe = g_ref[c, 0:1] * eo_refs[0][c, :]
        for k in range(1, TOP_K):
            moe = moe + g_ref[c, k:k + 1] * eo_refs[k][c, :]
        x2.append(x1_ref[c, :] + moe)
    e = [_rms(_bdot(p_ref[c, :], wpp_ref[...]), gpn_ref[...]) for c in chunks]
    gate = [jax.nn.sigmoid(_bdot(_rms(v, ggi_ref[...]), wpg_ref[...])) for v in x2]
    for c, v, g, ee in zip(chunks, x2, gate, e):
        x3 = v + g * ee
        y_ref[c, :] = _rms(x3, gfin_ref[...]) if last_layer else x3


def _ple(x1, gates, expert_out, p2d, w_pp, g_pn, g_gi, w_pg, g_final, last_layer):
    T = x1.shape[0]
    tm = MIX_CHUNKS * TM_PROJ
    nt = T // tm
    row = lambda i: (i, 0)
    fixed = lambda i: (0, 0)
    vec = lambda a: a.reshape(1, D_MODEL)
    slot_rows = [pl.BlockSpec((tm, D_MODEL), functools.partial(lambda i, k: (i + k * nt, 0), k=k))
                 for k in range(TOP_K)]
    return pl.pallas_call(
        functools.partial(_ple_kernel, last_layer=last_layer),
        grid=(nt,),
        in_specs=[pl.BlockSpec((tm, D_MODEL), row), pl.BlockSpec((tm, TOP_K), row),
                  pl.BlockSpec((tm, PLE_DIM), row)] + slot_rows + [
                  pl.BlockSpec((PLE_DIM, D_MODEL), fixed), pl.BlockSpec((1, D_MODEL), fixed),
                  pl.BlockSpec((1, D_MODEL), fixed), pl.BlockSpec((D_MODEL, D_MODEL), fixed),
                  pl.BlockSpec((1, D_MODEL), fixed)],
        out_specs=pl.BlockSpec((tm, D_MODEL), row),
        out_shape=jax.ShapeDtypeStruct((T, D_MODEL), F32),
        compiler_params=pltpu.CompilerParams(dimension_semantics=("parallel",),
                                             vmem_limit_bytes=VMEM_LIMIT),
        name="ple",
    )(x1, gates, p2d, *([expert_out] * TOP_K), w_pp.astype(BF16), vec(g_pn), vec(g_gi),
      w_pg.astype(BF16), vec(g_final))


def kernel(x, p, g_mix, w_in, mu_shift, rw_w0, rw_w2, rw_a0, rw_a2, rw_g2, rw_k_k, rw_k_a, rw_r_k, rw_ln_w, rw_ln_b, cmp_pos_k, cmp_pos_v, cmp_k_w1, cmp_k_w2, cmp_v_w1, cmp_v_w2, nsa_gate_b, w_up_rwkv, w_up_nsa, w_out, g_ffn, w_group, b_group, w_router, b_router, w_exp_gate_up, w_exp_down, w_ple_proj, g_ple_norm, g_ple_gate_in, w_ple_gate, g_final):
    B, S, D = x.shape
    T = B * S
    depth = p.shape[0]
    xc = x.reshape(T, D)
    for i in range(depth):
        zrw, zq, zkv, zmg, zgate = _proj(xc, g_mix[i], w_in[i], mu_shift[i], S)
        y_rw = _rwkv(zrw.reshape(B, S, RW_IN), rw_w0[i], rw_w2[i], rw_a0[i], rw_a2[i], rw_g2[i],
                     rw_k_k[i], rw_k_a[i], rw_r_k[i], rw_ln_w[i], rw_ln_b[i])
        zkv3 = zkv.reshape(B, S, KV_IN)
        nsa_kv = _nsa_prep(zkv3, cmp_pos_k[i], cmp_pos_v[i], cmp_k_w1[i], cmp_k_w2[i],
                           cmp_v_w1[i], cmp_v_w2[i])
        y_nsa = _nsa_attn(zq.reshape(B, S, NSA_WIDTH), *nsa_kv, zgate.reshape(B, S, LANE),
                          nsa_gate_b[i])
        x1, h2, rt, cnt = _mix(xc, y_rw.reshape(T, RW_WIDTH), y_nsa.reshape(T, NSA_WIDTH), zmg,
                               w_up_rwkv[i], w_up_nsa[i], w_out[i], g_ffn[i], w_group[i], b_group[i],
                               w_router[i], b_router[i])
        dest, gates, blk_expert, n_active = _route_tables(rt, cnt, T)
        expert_out = _experts(h2, _row_info(dest, T), blk_expert, n_active, w_exp_gate_up[i],
                              w_exp_down[i])
        xc = _ple(x1, gates, expert_out, p[i].reshape(T, PLE_DIM), w_ple_proj[i], g_ple_norm[i],
                  g_ple_gate_in[i], w_ple_gate[i], g_final, i == depth - 1)
    return xc.reshape(B, S, D)
```

```python
import functools
import math

import jax
import jax.numpy as jnp
import numpy as np
from jax import lax
from jax.experimental import pallas as pl
from jax.experimental.pallas import tpu as pltpu

F32 = jnp.float32
BF16 = jnp.bfloat16

D_MODEL = 1024
RW_HEADS = 8
RW_HEAD_DIM = 64
RW_WIDTH = 512
DECAY_LORA = 64
AAA_LORA = 64
GATE_LORA = 128
GN_EPS = 64e-5
RW_IN = 3 * RW_WIDTH + DECAY_LORA + AAA_LORA + GATE_LORA

NSA_HEADS = 8
NSA_KV_HEADS = 2
NSA_REP = NSA_HEADS // NSA_KV_HEADS
NSA_HEAD_DIM = 64
NSA_WIDTH = 512
NSA_KV_WIDTH = 128
CMP_BLOCK = 32
CMP_STRIDE = 16
SEL_BLOCK = 64
N_SELECT = 8
WINDOW = 512
N_NSA_BRANCH = 3
FORCE_SCORE = 1e6
NEG_INF = -1e30

N_GROUPS = 4
EXPERTS_PER_GROUP = 8
N_EXPERTS = 32
TOP_K = 2
D_EXPERT = 512
ROW_BLOCK = 128
PLE_DIM = 256
NORM_EPS = 1e-6

N_GATE = N_NSA_BRANCH * NSA_HEADS
ATT_IN = NSA_WIDTH + 6 * NSA_KV_WIDTH
KV_OFF = RW_IN + NSA_WIDTH
KV_IN = 6 * NSA_KV_WIDTH
GATE_OFF = RW_IN + ATT_IN
MERGE_OFF = GATE_OFF + N_GATE
LANE = 128

RW_CHUNK = 64
RW_SUB = 16
RW_ROWS = 4
TQ = 256
TK = 256
V_ROWS = NSA_HEAD_DIM + 16
TM_PROJ = 256
MIX_CHUNKS = 4
VMEM_LIMIT = 56 * 1024 * 1024


def _bdot(a, b):
    return jnp.dot(a.astype(BF16), b.astype(BF16), preferred_element_type=F32)


def _bdot_nt(a, b):
    return lax.dot_general(a.astype(BF16), b.astype(BF16), (((1,), (1,)), ((), ())),
                           preferred_element_type=F32)


def _split_bf16(x):
    hi = x.astype(BF16)
    return hi, (x - hi.astype(F32)).astype(BF16)


def _dot3_nt(a, b):
    (a_hi, a_lo), (b_hi, b_lo) = _split_bf16(a), _split_bf16(b)
    return _bdot_nt(a_hi, b_hi) + _bdot_nt(a_hi, b_lo) + _bdot_nt(a_lo, b_hi)


def _rms(x, g):
    return x * lax.rsqrt(jnp.mean(x * x, axis=-1, keepdims=True) + NORM_EPS) * g


def _proj_kernel(x_ref, g_ref, w_ref, mu_ref, zrw_ref, zq_ref, zkv_ref, zmg_ref, zgate_ref,
                 carry_ref, *, tiles_per_seq):
    i = pl.program_id(0)
    tm = x_ref.shape[0]

    @pl.when(i % tiles_per_seq == 0)
    def _():
        carry_ref[...] = jnp.zeros_like(carry_ref)

    h = _rms(x_ref[...], g_ref[...]).astype(BF16)
    z = jnp.dot(h, w_ref[:, 0:RW_IN], preferred_element_type=F32)
    row = lax.broadcasted_iota(jnp.int32, (tm, 1), 0)
    prev = jnp.where(row == 0, carry_ref[7:8, :], pltpu.roll(z, 1, 0))
    carry_ref[...] = z[tm - 8:tm, :]
    zrw_ref[...] = z + (prev - z) * mu_ref[...]
    zq_ref[...] = jnp.dot(h, w_ref[:, RW_IN:KV_OFF], preferred_element_type=F32)
    zkv_ref[...] = jnp.dot(h, w_ref[:, KV_OFF:GATE_OFF], preferred_element_type=F32)
    zmg_ref[...] = jnp.dot(h, w_ref[:, GATE_OFF:GATE_OFF + 2 * D_MODEL],
                           preferred_element_type=F32).astype(zmg_ref.dtype)
    zgate_ref[...] = jnp.dot(h, w_ref[:, GATE_OFF + 2 * D_MODEL:], preferred_element_type=F32)


def _proj(x2d, g_mix, w_in, mu, seq):
    T = x2d.shape[0]
    tm = TM_PROJ
    wp = jnp.concatenate(
        [w_in[:, :GATE_OFF], w_in[:, MERGE_OFF:],
         jnp.pad(w_in[:, GATE_OFF:MERGE_OFF], ((0, 0), (0, LANE - N_GATE)))], axis=1).astype(BF16)
    npad = wp.shape[1]
    row = lambda i: (i, 0)
    fixed = lambda i: (0, 0)
    return pl.pallas_call(
        functools.partial(_proj_kernel, tiles_per_seq=seq // tm),
        grid=(T // tm,),
        in_specs=[pl.BlockSpec((tm, D_MODEL), row), pl.BlockSpec((1, D_MODEL), fixed),
                  pl.BlockSpec((D_MODEL, npad), fixed), pl.BlockSpec((1, RW_IN), fixed)],
        out_specs=[pl.BlockSpec((tm, RW_IN), row), pl.BlockSpec((tm, NSA_WIDTH), row),
                   pl.BlockSpec((tm, KV_IN), row), pl.BlockSpec((tm, 2 * D_MODEL), row),
                   pl.BlockSpec((tm, LANE), row)],
        out_shape=[jax.ShapeDtypeStruct((T, RW_IN), F32), jax.ShapeDtypeStruct((T, NSA_WIDTH), F32),
                   jax.ShapeDtypeStruct((T, KV_IN), F32), jax.ShapeDtypeStruct((T, 2 * D_MODEL), BF16),
                   jax.ShapeDtypeStruct((T, LANE), F32)],
        scratch_shapes=[pltpu.VMEM((8, RW_IN), F32)],
        compiler_params=pltpu.CompilerParams(dimension_semantics=("arbitrary",),
                                             vmem_limit_bytes=VMEM_LIMIT),
        name="proj",
    )(x2d, g_mix.reshape(1, D_MODEL), wp, mu.reshape(1, RW_IN))


PAIR = 2 * RW_HEAD_DIM


def _pair_blocks(x):
    low = lax.broadcasted_iota(jnp.int32, (1, PAIR), 1) < RW_HEAD_DIM
    return jnp.concatenate([jnp.where(low, x, 0.0), jnp.where(low, 0.0, x)], axis=0)


def _pmm(a, b):
    return _bdot(a, _pair_blocks(b))


def _unit_lower_inverse(a_strict, sub_mask, eye):
    ad = [jnp.where(sub_mask, a, 0.0) for a in a_strict]
    ao = [a - d for a, d in zip(a_strict, ad)]
    td = [eye - d for d in ad]
    pw = ad
    for _ in range(int(math.log2(RW_SUB)) - 1):
        pw = [_pmm(x, x) for x in pw]
        td = [_pmm(t, eye + x) for t, x in zip(td, pw)]
    n = [_pmm(t, o) for t, o in zip(td, ao)]
    t = [eye - x for x in n]
    pw = n
    for _ in range(int(math.log2(RW_CHUNK // RW_SUB)) - 1):
        pw = [_pmm(x, x) for x in pw]
        t = [_pmm(a, eye + x) for a, x in zip(t, pw)]
    return [_pmm(a, d) for a, d in zip(t, td)]


def _rwkv_kernel(z_ref, w0_ref, w2_ref, a0_ref, a2_ref, g2_ref, kk_ref, ka_ref, rk_ref, lnw_ref,
                 lnb_ref, avg_ref, o_ref, h_ref):
    c = pl.program_id(1)
    C = RW_CHUNK
    n_pair = RW_WIDTH // PAIR
    nt = (((1,), (1,)), ((), ()))

    @pl.when(c == 0)
    def _():
        h_ref[...] = jnp.zeros_like(h_ref)

    ti = lax.broadcasted_iota(jnp.int32, (C, 1), 0)
    si = lax.broadcasted_iota(jnp.int32, (1, PAIR), 1) & (RW_HEAD_DIM - 1)
    incl, strict = ti >= si, ti > si
    eye = (ti == si).astype(F32)
    sub_shift = int(math.log2(RW_SUB))
    sub_mask = (ti >> sub_shift) == (si >> sub_shift)
    row2 = lax.broadcasted_iota(jnp.int32, (PAIR, 1), 0)
    col2 = lax.broadcasted_iota(jnp.int32, (1, PAIR), 1)
    same_head = (row2 < RW_HEAD_DIM) == (col2 < RW_HEAD_DIM)
    eye2 = row2 == col2
    tri = (lax.broadcasted_iota(jnp.int32, (C, C), 0)
           >= lax.broadcasted_iota(jnp.int32, (C, C), 1)).astype(BF16)

    def head_mean(x):
        xs = jnp.concatenate([x[:, p * PAIR:(p + 1) * PAIR] for p in range(n_pair)], axis=0)
        ms = _bdot(xs, avg_ref[...])
        return jnp.concatenate([ms[p * C:(p + 1) * C] for p in range(n_pair)], axis=1)

    n_rows = z_ref.shape[0]
    rows = []
    for n in range(n_rows):
        z = z_ref[n]
        zr, zk, zv = z[:, 0:512], z[:, 512:1024], z[:, 1024:1536]
        zw, za, zg = z[:, 1536:1600], z[:, 1600:1664], z[:, 1664:1792]
        w_raw = w0_ref[...] + _bdot(jnp.tanh(zw), w2_ref[...])
        logw = -jax.nn.sigmoid(w_raw) * math.exp(-0.5)
        a = jax.nn.sigmoid(a0_ref[...] + _bdot(za, a2_ref[...]))
        gate = _bdot(jax.nn.sigmoid(zg), g2_ref[...])
        kk = zk * kk_ref[...]
        kk = kk / jnp.maximum(jnp.sqrt(head_mean(kk * kk) * RW_HEAD_DIM), 1e-12)
        k = zk * (1.0 + (a - 1.0) * ka_ref[...])
        b = kk * a

        w_hi = logw.astype(BF16)
        w_lo = (logw - w_hi.astype(F32)).astype(BF16)
        cum = (jnp.dot(tri, w_hi, preferred_element_type=F32)
               + jnp.dot(tri, w_lo, preferred_element_type=F32))
        cum_last = cum[C - 1:C, :]
        g_inv = jnp.exp(-cum)
        g_end = jnp.exp(cum_last - cum)
        rows.append(dict(rt=zr * jnp.exp(cum), kt=k * g_inv, bt=b * g_inv, qt=kk * jnp.exp(cum - logw),
                         kh=k * g_end, bh=b * g_end, v=zv, g_last=jnp.exp(cum_last), gate=gate,
                         bonus=head_mean(zr * k * rk_ref[...]) * RW_HEAD_DIM * zv))

    chains = [(n, slice(p * PAIR, (p + 1) * PAIR)) for n in range(n_rows) for p in range(n_pair)]
    part = lambda name: [rows[n][name][:, sl] for n, sl in chains]
    qt, rt, kt, bt, kh, bh, v = (part(x) for x in ("qt", "rt", "kt", "bt", "kh", "bh", "v"))
    lhs = [jnp.concatenate([q, r], axis=0).astype(BF16) for q, r in zip(qt, rt)]
    ab = [lax.dot_general(l, _pair_blocks(x).astype(BF16), nt, preferred_element_type=F32)
          for l, x in zip(lhs, bt)]
    ak = [lax.dot_general(l, _pair_blocks(x).astype(BF16), nt, preferred_element_type=F32)
          for l, x in zip(lhs, kt)]
    a_kb = [jnp.where(strict, x[0:C], 0.0) for x in ab]
    a_rb = [jnp.where(incl, x[C:2 * C], 0.0) for x in ab]
    a_kk = [jnp.where(strict, x[0:C], 0.0) for x in ak]
    a_rk = [jnp.where(incl, x[C:2 * C], 0.0) for x in ak]
    t_inv = _unit_lower_inverse(a_kb, sub_mask, eye)

    h = [h_ref[n, sl.start // PAIR] for n, sl in chains]
    vb = [_pair_blocks(x) for x in v]
    rhs = [_bdot(jnp.concatenate([q, akk], axis=1), jnp.concatenate([hh, vv], axis=0))
           for q, akk, hh, vv in zip(qt, a_kk, h, vb)]
    u = [_pmm(t, x) for t, x in zip(t_inv, rhs)]
    outs = [_bdot(jnp.concatenate([r, ark, -arb], axis=1), jnp.concatenate([hh, vv, _pair_blocks(uu)], axis=0))
            for r, ark, arb, hh, vv, uu in zip(rt, a_rk, a_rb, h, vb, u)]
    upd = [_bdot(jnp.concatenate([x, -y], axis=0).T, jnp.concatenate([vv, uu], axis=0))
           for x, y, vv, uu in zip(kh, bh, v, u)]
    for (n, sl), hh, dd in zip(chains, h, upd):
        decay_col = jnp.sum(jnp.where(eye2, rows[n]["g_last"][:, sl], 0.0), axis=1, keepdims=True)
        h_ref[n, sl.start // PAIR] = decay_col * hh + jnp.where(same_head, dd, 0.0)

    for n in range(n_rows):
        o = jnp.concatenate(outs[n * n_pair:(n + 1) * n_pair], axis=1)
        d = o - head_mean(o)
        on = d * lax.rsqrt(head_mean(d * d) + GN_EPS)
        o_ref[n] = ((on * lnw_ref[...] + lnb_ref[...] + rows[n]["bonus"]) * rows[n]["gate"]).astype(o_ref.dtype)


def _rwkv(zrw, w0, w2, a0, a2, g2, k_k, k_a, r_k, ln_w, ln_b):
    B, S, _ = zrw.shape
    C = RW_CHUNK
    nb = RW_ROWS
    hid = np.arange(PAIR) // RW_HEAD_DIM
    avg = jnp.asarray((hid[:, None] == hid[None, :]).astype(np.float32) / RW_HEAD_DIM)
    vec = lambda a: a.reshape(1, RW_WIDTH)
    fixed = lambda shape: pl.BlockSpec(shape, lambda b, c: (0,) * len(shape))
    return pl.pallas_call(
        _rwkv_kernel,
        grid=(B // nb, S // C),
        in_specs=[pl.BlockSpec((nb, C, RW_IN), lambda b, c: (b, c, 0)),
                  fixed((1, RW_WIDTH)), fixed((DECAY_LORA, RW_WIDTH)),
                  fixed((1, RW_WIDTH)), fixed((AAA_LORA, RW_WIDTH)),
                  fixed((GATE_LORA, RW_WIDTH)), fixed((1, RW_WIDTH)), fixed((1, RW_WIDTH)),
                  fixed((1, RW_WIDTH)), fixed((1, RW_WIDTH)), fixed((1, RW_WIDTH)),
                  fixed((PAIR, PAIR))],
        out_specs=pl.BlockSpec((nb, C, RW_WIDTH), lambda b, c: (b, c, 0)),
        out_shape=jax.ShapeDtypeStruct((B, S, RW_WIDTH), BF16),
        scratch_shapes=[pltpu.VMEM((nb, RW_WIDTH // PAIR, PAIR, PAIR), F32)],
        compiler_params=pltpu.CompilerParams(dimension_semantics=("parallel", "arbitrary"),
                                             vmem_limit_bytes=VMEM_LIMIT),
        name="rwkv",
    )(zrw, vec(w0), w2, vec(a0), a2, g2, vec(k_k), vec(k_a), vec(r_k), vec(ln_w), vec(ln_b), avg)


def _gelu_tanh(x):
    return 0.5 * x * (1.0 + jnp.tanh(math.sqrt(2.0 / math.pi) * (x + 0.044715 * (x * x * x))))


def _key_features(pos_hi, pos_lo, block, n, n_sel):
    lane = lax.broadcasted_iota(jnp.int32, (n, NSA_HEAD_DIM), 1)
    feat = jnp.where(lane == n_sel, pos_hi, jnp.where(lane == n_sel + 1, pos_lo, 0.0))
    return feat if block is None else jnp.where(lane == block, 1.0, feat)


def _nsa_prep_kernel(zkc_ref, zvc_ref, zks_ref, zvs_ref, zkw_ref, zvw_ref, pk_ref, pv_ref, kw1_ref,
                     kw2_ref, vw1_ref, vw2_ref, kc_ref, vct_ref, ksa_ref, kwa_ref, vst_ref, vwt_ref):
    S = zkc_ref.shape[1]
    n_grp = S // CMP_STRIDE
    Dh = NSA_HEAD_DIM
    half = CMP_BLOCK // 2
    n_sel = S // SEL_BLOCK
    jrow = lax.broadcasted_iota(jnp.int32, (n_grp, 1), 0)
    cmp_feat = _key_features((jrow >> 3).astype(F32),
                             ((jrow & 7) * CMP_STRIDE).astype(F32) + 0.5 * (CMP_BLOCK - 1),
                             None, n_grp, n_sel)
    for is_v, (z_ref, pos_ref, w1_ref, w2_ref) in enumerate(((zkc_ref, pk_ref, kw1_ref, kw2_ref),
                                                            (zvc_ref, pv_ref, vw1_ref, vw2_ref))):
        for g in range(NSA_KV_HEADS):
            lo = jnp.zeros((n_grp, Dh), F32)
            hi = jnp.zeros((n_grp, Dh), F32)
            for l in range(half):
                xs = z_ref[0, pl.ds(l, n_grp, stride=CMP_STRIDE), :]
                xg = xs[:, g * Dh:(g + 1) * Dh]
                lo = lo + _bdot(xg + pos_ref[l:l + 1, :], w1_ref[l * Dh:(l + 1) * Dh, :])
                hi = hi + _bdot(xg + pos_ref[half + l:half + l + 1, :],
                                w1_ref[(half + l) * Dh:(half + l + 1) * Dh, :])
            pre = lo + pltpu.roll(hi, n_grp - 1, 0)
            out = jnp.where(jrow < n_grp - 1, _bdot(_gelu_tanh(pre), w2_ref[...]), 0.0)
            if is_v:
                out_t = jnp.concatenate([out, jnp.zeros_like(out)], axis=1).T
                vct_ref[0, g] = out_t[0:Dh, :].astype(BF16)
            else:
                kc_ref[0, g] = jnp.concatenate([out, cmp_feat], axis=1)

    prow = lax.broadcasted_iota(jnp.int32, (S, 1), 0)
    p_hi, p_lo = (prow >> 7).astype(F32), (prow & (LANE - 1)).astype(F32)
    for z_ref, out_ref, block in ((zks_ref, ksa_ref, prow >> int(math.log2(SEL_BLOCK))),
                                  (zkw_ref, kwa_ref, None)):
        kfull = z_ref[0]
        key_feat = _key_features(p_hi, p_lo, block, S, n_sel)
        for g in range(NSA_KV_HEADS):
            out_ref[0, g] = jnp.concatenate([kfull[:, g * Dh:(g + 1) * Dh], key_feat], axis=1).astype(BF16)
    ones_row = (lax.broadcasted_iota(jnp.int32, (V_ROWS - Dh, TK), 0) == 0).astype(F32)
    for z_ref, out_ref in ((zvs_ref, vst_ref), (zvw_ref, vwt_ref)):
        for j in range(S // TK):
            vt = z_ref[0, j * TK:(j + 1) * TK, :].T
            out_ref[0, j] = jnp.concatenate(
                [piece for g in range(NSA_KV_HEADS) for piece in (vt[g * Dh:(g + 1) * Dh], ones_row)],
                axis=0).astype(BF16)


def _nsa_prep(zkv, pos_k, pos_v, kw1, kw2, vw1, vw2):
    B, S, _ = zkv.shape
    n_grp = S // CMP_STRIDE
    n_kt = S // TK
    Dh = NSA_HEAD_DIM
    G = NSA_KV_HEADS
    fixed = lambda shape: pl.BlockSpec(shape, lambda b: (0,) * len(shape))
    col = lambda c: pl.BlockSpec((1, S, NSA_KV_WIDTH), lambda b: (b, 0, c))
    whole = lambda shape: pl.BlockSpec((1,) + shape, lambda b: (b,) + (0,) * len(shape))
    shapes = [((G, n_grp, 2 * Dh), F32), ((G, Dh, n_grp), BF16), ((G, S, 2 * Dh), BF16),
              ((G, S, 2 * Dh), BF16), ((n_kt, G * V_ROWS, TK), BF16), ((n_kt, G * V_ROWS, TK), BF16)]
    return pl.pallas_call(
        _nsa_prep_kernel,
        grid=(B,),
        in_specs=[col(c) for c in range(6)] + [
            fixed((CMP_BLOCK, Dh)), fixed((CMP_BLOCK, Dh)),
            fixed((CMP_BLOCK * Dh, Dh)), fixed((Dh, Dh)),
            fixed((CMP_BLOCK * Dh, Dh)), fixed((Dh, Dh))],
        out_specs=[whole(s) for s, _ in shapes],
        out_shape=[jax.ShapeDtypeStruct((B,) + s, d) for s, d in shapes],
        compiler_params=pltpu.CompilerParams(dimension_semantics=("parallel",),
                                             vmem_limit_bytes=VMEM_LIMIT),
        name="nsa_prep",
    )(zkv, zkv, zkv, zkv, zkv, zkv, pos_k, pos_v, kw1, kw2, vw1, vw2)


def _nsa_attn_kernel(q_ref, kc_ref, vct_ref, ksa_ref, kwa_ref, vst_ref, vwt_ref, gl_ref, gb_ref,
                     ovlt_ref, slope_ref, o_ref, acc_ref, ot_ref):
    i = pl.program_id(1)
    Dh = NSA_HEAD_DIM
    R = NSA_REP
    N = R * TQ
    n_cmp_pad = kc_ref.shape[2]
    n_sel = ovlt_ref.shape[0]
    G = NSA_KV_HEADS
    nt = (((1,), (1,)), ((), ()))
    log2e = math.log2(math.e)
    t0 = i * TQ
    t_row = t0 + lax.broadcasted_iota(jnp.int32, (1, TQ), 1)
    c_col = lax.broadcasted_iota(jnp.int32, (TK, 1), 0)
    sgate_t = jax.nn.sigmoid(gl_ref[0] + gb_ref[...]).T
    lane_f = lax.broadcasted_iota(jnp.int32, (1, Dh), 1)
    heads = lambda x: jnp.concatenate([x] * R, axis=1)

    def key_dist(j):
        return t_row - (j * TK + c_col)

    def queries(g):
        parts = []
        for r in range(R):
            h = g * R + r
            sl = slope_ref[:, h:h + 1] * log2e
            feat = jnp.where(lane_f == n_sel, sl * LANE, jnp.where(lane_f == n_sel + 1, sl, 0.0))
            parts.append(jnp.concatenate(
                [q_ref[0, :, h * Dh:(h + 1) * Dh] * (Dh ** -0.5 * log2e), jnp.broadcast_to(feat, (TQ, Dh))],
                axis=1))
        return jnp.concatenate(parts, axis=0)

    sel_lanes = ((lax.broadcasted_iota(jnp.int32, (1, 2 * Dh), 1) >= Dh)
                 & (lax.broadcasted_iota(jnp.int32, (1, 2 * Dh), 1) < Dh + n_sel))
    qab, o_cmp = [], []
    for g in range(G):
        jc = lax.broadcasted_iota(jnp.int32, (n_cmp_pad, 1), 0)
        ok_c = (jc * CMP_STRIDE + (CMP_BLOCK - 1) <= t_row) & (jc < n_cmp_pad - 1)
        (q_hi, q_lo), (k_hi, k_lo) = _split_bf16(queries(g)), _split_bf16(kc_ref[0, g])
        s_c = (lax.dot_general(k_hi, q_hi, nt, preferred_element_type=F32)
               + lax.dot_general(k_hi, q_lo, nt, preferred_element_type=F32)
               + lax.dot_general(k_lo, q_hi, nt, preferred_element_type=F32)
               + heads(jnp.where(ok_c, 0.0, NEG_INF)))
        e_c = jnp.exp2(s_c - jnp.max(s_c, axis=0, keepdims=True))
        any_c = heads(t_row >= CMP_BLOCK - 1)
        p_c = e_c * jnp.where(any_c, 1.0 / jnp.sum(e_c, axis=0, keepdims=True), 0.0)
        o_cmp.append(jnp.dot(vct_ref[0, g], p_c.astype(BF16), preferred_element_type=F32))

        p_sum = p_c[:, 0:TQ]
        for r in range(1, R):
            p_sum = p_sum + p_c[:, r * TQ:(r + 1) * TQ]
        p_hi, p_lo = _split_bf16(p_sum)
        imp = (jnp.dot(ovlt_ref[...], p_hi, preferred_element_type=F32)
               + jnp.dot(ovlt_ref[...], p_lo, preferred_element_type=F32))
        kb = lax.broadcasted_iota(jnp.int32, (n_sel, 1), 0)
        kbf = kb.astype(F32)
        blk_t = t_row >> int(math.log2(SEL_BLOCK))
        forced = (kb == 0) | (kb == blk_t) | (kb == blk_t - 1)
        cur = jnp.where(forced, FORCE_SCORE, jnp.where(kb <= blk_t, imp, -FORCE_SCORE))
        sel_bias = jnp.full((n_sel, TQ), NEG_INF, F32)
        for _ in range(min(N_SELECT, n_sel)):
            mx = jnp.max(cur, axis=0, keepdims=True)
            first = jnp.min(jnp.where(cur == mx, kbf, float(n_sel)), axis=0, keepdims=True)
            hit = kbf == first
            sel_bias = jnp.where(hit, 0.0, sel_bias)
            cur = jnp.where(hit, -3e38, cur)
        sel_feat = jnp.concatenate([jnp.zeros((Dh, TQ), F32), sel_bias,
                                    jnp.zeros((Dh - n_sel, TQ), F32)], axis=0).T.astype(BF16)
        qab.append(jnp.where(sel_lanes, jnp.concatenate([sel_feat] * R, axis=0), q_hi))

    VR = vst_ref.shape[2] // G

    SEL, WIN = (ksa_ref, vst_ref, 0), (kwa_ref, vwt_ref, 1)

    def tile(j, branches):
        chains = [(b, h) for b in range(len(branches)) for h in range(NSA_HEADS)]
        ks = [[k_ref[0, g, pl.ds(pl.multiple_of(j * TK, TK), TK), :] for g in range(G)]
              for (k_ref, _, _), _, _ in branches]
        vts = [[vt_ref[0, j, g * VR:(g + 1) * VR, :] for g in range(G)]
               for (_, vt_ref, _), _, _ in branches]
        s = {(b, h): lax.dot_general(ks[b][h // R], qab[h // R][(h % R) * TQ:(h % R + 1) * TQ], nt,
                                     preferred_element_type=F32) for b, h in chains}
        s = {(b, h): x if branches[b][2] is None else x + branches[b][2] for (b, h), x in s.items()}
        m_new = {(b, h): jnp.maximum(branches[b][1][h], jnp.max(s[b, h], axis=0, keepdims=True))
                 for b, h in chains}
        alpha = {(b, h): jnp.exp2(branches[b][1][h] - m_new[b, h]) for b, h in chains}
        p = {c: jnp.exp2(s[c] - m_new[c]).astype(BF16) for c in chains}
        pv = {(b, h): jnp.dot(vts[b][h // R], p[b, h], preferred_element_type=F32) for b, h in chains}
        for b, h in chains:
            slot = branches[b][0][2]
            acc_ref[slot, h] = alpha[b, h] * acc_ref[slot, h] + pv[b, h]
        return tuple(tuple(m_new[b, h] for h in range(NSA_HEADS)) for b in range(len(branches)))

    def window_bias(j):
        d = key_dist(j)
        return jnp.where((d >= 0) & (d < WINDOW), 0.0, NEG_INF)

    init = (jnp.full((1, TQ), NEG_INF, F32),) * NSA_HEADS
    acc_ref[...] = jnp.zeros_like(acc_ref)
    causal = jnp.where(key_dist(i) >= 0, 0.0, NEG_INF)
    win_lo = jnp.maximum(i - WINDOW // TK, 0)
    m_sel = lax.fori_loop(0, win_lo, lambda j, m: tile(j, [(SEL, m, None)])[0], init)
    m_sel, m_win = lax.fori_loop(
        win_lo, i, lambda j, ms: tile(j, [(SEL, ms[0], None), (WIN, ms[1], window_bias(j))]), (m_sel, init))
    tile(i, [(SEL, m_sel, causal), (WIN, m_win, causal)])

    for h in range(NSA_HEADS):
        g, r = divmod(h, R)
        acc_s, acc_w = acc_ref[0, h], acc_ref[1, h]
        ot_ref[h * Dh:(h + 1) * Dh, :] = (
            sgate_t[3 * h:3 * h + 1, :] * o_cmp[g][:, r * TQ:(r + 1) * TQ]
            + sgate_t[3 * h + 1:3 * h + 2, :] * (acc_s[0:Dh] * (1.0 / acc_s[Dh:Dh + 1]))
            + sgate_t[3 * h + 2:3 * h + 3, :] * (acc_w[0:Dh] * (1.0 / acc_w[Dh:Dh + 1])))
    o_ref[0] = ot_ref[...].T.astype(o_ref.dtype)


def _nsa_attn(zq, kc, vct, ksa, kwa, vst, vwt, zgate, gate_b):
    B, S, _ = zq.shape
    n_sel = S // SEL_BLOCK
    n_cmp = (S - CMP_BLOCK) // CMP_STRIDE + 1
    n_cmp_pad = kc.shape[2]
    n_kt = S // TK
    G, Dh = NSA_KV_HEADS, NSA_HEAD_DIM
    cmp_start = np.arange(n_cmp) * CMP_STRIDE
    sel_start = np.arange(n_sel) * SEL_BLOCK
    overlap = np.clip(np.minimum(cmp_start[:, None] + CMP_BLOCK, sel_start[None, :] + SEL_BLOCK)
                      - np.maximum(cmp_start[:, None], sel_start[None, :]), 0, None) / CMP_BLOCK
    ovlt = np.zeros((n_sel, n_cmp_pad), np.float32)
    ovlt[:, :n_cmp] = overlap.T
    slopes = (2.0 ** (-8.0 * np.arange(1, NSA_HEADS + 1) / NSA_HEADS)).astype(np.float32).reshape(1, NSA_HEADS)
    gb = jnp.pad(gate_b, (0, LANE - N_GATE)).reshape(1, LANE)
    fixed = lambda shape: pl.BlockSpec(shape, lambda b, i: (0,) * len(shape))
    per_b = lambda shape: pl.BlockSpec((1,) + shape, lambda b, i: (b,) + (0,) * len(shape))
    return pl.pallas_call(
        _nsa_attn_kernel,
        grid=(B, S // TQ),
        in_specs=[pl.BlockSpec((1, TQ, NSA_WIDTH), lambda b, i: (b, i, 0)),
                  per_b((G, n_cmp_pad, 2 * Dh)), per_b((G, Dh, n_cmp_pad)),
                  per_b((G, S, 2 * Dh)), per_b((G, S, 2 * Dh)),
                  per_b((n_kt, G * V_ROWS, TK)), per_b((n_kt, G * V_ROWS, TK)),
                  pl.BlockSpec((1, TQ, LANE), lambda b, i: (b, i, 0)),
                  fixed((1, LANE)), fixed((n_sel, n_cmp_pad)), fixed((1, NSA_HEADS))],
        out_specs=pl.BlockSpec((1, TQ, NSA_WIDTH), lambda b, i: (b, i, 0)),
        out_shape=jax.ShapeDtypeStruct((B, S, NSA_WIDTH), BF16),
        scratch_shapes=[pltpu.VMEM((2, NSA_HEADS, V_ROWS, TQ), F32),
                        pltpu.VMEM((NSA_WIDTH, TQ), F32)],
        compiler_params=pltpu.CompilerParams(dimension_semantics=("parallel", "arbitrary"),
                                             vmem_limit_bytes=VMEM_LIMIT),
        name="nsa_attn",
    )(zq, kc, vct, ksa, kwa, vst, vwt, zgate, gb, jnp.asarray(ovlt, dtype=BF16), jnp.asarray(slopes))


def _mix_kernel(x_ref, yr_ref, yn_ref, zmg_ref, ur_ref, un_ref, wo_ref, gf_ref, wr_ref, br_ref,
                x1_ref, h2_ref, rt_ref, cnt_ref):
    tm = x_ref.shape[0]
    nt = (((1,), (1,)), ((), ()))
    chunks = [slice(c * TM_PROJ, (c + 1) * TM_PROJ) for c in range(tm // TM_PROJ)]
    up_r = [_bdot(yr_ref[c, :], ur_ref[...]) for c in chunks]
    up_n = [_bdot(yn_ref[c, :], un_ref[...]) for c in chunks]
    mixed = [jax.nn.sigmoid(zmg_ref[c, 0:D_MODEL].astype(F32)) * a
             + jax.nn.sigmoid(zmg_ref[c, D_MODEL:2 * D_MODEL].astype(F32)) * b
             for c, a, b in zip(chunks, up_r, up_n)]
    x1 = [x_ref[c, :] + _bdot(m, wo_ref[...]) for c, m in zip(chunks, mixed)]
    h2 = [_rms(v, gf_ref[...]) for v in x1]
    for c, v, h in zip(chunks, x1, h2):
        x1_ref[c, :] = v
        _store_row_tiles(h2_ref.at[c], h)
    n_row = wr_ref.shape[0]
    logits = jnp.concatenate([_dot3_nt(wr_ref[...], h) for h in h2], axis=1) + br_ref[...]
    row = lax.broadcasted_iota(jnp.int32, (n_row, 1), 0).astype(F32)
    gl = jnp.where(row < N_GROUPS, logits, NEG_INF)
    gmax = jnp.max(gl, axis=0, keepdims=True)
    g_sel = jnp.min(jnp.where(gl == gmax, row, float(n_row)), axis=0, keepdims=True)
    p_group = 1.0 / jnp.sum(jnp.exp(gl - gmax), axis=0, keepdims=True)
    e_row = row - N_GROUPS
    in_grp = ((e_row >= g_sel * EXPERTS_PER_GROUP) & (e_row < (g_sel + 1.0) * EXPERTS_PER_GROUP)
              & (e_row < N_EXPERTS))
    el = jnp.where(in_grp, logits, NEG_INF)
    m1 = jnp.max(el, axis=0, keepdims=True)
    i1 = jnp.min(jnp.where(el == m1, e_row, float(n_row)), axis=0, keepdims=True)
    el2 = jnp.where(e_row == i1, 2.0 * NEG_INF, el)
    m2 = jnp.max(el2, axis=0, keepdims=True)
    i2 = jnp.min(jnp.where(el2 == m2, e_row, float(n_row)), axis=0, keepdims=True)
    r2 = jnp.exp(m2 - m1)
    g1 = p_group / (1.0 + r2)
    g2 = p_group * r2 / (1.0 + r2)

    @pl.when(pl.program_id(0) == 0)
    def _():
        cnt_ref[...] = jnp.zeros_like(cnt_ref)

    pick1, pick2 = e_row == i1, e_row == i2
    both = pick1.astype(F32) + pick2.astype(F32)
    earlier = (lax.broadcasted_iota(jnp.int32, (tm, tm), 0)
               < lax.broadcasted_iota(jnp.int32, (tm, tm), 1)).astype(BF16)
    before = jnp.dot(both.astype(BF16), earlier, preferred_element_type=F32) + cnt_ref[:, 0:1]
    rank1 = jnp.sum(jnp.where(pick1, before, 0.0), axis=0, keepdims=True)
    rank2 = jnp.sum(jnp.where(pick2, before, 0.0), axis=0, keepdims=True)
    cnt_ref[...] = cnt_ref[...] + jnp.sum(both, axis=1, keepdims=True)
    rt_ref[...] = jnp.concatenate([i1, i2, g1, g2, rank1, rank2, jnp.zeros((2, tm), F32)], axis=0)


def _mix(x2d, y_rw, y_nsa, zmg, w_up_r, w_up_n, w_out, g_ffn, w_group, b_group, w_router, b_router):
    T = x2d.shape[0]
    tm = MIX_CHUNKS * TM_PROJ
    n_r = N_GROUPS + N_EXPERTS
    n_row = -(-n_r // 8) * 8
    wr = jnp.pad(jnp.concatenate([w_group, w_router], axis=1).T, ((0, n_row - n_r), (0, 0)))
    br = jnp.pad(jnp.concatenate([b_group, b_router]), (0, n_row - n_r)).reshape(n_row, 1)
    row = lambda i: (i, 0)
    fixed = lambda i: (0, 0)
    return pl.pallas_call(
        _mix_kernel,
        grid=(T // tm,),
        in_specs=[pl.BlockSpec((tm, D_MODEL), row), pl.BlockSpec((tm, RW_WIDTH), row),
                  pl.BlockSpec((tm, NSA_WIDTH), row), pl.BlockSpec((tm, 2 * D_MODEL), row),
                  pl.BlockSpec((RW_WIDTH, D_MODEL), fixed), pl.BlockSpec((NSA_WIDTH, D_MODEL), fixed),
                  pl.BlockSpec((D_MODEL, D_MODEL), fixed), pl.BlockSpec((1, D_MODEL), fixed),
                  pl.BlockSpec((n_row, D_MODEL), fixed), pl.BlockSpec((n_row, 1), fixed)],
        out_specs=[pl.BlockSpec((tm, D_MODEL), row), pl.BlockSpec((tm,) + ROW_TILE, lambda i: (i, 0, 0)),
                   pl.BlockSpec((8, tm), lambda i: (0, i)), pl.BlockSpec((n_row, LANE), fixed)],
        out_shape=[jax.ShapeDtypeStruct((T, D_MODEL), F32), jax.ShapeDtypeStruct((T,) + ROW_TILE, F32),
                   jax.ShapeDtypeStruct((8, T), F32), jax.ShapeDtypeStruct((n_row, LANE), F32)],
        compiler_params=pltpu.CompilerParams(dimension_semantics=("arbitrary",),
                                             vmem_limit_bytes=VMEM_LIMIT),
        name="mix",
    )(x2d, y_rw, y_nsa, zmg, w_up_r.astype(BF16), w_up_n.astype(BF16), w_out.astype(BF16),
      g_ffn.reshape(1, D_MODEL), wr, br)


def _route_tables(rt, cnt, T):
    n_rows = T * TOP_K + N_EXPERTS * ROW_BLOCK
    n_blk = n_rows // ROW_BLOCK
    counts = cnt[N_GROUPS:N_GROUPS + N_EXPERTS, 0].astype(jnp.int32)
    padded = (counts + ROW_BLOCK - 1) // ROW_BLOCK * ROW_BLOCK
    pends = jnp.cumsum(padded)
    pstarts = pends - padded
    expert = rt[0:TOP_K].astype(jnp.int32)
    rank = rt[2 * TOP_K:3 * TOP_K].astype(jnp.int32)
    seg_start = jnp.sum(jnp.where(expert[..., None] == jnp.arange(N_EXPERTS), pstarts, 0), axis=-1)
    dest = (seg_start + rank).T.reshape(T * TOP_K)
    gates = rt[TOP_K:2 * TOP_K].T
    blk_start = jnp.arange(n_blk) * ROW_BLOCK
    blk_expert = jnp.minimum(jnp.sum(pends[None, :] <= blk_start[:, None], axis=1), N_EXPERTS - 1)
    n_active = (pends[N_EXPERTS - 1:] // ROW_BLOCK).astype(jnp.int32)
    return dest.astype(jnp.int32), gates, blk_expert.astype(jnp.int32), n_active


ROW_TILE = (D_MODEL // LANE, LANE)


def _store_row_tiles(ref, x):
    for c in range(ROW_TILE[0]):
        ref[:, c, :] = x[:, c * LANE:(c + 1) * LANE]


def _load_row_tiles(ref, idx):
    return jnp.concatenate([ref[(*idx, slice(None), c, slice(None))] for c in range(ROW_TILE[0])], axis=1)


EXPERT_BUFS = 3
TOK_BITS = 14


def _row_info_kernel(dest_ref, fill_hbm, info_ref, sem, *, n_tok):
    fill = pltpu.make_async_copy(fill_hbm, info_ref, sem)
    fill.start()
    fill.wait()

    def body(t, carry):
        word = t + (t << TOK_BITS)
        for k in range(TOP_K):
            info_ref[dest_ref[t * TOP_K + k]] = word + ((k * n_tok) << TOK_BITS)
        return carry

    lax.fori_loop(0, n_tok, body, 0, unroll=8)


def _row_info(dest, n_tok):
    n_rows = n_tok * TOP_K + N_EXPERTS * ROW_BLOCK
    assert n_tok <= 1 << TOK_BITS and (n_tok * TOP_K + 2 * ROW_BLOCK) << TOK_BITS < 2 ** 31
    row = jnp.arange(n_rows, dtype=jnp.int32)
    spare = n_tok * TOP_K + ((row // ROW_BLOCK) % 2) * ROW_BLOCK + row % ROW_BLOCK
    return pl.pallas_call(
        functools.partial(_row_info_kernel, n_tok=n_tok),
        in_specs=[pl.BlockSpec(memory_space=pltpu.SMEM), pl.BlockSpec(memory_space=pl.ANY)],
        out_specs=pl.BlockSpec(memory_space=pltpu.SMEM),
        out_shape=jax.ShapeDtypeStruct((n_rows,), jnp.int32),
        scratch_shapes=[pltpu.SemaphoreType.DMA(())],
        name="row_info",
    )(dest, spare << TOK_BITS)


def _expert_kernel(be_ref, nact_ref, info_ref, h2_hbm, wgu_ref, wd_ref, eo_hbm,
                   xbuf, obuf, wgu_b, wd_b, gsem, ssem):
    i = pl.program_id(0)
    n_act = nact_ref[0]
    tok_mask = (1 << TOK_BITS) - 1

    def gather(blk, s):
        for r in range(ROW_BLOCK):
            tok = info_ref[blk * ROW_BLOCK + r] & tok_mask
            pltpu.make_async_copy(h2_hbm.at[tok], xbuf.at[s, r], gsem.at[s]).start()

    def scatter(blk, s):
        for r in range(ROW_BLOCK):
            row = info_ref[blk * ROW_BLOCK + r] >> TOK_BITS
            pltpu.make_async_copy(obuf.at[s, pl.ds(r, 1)], eo_hbm.at[pl.ds(row, 1)], ssem.at[s]).start()

    def drain_gather(s):
        pltpu.make_async_copy(h2_hbm.at[pl.ds(0, ROW_BLOCK)], xbuf.at[s], gsem.at[s]).wait()

    def drain_scatter(s):
        pltpu.make_async_copy(obuf.at[s], eo_hbm.at[pl.ds(0, ROW_BLOCK)], ssem.at[s]).wait()

    n_buf = xbuf.shape[0]

    def block(s, first):
        drain_gather(s)
        gather(jnp.minimum(i + 2, n_act - 1), (s + 2) % n_buf)
        if not first:
            scatter(i - 1, (s + 2) % n_buf)
        gu = jnp.dot(_load_row_tiles(xbuf, (s,)).astype(BF16), wgu_b[...], preferred_element_type=F32)
        gate_h, up_h = gu[:, :D_EXPERT], gu[:, D_EXPERT:]
        mid = gate_h * jax.nn.sigmoid(gate_h) * up_h
        obuf[s] = jnp.dot(mid.astype(BF16), wd_b[...], preferred_element_type=F32)

    @pl.when(jnp.logical_and(i < n_act, jnp.logical_or(i == 0, be_ref[i] != be_ref[jnp.maximum(i - 1, 0)])))
    def _():
        wgu_b[...] = wgu_ref[0].astype(BF16)
        wd_b[...] = wd_ref[0].astype(BF16)

    @pl.when(i == 0)
    def _():
        gather(0, 0)
        gather(jnp.minimum(1, n_act - 1), 1)
        block(0, first=True)

    for s in range(n_buf):
        mine = i % n_buf == s

        @pl.when(jnp.logical_and(mine, jnp.logical_and(i >= 1, i < n_act)))
        def _():
            @pl.when(i >= n_buf)
            def _():
                drain_scatter(s)
            block(s, first=False)

        @pl.when(jnp.logical_and(mine, i == n_act - 1))
        def _():
            for other in ((s + 1) % n_buf, (s + 2) % n_buf):
                drain_gather(other)

            @pl.when(i >= 2)
            def _():
                drain_scatter((s + 1) % n_buf)

            @pl.when(i >= 1)
            def _():
                drain_scatter((s + 2) % n_buf)
            scatter(i, s)
            drain_scatter(s)
            obuf[s] = jnp.zeros_like(obuf[s])
            for half in range(2):
                spare = eo_hbm.at[pl.ds(eo_hbm.shape[0] - (2 - half) * ROW_BLOCK, ROW_BLOCK)]
                pltpu.make_async_copy(obuf.at[s], spare, ssem.at[s]).start()
            for half in range(2):
                drain_scatter(s)


def _experts(h2, info, blk_expert, n_active, w_gate_up, w_down):
    n_tok = h2.shape[0]
    n_blk = blk_expert.shape[0]
    grid_spec = pltpu.PrefetchScalarGridSpec(
        num_scalar_prefetch=3,
        grid=(n_blk,),
        in_specs=[pl.BlockSpec(memory_space=pl.ANY),
                  pl.BlockSpec((1, D_MODEL, 2 * D_EXPERT), lambda i, be, na, info: (be[i], 0, 0)),
                  pl.BlockSpec((1, D_EXPERT, D_MODEL), lambda i, be, na, info: (be[i], 0, 0))],
        out_specs=pl.BlockSpec(memory_space=pl.ANY),
        scratch_shapes=[pltpu.VMEM((EXPERT_BUFS, ROW_BLOCK) + ROW_TILE, F32),
                        pltpu.VMEM((EXPERT_BUFS, ROW_BLOCK, D_MODEL), F32),
                        pltpu.VMEM((D_MODEL, 2 * D_EXPERT), BF16), pltpu.VMEM((D_EXPERT, D_MODEL), BF16),
                        pltpu.SemaphoreType.DMA((EXPERT_BUFS,)), pltpu.SemaphoreType.DMA((EXPERT_BUFS,))])
    return pl.pallas_call(
        _expert_kernel,
        grid_spec=grid_spec,
        out_shape=jax.ShapeDtypeStruct((n_tok * TOP_K + 2 * ROW_BLOCK, D_MODEL), F32),
        compiler_params=pltpu.CompilerParams(dimension_semantics=("arbitrary",),
                                             vmem_limit_bytes=VMEM_LIMIT),
        name="experts",
    )(blk_expert, n_active, info, h2, w_gate_up, w_down)


def _ple_kernel(x1_ref, g_ref, p_ref, *rest, last_layer):
    eo_refs, (wpp_ref, gpn_ref, ggi_ref, wpg_ref, gfin_ref, y_ref) = rest[:TOP_K], rest[TOP_K:]
    tm = x1_ref.shape[0]
    chunks = [slice(c * TM_PROJ, (c + 1) * TM_PROJ) for c in range(tm // TM_PROJ)]
    x2 = []
    for c in chunks:
        moe = g_ref[c, 0:1] * eo_refs[0][c, :]
        for k in range(1, TOP_K):
            moe = moe + g_ref[c, k:k + 1] * eo_refs[k][c, :]
        x2.append(x1_ref[c, :] + moe)
    e = [_rms(_bdot(p_ref[c, :], wpp_ref[...]), gpn_ref[...]) for c in chunks]
    gate = [jax.nn.sigmoid(_bdot(_rms(v, ggi_ref[...]), wpg_ref[...])) for v in x2]
    for c, v, g, ee in zip(chunks, x2, gate, e):
        x3 = v + g * ee
        y_ref[c, :] = _rms(x3, gfin_ref[...]) if last_layer else x3


def _ple(x1, gates, expert_out, p2d, w_pp, g_pn, g_gi, w_pg, g_final, last_layer):
    T = x1.shape[0]
    tm = MIX_CHUNKS * TM_PROJ
    nt = T // tm
    row = lambda i: (i, 0)
    fixed = lambda i: (0, 0)
    vec = lambda a: a.reshape(1, D_MODEL)
    slot_rows = [pl.BlockSpec((tm, D_MODEL), functools.partial(lambda i, k: (i + k * nt, 0), k=k))
                 for k in range(TOP_K)]
    return pl.pallas_call(
        functools.partial(_ple_kernel, last_layer=last_layer),
        grid=(nt,),
        in_specs=[pl.BlockSpec((tm, D_MODEL), row), pl.BlockSpec((tm, TOP_K), row),
                  pl.BlockSpec((tm, PLE_DIM), row)] + slot_rows + [
                  pl.BlockSpec((PLE_DIM, D_MODEL), fixed), pl.BlockSpec((1, D_MODEL), fixed),
                  pl.BlockSpec((1, D_MODEL), fixed), pl.BlockSpec((D_MODEL, D_MODEL), fixed),
                  pl.BlockSpec((1, D_MODEL), fixed)],
        out_specs=pl.BlockSpec((tm, D_MODEL), row),
        out_shape=jax.ShapeDtypeStruct((T, D_MODEL), F32),
        compiler_params=pltpu.CompilerParams(dimension_semantics=("parallel",),
                                             vmem_limit_bytes=VMEM_LIMIT),
        name="ple",
    )(x1, gates, p2d, *([expert_out] * TOP_K), w_pp.astype(BF16), vec(g_pn), vec(g_gi),
      w_pg.astype(BF16), vec(g_final))


def kernel(x, p, g_mix, w_in, mu_shift, rw_w0, rw_w2, rw_a0, rw_a2, rw_g2, rw_k_k, rw_k_a, rw_r_k, rw_ln_w, rw_ln_b, cmp_pos_k, cmp_pos_v, cmp_k_w1, cmp_k_w2, cmp_v_w1, cmp_v_w2, nsa_gate_b, w_up_rwkv, w_up_nsa, w_out, g_ffn, w_group, b_group, w_router, b_router, w_exp_gate_up, w_exp_down, w_ple_proj, g_ple_norm, g_ple_gate_in, w_ple_gate, g_final):
    B, S, D = x.shape
    T = B * S
    depth = p.shape[0]
    xc = x.reshape(T, D)
    for i in range(depth):
        zrw, zq, zkv, zmg, zgate = _proj(xc, g_mix[i], w_in[i], mu_shift[i], S)
        y_rw = _rwkv(zrw.reshape(B, S, RW_IN), rw_w0[i], rw_w2[i], rw_a0[i], rw_a2[i], rw_g2[i],
                     rw_k_k[i], rw_k_a[i], rw_r_k[i], rw_ln_w[i], rw_ln_b[i])
        zkv3 = zkv.reshape(B, S, KV_IN)
        nsa_kv = _nsa_prep(zkv3, cmp_pos_k[i], cmp_pos_v[i], cmp_k_w1[i], cmp_k_w2[i],
                           cmp_v_w1[i], cmp_v_w2[i])
        y_nsa = _nsa_attn(zq.reshape(B, S, NSA_WIDTH), *nsa_kv, zgate.reshape(B, S, LANE),
                          nsa_gate_b[i])
        x1, h2, rt, cnt = _mix(xc, y_rw.reshape(T, RW_WIDTH), y_nsa.reshape(T, NSA_WIDTH), zmg,
                               w_up_rwkv[i], w_up_nsa[i], w_out[i], g_ffn[i], w_group[i], b_group[i],
                               w_router[i], b_router[i])
        dest, gates, blk_expert, n_active = _route_tables(rt, cnt, T)
        expert_out = _experts(h2, _row_info(dest, T), blk_expert, n_active, w_exp_gate_up[i],
                              w_exp_down[i])
        xc = _ple(x1, gates, expert_out, p[i].reshape(T, PLE_DIM), w_ple_proj[i], g_ple_norm[i],
                  g_ple_gate_in[i], w_ple_gate[i], g_final, i == depth - 1)
    return xc.reshape(B, S, D)
```

```python
import functools
import math

import jax
import jax.numpy as jnp
import numpy as np
from jax import lax
from jax.experimental import pallas as pl
from jax.experimental.pallas import tpu as pltpu

F32 = jnp.float32
BF16 = jnp.bfloat16

D_MODEL = 1024
RW_HEADS = 8
RW_HEAD_DIM = 64
RW_WIDTH = 512
DECAY_LORA = 64
AAA_LORA = 64
GATE_LORA = 128
GN_EPS = 64e-5
RW_IN = 3 * RW_WIDTH + DECAY_LORA + AAA_LORA + GATE_LORA

NSA_HEADS = 8
NSA_KV_HEADS = 2
NSA_REP = NSA_HEADS // NSA_KV_HEADS
NSA_HEAD_DIM = 64
NSA_WIDTH = 512
NSA_KV_WIDTH = 128
CMP_BLOCK = 32
CMP_STRIDE = 16
SEL_BLOCK = 64
N_SELECT = 8
WINDOW = 512
N_NSA_BRANCH = 3
FORCE_SCORE = 1e6
NEG_INF = -1e30

N_GROUPS = 4
EXPERTS_PER_GROUP = 8
N_EXPERTS = 32
TOP_K = 2
D_EXPERT = 512
ROW_BLOCK = 128
PLE_DIM = 256
NORM_EPS = 1e-6

N_GATE = N_NSA_BRANCH * NSA_HEADS
ATT_IN = NSA_WIDTH + 6 * NSA_KV_WIDTH
KV_OFF = RW_IN + NSA_WIDTH
KV_IN = 6 * NSA_KV_WIDTH
GATE_OFF = RW_IN + ATT_IN
MERGE_OFF = GATE_OFF + N_GATE
LANE = 128

RW_CHUNK = 64
RW_SUB = 16
RW_ROWS = 4
TQ = 256
TK = 256
V_ROWS = NSA_HEAD_DIM + 16
TM_PROJ = 256
MIX_CHUNKS = 4
VMEM_LIMIT = 56 * 1024 * 1024


def _bdot(a, b):
    return jnp.dot(a.astype(BF16), b.astype(BF16), preferred_element_type=F32)


def _bdot_nt(a, b):
    return lax.dot_general(a.astype(BF16), b.astype(BF16), (((1,), (1,)), ((), ())),
                           preferred_element_type=F32)


def _split_bf16(x):
    hi = x.astype(BF16)
    return hi, (x - hi.astype(F32)).astype(BF16)


def _dot3_nt(a, b):
    (a_hi, a_lo), (b_hi, b_lo) = _split_bf16(a), _split_bf16(b)
    return _bdot_nt(a_hi, b_hi) + _bdot_nt(a_hi, b_lo) + _bdot_nt(a_lo, b_hi)


def _rms(x, g):
    return x * lax.rsqrt(jnp.mean(x * x, axis=-1, keepdims=True) + NORM_EPS) * g


def _proj_kernel(x_ref, g_ref, w_ref, mu_ref, zrw_ref, zq_ref, zkv_ref, zmg_ref, zgate_ref,
                 carry_ref, *, tiles_per_seq):
    i = pl.program_id(0)
    tm = x_ref.shape[0]

    @pl.when(i % tiles_per_seq == 0)
    def _():
        carry_ref[...] = jnp.zeros_like(carry_ref)

    h = _rms(x_ref[...], g_ref[...]).astype(BF16)
    z = jnp.dot(h, w_ref[:, 0:RW_IN], preferred_element_type=F32)
    row = lax.broadcasted_iota(jnp.int32, (tm, 1), 0)
    prev = jnp.where(row == 0, carry_ref[7:8, :], pltpu.roll(z, 1, 0))
    carry_ref[...] = z[tm - 8:tm, :]
    zrw_ref[...] = z + (prev - z) * mu_ref[...]
    zq_ref[...] = jnp.dot(h, w_ref[:, RW_IN:KV_OFF], preferred_element_type=F32)
    zkv_ref[...] = jnp.dot(h, w_ref[:, KV_OFF:GATE_OFF], preferred_element_type=F32)
    zmg_ref[...] = jnp.dot(h, w_ref[:, GATE_OFF:GATE_OFF + 2 * D_MODEL],
                           preferred_element_type=F32).astype(zmg_ref.dtype)
    zgate_ref[...] = jnp.dot(h, w_ref[:, GATE_OFF + 2 * D_MODEL:], preferred_element_type=F32)


def _proj(x2d, g_mix, w_in, mu, seq):
    T = x2d.shape[0]
    tm = 2 * TM_PROJ
    wp = jnp.concatenate(
        [w_in[:, :GATE_OFF], w_in[:, MERGE_OFF:],
         jnp.pad(w_in[:, GATE_OFF:MERGE_OFF], ((0, 0), (0, LANE - N_GATE)))], axis=1).astype(BF16)
    npad = wp.shape[1]
    row = lambda i: (i, 0)
    fixed = lambda i: (0, 0)
    return pl.pallas_call(
        functools.partial(_proj_kernel, tiles_per_seq=seq // tm),
        grid=(T // tm,),
        in_specs=[pl.BlockSpec((tm, D_MODEL), row), pl.BlockSpec((1, D_MODEL), fixed),
                  pl.BlockSpec((D_MODEL, npad), fixed), pl.BlockSpec((1, RW_IN), fixed)],
        out_specs=[pl.BlockSpec((tm, RW_IN), row), pl.BlockSpec((tm, NSA_WIDTH), row),
                   pl.BlockSpec((tm, KV_IN), row), pl.BlockSpec((tm, 2 * D_MODEL), row),
                   pl.BlockSpec((tm, LANE), row)],
        out_shape=[jax.ShapeDtypeStruct((T, RW_IN), F32), jax.ShapeDtypeStruct((T, NSA_WIDTH), F32),
                   jax.ShapeDtypeStruct((T, KV_IN), F32), jax.ShapeDtypeStruct((T, 2 * D_MODEL), BF16),
                   jax.ShapeDtypeStruct((T, LANE), F32)],
        scratch_shapes=[pltpu.VMEM((8, RW_IN), F32)],
        compiler_params=pltpu.CompilerParams(dimension_semantics=("arbitrary",),
                                             vmem_limit_bytes=VMEM_LIMIT),
        name="proj",
    )(x2d, g_mix.reshape(1, D_MODEL), wp, mu.reshape(1, RW_IN))


PAIR = 2 * RW_HEAD_DIM


def _pair_blocks(x):
    low = lax.broadcasted_iota(jnp.int32, (1, PAIR), 1) < RW_HEAD_DIM
    return jnp.concatenate([jnp.where(low, x, 0.0), jnp.where(low, 0.0, x)], axis=0)


def _pmm(a, b):
    return _bdot(a, _pair_blocks(b))


def _unit_lower_inverse(a_strict, sub_mask, eye):
    ad = [jnp.where(sub_mask, a, 0.0) for a in a_strict]
    ao = [a - d for a, d in zip(a_strict, ad)]
    td = [eye - d for d in ad]
    pw = ad
    for _ in range(int(math.log2(RW_SUB)) - 1):
        pw = [_pmm(x, x) for x in pw]
        td = [_pmm(t, eye + x) for t, x in zip(td, pw)]
    n = [_pmm(t, o) for t, o in zip(td, ao)]
    t = [eye - x for x in n]
    pw = n
    for _ in range(int(math.log2(RW_CHUNK // RW_SUB)) - 1):
        pw = [_pmm(x, x) for x in pw]
        t = [_pmm(a, eye + x) for a, x in zip(t, pw)]
    return [_pmm(a, d) for a, d in zip(t, td)]


def _rwkv_kernel(z_ref, w0_ref, w2_ref, a0_ref, a2_ref, g2_ref, kk_ref, ka_ref, rk_ref, lnw_ref,
                 lnb_ref, avg_ref, o_ref, h_ref):
    c = pl.program_id(1)
    C = RW_CHUNK
    n_pair = RW_WIDTH // PAIR
    nt = (((1,), (1,)), ((), ()))

    @pl.when(c == 0)
    def _():
        h_ref[...] = jnp.zeros_like(h_ref)

    ti = lax.broadcasted_iota(jnp.int32, (C, 1), 0)
    si = lax.broadcasted_iota(jnp.int32, (1, PAIR), 1) & (RW_HEAD_DIM - 1)
    incl, strict = ti >= si, ti > si
    eye = (ti == si).astype(F32)
    sub_shift = int(math.log2(RW_SUB))
    sub_mask = (ti >> sub_shift) == (si >> sub_shift)
    row2 = lax.broadcasted_iota(jnp.int32, (PAIR, 1), 0)
    col2 = lax.broadcasted_iota(jnp.int32, (1, PAIR), 1)
    same_head = (row2 < RW_HEAD_DIM) == (col2 < RW_HEAD_DIM)
    eye2 = row2 == col2
    tri = (lax.broadcasted_iota(jnp.int32, (C, C), 0)
           >= lax.broadcasted_iota(jnp.int32, (C, C), 1)).astype(BF16)

    def head_mean(x):
        xs = jnp.concatenate([x[:, p * PAIR:(p + 1) * PAIR] for p in range(n_pair)], axis=0)
        ms = _bdot(xs, avg_ref[...])
        return jnp.concatenate([ms[p * C:(p + 1) * C] for p in range(n_pair)], axis=1)

    n_rows = z_ref.shape[0]
    rows = []
    for n in range(n_rows):
        z = z_ref[n]
        zr, zk, zv = z[:, 0:512], z[:, 512:1024], z[:, 1024:1536]
        zw, za, zg = z[:, 1536:1600], z[:, 1600:1664], z[:, 1664:1792]
        w_raw = w0_ref[...] + _bdot(jnp.tanh(zw), w2_ref[...])
        logw = -jax.nn.sigmoid(w_raw) * math.exp(-0.5)
        a = jax.nn.sigmoid(a0_ref[...] + _bdot(za, a2_ref[...]))
        gate = _bdot(jax.nn.sigmoid(zg), g2_ref[...])
        kk = zk * kk_ref[...]
        kk = kk * lax.rsqrt(jnp.maximum(head_mean(kk * kk) * RW_HEAD_DIM, 1e-24))
        k = zk * (1.0 + (a - 1.0) * ka_ref[...])
        b = kk * a

        w_hi = logw.astype(BF16)
        w_lo = (logw - w_hi.astype(F32)).astype(BF16)
        cum = (jnp.dot(tri, w_hi, preferred_element_type=F32)
               + jnp.dot(tri, w_lo, preferred_element_type=F32))
        cum_last = cum[C - 1:C, :]
        g_inv = jnp.exp(-cum)
        g_end = jnp.exp(cum_last - cum)
        rows.append(dict(rt=zr * jnp.exp(cum), kt=k * g_inv, bt=b * g_inv, qt=kk * jnp.exp(cum - logw),
                         kh=k * g_end, bh=b * g_end, v=zv, g_last=jnp.exp(cum_last), gate=gate,
                         bonus=head_mean(zr * k * rk_ref[...]) * RW_HEAD_DIM * zv))

    chains = [(n, slice(p * PAIR, (p + 1) * PAIR)) for n in range(n_rows) for p in range(n_pair)]
    part = lambda name: [rows[n][name][:, sl] for n, sl in chains]
    qt, rt, kt, bt, kh, bh, v = (part(x) for x in ("qt", "rt", "kt", "bt", "kh", "bh", "v"))
    lhs = [jnp.concatenate([q, r], axis=0).astype(BF16) for q, r in zip(qt, rt)]
    ab = [lax.dot_general(l, _pair_blocks(x).astype(BF16), nt, preferred_element_type=F32)
          for l, x in zip(lhs, bt)]
    ak = [lax.dot_general(l, _pair_blocks(x).astype(BF16), nt, preferred_element_type=F32)
          for l, x in zip(lhs, kt)]
    a_kb = [jnp.where(strict, x[0:C], 0.0) for x in ab]
    a_rb = [jnp.where(incl, x[C:2 * C], 0.0) for x in ab]
    a_kk = [jnp.where(strict, x[0:C], 0.0) for x in ak]
    a_rk = [jnp.where(incl, x[C:2 * C], 0.0) for x in ak]
    t_inv = _unit_lower_inverse(a_kb, sub_mask, eye)

    h = [h_ref[n, sl.start // PAIR] for n, sl in chains]
    vb = [_pair_blocks(x) for x in v]
    rhs = [_bdot(jnp.concatenate([q, akk], axis=1), jnp.concatenate([hh, vv], axis=0))
           for q, akk, hh, vv in zip(qt, a_kk, h, vb)]
    u = [_pmm(t, x) for t, x in zip(t_inv, rhs)]
    outs = [_bdot(jnp.concatenate([r, ark, -arb], axis=1), jnp.concatenate([hh, vv, _pair_blocks(uu)], axis=0))
            for r, ark, arb, hh, vv, uu in zip(rt, a_rk, a_rb, h, vb, u)]
    upd = [_bdot(jnp.concatenate([x, -y], axis=0).T, jnp.concatenate([vv, uu], axis=0))
           for x, y, vv, uu in zip(kh, bh, v, u)]
    for (n, sl), hh, dd in zip(chains, h, upd):
        decay_col = jnp.sum(jnp.where(eye2, rows[n]["g_last"][:, sl], 0.0), axis=1, keepdims=True)
        h_ref[n, sl.start // PAIR] = decay_col * hh + jnp.where(same_head, dd, 0.0)

    for n in range(n_rows):
        o = jnp.concatenate(outs[n * n_pair:(n + 1) * n_pair], axis=1)
        d = o - head_mean(o)
        on = d * lax.rsqrt(head_mean(d * d) + GN_EPS)
        o_ref[n] = ((on * lnw_ref[...] + lnb_ref[...] + rows[n]["bonus"]) * rows[n]["gate"]).astype(o_ref.dtype)


def _rwkv(zrw, w0, w2, a0, a2, g2, k_k, k_a, r_k, ln_w, ln_b):
    B, S, _ = zrw.shape
    C = RW_CHUNK
    nb = RW_ROWS
    hid = np.arange(PAIR) // RW_HEAD_DIM
    avg = jnp.asarray((hid[:, None] == hid[None, :]).astype(np.float32) / RW_HEAD_DIM)
    vec = lambda a: a.reshape(1, RW_WIDTH)
    fixed = lambda shape: pl.BlockSpec(shape, lambda b, c: (0,) * len(shape))
    return pl.pallas_call(
        _rwkv_kernel,
        grid=(B // nb, S // C),
        in_specs=[pl.BlockSpec((nb, C, RW_IN), lambda b, c: (b, c, 0)),
                  fixed((1, RW_WIDTH)), fixed((DECAY_LORA, RW_WIDTH)),
                  fixed((1, RW_WIDTH)), fixed((AAA_LORA, RW_WIDTH)),
                  fixed((GATE_LORA, RW_WIDTH)), fixed((1, RW_WIDTH)), fixed((1, RW_WIDTH)),
                  fixed((1, RW_WIDTH)), fixed((1, RW_WIDTH)), fixed((1, RW_WIDTH)),
                  fixed((PAIR, PAIR))],
        out_specs=pl.BlockSpec((nb, C, RW_WIDTH), lambda b, c: (b, c, 0)),
        out_shape=jax.ShapeDtypeStruct((B, S, RW_WIDTH), BF16),
        scratch_shapes=[pltpu.VMEM((nb, RW_WIDTH // PAIR, PAIR, PAIR), F32)],
        compiler_params=pltpu.CompilerParams(dimension_semantics=("parallel", "arbitrary"),
                                             vmem_limit_bytes=VMEM_LIMIT),
        name="rwkv",
    )(zrw, vec(w0), w2, vec(a0), a2, g2, vec(k_k), vec(k_a), vec(r_k), vec(ln_w), vec(ln_b), avg)


def _gelu_tanh(x):
    return 0.5 * x * (1.0 + jnp.tanh(math.sqrt(2.0 / math.pi) * (x + 0.044715 * (x * x * x))))


def _key_features(pos_hi, pos_lo, block, n, n_sel):
    lane = lax.broadcasted_iota(jnp.int32, (n, NSA_HEAD_DIM), 1)
    feat = jnp.where(lane == n_sel, pos_hi, jnp.where(lane == n_sel + 1, pos_lo, 0.0))
    return feat if block is None else jnp.where(lane == block, 1.0, feat)


def _nsa_prep_kernel(zkc_ref, zvc_ref, zks_ref, zvs_ref, zkw_ref, zvw_ref, pk_ref, pv_ref, kw1_ref,
                     kw2_ref, vw1_ref, vw2_ref, kc_ref, vct_ref, ksa_ref, kwa_ref, vst_ref, vwt_ref):
    S = zkc_ref.shape[1]
    n_grp = S // CMP_STRIDE
    Dh = NSA_HEAD_DIM
    half = CMP_BLOCK // 2
    n_sel = S // SEL_BLOCK
    jrow = lax.broadcasted_iota(jnp.int32, (n_grp, 1), 0)
    cmp_feat = _key_features((jrow >> 3).astype(F32),
                             ((jrow & 7) * CMP_STRIDE).astype(F32) + 0.5 * (CMP_BLOCK - 1),
                             None, n_grp, n_sel)
    for is_v, (z_ref, pos_ref, w1_ref, w2_ref) in enumerate(((zkc_ref, pk_ref, kw1_ref, kw2_ref),
                                                            (zvc_ref, pv_ref, vw1_ref, vw2_ref))):
        lo = jnp.zeros((n_grp, NSA_KV_WIDTH), F32)
        hi = jnp.zeros((n_grp, NSA_KV_WIDTH), F32)
        for l in range(half):
            xs = z_ref[0, pl.ds(l, n_grp, stride=CMP_STRIDE), :]
            lo = lo + _bdot(xs + pos_ref[l:l + 1, :], w1_ref[l])
            hi = hi + _bdot(xs + pos_ref[half + l:half + l + 1, :], w1_ref[half + l])
        pre = lo + pltpu.roll(hi, n_grp - 1, 0)
        out = jnp.where(jrow < n_grp - 1, _bdot(_gelu_tanh(pre), w2_ref[...]), 0.0)
        out_t = out.T
        for g in range(NSA_KV_HEADS):
            if is_v:
                vct_ref[0, g] = out_t[g * Dh:(g + 1) * Dh, :].astype(BF16)
            else:
                kc_ref[0, g] = jnp.concatenate([out[:, g * Dh:(g + 1) * Dh], cmp_feat], axis=1)

    prow = lax.broadcasted_iota(jnp.int32, (S, 1), 0)
    p_hi, p_lo = (prow >> 7).astype(F32), (prow & (LANE - 1)).astype(F32)
    for z_ref, out_ref, block in ((zks_ref, ksa_ref, prow >> int(math.log2(SEL_BLOCK))),
                                  (zkw_ref, kwa_ref, None)):
        kfull = z_ref[0]
        key_feat = _key_features(p_hi, p_lo, block, S, n_sel)
        for g in range(NSA_KV_HEADS):
            out_ref[0, g] = jnp.concatenate([kfull[:, g * Dh:(g + 1) * Dh], key_feat], axis=1).astype(BF16)
    ones_row = (lax.broadcasted_iota(jnp.int32, (V_ROWS - Dh, TK), 0) == 0).astype(F32)
    for z_ref, out_ref in ((zvs_ref, vst_ref), (zvw_ref, vwt_ref)):
        for j in range(S // TK):
            vt = z_ref[0, j * TK:(j + 1) * TK, :].T
            out_ref[0, j] = jnp.concatenate(
                [piece for g in range(NSA_KV_HEADS) for piece in (vt[g * Dh:(g + 1) * Dh], ones_row)],
                axis=0).astype(BF16)


def _nsa_prep(zkv, pos_k, pos_v, kw1, kw2, vw1, vw2):
    B, S, _ = zkv.shape
    n_grp = S // CMP_STRIDE
    n_kt = S // TK
    Dh = NSA_HEAD_DIM
    G = NSA_KV_HEADS
    fixed = lambda shape: pl.BlockSpec(shape, lambda b: (0,) * len(shape))
    col = lambda c: pl.BlockSpec((1, S, NSA_KV_WIDTH), lambda b: (b, 0, c))
    whole = lambda shape: pl.BlockSpec((1,) + shape, lambda b: (b,) + (0,) * len(shape))
    shapes = [((G, n_grp, 2 * Dh), F32), ((G, Dh, n_grp), BF16), ((G, S, 2 * Dh), BF16),
              ((G, S, 2 * Dh), BF16), ((n_kt, G * V_ROWS, TK), BF16), ((n_kt, G * V_ROWS, TK), BF16)]
    eye = jnp.eye(G, dtype=F32)
    per_group = lambda w: jnp.einsum("ab,...ij->...aibj", eye, w).reshape(
        w.shape[:-2] + (G * w.shape[-2], G * w.shape[-1])).astype(BF16)
    kw1, vw1 = (per_group(w.reshape(CMP_BLOCK, Dh, Dh)) for w in (kw1, vw1))
    kw2, vw2 = per_group(kw2), per_group(vw2)
    pos_k, pos_v = (jnp.concatenate([p] * G, axis=1) for p in (pos_k, pos_v))
    return pl.pallas_call(
        _nsa_prep_kernel,
        grid=(B,),
        in_specs=[col(c) for c in range(6)] + [
            fixed((CMP_BLOCK, G * Dh)), fixed((CMP_BLOCK, G * Dh)),
            fixed((CMP_BLOCK, G * Dh, G * Dh)), fixed((G * Dh, G * Dh)),
            fixed((CMP_BLOCK, G * Dh, G * Dh)), fixed((G * Dh, G * Dh))],
        out_specs=[whole(s) for s, _ in shapes],
        out_shape=[jax.ShapeDtypeStruct((B,) + s, d) for s, d in shapes],
        compiler_params=pltpu.CompilerParams(dimension_semantics=("parallel",),
                                             vmem_limit_bytes=VMEM_LIMIT),
        name="nsa_prep",
    )(zkv, zkv, zkv, zkv, zkv, zkv, pos_k, pos_v, kw1, kw2, vw1, vw2)


def _nsa_attn_kernel(q_ref, kc_ref, vct_ref, ksa_ref, kwa_ref, vst_ref, vwt_ref, gl_ref, gb_ref,
                     ovlt_ref, slope_ref, o_ref, acc_ref, ot_ref):
    i = pl.program_id(1)
    Dh = NSA_HEAD_DIM
    R = NSA_REP
    N = R * TQ
    n_cmp_pad = kc_ref.shape[2]
    n_sel = ovlt_ref.shape[0]
    G = NSA_KV_HEADS
    nt = (((1,), (1,)), ((), ()))
    log2e = math.log2(math.e)
    t0 = i * TQ
    t_row = t0 + lax.broadcasted_iota(jnp.int32, (1, TQ), 1)
    c_col = lax.broadcasted_iota(jnp.int32, (TK, 1), 0)
    sgate_t = jax.nn.sigmoid(gl_ref[0] + gb_ref[...]).T
    lane_f = lax.broadcasted_iota(jnp.int32, (1, Dh), 1)
    heads = lambda x: jnp.concatenate([x] * R, axis=1)

    def key_dist(j):
        return t_row - (j * TK + c_col)

    def queries(g):
        parts = []
        for r in range(R):
            h = g * R + r
            sl = slope_ref[:, h:h + 1] * log2e
            feat = jnp.where(lane_f == n_sel, sl * LANE, jnp.where(lane_f == n_sel + 1, sl, 0.0))
            parts.append(jnp.concatenate(
                [q_ref[0, :, h * Dh:(h + 1) * Dh] * (Dh ** -0.5 * log2e), jnp.broadcast_to(feat, (TQ, Dh))],
                axis=1))
        return jnp.concatenate(parts, axis=0)

    sel_lanes = ((lax.broadcasted_iota(jnp.int32, (1, 2 * Dh), 1) >= Dh)
                 & (lax.broadcasted_iota(jnp.int32, (1, 2 * Dh), 1) < Dh + n_sel))
    qab, o_cmp = [], []
    for g in range(G):
        jc = lax.broadcasted_iota(jnp.int32, (n_cmp_pad, 1), 0)
        ok_c = (jc * CMP_STRIDE + (CMP_BLOCK - 1) <= t_row) & (jc < n_cmp_pad - 1)
        (q_hi, q_lo), (k_hi, k_lo) = _split_bf16(queries(g)), _split_bf16(kc_ref[0, g])
        s_c = (lax.dot_general(k_hi, q_hi, nt, preferred_element_type=F32)
               + lax.dot_general(k_hi, q_lo, nt, preferred_element_type=F32)
               + lax.dot_general(k_lo, q_hi, nt, preferred_element_type=F32)
               + heads(jnp.where(ok_c, 0.0, NEG_INF)))
        e_c = jnp.exp2(s_c - jnp.max(s_c, axis=0, keepdims=True))
        any_c = heads(t_row >= CMP_BLOCK - 1)
        p_c = e_c * jnp.where(any_c, 1.0 / jnp.sum(e_c, axis=0, keepdims=True), 0.0)
        o_cmp.append(jnp.dot(vct_ref[0, g], p_c.astype(BF16), preferred_element_type=F32))

        p_sum = p_c[:, 0:TQ]
        for r in range(1, R):
            p_sum = p_sum + p_c[:, r * TQ:(r + 1) * TQ]
        p_hi, p_lo = _split_bf16(p_sum)
        imp = (jnp.dot(ovlt_ref[...], p_hi, preferred_element_type=F32)
               + jnp.dot(ovlt_ref[...], p_lo, preferred_element_type=F32))
        kb = lax.broadcasted_iota(jnp.int32, (n_sel, 1), 0)
        kbf = kb.astype(F32)
        blk_t = t_row >> int(math.log2(SEL_BLOCK))
        forced = (kb == 0) | (kb == blk_t) | (kb == blk_t - 1)
        cur = jnp.where(forced, FORCE_SCORE, jnp.where(kb <= blk_t, imp, -FORCE_SCORE))
        sel_bias = jnp.full((n_sel, TQ), NEG_INF, F32)
        for _ in range(min(N_SELECT, n_sel)):
            mx = jnp.max(cur, axis=0, keepdims=True)
            first = jnp.min(jnp.where(cur == mx, kbf, float(n_sel)), axis=0, keepdims=True)
            hit = kbf == first
            sel_bias = jnp.where(hit, 0.0, sel_bias)
            cur = jnp.where(hit, -3e38, cur)
        sel_feat = jnp.concatenate([jnp.zeros((Dh, TQ), F32), sel_bias,
                                    jnp.zeros((Dh - n_sel, TQ), F32)], axis=0).T.astype(BF16)
        qab.append(jnp.where(sel_lanes, jnp.concatenate([sel_feat] * R, axis=0), q_hi))

    VR = vst_ref.shape[2] // G

    SEL, WIN = (ksa_ref, vst_ref, 0), (kwa_ref, vwt_ref, 1)

    def tile(j, branches, keys=(0, TK), cols=(0, TQ)):
        (k0, k1), (c0, c1) = keys, cols
        chains = [(b, h) for b in range(len(branches)) for h in range(NSA_HEADS)]
        ks = [[k_ref[0, g, pl.ds(pl.multiple_of(j * TK, TK) + k0, k1 - k0), :] for g in range(G)]
              for (k_ref, _, _), _, _ in branches]
        vts = [[vt_ref[0, j, g * VR:(g + 1) * VR, k0:k1] for g in range(G)]
               for (_, vt_ref, _), _, _ in branches]
        m_old = {(b, h): branches[b][1][h][:, c0:c1] for b, h in chains}
        s = {(b, h): lax.dot_general(ks[b][h // R], qab[h // R][(h % R) * TQ + c0:(h % R) * TQ + c1], nt,
                                     preferred_element_type=F32) for b, h in chains}
        s = {(b, h): x if branches[b][2] is None else x + branches[b][2] for (b, h), x in s.items()}
        m_new = {c: jnp.maximum(m_old[c], jnp.max(s[c], axis=0, keepdims=True)) for c in chains}
        alpha = {c: jnp.exp2(m_old[c] - m_new[c]) for c in chains}
        p = {c: jnp.exp2(s[c] - m_new[c]).astype(BF16) for c in chains}
        pv = {(b, h): jnp.dot(vts[b][h // R], p[b, h], preferred_element_type=F32) for b, h in chains}
        for b, h in chains:
            slot = branches[b][0][2]
            acc_ref[slot, h, :, c0:c1] = alpha[b, h] * acc_ref[slot, h, :, c0:c1] + pv[b, h]

        def merged(b, h):
            m = branches[b][1][h]
            return jnp.concatenate([m[:, 0:c0], m_new[b, h], m[:, c1:TQ]], axis=1) if (c0, c1) != (0, TQ) \
                else m_new[b, h]
        return tuple(tuple(merged(b, h) for h in range(NSA_HEADS)) for b in range(len(branches)))

    def window_bias(j):
        d = key_dist(j)
        return jnp.where((d >= 0) & (d < WINDOW), 0.0, NEG_INF)

    init = (jnp.full((1, TQ), NEG_INF, F32),) * NSA_HEADS
    acc_ref[...] = jnp.zeros_like(acc_ref)
    causal = jnp.where(key_dist(i) >= 0, 0.0, NEG_INF)
    win_lo = jnp.maximum(i - WINDOW // TK, 0)
    m_sel = lax.fori_loop(0, win_lo, lambda j, m: tile(j, [(SEL, m, None)])[0], init)
    m_sel, m_win = lax.fori_loop(
        win_lo, i, lambda j, ms: tile(j, [(SEL, ms[0], None), (WIN, ms[1], window_bias(j))]), (m_sel, init))
    half = TK // 2
    m_sel, m_win = tile(i, [(SEL, m_sel, causal[0:half]), (WIN, m_win, causal[0:half])], keys=(0, half))
    tile(i, [(SEL, m_sel, causal[half:TK, half:TQ]), (WIN, m_win, causal[half:TK, half:TQ])],
         keys=(half, TK), cols=(half, TQ))

    for h in range(NSA_HEADS):
        g, r = divmod(h, R)
        acc_s, acc_w = acc_ref[0, h], acc_ref[1, h]
        ot_ref[h * Dh:(h + 1) * Dh, :] = (
            sgate_t[3 * h:3 * h + 1, :] * o_cmp[g][:, r * TQ:(r + 1) * TQ]
            + sgate_t[3 * h + 1:3 * h + 2, :] * (acc_s[0:Dh] * (1.0 / acc_s[Dh:Dh + 1]))
            + sgate_t[3 * h + 2:3 * h + 3, :] * (acc_w[0:Dh] * (1.0 / acc_w[Dh:Dh + 1])))
    o_ref[0] = ot_ref[...].T.astype(o_ref.dtype)


def _nsa_attn(zq, kc, vct, ksa, kwa, vst, vwt, zgate, gate_b):
    B, S, _ = zq.shape
    assert TQ == TK and WINDOW % TK == 0 and S % TQ == 0
    n_sel = S // SEL_BLOCK
    n_cmp = (S - CMP_BLOCK) // CMP_STRIDE + 1
    n_cmp_pad = kc.shape[2]
    n_kt = S // TK
    G, Dh = NSA_KV_HEADS, NSA_HEAD_DIM
    cmp_start = np.arange(n_cmp) * CMP_STRIDE
    sel_start = np.arange(n_sel) * SEL_BLOCK
    overlap = np.clip(np.minimum(cmp_start[:, None] + CMP_BLOCK, sel_start[None, :] + SEL_BLOCK)
                      - np.maximum(cmp_start[:, None], sel_start[None, :]), 0, None) / CMP_BLOCK
    ovlt = np.zeros((n_sel, n_cmp_pad), np.float32)
    ovlt[:, :n_cmp] = overlap.T
    slopes = (2.0 ** (-8.0 * np.arange(1, NSA_HEADS + 1) / NSA_HEADS)).astype(np.float32).reshape(1, NSA_HEADS)
    gb = jnp.pad(gate_b, (0, LANE - N_GATE)).reshape(1, LANE)
    fixed = lambda shape: pl.BlockSpec(shape, lambda b, i: (0,) * len(shape))
    per_b = lambda shape: pl.BlockSpec((1,) + shape, lambda b, i: (b,) + (0,) * len(shape))
    return pl.pallas_call(
        _nsa_attn_kernel,
        grid=(B, S // TQ),
        in_specs=[pl.BlockSpec((1, TQ, NSA_WIDTH), lambda b, i: (b, i, 0)),
                  per_b((G, n_cmp_pad, 2 * Dh)), per_b((G, Dh, n_cmp_pad)),
                  per_b((G, S, 2 * Dh)), per_b((G, S, 2 * Dh)),
                  per_b((n_kt, G * V_ROWS, TK)), per_b((n_kt, G * V_ROWS, TK)),
                  pl.BlockSpec((1, TQ, LANE), lambda b, i: (b, i, 0)),
                  fixed((1, LANE)), fixed((n_sel, n_cmp_pad)), fixed((1, NSA_HEADS))],
        out_specs=pl.BlockSpec((1, TQ, NSA_WIDTH), lambda b, i: (b, i, 0)),
        out_shape=jax.ShapeDtypeStruct((B, S, NSA_WIDTH), BF16),
        scratch_shapes=[pltpu.VMEM((2, NSA_HEADS, V_ROWS, TQ), F32),
                        pltpu.VMEM((NSA_WIDTH, TQ), F32)],
        compiler_params=pltpu.CompilerParams(dimension_semantics=("parallel", "arbitrary"),
                                             vmem_limit_bytes=VMEM_LIMIT),
        name="nsa_attn",
    )(zq, kc, vct, ksa, kwa, vst, vwt, zgate, gb, jnp.asarray(ovlt, dtype=BF16), jnp.asarray(slopes))


def _mix_kernel(x_ref, yr_ref, yn_ref, zmg_ref, ur_ref, un_ref, wo_ref, gf_ref, wr_ref, br_ref,
                x1_ref, h2_ref, rt_ref, cnt_ref):
    tm = x_ref.shape[0]
    nt = (((1,), (1,)), ((), ()))
    chunks = [slice(c * TM_PROJ, (c + 1) * TM_PROJ) for c in range(tm // TM_PROJ)]
    up_r = [_bdot(yr_ref[c, :], ur_ref[...]) for c in chunks]
    up_n = [_bdot(yn_ref[c, :], un_ref[...]) for c in chunks]
    mixed = [jax.nn.sigmoid(zmg_ref[c, 0:D_MODEL].astype(F32)) * a
             + jax.nn.sigmoid(zmg_ref[c, D_MODEL:2 * D_MODEL].astype(F32)) * b
             for c, a, b in zip(chunks, up_r, up_n)]
    x1 = [x_ref[c, :] + _bdot(m, wo_ref[...]) for c, m in zip(chunks, mixed)]
    h2 = [_rms(v, gf_ref[...]) for v in x1]
    for c, v, h in zip(chunks, x1, h2):
        x1_ref[c, :] = v
        _store_row_tiles(h2_ref.at[c], h)
    n_row = wr_ref.shape[0]
    logits = jnp.concatenate([_dot3_nt(wr_ref[...], h) for h in h2], axis=1) + br_ref[...]
    row = lax.broadcasted_iota(jnp.int32, (n_row, 1), 0).astype(F32)
    gl = jnp.where(row < N_GROUPS, logits, NEG_INF)
    gmax = jnp.max(gl, axis=0, keepdims=True)
    g_sel = jnp.min(jnp.where(gl == gmax, row, float(n_row)), axis=0, keepdims=True)
    p_group = 1.0 / jnp.sum(jnp.exp(gl - gmax), axis=0, keepdims=True)
    e_row = row - N_GROUPS
    in_grp = ((e_row >= g_sel * EXPERTS_PER_GROUP) & (e_row < (g_sel + 1.0) * EXPERTS_PER_GROUP)
              & (e_row < N_EXPERTS))
    el = jnp.where(in_grp, logits, NEG_INF)
    m1 = jnp.max(el, axis=0, keepdims=True)
    i1 = jnp.min(jnp.where(el == m1, e_row, float(n_row)), axis=0, keepdims=True)
    el2 = jnp.where(e_row == i1, 2.0 * NEG_INF, el)
    m2 = jnp.max(el2, axis=0, keepdims=True)
    i2 = jnp.min(jnp.where(el2 == m2, e_row, float(n_row)), axis=0, keepdims=True)
    r2 = jnp.exp(m2 - m1)
    g1 = p_group / (1.0 + r2)
    g2 = p_group * r2 / (1.0 + r2)

    @pl.when(pl.program_id(0) == 0)
    def _():
        cnt_ref[...] = jnp.zeros_like(cnt_ref)

    pick1, pick2 = e_row == i1, e_row == i2
    both = pick1.astype(F32) + pick2.astype(F32)
    earlier = (lax.broadcasted_iota(jnp.int32, (tm, tm), 0)
               < lax.broadcasted_iota(jnp.int32, (tm, tm), 1)).astype(BF16)
    before = jnp.dot(both.astype(BF16), earlier, preferred_element_type=F32) + cnt_ref[:, 0:1]
    rank1 = jnp.sum(jnp.where(pick1, before, 0.0), axis=0, keepdims=True)
    rank2 = jnp.sum(jnp.where(pick2, before, 0.0), axis=0, keepdims=True)
    cnt_ref[...] = cnt_ref[...] + jnp.sum(both, axis=1, keepdims=True)
    rt_ref[...] = jnp.concatenate([i1, i2, g1, g2, rank1, rank2, jnp.zeros((2, tm), F32)], axis=0)


def _mix(x2d, y_rw, y_nsa, zmg, w_up_r, w_up_n, w_out, g_ffn, w_group, b_group, w_router, b_router):
    T = x2d.shape[0]
    tm = MIX_CHUNKS * TM_PROJ
    n_r = N_GROUPS + N_EXPERTS
    n_row = -(-n_r // 8) * 8
    wr = jnp.pad(jnp.concatenate([w_group, w_router], axis=1).T, ((0, n_row - n_r), (0, 0)))
    br = jnp.pad(jnp.concatenate([b_group, b_router]), (0, n_row - n_r)).reshape(n_row, 1)
    row = lambda i: (i, 0)
    fixed = lambda i: (0, 0)
    return pl.pallas_call(
        _mix_kernel,
        grid=(T // tm,),
        in_specs=[pl.BlockSpec((tm, D_MODEL), row), pl.BlockSpec((tm, RW_WIDTH), row),
                  pl.BlockSpec((tm, NSA_WIDTH), row), pl.BlockSpec((tm, 2 * D_MODEL), row),
                  pl.BlockSpec((RW_WIDTH, D_MODEL), fixed), pl.BlockSpec((NSA_WIDTH, D_MODEL), fixed),
                  pl.BlockSpec((D_MODEL, D_MODEL), fixed), pl.BlockSpec((1, D_MODEL), fixed),
                  pl.BlockSpec((n_row, D_MODEL), fixed), pl.BlockSpec((n_row, 1), fixed)],
        out_specs=[pl.BlockSpec((tm, D_MODEL), row), pl.BlockSpec((tm,) + ROW_TILE, lambda i: (i, 0, 0)),
                   pl.BlockSpec((8, tm), lambda i: (0, i)), pl.BlockSpec((n_row, LANE), fixed)],
        out_shape=[jax.ShapeDtypeStruct((T, D_MODEL), F32), jax.ShapeDtypeStruct((T,) + ROW_TILE, F32),
                   jax.ShapeDtypeStruct((8, T), F32), jax.ShapeDtypeStruct((n_row, LANE), F32)],
        compiler_params=pltpu.CompilerParams(dimension_semantics=("arbitrary",),
                                             vmem_limit_bytes=VMEM_LIMIT),
        name="mix",
    )(x2d, y_rw, y_nsa, zmg, w_up_r.astype(BF16), w_up_n.astype(BF16), w_out.astype(BF16),
      g_ffn.reshape(1, D_MODEL), wr, br)


def _route_tables(rt, cnt, T):
    n_rows = T * TOP_K + N_EXPERTS * ROW_BLOCK
    n_blk = n_rows // ROW_BLOCK
    counts = cnt[N_GROUPS:N_GROUPS + N_EXPERTS, 0].astype(jnp.int32)
    padded = (counts + ROW_BLOCK - 1) // ROW_BLOCK * ROW_BLOCK
    pends = jnp.cumsum(padded)
    pstarts = pends - padded
    expert = rt[0:TOP_K].astype(jnp.int32)
    rank = rt[2 * TOP_K:3 * TOP_K].astype(jnp.int32)
    seg_start = jnp.sum(jnp.where(expert[..., None] == jnp.arange(N_EXPERTS), pstarts, 0), axis=-1)
    dest = (seg_start + rank).T.reshape(T * TOP_K)
    gates = rt[TOP_K:2 * TOP_K].T
    blk_start = jnp.arange(n_blk) * ROW_BLOCK
    blk_expert = jnp.minimum(jnp.sum(pends[None, :] <= blk_start[:, None], axis=1), N_EXPERTS - 1)
    n_active = (pends[N_EXPERTS - 1:] // ROW_BLOCK).astype(jnp.int32)
    return dest.astype(jnp.int32), gates, blk_expert.astype(jnp.int32), n_active


ROW_TILE = (D_MODEL // LANE, LANE)


def _store_row_tiles(ref, x):
    for c in range(ROW_TILE[0]):
        ref[:, c, :] = x[:, c * LANE:(c + 1) * LANE]


def _load_row_tiles(ref, idx):
    return jnp.concatenate([ref[(*idx, slice(None), c, slice(None))] for c in range(ROW_TILE[0])], axis=1)


EXPERT_BUFS = 3
TOK_BITS = 14


def _row_info_kernel(dest_ref, fill_hbm, info_ref, sem, *, n_tok):
    fill = pltpu.make_async_copy(fill_hbm, info_ref, sem)
    fill.start()
    fill.wait()

    def body(t, carry):
        word = t + (t << TOK_BITS)
        for k in range(TOP_K):
            info_ref[dest_ref[t * TOP_K + k]] = word + ((k * n_tok) << TOK_BITS)
        return carry

    lax.fori_loop(0, n_tok, body, 0, unroll=8)


def _row_info(dest, n_tok):
    n_rows = n_tok * TOP_K + N_EXPERTS * ROW_BLOCK
    assert n_tok <= 1 << TOK_BITS and (n_tok * TOP_K + 2 * ROW_BLOCK) << TOK_BITS < 2 ** 31
    row = jnp.arange(n_rows, dtype=jnp.int32)
    spare = n_tok * TOP_K + ((row // ROW_BLOCK) % 2) * ROW_BLOCK + row % ROW_BLOCK
    return pl.pallas_call(
        functools.partial(_row_info_kernel, n_tok=n_tok),
        in_specs=[pl.BlockSpec(memory_space=pltpu.SMEM), pl.BlockSpec(memory_space=pl.ANY)],
        out_specs=pl.BlockSpec(memory_space=pltpu.SMEM),
        out_shape=jax.ShapeDtypeStruct((n_rows,), jnp.int32),
        scratch_shapes=[pltpu.SemaphoreType.DMA(())],
        name="row_info",
    )(dest, spare << TOK_BITS)


def _expert_kernel(be_ref, nact_ref, info_ref, h2_hbm, wgu_ref, wd_ref, eo_hbm,
                   xbuf, obuf, wgu_b, wd_b, gsem, ssem):
    i = pl.program_id(0)
    n_act = nact_ref[0]
    tok_mask = (1 << TOK_BITS) - 1

    def gather(blk, s):
        for r in range(ROW_BLOCK):
            tok = info_ref[blk * ROW_BLOCK + r] & tok_mask
            pltpu.make_async_copy(h2_hbm.at[tok], xbuf.at[s, r], gsem.at[s]).start()

    def scatter(blk, s):
        for r in range(ROW_BLOCK):
            row = info_ref[blk * ROW_BLOCK + r] >> TOK_BITS
            pltpu.make_async_copy(obuf.at[s, pl.ds(r, 1)], eo_hbm.at[pl.ds(row, 1)], ssem.at[s]).start()

    def drain_gather(s):
        pltpu.make_async_copy(h2_hbm.at[pl.ds(0, ROW_BLOCK)], xbuf.at[s], gsem.at[s]).wait()

    def drain_scatter(s):
        pltpu.make_async_copy(obuf.at[s], eo_hbm.at[pl.ds(0, ROW_BLOCK)], ssem.at[s]).wait()

    n_buf = xbuf.shape[0]

    def block(s, first):
        drain_gather(s)
        gather(jnp.minimum(i + 2, n_act - 1), (s + 2) % n_buf)
        if not first:
            scatter(i - 1, (s + 2) % n_buf)
        gu = jnp.dot(_load_row_tiles(xbuf, (s,)).astype(BF16), wgu_b[...], preferred_element_type=F32)
        gate_h, up_h = gu[:, :D_EXPERT], gu[:, D_EXPERT:]
        mid = gate_h * jax.nn.sigmoid(gate_h) * up_h
        obuf[s] = jnp.dot(mid.astype(BF16), wd_b[...], preferred_element_type=F32)

    @pl.when(jnp.logical_and(i < n_act, jnp.logical_or(i == 0, be_ref[i] != be_ref[jnp.maximum(i - 1, 0)])))
    def _():
        wgu_b[...] = wgu_ref[0].astype(BF16)
        wd_b[...] = wd_ref[0].astype(BF16)

    @pl.when(i == 0)
    def _():
        gather(0, 0)
        gather(jnp.minimum(1, n_act - 1), 1)
        block(0, first=True)

    for s in range(n_buf):
        mine = i % n_buf == s

        @pl.when(jnp.logical_and(mine, jnp.logical_and(i >= 1, i < n_act)))
        def _():
            @pl.when(i >= n_buf)
            def _():
                drain_scatter(s)
            block(s, first=False)

        @pl.when(jnp.logical_and(mine, i == n_act - 1))
        def _():
            for other in ((s + 1) % n_buf, (s + 2) % n_buf):
                drain_gather(other)

            @pl.when(i >= 2)
            def _():
                drain_scatter((s + 1) % n_buf)

            @pl.when(i >= 1)
            def _():
                drain_scatter((s + 2) % n_buf)
            scatter(i, s)
            drain_scatter(s)
            obuf[s] = jnp.zeros_like(obuf[s])
            for half in range(2):
                spare = eo_hbm.at[pl.ds(eo_hbm.shape[0] - (2 - half) * ROW_BLOCK, ROW_BLOCK)]
                pltpu.make_async_copy(obuf.at[s], spare, ssem.at[s]).start()
            for half in range(2):
                drain_scatter(s)


def _experts(h2, info, blk_expert, n_active, w_gate_up, w_down):
    n_tok = h2.shape[0]
    n_blk = blk_expert.shape[0]
    grid_spec = pltpu.PrefetchScalarGridSpec(
        num_scalar_prefetch=3,
        grid=(n_blk,),
        in_specs=[pl.BlockSpec(memory_space=pl.ANY),
                  pl.BlockSpec((1, D_MODEL, 2 * D_EXPERT), lambda i, be, na, info: (be[i], 0, 0)),
                  pl.BlockSpec((1, D_EXPERT, D_MODEL), lambda i, be, na, info: (be[i], 0, 0))],
        out_specs=pl.BlockSpec(memory_space=pl.ANY),
        scratch_shapes=[pltpu.VMEM((EXPERT_BUFS, ROW_BLOCK) + ROW_TILE, F32),
                        pltpu.VMEM((EXPERT_BUFS, ROW_BLOCK, D_MODEL), F32),
                        pltpu.VMEM((D_MODEL, 2 * D_EXPERT), BF16), pltpu.VMEM((D_EXPERT, D_MODEL), BF16),
                        pltpu.SemaphoreType.DMA((EXPERT_BUFS,)), pltpu.SemaphoreType.DMA((EXPERT_BUFS,))])
    return pl.pallas_call(
        _expert_kernel,
        grid_spec=grid_spec,
        out_shape=jax.ShapeDtypeStruct((n_tok * TOP_K + 2 * ROW_BLOCK, D_MODEL), F32),
        compiler_params=pltpu.CompilerParams(dimension_semantics=("arbitrary",),
                                             vmem_limit_bytes=VMEM_LIMIT),
        name="experts",
    )(blk_expert, n_active, info, h2, w_gate_up, w_down)


def _ple_kernel(x1_ref, g_ref, p_ref, *rest, last_layer):
    eo_refs, (wpp_ref, gpn_ref, ggi_ref, wpg_ref, gfin_ref, y_ref) = rest[:TOP_K], rest[TOP_K:]
    tm = x1_ref.shape[0]
    chunks = [slice(c * TM_PROJ, (c + 1) * TM_PROJ) for c in range(tm // TM_PROJ)]
    x2 = []
    for c in chunks:
        moe = g_ref[c, 0:1] * eo_refs[0][c, :]
        for k in range(1, TOP_K):
            moe = moe + g_ref[c, k:k + 1] * eo_refs[k][c, :]
        x2.append(x1_ref[c, :] + moe)
    e = [_rms(_bdot(p_ref[c, :], wpp_ref[...]), gpn_ref[...]) for c in chunks]
    gate = [jax.nn.sigmoid(_bdot(_rms(v, ggi_ref[...]), wpg_ref[...])) for v in x2]
    for c, v, g, ee in zip(chunks, x2, gate, e):
        x3 = v + g * ee
        y_ref[c, :] = _rms(x3, gfin_ref[...]) if last_layer else x3


def _ple(x1, gates, expert_out, p2d, w_pp, g_pn, g_gi, w_pg, g_final, last_layer):
    T = x1.shape[0]
    tm = MIX_CHUNKS * TM_PROJ
    nt = T // tm
    row = lambda i: (i, 0)
    fixed = lambda i: (0, 0)
    vec = lambda a: a.reshape(1, D_MODEL)
    slot_rows = [pl.BlockSpec((tm, D_MODEL), functools.partial(lambda i, k: (i + k * nt, 0), k=k))
                 for k in range(TOP_K)]
    return pl.pallas_call(
        functools.partial(_ple_kernel, last_layer=last_layer),
        grid=(nt,),
        in_specs=[pl.BlockSpec((tm, D_MODEL), row), pl.BlockSpec((tm, TOP_K), row),
                  pl.BlockSpec((tm, PLE_DIM), row)] + slot_rows + [
                  pl.BlockSpec((PLE_DIM, D_MODEL), fixed), pl.BlockSpec((1, D_MODEL), fixed),
                  pl.BlockSpec((1, D_MODEL), fixed), pl.BlockSpec((D_MODEL, D_MODEL), fixed),
                  pl.BlockSpec((1, D_MODEL), fixed)],
        out_specs=pl.BlockSpec((tm, D_MODEL), row),
        out_shape=jax.ShapeDtypeStruct((T, D_MODEL), F32),
        compiler_params=pltpu.CompilerParams(dimension_semantics=("parallel",),
                                             vmem_limit_bytes=VMEM_LIMIT),
        name="ple",
    )(x1, gates, p2d, *([expert_out] * TOP_K), w_pp.astype(BF16), vec(g_pn), vec(g_gi),
      w_pg.astype(BF16), vec(g_final))


def kernel(x, p, g_mix, w_in, mu_shift, rw_w0, rw_w2, rw_a0, rw_a2, rw_g2, rw_k_k, rw_k_a, rw_r_k, rw_ln_w, rw_ln_b, cmp_pos_k, cmp_pos_v, cmp_k_w1, cmp_k_w2, cmp_v_w1, cmp_v_w2, nsa_gate_b, w_up_rwkv, w_up_nsa, w_out, g_ffn, w_group, b_group, w_router, b_router, w_exp_gate_up, w_exp_down, w_ple_proj, g_ple_norm, g_ple_gate_in, w_ple_gate, g_final):
    B, S, D = x.shape
    T = B * S
    depth = p.shape[0]
    xc = x.reshape(T, D)
    for i in range(depth):
        zrw, zq, zkv, zmg, zgate = _proj(xc, g_mix[i], w_in[i], mu_shift[i], S)
        y_rw = _rwkv(zrw.reshape(B, S, RW_IN), rw_w0[i], rw_w2[i], rw_a0[i], rw_a2[i], rw_g2[i],
                     rw_k_k[i], rw_k_a[i], rw_r_k[i], rw_ln_w[i], rw_ln_b[i])
        zkv3 = zkv.reshape(B, S, KV_IN)
        nsa_kv = _nsa_prep(zkv3, cmp_pos_k[i], cmp_pos_v[i], cmp_k_w1[i], cmp_k_w2[i],
                           cmp_v_w1[i], cmp_v_w2[i])
        y_nsa = _nsa_attn(zq.reshape(B, S, NSA_WIDTH), *nsa_kv, zgate.reshape(B, S, LANE),
                          nsa_gate_b[i])
        x1, h2, rt, cnt = _mix(xc, y_rw.reshape(T, RW_WIDTH), y_nsa.reshape(T, NSA_WIDTH), zmg,
                               w_up_rwkv[i], w_up_nsa[i], w_out[i], g_ffn[i], w_group[i], b_group[i],
                               w_router[i], b_router[i])
        dest, gates, blk_expert, n_active = _route_tables(rt, cnt, T)
        expert_out = _experts(h2, _row_info(dest, T), blk_expert, n_active, w_exp_gate_up[i],
                              w_exp_down[i])
        xc = _ple(x1, gates, expert_out, p[i].reshape(T, PLE_DIM), w_ple_proj[i], g_ple_norm[i],
                  g_ple_gate_in[i], w_ple_gate[i], g_final, i == depth - 1)
    return xc.reshape(B, S, D)
```

```python
import functools
import math

import jax
import jax.numpy as jnp
import numpy as np
from jax import lax
from jax.experimental import pallas as pl
from jax.experimental.pallas import tpu as pltpu

F32 = jnp.float32
BF16 = jnp.bfloat16

D_MODEL = 1024
RW_HEADS = 8
RW_HEAD_DIM = 64
RW_WIDTH = 512
DECAY_LORA = 64
AAA_LORA = 64
GATE_LORA = 128
GN_EPS = 64e-5
RW_IN = 3 * RW_WIDTH + DECAY_LORA + AAA_LORA + GATE_LORA

NSA_HEADS = 8
NSA_KV_HEADS = 2
NSA_REP = NSA_HEADS // NSA_KV_HEADS
NSA_HEAD_DIM = 64
NSA_WIDTH = 512
NSA_KV_WIDTH = 128
CMP_BLOCK = 32
CMP_STRIDE = 16
SEL_BLOCK = 64
N_SELECT = 8
WINDOW = 512
N_NSA_BRANCH = 3
FORCE_SCORE = 1e6
NEG_INF = -1e30

N_GROUPS = 4
EXPERTS_PER_GROUP = 8
N_EXPERTS = 32
TOP_K = 2
D_EXPERT = 512
ROW_BLOCK = 128
PLE_DIM = 256
NORM_EPS = 1e-6

N_GATE = N_NSA_BRANCH * NSA_HEADS
ATT_IN = NSA_WIDTH + 6 * NSA_KV_WIDTH
KV_OFF = RW_IN + NSA_WIDTH
KV_IN = 6 * NSA_KV_WIDTH
GATE_OFF = RW_IN + ATT_IN
MERGE_OFF = GATE_OFF + N_GATE
LANE = 128

RW_CHUNK = 64
RW_SUB = 16
RW_ROWS = 4
TQ = 256
TK = 256
V_ROWS = NSA_HEAD_DIM + 16
TM_PROJ = 256
MIX_CHUNKS = 4
VMEM_LIMIT = 56 * 1024 * 1024


def _bdot(a, b):
    return jnp.dot(a.astype(BF16), b.astype(BF16), preferred_element_type=F32)


def _bdot_nt(a, b):
    return lax.dot_general(a.astype(BF16), b.astype(BF16), (((1,), (1,)), ((), ())),
                           preferred_element_type=F32)


def _split_bf16(x):
    hi = x.astype(BF16)
    return hi, (x - hi.astype(F32)).astype(BF16)


def _dot3_nt(a, b):
    (a_hi, a_lo), (b_hi, b_lo) = _split_bf16(a), _split_bf16(b)
    return _bdot_nt(a_hi, b_hi) + _bdot_nt(a_hi, b_lo) + _bdot_nt(a_lo, b_hi)


def _rms(x, g):
    return x * lax.rsqrt(jnp.mean(x * x, axis=-1, keepdims=True) + NORM_EPS) * g


def _proj_kernel(x_ref, g_ref, w_ref, mu_ref, zrw_ref, zq_ref, zkv_ref, zmg_ref, zgate_ref,
                 carry_ref, *, tiles_per_seq):
    i = pl.program_id(0)
    tm = x_ref.shape[0]

    @pl.when(i % tiles_per_seq == 0)
    def _():
        carry_ref[...] = jnp.zeros_like(carry_ref)

    h = _rms(x_ref[...], g_ref[...]).astype(BF16)
    z = jnp.dot(h, w_ref[:, 0:RW_IN], preferred_element_type=F32)
    row = lax.broadcasted_iota(jnp.int32, (tm, 1), 0)
    prev = jnp.where(row == 0, carry_ref[7:8, :], pltpu.roll(z, 1, 0))
    carry_ref[...] = z[tm - 8:tm, :]
    zrw_ref[...] = z + (prev - z) * mu_ref[...]
    zq_ref[...] = jnp.dot(h, w_ref[:, RW_IN:KV_OFF], preferred_element_type=F32)
    zkv_ref[...] = jnp.dot(h, w_ref[:, KV_OFF:GATE_OFF], preferred_element_type=F32)
    zmg_ref[...] = jnp.dot(h, w_ref[:, GATE_OFF:GATE_OFF + 2 * D_MODEL],
                           preferred_element_type=F32).astype(zmg_ref.dtype)
    zgate_ref[...] = jnp.dot(h, w_ref[:, GATE_OFF + 2 * D_MODEL:], preferred_element_type=F32)


def _proj(x2d, g_mix, w_in, mu, seq):
    T = x2d.shape[0]
    tm = 2 * TM_PROJ
    wp = jnp.concatenate(
        [w_in[:, :GATE_OFF], w_in[:, MERGE_OFF:],
         jnp.pad(w_in[:, GATE_OFF:MERGE_OFF], ((0, 0), (0, LANE - N_GATE)))], axis=1).astype(BF16)
    npad = wp.shape[1]
    row = lambda i: (i, 0)
    fixed = lambda i: (0, 0)
    return pl.pallas_call(
        functools.partial(_proj_kernel, tiles_per_seq=seq // tm),
        grid=(T // tm,),
        in_specs=[pl.BlockSpec((tm, D_MODEL), row), pl.BlockSpec((1, D_MODEL), fixed),
                  pl.BlockSpec((D_MODEL, npad), fixed), pl.BlockSpec((1, RW_IN), fixed)],
        out_specs=[pl.BlockSpec((tm, RW_IN), row), pl.BlockSpec((tm, NSA_WIDTH), row),
                   pl.BlockSpec((tm, KV_IN), row), pl.BlockSpec((tm, 2 * D_MODEL), row),
                   pl.BlockSpec((tm, LANE), row)],
        out_shape=[jax.ShapeDtypeStruct((T, RW_IN), F32), jax.ShapeDtypeStruct((T, NSA_WIDTH), F32),
                   jax.ShapeDtypeStruct((T, KV_IN), F32), jax.ShapeDtypeStruct((T, 2 * D_MODEL), BF16),
                   jax.ShapeDtypeStruct((T, LANE), F32)],
        scratch_shapes=[pltpu.VMEM((8, RW_IN), F32)],
        compiler_params=pltpu.CompilerParams(dimension_semantics=("arbitrary",),
                                             vmem_limit_bytes=VMEM_LIMIT),
        name="proj",
    )(x2d, g_mix.reshape(1, D_MODEL), wp, mu.reshape(1, RW_IN))


PAIR = 2 * RW_HEAD_DIM


def _pair_blocks(x):
    low = lax.broadcasted_iota(jnp.int32, (1, PAIR), 1) < RW_HEAD_DIM
    return jnp.concatenate([jnp.where(low, x, 0.0), jnp.where(low, 0.0, x)], axis=0)


def _pmm(a, b):
    return _bdot(a, _pair_blocks(b))


def _unit_lower_inverse(a_strict, sub_mask, eye):
    ad = [jnp.where(sub_mask, a, 0.0) for a in a_strict]
    ao = [a - d for a, d in zip(a_strict, ad)]
    td = [eye - d for d in ad]
    pw = ad
    for _ in range(int(math.log2(RW_SUB)) - 1):
        pw = [_pmm(x, x) for x in pw]
        td = [_pmm(t, eye + x) for t, x in zip(td, pw)]
    n = [_pmm(t, o) for t, o in zip(td, ao)]
    t = [eye - x for x in n]
    pw = n
    for _ in range(int(math.log2(RW_CHUNK // RW_SUB)) - 1):
        pw = [_pmm(x, x) for x in pw]
        t = [_pmm(a, eye + x) for a, x in zip(t, pw)]
    return [_pmm(a, d) for a, d in zip(t, td)]


def _rwkv_kernel(z_ref, w0_ref, w2_ref, a0_ref, a2_ref, g2_ref, kk_ref, ka_ref, rk_ref, lnw_ref,
                 lnb_ref, avg_ref, o_ref, h_ref):
    c = pl.program_id(1)
    C = RW_CHUNK
    n_pair = RW_WIDTH // PAIR
    nt = (((1,), (1,)), ((), ()))

    @pl.when(c == 0)
    def _():
        h_ref[...] = jnp.zeros_like(h_ref)

    ti = lax.broadcasted_iota(jnp.int32, (C, 1), 0)
    si = lax.broadcasted_iota(jnp.int32, (1, PAIR), 1) & (RW_HEAD_DIM - 1)
    incl, strict = ti >= si, ti > si
    eye = (ti == si).astype(F32)
    sub_shift = int(math.log2(RW_SUB))
    sub_mask = (ti >> sub_shift) == (si >> sub_shift)
    row2 = lax.broadcasted_iota(jnp.int32, (PAIR, 1), 0)
    col2 = lax.broadcasted_iota(jnp.int32, (1, PAIR), 1)
    same_head = (row2 < RW_HEAD_DIM) == (col2 < RW_HEAD_DIM)
    eye2 = row2 == col2
    tri = (lax.broadcasted_iota(jnp.int32, (C, C), 0)
           >= lax.broadcasted_iota(jnp.int32, (C, C), 1)).astype(BF16)

    def head_mean(x):
        xs = jnp.concatenate([x[:, p * PAIR:(p + 1) * PAIR] for p in range(n_pair)], axis=0)
        ms = _bdot(xs, avg_ref[...])
        return jnp.concatenate([ms[p * C:(p + 1) * C] for p in range(n_pair)], axis=1)

    n_rows = z_ref.shape[0]
    rows = []
    for n in range(n_rows):
        z = z_ref[n]
        zr, zk, zv = z[:, 0:512], z[:, 512:1024], z[:, 1024:1536]
        zw, za, zg = z[:, 1536:1600], z[:, 1600:1664], z[:, 1664:1792]
        w_raw = w0_ref[...] + _bdot(jnp.tanh(zw), w2_ref[...])
        logw = -jax.nn.sigmoid(w_raw) * math.exp(-0.5)
        a = jax.nn.sigmoid(a0_ref[...] + _bdot(za, a2_ref[...]))
        gate = _bdot(jax.nn.sigmoid(zg), g2_ref[...])
        kk = zk * kk_ref[...]
        kk = kk / jnp.maximum(jnp.sqrt(head_mean(kk * kk) * RW_HEAD_DIM), 1e-12)
        k = zk * (1.0 + (a - 1.0) * ka_ref[...])
        b = kk * a

        w_hi = logw.astype(BF16)
        w_lo = (logw - w_hi.astype(F32)).astype(BF16)
        cum = (jnp.dot(tri, w_hi, preferred_element_type=F32)
               + jnp.dot(tri, w_lo, preferred_element_type=F32))
        cum_last = cum[C - 1:C, :]
        g_inv = jnp.exp(-cum)
        g_end = jnp.exp(cum_last - cum)
        rows.append(dict(rt=zr * jnp.exp(cum), kt=k * g_inv, bt=b * g_inv, qt=kk * jnp.exp(cum - logw),
                         kh=k * g_end, bh=b * g_end, v=zv, g_last=jnp.exp(cum_last), gate=gate,
                         bonus=head_mean(zr * k * rk_ref[...]) * RW_HEAD_DIM * zv))

    chains = [(n, slice(p * PAIR, (p + 1) * PAIR)) for n in range(n_rows) for p in range(n_pair)]
    part = lambda name: [rows[n][name][:, sl] for n, sl in chains]
    qt, rt, kt, bt, kh, bh, v = (part(x) for x in ("qt", "rt", "kt", "bt", "kh", "bh", "v"))
    lhs = [jnp.concatenate([q, r], axis=0).astype(BF16) for q, r in zip(qt, rt)]
    ab = [lax.dot_general(l, _pair_blocks(x).astype(BF16), nt, preferred_element_type=F32)
          for l, x in zip(lhs, bt)]
    ak = [lax.dot_general(l, _pair_blocks(x).astype(BF16), nt, preferred_element_type=F32)
          for l, x in zip(lhs, kt)]
    a_kb = [jnp.where(strict, x[0:C], 0.0) for x in ab]
    a_rb = [jnp.where(incl, x[C:2 * C], 0.0) for x in ab]
    a_kk = [jnp.where(strict, x[0:C], 0.0) for x in ak]
    a_rk = [jnp.where(incl, x[C:2 * C], 0.0) for x in ak]
    t_inv = _unit_lower_inverse(a_kb, sub_mask, eye)

    h = [h_ref[n, sl.start // PAIR] for n, sl in chains]
    vb = [_pair_blocks(x) for x in v]
    rhs = [_bdot(jnp.concatenate([q, akk], axis=1), jnp.concatenate([hh, vv], axis=0))
           for q, akk, hh, vv in zip(qt, a_kk, h, vb)]
    u = [_pmm(t, x) for t, x in zip(t_inv, rhs)]
    outs = [_bdot(jnp.concatenate([r, ark, -arb], axis=1), jnp.concatenate([hh, vv, _pair_blocks(uu)], axis=0))
            for r, ark, arb, hh, vv, uu in zip(rt, a_rk, a_rb, h, vb, u)]
    upd = [_bdot(jnp.concatenate([x, -y], axis=0).T, jnp.concatenate([vv, uu], axis=0))
           for x, y, vv, uu in zip(kh, bh, v, u)]
    for (n, sl), hh, dd in zip(chains, h, upd):
        decay_col = jnp.sum(jnp.where(eye2, rows[n]["g_last"][:, sl], 0.0), axis=1, keepdims=True)
        h_ref[n, sl.start // PAIR] = decay_col * hh + jnp.where(same_head, dd, 0.0)

    for n in range(n_rows):
        o = jnp.concatenate(outs[n * n_pair:(n + 1) * n_pair], axis=1)
        d = o - head_mean(o)
        on = d * lax.rsqrt(head_mean(d * d) + GN_EPS)
        o_ref[n] = ((on * lnw_ref[...] + lnb_ref[...] + rows[n]["bonus"]) * rows[n]["gate"]).astype(o_ref.dtype)


def _rwkv(zrw, w0, w2, a0, a2, g2, k_k, k_a, r_k, ln_w, ln_b):
    B, S, _ = zrw.shape
    C = RW_CHUNK
    nb = RW_ROWS
    hid = np.arange(PAIR) // RW_HEAD_DIM
    avg = jnp.asarray((hid[:, None] == hid[None, :]).astype(np.float32) / RW_HEAD_DIM)
    vec = lambda a: a.reshape(1, RW_WIDTH)
    fixed = lambda shape: pl.BlockSpec(shape, lambda b, c: (0,) * len(shape))
    return pl.pallas_call(
        _rwkv_kernel,
        grid=(B // nb, S // C),
        in_specs=[pl.BlockSpec((nb, C, RW_IN), lambda b, c: (b, c, 0)),
                  fixed((1, RW_WIDTH)), fixed((DECAY_LORA, RW_WIDTH)),
                  fixed((1, RW_WIDTH)), fixed((AAA_LORA, RW_WIDTH)),
                  fixed((GATE_LORA, RW_WIDTH)), fixed((1, RW_WIDTH)), fixed((1, RW_WIDTH)),
                  fixed((1, RW_WIDTH)), fixed((1, RW_WIDTH)), fixed((1, RW_WIDTH)),
                  fixed((PAIR, PAIR))],
        out_specs=pl.BlockSpec((nb, C, RW_WIDTH), lambda b, c: (b, c, 0)),
        out_shape=jax.ShapeDtypeStruct((B, S, RW_WIDTH), BF16),
        scratch_shapes=[pltpu.VMEM((nb, RW_WIDTH // PAIR, PAIR, PAIR), F32)],
        compiler_params=pltpu.CompilerParams(dimension_semantics=("parallel", "arbitrary"),
                                             vmem_limit_bytes=VMEM_LIMIT),
        name="rwkv",
    )(zrw, vec(w0), w2, vec(a0), a2, g2, vec(k_k), vec(k_a), vec(r_k), vec(ln_w), vec(ln_b), avg)


def _gelu_tanh(x):
    return 0.5 * x * (1.0 + jnp.tanh(math.sqrt(2.0 / math.pi) * (x + 0.044715 * (x * x * x))))


def _key_features(pos_hi, pos_lo, block, n, n_sel):
    lane = lax.broadcasted_iota(jnp.int32, (n, NSA_HEAD_DIM), 1)
    feat = jnp.where(lane == n_sel, pos_hi, jnp.where(lane == n_sel + 1, pos_lo, 0.0))
    return feat if block is None else jnp.where(lane == block, 1.0, feat)


def _nsa_prep_kernel(zkc_ref, zvc_ref, zks_ref, zvs_ref, zkw_ref, zvw_ref, pk_ref, pv_ref, kw1_ref,
                     kw2_ref, vw1_ref, vw2_ref, kc_ref, vct_ref, ksa_ref, kwa_ref, vst_ref, vwt_ref):
    S = zkc_ref.shape[1]
    n_grp = S // CMP_STRIDE
    Dh = NSA_HEAD_DIM
    half = CMP_BLOCK // 2
    n_sel = S // SEL_BLOCK
    jrow = lax.broadcasted_iota(jnp.int32, (n_grp, 1), 0)
    cmp_feat = _key_features((jrow >> 3).astype(F32),
                             ((jrow & 7) * CMP_STRIDE).astype(F32) + 0.5 * (CMP_BLOCK - 1),
                             None, n_grp, n_sel)
    for is_v, (z_ref, pos_ref, w1_ref, w2_ref) in enumerate(((zkc_ref, pk_ref, kw1_ref, kw2_ref),
                                                            (zvc_ref, pv_ref, vw1_ref, vw2_ref))):
        lo = jnp.zeros((n_grp, NSA_KV_WIDTH), F32)
        hi = jnp.zeros((n_grp, NSA_KV_WIDTH), F32)
        for l in range(half):
            xs = z_ref[0, pl.ds(l, n_grp, stride=CMP_STRIDE), :]
            lo = lo + _bdot(xs + pos_ref[l:l + 1, :], w1_ref[l])
            hi = hi + _bdot(xs + pos_ref[half + l:half + l + 1, :], w1_ref[half + l])
        pre = lo + pltpu.roll(hi, n_grp - 1, 0)
        out = jnp.where(jrow < n_grp - 1, _bdot(_gelu_tanh(pre), w2_ref[...]), 0.0)
        out_t = out.T
        for g in range(NSA_KV_HEADS):
            if is_v:
                vct_ref[0, g] = out_t[g * Dh:(g + 1) * Dh, :].astype(BF16)
            else:
                kc_ref[0, g] = jnp.concatenate([out[:, g * Dh:(g + 1) * Dh], cmp_feat], axis=1)

    prow = lax.broadcasted_iota(jnp.int32, (S, 1), 0)
    p_hi, p_lo = (prow >> 7).astype(F32), (prow & (LANE - 1)).astype(F32)
    for z_ref, out_ref, block in ((zks_ref, ksa_ref, prow >> int(math.log2(SEL_BLOCK))),
                                  (zkw_ref, kwa_ref, None)):
        kfull = z_ref[0]
        key_feat = _key_features(p_hi, p_lo, block, S, n_sel)
        for g in range(NSA_KV_HEADS):
            out_ref[0, g] = jnp.concatenate([kfull[:, g * Dh:(g + 1) * Dh], key_feat], axis=1).astype(BF16)
    ones_row = (lax.broadcasted_iota(jnp.int32, (V_ROWS - Dh, TK), 0) == 0).astype(F32)
    for z_ref, out_ref in ((zvs_ref, vst_ref), (zvw_ref, vwt_ref)):
        for j in range(S // TK):
            vt = z_ref[0, j * TK:(j + 1) * TK, :].T
            out_ref[0, j] = jnp.concatenate(
                [piece for g in range(NSA_KV_HEADS) for piece in (vt[g * Dh:(g + 1) * Dh], ones_row)],
                axis=0).astype(BF16)


def _nsa_prep(zkv, pos_k, pos_v, kw1, kw2, vw1, vw2):
    B, S, _ = zkv.shape
    n_grp = S // CMP_STRIDE
    n_kt = S // TK
    Dh = NSA_HEAD_DIM
    G = NSA_KV_HEADS
    fixed = lambda shape: pl.BlockSpec(shape, lambda b: (0,) * len(shape))
    col = lambda c: pl.BlockSpec((1, S, NSA_KV_WIDTH), lambda b: (b, 0, c))
    whole = lambda shape: pl.BlockSpec((1,) + shape, lambda b: (b,) + (0,) * len(shape))
    shapes = [((G, n_grp, 2 * Dh), F32), ((G, Dh, n_grp), BF16), ((G, S, 2 * Dh), BF16),
              ((G, S, 2 * Dh), BF16), ((n_kt, G * V_ROWS, TK), BF16), ((n_kt, G * V_ROWS, TK), BF16)]
    eye = jnp.eye(G, dtype=F32)
    per_group = lambda w: jnp.einsum("ab,...ij->...aibj", eye, w).reshape(
        w.shape[:-2] + (G * w.shape[-2], G * w.shape[-1])).astype(BF16)
    kw1, vw1 = (per_group(w.reshape(CMP_BLOCK, Dh, Dh)) for w in (kw1, vw1))
    kw2, vw2 = per_group(kw2), per_group(vw2)
    pos_k, pos_v = (jnp.concatenate([p] * G, axis=1) for p in (pos_k, pos_v))
    return pl.pallas_call(
        _nsa_prep_kernel,
        grid=(B,),
        in_specs=[col(c) for c in range(6)] + [
            fixed((CMP_BLOCK, G * Dh)), fixed((CMP_BLOCK, G * Dh)),
            fixed((CMP_BLOCK, G * Dh, G * Dh)), fixed((G * Dh, G * Dh)),
            fixed((CMP_BLOCK, G * Dh, G * Dh)), fixed((G * Dh, G * Dh))],
        out_specs=[whole(s) for s, _ in shapes],
        out_shape=[jax.ShapeDtypeStruct((B,) + s, d) for s, d in shapes],
        compiler_params=pltpu.CompilerParams(dimension_semantics=("parallel",),
                                             vmem_limit_bytes=VMEM_LIMIT),
        name="nsa_prep",
    )(zkv, zkv, zkv, zkv, zkv, zkv, pos_k, pos_v, kw1, kw2, vw1, vw2)


def _nsa_attn_kernel(q_ref, kc_ref, vct_ref, ksa_ref, kwa_ref, vst_ref, vwt_ref, gl_ref, gb_ref,
                     ovlt_ref, slope_ref, o_ref, acc_ref, ot_ref):
    i = pl.program_id(1)
    Dh = NSA_HEAD_DIM
    R = NSA_REP
    N = R * TQ
    n_cmp_pad = kc_ref.shape[2]
    n_sel = ovlt_ref.shape[0]
    G = NSA_KV_HEADS
    nt = (((1,), (1,)), ((), ()))
    log2e = math.log2(math.e)
    t0 = i * TQ
    t_row = t0 + lax.broadcasted_iota(jnp.int32, (1, TQ), 1)
    c_col = lax.broadcasted_iota(jnp.int32, (TK, 1), 0)
    sgate_t = jax.nn.sigmoid(gl_ref[0] + gb_ref[...]).T
    lane_f = lax.broadcasted_iota(jnp.int32, (1, Dh), 1)
    heads = lambda x: jnp.concatenate([x] * R, axis=1)

    def key_dist(j):
        return t_row - (j * TK + c_col)

    def queries(g):
        parts = []
        for r in range(R):
            h = g * R + r
            sl = slope_ref[:, h:h + 1] * log2e
            feat = jnp.where(lane_f == n_sel, sl * LANE, jnp.where(lane_f == n_sel + 1, sl, 0.0))
            parts.append(jnp.concatenate(
                [q_ref[0, :, h * Dh:(h + 1) * Dh] * (Dh ** -0.5 * log2e), jnp.broadcast_to(feat, (TQ, Dh))],
                axis=1))
        return jnp.concatenate(parts, axis=0)

    sel_lanes = ((lax.broadcasted_iota(jnp.int32, (1, 2 * Dh), 1) >= Dh)
                 & (lax.broadcasted_iota(jnp.int32, (1, 2 * Dh), 1) < Dh + n_sel))
    qab, o_cmp = [], []
    for g in range(G):
        jc = lax.broadcasted_iota(jnp.int32, (n_cmp_pad, 1), 0)
        ok_c = (jc * CMP_STRIDE + (CMP_BLOCK - 1) <= t_row) & (jc < n_cmp_pad - 1)
        (q_hi, q_lo), (k_hi, k_lo) = _split_bf16(queries(g)), _split_bf16(kc_ref[0, g])
        s_c = (lax.dot_general(k_hi, q_hi, nt, preferred_element_type=F32)
               + lax.dot_general(k_hi, q_lo, nt, preferred_element_type=F32)
               + lax.dot_general(k_lo, q_hi, nt, preferred_element_type=F32)
               + heads(jnp.where(ok_c, 0.0, NEG_INF)))
        e_c = jnp.exp2(s_c - jnp.max(s_c, axis=0, keepdims=True))
        any_c = heads(t_row >= CMP_BLOCK - 1)
        p_c = e_c * jnp.where(any_c, 1.0 / jnp.sum(e_c, axis=0, keepdims=True), 0.0)
        o_cmp.append(jnp.dot(vct_ref[0, g], p_c.astype(BF16), preferred_element_type=F32))

        p_sum = p_c[:, 0:TQ]
        for r in range(1, R):
            p_sum = p_sum + p_c[:, r * TQ:(r + 1) * TQ]
        p_hi, p_lo = _split_bf16(p_sum)
        imp = (jnp.dot(ovlt_ref[...], p_hi, preferred_element_type=F32)
               + jnp.dot(ovlt_ref[...], p_lo, preferred_element_type=F32))
        kb = lax.broadcasted_iota(jnp.int32, (n_sel, 1), 0)
        kbf = kb.astype(F32)
        blk_t = t_row >> int(math.log2(SEL_BLOCK))
        forced = (kb == 0) | (kb == blk_t) | (kb == blk_t - 1)
        cur = jnp.where(forced, FORCE_SCORE, jnp.where(kb <= blk_t, imp, -FORCE_SCORE))
        sel_bias = jnp.full((n_sel, TQ), NEG_INF, F32)
        for _ in range(min(N_SELECT, n_sel)):
            mx = jnp.max(cur, axis=0, keepdims=True)
            first = jnp.min(jnp.where(cur == mx, kbf, float(n_sel)), axis=0, keepdims=True)
            hit = kbf == first
            sel_bias = jnp.where(hit, 0.0, sel_bias)
            cur = jnp.where(hit, -3e38, cur)
        sel_feat = jnp.concatenate([jnp.zeros((Dh, TQ), F32), sel_bias,
                                    jnp.zeros((Dh - n_sel, TQ), F32)], axis=0).T.astype(BF16)
        qab.append(jnp.where(sel_lanes, jnp.concatenate([sel_feat] * R, axis=0), q_hi))

    VR = vst_ref.shape[2] // G

    SEL, WIN = (ksa_ref, vst_ref, 0), (kwa_ref, vwt_ref, 1)

    def tile(j, branches, keys=(0, TK), cols=(0, TQ)):
        (k0, k1), (c0, c1) = keys, cols
        chains = [(b, h) for b in range(len(branches)) for h in range(NSA_HEADS)]
        ks = [[k_ref[0, g, pl.ds(pl.multiple_of(j * TK, TK) + k0, k1 - k0), :] for g in range(G)]
              for (k_ref, _, _), _, _ in branches]
        vts = [[vt_ref[0, j, g * VR:(g + 1) * VR, k0:k1] for g in range(G)]
               for (_, vt_ref, _), _, _ in branches]
        m_old = {(b, h): branches[b][1][h][:, c0:c1] for b, h in chains}
        s = {(b, h): lax.dot_general(ks[b][h // R], qab[h // R][(h % R) * TQ + c0:(h % R) * TQ + c1], nt,
                                     preferred_element_type=F32) for b, h in chains}
        s = {(b, h): x if branches[b][2] is None else x + branches[b][2] for (b, h), x in s.items()}
        m_new = {c: jnp.maximum(m_old[c], jnp.max(s[c], axis=0, keepdims=True)) for c in chains}
        alpha = {c: jnp.exp2(m_old[c] - m_new[c]) for c in chains}
        p = {c: jnp.exp2(s[c] - m_new[c]).astype(BF16) for c in chains}
        pv = {(b, h): jnp.dot(vts[b][h // R], p[b, h], preferred_element_type=F32) for b, h in chains}
        for b, h in chains:
            slot = branches[b][0][2]
            acc_ref[slot, h, :, c0:c1] = alpha[b, h] * acc_ref[slot, h, :, c0:c1] + pv[b, h]

        def merged(b, h):
            m = branches[b][1][h]
            return jnp.concatenate([m[:, 0:c0], m_new[b, h], m[:, c1:TQ]], axis=1) if (c0, c1) != (0, TQ) \
                else m_new[b, h]
        return tuple(tuple(merged(b, h) for h in range(NSA_HEADS)) for b in range(len(branches)))

    def window_bias(j):
        d = key_dist(j)
        return jnp.where((d >= 0) & (d < WINDOW), 0.0, NEG_INF)

    init = (jnp.full((1, TQ), NEG_INF, F32),) * NSA_HEADS
    acc_ref[...] = jnp.zeros_like(acc_ref)
    causal = jnp.where(key_dist(i) >= 0, 0.0, NEG_INF)
    win_lo = jnp.maximum(i - WINDOW // TK, 0)
    m_sel = lax.fori_loop(0, win_lo, lambda j, m: tile(j, [(SEL, m, None)])[0], init)
    m_sel, m_win = lax.fori_loop(
        win_lo, i, lambda j, ms: tile(j, [(SEL, ms[0], None), (WIN, ms[1], window_bias(j))]), (m_sel, init))
    half = TK // 2
    m_sel, m_win = tile(i, [(SEL, m_sel, causal[0:half]), (WIN, m_win, causal[0:half])], keys=(0, half))
    tile(i, [(SEL, m_sel, causal[half:TK, half:TQ]), (WIN, m_win, causal[half:TK, half:TQ])],
         keys=(half, TK), cols=(half, TQ))

    for h in range(NSA_HEADS):
        g, r = divmod(h, R)
        acc_s, acc_w = acc_ref[0, h], acc_ref[1, h]
        ot_ref[h * Dh:(h + 1) * Dh, :] = (
            sgate_t[3 * h:3 * h + 1, :] * o_cmp[g][:, r * TQ:(r + 1) * TQ]
            + sgate_t[3 * h + 1:3 * h + 2, :] * (acc_s[0:Dh] * (1.0 / acc_s[Dh:Dh + 1]))
            + sgate_t[3 * h + 2:3 * h + 3, :] * (acc_w[0:Dh] * (1.0 / acc_w[Dh:Dh + 1])))
    o_ref[0] = ot_ref[...].T.astype(o_ref.dtype)


def _nsa_attn(zq, kc, vct, ksa, kwa, vst, vwt, zgate, gate_b):
    B, S, _ = zq.shape
    assert TQ == TK and WINDOW % TK == 0 and S % TQ == 0
    n_sel = S // SEL_BLOCK
    n_cmp = (S - CMP_BLOCK) // CMP_STRIDE + 1
    n_cmp_pad = kc.shape[2]
    n_kt = S // TK
    G, Dh = NSA_KV_HEADS, NSA_HEAD_DIM
    cmp_start = np.arange(n_cmp) * CMP_STRIDE
    sel_start = np.arange(n_sel) * SEL_BLOCK
    overlap = np.clip(np.minimum(cmp_start[:, None] + CMP_BLOCK, sel_start[None, :] + SEL_BLOCK)
                      - np.maximum(cmp_start[:, None], sel_start[None, :]), 0, None) / CMP_BLOCK
    ovlt = np.zeros((n_sel, n_cmp_pad), np.float32)
    ovlt[:, :n_cmp] = overlap.T
    slopes = (2.0 ** (-8.0 * np.arange(1, NSA_HEADS + 1) / NSA_HEADS)).astype(np.float32).reshape(1, NSA_HEADS)
    gb = jnp.pad(gate_b, (0, LANE - N_GATE)).reshape(1, LANE)
    fixed = lambda shape: pl.BlockSpec(shape, lambda b, i: (0,) * len(shape))
    per_b = lambda shape: pl.BlockSpec((1,) + shape, lambda b, i: (b,) + (0,) * len(shape))
    return pl.pallas_call(
        _nsa_attn_kernel,
        grid=(B, S // TQ),
        in_specs=[pl.BlockSpec((1, TQ, NSA_WIDTH), lambda b, i: (b, i, 0)),
                  per_b((G, n_cmp_pad, 2 * Dh)), per_b((G, Dh, n_cmp_pad)),
                  per_b((G, S, 2 * Dh)), per_b((G, S, 2 * Dh)),
                  per_b((n_kt, G * V_ROWS, TK)), per_b((n_kt, G * V_ROWS, TK)),
                  pl.BlockSpec((1, TQ, LANE), lambda b, i: (b, i, 0)),
                  fixed((1, LANE)), fixed((n_sel, n_cmp_pad)), fixed((1, NSA_HEADS))],
        out_specs=pl.BlockSpec((1, TQ, NSA_WIDTH), lambda b, i: (b, i, 0)),
        out_shape=jax.ShapeDtypeStruct((B, S, NSA_WIDTH), BF16),
        scratch_shapes=[pltpu.VMEM((2, NSA_HEADS, V_ROWS, TQ), F32),
                        pltpu.VMEM((NSA_WIDTH, TQ), F32)],
        compiler_params=pltpu.CompilerParams(dimension_semantics=("parallel", "arbitrary"),
                                             vmem_limit_bytes=VMEM_LIMIT),
        name="nsa_attn",
    )(zq, kc, vct, ksa, kwa, vst, vwt, zgate, gb, jnp.asarray(ovlt, dtype=BF16), jnp.asarray(slopes))


def _mix_kernel(x_ref, yr_ref, yn_ref, zmg_ref, ur_ref, un_ref, wo_ref, gf_ref, wr_ref, br_ref,
                x1_ref, h2_ref, rt_ref, cnt_ref):
    tm = x_ref.shape[0]
    nt = (((1,), (1,)), ((), ()))
    chunks = [slice(c * TM_PROJ, (c + 1) * TM_PROJ) for c in range(tm // TM_PROJ)]
    up_r = [_bdot(yr_ref[c, :], ur_ref[...]) for c in chunks]
    up_n = [_bdot(yn_ref[c, :], un_ref[...]) for c in chunks]
    mixed = [jax.nn.sigmoid(zmg_ref[c, 0:D_MODEL].astype(F32)) * a
             + jax.nn.sigmoid(zmg_ref[c, D_MODEL:2 * D_MODEL].astype(F32)) * b
             for c, a, b in zip(chunks, up_r, up_n)]
    x1 = [x_ref[c, :] + _bdot(m, wo_ref[...]) for c, m in zip(chunks, mixed)]
    h2 = [_rms(v, gf_ref[...]) for v in x1]
    for c, v, h in zip(chunks, x1, h2):
        x1_ref[c, :] = v
        h2_ref[c, :] = h
    n_row = wr_ref.shape[0]
    logits = jnp.concatenate([_dot3_nt(wr_ref[...], h) for h in h2], axis=1) + br_ref[...]
    row = lax.broadcasted_iota(jnp.int32, (n_row, 1), 0).astype(F32)
    gl = jnp.where(row < N_GROUPS, logits, NEG_INF)
    gmax = jnp.max(gl, axis=0, keepdims=True)
    g_sel = jnp.min(jnp.where(gl == gmax, row, float(n_row)), axis=0, keepdims=True)
    p_group = 1.0 / jnp.sum(jnp.exp(gl - gmax), axis=0, keepdims=True)
    e_row = row - N_GROUPS
    in_grp = ((e_row >= g_sel * EXPERTS_PER_GROUP) & (e_row < (g_sel + 1.0) * EXPERTS_PER_GROUP)
              & (e_row < N_EXPERTS))
    el = jnp.where(in_grp, logits, NEG_INF)
    m1 = jnp.max(el, axis=0, keepdims=True)
    i1 = jnp.min(jnp.where(el == m1, e_row, float(n_row)), axis=0, keepdims=True)
    el2 = jnp.where(e_row == i1, 2.0 * NEG_INF, el)
    m2 = jnp.max(el2, axis=0, keepdims=True)
    i2 = jnp.min(jnp.where(el2 == m2, e_row, float(n_row)), axis=0, keepdims=True)
    r2 = jnp.exp(m2 - m1)
    g1 = p_group / (1.0 + r2)
    g2 = p_group * r2 / (1.0 + r2)

    @pl.when(pl.program_id(0) == 0)
    def _():
        cnt_ref[...] = jnp.zeros_like(cnt_ref)

    pick1, pick2 = e_row == i1, e_row == i2
    both = pick1.astype(F32) + pick2.astype(F32)
    earlier = (lax.broadcasted_iota(jnp.int32, (tm, tm), 0)
               < lax.broadcasted_iota(jnp.int32, (tm, tm), 1)).astype(BF16)
    before = jnp.dot(both.astype(BF16), earlier, preferred_element_type=F32) + cnt_ref[:, 0:1]
    rank1 = jnp.sum(jnp.where(pick1, before, 0.0), axis=0, keepdims=True)
    rank2 = jnp.sum(jnp.where(pick2, before, 0.0), axis=0, keepdims=True)
    cnt_ref[...] = cnt_ref[...] + jnp.sum(both, axis=1, keepdims=True)
    rt_ref[...] = jnp.concatenate([i1, i2, g1, g2, rank1, rank2, jnp.zeros((2, tm), F32)], axis=0)


def _mix(x2d, y_rw, y_nsa, zmg, w_up_r, w_up_n, w_out, g_ffn, w_group, b_group, w_router, b_router):
    T = x2d.shape[0]
    tm = MIX_CHUNKS * TM_PROJ
    n_r = N_GROUPS + N_EXPERTS
    n_row = -(-n_r // 8) * 8
    wr = jnp.pad(jnp.concatenate([w_group, w_router], axis=1).T, ((0, n_row - n_r), (0, 0)))
    br = jnp.pad(jnp.concatenate([b_group, b_router]), (0, n_row - n_r)).reshape(n_row, 1)
    row = lambda i: (i, 0)
    fixed = lambda i: (0, 0)
    return pl.pallas_call(
        _mix_kernel,
        grid=(T // tm,),
        in_specs=[pl.BlockSpec((tm, D_MODEL), row), pl.BlockSpec((tm, RW_WIDTH), row),
                  pl.BlockSpec((tm, NSA_WIDTH), row), pl.BlockSpec((tm, 2 * D_MODEL), row),
                  pl.BlockSpec((RW_WIDTH, D_MODEL), fixed), pl.BlockSpec((NSA_WIDTH, D_MODEL), fixed),
                  pl.BlockSpec((D_MODEL, D_MODEL), fixed), pl.BlockSpec((1, D_MODEL), fixed),
                  pl.BlockSpec((n_row, D_MODEL), fixed), pl.BlockSpec((n_row, 1), fixed)],
        out_specs=[pl.BlockSpec((tm, D_MODEL), row), pl.BlockSpec((tm, D_MODEL), row),
                   pl.BlockSpec((8, tm), lambda i: (0, i)), pl.BlockSpec((n_row, LANE), fixed)],
        out_shape=[jax.ShapeDtypeStruct((T, D_MODEL), F32), jax.ShapeDtypeStruct((T, D_MODEL), F32),
                   jax.ShapeDtypeStruct((8, T), F32), jax.ShapeDtypeStruct((n_row, LANE), F32)],
        compiler_params=pltpu.CompilerParams(dimension_semantics=("arbitrary",),
                                             vmem_limit_bytes=VMEM_LIMIT),
        name="mix",
    )(x2d, y_rw, y_nsa, zmg, w_up_r.astype(BF16), w_up_n.astype(BF16), w_out.astype(BF16),
      g_ffn.reshape(1, D_MODEL), wr, br)


def _route_tables(rt, cnt, T):
    n_rows = T * TOP_K + N_EXPERTS * ROW_BLOCK
    n_blk = n_rows // ROW_BLOCK
    counts = cnt[N_GROUPS:N_GROUPS + N_EXPERTS, 0].astype(jnp.int32)
    padded = (counts + ROW_BLOCK - 1) // ROW_BLOCK * ROW_BLOCK
    pends = jnp.cumsum(padded)
    pstarts = pends - padded
    expert = rt[0:TOP_K].astype(jnp.int32)
    rank = rt[2 * TOP_K:3 * TOP_K].astype(jnp.int32)
    seg_start = jnp.sum(jnp.where(expert[..., None] == jnp.arange(N_EXPERTS), pstarts, 0), axis=-1)
    dest = (seg_start + rank).T.reshape(T * TOP_K)
    gates = rt[TOP_K:2 * TOP_K].T
    blk_start = jnp.arange(n_blk) * ROW_BLOCK
    blk_expert = jnp.minimum(jnp.sum(pends[None, :] <= blk_start[:, None], axis=1), N_EXPERTS - 1)
    n_active = (pends[N_EXPERTS - 1:] // ROW_BLOCK).astype(jnp.int32)
    return dest.astype(jnp.int32), gates, blk_expert.astype(jnp.int32), n_active


EXPERT_BUFS = 3
TOK_BITS = 14


def _row_info_kernel(dest_ref, fill_hbm, info_ref, sem, *, n_tok):
    fill = pltpu.make_async_copy(fill_hbm, info_ref, sem)
    fill.start()
    fill.wait()

    def body(t, carry):
        word = t + (t << TOK_BITS)
        for k in range(TOP_K):
            info_ref[dest_ref[t * TOP_K + k]] = word + ((k * n_tok) << TOK_BITS)
        return carry

    lax.fori_loop(0, n_tok, body, 0, unroll=8)


def _row_info(dest, n_tok):
    n_rows = n_tok * TOP_K + N_EXPERTS * ROW_BLOCK
    assert n_tok <= 1 << TOK_BITS and (n_tok * TOP_K + 2 * ROW_BLOCK) << TOK_BITS < 2 ** 31
    row = jnp.arange(n_rows, dtype=jnp.int32)
    spare = n_tok * TOP_K + ((row // ROW_BLOCK) % 2) * ROW_BLOCK + row % ROW_BLOCK
    return pl.pallas_call(
        functools.partial(_row_info_kernel, n_tok=n_tok),
        in_specs=[pl.BlockSpec(memory_space=pltpu.SMEM), pl.BlockSpec(memory_space=pl.ANY)],
        out_specs=pl.BlockSpec(memory_space=pltpu.SMEM),
        out_shape=jax.ShapeDtypeStruct((n_rows,), jnp.int32),
        scratch_shapes=[pltpu.SemaphoreType.DMA(())],
        name="row_info",
    )(dest, spare << TOK_BITS)


def _expert_kernel(be_ref, nact_ref, info_ref, h2_hbm, wgu_ref, wd_ref, eo_hbm,
                   xbuf, obuf, wgu_b, wd_b, gsem, ssem):
    i = pl.program_id(0)
    n_act = nact_ref[0]
    tok_mask = (1 << TOK_BITS) - 1

    def gather(blk, s):
        for r in range(ROW_BLOCK):
            tok = info_ref[blk * ROW_BLOCK + r] & tok_mask
            pltpu.make_async_copy(h2_hbm.at[pl.ds(tok, 1)], xbuf.at[s, pl.ds(r, 1)], gsem.at[s]).start()

    def scatter(blk, s):
        for r in range(ROW_BLOCK):
            row = info_ref[blk * ROW_BLOCK + r] >> TOK_BITS
            pltpu.make_async_copy(obuf.at[s, pl.ds(r, 1)], eo_hbm.at[pl.ds(row, 1)], ssem.at[s]).start()

    def drain_gather(s):
        pltpu.make_async_copy(h2_hbm.at[pl.ds(0, ROW_BLOCK)], xbuf.at[s], gsem.at[s]).wait()

    def drain_scatter(s):
        pltpu.make_async_copy(obuf.at[s], eo_hbm.at[pl.ds(0, ROW_BLOCK)], ssem.at[s]).wait()

    n_buf = xbuf.shape[0]

    def block(s, first):
        drain_gather(s)
        gather(jnp.minimum(i + 2, n_act - 1), (s + 2) % n_buf)
        if not first:
            scatter(i - 1, (s + 2) % n_buf)
        gu = jnp.dot(xbuf[s].astype(BF16), wgu_b[...], preferred_element_type=F32)
        gate_h, up_h = gu[:, :D_EXPERT], gu[:, D_EXPERT:]
        mid = gate_h * jax.nn.sigmoid(gate_h) * up_h
        obuf[s] = jnp.dot(mid.astype(BF16), wd_b[...], preferred_element_type=F32)

    @pl.when(jnp.logical_and(i < n_act, jnp.logical_or(i == 0, be_ref[i] != be_ref[jnp.maximum(i - 1, 0)])))
    def _():
        wgu_b[...] = wgu_ref[0].astype(BF16)
        wd_b[...] = wd_ref[0].astype(BF16)

    @pl.when(i == 0)
    def _():
        gather(0, 0)
        gather(jnp.minimum(1, n_act - 1), 1)
        block(0, first=True)

    for s in range(n_buf):
        mine = i % n_buf == s

        @pl.when(jnp.logical_and(mine, jnp.logical_and(i >= 1, i < n_act)))
        def _():
            @pl.when(i >= n_buf)
            def _():
                drain_scatter(s)
            block(s, first=False)

        @pl.when(jnp.logical_and(mine, i == n_act - 1))
        def _():
            for other in ((s + 1) % n_buf, (s + 2) % n_buf):
                drain_gather(other)

            @pl.when(i >= 2)
            def _():
                drain_scatter((s + 1) % n_buf)

            @pl.when(i >= 1)
            def _():
                drain_scatter((s + 2) % n_buf)
            scatter(i, s)
            drain_scatter(s)
            obuf[s] = jnp.zeros_like(obuf[s])
            for half in range(2):
                spare = eo_hbm.at[pl.ds(eo_hbm.shape[0] - (2 - half) * ROW_BLOCK, ROW_BLOCK)]
                pltpu.make_async_copy(obuf.at[s], spare, ssem.at[s]).start()
            for half in range(2):
                drain_scatter(s)


def _experts(h2, info, blk_expert, n_active, w_gate_up, w_down):
    n_tok = h2.shape[0]
    n_blk = blk_expert.shape[0]
    grid_spec = pltpu.PrefetchScalarGridSpec(
        num_scalar_prefetch=3,
        grid=(n_blk,),
        in_specs=[pl.BlockSpec(memory_space=pl.ANY),
                  pl.BlockSpec((1, D_MODEL, 2 * D_EXPERT), lambda i, be, na, info: (be[i], 0, 0)),
                  pl.BlockSpec((1, D_EXPERT, D_MODEL), lambda i, be, na, info: (be[i], 0, 0))],
        out_specs=pl.BlockSpec(memory_space=pl.ANY),
        scratch_shapes=[pltpu.VMEM((EXPERT_BUFS, ROW_BLOCK, D_MODEL), F32),
                        pltpu.VMEM((EXPERT_BUFS, ROW_BLOCK, D_MODEL), F32),
                        pltpu.VMEM((D_MODEL, 2 * D_EXPERT), BF16), pltpu.VMEM((D_EXPERT, D_MODEL), BF16),
                        pltpu.SemaphoreType.DMA((EXPERT_BUFS,)), pltpu.SemaphoreType.DMA((EXPERT_BUFS,))])
    return pl.pallas_call(
        _expert_kernel,
        grid_spec=grid_spec,
        out_shape=jax.ShapeDtypeStruct((n_tok * TOP_K + 2 * ROW_BLOCK, D_MODEL), F32),
        compiler_params=pltpu.CompilerParams(dimension_semantics=("arbitrary",),
                                             vmem_limit_bytes=VMEM_LIMIT),
        name="experts",
    )(blk_expert, n_active, info, h2, w_gate_up, w_down)


def _ple_kernel(x1_ref, g_ref, p_ref, *rest, last_layer):
    eo_refs, (wpp_ref, gpn_ref, ggi_ref, wpg_ref, gfin_ref, y_ref) = rest[:TOP_K], rest[TOP_K:]
    tm = x1_ref.shape[0]
    chunks = [slice(c * TM_PROJ, (c + 1) * TM_PROJ) for c in range(tm // TM_PROJ)]
    x2 = []
    for c in chunks:
        moe = g_ref[c, 0:1] * eo_refs[0][c, :]
        for k in range(1, TOP_K):
            moe = moe + g_ref[c, k:k + 1] * eo_refs[k][c, :]
        x2.append(x1_ref[c, :] + moe)
    e = [_rms(_bdot(p_ref[c, :], wpp_ref[...]), gpn_ref[...]) for c in chunks]
    gate = [jax.nn.sigmoid(_bdot(_rms(v, ggi_ref[...]), wpg_ref[...])) for v in x2]
    for c, v, g, ee in zip(chunks, x2, gate, e):
        x3 = v + g * ee
        y_ref[c, :] = _rms(x3, gfin_ref[...]) if last_layer else x3


def _ple(x1, gates, expert_out, p2d, w_pp, g_pn, g_gi, w_pg, g_final, last_layer):
    T = x1.shape[0]
    tm = MIX_CHUNKS * TM_PROJ
    nt = T // tm
    row = lambda i: (i, 0)
    fixed = lambda i: (0, 0)
    vec = lambda a: a.reshape(1, D_MODEL)
    slot_rows = [pl.BlockSpec((tm, D_MODEL), functools.partial(lambda i, k: (i + k * nt, 0), k=k))
                 for k in range(TOP_K)]
    return pl.pallas_call(
        functools.partial(_ple_kernel, last_layer=last_layer),
        grid=(nt,),
        in_specs=[pl.BlockSpec((tm, D_MODEL), row), pl.BlockSpec((tm, TOP_K), row),
                  pl.BlockSpec((tm, PLE_DIM), row)] + slot_rows + [
                  pl.BlockSpec((PLE_DIM, D_MODEL), fixed), pl.BlockSpec((1, D_MODEL), fixed),
                  pl.BlockSpec((1, D_MODEL), fixed), pl.BlockSpec((D_MODEL, D_MODEL), fixed),
                  pl.BlockSpec((1, D_MODEL), fixed)],
        out_specs=pl.BlockSpec((tm, D_MODEL), row),
        out_shape=jax.ShapeDtypeStruct((T, D_MODEL), F32),
        compiler_params=pltpu.CompilerParams(dimension_semantics=("parallel",),
                                             vmem_limit_bytes=VMEM_LIMIT),
        name="ple",
    )(x1, gates, p2d, *([expert_out] * TOP_K), w_pp.astype(BF16), vec(g_pn), vec(g_gi),
      w_pg.astype(BF16), vec(g_final))


def kernel(x, p, g_mix, w_in, mu_shift, rw_w0, rw_w2, rw_a0, rw_a2, rw_g2, rw_k_k, rw_k_a, rw_r_k, rw_ln_w, rw_ln_b, cmp_pos_k, cmp_pos_v, cmp_k_w1, cmp_k_w2, cmp_v_w1, cmp_v_w2, nsa_gate_b, w_up_rwkv, w_up_nsa, w_out, g_ffn, w_group, b_group, w_router, b_router, w_exp_gate_up, w_exp_down, w_ple_proj, g_ple_norm, g_ple_gate_in, w_ple_gate, g_final):
    B, S, D = x.shape
    T = B * S
    depth = p.shape[0]
    xc = x.reshape(T, D)
    for i in range(depth):
        zrw, zq, zkv, zmg, zgate = _proj(xc, g_mix[i], w_in[i], mu_shift[i], S)
        y_rw = _rwkv(zrw.reshape(B, S, RW_IN), rw_w0[i], rw_w2[i], rw_a0[i], rw_a2[i], rw_g2[i],
                     rw_k_k[i], rw_k_a[i], rw_r_k[i], rw_ln_w[i], rw_ln_b[i])
        zkv3 = zkv.reshape(B, S, KV_IN)
        nsa_kv = _nsa_prep(zkv3, cmp_pos_k[i], cmp_pos_v[i], cmp_k_w1[i], cmp_k_w2[i],
                           cmp_v_w1[i], cmp_v_w2[i])
        y_nsa = _nsa_attn(zq.reshape(B, S, NSA_WIDTH), *nsa_kv, zgate.reshape(B, S, LANE),
                          nsa_gate_b[i])
        x1, h2, rt, cnt = _mix(xc, y_rw.reshape(T, RW_WIDTH), y_nsa.reshape(T, NSA_WIDTH), zmg,
                               w_up_rwkv[i], w_up_nsa[i], w_out[i], g_ffn[i], w_group[i], b_group[i],
                               w_router[i], b_router[i])
        dest, gates, blk_expert, n_active = _route_tables(rt, cnt, T)
        expert_out = _experts(h2, _row_info(dest, T), blk_expert, n_active, w_exp_gate_up[i],
                              w_exp_down[i])
        xc = _ple(x1, gates, expert_out, p[i].reshape(T, PLE_DIM), w_ple_proj[i], g_ple_norm[i],
                  g_ple_gate_in[i], w_ple_gate[i], g_final, i == depth - 1)
    return xc.reshape(B, S, D)
```
